```python
import jax, jax.numpy as jnp
from jax import lax
import numpy as np

D_MODEL = 2048
BATCH = 2
SEQ = 4096
DEPTH = 1

GRID_W = 64
CTX_LEN = 256
EPS = 1e-6
N_MOD = 6

N_FOURIER_GROUPS = 4
FOURIER_GROUP_DIM = 256
FOURIER_DIM = N_FOURIER_GROUPS * FOURIER_GROUP_DIM

MLA_HEADS = 8
QK_NOPE_DIM = 128
QK_ROPE_DIM = 64
QK_HEAD_DIM = QK_NOPE_DIM + QK_ROPE_DIM
V_HEAD_DIM = 128
Q_LORA_RANK = 768
KV_LORA_RANK = 512
MLA_DIM = MLA_HEADS * V_HEAD_DIM
ROPE_THETA = 10000.0
Q_BLOCK = 128

MIX_DIM = FOURIER_DIM + MLA_DIM
Q_OFF = FOURIER_DIM
KV_OFF = Q_OFF + Q_LORA_RANK
ROPE_OFF = KV_OFF + KV_LORA_RANK
IN_PROJ_DIM = ROPE_OFF + QK_ROPE_DIM

N_GROUPS = 8
EXPERTS_PER_GROUP = 8
N_EXPERTS = N_GROUPS * EXPERTS_PER_GROUP
TOP_K = 2
D_EXPERT = 768
MOE_BLOCK = 128

kernel_name = 'hybrid_fourier_mla_hmoe_dit_block'


def rms_norm(x, g):
    xf = x.astype(jnp.float32)
    y = xf * lax.rsqrt(jnp.mean(xf * xf, axis=-1, keepdims=True) + EPS)
    return (y * g.astype(jnp.float32)).astype(x.dtype)


def modulate(h, shift, scale):
    return h * (1 + scale) + shift


def ada_mod(cond, w_ada, b_ada):
    m = jax.nn.silu(cond) @ w_ada + b_ada
    return jnp.split(m, N_MOD, axis=-1)


def axial_rope_tables(n_tokens):
    rows = n_tokens // GRID_W
    row = jnp.repeat(jnp.arange(rows, dtype=jnp.int32), GRID_W).astype(jnp.float32)
    col = jnp.tile(jnp.arange(GRID_W, dtype=jnp.int32), rows).astype(jnp.float32)
    n_freq = QK_ROPE_DIM // 4
    inv = ROPE_THETA ** (-jnp.arange(n_freq, dtype=jnp.float32) / n_freq)
    ang_r = row[:, None] * inv[None, :]
    ang_c = col[:, None] * inv[None, :]
    return (jnp.cos(ang_r), jnp.sin(ang_r), jnp.cos(ang_c), jnp.sin(ang_c))


def rotate_half(x, cos, sin):
    x1, x2 = jnp.split(x, 2, axis=-1)
    cos = cos[None, :, None, :].astype(x.dtype)
    sin = sin[None, :, None, :].astype(x.dtype)
    return jnp.concatenate([x1 * cos - x2 * sin, x1 * sin + x2 * cos], axis=-1)


def apply_axial_rope(x, tables):
    cr, sr, cc, sc = tables
    xr, xcol = jnp.split(x, 2, axis=-1)
    return jnp.concatenate([rotate_half(xr, cr, sr), rotate_half(xcol, cc, sc)], axis=-1)


def mla_q(p, g_q_a, w_uq, g_qk_q, tables):
    b, n, _ = p.shape
    c_q = rms_norm(p[..., Q_OFF:KV_OFF], g_q_a)
    q = rms_norm((c_q @ w_uq).reshape(b, n, MLA_HEADS, QK_HEAD_DIM), g_qk_q)
    if tables is not None:
        q = jnp.concatenate([q[..., :QK_NOPE_DIM], apply_axial_rope(q[..., QK_NOPE_DIM:], tables)], axis=-1)
    return q


def mla_kv(p, g_kv_a, w_ukv, g_qk_k, tables):
    b, n, _ = p.shape
    c_kv = rms_norm(p[..., KV_OFF:ROPE_OFF], g_kv_a)
    k_rope = p[..., ROPE_OFF:]
    kv = (c_kv @ w_ukv).reshape(b, n, MLA_HEADS, QK_NOPE_DIM + V_HEAD_DIM)
    k_nope, v = jnp.split(kv, [QK_NOPE_DIM], axis=-1)
    k = jnp.concatenate([k_nope, jnp.broadcast_to(k_rope[:, :, None, :], (b, n, MLA_HEADS, QK_ROPE_DIM))], axis=-1)
    k = rms_norm(k, g_qk_k)
    if tables is not None:
        k = jnp.concatenate([k[..., :QK_NOPE_DIM], apply_axial_rope(k[..., QK_NOPE_DIM:], tables)], axis=-1)
    return k, v


def attend(q, k, v):
    sc = jnp.einsum('bqhd,bkhd->bhqk', q, k).astype(jnp.float32) * (QK_HEAD_DIM ** -0.5)
    pr = jax.nn.softmax(sc, axis=-1).astype(v.dtype)
    return jnp.einsum('bhqk,bkhd->bqhd', pr, v)


def block_attention(q, k, v):
    b, s, h, dq = q.shape
    nblk = s // Q_BLOCK
    qb = q.reshape(b, nblk, Q_BLOCK, h, dq).transpose(1, 0, 2, 3, 4)
    out = lax.map(lambda qblk: attend(qblk, k, v), qb)
    return out.transpose(1, 0, 2, 3, 4).reshape(b, s, h * V_HEAD_DIM)


def fourier_mix(p):
    b, n, _ = p.shape
    f = p[..., :FOURIER_DIM].reshape(b, n, N_FOURIER_GROUPS, FOURIER_GROUP_DIM).astype(jnp.float32)
    y = jnp.fft.fft2(f, axes=(1, 3), norm='ortho').real
    return y.reshape(b, n, FOURIER_DIM).astype(p.dtype)


def mix_out(four, attn, g_out_four, g_out_attn, w_out):
    y = jnp.concatenate([rms_norm(four, g_out_four), rms_norm(attn, g_out_attn)], axis=-1)
    return y @ w_out


def hier_moe(h, w_rg, b_rg, w_re, b_re, w_gate, w_up, w_down):
    b, n, d = h.shape
    n_tok = b * n
    t = h.reshape(n_tok, d)
    g_prob = jax.nn.softmax((t @ w_rg + b_rg).astype(jnp.float32), axis=-1)
    g_p, g_idx = lax.top_k(g_prob, 1)
    e_logits = (t @ w_re + b_re).astype(jnp.float32).reshape(n_tok, N_GROUPS, EXPERTS_PER_GROUP)
    e_logits = jnp.take_along_axis(e_logits, g_idx[:, :, None], axis=1)[:, 0]
    e_p, e_idx = lax.top_k(jax.nn.softmax(e_logits, axis=-1), TOP_K)
    e_p = e_p / jnp.sum(e_p, axis=-1, keepdims=True)
    weights = (g_p * e_p).reshape(-1)
    expert_ids = (g_idx * EXPERTS_PER_GROUP + e_idx).reshape(-1).astype(jnp.int32)
    token_ids = jnp.repeat(jnp.arange(n_tok, dtype=jnp.int32), TOP_K)
    n_slots = n_tok * TOP_K
    order = jnp.argsort(expert_ids)
    sorted_e = expert_ids[order]
    counts = jnp.zeros((N_EXPERTS,), jnp.int32).at[expert_ids].add(1)
    starts = jnp.cumsum(counts) - counts
    padded = (counts + MOE_BLOCK - 1) // MOE_BLOCK * MOE_BLOCK
    padded_ends = jnp.cumsum(padded)
    padded_starts = padded_ends - padded
    dest = padded_starts[sorted_e] + (jnp.arange(n_slots, dtype=jnp.int32) - starts[sorted_e])
    n_blocks = -(-(n_slots + N_EXPERTS * (MOE_BLOCK - 1)) // MOE_BLOCK)
    n_rows = n_blocks * MOE_BLOCK
    row_tok = jnp.full((n_rows,), n_tok, jnp.int32).at[dest].set(token_ids[order])
    row_w = jnp.zeros((n_rows,), jnp.float32).at[dest].set(weights[order])
    block_start = jnp.arange(n_blocks, dtype=jnp.int32) * MOE_BLOCK
    block_e = jnp.minimum(jnp.sum(block_start[:, None] >= padded_ends[None, :], axis=1), N_EXPERTS - 1)
    t_pad = jnp.concatenate([t, jnp.zeros((1, d), t.dtype)], axis=0)
    xb = t_pad[row_tok].reshape(n_blocks, MOE_BLOCK, d)

    def expert_block(args):
        xblk, e = args
        return (jax.nn.silu(xblk @ w_gate[e]) * (xblk @ w_up[e])) @ w_down[e]

    yb = lax.map(expert_block, (xb, block_e)).reshape(n_rows, d)
    out = jnp.zeros((n_tok + 1, d), t.dtype).at[row_tok].add(yb * row_w[:, None].astype(t.dtype))
    return out[:n_tok].reshape(b, n, d)


def setup_inputs(seed: int = 0) -> dict:
    key = jax.random.key(seed)
    ks = jax.random.split(key, 32)
    L, D = DEPTH, D_MODEL
    nrm = jax.random.normal
    f32 = jnp.float32
    return {
        'x': nrm(ks[0], (BATCH, SEQ, D), f32),
        'c': nrm(ks[1], (BATCH, D), f32),
        'ctx': nrm(ks[2], (BATCH, CTX_LEN, D), f32),
        'c_ctx': nrm(ks[3], (D,), f32),
        'w_ada': nrm(ks[4], (L, D, N_MOD * D), f32) * (0.3 * D ** -0.5),
        'b_ada': nrm(ks[5], (L, N_MOD * D), f32) * 0.02,
        'g_norm1': 1.0 + 0.02 * nrm(ks[6], (L, D), f32),
        'g_norm2': 1.0 + 0.02 * nrm(ks[7], (L, D), f32),
        'w_in': nrm(ks[8], (L, D, IN_PROJ_DIM), f32) * D ** -0.5,
        'g_q_a': 1.0 + 0.02 * nrm(ks[9], (L, Q_LORA_RANK), f32),
        'g_kv_a': 1.0 + 0.02 * nrm(ks[10], (L, KV_LORA_RANK), f32),
        'w_uq': nrm(ks[11], (L, Q_LORA_RANK, MLA_HEADS * QK_HEAD_DIM), f32) * Q_LORA_RANK ** -0.5,
        'w_ukv': nrm(ks[12], (L, KV_LORA_RANK, MLA_HEADS * (QK_NOPE_DIM + V_HEAD_DIM)), f32) * KV_LORA_RANK ** -0.5,
        'g_qk_q': 1.0 + 0.02 * nrm(ks[13], (L, QK_HEAD_DIM), f32),
        'g_qk_k': 1.0 + 0.02 * nrm(ks[14], (L, QK_HEAD_DIM), f32),
        'g_out_four': 1.0 + 0.02 * nrm(ks[15], (L, FOURIER_DIM), f32),
        'g_out_attn': 1.0 + 0.02 * nrm(ks[16], (L, MLA_DIM), f32),
        'w_out': nrm(ks[17], (L, MIX_DIM, D), f32) * MIX_DIM ** -0.5,
        'w_router_group': nrm(ks[18], (L, D, N_GROUPS), f32) * D ** -0.5,
        'b_router_group': nrm(ks[19], (L, N_GROUPS), f32) * 0.01,
        'w_router_expert': nrm(ks[20], (L, D, N_EXPERTS), f32) * D ** -0.5,
        'b_router_expert': nrm(ks[21], (L, N_EXPERTS), f32) * 0.01,
        'w_gate': nrm(ks[22], (L, N_EXPERTS, D, D_EXPERT), f32) * D ** -0.5,
        'w_up': nrm(ks[23], (L, N_EXPERTS, D, D_EXPERT), f32) * D ** -0.5,
        'w_down': nrm(ks[24], (L, N_EXPERTS, D_EXPERT, D), f32) * D_EXPERT ** -0.5,
    }


def reference(x, c, ctx, c_ctx, w_ada, b_ada, g_norm1, g_norm2, w_in, g_q_a, g_kv_a, w_uq, w_ukv, g_qk_q, g_qk_k,
              g_out_four, g_out_attn, w_out, w_router_group, b_router_group, w_router_expert, b_router_expert,
              w_gate, w_up, w_down):
    b, s, d = x.shape
    tables = axial_rope_tables(s)
    for l in range(DEPTH):
        sh1, sc1, gt1, sh2, sc2, gt2 = [m[:, None, :] for m in ada_mod(c, w_ada[l], b_ada[l])]
        csh1, csc1, cgt1, csh2, csc2, cgt2 = ada_mod(c_ctx, w_ada[l], b_ada[l])
        hx = modulate(rms_norm(x, g_norm1[l]), sh1, sc1)
        hc = modulate(rms_norm(ctx, g_norm1[l]), csh1, csc1)
        px = hx @ w_in[l]
        pc = hc @ w_in[l]
        kc, vc = mla_kv(pc, g_kv_a[l], w_ukv[l], g_qk_k[l], None)
        qx = mla_q(px, g_q_a[l], w_uq[l], g_qk_q[l], tables)
        kx, vx = mla_kv(px, g_kv_a[l], w_ukv[l], g_qk_k[l], tables)
        attn_x = block_attention(qx, jnp.concatenate([kx, kc], axis=1), jnp.concatenate([vx, vc], axis=1))
        four_x = fourier_mix(px)
        x_new = x + gt1 * mix_out(four_x, attn_x, g_out_four[l], g_out_attn[l], w_out[l])
        if l < DEPTH - 1:
            qc = mla_q(pc, g_q_a[l], w_uq[l], g_qk_q[l], None)
            attn_c = attend(qc, kc, vc).reshape(b, ctx.shape[1], MLA_DIM)
            four_c = fourier_mix(pc)
            ctx = ctx + cgt1 * mix_out(four_c, attn_c, g_out_four[l], g_out_attn[l], w_out[l])
            ctx = ctx + cgt2 * hier_moe(modulate(rms_norm(ctx, g_norm2[l]), csh2, csc2), w_router_group[l],
                                        b_router_group[l], w_router_expert[l], b_router_expert[l],
                                        w_gate[l], w_up[l], w_down[l])
        x = x_new
        hm = modulate(rms_norm(x, g_norm2[l]), sh2, sc2)
        x = x + gt2 * hier_moe(hm, w_router_group[l], b_router_group[l], w_router_expert[l], b_router_expert[l],
                               w_gate[l], w_up[l], w_down[l])
    return x
```

```python
import functools
import math

import numpy as np
import jax
import jax.numpy as jnp
from jax import lax
from jax.experimental import pallas as pl
from jax.experimental.pallas import tpu as pltpu

F32 = jnp.float32
BF16 = jnp.bfloat16

D_MODEL = 2048
GRID_W = 64
EPS = 1e-6
N_MOD = 6
N_FOURIER_GROUPS = 4
FOURIER_GROUP_DIM = 256
FOURIER_DIM = 1024
MLA_HEADS = 8
QK_NOPE_DIM = 128
QK_ROPE_DIM = 64
QK_HEAD_DIM = 192
V_HEAD_DIM = 128
Q_LORA_RANK = 768
KV_LORA_RANK = 512
MLA_DIM = 1024
ROPE_THETA = 10000.0
Q_OFF = FOURIER_DIM
KV_OFF = Q_OFF + Q_LORA_RANK
ROPE_OFF = KV_OFF + KV_LORA_RANK
N_GROUPS = 8
EXPERTS_PER_GROUP = 8
N_EXPERTS = 64
D_EXPERT = 768

LANES = 128
HEAD_PAD = 256
MOE_BLOCK = 128
ITEM_BLOCKS = 4
VMEM_LIMIT = 56 * 1024 * 1024
NEG_BIG = -1e30


def _cparams(sem):
    return pltpu.CompilerParams(dimension_semantics=sem, vmem_limit_bytes=VMEM_LIMIT)


def _bdot(a, b):
    return jnp.dot(a, b, preferred_element_type=F32)


def _ada_kernel(c_ref, w_ref, b_ref, o_ref):
    c = c_ref[...]
    s = (c * jax.nn.sigmoid(c)).astype(BF16)
    o_ref[...] = _bdot(s, w_ref[...].astype(BF16)) + b_ref[...]


def _ada_mod(cond8, w_ada, b_ada):
    d, n = w_ada.shape
    tn = 1024
    return pl.pallas_call(
        _ada_kernel,
        grid=(n // tn,),
        in_specs=[pl.BlockSpec((8, d), lambda i: (0, 0)),
                  pl.BlockSpec((d, tn), lambda i: (0, i)),
                  pl.BlockSpec((1, tn), lambda i: (0, i))],
        out_specs=pl.BlockSpec((8, tn), lambda i: (0, i)),
        out_shape=jax.ShapeDtypeStruct((8, n), F32),
        compiler_params=_cparams(("arbitrary",)),
        name="ada_mod",
    )(cond8, w_ada, b_ada.reshape(1, n))


def _in_proj_kernel(x_ref, sh_ref, sc_ref, g_ref, gq_ref, gkv_ref, wf_ref, wq_ref, wkv_ref, wr_ref, dc_ref,
                    *out_refs, with_q):
    x = x_ref[...]
    ms = jnp.mean(x * x, axis=-1, keepdims=True)
    a = g_ref[...] * (1.0 + sc_ref[...])
    h = (x * lax.rsqrt(ms + EPS) * a + sh_ref[...]).astype(BF16)
    if with_q:
        u_ref, cq_ref, ckv_ref, kr_ref = out_refs
        f = _bdot(h, wf_ref[...]).astype(BF16)
        dc = dc_ref[...]
        for g in range(N_FOURIER_GROUPS):
            lo = g * FOURIER_GROUP_DIM
            ug = _bdot(f[:, lo:lo + FOURIER_GROUP_DIM], dc)
            u_ref[:, lo:lo + FOURIER_GROUP_DIM] = ug[:, :FOURIER_GROUP_DIM]
            u_ref[:, FOURIER_DIM + lo:FOURIER_DIM + lo + FOURIER_GROUP_DIM] = ug[:, FOURIER_GROUP_DIM:]
        pq = _bdot(h, wq_ref[...])
        msq = jnp.mean(pq * pq, axis=-1, keepdims=True)
        cq_ref[...] = (pq * lax.rsqrt(msq + EPS) * gq_ref[...]).astype(BF16)
    else:
        ckv_ref, kr_ref = out_refs
    pkv = _bdot(h, wkv_ref[...])
    mskv = jnp.mean(pkv * pkv, axis=-1, keepdims=True)
    ckv_ref[...] = (pkv * lax.rsqrt(mskv + EPS) * gkv_ref[...]).astype(BF16)
    kr_ref[...] = _bdot(h, wr_ref[...])


def _in_proj(x2, shift, scale, rows_per_mod, g1, gq, gkv, wf, wq, wkv, wr2, dc, *, with_q, tm):
    t, d = x2.shape
    nt = t // tm
    tiles_per_mod = rows_per_mod // tm

    def const(shape):
        return pl.BlockSpec(shape, lambda i: (0,) * len(shape))

    mod_spec = pl.BlockSpec((None, 1, d), lambda i: (i // tiles_per_mod, 0, 0))
    in_specs = [pl.BlockSpec((tm, d), lambda i: (i, 0)), mod_spec, mod_spec,
                const((1, d)), const((1, Q_LORA_RANK)), const((1, KV_LORA_RANK)),
                const(wf.shape), const(wq.shape), const(wkv.shape), const(wr2.shape), const(dc.shape)]

    def rows(n):
        return pl.BlockSpec((tm, n), lambda i: (i, 0))

    out_specs = [rows(KV_LORA_RANK), rows(LANES)]
    out_shape = [jax.ShapeDtypeStruct((t, KV_LORA_RANK), BF16), jax.ShapeDtypeStruct((t, LANES), F32)]
    if with_q:
        out_specs = [rows(2 * FOURIER_DIM), rows(Q_LORA_RANK)] + out_specs
        out_shape = [jax.ShapeDtypeStruct((t, 2 * FOURIER_DIM), F32),
                     jax.ShapeDtypeStruct((t, Q_LORA_RANK), BF16)] + out_shape
    return pl.pallas_call(
        functools.partial(_in_proj_kernel, with_q=with_q),
        grid=(nt,),
        in_specs=in_specs,
        out_specs=out_specs,
        out_shape=out_shape,
        compiler_params=_cparams(("arbitrary",)),
        name="in_proj_x" if with_q else "in_proj_ctx",
    )(x2, shift, scale, g1, gq, gkv, wf, wq, wkv, wr2, dc)


def _swap_halves(y, first_half):
    return jnp.where(first_half, pltpu.roll(y, LANES - 16, 1), pltpu.roll(y, 16, 1))


def _qkv_kernel(*refs, with_q, with_rope):
    it = iter(refs)
    if with_q:
        cq_ref = next(it)
    ckv_ref = next(it)
    kr_ref = next(it)
    if with_rope:
        cos_ref = next(it)
        sin_ref = next(it)
    if with_q:
        wqn_ref = next(it)
        wqr_ref = next(it)
        gqn_ref = next(it)
        gqr_ref = next(it)
    wkn_ref = next(it)
    wv_ref = next(it)
    gkn_ref = next(it)
    gkr_ref = next(it)
    if with_q:
        q_ref = next(it)
    k_ref = next(it)
    v_ref = next(it)

    tm = ckv_ref.shape[0]
    lane = lax.broadcasted_iota(jnp.int32, (tm, LANES), 1)
    low = lane < QK_ROPE_DIM
    first_half = (lane % 32) < 16
    inv_dim = 1.0 / QK_HEAD_DIM

    def rope(y):
        if not with_rope:
            return y
        return y * cos_ref[...] + _swap_halves(y, first_half) * sin_ref[...]

    if with_q:
        cq = cq_ref[...]
        qn = _bdot(cq, wqn_ref[...])
        qr = _bdot(cq, wqr_ref[...])
        qscale = QK_HEAD_DIM ** -0.5
        for p in range(MLA_HEADS // 2):
            blk = qr[:, p * LANES:(p + 1) * LANES]
            sq = blk * blk
            ss_lo = jnp.sum(jnp.where(low, sq, 0.0), axis=-1, keepdims=True)
            ss_hi = jnp.sum(jnp.where(low, 0.0, sq), axis=-1, keepdims=True)
            scales = []
            for hh, ssr in ((2 * p, ss_lo), (2 * p + 1, ss_hi)):
                nh = qn[:, hh * LANES:(hh + 1) * LANES]
                ssq = jnp.sum(nh * nh, axis=-1, keepdims=True) + ssr
                s = lax.rsqrt(ssq * inv_dim + EPS)
                scales.append(s)
                q_ref[hh, :, 0:LANES] = (nh * s * gqn_ref[...] * qscale).astype(BF16)
            s_pair = jnp.where(low, scales[0], scales[1])
            r = rope(blk * s_pair * gqr_ref[...]) * qscale
            q_ref[2 * p, :, LANES:2 * LANES] = jnp.where(low, r, 0.0).astype(BF16)
            q_ref[2 * p + 1, :, LANES:2 * LANES] = jnp.where(low, pltpu.roll(r, QK_ROPE_DIM, 1), 0.0).astype(BF16)

    ckv = ckv_ref[...]
    kn = _bdot(ckv, wkn_ref[...])
    v = _bdot(ckv, wv_ref[...])
    kr = kr_ref[...]
    ss_r = jnp.sum(jnp.where(low, kr * kr, 0.0), axis=-1, keepdims=True)
    base = rope(kr * gkr_ref[...])
    for hh in range(MLA_HEADS):
        nh = kn[:, hh * LANES:(hh + 1) * LANES]
        ssq = jnp.sum(nh * nh, axis=-1, keepdims=True) + ss_r
        s = lax.rsqrt(ssq * inv_dim + EPS)
        k_ref[hh, :, 0:LANES] = (nh * s * gkn_ref[...]).astype(BF16)
        k_ref[hh, :, LANES:2 * LANES] = jnp.where(low, base * s, 0.0).astype(BF16)
        v_ref[hh] = v[:, hh * LANES:(hh + 1) * LANES].astype(BF16)


def _qkv(cq, ckv, kr2, cos_t, sin_t, wqn, wqr, gqn, gqr2, wkn, wv, gkn, gkr2, *, batch, seq, tm, with_q,
         with_rope):
    t = ckv.shape[0]
    nt = t // tm
    tiles_per_b = seq // tm

    def rows(n):
        return pl.BlockSpec((tm, n), lambda i: (i, 0))

    def const(arr):
        return pl.BlockSpec(arr.shape, lambda i: (0,) * arr.ndim)

    tab_spec = pl.BlockSpec((tm, LANES), lambda i: (i % tiles_per_b, 0))

    def head_out(width):
        return pl.BlockSpec((None, MLA_HEADS, tm, width), lambda i: (i // tiles_per_b, 0, i % tiles_per_b, 0))

    args, in_specs = [], []
    if with_q:
        args.append(cq)
        in_specs.append(rows(Q_LORA_RANK))
    args += [ckv, kr2]
    in_specs += [rows(KV_LORA_RANK), rows(LANES)]
    if with_rope:
        args += [cos_t, sin_t]
        in_specs += [tab_spec, tab_spec]
    if with_q:
        args += [wqn, wqr, gqn, gqr2]
        in_specs += [const(wqn), const(wqr), const(gqn), const(gqr2)]
    args += [wkn, wv, gkn, gkr2]
    in_specs += [const(wkn), const(wv), const(gkn), const(gkr2)]

    out_specs = [head_out(HEAD_PAD), head_out(V_HEAD_DIM)]
    out_shape = [jax.ShapeDtypeStruct((batch, MLA_HEADS, seq, HEAD_PAD), BF16),
                 jax.ShapeDtypeStruct((batch, MLA_HEADS, seq, V_HEAD_DIM), BF16)]
    if with_q:
        out_specs = [head_out(HEAD_PAD)] + out_specs
        out_shape = [jax.ShapeDtypeStruct((batch, MLA_HEADS, seq, HEAD_PAD), BF16)] + out_shape
    return pl.pallas_call(
        functools.partial(_qkv_kernel, with_q=with_q, with_rope=with_rope),
        grid=(nt,),
        in_specs=in_specs,
        out_specs=out_specs,
        out_shape=out_shape,
        compiler_params=_cparams(("arbitrary",)),
        name="qkv_x" if with_q else "kv_ctx",
    )(*args)


def _attn_kernel(q_ref, kx_ref, kc_ref, vx_ref, vc_ref, o_ref):
    q = q_ref[...]
    dn = (((1,), (1,)), ((), ()))
    s1 = lax.dot_general(q, kx_ref[...], dn, preferred_element_type=F32)
    s2 = lax.dot_general(q, kc_ref[...], dn, preferred_element_type=F32)
    m = jnp.maximum(jnp.max(s1, axis=-1, keepdims=True), jnp.max(s2, axis=-1, keepdims=True))
    p1 = jnp.exp(s1 - m)
    p2 = jnp.exp(s2 - m)
    l = jnp.sum(p1, axis=-1, keepdims=True) + jnp.sum(p2, axis=-1, keepdims=True)
    o = _bdot(p1.astype(BF16), vx_ref[...]) + _bdot(p2.astype(BF16), vc_ref[...])
    o_ref[...] = o / l


def _attention(q, kx, kc, vx, vc, *, tq):
    b, h, s, _ = q.shape
    lc = kc.shape[2]
    return pl.pallas_call(
        _attn_kernel,
        grid=(b, h, s // tq),
        in_specs=[pl.BlockSpec((None, None, tq, HEAD_PAD), lambda bi, hi, qi: (bi, hi, qi, 0)),
                  pl.BlockSpec((None, None, s, HEAD_PAD), lambda bi, hi, qi: (bi, hi, 0, 0)),
                  pl.BlockSpec((None, None, lc, HEAD_PAD), lambda bi, hi, qi: (bi, hi, 0, 0)),
                  pl.BlockSpec((None, None, s, V_HEAD_DIM), lambda bi, hi, qi: (bi, hi, 0, 0)),
                  pl.BlockSpec((None, None, lc, V_HEAD_DIM), lambda bi, hi, qi: (bi, hi, 0, 0))],
        out_specs=pl.BlockSpec((None, tq, V_HEAD_DIM), lambda bi, hi, qi: (bi, qi, hi)),
        out_shape=jax.ShapeDtypeStruct((b, s, h * V_HEAD_DIM), F32),
        compiler_params=_cparams(("arbitrary", "arbitrary", "arbitrary")),
        name="attention",
    )(q, kx, kc, vx, vc)


def _dft_stage1_kernel(u_ref, t_ref, o_ref):
    u = u_ref[...].reshape(GRID_W * 8, 2 * FOURIER_DIM)
    rhs = jnp.concatenate([u[:, :FOURIER_DIM], u[:, FOURIER_DIM:]], axis=0).astype(BF16)
    a = _bdot(t_ref[...], rhs)
    o_ref[...] = a.reshape(2, GRID_W, 8, FOURIER_DIM)


def _dft_stage2_kernel(a_ref, t_ref, o_ref):
    rhs = a_ref[...].reshape(2 * 8 * GRID_W, FOURIER_DIM).astype(BF16)
    y = _bdot(t_ref[...], rhs)
    o_ref[...] = y.reshape(GRID_W, 8, FOURIER_DIM)


def _seq_dft_tables(n_seq):
    w = GRID_W
    ch = np.arange(8).reshape(8, 1, 1, 1)
    kb = np.arange(w).reshape(1, w, 1, 1)
    j = np.arange(8).reshape(1, 1, 8, 1)
    r = np.arange(w).reshape(1, 1, 1, w)
    ang = (2.0 * np.pi / n_seq) * ((kb * (w * r + 8 * ch + j)) % n_seq)
    c, s = np.cos(ang), np.sin(ang)
    rot = np.stack([np.stack([c, s], axis=3), np.stack([-s, c], axis=3)], axis=1)
    rot = jnp.asarray(rot.astype(np.float32))
    eye8 = jnp.eye(8, dtype=F32)
    t1 = (rot[..., None] * eye8[None, None, None, :, None, None, :]).reshape(8, 2 * w * 8, 2 * w * 8)
    ka = np.arange(w).reshape(w, 1)
    cp = np.arange(w).reshape(1, w)
    ang2 = (2.0 * np.pi / w) * ((ka * cp) % w)
    norm = 1.0 / math.sqrt(n_seq * FOURIER_GROUP_DIM)
    cs = jnp.asarray((np.stack([np.cos(ang2), np.sin(ang2)], axis=1) * norm).astype(np.float32))
    t2 = (cs[:, None, :, None, :] * eye8[None, :, None, :, None]).reshape(w * 8, 2 * 8 * w)
    return t1.astype(BF16), t2.astype(BF16)


def _seq_dft(u, batch, n_seq):
    w = GRID_W
    t1, t2 = _seq_dft_tables(n_seq)
    u5 = u.reshape(batch, w, 8, 8, 2 * FOURIER_DIM)
    a = pl.pallas_call(
        _dft_stage1_kernel,
        grid=(batch, 8),
        in_specs=[pl.BlockSpec((None, w, None, 8, 2 * FOURIER_DIM), lambda b, c: (b, 0, c, 0, 0)),
                  pl.BlockSpec((None, 2 * w * 8, 2 * w * 8), lambda b, c: (c, 0, 0))],
        out_specs=pl.BlockSpec((None, 2, w, None, 8, FOURIER_DIM), lambda b, c: (b, 0, 0, c, 0, 0)),
        out_shape=jax.ShapeDtypeStruct((batch, 2, w, 8, 8, FOURIER_DIM), F32),
        compiler_params=_cparams(("arbitrary", "arbitrary")),
        name="seq_dft_stage1",
    )(u5, t1)
    y = pl.pallas_call(
        _dft_stage2_kernel,
        grid=(batch, 8),
        in_specs=[pl.BlockSpec((None, 2, 8, 8, 8, FOURIER_DIM), lambda b, k: (b, 0, k, 0, 0, 0)),
                  pl.BlockSpec((w * 8, 2 * 8 * w), lambda b, k: (0, 0))],
        out_specs=pl.BlockSpec((None, w, None, 8, FOURIER_DIM), lambda b, k: (b, 0, k, 0, 0)),
        out_shape=jax.ShapeDtypeStruct((batch, w, 8, 8, FOURIER_DIM), F32),
        compiler_params=_cparams(("arbitrary", "arbitrary")),
        name="seq_dft_stage2",
    )(a, t2)
    return y.reshape(batch * n_seq, FOURIER_DIM)


def _out_router_kernel(x_ref, four_ref, attn_ref, gt1_ref, sh2_ref, sc2_ref, gf_ref, ga_ref, g2_ref,
                       wof_ref, woa_ref, wr_ref, br_ref,
                       xnew_ref, hm_ref, ri_ref, rw_ref, cnt_ref, carry_ref):
    i = pl.program_id(0)
    tm = x_ref.shape[0]

    @pl.when(i == 0)
    def _():
        carry_ref[...] = jnp.zeros_like(carry_ref)

    def norm(v, g):
        return (v * lax.rsqrt(jnp.mean(v * v, axis=-1, keepdims=True) + EPS) * g).astype(BF16)

    mix = _bdot(norm(four_ref[...], gf_ref[...]), wof_ref[...]) + _bdot(norm(attn_ref[...], ga_ref[...]), woa_ref[...])
    xn = x_ref[...] + gt1_ref[...] * mix
    xnew_ref[...] = xn
    ms = jnp.mean(xn * xn, axis=-1, keepdims=True)
    hm = xn * lax.rsqrt(ms + EPS) * (g2_ref[...] * (1.0 + sc2_ref[...])) + sh2_ref[...]
    hm_ref[...] = hm

    logits = jnp.dot(hm, wr_ref[...], preferred_element_type=F32, precision=lax.Precision.HIGHEST) + br_ref[...]
    lane = lax.broadcasted_iota(jnp.int32, (tm, LANES), 1)
    lanef = lane.astype(F32)
    far = 1e9

    lg = jnp.where(lane < N_GROUPS, logits, NEG_BIG)
    m1 = jnp.max(lg, axis=-1, keepdims=True)
    g_p = 1.0 / jnp.sum(jnp.exp(lg - m1), axis=-1, keepdims=True)
    gidx = jnp.min(jnp.where(lg >= m1, lanef, far), axis=-1, keepdims=True)
    lo = N_GROUPS + EXPERTS_PER_GROUP * gidx
    in_group = jnp.where(lanef >= lo, jnp.where(lanef < lo + EXPERTS_PER_GROUP, 1.0, 0.0), 0.0) > 0.5
    le = jnp.where(in_group, logits, NEG_BIG)
    m2 = jnp.max(le, axis=-1, keepdims=True)
    idx1 = jnp.min(jnp.where(le >= m2, lanef, far), axis=-1, keepdims=True)
    le2 = jnp.where(lanef == idx1, NEG_BIG, le)
    m3 = jnp.max(le2, axis=-1, keepdims=True)
    idx2 = jnp.min(jnp.where(le2 >= m3, lanef, far), axis=-1, keepdims=True)
    t = jnp.exp(m3 - m2)
    p1 = 1.0 / (1.0 + t)
    p2 = t / (1.0 + t)
    e1 = idx1 - N_GROUPS
    e2 = idx2 - N_GROUPS

    oh1 = jnp.where(lanef == e1, 1.0, 0.0)
    oh2 = jnp.where(lanef == e2, 1.0, 0.0)
    ohs = oh1 + oh2
    row = lax.broadcasted_iota(jnp.int32, (tm, tm), 0)
    col = lax.broadcasted_iota(jnp.int32, (tm, tm), 1)
    tri = jnp.where(row > col, 1.0, 0.0).astype(BF16)
    before = _bdot(tri, ohs.astype(BF16)) + carry_ref[...]
    rank1 = jnp.sum(oh1 * before, axis=-1, keepdims=True)
    rank2 = jnp.sum(oh2 * before, axis=-1, keepdims=True)
    carry = carry_ref[...] + jnp.sum(ohs, axis=0, keepdims=True)
    carry_ref[...] = carry
    cnt_ref[...] = jnp.broadcast_to(carry, cnt_ref.shape)

    ri = jnp.where(lane == 0, e1, jnp.where(lane == 1, e2, jnp.where(lane == 2, rank1, jnp.where(lane == 3, rank2, 0.0))))
    ri_ref[...] = ri.astype(jnp.int32)
    rw_ref[...] = jnp.where(lane == 0, g_p * p1, jnp.where(lane == 1, g_p * p2, 0.0))


def _out_router(x2, four, attn, gt1, sh2, sc2, gf, ga, g2, wof, woa, wr, br, *, seq, tm):
    t, d = x2.shape
    nt = t // tm
    tiles_per_b = seq // tm

    def rows(n):
        return pl.BlockSpec((tm, n), lambda i: (i, 0))

    def const(arr):
        return pl.BlockSpec(arr.shape, lambda i: (0,) * arr.ndim)

    mod_spec = pl.BlockSpec((None, 1, d), lambda i: (i // tiles_per_b, 0, 0))
    return pl.pallas_call(
        _out_router_kernel,
        grid=(nt,),
        in_specs=[rows(d), rows(FOURIER_DIM), rows(MLA_DIM), mod_spec, mod_spec, mod_spec,
                  const(gf), const(ga), const(g2), const(wof), const(woa), const(wr), const(br)],
        out_specs=[rows(d), rows(d), rows(LANES), rows(LANES), pl.BlockSpec((8, LANES), lambda i: (0, 0))],
        out_shape=[jax.ShapeDtypeStruct((t, d), F32), jax.ShapeDtypeStruct((t, d), F32),
                   jax.ShapeDtypeStruct((t, LANES), jnp.int32), jax.ShapeDtypeStruct((t, LANES), F32),
                   jax.ShapeDtypeStruct((8, LANES), F32)],
        scratch_shapes=[pltpu.VMEM((1, LANES), F32)],
        compiler_params=_cparams(("arbitrary",)),
        name="out_proj_router",
    )(x2, four, attn, gt1, sh2, sc2, gf, ga, g2, wof, woa, wr, br)


def _moe_kernel(item_e, item_blk0, item_nblk, row_tok, used_blocks,
                hm_hbm, wg_ref, wu_ref, wd_ref, y_hbm,
                xg, xb, acc, wgb, wub, wdb, gsem, osem):
    i = pl.program_id(0)
    j = pl.program_id(1)
    n_items = pl.num_programs(0)
    nj = pl.num_programs(1)
    slot = i % 2
    nblk = item_nblk[i]

    def gather_copy(tok, sl, r):
        return pltpu.make_async_copy(hm_hbm.at[pl.ds(tok, 1)], xg.at[sl, pl.ds(r, 1)], gsem.at[sl])

    def issue_gather(it, sl):
        r0 = item_blk0[it] * MOE_BLOCK

        def body(r, carry):
            gather_copy(row_tok[r0 + r], sl, r).start()
            return carry

        lax.fori_loop(0, item_nblk[it] * MOE_BLOCK, body, 0)

    def wait_gather(it, sl):
        for b in range(ITEM_BLOCKS):
            @pl.when(b < item_nblk[it])
            def _():
                pltpu.make_async_copy(hm_hbm.at[pl.ds(0, MOE_BLOCK)], xg.at[sl, pl.ds(b * MOE_BLOCK, MOE_BLOCK)],
                                      gsem.at[sl]).wait()

    def out_copy(it, m):
        r0 = pl.multiple_of(item_blk0[it] * MOE_BLOCK, MOE_BLOCK)
        return pltpu.make_async_copy(acc.at[pl.ds(0, m)], y_hbm.at[pl.ds(r0, m)], osem.at[0])

    def wait_out(it):
        for nb in range(1, ITEM_BLOCKS + 1):
            @pl.when(item_nblk[it] == nb)
            def _():
                out_copy(it, nb * MOE_BLOCK).wait()

    @pl.when(j == 0)
    def _():
        @pl.when(i == 0)
        def _():
            issue_gather(0, 0)

        @pl.when(i > 0)
        def _():
            wait_out(i - 1)

        wait_gather(i, slot)

        @pl.when(i + 1 < n_items)
        def _():
            issue_gather(i + 1, 1 - slot)

    wgb[...] = wg_ref[...].astype(BF16)
    wub[...] = wu_ref[...].astype(BF16)
    wdb[...] = wd_ref[...].astype(BF16)

    for nb in range(1, ITEM_BLOCKS + 1):
        m = nb * MOE_BLOCK

        @pl.when(nblk == nb)
        def _():
            @pl.when(j == 0)
            def _():
                xb[0:m, :] = xg[slot, 0:m, :].astype(BF16)
                acc[0:m, :] = jnp.zeros((m, acc.shape[1]), F32)

            x = xb[0:m, :]
            g = _bdot(x, wgb[...])
            u = _bdot(x, wub[...])
            a = (g * jax.nn.sigmoid(g) * u).astype(BF16)
            acc[0:m, :] += _bdot(a, wdb[...])

            @pl.when(j == nj - 1)
            def _():
                out_copy(i, m).start()

    @pl.when(jnp.logical_and(i == n_items - 1, j == nj - 1))
    def _():
        wait_out(i)
        n_blocks = y_hbm.shape[0] // MOE_BLOCK
        acc[0:MOE_BLOCK, :] = jnp.zeros((MOE_BLOCK, acc.shape[1]), F32)

        def tail_copy(blk):
            r0 = pl.multiple_of(blk * MOE_BLOCK, MOE_BLOCK)
            return pltpu.make_async_copy(acc.at[pl.ds(0, MOE_BLOCK)], y_hbm.at[pl.ds(r0, MOE_BLOCK)], osem.at[0])

        def start_body(blk, carry):
            tail_copy(blk).start()
            return carry

        def wait_body(blk, carry):
            tail_copy(blk).wait()
            return carry

        lax.fori_loop(used_blocks[0], n_blocks, start_body, 0)
        lax.fori_loop(used_blocks[0], n_blocks, wait_body, 0)


def _moe(hm, w_gate, w_up, w_down, item_e, item_blk0, item_nblk, row_tok, used_blocks, *, n_rows, tj):
    t, d = hm.shape
    n_items = item_e.shape[0]
    nj = D_EXPERT // tj
    rows = ITEM_BLOCKS * MOE_BLOCK
    grid_spec = pltpu.PrefetchScalarGridSpec(
        num_scalar_prefetch=5,
        grid=(n_items, nj),
        in_specs=[pl.BlockSpec(memory_space=pl.ANY),
                  pl.BlockSpec((None, d, tj), lambda i, j, ie, *_: (ie[i], 0, j)),
                  pl.BlockSpec((None, d, tj), lambda i, j, ie, *_: (ie[i], 0, j)),
                  pl.BlockSpec((None, tj, d), lambda i, j, ie, *_: (ie[i], j, 0))],
        out_specs=pl.BlockSpec(memory_space=pl.ANY),
        scratch_shapes=[pltpu.VMEM((2, rows, d), F32),
                        pltpu.VMEM((rows, d), BF16),
                        pltpu.VMEM((rows, d), F32),
                        pltpu.VMEM((d, tj), BF16),
                        pltpu.VMEM((d, tj), BF16),
                        pltpu.VMEM((tj, d), BF16),
                        pltpu.SemaphoreType.DMA((2,)),
                        pltpu.SemaphoreType.DMA((1,))],
    )
    return pl.pallas_call(
        _moe_kernel,
        grid_spec=grid_spec,
        out_shape=jax.ShapeDtypeStruct((n_rows, d), F32),
        compiler_params=_cparams(("arbitrary", "arbitrary")),
        name="moe_experts",
    )(item_e, item_blk0, item_nblk, row_tok, used_blocks, hm, w_gate, w_up, w_down)


def _combine_kernel(dest, x_ref, gt2_ref, rw_ref, y_hbm, o_ref, ybuf, sem):
    i = pl.program_id(0)
    n = pl.num_programs(0)
    tm = x_ref.shape[0]
    slot = i % 2

    def issue(it, sl):
        base = it * tm

        def body(r, carry):
            for k in range(2):
                pltpu.make_async_copy(y_hbm.at[pl.ds(dest[2 * (base + r) + k], 1)], ybuf.at[sl, k, pl.ds(r, 1)],
                                      sem.at[sl]).start()
            return carry

        lax.fori_loop(0, tm, body, 0)

    @pl.when(i == 0)
    def _():
        issue(0, 0)

    for k in range(2):
        pltpu.make_async_copy(y_hbm.at[pl.ds(0, tm)], ybuf.at[slot, k], sem.at[slot]).wait()

    @pl.when(i + 1 < n)
    def _():
        issue(i + 1, 1 - slot)

    w = rw_ref[...]
    moe = w[:, 0:1] * ybuf[slot, 0] + w[:, 1:2] * ybuf[slot, 1]
    o_ref[...] = x_ref[...] + gt2_ref[...] * moe


def _combine(dest, xnew, gt2, rw, y, *, seq, tm):
    t, d = xnew.shape
    tiles_per_b = seq // tm
    grid_spec = pltpu.PrefetchScalarGridSpec(
        num_scalar_prefetch=1,
        grid=(t // tm,),
        in_specs=[pl.BlockSpec((tm, d), lambda i, ds: (i, 0)),
                  pl.BlockSpec((None, 1, d), lambda i, ds: (i // tiles_per_b, 0, 0)),
                  pl.BlockSpec((tm, LANES), lambda i, ds: (i, 0)),
                  pl.BlockSpec(memory_space=pl.ANY)],
        out_specs=pl.BlockSpec((tm, d), lambda i, ds: (i, 0)),
        scratch_shapes=[pltpu.VMEM((2, 2, tm, d), F32), pltpu.SemaphoreType.DMA((2,))],
    )
    return pl.pallas_call(
        _combine_kernel,
        grid_spec=grid_spec,
        out_shape=jax.ShapeDtypeStruct((t, d), F32),
        compiler_params=_cparams(("arbitrary",)),
        name="moe_combine",
    )(dest, xnew, gt2, rw, y)


def _rope_tables(n_tokens):
    rows = n_tokens // GRID_W
    row = jnp.repeat(jnp.arange(rows, dtype=jnp.int32), GRID_W).astype(F32)
    col = jnp.tile(jnp.arange(GRID_W, dtype=jnp.int32), rows).astype(F32)
    n_freq = QK_ROPE_DIM // 4
    inv = ROPE_THETA ** (-jnp.arange(n_freq, dtype=F32) / n_freq)
    ar = row[:, None] * inv[None, :]
    ac = col[:, None] * inv[None, :]
    cr, sr, cc, sc = jnp.cos(ar), jnp.sin(ar), jnp.cos(ac), jnp.sin(ac)
    cos64 = jnp.concatenate([cr, cr, cc, cc], axis=-1)
    sin64 = jnp.concatenate([-sr, sr, -sc, sc], axis=-1)
    return jnp.tile(cos64, (1, 2)), jnp.tile(sin64, (1, 2))


def _channel_dft_table():
    c = np.arange(FOURIER_GROUP_DIM).reshape(-1, 1)
    k = np.arange(FOURIER_GROUP_DIM).reshape(1, -1)
    ang = (2.0 * np.pi / FOURIER_GROUP_DIM) * ((c * k) % FOURIER_GROUP_DIM)
    return jnp.asarray(np.concatenate([np.cos(ang), -np.sin(ang)], axis=1).astype(np.float32)).astype(BF16)


def _split_heads(w, widths):
    k = w.shape[0]
    wh = w.reshape(k, MLA_HEADS, sum(widths))
    outs, off = [], 0
    for wd in widths:
        outs.append(wh[:, :, off:off + wd].reshape(k, MLA_HEADS * wd))
        off += wd
    return outs


def kernel(x, c, ctx, c_ctx, w_ada, b_ada, g_norm1, g_norm2, w_in, g_q_a, g_kv_a, w_uq, w_ukv, g_qk_q, g_qk_k,
           g_out_four, g_out_attn, w_out, w_router_group, b_router_group, w_router_expert, b_router_expert,
           w_gate, w_up, w_down):
    b, s, d = x.shape
    lc = ctx.shape[1]
    t = b * s
    layer_params = (w_ada, b_ada, g_norm1, g_norm2, w_in, g_q_a, g_kv_a, w_uq, w_ukv, g_qk_q, g_qk_k, g_out_four,
                    g_out_attn, w_out, w_router_group, b_router_group, w_router_expert, b_router_expert,
                    w_gate, w_up, w_down)
    assert all(p.shape[0] == 1 for p in layer_params), "single-layer block"
    (w_ada, b_ada, g_norm1, g_norm2, w_in, g_q_a, g_kv_a, w_uq, w_ukv, g_qk_q, g_qk_k, g_out_four,
     g_out_attn, w_out, w_router_group, b_router_group, w_router_expert, b_router_expert,
     w_gate, w_up, w_down) = [p.reshape(p.shape[1:]) for p in layer_params]

    cond8 = jnp.concatenate([c, c_ctx[None, :], jnp.zeros((8 - b - 1, d), F32)], axis=0)
    mods = _ada_mod(cond8, w_ada, b_ada)
    sh1, sc1, gt1, sh2, sc2, gt2 = [m[:b].reshape(b, 1, d) for m in jnp.split(mods, N_MOD, axis=-1)]
    csh1, csc1 = [m[b:b + 1].reshape(1, 1, d) for m in jnp.split(mods, N_MOD, axis=-1)[:2]]

    wf = w_in[:, :Q_OFF].astype(BF16)
    wq = w_in[:, Q_OFF:KV_OFF].astype(BF16)
    wkv = w_in[:, KV_OFF:ROPE_OFF].astype(BF16)
    wr = w_in[:, ROPE_OFF:].astype(BF16)
    wr2 = jnp.concatenate([wr, wr], axis=1)
    dc = _channel_dft_table()
    wqn, wqr = [w.astype(BF16) for w in _split_heads(w_uq, (QK_NOPE_DIM, QK_ROPE_DIM))]
    wkn, wv = [w.astype(BF16) for w in _split_heads(w_ukv, (QK_NOPE_DIM, V_HEAD_DIM))]
    gqn = g_qk_q[:QK_NOPE_DIM].reshape(1, -1)
    gqr2 = jnp.tile(g_qk_q[QK_NOPE_DIM:], 2).reshape(1, -1)
    gkn = g_qk_k[:QK_NOPE_DIM].reshape(1, -1)
    gkr2 = jnp.tile(g_qk_k[QK_NOPE_DIM:], 2).reshape(1, -1)
    g1 = g_norm1.reshape(1, d)
    gq = g_q_a.reshape(1, -1)
    gkv = g_kv_a.reshape(1, -1)

    x2 = x.reshape(t, d)
    u, cq, ckv, kr2 = _in_proj(x2, sh1, sc1, s, g1, gq, gkv, wf, wq, wkv, wr2, dc, with_q=True, tm=512)
    ckv_c, kr2_c = _in_proj(ctx.reshape(b * lc, d), csh1, csc1, b * lc, g1, gq, gkv, wf, wq, wkv, wr2, dc,
                            with_q=False, tm=lc)

    cos_t, sin_t = _rope_tables(s)
    q, kx, vx = _qkv(cq, ckv, kr2, cos_t, sin_t, wqn, wqr, gqn, gqr2, wkn, wv, gkn, gkr2,
                     batch=b, seq=s, tm=512, with_q=True, with_rope=True)
    kc, vc = _qkv(None, ckv_c, kr2_c, None, None, None, None, None, None, wkn, wv, gkn, gkr2,
                  batch=b, seq=lc, tm=lc, with_q=False, with_rope=False)

    attn = _attention(q, kx, kc, vx, vc, tq=256).reshape(t, MLA_DIM)
    four = _seq_dft(u, b, s)

    wof = w_out[:FOURIER_DIM].astype(BF16)
    woa = w_out[FOURIER_DIM:].astype(BF16)
    n_route = N_GROUPS + N_EXPERTS
    wrt = jnp.concatenate([w_router_group, w_router_expert, jnp.zeros((d, LANES - n_route), F32)], axis=1)
    brt = jnp.concatenate([b_router_group, b_router_expert, jnp.zeros((LANES - n_route,), F32)]).reshape(1, -1)
    xnew, hm, ri, rw, cnt = _out_router(x2, four, attn, gt1, sh2, sc2, g_out_four.reshape(1, -1),
                                        g_out_attn.reshape(1, -1), g_norm2.reshape(1, d), wof, woa, wrt, brt,
                                        seq=s, tm=256)

    counts = cnt[0, :N_EXPERTS].astype(jnp.int32)
    nblk_e = (counts + MOE_BLOCK - 1) // MOE_BLOCK
    blk_end = jnp.cumsum(nblk_e)
    blk_start = blk_end - nblk_e
    n_slots = t * 2
    n_blocks = -(-(n_slots + N_EXPERTS * (MOE_BLOCK - 1)) // MOE_BLOCK)
    n_rows = n_blocks * MOE_BLOCK
    e12 = ri[:, 0:2]
    dest = (blk_start[e12] * MOE_BLOCK + ri[:, 2:4]).reshape(-1)
    tok_ids = jnp.repeat(jnp.arange(t, dtype=jnp.int32), 2)
    row_tok = jnp.zeros((n_rows,), jnp.int32).at[dest].set(tok_ids)
    items_e = (nblk_e + ITEM_BLOCKS - 1) // ITEM_BLOCKS
    item_end = jnp.cumsum(items_e)
    n_items = N_EXPERTS + -(-n_blocks // ITEM_BLOCKS)
    idx = jnp.arange(n_items, dtype=jnp.int32)
    total = item_end[-1]
    idx_c = jnp.minimum(idx, total - 1)
    ie = jnp.minimum(jnp.sum(idx_c[:, None] >= item_end[None, :], axis=1), N_EXPERTS - 1).astype(jnp.int32)
    local = idx_c - (item_end - items_e)[ie]
    item_blk0 = (blk_start[ie] + ITEM_BLOCKS * local).astype(jnp.int32)
    item_nblk = jnp.where(idx < total, jnp.clip(nblk_e[ie] - ITEM_BLOCKS * local, 0, ITEM_BLOCKS), 0).astype(jnp.int32)

    used_blocks = blk_end[-1:].astype(jnp.int32)
    y = _moe(hm, w_gate, w_up, w_down, ie, item_blk0, item_nblk, row_tok, used_blocks, n_rows=n_rows, tj=384)
    out = _combine(dest.astype(jnp.int32), xnew, gt2, rw, y, seq=s, tm=128)
    return out.reshape(b, s, d)
```

```python
import functools
import math

import numpy as np
import jax
import jax.numpy as jnp
from jax import lax
from jax.experimental import pallas as pl
from jax.experimental.pallas import tpu as pltpu

F32 = jnp.float32
BF16 = jnp.bfloat16

D_MODEL = 2048
GRID_W = 64
EPS = 1e-6
N_MOD = 6
N_FOURIER_GROUPS = 4
FOURIER_GROUP_DIM = 256
FOURIER_DIM = 1024
MLA_HEADS = 8
QK_NOPE_DIM = 128
QK_ROPE_DIM = 64
QK_HEAD_DIM = 192
V_HEAD_DIM = 128
Q_LORA_RANK = 768
KV_LORA_RANK = 512
MLA_DIM = 1024
ROPE_THETA = 10000.0
Q_OFF = FOURIER_DIM
KV_OFF = Q_OFF + Q_LORA_RANK
ROPE_OFF = KV_OFF + KV_LORA_RANK
N_GROUPS = 8
EXPERTS_PER_GROUP = 8
N_EXPERTS = 64
D_EXPERT = 768

LANES = 128
HEAD_PAD = 256
MOE_BLOCK = 128
ITEM_BLOCKS = 4
WEIGHT_SLOTS = 3
GATHER_UNROLL = 8
VMEM_LIMIT = 56 * 1024 * 1024
NEG_BIG = -1e30


def _cparams(sem):
    return pltpu.CompilerParams(dimension_semantics=sem, vmem_limit_bytes=VMEM_LIMIT)


def _bdot(a, b):
    return jnp.dot(a, b, preferred_element_type=F32)


def _ada_kernel(c_ref, w_ref, b_ref, o_ref):
    c = c_ref[...]
    s = (c * jax.nn.sigmoid(c)).astype(BF16)
    o_ref[...] = _bdot(s, w_ref[...].astype(BF16)) + b_ref[...]


def _ada_mod(cond8, w_ada, b_ada):
    d, n = w_ada.shape
    tn = 1024
    return pl.pallas_call(
        _ada_kernel,
        grid=(n // tn,),
        in_specs=[pl.BlockSpec((8, d), lambda i: (0, 0)),
                  pl.BlockSpec((d, tn), lambda i: (0, i)),
                  pl.BlockSpec((1, tn), lambda i: (0, i))],
        out_specs=pl.BlockSpec((8, tn), lambda i: (0, i)),
        out_shape=jax.ShapeDtypeStruct((8, n), F32),
        compiler_params=_cparams(("arbitrary",)),
        name="ada_mod",
    )(cond8, w_ada, b_ada.reshape(1, n))


def _in_proj_kernel(x_ref, sh_ref, sc_ref, g_ref, gq_ref, gkv_ref, wf_ref, wq_ref, wkv_ref, wr_ref, dc_ref,
                    *out_refs, with_q):
    x = x_ref[...]
    ms = jnp.mean(x * x, axis=-1, keepdims=True)
    a = g_ref[...] * (1.0 + sc_ref[...])
    h = (x * lax.rsqrt(ms + EPS) * a + sh_ref[...]).astype(BF16)
    if with_q:
        u_ref, cq_ref, ckv_ref, kr_ref = out_refs
        f = _bdot(h, wf_ref[...]).astype(BF16)
        dc = dc_ref[...]
        for g in range(N_FOURIER_GROUPS):
            lo = g * FOURIER_GROUP_DIM
            ug = _bdot(f[:, lo:lo + FOURIER_GROUP_DIM], dc)
            u_ref[:, lo:lo + FOURIER_GROUP_DIM] = ug[:, :FOURIER_GROUP_DIM]
            u_ref[:, FOURIER_DIM + lo:FOURIER_DIM + lo + FOURIER_GROUP_DIM] = ug[:, FOURIER_GROUP_DIM:]
        pq = _bdot(h, wq_ref[...])
        msq = jnp.mean(pq * pq, axis=-1, keepdims=True)
        cq_ref[...] = (pq * lax.rsqrt(msq + EPS) * gq_ref[...]).astype(BF16)
    else:
        ckv_ref, kr_ref = out_refs
    pkv = _bdot(h, wkv_ref[...])
    mskv = jnp.mean(pkv * pkv, axis=-1, keepdims=True)
    ckv_ref[...] = (pkv * lax.rsqrt(mskv + EPS) * gkv_ref[...]).astype(BF16)
    kr_ref[...] = _bdot(h, wr_ref[...])


def _in_proj(x2, shift, scale, rows_per_mod, g1, gq, gkv, wf, wq, wkv, wr2, dc, *, with_q, tm):
    t, d = x2.shape
    nt = t // tm
    tiles_per_mod = rows_per_mod // tm

    def const(shape):
        return pl.BlockSpec(shape, lambda i: (0,) * len(shape))

    mod_spec = pl.BlockSpec((None, 1, d), lambda i: (i // tiles_per_mod, 0, 0))
    in_specs = [pl.BlockSpec((tm, d), lambda i: (i, 0)), mod_spec, mod_spec,
                const((1, d)), const((1, Q_LORA_RANK)), const((1, KV_LORA_RANK)),
                const(wf.shape), const(wq.shape), const(wkv.shape), const(wr2.shape), const(dc.shape)]

    def rows(n):
        return pl.BlockSpec((tm, n), lambda i: (i, 0))

    out_specs = [rows(KV_LORA_RANK), rows(LANES)]
    out_shape = [jax.ShapeDtypeStruct((t, KV_LORA_RANK), BF16), jax.ShapeDtypeStruct((t, LANES), F32)]
    if with_q:
        out_specs = [rows(2 * FOURIER_DIM), rows(Q_LORA_RANK)] + out_specs
        out_shape = [jax.ShapeDtypeStruct((t, 2 * FOURIER_DIM), F32),
                     jax.ShapeDtypeStruct((t, Q_LORA_RANK), BF16)] + out_shape
    return pl.pallas_call(
        functools.partial(_in_proj_kernel, with_q=with_q),
        grid=(nt,),
        in_specs=in_specs,
        out_specs=out_specs,
        out_shape=out_shape,
        compiler_params=_cparams(("arbitrary",)),
        name="in_proj_x" if with_q else "in_proj_ctx",
    )(x2, shift, scale, g1, gq, gkv, wf, wq, wkv, wr2, dc)


def _swap_halves(y, first_half):
    return jnp.where(first_half, pltpu.roll(y, LANES - 16, 1), pltpu.roll(y, 16, 1))


def _qkv_kernel(*refs, with_q, with_rope):
    it = iter(refs)
    if with_q:
        cq_ref = next(it)
    ckv_ref = next(it)
    kr_ref = next(it)
    if with_rope:
        cos_ref = next(it)
        sin_ref = next(it)
    if with_q:
        wqn_ref = next(it)
        wqr_ref = next(it)
        gqn_ref = next(it)
        gqr_ref = next(it)
    wkn_ref = next(it)
    wv_ref = next(it)
    gkn_ref = next(it)
    gkr_ref = next(it)
    if with_q:
        q_ref = next(it)
    k_ref = next(it)
    v_ref = next(it)

    tm = ckv_ref.shape[0]
    lane = lax.broadcasted_iota(jnp.int32, (tm, LANES), 1)
    low = lane < QK_ROPE_DIM
    first_half = (lane % 32) < 16
    inv_dim = 1.0 / QK_HEAD_DIM

    def rope(y):
        if not with_rope:
            return y
        return y * cos_ref[...] + _swap_halves(y, first_half) * sin_ref[...]

    if with_q:
        cq = cq_ref[...]
        qn = _bdot(cq, wqn_ref[...])
        qr = _bdot(cq, wqr_ref[...])
        qscale = QK_HEAD_DIM ** -0.5
        for p in range(MLA_HEADS // 2):
            blk = qr[:, p * LANES:(p + 1) * LANES]
            sq = blk * blk
            ss_lo = jnp.sum(jnp.where(low, sq, 0.0), axis=-1, keepdims=True)
            ss_hi = jnp.sum(jnp.where(low, 0.0, sq), axis=-1, keepdims=True)
            scales = []
            for hh, ssr in ((2 * p, ss_lo), (2 * p + 1, ss_hi)):
                nh = qn[:, hh * LANES:(hh + 1) * LANES]
                ssq = jnp.sum(nh * nh, axis=-1, keepdims=True) + ssr
                s = lax.rsqrt(ssq * inv_dim + EPS)
                scales.append(s)
                q_ref[hh, :, 0:LANES] = (nh * s * gqn_ref[...] * qscale).astype(BF16)
            s_pair = jnp.where(low, scales[0], scales[1])
            r = rope(blk * s_pair * gqr_ref[...]) * qscale
            q_ref[2 * p, :, LANES:2 * LANES] = jnp.where(low, r, 0.0).astype(BF16)
            q_ref[2 * p + 1, :, LANES:2 * LANES] = jnp.where(low, pltpu.roll(r, QK_ROPE_DIM, 1), 0.0).astype(BF16)

    ckv = ckv_ref[...]
    kn = _bdot(ckv, wkn_ref[...])
    v = _bdot(ckv, wv_ref[...])
    kr = kr_ref[...]
    ss_r = jnp.sum(jnp.where(low, kr * kr, 0.0), axis=-1, keepdims=True)
    base = rope(kr * gkr_ref[...])
    for hh in range(MLA_HEADS):
        nh = kn[:, hh * LANES:(hh + 1) * LANES]
        ssq = jnp.sum(nh * nh, axis=-1, keepdims=True) + ss_r
        s = lax.rsqrt(ssq * inv_dim + EPS)
        k_ref[hh, :, 0:LANES] = (nh * s * gkn_ref[...]).astype(BF16)
        k_ref[hh, :, LANES:2 * LANES] = jnp.where(low, base * s, 0.0).astype(BF16)
        v_ref[hh] = v[:, hh * LANES:(hh + 1) * LANES].astype(BF16)


def _qkv(cq, ckv, kr2, cos_t, sin_t, wqn, wqr, gqn, gqr2, wkn, wv, gkn, gkr2, *, batch, seq, tm, with_q,
         with_rope):
    t = ckv.shape[0]
    nt = t // tm
    tiles_per_b = seq // tm

    def rows(n):
        return pl.BlockSpec((tm, n), lambda i: (i, 0))

    def const(arr):
        return pl.BlockSpec(arr.shape, lambda i: (0,) * arr.ndim)

    tab_spec = pl.BlockSpec((tm, LANES), lambda i: (i % tiles_per_b, 0))

    def head_out(width):
        return pl.BlockSpec((None, MLA_HEADS, tm, width), lambda i: (i // tiles_per_b, 0, i % tiles_per_b, 0))

    args, in_specs = [], []
    if with_q:
        args.append(cq)
        in_specs.append(rows(Q_LORA_RANK))
    args += [ckv, kr2]
    in_specs += [rows(KV_LORA_RANK), rows(LANES)]
    if with_rope:
        args += [cos_t, sin_t]
        in_specs += [tab_spec, tab_spec]
    if with_q:
        args += [wqn, wqr, gqn, gqr2]
        in_specs += [const(wqn), const(wqr), const(gqn), const(gqr2)]
    args += [wkn, wv, gkn, gkr2]
    in_specs += [const(wkn), const(wv), const(gkn), const(gkr2)]

    out_specs = [head_out(HEAD_PAD), head_out(V_HEAD_DIM)]
    out_shape = [jax.ShapeDtypeStruct((batch, MLA_HEADS, seq, HEAD_PAD), BF16),
                 jax.ShapeDtypeStruct((batch, MLA_HEADS, seq, V_HEAD_DIM), BF16)]
    if with_q:
        out_specs = [head_out(HEAD_PAD)] + out_specs
        out_shape = [jax.ShapeDtypeStruct((batch, MLA_HEADS, seq, HEAD_PAD), BF16)] + out_shape
    return pl.pallas_call(
        functools.partial(_qkv_kernel, with_q=with_q, with_rope=with_rope),
        grid=(nt,),
        in_specs=in_specs,
        out_specs=out_specs,
        out_shape=out_shape,
        compiler_params=_cparams(("arbitrary",)),
        name="qkv_x" if with_q else "kv_ctx",
    )(*args)


def _attn_kernel(q_ref, kx_ref, kc_ref, vx_ref, vc_ref, o_ref):
    q = q_ref[...]
    dn = (((1,), (1,)), ((), ()))
    s1 = lax.dot_general(q, kx_ref[...], dn, preferred_element_type=F32)
    s2 = lax.dot_general(q, kc_ref[...], dn, preferred_element_type=F32)
    m = jnp.maximum(jnp.max(s1, axis=-1, keepdims=True), jnp.max(s2, axis=-1, keepdims=True))
    p1 = jnp.exp(s1 - m)
    p2 = jnp.exp(s2 - m)
    l = jnp.sum(p1, axis=-1, keepdims=True) + jnp.sum(p2, axis=-1, keepdims=True)
    o = _bdot(p1.astype(BF16), vx_ref[...]) + _bdot(p2.astype(BF16), vc_ref[...])
    o_ref[...] = o / l


def _attention(q, kx, kc, vx, vc, *, tq):
    b, h, s, _ = q.shape
    lc = kc.shape[2]
    return pl.pallas_call(
        _attn_kernel,
        grid=(b, h, s // tq),
        in_specs=[pl.BlockSpec((None, None, tq, HEAD_PAD), lambda bi, hi, qi: (bi, hi, qi, 0)),
                  pl.BlockSpec((None, None, s, HEAD_PAD), lambda bi, hi, qi: (bi, hi, 0, 0)),
                  pl.BlockSpec((None, None, lc, HEAD_PAD), lambda bi, hi, qi: (bi, hi, 0, 0)),
                  pl.BlockSpec((None, None, s, V_HEAD_DIM), lambda bi, hi, qi: (bi, hi, 0, 0)),
                  pl.BlockSpec((None, None, lc, V_HEAD_DIM), lambda bi, hi, qi: (bi, hi, 0, 0))],
        out_specs=pl.BlockSpec((None, tq, V_HEAD_DIM), lambda bi, hi, qi: (bi, qi, hi)),
        out_shape=jax.ShapeDtypeStruct((b, s, h * V_HEAD_DIM), F32),
        compiler_params=_cparams(("arbitrary", "arbitrary", "arbitrary")),
        name="attention",
    )(q, kx, kc, vx, vc)


def _dft_stage1_kernel(u_ref, r_ref, e_ref, o_ref):
    n = 2 * GRID_W * 8
    t = _bdot(r_ref[...].astype(BF16), e_ref[...])
    row = lax.broadcasted_iota(jnp.int32, (n, n), 0)
    col = lax.broadcasted_iota(jnp.int32, (n, n), 1)
    t = jnp.where((row % 8) == (col % 8), t, 0.0).astype(BF16)
    u = u_ref[...].reshape(GRID_W * 8, 2 * FOURIER_DIM)
    rhs = jnp.concatenate([u[:, :FOURIER_DIM], u[:, FOURIER_DIM:]], axis=0).astype(BF16)
    a = _bdot(t, rhs)
    o_ref[...] = a.reshape(2, GRID_W, 8, FOURIER_DIM)


def _dft_stage2_kernel(a_ref, t_ref, o_ref):
    rhs = a_ref[...].reshape(2 * 8 * GRID_W, FOURIER_DIM).astype(BF16)
    y = _bdot(t_ref[...].astype(BF16), rhs)
    o_ref[...] = y.reshape(GRID_W, 8, FOURIER_DIM)


def _seq_dft_tables(n_seq):
    w = GRID_W
    ch = np.arange(8).reshape(8, 1, 1, 1)
    kb = np.arange(w).reshape(1, w, 1, 1)
    j = np.arange(8).reshape(1, 1, 8, 1)
    r = np.arange(w).reshape(1, 1, 1, w)
    ang = (2.0 * np.pi / n_seq) * ((kb * (w * r + 8 * ch + j)) % n_seq)
    c, s = np.cos(ang), np.sin(ang)
    rot = np.stack([np.stack([c, s], axis=3), np.stack([-s, c], axis=3)], axis=1)
    r1 = rot.reshape(8, 2 * w * 8, 2 * w).astype(np.float32)
    expand = (np.arange(2 * w * 8)[None, :] // 8 == np.arange(2 * w)[:, None]).astype(np.float32)
    ka = np.arange(w).reshape(w, 1)
    cp = np.arange(w).reshape(1, w)
    ang2 = (2.0 * np.pi / w) * ((ka * cp) % w)
    norm = 1.0 / math.sqrt(n_seq * FOURIER_GROUP_DIM)
    cs = np.stack([np.cos(ang2), np.sin(ang2)], axis=1) * norm
    eye8 = np.eye(8)
    t2 = (cs[:, None, :, None, :] * eye8[None, :, None, :, None]).reshape(w * 8, 2 * 8 * w).astype(np.float32)
    return jnp.asarray(r1), jnp.asarray(expand).astype(BF16), jnp.asarray(t2)


def _seq_dft(u, batch, n_seq):
    w = GRID_W
    r1, expand, t2 = _seq_dft_tables(n_seq)
    u5 = u.reshape(batch, w, 8, 8, 2 * FOURIER_DIM)
    a = pl.pallas_call(
        _dft_stage1_kernel,
        grid=(batch, 8),
        in_specs=[pl.BlockSpec((None, w, None, 8, 2 * FOURIER_DIM), lambda b, c: (b, 0, c, 0, 0)),
                  pl.BlockSpec((None, 2 * w * 8, 2 * w), lambda b, c: (c, 0, 0)),
                  pl.BlockSpec((2 * w, 2 * w * 8), lambda b, c: (0, 0))],
        out_specs=pl.BlockSpec((None, 2, w, None, 8, FOURIER_DIM), lambda b, c: (b, 0, 0, c, 0, 0)),
        out_shape=jax.ShapeDtypeStruct((batch, 2, w, 8, 8, FOURIER_DIM), F32),
        compiler_params=_cparams(("arbitrary", "arbitrary")),
        name="seq_dft_stage1",
    )(u5, r1, expand)
    y = pl.pallas_call(
        _dft_stage2_kernel,
        grid=(batch, 8),
        in_specs=[pl.BlockSpec((None, 2, 8, 8, 8, FOURIER_DIM), lambda b, k: (b, 0, k, 0, 0, 0)),
                  pl.BlockSpec((w * 8, 2 * 8 * w), lambda b, k: (0, 0))],
        out_specs=pl.BlockSpec((None, w, None, 8, FOURIER_DIM), lambda b, k: (b, 0, k, 0, 0)),
        out_shape=jax.ShapeDtypeStruct((batch, w, 8, 8, FOURIER_DIM), F32),
        compiler_params=_cparams(("arbitrary", "arbitrary")),
        name="seq_dft_stage2",
    )(a, t2)
    return y.reshape(batch * n_seq, FOURIER_DIM)


def _out_router_kernel(x_ref, four_ref, attn_ref, gt1_ref, sh2_ref, sc2_ref, gf_ref, ga_ref, g2_ref,
                       wof_ref, woa_ref, wrh_ref, wrl_ref, br_ref,
                       xnew_ref, hm_ref, ri_ref, rw_ref, cnt_ref, carry_ref):
    i = pl.program_id(0)
    tm = x_ref.shape[0]

    @pl.when(i == 0)
    def _():
        carry_ref[...] = jnp.zeros_like(carry_ref)

    def norm(v, g):
        return (v * lax.rsqrt(jnp.mean(v * v, axis=-1, keepdims=True) + EPS) * g).astype(BF16)

    mix = _bdot(norm(four_ref[...], gf_ref[...]), wof_ref[...]) + _bdot(norm(attn_ref[...], ga_ref[...]), woa_ref[...])
    xn = x_ref[...] + gt1_ref[...] * mix
    xnew_ref[...] = xn
    ms = jnp.mean(xn * xn, axis=-1, keepdims=True)
    hm = xn * lax.rsqrt(ms + EPS) * (g2_ref[...] * (1.0 + sc2_ref[...])) + sh2_ref[...]
    hm_ref[...] = hm

    hm_hi = hm.astype(BF16)
    hm_lo = (hm - hm_hi.astype(F32)).astype(BF16)
    logits = _bdot(hm_hi, wrh_ref[...]) + _bdot(hm_lo, wrh_ref[...]) + _bdot(hm_hi, wrl_ref[...]) + br_ref[...]
    lane = lax.broadcasted_iota(jnp.int32, (tm, LANES), 1)
    lanef = lane.astype(F32)
    far = 1e9

    lg = jnp.where(lane < N_GROUPS, logits, NEG_BIG)
    m1 = jnp.max(lg, axis=-1, keepdims=True)
    g_p = 1.0 / jnp.sum(jnp.exp(lg - m1), axis=-1, keepdims=True)
    gidx = jnp.min(jnp.where(lg >= m1, lanef, far), axis=-1, keepdims=True)
    lo = N_GROUPS + EXPERTS_PER_GROUP * gidx
    in_group = jnp.where(lanef >= lo, jnp.where(lanef < lo + EXPERTS_PER_GROUP, 1.0, 0.0), 0.0) > 0.5
    le = jnp.where(in_group, logits, NEG_BIG)
    m2 = jnp.max(le, axis=-1, keepdims=True)
    idx1 = jnp.min(jnp.where(le >= m2, lanef, far), axis=-1, keepdims=True)
    le2 = jnp.where(lanef == idx1, NEG_BIG, le)
    m3 = jnp.max(le2, axis=-1, keepdims=True)
    idx2 = jnp.min(jnp.where(le2 >= m3, lanef, far), axis=-1, keepdims=True)
    t = jnp.exp(m3 - m2)
    p1 = 1.0 / (1.0 + t)
    p2 = t / (1.0 + t)
    e1 = idx1 - N_GROUPS
    e2 = idx2 - N_GROUPS

    oh1 = jnp.where(lanef == e1, 1.0, 0.0)
    oh2 = jnp.where(lanef == e2, 1.0, 0.0)
    ohs = oh1 + oh2
    row = lax.broadcasted_iota(jnp.int32, (tm, tm), 0)
    col = lax.broadcasted_iota(jnp.int32, (tm, tm), 1)
    tri = jnp.where(row > col, 1.0, 0.0).astype(BF16)
    before = _bdot(tri, ohs.astype(BF16)) + carry_ref[...]
    rank1 = jnp.sum(oh1 * before, axis=-1, keepdims=True)
    rank2 = jnp.sum(oh2 * before, axis=-1, keepdims=True)
    carry = carry_ref[...] + jnp.sum(ohs, axis=0, keepdims=True)
    carry_ref[...] = carry
    cnt_ref[...] = jnp.broadcast_to(carry, cnt_ref.shape)

    ri = jnp.where(lane == 0, e1, jnp.where(lane == 1, e2, jnp.where(lane == 2, rank1, jnp.where(lane == 3, rank2, 0.0))))
    ri_ref[...] = ri.astype(jnp.int32)
    rw_ref[...] = jnp.where(lane == 0, g_p * p1, jnp.where(lane == 1, g_p * p2, 0.0))


def _out_router(x2, four, attn, gt1, sh2, sc2, gf, ga, g2, wof, woa, wrh, wrl, br, *, seq, tm):
    t, d = x2.shape
    nt = t // tm
    tiles_per_b = seq // tm

    def rows(n):
        return pl.BlockSpec((tm, n), lambda i: (i, 0))

    def const(arr):
        return pl.BlockSpec(arr.shape, lambda i: (0,) * arr.ndim)

    mod_spec = pl.BlockSpec((None, 1, d), lambda i: (i // tiles_per_b, 0, 0))
    return pl.pallas_call(
        _out_router_kernel,
        grid=(nt,),
        in_specs=[rows(d), rows(FOURIER_DIM), rows(MLA_DIM), mod_spec, mod_spec, mod_spec,
                  const(gf), const(ga), const(g2), const(wof), const(woa), const(wrh), const(wrl), const(br)],
        out_specs=[rows(d), rows(d), rows(LANES), rows(LANES), pl.BlockSpec((8, LANES), lambda i: (0, 0))],
        out_shape=[jax.ShapeDtypeStruct((t, d), F32), jax.ShapeDtypeStruct((t, d), F32),
                   jax.ShapeDtypeStruct((t, LANES), jnp.int32), jax.ShapeDtypeStruct((t, LANES), F32),
                   jax.ShapeDtypeStruct((8, LANES), F32)],
        scratch_shapes=[pltpu.VMEM((1, LANES), F32)],
        compiler_params=_cparams(("arbitrary",)),
        name="out_proj_router",
    )(x2, four, attn, gt1, sh2, sc2, gf, ga, g2, wof, woa, wrh, wrl, br)


def _moe_kernel(item_e, item_blk0, item_nblk, row_tok, used_blocks,
                hm_hbm, wg_hbm, wu_hbm, wd_hbm, y_hbm,
                xg, xb, acc, wg_buf, wu_buf, wd_buf, gsem, osem, wsem):
    i = pl.program_id(0)
    j = pl.program_id(1)
    n_items = pl.num_programs(0)
    nj = pl.num_programs(1)
    tj = wg_buf.shape[2]
    step = i * nj + j
    n_steps = n_items * nj
    slot = i % 2
    wslot = step % WEIGHT_SLOTS
    nblk = item_nblk[i]

    def weight_copies(st, ws):
        e = item_e[st // nj]
        c0 = pl.multiple_of((st % nj) * tj, tj)
        return (pltpu.make_async_copy(wg_hbm.at[e, :, pl.ds(c0, tj)], wg_buf.at[ws], wsem.at[ws]),
                pltpu.make_async_copy(wu_hbm.at[e, :, pl.ds(c0, tj)], wu_buf.at[ws], wsem.at[ws]),
                pltpu.make_async_copy(wd_hbm.at[e, pl.ds(c0, tj), :], wd_buf.at[ws], wsem.at[ws]))

    def start_weights(st):
        st_c = jnp.minimum(st, n_steps - 1)

        @pl.when(jnp.logical_and(st < n_steps, item_nblk[st_c // nj] > 0))
        def _():
            for cp in weight_copies(st_c, st_c % WEIGHT_SLOTS):
                cp.start()

    def gather_copy(tok, sl, r):
        return pltpu.make_async_copy(hm_hbm.at[pl.ds(tok, 1)], xg.at[sl, pl.ds(r, 1)], gsem.at[sl])

    def issue_gather(it, sl):
        r0 = item_blk0[it] * MOE_BLOCK

        def body(r8, carry):
            for k in range(GATHER_UNROLL):
                r = r8 * GATHER_UNROLL + k
                gather_copy(row_tok[r0 + r], sl, r).start()
            return carry

        lax.fori_loop(0, item_nblk[it] * (MOE_BLOCK // GATHER_UNROLL), body, 0)

    def wait_gather(it, sl):
        for b in range(ITEM_BLOCKS):
            @pl.when(b < item_nblk[it])
            def _():
                pltpu.make_async_copy(hm_hbm.at[pl.ds(0, MOE_BLOCK)], xg.at[sl, pl.ds(b * MOE_BLOCK, MOE_BLOCK)],
                                      gsem.at[sl]).wait()

    def out_copy(it, m):
        r0 = pl.multiple_of(item_blk0[it] * MOE_BLOCK, MOE_BLOCK)
        return pltpu.make_async_copy(acc.at[pl.ds(0, m)], y_hbm.at[pl.ds(r0, m)], osem.at[0])

    def wait_out(it):
        for nb in range(1, ITEM_BLOCKS + 1):
            @pl.when(item_nblk[it] == nb)
            def _():
                out_copy(it, nb * MOE_BLOCK).wait()

    @pl.when(step == 0)
    def _():
        for st in range(WEIGHT_SLOTS - 1):
            start_weights(st)
        issue_gather(0, 0)

    start_weights(step + WEIGHT_SLOTS - 1)

    @pl.when(j == 0)
    def _():
        @pl.when(i > 0)
        def _():
            wait_out(i - 1)

        wait_gather(i, slot)

        @pl.when(i + 1 < n_items)
        def _():
            issue_gather(i + 1, 1 - slot)

    @pl.when(nblk > 0)
    def _():
        for cp in weight_copies(step, wslot):
            cp.wait()

    for nb in range(1, ITEM_BLOCKS + 1):
        m = nb * MOE_BLOCK

        @pl.when(nblk == nb)
        def _():
            @pl.when(j == 0)
            def _():
                xb[0:m, :] = xg[slot, 0:m, :].astype(BF16)

            x = xb[0:m, :]
            g = _bdot(x, wg_buf[wslot].astype(BF16))
            u = _bdot(x, wu_buf[wslot].astype(BF16))
            a = (g * jax.nn.sigmoid(g) * u).astype(BF16)
            y = _bdot(a, wd_buf[wslot].astype(BF16))

            @pl.when(j == 0)
            def _():
                acc[0:m, :] = y

            @pl.when(j > 0)
            def _():
                acc[0:m, :] += y

            @pl.when(j == nj - 1)
            def _():
                out_copy(i, m).start()

    @pl.when(step == n_steps - 1)
    def _():
        wait_out(i)
        n_blocks = y_hbm.shape[0] // MOE_BLOCK
        acc[0:MOE_BLOCK, :] = jnp.zeros((MOE_BLOCK, acc.shape[1]), F32)

        def tail_copy(blk):
            r0 = pl.multiple_of(blk * MOE_BLOCK, MOE_BLOCK)
            return pltpu.make_async_copy(acc.at[pl.ds(0, MOE_BLOCK)], y_hbm.at[pl.ds(r0, MOE_BLOCK)], osem.at[0])

        def start_body(blk, carry):
            tail_copy(blk).start()
            return carry

        def wait_body(blk, carry):
            tail_copy(blk).wait()
            return carry

        lax.fori_loop(used_blocks[0], n_blocks, start_body, 0)
        lax.fori_loop(used_blocks[0], n_blocks, wait_body, 0)


def _moe(hm, w_gate, w_up, w_down, item_e, item_blk0, item_nblk, row_tok, used_blocks, *, n_rows, tj):
    t, d = hm.shape
    n_items = item_e.shape[0]
    nj = D_EXPERT // tj
    rows = ITEM_BLOCKS * MOE_BLOCK
    any_spec = pl.BlockSpec(memory_space=pl.ANY)
    grid_spec = pltpu.PrefetchScalarGridSpec(
        num_scalar_prefetch=5,
        grid=(n_items, nj),
        in_specs=[any_spec, any_spec, any_spec, any_spec],
        out_specs=any_spec,
        scratch_shapes=[pltpu.VMEM((2, rows, d), F32),
                        pltpu.VMEM((rows, d), BF16),
                        pltpu.VMEM((rows, d), F32),
                        pltpu.VMEM((WEIGHT_SLOTS, d, tj), F32),
                        pltpu.VMEM((WEIGHT_SLOTS, d, tj), F32),
                        pltpu.VMEM((WEIGHT_SLOTS, tj, d), F32),
                        pltpu.SemaphoreType.DMA((2,)),
                        pltpu.SemaphoreType.DMA((1,)),
                        pltpu.SemaphoreType.DMA((WEIGHT_SLOTS,))],
    )
    return pl.pallas_call(
        _moe_kernel,
        grid_spec=grid_spec,
        out_shape=jax.ShapeDtypeStruct((n_rows, d), F32),
        compiler_params=_cparams(("arbitrary", "arbitrary")),
        name="moe_experts",
    )(item_e, item_blk0, item_nblk, row_tok, used_blocks, hm, w_gate, w_up, w_down)


def _combine_kernel(dest, x_ref, gt2_ref, rw_ref, y_hbm, o_ref, ybuf, sem):
    i = pl.program_id(0)
    n = pl.num_programs(0)
    tm = x_ref.shape[0]
    slot = i % 2

    def issue(it, sl):
        base = it * tm

        def body(r, carry):
            for k in range(2):
                pltpu.make_async_copy(y_hbm.at[pl.ds(dest[2 * (base + r) + k], 1)], ybuf.at[sl, k, pl.ds(r, 1)],
                                      sem.at[sl]).start()
            return carry

        lax.fori_loop(0, tm, body, 0)

    @pl.when(i == 0)
    def _():
        issue(0, 0)

    for k in range(2):
        pltpu.make_async_copy(y_hbm.at[pl.ds(0, tm)], ybuf.at[slot, k], sem.at[slot]).wait()

    @pl.when(i + 1 < n)
    def _():
        issue(i + 1, 1 - slot)

    w = rw_ref[...]
    moe = w[:, 0:1] * ybuf[slot, 0] + w[:, 1:2] * ybuf[slot, 1]
    o_ref[...] = x_ref[...] + gt2_ref[...] * moe


def _combine(dest, xnew, gt2, rw, y, *, seq, tm):
    t, d = xnew.shape
    tiles_per_b = seq // tm
    grid_spec = pltpu.PrefetchScalarGridSpec(
        num_scalar_prefetch=1,
        grid=(t // tm,),
        in_specs=[pl.BlockSpec((tm, d), lambda i, ds: (i, 0)),
                  pl.BlockSpec((None, 1, d), lambda i, ds: (i // tiles_per_b, 0, 0)),
                  pl.BlockSpec((tm, LANES), lambda i, ds: (i, 0)),
                  pl.BlockSpec(memory_space=pl.ANY)],
        out_specs=pl.BlockSpec((tm, d), lambda i, ds: (i, 0)),
        scratch_shapes=[pltpu.VMEM((2, 2, tm, d), F32), pltpu.SemaphoreType.DMA((2,))],
    )
    return pl.pallas_call(
        _combine_kernel,
        grid_spec=grid_spec,
        out_shape=jax.ShapeDtypeStruct((t, d), F32),
        compiler_params=_cparams(("arbitrary",)),
        name="moe_combine",
    )(dest, xnew, gt2, rw, y)


def _rope_tables(n_tokens):
    rows = n_tokens // GRID_W
    row = jnp.repeat(jnp.arange(rows, dtype=jnp.int32), GRID_W).astype(F32)
    col = jnp.tile(jnp.arange(GRID_W, dtype=jnp.int32), rows).astype(F32)
    n_freq = QK_ROPE_DIM // 4
    inv = ROPE_THETA ** (-jnp.arange(n_freq, dtype=F32) / n_freq)
    ar = row[:, None] * inv[None, :]
    ac = col[:, None] * inv[None, :]
    cr, sr, cc, sc = jnp.cos(ar), jnp.sin(ar), jnp.cos(ac), jnp.sin(ac)
    cos64 = jnp.concatenate([cr, cr, cc, cc], axis=-1)
    sin64 = jnp.concatenate([-sr, sr, -sc, sc], axis=-1)
    return jnp.tile(cos64, (1, 2)), jnp.tile(sin64, (1, 2))


def _channel_dft_table():
    c = np.arange(FOURIER_GROUP_DIM).reshape(-1, 1)
    k = np.arange(FOURIER_GROUP_DIM).reshape(1, -1)
    ang = (2.0 * np.pi / FOURIER_GROUP_DIM) * ((c * k) % FOURIER_GROUP_DIM)
    return jnp.asarray(np.concatenate([np.cos(ang), -np.sin(ang)], axis=1).astype(np.float32)).astype(BF16)


def _split_heads(w, widths):
    k = w.shape[0]
    wh = w.reshape(k, MLA_HEADS, sum(widths))
    outs, off = [], 0
    for wd in widths:
        outs.append(wh[:, :, off:off + wd].reshape(k, MLA_HEADS * wd))
        off += wd
    return outs


def kernel(x, c, ctx, c_ctx, w_ada, b_ada, g_norm1, g_norm2, w_in, g_q_a, g_kv_a, w_uq, w_ukv, g_qk_q, g_qk_k,
           g_out_four, g_out_attn, w_out, w_router_group, b_router_group, w_router_expert, b_router_expert,
           w_gate, w_up, w_down):
    b, s, d = x.shape
    lc = ctx.shape[1]
    t = b * s
    layer_params = (w_ada, b_ada, g_norm1, g_norm2, w_in, g_q_a, g_kv_a, w_uq, w_ukv, g_qk_q, g_qk_k, g_out_four,
                    g_out_attn, w_out, w_router_group, b_router_group, w_router_expert, b_router_expert,
                    w_gate, w_up, w_down)
    assert all(p.shape[0] == 1 for p in layer_params), "single-layer block"
    (w_ada, b_ada, g_norm1, g_norm2, w_in, g_q_a, g_kv_a, w_uq, w_ukv, g_qk_q, g_qk_k, g_out_four,
     g_out_attn, w_out, w_router_group, b_router_group, w_router_expert, b_router_expert,
     w_gate, w_up, w_down) = [p.reshape(p.shape[1:]) for p in layer_params]

    cond8 = jnp.concatenate([c, c_ctx[None, :], jnp.zeros((8 - b - 1, d), F32)], axis=0)
    mods = _ada_mod(cond8, w_ada, b_ada)
    sh1, sc1, gt1, sh2, sc2, gt2 = [m[:b].reshape(b, 1, d) for m in jnp.split(mods, N_MOD, axis=-1)]
    csh1, csc1 = [m[b:b + 1].reshape(1, 1, d) for m in jnp.split(mods, N_MOD, axis=-1)[:2]]

    wf = w_in[:, :Q_OFF].astype(BF16)
    wq = w_in[:, Q_OFF:KV_OFF].astype(BF16)
    wkv = w_in[:, KV_OFF:ROPE_OFF].astype(BF16)
    wr = w_in[:, ROPE_OFF:].astype(BF16)
    wr2 = jnp.concatenate([wr, wr], axis=1)
    dc = _channel_dft_table()
    wqn, wqr = [w.astype(BF16) for w in _split_heads(w_uq, (QK_NOPE_DIM, QK_ROPE_DIM))]
    wkn, wv = [w.astype(BF16) for w in _split_heads(w_ukv, (QK_NOPE_DIM, V_HEAD_DIM))]
    gqn = g_qk_q[:QK_NOPE_DIM].reshape(1, -1)
    gqr2 = jnp.tile(g_qk_q[QK_NOPE_DIM:], 2).reshape(1, -1)
    gkn = g_qk_k[:QK_NOPE_DIM].reshape(1, -1)
    gkr2 = jnp.tile(g_qk_k[QK_NOPE_DIM:], 2).reshape(1, -1)
    g1 = g_norm1.reshape(1, d)
    gq = g_q_a.reshape(1, -1)
    gkv = g_kv_a.reshape(1, -1)

    x2 = x.reshape(t, d)
    u, cq, ckv, kr2 = _in_proj(x2, sh1, sc1, s, g1, gq, gkv, wf, wq, wkv, wr2, dc, with_q=True, tm=512)
    ckv_c, kr2_c = _in_proj(ctx.reshape(b * lc, d), csh1, csc1, b * lc, g1, gq, gkv, wf, wq, wkv, wr2, dc,
                            with_q=False, tm=lc)

    cos_t, sin_t = _rope_tables(s)
    q, kx, vx = _qkv(cq, ckv, kr2, cos_t, sin_t, wqn, wqr, gqn, gqr2, wkn, wv, gkn, gkr2,
                     batch=b, seq=s, tm=512, with_q=True, with_rope=True)
    kc, vc = _qkv(None, ckv_c, kr2_c, None, None, None, None, None, None, wkn, wv, gkn, gkr2,
                  batch=b, seq=lc, tm=lc, with_q=False, with_rope=False)

    attn = _attention(q, kx, kc, vx, vc, tq=512).reshape(t, MLA_DIM)
    four = _seq_dft(u, b, s)

    wof = w_out[:FOURIER_DIM].astype(BF16)
    woa = w_out[FOURIER_DIM:].astype(BF16)
    n_route = N_GROUPS + N_EXPERTS
    wrt = jnp.concatenate([w_router_group, w_router_expert, jnp.zeros((d, LANES - n_route), F32)], axis=1)
    wrt_hi = wrt.astype(BF16)
    wrt_lo = (wrt - wrt_hi.astype(F32)).astype(BF16)
    brt = jnp.concatenate([b_router_group, b_router_expert, jnp.zeros((LANES - n_route,), F32)]).reshape(1, -1)
    xnew, hm, ri, rw, cnt = _out_router(x2, four, attn, gt1, sh2, sc2, g_out_four.reshape(1, -1),
                                        g_out_attn.reshape(1, -1), g_norm2.reshape(1, d), wof, woa, wrt_hi, wrt_lo, brt,
                                        seq=s, tm=256)

    counts = cnt[0, :N_EXPERTS].astype(jnp.int32)
    nblk_e = (counts + MOE_BLOCK - 1) // MOE_BLOCK
    blk_end = jnp.cumsum(nblk_e)
    blk_start = blk_end - nblk_e
    n_slots = t * 2
    n_blocks = -(-(n_slots + N_EXPERTS * (MOE_BLOCK - 1)) // MOE_BLOCK)
    n_rows = n_blocks * MOE_BLOCK
    e12 = ri[:, 0:2]
    dest = (blk_start[e12] * MOE_BLOCK + ri[:, 2:4]).reshape(-1)
    tok_ids = jnp.repeat(jnp.arange(t, dtype=jnp.int32), 2)
    row_tok = jnp.zeros((n_rows,), jnp.int32).at[dest].set(tok_ids)
    items_e = (nblk_e + ITEM_BLOCKS - 1) // ITEM_BLOCKS
    item_end = jnp.cumsum(items_e)
    n_items = (n_blocks + (ITEM_BLOCKS - 1) * N_EXPERTS) // ITEM_BLOCKS
    idx = jnp.arange(n_items, dtype=jnp.int32)
    total = item_end[-1]
    idx_c = jnp.minimum(idx, total - 1)
    ie = jnp.minimum(jnp.sum(idx_c[:, None] >= item_end[None, :], axis=1), N_EXPERTS - 1).astype(jnp.int32)
    local = idx_c - (item_end - items_e)[ie]
    item_blk0 = (blk_start[ie] + ITEM_BLOCKS * local).astype(jnp.int32)
    item_nblk = jnp.where(idx < total, jnp.clip(nblk_e[ie] - ITEM_BLOCKS * local, 0, ITEM_BLOCKS), 0).astype(jnp.int32)

    used_blocks = blk_end[-1:].astype(jnp.int32)
    y = _moe(hm, w_gate, w_up, w_down, ie, item_blk0, item_nblk, row_tok, used_blocks, n_rows=n_rows, tj=256)
    out = _combine(dest.astype(jnp.int32), xnew, gt2, rw, y, seq=s, tm=128)
    return out.reshape(b, s, d)
```

```python
import functools
import math

import numpy as np
import jax
import jax.numpy as jnp
from jax import lax
from jax.experimental import pallas as pl
from jax.experimental.pallas import tpu as pltpu

F32 = jnp.float32
BF16 = jnp.bfloat16

D_MODEL = 2048
GRID_W = 64
EPS = 1e-6
N_MOD = 6
N_FOURIER_GROUPS = 4
FOURIER_GROUP_DIM = 256
FOURIER_DIM = 1024
MLA_HEADS = 8
QK_NOPE_DIM = 128
QK_ROPE_DIM = 64
QK_HEAD_DIM = 192
V_HEAD_DIM = 128
Q_LORA_RANK = 768
KV_LORA_RANK = 512
MLA_DIM = 1024
ROPE_THETA = 10000.0
Q_OFF = FOURIER_DIM
KV_OFF = Q_OFF + Q_LORA_RANK
ROPE_OFF = KV_OFF + KV_LORA_RANK
N_GROUPS = 8
EXPERTS_PER_GROUP = 8
N_EXPERTS = 64
D_EXPERT = 768

LANES = 128
HEAD_PAD = 256
V_PAD = 256
MOE_BLOCK = 128
ITEM_BLOCKS = 4
WEIGHT_SLOTS = 3
GATHER_UNROLL = 8
ATTN_KEY_CHUNK = 512
ROW_DMA_PRIORITY = 1
VMEM_LIMIT = 56 * 1024 * 1024
NEG_BIG = -1e30


def _cparams(sem):
    return pltpu.CompilerParams(dimension_semantics=sem, vmem_limit_bytes=VMEM_LIMIT)


def _bdot(a, b):
    return jnp.dot(a, b, preferred_element_type=F32)


def _ada_kernel(c_ref, w_ref, b_ref, o_ref):
    c = c_ref[...]
    s = (c * jax.nn.sigmoid(c)).astype(BF16)
    o_ref[...] = _bdot(s, w_ref[...].astype(BF16)) + b_ref[...]


def _ada_mod(cond8, w_ada, b_ada):
    d, n = w_ada.shape
    tn = 1024
    return pl.pallas_call(
        _ada_kernel,
        grid=(n // tn,),
        in_specs=[pl.BlockSpec((8, d), lambda i: (0, 0)),
                  pl.BlockSpec((d, tn), lambda i: (0, i)),
                  pl.BlockSpec((1, tn), lambda i: (0, i))],
        out_specs=pl.BlockSpec((8, tn), lambda i: (0, i)),
        out_shape=jax.ShapeDtypeStruct((8, n), F32),
        compiler_params=_cparams(("arbitrary",)),
        name="ada_mod",
    )(cond8, w_ada, b_ada.reshape(1, n))


def _in_proj_kernel(x_ref, sh_ref, sc_ref, g_ref, gq_ref, gkv_ref, wf_ref, wq_ref, wkv_ref, wr_ref, dc_ref,
                    *out_refs, with_q):
    x = x_ref[...]
    ms = jnp.mean(x * x, axis=-1, keepdims=True)
    a = g_ref[...] * (1.0 + sc_ref[...])
    h = (x * lax.rsqrt(ms + EPS) * a + sh_ref[...]).astype(BF16)
    if with_q:
        u_ref, cq_ref, ckv_ref, kr_ref = out_refs
        f = _bdot(h, wf_ref[...]).astype(BF16)
        dc = dc_ref[...]
        for g in range(N_FOURIER_GROUPS):
            lo = g * FOURIER_GROUP_DIM
            ug = _bdot(f[:, lo:lo + FOURIER_GROUP_DIM], dc)
            u_ref[:, lo:lo + FOURIER_GROUP_DIM] = ug[:, :FOURIER_GROUP_DIM]
            u_ref[:, FOURIER_DIM + lo:FOURIER_DIM + lo + FOURIER_GROUP_DIM] = ug[:, FOURIER_GROUP_DIM:]
        pq = _bdot(h, wq_ref[...])
        msq = jnp.mean(pq * pq, axis=-1, keepdims=True)
        cq_ref[...] = (pq * lax.rsqrt(msq + EPS) * gq_ref[...]).astype(BF16)
    else:
        ckv_ref, kr_ref = out_refs
    pkv = _bdot(h, wkv_ref[...])
    mskv = jnp.mean(pkv * pkv, axis=-1, keepdims=True)
    ckv_ref[...] = (pkv * lax.rsqrt(mskv + EPS) * gkv_ref[...]).astype(BF16)
    kr_ref[...] = _bdot(h, wr_ref[...])


def _in_proj(x2, shift, scale, rows_per_mod, g1, gq, gkv, wf, wq, wkv, wr2, dc, *, with_q, tm):
    t, d = x2.shape
    nt = t // tm
    tiles_per_mod = rows_per_mod // tm

    def const(shape):
        return pl.BlockSpec(shape, lambda i: (0,) * len(shape))

    mod_spec = pl.BlockSpec((None, 1, d), lambda i: (i // tiles_per_mod, 0, 0))
    in_specs = [pl.BlockSpec((tm, d), lambda i: (i, 0)), mod_spec, mod_spec,
                const((1, d)), const((1, Q_LORA_RANK)), const((1, KV_LORA_RANK)),
                const(wf.shape), const(wq.shape), const(wkv.shape), const(wr2.shape), const(dc.shape)]

    def rows(n):
        return pl.BlockSpec((tm, n), lambda i: (i, 0))

    out_specs = [rows(KV_LORA_RANK), rows(LANES)]
    out_shape = [jax.ShapeDtypeStruct((t, KV_LORA_RANK), BF16), jax.ShapeDtypeStruct((t, LANES), F32)]
    if with_q:
        out_specs = [rows(2 * FOURIER_DIM), rows(Q_LORA_RANK)] + out_specs
        out_shape = [jax.ShapeDtypeStruct((t, 2 * FOURIER_DIM), F32),
                     jax.ShapeDtypeStruct((t, Q_LORA_RANK), BF16)] + out_shape
    return pl.pallas_call(
        functools.partial(_in_proj_kernel, with_q=with_q),
        grid=(nt,),
        in_specs=in_specs,
        out_specs=out_specs,
        out_shape=out_shape,
        compiler_params=_cparams(("arbitrary",)),
        name="in_proj_x" if with_q else "in_proj_ctx",
    )(x2, shift, scale, g1, gq, gkv, wf, wq, wkv, wr2, dc)


def _swap_halves(y, first_half):
    return jnp.where(first_half, pltpu.roll(y, LANES - 16, 1), pltpu.roll(y, 16, 1))


def _qkv_kernel(*refs, with_q, with_rope):
    it = iter(refs)
    if with_q:
        cq_ref = next(it)
    ckv_ref = next(it)
    kr_ref = next(it)
    if with_rope:
        cos_ref = next(it)
        sin_ref = next(it)
    if with_q:
        wqn_ref = next(it)
        wqr_ref = next(it)
        gqn_ref = next(it)
        gqr_ref = next(it)
    wkn_ref = next(it)
    wv_ref = next(it)
    gkn_ref = next(it)
    gkr_ref = next(it)
    if with_q:
        q_ref = next(it)
    k_ref = next(it)
    v_ref = next(it)

    tm = ckv_ref.shape[0]
    lane = lax.broadcasted_iota(jnp.int32, (tm, LANES), 1)
    low = lane < QK_ROPE_DIM
    first_half = (lane % 32) < 16
    inv_dim = 1.0 / QK_HEAD_DIM

    def rope(y):
        if not with_rope:
            return y
        return y * cos_ref[...] + _swap_halves(y, first_half) * sin_ref[...]

    if with_q:
        cq = cq_ref[...]
        qn = _bdot(cq, wqn_ref[...])
        qr = _bdot(cq, wqr_ref[...])
        qscale = QK_HEAD_DIM ** -0.5
        for p in range(MLA_HEADS // 2):
            blk = qr[:, p * LANES:(p + 1) * LANES]
            sq = blk * blk
            ss_lo = jnp.sum(jnp.where(low, sq, 0.0), axis=-1, keepdims=True)
            ss_hi = jnp.sum(jnp.where(low, 0.0, sq), axis=-1, keepdims=True)
            scales = []
            for hh, ssr in ((2 * p, ss_lo), (2 * p + 1, ss_hi)):
                nh = qn[:, hh * LANES:(hh + 1) * LANES]
                ssq = jnp.sum(nh * nh, axis=-1, keepdims=True) + ssr
                s = lax.rsqrt(ssq * inv_dim + EPS)
                scales.append(s)
                q_ref[hh, :, 0:LANES] = (nh * s * gqn_ref[...] * qscale).astype(BF16)
            s_pair = jnp.where(low, scales[0], scales[1])
            r = rope(blk * s_pair * gqr_ref[...]) * qscale
            q_ref[2 * p, :, LANES:2 * LANES] = jnp.where(low, r, 0.0).astype(BF16)
            q_ref[2 * p + 1, :, LANES:2 * LANES] = jnp.where(low, pltpu.roll(r, QK_ROPE_DIM, 1), 0.0).astype(BF16)

    ckv = ckv_ref[...]
    kn = _bdot(ckv, wkn_ref[...])
    v = _bdot(ckv, wv_ref[...])
    kr = kr_ref[...]
    ss_r = jnp.sum(jnp.where(low, kr * kr, 0.0), axis=-1, keepdims=True)
    base = rope(kr * gkr_ref[...])
    ones_col = jnp.where(lane == 0, 1.0, 0.0).astype(BF16)
    for hh in range(MLA_HEADS):
        nh = kn[:, hh * LANES:(hh + 1) * LANES]
        ssq = jnp.sum(nh * nh, axis=-1, keepdims=True) + ss_r
        s = lax.rsqrt(ssq * inv_dim + EPS)
        k_ref[hh, :, 0:LANES] = (nh * s * gkn_ref[...]).astype(BF16)
        k_ref[hh, :, LANES:2 * LANES] = jnp.where(low, base * s, 0.0).astype(BF16)
        v_ref[hh, :, 0:LANES] = v[:, hh * LANES:(hh + 1) * LANES].astype(BF16)
        v_ref[hh, :, LANES:2 * LANES] = ones_col


def _qkv(cq, ckv, kr2, cos_t, sin_t, wqn, wqr, gqn, gqr2, wkn, wv, gkn, gkr2, *, batch, seq, tm, with_q,
         with_rope):
    t = ckv.shape[0]
    nt = t // tm
    tiles_per_b = seq // tm

    def rows(n):
        return pl.BlockSpec((tm, n), lambda i: (i, 0))

    def const(arr):
        return pl.BlockSpec(arr.shape, lambda i: (0,) * arr.ndim)

    tab_spec = pl.BlockSpec((tm, LANES), lambda i: (i % tiles_per_b, 0))

    def head_out(width):
        return pl.BlockSpec((None, MLA_HEADS, tm, width), lambda i: (i // tiles_per_b, 0, i % tiles_per_b, 0))

    args, in_specs = [], []
    if with_q:
        args.append(cq)
        in_specs.append(rows(Q_LORA_RANK))
    args += [ckv, kr2]
    in_specs += [rows(KV_LORA_RANK), rows(LANES)]
    if with_rope:
        args += [cos_t, sin_t]
        in_specs += [tab_spec, tab_spec]
    if with_q:
        args += [wqn, wqr, gqn, gqr2]
        in_specs += [const(wqn), const(wqr), const(gqn), const(gqr2)]
    args += [wkn, wv, gkn, gkr2]
    in_specs += [const(wkn), const(wv), const(gkn), const(gkr2)]

    out_specs = [head_out(HEAD_PAD), head_out(V_PAD)]
    out_shape = [jax.ShapeDtypeStruct((batch, MLA_HEADS, seq, HEAD_PAD), BF16),
                 jax.ShapeDtypeStruct((batch, MLA_HEADS, seq, V_PAD), BF16)]
    if with_q:
        out_specs = [head_out(HEAD_PAD)] + out_specs
        out_shape = [jax.ShapeDtypeStruct((batch, MLA_HEADS, seq, HEAD_PAD), BF16)] + out_shape
    return pl.pallas_call(
        functools.partial(_qkv_kernel, with_q=with_q, with_rope=with_rope),
        grid=(nt,),
        in_specs=in_specs,
        out_specs=out_specs,
        out_shape=out_shape,
        compiler_params=_cparams(("arbitrary",)),
        name="qkv_x" if with_q else "kv_ctx",
    )(*args)


def _attn_kernel(q_ref, kx_ref, kc_ref, vx_ref, vc_ref, o_ref):
    q = q_ref[...]
    tq = q.shape[0]
    dn = (((1,), (1,)), ((), ()))

    def chunk(k, v, state):
        m, acc = state
        s = lax.dot_general(q, k, dn, preferred_element_type=F32)
        m_new = jnp.maximum(m, jnp.max(s, axis=-1, keepdims=True))
        p = jnp.exp(s - m_new).astype(BF16)
        acc = jnp.exp(m - m_new) * acc + _bdot(p, v)
        return m_new, acc

    state = (jnp.full((tq, 1), NEG_BIG, F32), jnp.zeros((tq, V_PAD), F32))
    for c in range(kx_ref.shape[0] // ATTN_KEY_CHUNK):
        rows = slice(c * ATTN_KEY_CHUNK, (c + 1) * ATTN_KEY_CHUNK)
        state = chunk(kx_ref[rows, :], vx_ref[rows, :], state)
    _, acc = chunk(kc_ref[...], vc_ref[...], state)
    o_ref[...] = acc[:, :V_HEAD_DIM] / acc[:, V_HEAD_DIM:V_HEAD_DIM + 1]


def _attention(q, kx, kc, vx, vc, *, tq):
    b, h, s, _ = q.shape
    lc = kc.shape[2]
    return pl.pallas_call(
        _attn_kernel,
        grid=(b, h, s // tq),
        in_specs=[pl.BlockSpec((None, None, tq, HEAD_PAD), lambda bi, hi, qi: (bi, hi, qi, 0)),
                  pl.BlockSpec((None, None, s, HEAD_PAD), lambda bi, hi, qi: (bi, hi, 0, 0)),
                  pl.BlockSpec((None, None, lc, HEAD_PAD), lambda bi, hi, qi: (bi, hi, 0, 0)),
                  pl.BlockSpec((None, None, s, V_PAD), lambda bi, hi, qi: (bi, hi, 0, 0)),
                  pl.BlockSpec((None, None, lc, V_PAD), lambda bi, hi, qi: (bi, hi, 0, 0))],
        out_specs=pl.BlockSpec((None, tq, V_HEAD_DIM), lambda bi, hi, qi: (bi, qi, hi)),
        out_shape=jax.ShapeDtypeStruct((b, s, h * V_HEAD_DIM), F32),
        compiler_params=_cparams(("arbitrary", "arbitrary", "arbitrary")),
        name="attention",
    )(q, kx, kc, vx, vc)


def _dft_stage1_kernel(u_ref, r_ref, e_ref, o_ref):
    n = 2 * GRID_W * 8
    t = _bdot(r_ref[...].astype(BF16), e_ref[...])
    row = lax.broadcasted_iota(jnp.int32, (n, n), 0)
    col = lax.broadcasted_iota(jnp.int32, (n, n), 1)
    t = jnp.where((row % 8) == (col % 8), t, 0.0).astype(BF16)
    u = u_ref[...].reshape(GRID_W * 8, 2 * FOURIER_DIM)
    rhs = jnp.concatenate([u[:, :FOURIER_DIM], u[:, FOURIER_DIM:]], axis=0).astype(BF16)
    a = _bdot(t, rhs)
    o_ref[...] = a.reshape(2, GRID_W, 8, FOURIER_DIM)


def _dft_stage2_kernel(a_ref, t_ref, o_ref):
    rhs = a_ref[...].reshape(2 * 8 * GRID_W, FOURIER_DIM).astype(BF16)
    y = _bdot(t_ref[...].astype(BF16), rhs)
    o_ref[...] = y.reshape(GRID_W, 8, FOURIER_DIM)


def _seq_dft_tables(n_seq):
    w = GRID_W
    ch = np.arange(8).reshape(8, 1, 1, 1)
    kb = np.arange(w).reshape(1, w, 1, 1)
    j = np.arange(8).reshape(1, 1, 8, 1)
    r = np.arange(w).reshape(1, 1, 1, w)
    ang = (2.0 * np.pi / n_seq) * ((kb * (w * r + 8 * ch + j)) % n_seq)
    c, s = np.cos(ang), np.sin(ang)
    rot = np.stack([np.stack([c, s], axis=3), np.stack([-s, c], axis=3)], axis=1)
    r1 = rot.reshape(8, 2 * w * 8, 2 * w).astype(np.float32)
    expand = (np.arange(2 * w * 8)[None, :] // 8 == np.arange(2 * w)[:, None]).astype(np.float32)
    ka = np.arange(w).reshape(w, 1)
    cp = np.arange(w).reshape(1, w)
    ang2 = (2.0 * np.pi / w) * ((ka * cp) % w)
    norm = 1.0 / math.sqrt(n_seq * FOURIER_GROUP_DIM)
    cs = np.stack([np.cos(ang2), np.sin(ang2)], axis=1) * norm
    eye8 = np.eye(8)
    t2 = (cs[:, None, :, None, :] * eye8[None, :, None, :, None]).reshape(w * 8, 2 * 8 * w).astype(np.float32)
    return jnp.asarray(r1), jnp.asarray(expand).astype(BF16), jnp.asarray(t2)


def _seq_dft(u, batch, n_seq):
    w = GRID_W
    r1, expand, t2 = _seq_dft_tables(n_seq)
    u5 = u.reshape(batch, w, 8, 8, 2 * FOURIER_DIM)
    a = pl.pallas_call(
        _dft_stage1_kernel,
        grid=(batch, 8),
        in_specs=[pl.BlockSpec((None, w, None, 8, 2 * FOURIER_DIM), lambda b, c: (b, 0, c, 0, 0)),
                  pl.BlockSpec((None, 2 * w * 8, 2 * w), lambda b, c: (c, 0, 0)),
                  pl.BlockSpec((2 * w, 2 * w * 8), lambda b, c: (0, 0))],
        out_specs=pl.BlockSpec((None, 2, w, None, 8, FOURIER_DIM), lambda b, c: (b, 0, 0, c, 0, 0)),
        out_shape=jax.ShapeDtypeStruct((batch, 2, w, 8, 8, FOURIER_DIM), F32),
        compiler_params=_cparams(("arbitrary", "arbitrary")),
        name="seq_dft_stage1",
    )(u5, r1, expand)
    y = pl.pallas_call(
        _dft_stage2_kernel,
        grid=(batch, 8),
        in_specs=[pl.BlockSpec((None, 2, 8, 8, 8, FOURIER_DIM), lambda b, k: (b, 0, k, 0, 0, 0)),
                  pl.BlockSpec((w * 8, 2 * 8 * w), lambda b, k: (0, 0))],
        out_specs=pl.BlockSpec((None, w, None, 8, FOURIER_DIM), lambda b, k: (b, 0, k, 0, 0)),
        out_shape=jax.ShapeDtypeStruct((batch, w, 8, 8, FOURIER_DIM), F32),
        compiler_params=_cparams(("arbitrary", "arbitrary")),
        name="seq_dft_stage2",
    )(a, t2)
    return y.reshape(batch * n_seq, FOURIER_DIM)


def _out_router_kernel(x_ref, four_ref, attn_ref, gt1_ref, sh2_ref, sc2_ref, gf_ref, ga_ref, g2_ref,
                       wof_ref, woa_ref, wrh_ref, wrl_ref, br_ref,
                       xnew_ref, hm_ref, ri_ref, rw_ref, cnt_ref, carry_ref):
    i = pl.program_id(0)
    tm = x_ref.shape[0]

    @pl.when(i == 0)
    def _():
        carry_ref[...] = jnp.zeros_like(carry_ref)

    def norm(v, g):
        return (v * lax.rsqrt(jnp.mean(v * v, axis=-1, keepdims=True) + EPS) * g).astype(BF16)

    mix = _bdot(norm(four_ref[...], gf_ref[...]), wof_ref[...]) + _bdot(norm(attn_ref[...], ga_ref[...]), woa_ref[...])
    xn = x_ref[...] + gt1_ref[...] * mix
    xnew_ref[...] = xn
    ms = jnp.mean(xn * xn, axis=-1, keepdims=True)
    hm = xn * lax.rsqrt(ms + EPS) * (g2_ref[...] * (1.0 + sc2_ref[...])) + sh2_ref[...]
    hm_ref[...] = hm

    hm_hi = hm.astype(BF16)
    hm_lo = (hm - hm_hi.astype(F32)).astype(BF16)
    logits = _bdot(hm_hi, wrh_ref[...]) + _bdot(hm_lo, wrh_ref[...]) + _bdot(hm_hi, wrl_ref[...]) + br_ref[...]
    lane = lax.broadcasted_iota(jnp.int32, (tm, LANES), 1)
    lanef = lane.astype(F32)
    far = 1e9

    lg = jnp.where(lane < N_GROUPS, logits, NEG_BIG)
    m1 = jnp.max(lg, axis=-1, keepdims=True)
    g_p = 1.0 / jnp.sum(jnp.exp(lg - m1), axis=-1, keepdims=True)
    gidx = jnp.min(jnp.where(lg >= m1, lanef, far), axis=-1, keepdims=True)
    lo = N_GROUPS + EXPERTS_PER_GROUP * gidx
    in_group = jnp.where(lanef >= lo, jnp.where(lanef < lo + EXPERTS_PER_GROUP, 1.0, 0.0), 0.0) > 0.5
    le = jnp.where(in_group, logits, NEG_BIG)
    m2 = jnp.max(le, axis=-1, keepdims=True)
    idx1 = jnp.min(jnp.where(le >= m2, lanef, far), axis=-1, keepdims=True)
    le2 = jnp.where(lanef == idx1, NEG_BIG, le)
    m3 = jnp.max(le2, axis=-1, keepdims=True)
    idx2 = jnp.min(jnp.where(le2 >= m3, lanef, far), axis=-1, keepdims=True)
    t = jnp.exp(m3 - m2)
    p1 = 1.0 / (1.0 + t)
    p2 = t / (1.0 + t)
    e1 = idx1 - N_GROUPS
    e2 = idx2 - N_GROUPS

    oh1 = jnp.where(lanef == e1, 1.0, 0.0)
    oh2 = jnp.where(lanef == e2, 1.0, 0.0)
    ohs = oh1 + oh2
    row = lax.broadcasted_iota(jnp.int32, (tm, tm), 0)
    col = lax.broadcasted_iota(jnp.int32, (tm, tm), 1)
    tri = jnp.where(row > col, 1.0, 0.0).astype(BF16)
    before = _bdot(tri, ohs.astype(BF16)) + carry_ref[...]
    rank1 = jnp.sum(oh1 * before, axis=-1, keepdims=True)
    rank2 = jnp.sum(oh2 * before, axis=-1, keepdims=True)
    carry = carry_ref[...] + jnp.sum(ohs, axis=0, keepdims=True)
    carry_ref[...] = carry
    cnt_ref[...] = jnp.broadcast_to(carry, cnt_ref.shape)

    ri = jnp.where(lane == 0, e1, jnp.where(lane == 1, e2, jnp.where(lane == 2, rank1, jnp.where(lane == 3, rank2, 0.0))))
    ri_ref[...] = ri.astype(jnp.int32)
    rw_ref[...] = jnp.where(lane == 0, g_p * p1, jnp.where(lane == 1, g_p * p2, 0.0))


def _out_router(x2, four, attn, gt1, sh2, sc2, gf, ga, g2, wof, woa, wrh, wrl, br, *, seq, tm):
    t, d = x2.shape
    nt = t // tm
    tiles_per_b = seq // tm

    def rows(n):
        return pl.BlockSpec((tm, n), lambda i: (i, 0))

    def const(arr):
        return pl.BlockSpec(arr.shape, lambda i: (0,) * arr.ndim)

    mod_spec = pl.BlockSpec((None, 1, d), lambda i: (i // tiles_per_b, 0, 0))
    return pl.pallas_call(
        _out_router_kernel,
        grid=(nt,),
        in_specs=[rows(d), rows(FOURIER_DIM), rows(MLA_DIM), mod_spec, mod_spec, mod_spec,
                  const(gf), const(ga), const(g2), const(wof), const(woa), const(wrh), const(wrl), const(br)],
        out_specs=[rows(d), rows(d), rows(LANES), rows(LANES), pl.BlockSpec((8, LANES), lambda i: (0, 0))],
        out_shape=[jax.ShapeDtypeStruct((t, d), F32), jax.ShapeDtypeStruct((t, d), F32),
                   jax.ShapeDtypeStruct((t, LANES), jnp.int32), jax.ShapeDtypeStruct((t, LANES), F32),
                   jax.ShapeDtypeStruct((8, LANES), F32)],
        scratch_shapes=[pltpu.VMEM((1, LANES), F32)],
        compiler_params=_cparams(("arbitrary",)),
        name="out_proj_router",
    )(x2, four, attn, gt1, sh2, sc2, gf, ga, g2, wof, woa, wrh, wrl, br)


def _moe_kernel(item_e, item_blk0, item_nblk, row_tok, used_blocks,
                hm_hbm, wg_hbm, wu_hbm, wd_hbm, y_hbm,
                xg, xb, acc, wg_buf, wu_buf, wd_buf, gsem, osem, wsem):
    i = pl.program_id(0)
    j = pl.program_id(1)
    n_items = pl.num_programs(0)
    nj = pl.num_programs(1)
    tj = wg_buf.shape[2]
    step = i * nj + j
    n_steps = n_items * nj
    slot = i % 2
    wslot = step % WEIGHT_SLOTS
    nblk = item_nblk[i]

    def weight_copies(st, ws):
        e = item_e[st // nj]
        c0 = pl.multiple_of((st % nj) * tj, tj)
        return (pltpu.make_async_copy(wg_hbm.at[e, :, pl.ds(c0, tj)], wg_buf.at[ws], wsem.at[ws]),
                pltpu.make_async_copy(wu_hbm.at[e, :, pl.ds(c0, tj)], wu_buf.at[ws], wsem.at[ws]),
                pltpu.make_async_copy(wd_hbm.at[e, pl.ds(c0, tj), :], wd_buf.at[ws], wsem.at[ws]))

    def start_weights(st):
        st_c = jnp.minimum(st, n_steps - 1)

        @pl.when(jnp.logical_and(st < n_steps, item_nblk[st_c // nj] > 0))
        def _():
            for cp in weight_copies(st_c, st_c % WEIGHT_SLOTS):
                cp.start()

    def gather_copy(tok, sl, r):
        return pltpu.make_async_copy(hm_hbm.at[pl.ds(tok, 1)], xg.at[sl, pl.ds(r, 1)], gsem.at[sl])

    def issue_gather(it, sl):
        r0 = item_blk0[it] * MOE_BLOCK

        def body(r8, carry):
            for k in range(GATHER_UNROLL):
                r = r8 * GATHER_UNROLL + k
                gather_copy(row_tok[r0 + r], sl, r).start(priority=ROW_DMA_PRIORITY)
            return carry

        lax.fori_loop(0, item_nblk[it] * (MOE_BLOCK // GATHER_UNROLL), body, 0)

    def wait_gather(it, sl):
        for b in range(ITEM_BLOCKS):
            @pl.when(b < item_nblk[it])
            def _():
                pltpu.make_async_copy(hm_hbm.at[pl.ds(0, MOE_BLOCK)], xg.at[sl, pl.ds(b * MOE_BLOCK, MOE_BLOCK)],
                                      gsem.at[sl]).wait()

    def out_copy(it, m):
        r0 = pl.multiple_of(item_blk0[it] * MOE_BLOCK, MOE_BLOCK)
        return pltpu.make_async_copy(acc.at[pl.ds(0, m)], y_hbm.at[pl.ds(r0, m)], osem.at[0])

    def wait_out(it):
        for nb in range(1, ITEM_BLOCKS + 1):
            @pl.when(item_nblk[it] == nb)
            def _():
                out_copy(it, nb * MOE_BLOCK).wait()

    @pl.when(step == 0)
    def _():
        for st in range(WEIGHT_SLOTS - 1):
            start_weights(st)
        issue_gather(0, 0)

    start_weights(step + WEIGHT_SLOTS - 1)

    @pl.when(j == 0)
    def _():
        @pl.when(i > 0)
        def _():
            wait_out(i - 1)

        wait_gather(i, slot)

        @pl.when(i + 1 < n_items)
        def _():
            issue_gather(i + 1, 1 - slot)

    @pl.when(nblk > 0)
    def _():
        for cp in weight_copies(step, wslot):
            cp.wait()

    for nb in range(1, ITEM_BLOCKS + 1):
        m = nb * MOE_BLOCK

        @pl.when(nblk == nb)
        def _():
            @pl.when(j == 0)
            def _():
                xb[0:m, :] = xg[slot, 0:m, :].astype(BF16)

            x = xb[0:m, :]
            g = _bdot(x, wg_buf[wslot].astype(BF16))
            u = _bdot(x, wu_buf[wslot].astype(BF16))
            a = (g * jax.nn.sigmoid(g) * u).astype(BF16)
            y = _bdot(a, wd_buf[wslot].astype(BF16))

            @pl.when(j == 0)
            def _():
                acc[0:m, :] = y

            @pl.when(j > 0)
            def _():
                acc[0:m, :] += y

            @pl.when(j == nj - 1)
            def _():
                out_copy(i, m).start(priority=ROW_DMA_PRIORITY)

    @pl.when(step == n_steps - 1)
    def _():
        wait_out(i)
        n_blocks = y_hbm.shape[0] // MOE_BLOCK
        acc[0:MOE_BLOCK, :] = jnp.zeros((MOE_BLOCK, acc.shape[1]), F32)

        def tail_copy(blk):
            r0 = pl.multiple_of(blk * MOE_BLOCK, MOE_BLOCK)
            return pltpu.make_async_copy(acc.at[pl.ds(0, MOE_BLOCK)], y_hbm.at[pl.ds(r0, MOE_BLOCK)], osem.at[0])

        def start_body(blk, carry):
            tail_copy(blk).start()
            return carry

        def wait_body(blk, carry):
            tail_copy(blk).wait()
            return carry

        lax.fori_loop(used_blocks[0], n_blocks, start_body, 0)
        lax.fori_loop(used_blocks[0], n_blocks, wait_body, 0)


def _moe(hm, w_gate, w_up, w_down, item_e, item_blk0, item_nblk, row_tok, used_blocks, *, n_rows, tj):
    t, d = hm.shape
    n_items = item_e.shape[0]
    nj = D_EXPERT // tj
    rows = ITEM_BLOCKS * MOE_BLOCK
    any_spec = pl.BlockSpec(memory_space=pl.ANY)
    grid_spec = pltpu.PrefetchScalarGridSpec(
        num_scalar_prefetch=5,
        grid=(n_items, nj),
        in_specs=[any_spec, any_spec, any_spec, any_spec],
        out_specs=any_spec,
        scratch_shapes=[pltpu.VMEM((2, rows, d), F32),
                        pltpu.VMEM((rows, d), BF16),
                        pltpu.VMEM((rows, d), F32),
                        pltpu.VMEM((WEIGHT_SLOTS, d, tj), F32),
                        pltpu.VMEM((WEIGHT_SLOTS, d, tj), F32),
                        pltpu.VMEM((WEIGHT_SLOTS, tj, d), F32),
                        pltpu.SemaphoreType.DMA((2,)),
                        pltpu.SemaphoreType.DMA((1,)),
                        pltpu.SemaphoreType.DMA((WEIGHT_SLOTS,))],
    )
    return pl.pallas_call(
        _moe_kernel,
        grid_spec=grid_spec,
        out_shape=jax.ShapeDtypeStruct((n_rows, d), F32),
        compiler_params=_cparams(("arbitrary", "arbitrary")),
        name="moe_experts",
    )(item_e, item_blk0, item_nblk, row_tok, used_blocks, hm, w_gate, w_up, w_down)


def _combine_kernel(dest, x_ref, gt2_ref, rw_ref, y_hbm, o_ref, ybuf, sem):
    i = pl.program_id(0)
    n = pl.num_programs(0)
    tm = x_ref.shape[0]
    slot = i % 2

    def issue(it, sl):
        base = it * tm

        def body(r4, carry):
            for rr in range(GATHER_UNROLL // 2):
                r = r4 * (GATHER_UNROLL // 2) + rr
                for k in range(2):
                    pltpu.make_async_copy(y_hbm.at[pl.ds(dest[2 * (base + r) + k], 1)], ybuf.at[sl, k, pl.ds(r, 1)],
                                          sem.at[sl]).start(priority=ROW_DMA_PRIORITY)
            return carry

        lax.fori_loop(0, tm // (GATHER_UNROLL // 2), body, 0)

    @pl.when(i == 0)
    def _():
        issue(0, 0)

    for k in range(2):
        pltpu.make_async_copy(y_hbm.at[pl.ds(0, tm)], ybuf.at[slot, k], sem.at[slot]).wait()

    @pl.when(i + 1 < n)
    def _():
        issue(i + 1, 1 - slot)

    w = rw_ref[...]
    moe = w[:, 0:1] * ybuf[slot, 0] + w[:, 1:2] * ybuf[slot, 1]
    o_ref[...] = x_ref[...] + gt2_ref[...] * moe


def _combine(dest, xnew, gt2, rw, y, *, seq, tm):
    t, d = xnew.shape
    tiles_per_b = seq // tm
    grid_spec = pltpu.PrefetchScalarGridSpec(
        num_scalar_prefetch=1,
        grid=(t // tm,),
        in_specs=[pl.BlockSpec((tm, d), lambda i, ds: (i, 0)),
                  pl.BlockSpec((None, 1, d), lambda i, ds: (i // tiles_per_b, 0, 0)),
                  pl.BlockSpec((tm, LANES), lambda i, ds: (i, 0)),
                  pl.BlockSpec(memory_space=pl.ANY)],
        out_specs=pl.BlockSpec((tm, d), lambda i, ds: (i, 0)),
        scratch_shapes=[pltpu.VMEM((2, 2, tm, d), F32), pltpu.SemaphoreType.DMA((2,))],
    )
    return pl.pallas_call(
        _combine_kernel,
        grid_spec=grid_spec,
        out_shape=jax.ShapeDtypeStruct((t, d), F32),
        compiler_params=_cparams(("arbitrary",)),
        name="moe_combine",
    )(dest, xnew, gt2, rw, y)


def _rope_tables(n_tokens):
    rows = n_tokens // GRID_W
    row = jnp.repeat(jnp.arange(rows, dtype=jnp.int32), GRID_W).astype(F32)
    col = jnp.tile(jnp.arange(GRID_W, dtype=jnp.int32), rows).astype(F32)
    n_freq = QK_ROPE_DIM // 4
    inv = ROPE_THETA ** (-jnp.arange(n_freq, dtype=F32) / n_freq)
    ar = row[:, None] * inv[None, :]
    ac = col[:, None] * inv[None, :]
    cr, sr, cc, sc = jnp.cos(ar), jnp.sin(ar), jnp.cos(ac), jnp.sin(ac)
    cos64 = jnp.concatenate([cr, cr, cc, cc], axis=-1)
    sin64 = jnp.concatenate([-sr, sr, -sc, sc], axis=-1)
    return jnp.tile(cos64, (1, 2)), jnp.tile(sin64, (1, 2))


def _channel_dft_table():
    c = np.arange(FOURIER_GROUP_DIM).reshape(-1, 1)
    k = np.arange(FOURIER_GROUP_DIM).reshape(1, -1)
    ang = (2.0 * np.pi / FOURIER_GROUP_DIM) * ((c * k) % FOURIER_GROUP_DIM)
    return jnp.asarray(np.concatenate([np.cos(ang), -np.sin(ang)], axis=1).astype(np.float32)).astype(BF16)


def _split_heads(w, widths):
    k = w.shape[0]
    wh = w.reshape(k, MLA_HEADS, sum(widths))
    outs, off = [], 0
    for wd in widths:
        outs.append(wh[:, :, off:off + wd].reshape(k, MLA_HEADS * wd))
        off += wd
    return outs


def kernel(x, c, ctx, c_ctx, w_ada, b_ada, g_norm1, g_norm2, w_in, g_q_a, g_kv_a, w_uq, w_ukv, g_qk_q, g_qk_k,
           g_out_four, g_out_attn, w_out, w_router_group, b_router_group, w_router_expert, b_router_expert,
           w_gate, w_up, w_down):
    b, s, d = x.shape
    lc = ctx.shape[1]
    t = b * s
    layer_params = (w_ada, b_ada, g_norm1, g_norm2, w_in, g_q_a, g_kv_a, w_uq, w_ukv, g_qk_q, g_qk_k, g_out_four,
                    g_out_attn, w_out, w_router_group, b_router_group, w_router_expert, b_router_expert,
                    w_gate, w_up, w_down)
    assert all(p.shape[0] == 1 for p in layer_params), "single-layer block"
    (w_ada, b_ada, g_norm1, g_norm2, w_in, g_q_a, g_kv_a, w_uq, w_ukv, g_qk_q, g_qk_k, g_out_four,
     g_out_attn, w_out, w_router_group, b_router_group, w_router_expert, b_router_expert,
     w_gate, w_up, w_down) = [p.reshape(p.shape[1:]) for p in layer_params]

    cond8 = jnp.concatenate([c, c_ctx[None, :], jnp.zeros((8 - b - 1, d), F32)], axis=0)
    mods = _ada_mod(cond8, w_ada, b_ada)
    sh1, sc1, gt1, sh2, sc2, gt2 = [m[:b].reshape(b, 1, d) for m in jnp.split(mods, N_MOD, axis=-1)]
    csh1, csc1 = [m[b:b + 1].reshape(1, 1, d) for m in jnp.split(mods, N_MOD, axis=-1)[:2]]

    wf = w_in[:, :Q_OFF].astype(BF16)
    wq = w_in[:, Q_OFF:KV_OFF].astype(BF16)
    wkv = w_in[:, KV_OFF:ROPE_OFF].astype(BF16)
    wr = w_in[:, ROPE_OFF:].astype(BF16)
    wr2 = jnp.concatenate([wr, wr], axis=1)
    dc = _channel_dft_table()
    wqn, wqr = [w.astype(BF16) for w in _split_heads(w_uq, (QK_NOPE_DIM, QK_ROPE_DIM))]
    wkn, wv = [w.astype(BF16) for w in _split_heads(w_ukv, (QK_NOPE_DIM, V_HEAD_DIM))]
    gqn = g_qk_q[:QK_NOPE_DIM].reshape(1, -1)
    gqr2 = jnp.tile(g_qk_q[QK_NOPE_DIM:], 2).reshape(1, -1)
    gkn = g_qk_k[:QK_NOPE_DIM].reshape(1, -1)
    gkr2 = jnp.tile(g_qk_k[QK_NOPE_DIM:], 2).reshape(1, -1)
    g1 = g_norm1.reshape(1, d)
    gq = g_q_a.reshape(1, -1)
    gkv = g_kv_a.reshape(1, -1)

    x2 = x.reshape(t, d)
    u, cq, ckv, kr2 = _in_proj(x2, sh1, sc1, s, g1, gq, gkv, wf, wq, wkv, wr2, dc, with_q=True, tm=512)
    ckv_c, kr2_c = _in_proj(ctx.reshape(b * lc, d), csh1, csc1, b * lc, g1, gq, gkv, wf, wq, wkv, wr2, dc,
                            with_q=False, tm=lc)

    cos_t, sin_t = _rope_tables(s)
    q, kx, vx = _qkv(cq, ckv, kr2, cos_t, sin_t, wqn, wqr, gqn, gqr2, wkn, wv, gkn, gkr2,
                     batch=b, seq=s, tm=512, with_q=True, with_rope=True)
    kc, vc = _qkv(None, ckv_c, kr2_c, None, None, None, None, None, None, wkn, wv, gkn, gkr2,
                  batch=b, seq=lc, tm=lc, with_q=False, with_rope=False)

    attn = _attention(q, kx, kc, vx, vc, tq=1024).reshape(t, MLA_DIM)
    four = _seq_dft(u, b, s)

    wof = w_out[:FOURIER_DIM].astype(BF16)
    woa = w_out[FOURIER_DIM:].astype(BF16)
    n_route = N_GROUPS + N_EXPERTS
    wrt = jnp.concatenate([w_router_group, w_router_expert, jnp.zeros((d, LANES - n_route), F32)], axis=1)
    wrt_hi = wrt.astype(BF16)
    wrt_lo = (wrt - wrt_hi.astype(F32)).astype(BF16)
    brt = jnp.concatenate([b_router_group, b_router_expert, jnp.zeros((LANES - n_route,), F32)]).reshape(1, -1)
    xnew, hm, ri, rw, cnt = _out_router(x2, four, attn, gt1, sh2, sc2, g_out_four.reshape(1, -1),
                                        g_out_attn.reshape(1, -1), g_norm2.reshape(1, d), wof, woa, wrt_hi, wrt_lo, brt,
                                        seq=s, tm=256)

    counts = cnt[0, :N_EXPERTS].astype(jnp.int32)
    nblk_e = (counts + MOE_BLOCK - 1) // MOE_BLOCK
    blk_end = jnp.cumsum(nblk_e)
    blk_start = blk_end - nblk_e
    n_slots = t * 2
    n_blocks = -(-(n_slots + N_EXPERTS * (MOE_BLOCK - 1)) // MOE_BLOCK)
    n_rows = n_blocks * MOE_BLOCK
    e12 = ri[:, 0:2]
    seg_start = jnp.sum(jnp.where(e12[:, :, None] == jnp.arange(N_EXPERTS, dtype=jnp.int32), blk_start, 0), axis=-1)
    dest = (seg_start * MOE_BLOCK + ri[:, 2:4]).reshape(-1)
    tok_ids = jnp.repeat(jnp.arange(t, dtype=jnp.int32), 2)
    row_tok = jnp.zeros((n_rows,), jnp.int32).at[dest].set(tok_ids)
    items_e = (nblk_e + ITEM_BLOCKS - 1) // ITEM_BLOCKS
    item_end = jnp.cumsum(items_e)
    n_items = (n_blocks + (ITEM_BLOCKS - 1) * N_EXPERTS) // ITEM_BLOCKS
    idx = jnp.arange(n_items, dtype=jnp.int32)
    total = item_end[-1]
    idx_c = jnp.minimum(idx, total - 1)
    ie = jnp.minimum(jnp.sum(idx_c[:, None] >= item_end[None, :], axis=1), N_EXPERTS - 1).astype(jnp.int32)
    local = idx_c - (item_end - items_e)[ie]
    item_blk0 = (blk_start[ie] + ITEM_BLOCKS * local).astype(jnp.int32)
    item_nblk = jnp.where(idx < total, jnp.clip(nblk_e[ie] - ITEM_BLOCKS * local, 0, ITEM_BLOCKS), 0).astype(jnp.int32)

    used_blocks = blk_end[-1:].astype(jnp.int32)
    y = _moe(hm, w_gate, w_up, w_down, ie, item_blk0, item_nblk, row_tok, used_blocks, n_rows=n_rows, tj=256)
    out = _combine(dest.astype(jnp.int32), xnew, gt2, rw, y, seq=s, tm=128)
    return out.reshape(b, s, d)
```

```python
import functools
import math

import numpy as np
import jax
import jax.numpy as jnp
from jax import lax
from jax.experimental import pallas as pl
from jax.experimental.pallas import tpu as pltpu

F32 = jnp.float32
BF16 = jnp.bfloat16

D_MODEL = 2048
GRID_W = 64
EPS = 1e-6
N_MOD = 6
N_FOURIER_GROUPS = 4
FOURIER_GROUP_DIM = 256
FOURIER_DIM = 1024
MLA_HEADS = 8
QK_NOPE_DIM = 128
QK_ROPE_DIM = 64
QK_HEAD_DIM = 192
V_HEAD_DIM = 128
Q_LORA_RANK = 768
KV_LORA_RANK = 512
MLA_DIM = 1024
ROPE_THETA = 10000.0
Q_OFF = FOURIER_DIM
KV_OFF = Q_OFF + Q_LORA_RANK
ROPE_OFF = KV_OFF + KV_LORA_RANK
N_GROUPS = 8
EXPERTS_PER_GROUP = 8
N_EXPERTS = 64
D_EXPERT = 768

LANES = 128
HEAD_PAD = 256
V_PAD = 256
MOE_BLOCK = 128
ITEM_BLOCKS = 4
WEIGHT_SLOTS = 3
GATHER_UNROLL = 8
ATTN_KEY_CHUNK = 512
ROW_DMA_PRIORITY = 1
VMEM_LIMIT = 56 * 1024 * 1024
NEG_BIG = -1e30


def _cparams(sem):
    return pltpu.CompilerParams(dimension_semantics=sem, vmem_limit_bytes=VMEM_LIMIT)


def _bdot(a, b):
    return jnp.dot(a, b, preferred_element_type=F32)


def _pack_halves(v):
    n = v.shape[1] // 2
    hi = pltpu.bitcast(v[:, :n].astype(BF16).astype(F32), jnp.uint32)
    lo = pltpu.bitcast(v[:, n:].astype(BF16).astype(F32), jnp.uint32)
    return hi | (lo >> 16)


def _unpack_halves(u):
    hi = pltpu.bitcast(u & jnp.uint32(0xFFFF0000), F32)
    lo = pltpu.bitcast(u << 16, F32)
    return hi, lo


def _ada_kernel(c_ref, w_ref, b_ref, o_ref):
    c = c_ref[...]
    s = (c * jax.nn.sigmoid(c)).astype(BF16)
    o_ref[...] = _bdot(s, w_ref[...].astype(BF16)) + b_ref[...]


def _ada_mod(cond8, w_ada, b_ada):
    d, n = w_ada.shape
    tn = 1024
    return pl.pallas_call(
        _ada_kernel,
        grid=(n // tn,),
        in_specs=[pl.BlockSpec((8, d), lambda i: (0, 0)),
                  pl.BlockSpec((d, tn), lambda i: (0, i)),
                  pl.BlockSpec((1, tn), lambda i: (0, i))],
        out_specs=pl.BlockSpec((8, tn), lambda i: (0, i)),
        out_shape=jax.ShapeDtypeStruct((8, n), F32),
        compiler_params=_cparams(("arbitrary",)),
        name="ada_mod",
    )(cond8, w_ada, b_ada.reshape(1, n))


def _in_proj_kernel(x_ref, sh_ref, sc_ref, g_ref, gq_ref, gkv_ref, wf_ref, wq_ref, wkv_ref, wr_ref, dc_ref,
                    *out_refs, with_q):
    x = x_ref[...]
    ms = jnp.mean(x * x, axis=-1, keepdims=True)
    a = g_ref[...] * (1.0 + sc_ref[...])
    h = (x * lax.rsqrt(ms + EPS) * a + sh_ref[...]).astype(BF16)
    if with_q:
        u_ref, cq_ref, ckv_ref, kr_ref = out_refs
        f = _bdot(h, wf_ref[...]).astype(BF16)
        dc = dc_ref[...]
        for g in range(N_FOURIER_GROUPS):
            lo = g * FOURIER_GROUP_DIM
            ug = _bdot(f[:, lo:lo + FOURIER_GROUP_DIM], dc)
            u_ref[:, lo:lo + FOURIER_GROUP_DIM] = ug[:, :FOURIER_GROUP_DIM]
            u_ref[:, FOURIER_DIM + lo:FOURIER_DIM + lo + FOURIER_GROUP_DIM] = ug[:, FOURIER_GROUP_DIM:]
        pq = _bdot(h, wq_ref[...])
        msq = jnp.mean(pq * pq, axis=-1, keepdims=True)
        cq_ref[...] = (pq * lax.rsqrt(msq + EPS) * gq_ref[...]).astype(BF16)
    else:
        ckv_ref, kr_ref = out_refs
    pkv = _bdot(h, wkv_ref[...])
    mskv = jnp.mean(pkv * pkv, axis=-1, keepdims=True)
    ckv_ref[...] = (pkv * lax.rsqrt(mskv + EPS) * gkv_ref[...]).astype(BF16)
    kr_ref[...] = _bdot(h, wr_ref[...])


def _in_proj(x2, shift, scale, rows_per_mod, g1, gq, gkv, wf, wq, wkv, wr2, dc, *, with_q, tm):
    t, d = x2.shape
    nt = t // tm
    tiles_per_mod = rows_per_mod // tm

    def const(shape):
        return pl.BlockSpec(shape, lambda i: (0,) * len(shape))

    mod_spec = pl.BlockSpec((None, 1, d), lambda i: (i // tiles_per_mod, 0, 0))
    in_specs = [pl.BlockSpec((tm, d), lambda i: (i, 0)), mod_spec, mod_spec,
                const((1, d)), const((1, Q_LORA_RANK)), const((1, KV_LORA_RANK)),
                const(wf.shape), const(wq.shape), const(wkv.shape), const(wr2.shape), const(dc.shape)]

    def rows(n):
        return pl.BlockSpec((tm, n), lambda i: (i, 0))

    out_specs = [rows(KV_LORA_RANK), rows(LANES)]
    out_shape = [jax.ShapeDtypeStruct((t, KV_LORA_RANK), BF16), jax.ShapeDtypeStruct((t, LANES), F32)]
    if with_q:
        out_specs = [rows(2 * FOURIER_DIM), rows(Q_LORA_RANK)] + out_specs
        out_shape = [jax.ShapeDtypeStruct((t, 2 * FOURIER_DIM), F32),
                     jax.ShapeDtypeStruct((t, Q_LORA_RANK), BF16)] + out_shape
    return pl.pallas_call(
        functools.partial(_in_proj_kernel, with_q=with_q),
        grid=(nt,),
        in_specs=in_specs,
        out_specs=out_specs,
        out_shape=out_shape,
        compiler_params=_cparams(("arbitrary",)),
        name="in_proj_x" if with_q else "in_proj_ctx",
    )(x2, shift, scale, g1, gq, gkv, wf, wq, wkv, wr2, dc)


def _swap_halves(y, first_half):
    return jnp.where(first_half, pltpu.roll(y, LANES - 16, 1), pltpu.roll(y, 16, 1))


def _qkv_kernel(*refs, with_q, with_rope):
    it = iter(refs)
    if with_q:
        cq_ref = next(it)
    ckv_ref = next(it)
    kr_ref = next(it)
    if with_rope:
        cos_ref = next(it)
        sin_ref = next(it)
    if with_q:
        wqn_ref = next(it)
        wqr_ref = next(it)
        gqn_ref = next(it)
        gqr_ref = next(it)
    wkn_ref = next(it)
    wv_ref = next(it)
    gkn_ref = next(it)
    gkr_ref = next(it)
    if with_q:
        q_ref = next(it)
    k_ref = next(it)
    v_ref = next(it)

    tm = ckv_ref.shape[0]
    lane = lax.broadcasted_iota(jnp.int32, (tm, LANES), 1)
    low = lane < QK_ROPE_DIM
    first_half = (lane % 32) < 16
    inv_dim = 1.0 / QK_HEAD_DIM

    def rope(y):
        if not with_rope:
            return y
        return y * cos_ref[...] + _swap_halves(y, first_half) * sin_ref[...]

    if with_q:
        cq = cq_ref[...]
        qn = _bdot(cq, wqn_ref[...])
        qr = _bdot(cq, wqr_ref[...])
        qscale = QK_HEAD_DIM ** -0.5
        for p in range(MLA_HEADS // 2):
            blk = qr[:, p * LANES:(p + 1) * LANES]
            sq = blk * blk
            ss_lo = jnp.sum(jnp.where(low, sq, 0.0), axis=-1, keepdims=True)
            ss_hi = jnp.sum(jnp.where(low, 0.0, sq), axis=-1, keepdims=True)
            scales = []
            for hh, ssr in ((2 * p, ss_lo), (2 * p + 1, ss_hi)):
                nh = qn[:, hh * LANES:(hh + 1) * LANES]
                ssq = jnp.sum(nh * nh, axis=-1, keepdims=True) + ssr
                s = lax.rsqrt(ssq * inv_dim + EPS)
                scales.append(s)
                q_ref[hh, :, 0:LANES] = (nh * s * gqn_ref[...] * qscale).astype(BF16)
            s_pair = jnp.where(low, scales[0], scales[1])
            r = rope(blk * s_pair * gqr_ref[...]) * qscale
            q_ref[2 * p, :, LANES:2 * LANES] = jnp.where(low, r, 0.0).astype(BF16)
            q_ref[2 * p + 1, :, LANES:2 * LANES] = jnp.where(low, pltpu.roll(r, QK_ROPE_DIM, 1), 0.0).astype(BF16)

    ckv = ckv_ref[...]
    kn = _bdot(ckv, wkn_ref[...])
    v = _bdot(ckv, wv_ref[...])
    kr = kr_ref[...]
    ss_r = jnp.sum(jnp.where(low, kr * kr, 0.0), axis=-1, keepdims=True)
    base = rope(kr * gkr_ref[...])
    ones_col = jnp.where(lane == 0, 1.0, 0.0).astype(BF16)
    for hh in range(MLA_HEADS):
        nh = kn[:, hh * LANES:(hh + 1) * LANES]
        ssq = jnp.sum(nh * nh, axis=-1, keepdims=True) + ss_r
        s = lax.rsqrt(ssq * inv_dim + EPS)
        k_ref[hh, :, 0:LANES] = (nh * s * gkn_ref[...]).astype(BF16)
        k_ref[hh, :, LANES:2 * LANES] = jnp.where(low, base * s, 0.0).astype(BF16)
        v_ref[hh, :, 0:LANES] = v[:, hh * LANES:(hh + 1) * LANES].astype(BF16)
        v_ref[hh, :, LANES:2 * LANES] = ones_col


def _qkv(cq, ckv, kr2, cos_t, sin_t, wqn, wqr, gqn, gqr2, wkn, wv, gkn, gkr2, *, batch, seq, tm, with_q,
         with_rope):
    t = ckv.shape[0]
    nt = t // tm
    tiles_per_b = seq // tm

    def rows(n):
        return pl.BlockSpec((tm, n), lambda i: (i, 0))

    def const(arr):
        return pl.BlockSpec(arr.shape, lambda i: (0,) * arr.ndim)

    tab_spec = pl.BlockSpec((tm, LANES), lambda i: (i % tiles_per_b, 0))

    def head_out(width):
        return pl.BlockSpec((None, MLA_HEADS, tm, width), lambda i: (i // tiles_per_b, 0, i % tiles_per_b, 0))

    args, in_specs = [], []
    if with_q:
        args.append(cq)
        in_specs.append(rows(Q_LORA_RANK))
    args += [ckv, kr2]
    in_specs += [rows(KV_LORA_RANK), rows(LANES)]
    if with_rope:
        args += [cos_t, sin_t]
        in_specs += [tab_spec, tab_spec]
    if with_q:
        args += [wqn, wqr, gqn, gqr2]
        in_specs += [const(wqn), const(wqr), const(gqn), const(gqr2)]
    args += [wkn, wv, gkn, gkr2]
    in_specs += [const(wkn), const(wv), const(gkn), const(gkr2)]

    out_specs = [head_out(HEAD_PAD), head_out(V_PAD)]
    out_shape = [jax.ShapeDtypeStruct((batch, MLA_HEADS, seq, HEAD_PAD), BF16),
                 jax.ShapeDtypeStruct((batch, MLA_HEADS, seq, V_PAD), BF16)]
    if with_q:
        out_specs = [head_out(HEAD_PAD)] + out_specs
        out_shape = [jax.ShapeDtypeStruct((batch, MLA_HEADS, seq, HEAD_PAD), BF16)] + out_shape
    return pl.pallas_call(
        functools.partial(_qkv_kernel, with_q=with_q, with_rope=with_rope),
        grid=(nt,),
        in_specs=in_specs,
        out_specs=out_specs,
        out_shape=out_shape,
        compiler_params=_cparams(("arbitrary",)),
        name="qkv_x" if with_q else "kv_ctx",
    )(*args)


def _attn_kernel(q_ref, kx_ref, kc_ref, vx_ref, vc_ref, o_ref):
    q = q_ref[...]
    tq = q.shape[0]
    dn = (((1,), (1,)), ((), ()))

    def chunk(k, v, state):
        m, acc = state
        s = lax.dot_general(q, k, dn, preferred_element_type=F32)
        m_new = jnp.maximum(m, jnp.max(s, axis=-1, keepdims=True))
        p = jnp.exp(s - m_new).astype(BF16)
        acc = jnp.exp(m - m_new) * acc + _bdot(p, v)
        return m_new, acc

    state = (jnp.full((tq, 1), NEG_BIG, F32), jnp.zeros((tq, V_PAD), F32))
    for c in range(kx_ref.shape[0] // ATTN_KEY_CHUNK):
        rows = slice(c * ATTN_KEY_CHUNK, (c + 1) * ATTN_KEY_CHUNK)
        state = chunk(kx_ref[rows, :], vx_ref[rows, :], state)
    _, acc = chunk(kc_ref[...], vc_ref[...], state)
    o_ref[...] = acc[:, :V_HEAD_DIM] / acc[:, V_HEAD_DIM:V_HEAD_DIM + 1]


def _attention(q, kx, kc, vx, vc, *, tq):
    b, h, s, _ = q.shape
    lc = kc.shape[2]
    return pl.pallas_call(
        _attn_kernel,
        grid=(b, h, s // tq),
        in_specs=[pl.BlockSpec((None, None, tq, HEAD_PAD), lambda bi, hi, qi: (bi, hi, qi, 0)),
                  pl.BlockSpec((None, None, s, HEAD_PAD), lambda bi, hi, qi: (bi, hi, 0, 0)),
                  pl.BlockSpec((None, None, lc, HEAD_PAD), lambda bi, hi, qi: (bi, hi, 0, 0)),
                  pl.BlockSpec((None, None, s, V_PAD), lambda bi, hi, qi: (bi, hi, 0, 0)),
                  pl.BlockSpec((None, None, lc, V_PAD), lambda bi, hi, qi: (bi, hi, 0, 0))],
        out_specs=pl.BlockSpec((None, tq, V_HEAD_DIM), lambda bi, hi, qi: (bi, qi, hi)),
        out_shape=jax.ShapeDtypeStruct((b, s, h * V_HEAD_DIM), F32),
        compiler_params=_cparams(("arbitrary", "arbitrary", "arbitrary")),
        name="attention",
    )(q, kx, kc, vx, vc)


def _dft_stage1_kernel(u_ref, r_ref, e_ref, o_ref):
    n = 2 * GRID_W * 8
    t = _bdot(r_ref[...].astype(BF16), e_ref[...])
    row = lax.broadcasted_iota(jnp.int32, (n, n), 0)
    col = lax.broadcasted_iota(jnp.int32, (n, n), 1)
    t = jnp.where((row % 8) == (col % 8), t, 0.0).astype(BF16)
    u = u_ref[...].reshape(GRID_W * 8, 2 * FOURIER_DIM)
    rhs = jnp.concatenate([u[:, :FOURIER_DIM], u[:, FOURIER_DIM:]], axis=0).astype(BF16)
    a = _bdot(t, rhs)
    o_ref[...] = a.reshape(2, GRID_W, 8, FOURIER_DIM)


def _dft_stage2_kernel(a_ref, t_ref, o_ref):
    rhs = a_ref[...].reshape(2 * 8 * GRID_W, FOURIER_DIM).astype(BF16)
    y = _bdot(t_ref[...].astype(BF16), rhs)
    o_ref[...] = y.reshape(GRID_W, 8, FOURIER_DIM)


def _seq_dft_tables(n_seq):
    w = GRID_W
    ch = np.arange(8).reshape(8, 1, 1, 1)
    kb = np.arange(w).reshape(1, w, 1, 1)
    j = np.arange(8).reshape(1, 1, 8, 1)
    r = np.arange(w).reshape(1, 1, 1, w)
    ang = (2.0 * np.pi / n_seq) * ((kb * (w * r + 8 * ch + j)) % n_seq)
    c, s = np.cos(ang), np.sin(ang)
    rot = np.stack([np.stack([c, s], axis=3), np.stack([-s, c], axis=3)], axis=1)
    r1 = rot.reshape(8, 2 * w * 8, 2 * w).astype(np.float32)
    expand = (np.arange(2 * w * 8)[None, :] // 8 == np.arange(2 * w)[:, None]).astype(np.float32)
    ka = np.arange(w).reshape(w, 1)
    cp = np.arange(w).reshape(1, w)
    ang2 = (2.0 * np.pi / w) * ((ka * cp) % w)
    norm = 1.0 / math.sqrt(n_seq * FOURIER_GROUP_DIM)
    cs = np.stack([np.cos(ang2), np.sin(ang2)], axis=1) * norm
    eye8 = np.eye(8)
    t2 = (cs[:, None, :, None, :] * eye8[None, :, None, :, None]).reshape(w * 8, 2 * 8 * w).astype(np.float32)
    return jnp.asarray(r1), jnp.asarray(expand).astype(BF16), jnp.asarray(t2)


def _seq_dft(u, batch, n_seq):
    w = GRID_W
    r1, expand, t2 = _seq_dft_tables(n_seq)
    u5 = u.reshape(batch, w, 8, 8, 2 * FOURIER_DIM)
    a = pl.pallas_call(
        _dft_stage1_kernel,
        grid=(batch, 8),
        in_specs=[pl.BlockSpec((None, w, None, 8, 2 * FOURIER_DIM), lambda b, c: (b, 0, c, 0, 0)),
                  pl.BlockSpec((None, 2 * w * 8, 2 * w), lambda b, c: (c, 0, 0)),
                  pl.BlockSpec((2 * w, 2 * w * 8), lambda b, c: (0, 0))],
        out_specs=pl.BlockSpec((None, 2, w, None, 8, FOURIER_DIM), lambda b, c: (b, 0, 0, c, 0, 0)),
        out_shape=jax.ShapeDtypeStruct((batch, 2, w, 8, 8, FOURIER_DIM), F32),
        compiler_params=_cparams(("arbitrary", "arbitrary")),
        name="seq_dft_stage1",
    )(u5, r1, expand)
    y = pl.pallas_call(
        _dft_stage2_kernel,
        grid=(batch, 8),
        in_specs=[pl.BlockSpec((None, 2, 8, 8, 8, FOURIER_DIM), lambda b, k: (b, 0, k, 0, 0, 0)),
                  pl.BlockSpec((w * 8, 2 * 8 * w), lambda b, k: (0, 0))],
        out_specs=pl.BlockSpec((None, w, None, 8, FOURIER_DIM), lambda b, k: (b, 0, k, 0, 0)),
        out_shape=jax.ShapeDtypeStruct((batch, w, 8, 8, FOURIER_DIM), F32),
        compiler_params=_cparams(("arbitrary", "arbitrary")),
        name="seq_dft_stage2",
    )(a, t2)
    return y.reshape(batch * n_seq, FOURIER_DIM)


def _out_router_kernel(x_ref, four_ref, attn_ref, gt1_ref, sh2_ref, sc2_ref, gf_ref, ga_ref, g2_ref,
                       wof_ref, woa_ref, wrh_ref, wrl_ref, br_ref,
                       xnew_ref, hm_ref, ri_ref, rw_ref, cnt_ref, carry_ref):
    i = pl.program_id(0)
    tm = x_ref.shape[0]

    @pl.when(i == 0)
    def _():
        carry_ref[...] = jnp.zeros_like(carry_ref)

    def norm(v, g):
        return (v * lax.rsqrt(jnp.mean(v * v, axis=-1, keepdims=True) + EPS) * g).astype(BF16)

    mix = _bdot(norm(four_ref[...], gf_ref[...]), wof_ref[...]) + _bdot(norm(attn_ref[...], ga_ref[...]), woa_ref[...])
    xn = x_ref[...] + gt1_ref[...] * mix
    xnew_ref[...] = xn
    ms = jnp.mean(xn * xn, axis=-1, keepdims=True)
    hm = xn * lax.rsqrt(ms + EPS) * (g2_ref[...] * (1.0 + sc2_ref[...])) + sh2_ref[...]
    hm_ref[...] = _pack_halves(hm)

    hm_hi = hm.astype(BF16)
    hm_lo = (hm - hm_hi.astype(F32)).astype(BF16)
    logits = _bdot(hm_hi, wrh_ref[...]) + _bdot(hm_lo, wrh_ref[...]) + _bdot(hm_hi, wrl_ref[...]) + br_ref[...]
    lane = lax.broadcasted_iota(jnp.int32, (tm, LANES), 1)
    lanef = lane.astype(F32)
    far = 1e9

    lg = jnp.where(lane < N_GROUPS, logits, NEG_BIG)
    m1 = jnp.max(lg, axis=-1, keepdims=True)
    g_p = 1.0 / jnp.sum(jnp.exp(lg - m1), axis=-1, keepdims=True)
    gidx = jnp.min(jnp.where(lg >= m1, lanef, far), axis=-1, keepdims=True)
    lo = N_GROUPS + EXPERTS_PER_GROUP * gidx
    in_group = jnp.where(lanef >= lo, jnp.where(lanef < lo + EXPERTS_PER_GROUP, 1.0, 0.0), 0.0) > 0.5
    le = jnp.where(in_group, logits, NEG_BIG)
    m2 = jnp.max(le, axis=-1, keepdims=True)
    idx1 = jnp.min(jnp.where(le >= m2, lanef, far), axis=-1, keepdims=True)
    le2 = jnp.where(lanef == idx1, NEG_BIG, le)
    m3 = jnp.max(le2, axis=-1, keepdims=True)
    idx2 = jnp.min(jnp.where(le2 >= m3, lanef, far), axis=-1, keepdims=True)
    t = jnp.exp(m3 - m2)
    p1 = 1.0 / (1.0 + t)
    p2 = t / (1.0 + t)
    e1 = idx1 - N_GROUPS
    e2 = idx2 - N_GROUPS

    oh1 = jnp.where(lanef == e1, 1.0, 0.0)
    oh2 = jnp.where(lanef == e2, 1.0, 0.0)
    ohs = oh1 + oh2
    row = lax.broadcasted_iota(jnp.int32, (tm, tm), 0)
    col = lax.broadcasted_iota(jnp.int32, (tm, tm), 1)
    tri = jnp.where(row > col, 1.0, 0.0).astype(BF16)
    before = _bdot(tri, ohs.astype(BF16)) + carry_ref[...]
    rank1 = jnp.sum(oh1 * before, axis=-1, keepdims=True)
    rank2 = jnp.sum(oh2 * before, axis=-1, keepdims=True)
    carry = carry_ref[...] + jnp.sum(ohs, axis=0, keepdims=True)
    carry_ref[...] = carry
    cnt_ref[...] = jnp.broadcast_to(carry, cnt_ref.shape)

    ri = jnp.where(lane == 0, e1, jnp.where(lane == 1, e2, jnp.where(lane == 2, rank1, jnp.where(lane == 3, rank2, 0.0))))
    ri_ref[...] = ri.astype(jnp.int32)
    rw_ref[...] = jnp.where(lane == 0, g_p * p1, jnp.where(lane == 1, g_p * p2, 0.0))


def _out_router(x2, four, attn, gt1, sh2, sc2, gf, ga, g2, wof, woa, wrh, wrl, br, *, seq, tm):
    t, d = x2.shape
    nt = t // tm
    tiles_per_b = seq // tm

    def rows(n):
        return pl.BlockSpec((tm, n), lambda i: (i, 0))

    def const(arr):
        return pl.BlockSpec(arr.shape, lambda i: (0,) * arr.ndim)

    mod_spec = pl.BlockSpec((None, 1, d), lambda i: (i // tiles_per_b, 0, 0))
    return pl.pallas_call(
        _out_router_kernel,
        grid=(nt,),
        in_specs=[rows(d), rows(FOURIER_DIM), rows(MLA_DIM), mod_spec, mod_spec, mod_spec,
                  const(gf), const(ga), const(g2), const(wof), const(woa), const(wrh), const(wrl), const(br)],
        out_specs=[rows(d), rows(d // 2), rows(LANES), rows(LANES), pl.BlockSpec((8, LANES), lambda i: (0, 0))],
        out_shape=[jax.ShapeDtypeStruct((t, d), F32), jax.ShapeDtypeStruct((t, d // 2), jnp.uint32),
                   jax.ShapeDtypeStruct((t, LANES), jnp.int32), jax.ShapeDtypeStruct((t, LANES), F32),
                   jax.ShapeDtypeStruct((8, LANES), F32)],
        scratch_shapes=[pltpu.VMEM((1, LANES), F32)],
        compiler_params=_cparams(("arbitrary",)),
        name="out_proj_router",
    )(x2, four, attn, gt1, sh2, sc2, gf, ga, g2, wof, woa, wrh, wrl, br)


def _moe_kernel(item_e, item_blk0, item_nblk, row_tok, used_blocks,
                hm_hbm, wg_hbm, wu_hbm, wd_hbm, y_hbm,
                xg, xb, acc, yp, wg_buf, wu_buf, wd_buf, gsem, osem, wsem):
    i = pl.program_id(0)
    j = pl.program_id(1)
    n_items = pl.num_programs(0)
    nj = pl.num_programs(1)
    tj = wg_buf.shape[2]
    step = i * nj + j
    n_steps = n_items * nj
    slot = i % 2
    wslot = step % WEIGHT_SLOTS
    nblk = item_nblk[i]

    def weight_copies(st, ws):
        e = item_e[st // nj]
        c0 = pl.multiple_of((st % nj) * tj, tj)
        return (pltpu.make_async_copy(wg_hbm.at[e, :, pl.ds(c0, tj)], wg_buf.at[ws], wsem.at[ws]),
                pltpu.make_async_copy(wu_hbm.at[e, :, pl.ds(c0, tj)], wu_buf.at[ws], wsem.at[ws]),
                pltpu.make_async_copy(wd_hbm.at[e, pl.ds(c0, tj), :], wd_buf.at[ws], wsem.at[ws]))

    def start_weights(st):
        st_c = jnp.minimum(st, n_steps - 1)

        @pl.when(jnp.logical_and(st < n_steps, item_nblk[st_c // nj] > 0))
        def _():
            for cp in weight_copies(st_c, st_c % WEIGHT_SLOTS):
                cp.start()

    def gather_copy(tok, sl, r):
        return pltpu.make_async_copy(hm_hbm.at[pl.ds(tok, 1)], xg.at[sl, pl.ds(r, 1)], gsem.at[sl])

    def issue_gather(it, sl):
        r0 = item_blk0[it] * MOE_BLOCK

        def body(r8, carry):
            for k in range(GATHER_UNROLL):
                r = r8 * GATHER_UNROLL + k
                gather_copy(row_tok[r0 + r], sl, r).start(priority=ROW_DMA_PRIORITY)
            return carry

        lax.fori_loop(0, item_nblk[it] * (MOE_BLOCK // GATHER_UNROLL), body, 0)

    def wait_gather(it, sl):
        for b in range(ITEM_BLOCKS):
            @pl.when(b < item_nblk[it])
            def _():
                pltpu.make_async_copy(hm_hbm.at[pl.ds(0, MOE_BLOCK)], xg.at[sl, pl.ds(b * MOE_BLOCK, MOE_BLOCK)],
                                      gsem.at[sl]).wait()

    def out_copy(it, m):
        r0 = pl.multiple_of(item_blk0[it] * MOE_BLOCK, MOE_BLOCK)
        return pltpu.make_async_copy(yp.at[pl.ds(0, m)], y_hbm.at[pl.ds(r0, m)], osem.at[0])

    def wait_out(it):
        for nb in range(1, ITEM_BLOCKS + 1):
            @pl.when(item_nblk[it] == nb)
            def _():
                out_copy(it, nb * MOE_BLOCK).wait()

    @pl.when(step == 0)
    def _():
        for st in range(WEIGHT_SLOTS - 1):
            start_weights(st)
        issue_gather(0, 0)

    start_weights(step + WEIGHT_SLOTS - 1)

    @pl.when(j == 0)
    def _():
        wait_gather(i, slot)

        @pl.when(i + 1 < n_items)
        def _():
            issue_gather(i + 1, 1 - slot)

    @pl.when(nblk > 0)
    def _():
        for cp in weight_copies(step, wslot):
            cp.wait()

    @pl.when(jnp.logical_and(j == nj - 1, i > 0))
    def _():
        wait_out(i - 1)

    for nb in range(1, ITEM_BLOCKS + 1):
        m = nb * MOE_BLOCK

        @pl.when(nblk == nb)
        def _():
            @pl.when(j == 0)
            def _():
                hi, lo = _unpack_halves(xg[slot, 0:m, :])
                half = hi.shape[1]
                xb[0:m, 0:half] = hi.astype(BF16)
                xb[0:m, half:2 * half] = lo.astype(BF16)

            x = xb[0:m, :]
            g = _bdot(x, wg_buf[wslot].astype(BF16))
            u = _bdot(x, wu_buf[wslot].astype(BF16))
            a = (g * jax.nn.sigmoid(g) * u).astype(BF16)
            y = _bdot(a, wd_buf[wslot].astype(BF16))

            @pl.when(j == 0)
            def _():
                acc[0:m, :] = y

            @pl.when(jnp.logical_and(j > 0, j < nj - 1))
            def _():
                acc[0:m, :] += y

            @pl.when(j == nj - 1)
            def _():
                yp[0:m, :] = _pack_halves(acc[0:m, :] + y)
                out_copy(i, m).start(priority=ROW_DMA_PRIORITY)

    @pl.when(step == n_steps - 1)
    def _():
        wait_out(i)
        n_blocks = y_hbm.shape[0] // MOE_BLOCK
        yp[0:MOE_BLOCK, :] = jnp.zeros((MOE_BLOCK, yp.shape[1]), jnp.uint32)

        def tail_copy(blk):
            r0 = pl.multiple_of(blk * MOE_BLOCK, MOE_BLOCK)
            return pltpu.make_async_copy(yp.at[pl.ds(0, MOE_BLOCK)], y_hbm.at[pl.ds(r0, MOE_BLOCK)], osem.at[0])

        def start_body(blk, carry):
            tail_copy(blk).start()
            return carry

        def wait_body(blk, carry):
            tail_copy(blk).wait()
            return carry

        lax.fori_loop(used_blocks[0], n_blocks, start_body, 0)
        lax.fori_loop(used_blocks[0], n_blocks, wait_body, 0)


def _moe(hm, w_gate, w_up, w_down, item_e, item_blk0, item_nblk, row_tok, used_blocks, *, n_rows, tj):
    d = w_gate.shape[1]
    n_items = item_e.shape[0]
    nj = D_EXPERT // tj
    rows = ITEM_BLOCKS * MOE_BLOCK
    any_spec = pl.BlockSpec(memory_space=pl.ANY)
    grid_spec = pltpu.PrefetchScalarGridSpec(
        num_scalar_prefetch=5,
        grid=(n_items, nj),
        in_specs=[any_spec, any_spec, any_spec, any_spec],
        out_specs=any_spec,
        scratch_shapes=[pltpu.VMEM((2, rows, d // 2), jnp.uint32),
                        pltpu.VMEM((rows, d), BF16),
                        pltpu.VMEM((rows, d), F32),
                        pltpu.VMEM((rows, d // 2), jnp.uint32),
                        pltpu.VMEM((WEIGHT_SLOTS, d, tj), F32),
                        pltpu.VMEM((WEIGHT_SLOTS, d, tj), F32),
                        pltpu.VMEM((WEIGHT_SLOTS, tj, d), F32),
                        pltpu.SemaphoreType.DMA((2,)),
                        pltpu.SemaphoreType.DMA((1,)),
                        pltpu.SemaphoreType.DMA((WEIGHT_SLOTS,))],
    )
    return pl.pallas_call(
        _moe_kernel,
        grid_spec=grid_spec,
        out_shape=jax.ShapeDtypeStruct((n_rows, d // 2), jnp.uint32),
        compiler_params=_cparams(("arbitrary", "arbitrary")),
        name="moe_experts",
    )(item_e, item_blk0, item_nblk, row_tok, used_blocks, hm, w_gate, w_up, w_down)


def _combine_kernel(dest, x_ref, gt2_ref, rw_ref, y_hbm, o_ref, ybuf, sem):
    i = pl.program_id(0)
    n = pl.num_programs(0)
    tm = x_ref.shape[0]
    slot = i % 2

    def issue(it, sl):
        base = it * tm

        def body(r4, carry):
            for rr in range(GATHER_UNROLL // 2):
                r = r4 * (GATHER_UNROLL // 2) + rr
                for k in range(2):
                    pltpu.make_async_copy(y_hbm.at[pl.ds(dest[2 * (base + r) + k], 1)], ybuf.at[sl, k, pl.ds(r, 1)],
                                          sem.at[sl]).start(priority=ROW_DMA_PRIORITY)
            return carry

        lax.fori_loop(0, tm // (GATHER_UNROLL // 2), body, 0)

    @pl.when(i == 0)
    def _():
        issue(0, 0)

    for k in range(2):
        pltpu.make_async_copy(y_hbm.at[pl.ds(0, tm)], ybuf.at[slot, k], sem.at[slot]).wait()

    @pl.when(i + 1 < n)
    def _():
        issue(i + 1, 1 - slot)

    w = rw_ref[...]
    hi0, lo0 = _unpack_halves(ybuf[slot, 0])
    hi1, lo1 = _unpack_halves(ybuf[slot, 1])
    half = hi0.shape[1]
    o_ref[:, 0:half] = x_ref[:, 0:half] + gt2_ref[:, 0:half] * (w[:, 0:1] * hi0 + w[:, 1:2] * hi1)
    o_ref[:, half:2 * half] = (x_ref[:, half:2 * half]
                               + gt2_ref[:, half:2 * half] * (w[:, 0:1] * lo0 + w[:, 1:2] * lo1))


def _combine(dest, xnew, gt2, rw, y, *, seq, tm):
    t, d = xnew.shape
    tiles_per_b = seq // tm
    grid_spec = pltpu.PrefetchScalarGridSpec(
        num_scalar_prefetch=1,
        grid=(t // tm,),
        in_specs=[pl.BlockSpec((tm, d), lambda i, ds: (i, 0)),
                  pl.BlockSpec((None, 1, d), lambda i, ds: (i // tiles_per_b, 0, 0)),
                  pl.BlockSpec((tm, LANES), lambda i, ds: (i, 0)),
                  pl.BlockSpec(memory_space=pl.ANY)],
        out_specs=pl.BlockSpec((tm, d), lambda i, ds: (i, 0)),
        scratch_shapes=[pltpu.VMEM((2, 2, tm, d // 2), jnp.uint32), pltpu.SemaphoreType.DMA((2,))],
    )
    return pl.pallas_call(
        _combine_kernel,
        grid_spec=grid_spec,
        out_shape=jax.ShapeDtypeStruct((t, d), F32),
        compiler_params=_cparams(("arbitrary",)),
        name="moe_combine",
    )(dest, xnew, gt2, rw, y)


def _rope_tables(n_tokens):
    rows = n_tokens // GRID_W
    row = jnp.repeat(jnp.arange(rows, dtype=jnp.int32), GRID_W).astype(F32)
    col = jnp.tile(jnp.arange(GRID_W, dtype=jnp.int32), rows).astype(F32)
    n_freq = QK_ROPE_DIM // 4
    inv = ROPE_THETA ** (-jnp.arange(n_freq, dtype=F32) / n_freq)
    ar = row[:, None] * inv[None, :]
    ac = col[:, None] * inv[None, :]
    cr, sr, cc, sc = jnp.cos(ar), jnp.sin(ar), jnp.cos(ac), jnp.sin(ac)
    cos64 = jnp.concatenate([cr, cr, cc, cc], axis=-1)
    sin64 = jnp.concatenate([-sr, sr, -sc, sc], axis=-1)
    return jnp.tile(cos64, (1, 2)), jnp.tile(sin64, (1, 2))


def _channel_dft_table():
    c = np.arange(FOURIER_GROUP_DIM).reshape(-1, 1)
    k = np.arange(FOURIER_GROUP_DIM).reshape(1, -1)
    ang = (2.0 * np.pi / FOURIER_GROUP_DIM) * ((c * k) % FOURIER_GROUP_DIM)
    return jnp.asarray(np.concatenate([np.cos(ang), -np.sin(ang)], axis=1).astype(np.float32)).astype(BF16)


def _split_heads(w, widths):
    k = w.shape[0]
    wh = w.reshape(k, MLA_HEADS, sum(widths))
    outs, off = [], 0
    for wd in widths:
        outs.append(wh[:, :, off:off + wd].reshape(k, MLA_HEADS * wd))
        off += wd
    return outs


def kernel(x, c, ctx, c_ctx, w_ada, b_ada, g_norm1, g_norm2, w_in, g_q_a, g_kv_a, w_uq, w_ukv, g_qk_q, g_qk_k,
           g_out_four, g_out_attn, w_out, w_router_group, b_router_group, w_router_expert, b_router_expert,
           w_gate, w_up, w_down):
    b, s, d = x.shape
    lc = ctx.shape[1]
    t = b * s
    layer_params = (w_ada, b_ada, g_norm1, g_norm2, w_in, g_q_a, g_kv_a, w_uq, w_ukv, g_qk_q, g_qk_k, g_out_four,
                    g_out_attn, w_out, w_router_group, b_router_group, w_router_expert, b_router_expert,
                    w_gate, w_up, w_down)
    assert all(p.shape[0] == 1 for p in layer_params), "single-layer block"
    (w_ada, b_ada, g_norm1, g_norm2, w_in, g_q_a, g_kv_a, w_uq, w_ukv, g_qk_q, g_qk_k, g_out_four,
     g_out_attn, w_out, w_router_group, b_router_group, w_router_expert, b_router_expert,
     w_gate, w_up, w_down) = [p.reshape(p.shape[1:]) for p in layer_params]

    cond8 = jnp.concatenate([c, c_ctx[None, :], jnp.zeros((8 - b - 1, d), F32)], axis=0)
    mods = _ada_mod(cond8, w_ada, b_ada)
    sh1, sc1, gt1, sh2, sc2, gt2 = [m[:b].reshape(b, 1, d) for m in jnp.split(mods, N_MOD, axis=-1)]
    csh1, csc1 = [m[b:b + 1].reshape(1, 1, d) for m in jnp.split(mods, N_MOD, axis=-1)[:2]]

    wf = w_in[:, :Q_OFF].astype(BF16)
    wq = w_in[:, Q_OFF:KV_OFF].astype(BF16)
    wkv = w_in[:, KV_OFF:ROPE_OFF].astype(BF16)
    wr = w_in[:, ROPE_OFF:].astype(BF16)
    wr2 = jnp.concatenate([wr, wr], axis=1)
    dc = _channel_dft_table()
    wqn, wqr = [w.astype(BF16) for w in _split_heads(w_uq, (QK_NOPE_DIM, QK_ROPE_DIM))]
    wkn, wv = [w.astype(BF16) for w in _split_heads(w_ukv, (QK_NOPE_DIM, V_HEAD_DIM))]
    gqn = g_qk_q[:QK_NOPE_DIM].reshape(1, -1)
    gqr2 = jnp.tile(g_qk_q[QK_NOPE_DIM:], 2).reshape(1, -1)
    gkn = g_qk_k[:QK_NOPE_DIM].reshape(1, -1)
    gkr2 = jnp.tile(g_qk_k[QK_NOPE_DIM:], 2).reshape(1, -1)
    g1 = g_norm1.reshape(1, d)
    gq = g_q_a.reshape(1, -1)
    gkv = g_kv_a.reshape(1, -1)

    x2 = x.reshape(t, d)
    u, cq, ckv, kr2 = _in_proj(x2, sh1, sc1, s, g1, gq, gkv, wf, wq, wkv, wr2, dc, with_q=True, tm=512)
    ckv_c, kr2_c = _in_proj(ctx.reshape(b * lc, d), csh1, csc1, b * lc, g1, gq, gkv, wf, wq, wkv, wr2, dc,
                            with_q=False, tm=lc)

    cos_t, sin_t = _rope_tables(s)
    q, kx, vx = _qkv(cq, ckv, kr2, cos_t, sin_t, wqn, wqr, gqn, gqr2, wkn, wv, gkn, gkr2,
                     batch=b, seq=s, tm=512, with_q=True, with_rope=True)
    kc, vc = _qkv(None, ckv_c, kr2_c, None, None, None, None, None, None, wkn, wv, gkn, gkr2,
                  batch=b, seq=lc, tm=lc, with_q=False, with_rope=False)

    attn = _attention(q, kx, kc, vx, vc, tq=1024).reshape(t, MLA_DIM)
    four = _seq_dft(u, b, s)

    wof = w_out[:FOURIER_DIM].astype(BF16)
    woa = w_out[FOURIER_DIM:].astype(BF16)
    n_route = N_GROUPS + N_EXPERTS
    wrt = jnp.concatenate([w_router_group, w_router_expert, jnp.zeros((d, LANES - n_route), F32)], axis=1)
    wrt_hi = wrt.astype(BF16)
    wrt_lo = (wrt - wrt_hi.astype(F32)).astype(BF16)
    brt = jnp.concatenate([b_router_group, b_router_expert, jnp.zeros((LANES - n_route,), F32)]).reshape(1, -1)
    xnew, hm, ri, rw, cnt = _out_router(x2, four, attn, gt1, sh2, sc2, g_out_four.reshape(1, -1),
                                        g_out_attn.reshape(1, -1), g_norm2.reshape(1, d), wof, woa, wrt_hi, wrt_lo, brt,
                                        seq=s, tm=256)

    counts = cnt[0, :N_EXPERTS].astype(jnp.int32)
    nblk_e = (counts + MOE_BLOCK - 1) // MOE_BLOCK
    blk_end = jnp.cumsum(nblk_e)
    blk_start = blk_end - nblk_e
    n_slots = t * 2
    n_blocks = -(-(n_slots + N_EXPERTS * (MOE_BLOCK - 1)) // MOE_BLOCK)
    n_rows = n_blocks * MOE_BLOCK
    e12 = ri[:, 0:2]
    seg_start = jnp.sum(jnp.where(e12[:, :, None] == jnp.arange(N_EXPERTS, dtype=jnp.int32), blk_start, 0), axis=-1)
    dest = (seg_start * MOE_BLOCK + ri[:, 2:4]).reshape(-1)
    tok_ids = jnp.repeat(jnp.arange(t, dtype=jnp.int32), 2)
    row_tok = jnp.zeros((n_rows,), jnp.int32).at[dest].set(tok_ids)
    items_e = (nblk_e + ITEM_BLOCKS - 1) // ITEM_BLOCKS
    item_end = jnp.cumsum(items_e)
    n_items = (n_blocks + (ITEM_BLOCKS - 1) * N_EXPERTS) // ITEM_BLOCKS
    idx = jnp.arange(n_items, dtype=jnp.int32)
    total = item_end[-1]
    idx_c = jnp.minimum(idx, total - 1)
    ie = jnp.minimum(jnp.sum(idx_c[:, None] >= item_end[None, :], axis=1), N_EXPERTS - 1).astype(jnp.int32)
    local = idx_c - (item_end - items_e)[ie]
    item_blk0 = (blk_start[ie] + ITEM_BLOCKS * local).astype(jnp.int32)
    item_nblk = jnp.where(idx < total, jnp.clip(nblk_e[ie] - ITEM_BLOCKS * local, 0, ITEM_BLOCKS), 0).astype(jnp.int32)

    used_blocks = blk_end[-1:].astype(jnp.int32)
    y = _moe(hm, w_gate, w_up, w_down, ie, item_blk0, item_nblk, row_tok, used_blocks, n_rows=n_rows, tj=256)
    out = _combine(dest.astype(jnp.int32), xnew, gt2, rw, y, seq=s, tm=128)
    return out.reshape(b, s, d)
```

```python
import functools
import math

import numpy as np
import jax
import jax.numpy as jnp
from jax import lax
from jax.experimental import pallas as pl
from jax.experimental.pallas import tpu as pltpu

F32 = jnp.float32
BF16 = jnp.bfloat16

D_MODEL = 2048
GRID_W = 64
EPS = 1e-6
N_MOD = 6
N_FOURIER_GROUPS = 4
FOURIER_GROUP_DIM = 256
FOURIER_DIM = 1024
MLA_HEADS = 8
QK_NOPE_DIM = 128
QK_ROPE_DIM = 64
QK_HEAD_DIM = 192
V_HEAD_DIM = 128
Q_LORA_RANK = 768
KV_LORA_RANK = 512
MLA_DIM = 1024
ROPE_THETA = 10000.0
Q_OFF = FOURIER_DIM
KV_OFF = Q_OFF + Q_LORA_RANK
ROPE_OFF = KV_OFF + KV_LORA_RANK
N_GROUPS = 8
EXPERTS_PER_GROUP = 8
N_EXPERTS = 64
D_EXPERT = 768

LANES = 128
HEAD_PAD = 256
V_PAD = 256
MOE_BLOCK = 128
ITEM_BLOCKS = 4
GATHER_UNROLL = 8
ATTN_KEY_CHUNK = 512
ROW_DMA_PRIORITY = 0
WEIGHT_DMA_PRIORITY = 1
VMEM_LIMIT = 56 * 1024 * 1024
NEG_BIG = -1e30


def _cparams(sem):
    return pltpu.CompilerParams(dimension_semantics=sem, vmem_limit_bytes=VMEM_LIMIT)


def _bdot(a, b):
    return jnp.dot(a, b, preferred_element_type=F32)


def _pack_halves(v):
    n = v.shape[1] // 2
    hi = pltpu.bitcast(v[:, :n].astype(BF16).astype(F32), jnp.uint32)
    lo = pltpu.bitcast(v[:, n:].astype(BF16).astype(F32), jnp.uint32)
    return hi | (lo >> 16)


def _unpack_halves(u):
    hi = pltpu.bitcast(u & jnp.uint32(0xFFFF0000), F32)
    lo = pltpu.bitcast(u << 16, F32)
    return hi, lo


def _ada_kernel(c_ref, w_ref, b_ref, o_ref):
    c = c_ref[...]
    s = (c * jax.nn.sigmoid(c)).astype(BF16)
    o_ref[...] = _bdot(s, w_ref[...].astype(BF16)) + b_ref[...]


def _ada_mod(cond8, w_ada, b_ada):
    d, n = w_ada.shape
    tn = 1024
    return pl.pallas_call(
        _ada_kernel,
        grid=(n // tn,),
        in_specs=[pl.BlockSpec((8, d), lambda i: (0, 0)),
                  pl.BlockSpec((d, tn), lambda i: (0, i)),
                  pl.BlockSpec((1, tn), lambda i: (0, i))],
        out_specs=pl.BlockSpec((8, tn), lambda i: (0, i)),
        out_shape=jax.ShapeDtypeStruct((8, n), F32),
        compiler_params=_cparams(("arbitrary",)),
        name="ada_mod",
    )(cond8, w_ada, b_ada.reshape(1, n))


def _in_proj_kernel(x_ref, sh_ref, sc_ref, g_ref, gq_ref, gkv_ref, wf_ref, wq_ref, wkv_ref, wr_ref, dc_ref,
                    *out_refs, with_q):
    x = x_ref[...]
    ms = jnp.mean(x * x, axis=-1, keepdims=True)
    a = g_ref[...] * (1.0 + sc_ref[...])
    h = (x * lax.rsqrt(ms + EPS) * a + sh_ref[...]).astype(BF16)
    if with_q:
        u_ref, cq_ref, ckv_ref, kr_ref = out_refs
        f = _bdot(h, wf_ref[...]).astype(BF16)
        dc = dc_ref[...]
        for g in range(N_FOURIER_GROUPS):
            lo = g * FOURIER_GROUP_DIM
            ug = _bdot(f[:, lo:lo + FOURIER_GROUP_DIM], dc)
            u_ref[:, lo:lo + FOURIER_GROUP_DIM] = ug[:, :FOURIER_GROUP_DIM]
            u_ref[:, FOURIER_DIM + lo:FOURIER_DIM + lo + FOURIER_GROUP_DIM] = ug[:, FOURIER_GROUP_DIM:]
        pq = _bdot(h, wq_ref[...])
        msq = jnp.mean(pq * pq, axis=-1, keepdims=True)
        cq_ref[...] = (pq * lax.rsqrt(msq + EPS) * gq_ref[...]).astype(BF16)
    else:
        ckv_ref, kr_ref = out_refs
    pkv = _bdot(h, wkv_ref[...])
    mskv = jnp.mean(pkv * pkv, axis=-1, keepdims=True)
    ckv_ref[...] = (pkv * lax.rsqrt(mskv + EPS) * gkv_ref[...]).astype(BF16)
    kr_ref[...] = _bdot(h, wr_ref[...])


def _in_proj(x2, shift, scale, rows_per_mod, g1, gq, gkv, wf, wq, wkv, wr2, dc, *, with_q, tm):
    t, d = x2.shape
    nt = t // tm
    tiles_per_mod = rows_per_mod // tm

    def const(shape):
        return pl.BlockSpec(shape, lambda i: (0,) * len(shape))

    mod_spec = pl.BlockSpec((None, 1, d), lambda i: (i // tiles_per_mod, 0, 0))
    in_specs = [pl.BlockSpec((tm, d), lambda i: (i, 0)), mod_spec, mod_spec,
                const((1, d)), const((1, Q_LORA_RANK)), const((1, KV_LORA_RANK)),
                const(wf.shape), const(wq.shape), const(wkv.shape), const(wr2.shape), const(dc.shape)]

    def rows(n):
        return pl.BlockSpec((tm, n), lambda i: (i, 0))

    out_specs = [rows(KV_LORA_RANK), rows(LANES)]
    out_shape = [jax.ShapeDtypeStruct((t, KV_LORA_RANK), BF16), jax.ShapeDtypeStruct((t, LANES), F32)]
    if with_q:
        out_specs = [rows(2 * FOURIER_DIM), rows(Q_LORA_RANK)] + out_specs
        out_shape = [jax.ShapeDtypeStruct((t, 2 * FOURIER_DIM), F32),
                     jax.ShapeDtypeStruct((t, Q_LORA_RANK), BF16)] + out_shape
    return pl.pallas_call(
        functools.partial(_in_proj_kernel, with_q=with_q),
        grid=(nt,),
        in_specs=in_specs,
        out_specs=out_specs,
        out_shape=out_shape,
        compiler_params=_cparams(("arbitrary",)),
        name="in_proj_x" if with_q else "in_proj_ctx",
    )(x2, shift, scale, g1, gq, gkv, wf, wq, wkv, wr2, dc)


def _swap_halves(y, first_half):
    return jnp.where(first_half, pltpu.roll(y, LANES - 16, 1), pltpu.roll(y, 16, 1))


def _qkv_kernel(*refs, with_q, with_rope):
    it = iter(refs)
    if with_q:
        cq_ref = next(it)
    ckv_ref = next(it)
    kr_ref = next(it)
    if with_rope:
        cos_ref = next(it)
        sin_ref = next(it)
    if with_q:
        wqn_ref = next(it)
        wqr_ref = next(it)
        gqn_ref = next(it)
        gqr_ref = next(it)
    wkn_ref = next(it)
    wv_ref = next(it)
    gkn_ref = next(it)
    gkr_ref = next(it)
    if with_q:
        q_ref = next(it)
    k_ref = next(it)
    v_ref = next(it)

    tm = ckv_ref.shape[0]
    lane = lax.broadcasted_iota(jnp.int32, (tm, LANES), 1)
    low = lane < QK_ROPE_DIM
    first_half = (lane % 32) < 16
    inv_dim = 1.0 / QK_HEAD_DIM

    def rope(y):
        if not with_rope:
            return y
        return y * cos_ref[...] + _swap_halves(y, first_half) * sin_ref[...]

    if with_q:
        cq = cq_ref[...]
        qn = _bdot(cq, wqn_ref[...])
        qr = _bdot(cq, wqr_ref[...])
        qscale = QK_HEAD_DIM ** -0.5
        for p in range(MLA_HEADS // 2):
            blk = qr[:, p * LANES:(p + 1) * LANES]
            sq = blk * blk
            ss_lo = jnp.sum(jnp.where(low, sq, 0.0), axis=-1, keepdims=True)
            ss_hi = jnp.sum(jnp.where(low, 0.0, sq), axis=-1, keepdims=True)
            scales = []
            for hh, ssr in ((2 * p, ss_lo), (2 * p + 1, ss_hi)):
                nh = qn[:, hh * LANES:(hh + 1) * LANES]
                ssq = jnp.sum(nh * nh, axis=-1, keepdims=True) + ssr
                s = lax.rsqrt(ssq * inv_dim + EPS)
                scales.append(s)
                q_ref[hh, :, 0:LANES] = (nh * s * gqn_ref[...] * qscale).astype(BF16)
            s_pair = jnp.where(low, scales[0], scales[1])
            r = rope(blk * s_pair * gqr_ref[...]) * qscale
            q_ref[2 * p, :, LANES:2 * LANES] = jnp.where(low, r, 0.0).astype(BF16)
            q_ref[2 * p + 1, :, LANES:2 * LANES] = jnp.where(low, pltpu.roll(r, QK_ROPE_DIM, 1), 0.0).astype(BF16)

    ckv = ckv_ref[...]
    kn = _bdot(ckv, wkn_ref[...])
    v = _bdot(ckv, wv_ref[...])
    kr = kr_ref[...]
    ss_r = jnp.sum(jnp.where(low, kr * kr, 0.0), axis=-1, keepdims=True)
    base = rope(kr * gkr_ref[...])
    ones_col = jnp.where(lane == 0, 1.0, 0.0).astype(BF16)
    for hh in range(MLA_HEADS):
        nh = kn[:, hh * LANES:(hh + 1) * LANES]
        ssq = jnp.sum(nh * nh, axis=-1, keepdims=True) + ss_r
        s = lax.rsqrt(ssq * inv_dim + EPS)
        k_ref[hh, :, 0:LANES] = (nh * s * gkn_ref[...]).astype(BF16)
        k_ref[hh, :, LANES:2 * LANES] = jnp.where(low, base * s, 0.0).astype(BF16)
        v_ref[hh, :, 0:LANES] = v[:, hh * LANES:(hh + 1) * LANES].astype(BF16)
        v_ref[hh, :, LANES:2 * LANES] = ones_col


def _qkv(cq, ckv, kr2, cos_t, sin_t, wqn, wqr, gqn, gqr2, wkn, wv, gkn, gkr2, *, batch, seq, tm, with_q,
         with_rope):
    t = ckv.shape[0]
    nt = t // tm
    tiles_per_b = seq // tm

    def rows(n):
        return pl.BlockSpec((tm, n), lambda i: (i, 0))

    def const(arr):
        return pl.BlockSpec(arr.shape, lambda i: (0,) * arr.ndim)

    tab_spec = pl.BlockSpec((tm, LANES), lambda i: (i % tiles_per_b, 0))

    def head_out(width):
        return pl.BlockSpec((None, MLA_HEADS, tm, width), lambda i: (i // tiles_per_b, 0, i % tiles_per_b, 0))

    args, in_specs = [], []
    if with_q:
        args.append(cq)
        in_specs.append(rows(Q_LORA_RANK))
    args += [ckv, kr2]
    in_specs += [rows(KV_LORA_RANK), rows(LANES)]
    if with_rope:
        args += [cos_t, sin_t]
        in_specs += [tab_spec, tab_spec]
    if with_q:
        args += [wqn, wqr, gqn, gqr2]
        in_specs += [const(wqn), const(wqr), const(gqn), const(gqr2)]
    args += [wkn, wv, gkn, gkr2]
    in_specs += [const(wkn), const(wv), const(gkn), const(gkr2)]

    out_specs = [head_out(HEAD_PAD), head_out(V_PAD)]
    out_shape = [jax.ShapeDtypeStruct((batch, MLA_HEADS, seq, HEAD_PAD), BF16),
                 jax.ShapeDtypeStruct((batch, MLA_HEADS, seq, V_PAD), BF16)]
    if with_q:
        out_specs = [head_out(HEAD_PAD)] + out_specs
        out_shape = [jax.ShapeDtypeStruct((batch, MLA_HEADS, seq, HEAD_PAD), BF16)] + out_shape
    return pl.pallas_call(
        functools.partial(_qkv_kernel, with_q=with_q, with_rope=with_rope),
        grid=(nt,),
        in_specs=in_specs,
        out_specs=out_specs,
        out_shape=out_shape,
        compiler_params=_cparams(("arbitrary",)),
        name="qkv_x" if with_q else "kv_ctx",
    )(*args)


def _attn_kernel(q_ref, kx_ref, kc_ref, vx_ref, vc_ref, o_ref):
    q = q_ref[...]
    tq = q.shape[0]
    dn = (((1,), (1,)), ((), ()))

    def chunk(k, v, state):
        m, acc = state
        s = lax.dot_general(q, k, dn, preferred_element_type=F32)
        m_new = jnp.maximum(m, jnp.max(s, axis=-1, keepdims=True))
        p = jnp.exp(s - m_new).astype(BF16)
        acc = jnp.exp(m - m_new) * acc + _bdot(p, v)
        return m_new, acc

    state = (jnp.full((tq, 1), NEG_BIG, F32), jnp.zeros((tq, V_PAD), F32))
    for c in range(kx_ref.shape[0] // ATTN_KEY_CHUNK):
        rows = slice(c * ATTN_KEY_CHUNK, (c + 1) * ATTN_KEY_CHUNK)
        state = chunk(kx_ref[rows, :], vx_ref[rows, :], state)
    _, acc = chunk(kc_ref[...], vc_ref[...], state)
    o_ref[...] = acc[:, :V_HEAD_DIM] / acc[:, V_HEAD_DIM:V_HEAD_DIM + 1]


def _attention(q, kx, kc, vx, vc, *, tq):
    b, h, s, _ = q.shape
    lc = kc.shape[2]
    return pl.pallas_call(
        _attn_kernel,
        grid=(b, h, s // tq),
        in_specs=[pl.BlockSpec((None, None, tq, HEAD_PAD), lambda bi, hi, qi: (bi, hi, qi, 0)),
                  pl.BlockSpec((None, None, s, HEAD_PAD), lambda bi, hi, qi: (bi, hi, 0, 0)),
                  pl.BlockSpec((None, None, lc, HEAD_PAD), lambda bi, hi, qi: (bi, hi, 0, 0)),
                  pl.BlockSpec((None, None, s, V_PAD), lambda bi, hi, qi: (bi, hi, 0, 0)),
                  pl.BlockSpec((None, None, lc, V_PAD), lambda bi, hi, qi: (bi, hi, 0, 0))],
        out_specs=pl.BlockSpec((None, tq, V_HEAD_DIM), lambda bi, hi, qi: (bi, qi, hi)),
        out_shape=jax.ShapeDtypeStruct((b, s, h * V_HEAD_DIM), F32),
        compiler_params=_cparams(("arbitrary", "arbitrary", "arbitrary")),
        name="attention",
    )(q, kx, kc, vx, vc)


def _dft_stage1_kernel(u_ref, r_ref, e_ref, o_ref):
    n = 2 * GRID_W * 8
    t = _bdot(r_ref[...].astype(BF16), e_ref[...])
    row = lax.broadcasted_iota(jnp.int32, (n, n), 0)
    col = lax.broadcasted_iota(jnp.int32, (n, n), 1)
    t = jnp.where((row % 8) == (col % 8), t, 0.0).astype(BF16)
    u = u_ref[...].reshape(GRID_W * 8, 2 * FOURIER_DIM)
    rhs = jnp.concatenate([u[:, :FOURIER_DIM], u[:, FOURIER_DIM:]], axis=0).astype(BF16)
    a = _bdot(t, rhs)
    o_ref[...] = a.reshape(2, GRID_W, 8, FOURIER_DIM)


def _dft_stage2_kernel(a_ref, t_ref, o_ref):
    rhs = a_ref[...].reshape(2 * 8 * GRID_W, FOURIER_DIM).astype(BF16)
    y = _bdot(t_ref[...].astype(BF16), rhs)
    o_ref[...] = y.reshape(GRID_W, 8, FOURIER_DIM)


def _seq_dft_tables(n_seq):
    w = GRID_W
    ch = np.arange(8).reshape(8, 1, 1, 1)
    kb = np.arange(w).reshape(1, w, 1, 1)
    j = np.arange(8).reshape(1, 1, 8, 1)
    r = np.arange(w).reshape(1, 1, 1, w)
    ang = (2.0 * np.pi / n_seq) * ((kb * (w * r + 8 * ch + j)) % n_seq)
    c, s = np.cos(ang), np.sin(ang)
    rot = np.stack([np.stack([c, s], axis=3), np.stack([-s, c], axis=3)], axis=1)
    r1 = rot.reshape(8, 2 * w * 8, 2 * w).astype(np.float32)
    expand = (np.arange(2 * w * 8)[None, :] // 8 == np.arange(2 * w)[:, None]).astype(np.float32)
    ka = np.arange(w).reshape(w, 1)
    cp = np.arange(w).reshape(1, w)
    ang2 = (2.0 * np.pi / w) * ((ka * cp) % w)
    norm = 1.0 / math.sqrt(n_seq * FOURIER_GROUP_DIM)
    cs = np.stack([np.cos(ang2), np.sin(ang2)], axis=1) * norm
    eye8 = np.eye(8)
    t2 = (cs[:, None, :, None, :] * eye8[None, :, None, :, None]).reshape(w * 8, 2 * 8 * w).astype(np.float32)
    return jnp.asarray(r1), jnp.asarray(expand).astype(BF16), jnp.asarray(t2)


def _seq_dft(u, batch, n_seq):
    w = GRID_W
    r1, expand, t2 = _seq_dft_tables(n_seq)
    u5 = u.reshape(batch, w, 8, 8, 2 * FOURIER_DIM)
    a = pl.pallas_call(
        _dft_stage1_kernel,
        grid=(batch, 8),
        in_specs=[pl.BlockSpec((None, w, None, 8, 2 * FOURIER_DIM), lambda b, c: (b, 0, c, 0, 0)),
                  pl.BlockSpec((None, 2 * w * 8, 2 * w), lambda b, c: (c, 0, 0)),
                  pl.BlockSpec((2 * w, 2 * w * 8), lambda b, c: (0, 0))],
        out_specs=pl.BlockSpec((None, 2, w, None, 8, FOURIER_DIM), lambda b, c: (b, 0, 0, c, 0, 0)),
        out_shape=jax.ShapeDtypeStruct((batch, 2, w, 8, 8, FOURIER_DIM), F32),
        compiler_params=_cparams(("arbitrary", "arbitrary")),
        name="seq_dft_stage1",
    )(u5, r1, expand)
    y = pl.pallas_call(
        _dft_stage2_kernel,
        grid=(batch, 8),
        in_specs=[pl.BlockSpec((None, 2, 8, 8, 8, FOURIER_DIM), lambda b, k: (b, 0, k, 0, 0, 0)),
                  pl.BlockSpec((w * 8, 2 * 8 * w), lambda b, k: (0, 0))],
        out_specs=pl.BlockSpec((None, w, None, 8, FOURIER_DIM), lambda b, k: (b, 0, k, 0, 0)),
        out_shape=jax.ShapeDtypeStruct((batch, w, 8, 8, FOURIER_DIM), F32),
        compiler_params=_cparams(("arbitrary", "arbitrary")),
        name="seq_dft_stage2",
    )(a, t2)
    return y.reshape(batch * n_seq, FOURIER_DIM)


def _out_router_kernel(x_ref, four_ref, attn_ref, gt1_ref, sh2_ref, sc2_ref, gf_ref, ga_ref, g2_ref,
                       wof_ref, woa_ref, wrh_ref, wrl_ref, br_ref,
                       xnew_ref, hm_ref, ri_ref, rw_ref, cnt_ref, carry_ref):
    i = pl.program_id(0)
    tm = x_ref.shape[0]

    @pl.when(i == 0)
    def _():
        carry_ref[...] = jnp.zeros_like(carry_ref)

    def norm(v, g):
        return (v * lax.rsqrt(jnp.mean(v * v, axis=-1, keepdims=True) + EPS) * g).astype(BF16)

    mix = _bdot(norm(four_ref[...], gf_ref[...]), wof_ref[...]) + _bdot(norm(attn_ref[...], ga_ref[...]), woa_ref[...])
    xn = x_ref[...] + gt1_ref[...] * mix
    xnew_ref[...] = xn
    ms = jnp.mean(xn * xn, axis=-1, keepdims=True)
    hm = xn * lax.rsqrt(ms + EPS) * (g2_ref[...] * (1.0 + sc2_ref[...])) + sh2_ref[...]
    hm_ref[...] = _pack_halves(hm)

    hm_hi = hm.astype(BF16)
    hm_lo = (hm - hm_hi.astype(F32)).astype(BF16)
    logits = _bdot(hm_hi, wrh_ref[...]) + _bdot(hm_lo, wrh_ref[...]) + _bdot(hm_hi, wrl_ref[...]) + br_ref[...]
    lane = lax.broadcasted_iota(jnp.int32, (tm, LANES), 1)
    lanef = lane.astype(F32)
    far = 1e9

    lg = jnp.where(lane < N_GROUPS, logits, NEG_BIG)
    m1 = jnp.max(lg, axis=-1, keepdims=True)
    g_p = 1.0 / jnp.sum(jnp.exp(lg - m1), axis=-1, keepdims=True)
    gidx = jnp.min(jnp.where(lg >= m1, lanef, far), axis=-1, keepdims=True)
    lo = N_GROUPS + EXPERTS_PER_GROUP * gidx
    in_group = jnp.where(lanef >= lo, jnp.where(lanef < lo + EXPERTS_PER_GROUP, 1.0, 0.0), 0.0) > 0.5
    le = jnp.where(in_group, logits, NEG_BIG)
    m2 = jnp.max(le, axis=-1, keepdims=True)
    idx1 = jnp.min(jnp.where(le >= m2, lanef, far), axis=-1, keepdims=True)
    le2 = jnp.where(lanef == idx1, NEG_BIG, le)
    m3 = jnp.max(le2, axis=-1, keepdims=True)
    idx2 = jnp.min(jnp.where(le2 >= m3, lanef, far), axis=-1, keepdims=True)
    t = jnp.exp(m3 - m2)
    p1 = 1.0 / (1.0 + t)
    p2 = t / (1.0 + t)
    e1 = idx1 - N_GROUPS
    e2 = idx2 - N_GROUPS

    oh1 = jnp.where(lanef == e1, 1.0, 0.0)
    oh2 = jnp.where(lanef == e2, 1.0, 0.0)
    ohs = oh1 + oh2
    row = lax.broadcasted_iota(jnp.int32, (tm, tm), 0)
    col = lax.broadcasted_iota(jnp.int32, (tm, tm), 1)
    tri = jnp.where(row > col, 1.0, 0.0).astype(BF16)
    before = _bdot(tri, ohs.astype(BF16)) + carry_ref[...]
    rank1 = jnp.sum(oh1 * before, axis=-1, keepdims=True)
    rank2 = jnp.sum(oh2 * before, axis=-1, keepdims=True)
    carry = carry_ref[...] + jnp.sum(ohs, axis=0, keepdims=True)
    carry_ref[...] = carry
    cnt_ref[...] = jnp.broadcast_to(carry, cnt_ref.shape)

    ri = jnp.where(lane == 0, e1, jnp.where(lane == 1, e2, jnp.where(lane == 2, rank1, jnp.where(lane == 3, rank2, 0.0))))
    ri_ref[...] = ri.astype(jnp.int32)
    rw_ref[...] = jnp.where(lane == 0, g_p * p1, jnp.where(lane == 1, g_p * p2, 0.0))


def _out_router(x2, four, attn, gt1, sh2, sc2, gf, ga, g2, wof, woa, wrh, wrl, br, *, seq, tm):
    t, d = x2.shape
    nt = t // tm
    tiles_per_b = seq // tm

    def rows(n):
        return pl.BlockSpec((tm, n), lambda i: (i, 0))

    def const(arr):
        return pl.BlockSpec(arr.shape, lambda i: (0,) * arr.ndim)

    mod_spec = pl.BlockSpec((None, 1, d), lambda i: (i // tiles_per_b, 0, 0))
    return pl.pallas_call(
        _out_router_kernel,
        grid=(nt,),
        in_specs=[rows(d), rows(FOURIER_DIM), rows(MLA_DIM), mod_spec, mod_spec, mod_spec,
                  const(gf), const(ga), const(g2), const(wof), const(woa), const(wrh), const(wrl), const(br)],
        out_specs=[rows(d), rows(d // 2), rows(LANES), rows(LANES), pl.BlockSpec((8, LANES), lambda i: (0, 0))],
        out_shape=[jax.ShapeDtypeStruct((t, d), F32), jax.ShapeDtypeStruct((t, d // 2), jnp.uint32),
                   jax.ShapeDtypeStruct((t, LANES), jnp.int32), jax.ShapeDtypeStruct((t, LANES), F32),
                   jax.ShapeDtypeStruct((8, LANES), F32)],
        scratch_shapes=[pltpu.VMEM((1, LANES), F32)],
        compiler_params=_cparams(("arbitrary",)),
        name="out_proj_router",
    )(x2, four, attn, gt1, sh2, sc2, gf, ga, g2, wof, woa, wrh, wrl, br)


def _moe_kernel(item_e, item_blk0, item_nblk, row_tok, used_blocks,
                hm_hbm, wg_hbm, wu_hbm, wd_hbm, y_hbm,
                xg, xb, gs, ab, yp, wg_buf, wu_buf, wd_buf, gsem, osem, wsem):
    i = pl.program_id(0)
    j = pl.program_id(1)
    n_items = pl.num_programs(0)
    nj = pl.num_programs(1)
    slot = i % 2
    nblk = item_nblk[i]
    w_hbm = (wg_hbm, wu_hbm, wd_hbm)
    w_buf = (wg_buf, wu_buf, wd_buf)

    def weight_copy(it, ph):
        return pltpu.make_async_copy(w_hbm[ph].at[item_e[it]], w_buf[ph], wsem.at[ph])

    def start_weight(it, ph):
        it_c = jnp.minimum(it, n_items - 1)

        @pl.when(jnp.logical_and(it < n_items, item_nblk[it_c] > 0))
        def _():
            weight_copy(it_c, ph).start(priority=WEIGHT_DMA_PRIORITY)

    def gather_copy(tok, sl, r):
        return pltpu.make_async_copy(hm_hbm.at[pl.ds(tok, 1)], xg.at[sl, pl.ds(r, 1)], gsem.at[sl])

    def issue_gather(it, sl):
        r0 = item_blk0[it] * MOE_BLOCK

        def body(r8, carry):
            for k in range(GATHER_UNROLL):
                r = r8 * GATHER_UNROLL + k
                gather_copy(row_tok[r0 + r], sl, r).start(priority=ROW_DMA_PRIORITY)
            return carry

        lax.fori_loop(0, item_nblk[it] * (MOE_BLOCK // GATHER_UNROLL), body, 0)

    def wait_gather(it, sl):
        for b in range(ITEM_BLOCKS):
            @pl.when(b < item_nblk[it])
            def _():
                pltpu.make_async_copy(hm_hbm.at[pl.ds(0, MOE_BLOCK)], xg.at[sl, pl.ds(b * MOE_BLOCK, MOE_BLOCK)],
                                      gsem.at[sl]).wait()

    def out_copy(it, m):
        r0 = pl.multiple_of(item_blk0[it] * MOE_BLOCK, MOE_BLOCK)
        return pltpu.make_async_copy(yp.at[pl.ds(0, m)], y_hbm.at[pl.ds(r0, m)], osem.at[0])

    def wait_out(it):
        for nb in range(1, ITEM_BLOCKS + 1):
            @pl.when(item_nblk[it] == nb)
            def _():
                out_copy(it, nb * MOE_BLOCK).wait()

    @pl.when(j == 0)
    def _():
        @pl.when(i == 0)
        def _():
            start_weight(0, 0)
            start_weight(0, 1)
            issue_gather(0, 0)

        start_weight(i, 2)
        wait_gather(i, slot)

        @pl.when(i + 1 < n_items)
        def _():
            issue_gather(i + 1, 1 - slot)

    @pl.when(j == 1)
    def _():
        start_weight(i + 1, 0)

    @pl.when(j == 2)
    def _():
        start_weight(i + 1, 1)

        @pl.when(i > 0)
        def _():
            wait_out(i - 1)

    for ph in range(3):
        @pl.when(jnp.logical_and(j == ph, nblk > 0))
        def _():
            weight_copy(i, ph).wait()

    for nb in range(1, ITEM_BLOCKS + 1):
        m = nb * MOE_BLOCK

        @pl.when(jnp.logical_and(nblk == nb, j == 0))
        def _():
            hi, lo = _unpack_halves(xg[slot, 0:m, :])
            half = hi.shape[1]
            xb[0:m, 0:half] = hi.astype(BF16)
            xb[0:m, half:2 * half] = lo.astype(BF16)
            gs[0:m, :] = _bdot(xb[0:m, :], wg_buf[...].astype(BF16))

        @pl.when(jnp.logical_and(nblk == nb, j == 1))
        def _():
            g = gs[0:m, :]
            u = _bdot(xb[0:m, :], wu_buf[...].astype(BF16))
            ab[0:m, :] = (g * jax.nn.sigmoid(g) * u).astype(BF16)

        @pl.when(jnp.logical_and(nblk == nb, j == 2))
        def _():
            yp[0:m, :] = _pack_halves(_bdot(ab[0:m, :], wd_buf[...].astype(BF16)))
            out_copy(i, m).start()

    @pl.when(jnp.logical_and(i == n_items - 1, j == nj - 1))
    def _():
        wait_out(i)
        n_blocks = y_hbm.shape[0] // MOE_BLOCK
        yp[0:MOE_BLOCK, :] = jnp.zeros((MOE_BLOCK, yp.shape[1]), jnp.uint32)

        def tail_copy(blk):
            r0 = pl.multiple_of(blk * MOE_BLOCK, MOE_BLOCK)
            return pltpu.make_async_copy(yp.at[pl.ds(0, MOE_BLOCK)], y_hbm.at[pl.ds(r0, MOE_BLOCK)], osem.at[0])

        def start_body(blk, carry):
            tail_copy(blk).start()
            return carry

        def wait_body(blk, carry):
            tail_copy(blk).wait()
            return carry

        lax.fori_loop(used_blocks[0], n_blocks, start_body, 0)
        lax.fori_loop(used_blocks[0], n_blocks, wait_body, 0)


def _moe(hm, w_gate, w_up, w_down, item_e, item_blk0, item_nblk, row_tok, used_blocks, *, n_rows):
    d, de = w_gate.shape[1], w_gate.shape[2]
    n_items = item_e.shape[0]
    nj = 3
    rows = ITEM_BLOCKS * MOE_BLOCK
    any_spec = pl.BlockSpec(memory_space=pl.ANY)
    grid_spec = pltpu.PrefetchScalarGridSpec(
        num_scalar_prefetch=5,
        grid=(n_items, nj),
        in_specs=[any_spec, any_spec, any_spec, any_spec],
        out_specs=any_spec,
        scratch_shapes=[pltpu.VMEM((2, rows, d // 2), jnp.uint32),
                        pltpu.VMEM((rows, d), BF16),
                        pltpu.VMEM((rows, de), F32),
                        pltpu.VMEM((rows, de), BF16),
                        pltpu.VMEM((rows, d // 2), jnp.uint32),
                        pltpu.VMEM((d, de), F32),
                        pltpu.VMEM((d, de), F32),
                        pltpu.VMEM((de, d), F32),
                        pltpu.SemaphoreType.DMA((2,)),
                        pltpu.SemaphoreType.DMA((1,)),
                        pltpu.SemaphoreType.DMA((3,))],
    )
    return pl.pallas_call(
        _moe_kernel,
        grid_spec=grid_spec,
        out_shape=jax.ShapeDtypeStruct((n_rows, d // 2), jnp.uint32),
        compiler_params=_cparams(("arbitrary", "arbitrary")),
        name="moe_experts",
    )(item_e, item_blk0, item_nblk, row_tok, used_blocks, hm, w_gate, w_up, w_down)


def _combine_kernel(dest, x_ref, gt2_ref, rw_ref, y_hbm, o_ref, ybuf, sem):
    i = pl.program_id(0)
    n = pl.num_programs(0)
    tm = x_ref.shape[0]
    slot = i % 2

    def issue(it, sl):
        base = it * tm

        def body(r4, carry):
            for rr in range(GATHER_UNROLL // 2):
                r = r4 * (GATHER_UNROLL // 2) + rr
                for k in range(2):
                    pltpu.make_async_copy(y_hbm.at[pl.ds(dest[2 * (base + r) + k], 1)], ybuf.at[sl, k, pl.ds(r, 1)],
                                          sem.at[sl]).start(priority=k)
            return carry

        lax.fori_loop(0, tm // (GATHER_UNROLL // 2), body, 0)

    @pl.when(i == 0)
    def _():
        issue(0, 0)

    for k in range(2):
        pltpu.make_async_copy(y_hbm.at[pl.ds(0, tm)], ybuf.at[slot, k], sem.at[slot]).wait()

    @pl.when(i + 1 < n)
    def _():
        issue(i + 1, 1 - slot)

    w = rw_ref[...]
    hi0, lo0 = _unpack_halves(ybuf[slot, 0])
    hi1, lo1 = _unpack_halves(ybuf[slot, 1])
    half = hi0.shape[1]
    o_ref[:, 0:half] = x_ref[:, 0:half] + gt2_ref[:, 0:half] * (w[:, 0:1] * hi0 + w[:, 1:2] * hi1)
    o_ref[:, half:2 * half] = (x_ref[:, half:2 * half]
                               + gt2_ref[:, half:2 * half] * (w[:, 0:1] * lo0 + w[:, 1:2] * lo1))


def _combine(dest, xnew, gt2, rw, y, *, seq, tm):
    t, d = xnew.shape
    tiles_per_b = seq // tm
    grid_spec = pltpu.PrefetchScalarGridSpec(
        num_scalar_prefetch=1,
        grid=(t // tm,),
        in_specs=[pl.BlockSpec((tm, d), lambda i, ds: (i, 0)),
                  pl.BlockSpec((None, 1, d), lambda i, ds: (i // tiles_per_b, 0, 0)),
                  pl.BlockSpec((tm, LANES), lambda i, ds: (i, 0)),
                  pl.BlockSpec(memory_space=pl.ANY)],
        out_specs=pl.BlockSpec((tm, d), lambda i, ds: (i, 0)),
        scratch_shapes=[pltpu.VMEM((2, 2, tm, d // 2), jnp.uint32), pltpu.SemaphoreType.DMA((2,))],
    )
    return pl.pallas_call(
        _combine_kernel,
        grid_spec=grid_spec,
        out_shape=jax.ShapeDtypeStruct((t, d), F32),
        compiler_params=_cparams(("arbitrary",)),
        name="moe_combine",
    )(dest, xnew, gt2, rw, y)


def _rope_tables(n_tokens):
    rows = n_tokens // GRID_W
    row = jnp.repeat(jnp.arange(rows, dtype=jnp.int32), GRID_W).astype(F32)
    col = jnp.tile(jnp.arange(GRID_W, dtype=jnp.int32), rows).astype(F32)
    n_freq = QK_ROPE_DIM // 4
    inv = ROPE_THETA ** (-jnp.arange(n_freq, dtype=F32) / n_freq)
    ar = row[:, None] * inv[None, :]
    ac = col[:, None] * inv[None, :]
    cr, sr, cc, sc = jnp.cos(ar), jnp.sin(ar), jnp.cos(ac), jnp.sin(ac)
    cos64 = jnp.concatenate([cr, cr, cc, cc], axis=-1)
    sin64 = jnp.concatenate([-sr, sr, -sc, sc], axis=-1)
    return jnp.tile(cos64, (1, 2)), jnp.tile(sin64, (1, 2))


def _channel_dft_table():
    c = np.arange(FOURIER_GROUP_DIM).reshape(-1, 1)
    k = np.arange(FOURIER_GROUP_DIM).reshape(1, -1)
    ang = (2.0 * np.pi / FOURIER_GROUP_DIM) * ((c * k) % FOURIER_GROUP_DIM)
    return jnp.asarray(np.concatenate([np.cos(ang), -np.sin(ang)], axis=1).astype(np.float32)).astype(BF16)


def _split_heads(w, widths):
    k = w.shape[0]
    wh = w.reshape(k, MLA_HEADS, sum(widths))
    outs, off = [], 0
    for wd in widths:
        outs.append(wh[:, :, off:off + wd].reshape(k, MLA_HEADS * wd))
        off += wd
    return outs


def kernel(x, c, ctx, c_ctx, w_ada, b_ada, g_norm1, g_norm2, w_in, g_q_a, g_kv_a, w_uq, w_ukv, g_qk_q, g_qk_k,
           g_out_four, g_out_attn, w_out, w_router_group, b_router_group, w_router_expert, b_router_expert,
           w_gate, w_up, w_down):
    b, s, d = x.shape
    lc = ctx.shape[1]
    t = b * s
    layer_params = (w_ada, b_ada, g_norm1, g_norm2, w_in, g_q_a, g_kv_a, w_uq, w_ukv, g_qk_q, g_qk_k, g_out_four,
                    g_out_attn, w_out, w_router_group, b_router_group, w_router_expert, b_router_expert,
                    w_gate, w_up, w_down)
    assert all(p.shape[0] == 1 for p in layer_params), "single-layer block"
    (w_ada, b_ada, g_norm1, g_norm2, w_in, g_q_a, g_kv_a, w_uq, w_ukv, g_qk_q, g_qk_k, g_out_four,
     g_out_attn, w_out, w_router_group, b_router_group, w_router_expert, b_router_expert,
     w_gate, w_up, w_down) = [p.reshape(p.shape[1:]) for p in layer_params]

    cond8 = jnp.concatenate([c, c_ctx[None, :], jnp.zeros((8 - b - 1, d), F32)], axis=0)
    mods = _ada_mod(cond8, w_ada, b_ada)
    sh1, sc1, gt1, sh2, sc2, gt2 = [m[:b].reshape(b, 1, d) for m in jnp.split(mods, N_MOD, axis=-1)]
    csh1, csc1 = [m[b:b + 1].reshape(1, 1, d) for m in jnp.split(mods, N_MOD, axis=-1)[:2]]

    wf = w_in[:, :Q_OFF].astype(BF16)
    wq = w_in[:, Q_OFF:KV_OFF].astype(BF16)
    wkv = w_in[:, KV_OFF:ROPE_OFF].astype(BF16)
    wr = w_in[:, ROPE_OFF:].astype(BF16)
    wr2 = jnp.concatenate([wr, wr], axis=1)
    dc = _channel_dft_table()
    wqn, wqr = [w.astype(BF16) for w in _split_heads(w_uq, (QK_NOPE_DIM, QK_ROPE_DIM))]
    wkn, wv = [w.astype(BF16) for w in _split_heads(w_ukv, (QK_NOPE_DIM, V_HEAD_DIM))]
    gqn = g_qk_q[:QK_NOPE_DIM].reshape(1, -1)
    gqr2 = jnp.tile(g_qk_q[QK_NOPE_DIM:], 2).reshape(1, -1)
    gkn = g_qk_k[:QK_NOPE_DIM].reshape(1, -1)
    gkr2 = jnp.tile(g_qk_k[QK_NOPE_DIM:], 2).reshape(1, -1)
    g1 = g_norm1.reshape(1, d)
    gq = g_q_a.reshape(1, -1)
    gkv = g_kv_a.reshape(1, -1)

    x2 = x.reshape(t, d)
    u, cq, ckv, kr2 = _in_proj(x2, sh1, sc1, s, g1, gq, gkv, wf, wq, wkv, wr2, dc, with_q=True, tm=512)
    ckv_c, kr2_c = _in_proj(ctx.reshape(b * lc, d), csh1, csc1, b * lc, g1, gq, gkv, wf, wq, wkv, wr2, dc,
                            with_q=False, tm=lc)

    cos_t, sin_t = _rope_tables(s)
    q, kx, vx = _qkv(cq, ckv, kr2, cos_t, sin_t, wqn, wqr, gqn, gqr2, wkn, wv, gkn, gkr2,
                     batch=b, seq=s, tm=512, with_q=True, with_rope=True)
    kc, vc = _qkv(None, ckv_c, kr2_c, None, None, None, None, None, None, wkn, wv, gkn, gkr2,
                  batch=b, seq=lc, tm=lc, with_q=False, with_rope=False)

    attn = _attention(q, kx, kc, vx, vc, tq=1024).reshape(t, MLA_DIM)
    four = _seq_dft(u, b, s)

    wof = w_out[:FOURIER_DIM].astype(BF16)
    woa = w_out[FOURIER_DIM:].astype(BF16)
    n_route = N_GROUPS + N_EXPERTS
    wrt = jnp.concatenate([w_router_group, w_router_expert, jnp.zeros((d, LANES - n_route), F32)], axis=1)
    wrt_hi = wrt.astype(BF16)
    wrt_lo = (wrt - wrt_hi.astype(F32)).astype(BF16)
    brt = jnp.concatenate([b_router_group, b_router_expert, jnp.zeros((LANES - n_route,), F32)]).reshape(1, -1)
    xnew, hm, ri, rw, cnt = _out_router(x2, four, attn, gt1, sh2, sc2, g_out_four.reshape(1, -1),
                                        g_out_attn.reshape(1, -1), g_norm2.reshape(1, d), wof, woa, wrt_hi, wrt_lo, brt,
                                        seq=s, tm=256)

    counts = cnt[0, :N_EXPERTS].astype(jnp.int32)
    nblk_e = (counts + MOE_BLOCK - 1) // MOE_BLOCK
    blk_end = jnp.cumsum(nblk_e)
    blk_start = blk_end - nblk_e
    n_slots = t * 2
    n_blocks = -(-(n_slots + N_EXPERTS * (MOE_BLOCK - 1)) // MOE_BLOCK)
    n_rows = n_blocks * MOE_BLOCK
    e12 = ri[:, 0:2]
    seg_start = jnp.sum(jnp.where(e12[:, :, None] == jnp.arange(N_EXPERTS, dtype=jnp.int32), blk_start, 0), axis=-1)
    dest = (seg_start * MOE_BLOCK + ri[:, 2:4]).reshape(-1)
    tok_ids = jnp.repeat(jnp.arange(t, dtype=jnp.int32), 2)
    row_tok = jnp.zeros((n_rows,), jnp.int32).at[dest].set(tok_ids)
    items_e = (nblk_e + ITEM_BLOCKS - 1) // ITEM_BLOCKS
    item_end = jnp.cumsum(items_e)
    n_items = (n_blocks + (ITEM_BLOCKS - 1) * N_EXPERTS) // ITEM_BLOCKS
    idx = jnp.arange(n_items, dtype=jnp.int32)
    total = item_end[-1]
    idx_c = jnp.minimum(idx, total - 1)
    ie = jnp.minimum(jnp.sum(idx_c[:, None] >= item_end[None, :], axis=1), N_EXPERTS - 1).astype(jnp.int32)
    local = idx_c - (item_end - items_e)[ie]
    item_blk0 = (blk_start[ie] + ITEM_BLOCKS * local).astype(jnp.int32)
    item_nblk = jnp.where(idx < total, jnp.clip(nblk_e[ie] - ITEM_BLOCKS * local, 0, ITEM_BLOCKS), 0).astype(jnp.int32)

    used_blocks = blk_end[-1:].astype(jnp.int32)
    y = _moe(hm, w_gate, w_up, w_down, ie, item_blk0, item_nblk, row_tok, used_blocks, n_rows=n_rows)
    out = _combine(dest.astype(jnp.int32), xnew, gt2, rw, y, seq=s, tm=128)
    return out.reshape(b, s, d)
```

```python
import functools
import math

import numpy as np
import jax
import jax.numpy as jnp
from jax import lax
from jax.experimental import pallas as pl
from jax.experimental.pallas import tpu as pltpu

F32 = jnp.float32
BF16 = jnp.bfloat16

D_MODEL = 2048
GRID_W = 64
EPS = 1e-6
N_MOD = 6
N_FOURIER_GROUPS = 4
FOURIER_GROUP_DIM = 256
FOURIER_DIM = 1024
MLA_HEADS = 8
QK_NOPE_DIM = 128
QK_ROPE_DIM = 64
QK_HEAD_DIM = 192
V_HEAD_DIM = 128
Q_LORA_RANK = 768
KV_LORA_RANK = 512
MLA_DIM = 1024
ROPE_THETA = 10000.0
Q_OFF = FOURIER_DIM
KV_OFF = Q_OFF + Q_LORA_RANK
ROPE_OFF = KV_OFF + KV_LORA_RANK
N_GROUPS = 8
EXPERTS_PER_GROUP = 8
N_EXPERTS = 64
D_EXPERT = 768

LANES = 128
HEAD_PAD = 256
V_PAD = 256
MOE_BLOCK = 64
ITEM_BLOCKS = 8
GATHER_UNROLL = 8
ATTN_KEY_CHUNK = 512
ROW_DMA_PRIORITY = 0
WEIGHT_DMA_PRIORITY = 1
VMEM_LIMIT = 56 * 1024 * 1024
NEG_BIG = -1e30


def _cparams(sem):
    return pltpu.CompilerParams(dimension_semantics=sem, vmem_limit_bytes=VMEM_LIMIT)


def _bdot(a, b):
    return jnp.dot(a, b, preferred_element_type=F32)


def _pack_halves(v):
    n = v.shape[1] // 2
    hi = pltpu.bitcast(v[:, :n].astype(BF16).astype(F32), jnp.uint32)
    lo = pltpu.bitcast(v[:, n:].astype(BF16).astype(F32), jnp.uint32)
    return hi | (lo >> 16)


def _unpack_halves(u):
    hi = pltpu.bitcast(u & jnp.uint32(0xFFFF0000), F32)
    lo = pltpu.bitcast(u << 16, F32)
    return hi, lo


def _ada_kernel(c_ref, w_ref, b_ref, o_ref):
    c = c_ref[...]
    s = (c * jax.nn.sigmoid(c)).astype(BF16)
    o_ref[...] = _bdot(s, w_ref[...].astype(BF16)) + b_ref[...]


def _ada_mod(cond8, w_ada, b_ada):
    d, n = w_ada.shape
    tn = 1024
    return pl.pallas_call(
        _ada_kernel,
        grid=(n // tn,),
        in_specs=[pl.BlockSpec((8, d), lambda i: (0, 0)),
                  pl.BlockSpec((d, tn), lambda i: (0, i)),
                  pl.BlockSpec((1, tn), lambda i: (0, i))],
        out_specs=pl.BlockSpec((8, tn), lambda i: (0, i)),
        out_shape=jax.ShapeDtypeStruct((8, n), F32),
        compiler_params=_cparams(("arbitrary",)),
        name="ada_mod",
    )(cond8, w_ada, b_ada.reshape(1, n))


def _in_proj_kernel(x_ref, sh_ref, sc_ref, g_ref, gq_ref, gkv_ref, wf_ref, wq_ref, wkv_ref, wr_ref, dc_ref,
                    *out_refs, with_q):
    x = x_ref[...]
    ms = jnp.mean(x * x, axis=-1, keepdims=True)
    a = g_ref[...] * (1.0 + sc_ref[...])
    h = (x * lax.rsqrt(ms + EPS) * a + sh_ref[...]).astype(BF16)
    if with_q:
        u_ref, cq_ref, ckv_ref, kr_ref = out_refs
        f = _bdot(h, wf_ref[...]).astype(BF16)
        dc = dc_ref[...]
        for g in range(N_FOURIER_GROUPS):
            lo = g * FOURIER_GROUP_DIM
            ug = _bdot(f[:, lo:lo + FOURIER_GROUP_DIM], dc)
            u_ref[:, lo:lo + FOURIER_GROUP_DIM] = ug[:, :FOURIER_GROUP_DIM]
            u_ref[:, FOURIER_DIM + lo:FOURIER_DIM + lo + FOURIER_GROUP_DIM] = ug[:, FOURIER_GROUP_DIM:]
        pq = _bdot(h, wq_ref[...])
        msq = jnp.mean(pq * pq, axis=-1, keepdims=True)
        cq_ref[...] = (pq * lax.rsqrt(msq + EPS) * gq_ref[...]).astype(BF16)
    else:
        ckv_ref, kr_ref = out_refs
    pkv = _bdot(h, wkv_ref[...])
    mskv = jnp.mean(pkv * pkv, axis=-1, keepdims=True)
    ckv_ref[...] = (pkv * lax.rsqrt(mskv + EPS) * gkv_ref[...]).astype(BF16)
    kr_ref[...] = _bdot(h, wr_ref[...])


def _in_proj(x2, shift, scale, rows_per_mod, g1, gq, gkv, wf, wq, wkv, wr2, dc, *, with_q, tm):
    t, d = x2.shape
    nt = t // tm
    tiles_per_mod = rows_per_mod // tm

    def const(shape):
        return pl.BlockSpec(shape, lambda i: (0,) * len(shape))

    mod_spec = pl.BlockSpec((None, 1, d), lambda i: (i // tiles_per_mod, 0, 0))
    in_specs = [pl.BlockSpec((tm, d), lambda i: (i, 0)), mod_spec, mod_spec,
                const((1, d)), const((1, Q_LORA_RANK)), const((1, KV_LORA_RANK)),
                const(wf.shape), const(wq.shape), const(wkv.shape), const(wr2.shape), const(dc.shape)]

    def rows(n):
        return pl.BlockSpec((tm, n), lambda i: (i, 0))

    out_specs = [rows(KV_LORA_RANK), rows(LANES)]
    out_shape = [jax.ShapeDtypeStruct((t, KV_LORA_RANK), BF16), jax.ShapeDtypeStruct((t, LANES), F32)]
    if with_q:
        out_specs = [rows(2 * FOURIER_DIM), rows(Q_LORA_RANK)] + out_specs
        out_shape = [jax.ShapeDtypeStruct((t, 2 * FOURIER_DIM), F32),
                     jax.ShapeDtypeStruct((t, Q_LORA_RANK), BF16)] + out_shape
    return pl.pallas_call(
        functools.partial(_in_proj_kernel, with_q=with_q),
        grid=(nt,),
        in_specs=in_specs,
        out_specs=out_specs,
        out_shape=out_shape,
        compiler_params=_cparams(("arbitrary",)),
        name="in_proj_x" if with_q else "in_proj_ctx",
    )(x2, shift, scale, g1, gq, gkv, wf, wq, wkv, wr2, dc)


def _swap_halves(y, first_half):
    return jnp.where(first_half, pltpu.roll(y, LANES - 16, 1), pltpu.roll(y, 16, 1))


def _qkv_kernel(*refs, with_q, with_rope):
    it = iter(refs)
    if with_q:
        cq_ref = next(it)
    ckv_ref = next(it)
    kr_ref = next(it)
    if with_rope:
        cos_ref = next(it)
        sin_ref = next(it)
    if with_q:
        wqn_ref = next(it)
        wqr_ref = next(it)
        gqn_ref = next(it)
        gqr_ref = next(it)
    wkn_ref = next(it)
    wv_ref = next(it)
    gkn_ref = next(it)
    gkr_ref = next(it)
    if with_q:
        q_ref = next(it)
    k_ref = next(it)
    v_ref = next(it)

    tm = ckv_ref.shape[0]
    lane = lax.broadcasted_iota(jnp.int32, (tm, LANES), 1)
    low = lane < QK_ROPE_DIM
    first_half = (lane % 32) < 16
    inv_dim = 1.0 / QK_HEAD_DIM

    def rope(y):
        if not with_rope:
            return y
        return y * cos_ref[...] + _swap_halves(y, first_half) * sin_ref[...]

    if with_q:
        cq = cq_ref[...]
        qn = _bdot(cq, wqn_ref[...])
        qr = _bdot(cq, wqr_ref[...])
        qscale = QK_HEAD_DIM ** -0.5
        for p in range(MLA_HEADS // 2):
            blk = qr[:, p * LANES:(p + 1) * LANES]
            sq = blk * blk
            ss_lo = jnp.sum(jnp.where(low, sq, 0.0), axis=-1, keepdims=True)
            ss_hi = jnp.sum(jnp.where(low, 0.0, sq), axis=-1, keepdims=True)
            scales = []
            for hh, ssr in ((2 * p, ss_lo), (2 * p + 1, ss_hi)):
                nh = qn[:, hh * LANES:(hh + 1) * LANES]
                ssq = jnp.sum(nh * nh, axis=-1, keepdims=True) + ssr
                s = lax.rsqrt(ssq * inv_dim + EPS)
                scales.append(s)
                q_ref[hh, :, 0:LANES] = (nh * s * gqn_ref[...] * qscale).astype(BF16)
            s_pair = jnp.where(low, scales[0], scales[1])
            r = rope(blk * s_pair * gqr_ref[...]) * qscale
            q_ref[2 * p, :, LANES:2 * LANES] = jnp.where(low, r, 0.0).astype(BF16)
            q_ref[2 * p + 1, :, LANES:2 * LANES] = jnp.where(low, pltpu.roll(r, QK_ROPE_DIM, 1), 0.0).astype(BF16)

    ckv = ckv_ref[...]
    kn = _bdot(ckv, wkn_ref[...])
    v = _bdot(ckv, wv_ref[...])
    kr = kr_ref[...]
    ss_r = jnp.sum(jnp.where(low, kr * kr, 0.0), axis=-1, keepdims=True)
    base = rope(kr * gkr_ref[...])
    ones_col = jnp.where(lane == 0, 1.0, 0.0).astype(BF16)
    for hh in range(MLA_HEADS):
        nh = kn[:, hh * LANES:(hh + 1) * LANES]
        ssq = jnp.sum(nh * nh, axis=-1, keepdims=True) + ss_r
        s = lax.rsqrt(ssq * inv_dim + EPS)
        k_ref[hh, :, 0:LANES] = (nh * s * gkn_ref[...]).astype(BF16)
        k_ref[hh, :, LANES:2 * LANES] = jnp.where(low, base * s, 0.0).astype(BF16)
        v_ref[hh, :, 0:LANES] = v[:, hh * LANES:(hh + 1) * LANES].astype(BF16)
        v_ref[hh, :, LANES:2 * LANES] = ones_col


def _qkv(cq, ckv, kr2, cos_t, sin_t, wqn, wqr, gqn, gqr2, wkn, wv, gkn, gkr2, *, batch, seq, tm, with_q,
         with_rope):
    t = ckv.shape[0]
    nt = t // tm
    tiles_per_b = seq // tm

    def rows(n):
        return pl.BlockSpec((tm, n), lambda i: (i, 0))

    def const(arr):
        return pl.BlockSpec(arr.shape, lambda i: (0,) * arr.ndim)

    tab_spec = pl.BlockSpec((tm, LANES), lambda i: (i % tiles_per_b, 0))

    def head_out(width):
        return pl.BlockSpec((None, MLA_HEADS, tm, width), lambda i: (i // tiles_per_b, 0, i % tiles_per_b, 0))

    args, in_specs = [], []
    if with_q:
        args.append(cq)
        in_specs.append(rows(Q_LORA_RANK))
    args += [ckv, kr2]
    in_specs += [rows(KV_LORA_RANK), rows(LANES)]
    if with_rope:
        args += [cos_t, sin_t]
        in_specs += [tab_spec, tab_spec]
    if with_q:
        args += [wqn, wqr, gqn, gqr2]
        in_specs += [const(wqn), const(wqr), const(gqn), const(gqr2)]
    args += [wkn, wv, gkn, gkr2]
    in_specs += [const(wkn), const(wv), const(gkn), const(gkr2)]

    out_specs = [head_out(HEAD_PAD), head_out(V_PAD)]
    out_shape = [jax.ShapeDtypeStruct((batch, MLA_HEADS, seq, HEAD_PAD), BF16),
                 jax.ShapeDtypeStruct((batch, MLA_HEADS, seq, V_PAD), BF16)]
    if with_q:
        out_specs = [head_out(HEAD_PAD)] + out_specs
        out_shape = [jax.ShapeDtypeStruct((batch, MLA_HEADS, seq, HEAD_PAD), BF16)] + out_shape
    return pl.pallas_call(
        functools.partial(_qkv_kernel, with_q=with_q, with_rope=with_rope),
        grid=(nt,),
        in_specs=in_specs,
        out_specs=out_specs,
        out_shape=out_shape,
        compiler_params=_cparams(("arbitrary",)),
        name="qkv_x" if with_q else "kv_ctx",
    )(*args)


def _attn_kernel(q_ref, kx_ref, kc_ref, vx_ref, vc_ref, o_ref):
    q = q_ref[...]
    tq = q.shape[0]
    dn = (((1,), (1,)), ((), ()))

    def chunk(k, v, state):
        m, acc = state
        s = lax.dot_general(q, k, dn, preferred_element_type=F32)
        m_new = jnp.maximum(m, jnp.max(s, axis=-1, keepdims=True))
        p = jnp.exp(s - m_new).astype(BF16)
        acc = jnp.exp(m - m_new) * acc + _bdot(p, v)
        return m_new, acc

    state = (jnp.full((tq, 1), NEG_BIG, F32), jnp.zeros((tq, V_PAD), F32))
    for c in range(kx_ref.shape[0] // ATTN_KEY_CHUNK):
        rows = slice(c * ATTN_KEY_CHUNK, (c + 1) * ATTN_KEY_CHUNK)
        state = chunk(kx_ref[rows, :], vx_ref[rows, :], state)
    _, acc = chunk(kc_ref[...], vc_ref[...], state)
    o_ref[...] = acc[:, :V_HEAD_DIM] / acc[:, V_HEAD_DIM:V_HEAD_DIM + 1]


def _attention(q, kx, kc, vx, vc, *, tq):
    b, h, s, _ = q.shape
    lc = kc.shape[2]
    return pl.pallas_call(
        _attn_kernel,
        grid=(b, h, s // tq),
        in_specs=[pl.BlockSpec((None, None, tq, HEAD_PAD), lambda bi, hi, qi: (bi, hi, qi, 0)),
                  pl.BlockSpec((None, None, s, HEAD_PAD), lambda bi, hi, qi: (bi, hi, 0, 0)),
                  pl.BlockSpec((None, None, lc, HEAD_PAD), lambda bi, hi, qi: (bi, hi, 0, 0)),
                  pl.BlockSpec((None, None, s, V_PAD), lambda bi, hi, qi: (bi, hi, 0, 0)),
                  pl.BlockSpec((None, None, lc, V_PAD), lambda bi, hi, qi: (bi, hi, 0, 0))],
        out_specs=pl.BlockSpec((None, tq, V_HEAD_DIM), lambda bi, hi, qi: (bi, qi, hi)),
        out_shape=jax.ShapeDtypeStruct((b, s, h * V_HEAD_DIM), F32),
        compiler_params=_cparams(("arbitrary", "arbitrary", "arbitrary")),
        name="attention",
    )(q, kx, kc, vx, vc)


def _dft_stage1_kernel(u_ref, r_ref, e_ref, o_ref):
    n = 2 * GRID_W * 8
    t = _bdot(r_ref[...].astype(BF16), e_ref[...])
    row = lax.broadcasted_iota(jnp.int32, (n, n), 0)
    col = lax.broadcasted_iota(jnp.int32, (n, n), 1)
    t = jnp.where((row % 8) == (col % 8), t, 0.0).astype(BF16)
    u = u_ref[...].reshape(GRID_W * 8, 2 * FOURIER_DIM)
    rhs = jnp.concatenate([u[:, :FOURIER_DIM], u[:, FOURIER_DIM:]], axis=0).astype(BF16)
    a = _bdot(t, rhs)
    o_ref[...] = a.reshape(2, GRID_W, 8, FOURIER_DIM)


def _dft_stage2_kernel(a_ref, t_ref, o_ref):
    rhs = a_ref[...].reshape(2 * 8 * GRID_W, FOURIER_DIM).astype(BF16)
    y = _bdot(t_ref[...].astype(BF16), rhs)
    o_ref[...] = y.reshape(GRID_W, 8, FOURIER_DIM)


def _seq_dft_tables(n_seq):
    w = GRID_W
    ch = np.arange(8).reshape(8, 1, 1, 1)
    kb = np.arange(w).reshape(1, w, 1, 1)
    j = np.arange(8).reshape(1, 1, 8, 1)
    r = np.arange(w).reshape(1, 1, 1, w)
    ang = (2.0 * np.pi / n_seq) * ((kb * (w * r + 8 * ch + j)) % n_seq)
    c, s = np.cos(ang), np.sin(ang)
    rot = np.stack([np.stack([c, s], axis=3), np.stack([-s, c], axis=3)], axis=1)
    r1 = rot.reshape(8, 2 * w * 8, 2 * w).astype(np.float32)
    expand = (np.arange(2 * w * 8)[None, :] // 8 == np.arange(2 * w)[:, None]).astype(np.float32)
    ka = np.arange(w).reshape(w, 1)
    cp = np.arange(w).reshape(1, w)
    ang2 = (2.0 * np.pi / w) * ((ka * cp) % w)
    norm = 1.0 / math.sqrt(n_seq * FOURIER_GROUP_DIM)
    cs = np.stack([np.cos(ang2), np.sin(ang2)], axis=1) * norm
    eye8 = np.eye(8)
    t2 = (cs[:, None, :, None, :] * eye8[None, :, None, :, None]).reshape(w * 8, 2 * 8 * w).astype(np.float32)
    return jnp.asarray(r1), jnp.asarray(expand).astype(BF16), jnp.asarray(t2)


def _seq_dft(u, batch, n_seq):
    w = GRID_W
    r1, expand, t2 = _seq_dft_tables(n_seq)
    u5 = u.reshape(batch, w, 8, 8, 2 * FOURIER_DIM)
    a = pl.pallas_call(
        _dft_stage1_kernel,
        grid=(batch, 8),
        in_specs=[pl.BlockSpec((None, w, None, 8, 2 * FOURIER_DIM), lambda b, c: (b, 0, c, 0, 0)),
                  pl.BlockSpec((None, 2 * w * 8, 2 * w), lambda b, c: (c, 0, 0)),
                  pl.BlockSpec((2 * w, 2 * w * 8), lambda b, c: (0, 0))],
        out_specs=pl.BlockSpec((None, 2, w, None, 8, FOURIER_DIM), lambda b, c: (b, 0, 0, c, 0, 0)),
        out_shape=jax.ShapeDtypeStruct((batch, 2, w, 8, 8, FOURIER_DIM), F32),
        compiler_params=_cparams(("arbitrary", "arbitrary")),
        name="seq_dft_stage1",
    )(u5, r1, expand)
    y = pl.pallas_call(
        _dft_stage2_kernel,
        grid=(batch, 8),
        in_specs=[pl.BlockSpec((None, 2, 8, 8, 8, FOURIER_DIM), lambda b, k: (b, 0, k, 0, 0, 0)),
                  pl.BlockSpec((w * 8, 2 * 8 * w), lambda b, k: (0, 0))],
        out_specs=pl.BlockSpec((None, w, None, 8, FOURIER_DIM), lambda b, k: (b, 0, k, 0, 0)),
        out_shape=jax.ShapeDtypeStruct((batch, w, 8, 8, FOURIER_DIM), F32),
        compiler_params=_cparams(("arbitrary", "arbitrary")),
        name="seq_dft_stage2",
    )(a, t2)
    return y.reshape(batch * n_seq, FOURIER_DIM)


def _out_router_kernel(x_ref, four_ref, attn_ref, gt1_ref, sh2_ref, sc2_ref, gf_ref, ga_ref, g2_ref,
                       wof_ref, woa_ref, wrh_ref, wrl_ref, br_ref,
                       xnew_ref, hm_ref, ri_ref, rw_ref, cnt_ref, carry_ref):
    i = pl.program_id(0)
    tm = x_ref.shape[0]

    @pl.when(i == 0)
    def _():
        carry_ref[...] = jnp.zeros_like(carry_ref)

    def norm(v, g):
        return (v * lax.rsqrt(jnp.mean(v * v, axis=-1, keepdims=True) + EPS) * g).astype(BF16)

    mix = _bdot(norm(four_ref[...], gf_ref[...]), wof_ref[...]) + _bdot(norm(attn_ref[...], ga_ref[...]), woa_ref[...])
    xn = x_ref[...] + gt1_ref[...] * mix
    xnew_ref[...] = xn
    ms = jnp.mean(xn * xn, axis=-1, keepdims=True)
    hm = xn * lax.rsqrt(ms + EPS) * (g2_ref[...] * (1.0 + sc2_ref[...])) + sh2_ref[...]
    hm_ref[...] = _pack_halves(hm)

    hm_hi = hm.astype(BF16)
    hm_lo = (hm - hm_hi.astype(F32)).astype(BF16)
    logits = _bdot(hm_hi, wrh_ref[...]) + _bdot(hm_lo, wrh_ref[...]) + _bdot(hm_hi, wrl_ref[...]) + br_ref[...]
    lane = lax.broadcasted_iota(jnp.int32, (tm, LANES), 1)
    lanef = lane.astype(F32)
    far = 1e9

    lg = jnp.where(lane < N_GROUPS, logits, NEG_BIG)
    m1 = jnp.max(lg, axis=-1, keepdims=True)
    g_p = 1.0 / jnp.sum(jnp.exp(lg - m1), axis=-1, keepdims=True)
    gidx = jnp.min(jnp.where(lg >= m1, lanef, far), axis=-1, keepdims=True)
    lo = N_GROUPS + EXPERTS_PER_GROUP * gidx
    in_group = jnp.where(lanef >= lo, jnp.where(lanef < lo + EXPERTS_PER_GROUP, 1.0, 0.0), 0.0) > 0.5
    le = jnp.where(in_group, logits, NEG_BIG)
    m2 = jnp.max(le, axis=-1, keepdims=True)
    idx1 = jnp.min(jnp.where(le >= m2, lanef, far), axis=-1, keepdims=True)
    le2 = jnp.where(lanef == idx1, NEG_BIG, le)
    m3 = jnp.max(le2, axis=-1, keepdims=True)
    idx2 = jnp.min(jnp.where(le2 >= m3, lanef, far), axis=-1, keepdims=True)
    t = jnp.exp(m3 - m2)
    p1 = 1.0 / (1.0 + t)
    p2 = t / (1.0 + t)
    e1 = idx1 - N_GROUPS
    e2 = idx2 - N_GROUPS

    oh1 = jnp.where(lanef == e1, 1.0, 0.0)
    oh2 = jnp.where(lanef == e2, 1.0, 0.0)
    ohs = oh1 + oh2
    row = lax.broadcasted_iota(jnp.int32, (tm, tm), 0)
    col = lax.broadcasted_iota(jnp.int32, (tm, tm), 1)
    tri = jnp.where(row > col, 1.0, 0.0).astype(BF16)
    before = _bdot(tri, ohs.astype(BF16)) + carry_ref[...]
    rank1 = jnp.sum(oh1 * before, axis=-1, keepdims=True)
    rank2 = jnp.sum(oh2 * before, axis=-1, keepdims=True)
    carry = carry_ref[...] + jnp.sum(ohs, axis=0, keepdims=True)
    carry_ref[...] = carry
    cnt_ref[...] = jnp.broadcast_to(carry, cnt_ref.shape)

    ri = jnp.where(lane == 0, e1, jnp.where(lane == 1, e2, jnp.where(lane == 2, rank1, jnp.where(lane == 3, rank2, 0.0))))
    ri_ref[...] = ri.astype(jnp.int32)
    rw_ref[...] = jnp.where(lane == 0, g_p * p1, jnp.where(lane == 1, g_p * p2, 0.0))


def _out_router(x2, four, attn, gt1, sh2, sc2, gf, ga, g2, wof, woa, wrh, wrl, br, *, seq, tm):
    t, d = x2.shape
    nt = t // tm
    tiles_per_b = seq // tm

    def rows(n):
        return pl.BlockSpec((tm, n), lambda i: (i, 0))

    def const(arr):
        return pl.BlockSpec(arr.shape, lambda i: (0,) * arr.ndim, pipeline_mode=pl.Buffered(1))

    mod_spec = pl.BlockSpec((None, 1, d), lambda i: (i // tiles_per_b, 0, 0))
    return pl.pallas_call(
        _out_router_kernel,
        grid=(nt,),
        in_specs=[rows(d), rows(FOURIER_DIM), rows(MLA_DIM), mod_spec, mod_spec, mod_spec,
                  const(gf), const(ga), const(g2), const(wof), const(woa), const(wrh), const(wrl), const(br)],
        out_specs=[rows(d), rows(d // 2), rows(LANES), rows(LANES), pl.BlockSpec((8, LANES), lambda i: (0, 0))],
        out_shape=[jax.ShapeDtypeStruct((t, d), F32), jax.ShapeDtypeStruct((t, d // 2), jnp.uint32),
                   jax.ShapeDtypeStruct((t, LANES), jnp.int32), jax.ShapeDtypeStruct((t, LANES), F32),
                   jax.ShapeDtypeStruct((8, LANES), F32)],
        scratch_shapes=[pltpu.VMEM((1, LANES), F32)],
        compiler_params=_cparams(("arbitrary",)),
        name="out_proj_router",
    )(x2, four, attn, gt1, sh2, sc2, gf, ga, g2, wof, woa, wrh, wrl, br)


def _moe_kernel(item_e, item_blk0, item_nblk, row_tok, used_blocks,
                hm_hbm, wg_hbm, wu_hbm, wd_hbm, y_hbm,
                xg, xb, gs, ab, yp, wg_buf, wu_buf, wd_buf, gsem, osem, wsem):
    i = pl.program_id(0)
    j = pl.program_id(1)
    n_items = pl.num_programs(0)
    nj = pl.num_programs(1)
    slot = i % 2
    nblk = item_nblk[i]
    w_hbm = (wg_hbm, wu_hbm, wd_hbm)
    w_buf = (wg_buf, wu_buf, wd_buf)

    def weight_copy(it, ph):
        return pltpu.make_async_copy(w_hbm[ph].at[item_e[it]], w_buf[ph], wsem.at[ph])

    def start_weight(it, ph):
        it_c = jnp.minimum(it, n_items - 1)

        @pl.when(jnp.logical_and(it < n_items, item_nblk[it_c] > 0))
        def _():
            weight_copy(it_c, ph).start(priority=WEIGHT_DMA_PRIORITY)

    def gather_copy(tok, sl, r):
        return pltpu.make_async_copy(hm_hbm.at[pl.ds(tok, 1)], xg.at[sl, pl.ds(r, 1)], gsem.at[sl])

    def issue_gather(it, sl):
        r0 = item_blk0[it] * MOE_BLOCK

        def body(r8, carry):
            for k in range(GATHER_UNROLL):
                r = r8 * GATHER_UNROLL + k
                gather_copy(row_tok[r0 + r], sl, r).start(priority=ROW_DMA_PRIORITY)
            return carry

        lax.fori_loop(0, item_nblk[it] * (MOE_BLOCK // GATHER_UNROLL), body, 0)

    def wait_gather(it, sl):
        for b in range(ITEM_BLOCKS):
            @pl.when(b < item_nblk[it])
            def _():
                pltpu.make_async_copy(hm_hbm.at[pl.ds(0, MOE_BLOCK)], xg.at[sl, pl.ds(b * MOE_BLOCK, MOE_BLOCK)],
                                      gsem.at[sl]).wait()

    def out_copy(it, m):
        r0 = pl.multiple_of(item_blk0[it] * MOE_BLOCK, MOE_BLOCK)
        return pltpu.make_async_copy(yp.at[pl.ds(0, m)], y_hbm.at[pl.ds(r0, m)], osem.at[0])

    def wait_out(it):
        for nb in range(1, ITEM_BLOCKS + 1):
            @pl.when(item_nblk[it] == nb)
            def _():
                out_copy(it, nb * MOE_BLOCK).wait()

    @pl.when(j == 0)
    def _():
        @pl.when(i == 0)
        def _():
            start_weight(0, 0)
            start_weight(0, 1)
            issue_gather(0, 0)

        start_weight(i, 2)
        wait_gather(i, slot)

        @pl.when(i + 1 < n_items)
        def _():
            issue_gather(i + 1, 1 - slot)

    @pl.when(j == 1)
    def _():
        start_weight(i + 1, 0)

    @pl.when(j == 2)
    def _():
        start_weight(i + 1, 1)

        @pl.when(i > 0)
        def _():
            wait_out(i - 1)

    for ph in range(3):
        @pl.when(jnp.logical_and(j == ph, nblk > 0))
        def _():
            weight_copy(i, ph).wait()

    for nb in range(1, ITEM_BLOCKS + 1):
        m = nb * MOE_BLOCK

        @pl.when(jnp.logical_and(nblk == nb, j == 0))
        def _():
            hi, lo = _unpack_halves(xg[slot, 0:m, :])
            half = hi.shape[1]
            xb[0:m, 0:half] = hi.astype(BF16)
            xb[0:m, half:2 * half] = lo.astype(BF16)
            gs[0:m, :] = _bdot(xb[0:m, :], wg_buf[...].astype(BF16))

        @pl.when(jnp.logical_and(nblk == nb, j == 1))
        def _():
            g = gs[0:m, :]
            u = _bdot(xb[0:m, :], wu_buf[...].astype(BF16))
            ab[0:m, :] = (g * jax.nn.sigmoid(g) * u).astype(BF16)

        @pl.when(jnp.logical_and(nblk == nb, j == 2))
        def _():
            yp[0:m, :] = _pack_halves(_bdot(ab[0:m, :], wd_buf[...].astype(BF16)))
            out_copy(i, m).start()

    @pl.when(jnp.logical_and(i == n_items - 1, j == nj - 1))
    def _():
        wait_out(i)
        n_blocks = y_hbm.shape[0] // MOE_BLOCK
        yp[0:MOE_BLOCK, :] = jnp.zeros((MOE_BLOCK, yp.shape[1]), jnp.uint32)

        def tail_copy(blk):
            r0 = pl.multiple_of(blk * MOE_BLOCK, MOE_BLOCK)
            return pltpu.make_async_copy(yp.at[pl.ds(0, MOE_BLOCK)], y_hbm.at[pl.ds(r0, MOE_BLOCK)], osem.at[0])

        def start_body(blk, carry):
            tail_copy(blk).start()
            return carry

        def wait_body(blk, carry):
            tail_copy(blk).wait()
            return carry

        lax.fori_loop(used_blocks[0], n_blocks, start_body, 0)
        lax.fori_loop(used_blocks[0], n_blocks, wait_body, 0)


def _moe(hm, w_gate, w_up, w_down, item_e, item_blk0, item_nblk, row_tok, used_blocks, *, n_rows):
    d, de = w_gate.shape[1], w_gate.shape[2]
    n_items = item_e.shape[0]
    nj = 3
    rows = ITEM_BLOCKS * MOE_BLOCK
    any_spec = pl.BlockSpec(memory_space=pl.ANY)
    grid_spec = pltpu.PrefetchScalarGridSpec(
        num_scalar_prefetch=5,
        grid=(n_items, nj),
        in_specs=[any_spec, any_spec, any_spec, any_spec],
        out_specs=any_spec,
        scratch_shapes=[pltpu.VMEM((2, rows, d // 2), jnp.uint32),
                        pltpu.VMEM((rows, d), BF16),
                        pltpu.VMEM((rows, de), F32),
                        pltpu.VMEM((rows, de), BF16),
                        pltpu.VMEM((rows, d // 2), jnp.uint32),
                        pltpu.VMEM((d, de), F32),
                        pltpu.VMEM((d, de), F32),
                        pltpu.VMEM((de, d), F32),
                        pltpu.SemaphoreType.DMA((2,)),
                        pltpu.SemaphoreType.DMA((1,)),
                        pltpu.SemaphoreType.DMA((3,))],
    )
    return pl.pallas_call(
        _moe_kernel,
        grid_spec=grid_spec,
        out_shape=jax.ShapeDtypeStruct((n_rows, d // 2), jnp.uint32),
        compiler_params=_cparams(("arbitrary", "arbitrary")),
        name="moe_experts",
    )(item_e, item_blk0, item_nblk, row_tok, used_blocks, hm, w_gate, w_up, w_down)


def _combine_kernel(dest, x_ref, gt2_ref, rw_ref, y_hbm, o_ref, ybuf, sem):
    i = pl.program_id(0)
    n = pl.num_programs(0)
    tm = x_ref.shape[0]
    slot = i % 2

    def issue(it, sl):
        base = it * tm

        def body(r4, carry):
            for rr in range(GATHER_UNROLL // 2):
                r = r4 * (GATHER_UNROLL // 2) + rr
                for k in range(2):
                    pltpu.make_async_copy(y_hbm.at[pl.ds(dest[2 * (base + r) + k], 1)], ybuf.at[sl, k, pl.ds(r, 1)],
                                          sem.at[sl]).start(priority=k)
            return carry

        lax.fori_loop(0, tm // (GATHER_UNROLL // 2), body, 0)

    @pl.when(i == 0)
    def _():
        issue(0, 0)

    for k in range(2):
        pltpu.make_async_copy(y_hbm.at[pl.ds(0, tm)], ybuf.at[slot, k], sem.at[slot]).wait()

    @pl.when(i + 1 < n)
    def _():
        issue(i + 1, 1 - slot)

    w = rw_ref[...]
    hi0, lo0 = _unpack_halves(ybuf[slot, 0])
    hi1, lo1 = _unpack_halves(ybuf[slot, 1])
    half = hi0.shape[1]
    o_ref[:, 0:half] = x_ref[:, 0:half] + gt2_ref[:, 0:half] * (w[:, 0:1] * hi0 + w[:, 1:2] * hi1)
    o_ref[:, half:2 * half] = (x_ref[:, half:2 * half]
                               + gt2_ref[:, half:2 * half] * (w[:, 0:1] * lo0 + w[:, 1:2] * lo1))


def _combine(dest, xnew, gt2, rw, y, *, seq, tm):
    t, d = xnew.shape
    tiles_per_b = seq // tm
    grid_spec = pltpu.PrefetchScalarGridSpec(
        num_scalar_prefetch=1,
        grid=(t // tm,),
        in_specs=[pl.BlockSpec((tm, d), lambda i, ds: (i, 0)),
                  pl.BlockSpec((None, 1, d), lambda i, ds: (i // tiles_per_b, 0, 0)),
                  pl.BlockSpec((tm, LANES), lambda i, ds: (i, 0)),
                  pl.BlockSpec(memory_space=pl.ANY)],
        out_specs=pl.BlockSpec((tm, d), lambda i, ds: (i, 0)),
        scratch_shapes=[pltpu.VMEM((2, 2, tm, d // 2), jnp.uint32), pltpu.SemaphoreType.DMA((2,))],
    )
    return pl.pallas_call(
        _combine_kernel,
        grid_spec=grid_spec,
        out_shape=jax.ShapeDtypeStruct((t, d), F32),
        compiler_params=_cparams(("arbitrary",)),
        name="moe_combine",
    )(dest, xnew, gt2, rw, y)


def _rope_tables(n_tokens):
    rows = n_tokens // GRID_W
    row = jnp.repeat(jnp.arange(rows, dtype=jnp.int32), GRID_W).astype(F32)
    col = jnp.tile(jnp.arange(GRID_W, dtype=jnp.int32), rows).astype(F32)
    n_freq = QK_ROPE_DIM // 4
    inv = ROPE_THETA ** (-jnp.arange(n_freq, dtype=F32) / n_freq)
    ar = row[:, None] * inv[None, :]
    ac = col[:, None] * inv[None, :]
    cr, sr, cc, sc = jnp.cos(ar), jnp.sin(ar), jnp.cos(ac), jnp.sin(ac)
    cos64 = jnp.concatenate([cr, cr, cc, cc], axis=-1)
    sin64 = jnp.concatenate([-sr, sr, -sc, sc], axis=-1)
    return jnp.tile(cos64, (1, 2)), jnp.tile(sin64, (1, 2))


def _channel_dft_table():
    c = np.arange(FOURIER_GROUP_DIM).reshape(-1, 1)
    k = np.arange(FOURIER_GROUP_DIM).reshape(1, -1)
    ang = (2.0 * np.pi / FOURIER_GROUP_DIM) * ((c * k) % FOURIER_GROUP_DIM)
    return jnp.asarray(np.concatenate([np.cos(ang), -np.sin(ang)], axis=1).astype(np.float32)).astype(BF16)


def _split_heads(w, widths):
    k = w.shape[0]
    wh = w.reshape(k, MLA_HEADS, sum(widths))
    outs, off = [], 0
    for wd in widths:
        outs.append(wh[:, :, off:off + wd].reshape(k, MLA_HEADS * wd))
        off += wd
    return outs


def kernel(x, c, ctx, c_ctx, w_ada, b_ada, g_norm1, g_norm2, w_in, g_q_a, g_kv_a, w_uq, w_ukv, g_qk_q, g_qk_k,
           g_out_four, g_out_attn, w_out, w_router_group, b_router_group, w_router_expert, b_router_expert,
           w_gate, w_up, w_down):
    b, s, d = x.shape
    lc = ctx.shape[1]
    t = b * s
    layer_params = (w_ada, b_ada, g_norm1, g_norm2, w_in, g_q_a, g_kv_a, w_uq, w_ukv, g_qk_q, g_qk_k, g_out_four,
                    g_out_attn, w_out, w_router_group, b_router_group, w_router_expert, b_router_expert,
                    w_gate, w_up, w_down)
    assert all(p.shape[0] == 1 for p in layer_params), "single-layer block"
    (w_ada, b_ada, g_norm1, g_norm2, w_in, g_q_a, g_kv_a, w_uq, w_ukv, g_qk_q, g_qk_k, g_out_four,
     g_out_attn, w_out, w_router_group, b_router_group, w_router_expert, b_router_expert,
     w_gate, w_up, w_down) = [p.reshape(p.shape[1:]) for p in layer_params]

    cond8 = jnp.concatenate([c, c_ctx[None, :], jnp.zeros((8 - b - 1, d), F32)], axis=0)
    mods = _ada_mod(cond8, w_ada, b_ada)
    sh1, sc1, gt1, sh2, sc2, gt2 = [m[:b].reshape(b, 1, d) for m in jnp.split(mods, N_MOD, axis=-1)]
    csh1, csc1 = [m[b:b + 1].reshape(1, 1, d) for m in jnp.split(mods, N_MOD, axis=-1)[:2]]

    wf = w_in[:, :Q_OFF].astype(BF16)
    wq = w_in[:, Q_OFF:KV_OFF].astype(BF16)
    wkv = w_in[:, KV_OFF:ROPE_OFF].astype(BF16)
    wr = w_in[:, ROPE_OFF:].astype(BF16)
    wr2 = jnp.concatenate([wr, wr], axis=1)
    dc = _channel_dft_table()
    wqn, wqr = [w.astype(BF16) for w in _split_heads(w_uq, (QK_NOPE_DIM, QK_ROPE_DIM))]
    wkn, wv = [w.astype(BF16) for w in _split_heads(w_ukv, (QK_NOPE_DIM, V_HEAD_DIM))]
    gqn = g_qk_q[:QK_NOPE_DIM].reshape(1, -1)
    gqr2 = jnp.tile(g_qk_q[QK_NOPE_DIM:], 2).reshape(1, -1)
    gkn = g_qk_k[:QK_NOPE_DIM].reshape(1, -1)
    gkr2 = jnp.tile(g_qk_k[QK_NOPE_DIM:], 2).reshape(1, -1)
    g1 = g_norm1.reshape(1, d)
    gq = g_q_a.reshape(1, -1)
    gkv = g_kv_a.reshape(1, -1)

    x2 = x.reshape(t, d)
    u, cq, ckv, kr2 = _in_proj(x2, sh1, sc1, s, g1, gq, gkv, wf, wq, wkv, wr2, dc, with_q=True, tm=512)
    ckv_c, kr2_c = _in_proj(ctx.reshape(b * lc, d), csh1, csc1, b * lc, g1, gq, gkv, wf, wq, wkv, wr2, dc,
                            with_q=False, tm=lc)

    cos_t, sin_t = _rope_tables(s)
    q, kx, vx = _qkv(cq, ckv, kr2, cos_t, sin_t, wqn, wqr, gqn, gqr2, wkn, wv, gkn, gkr2,
                     batch=b, seq=s, tm=512, with_q=True, with_rope=True)
    kc, vc = _qkv(None, ckv_c, kr2_c, None, None, None, None, None, None, wkn, wv, gkn, gkr2,
                  batch=b, seq=lc, tm=lc, with_q=False, with_rope=False)

    attn = _attention(q, kx, kc, vx, vc, tq=1024).reshape(t, MLA_DIM)
    four = _seq_dft(u, b, s)

    wof = w_out[:FOURIER_DIM].astype(BF16)
    woa = w_out[FOURIER_DIM:].astype(BF16)
    n_route = N_GROUPS + N_EXPERTS
    wrt = jnp.concatenate([w_router_group, w_router_expert, jnp.zeros((d, LANES - n_route), F32)], axis=1)
    wrt_hi = wrt.astype(BF16)
    wrt_lo = (wrt - wrt_hi.astype(F32)).astype(BF16)
    brt = jnp.concatenate([b_router_group, b_router_expert, jnp.zeros((LANES - n_route,), F32)]).reshape(1, -1)
    xnew, hm, ri, rw, cnt = _out_router(x2, four, attn, gt1, sh2, sc2, g_out_four.reshape(1, -1),
                                        g_out_attn.reshape(1, -1), g_norm2.reshape(1, d), wof, woa, wrt_hi, wrt_lo, brt,
                                        seq=s, tm=512)

    counts = cnt[0, :N_EXPERTS].astype(jnp.int32)
    nblk_e = (counts + MOE_BLOCK - 1) // MOE_BLOCK
    blk_end = jnp.cumsum(nblk_e)
    blk_start = blk_end - nblk_e
    n_slots = t * 2
    n_blocks = -(-(n_slots + N_EXPERTS * (MOE_BLOCK - 1)) // MOE_BLOCK)
    n_rows = n_blocks * MOE_BLOCK
    e12 = ri[:, 0:2]
    seg_start = jnp.sum(jnp.where(e12[:, :, None] == jnp.arange(N_EXPERTS, dtype=jnp.int32), blk_start, 0), axis=-1)
    dest = (seg_start * MOE_BLOCK + ri[:, 2:4]).reshape(-1)
    tok_ids = jnp.repeat(jnp.arange(t, dtype=jnp.int32), 2)
    row_tok = jnp.zeros((n_rows,), jnp.int32).at[dest].set(tok_ids)
    items_e = (nblk_e + ITEM_BLOCKS - 1) // ITEM_BLOCKS
    item_end = jnp.cumsum(items_e)
    n_items = (n_blocks + (ITEM_BLOCKS - 1) * N_EXPERTS) // ITEM_BLOCKS
    idx = jnp.arange(n_items, dtype=jnp.int32)
    total = item_end[-1]
    idx_c = jnp.minimum(idx, total - 1)
    ie = jnp.minimum(jnp.sum(idx_c[:, None] >= item_end[None, :], axis=1), N_EXPERTS - 1).astype(jnp.int32)
    local = idx_c - (item_end - items_e)[ie]
    item_blk0 = (blk_start[ie] + ITEM_BLOCKS * local).astype(jnp.int32)
    item_nblk = jnp.where(idx < total, jnp.clip(nblk_e[ie] - ITEM_BLOCKS * local, 0, ITEM_BLOCKS), 0).astype(jnp.int32)

    used_blocks = blk_end[-1:].astype(jnp.int32)
    y = _moe(hm, w_gate, w_up, w_down, ie, item_blk0, item_nblk, row_tok, used_blocks, n_rows=n_rows)
    out = _combine(dest.astype(jnp.int32), xnew, gt2, rw, y, seq=s, tm=256)
    return out.reshape(b, s, d)
```

```python
import functools
import math

import numpy as np
import jax
import jax.numpy as jnp
from jax import lax
from jax.experimental import pallas as pl
from jax.experimental.pallas import tpu as pltpu

F32 = jnp.float32
BF16 = jnp.bfloat16

D_MODEL = 2048
GRID_W = 64
EPS = 1e-6
N_MOD = 6
N_FOURIER_GROUPS = 4
FOURIER_GROUP_DIM = 256
FOURIER_DIM = 1024
MLA_HEADS = 8
QK_NOPE_DIM = 128
QK_ROPE_DIM = 64
QK_HEAD_DIM = 192
V_HEAD_DIM = 128
Q_LORA_RANK = 768
KV_LORA_RANK = 512
MLA_DIM = 1024
ROPE_THETA = 10000.0
Q_OFF = FOURIER_DIM
KV_OFF = Q_OFF + Q_LORA_RANK
ROPE_OFF = KV_OFF + KV_LORA_RANK
N_GROUPS = 8
EXPERTS_PER_GROUP = 8
N_EXPERTS = 64
D_EXPERT = 768

LANES = 128
HEAD_PAD = 256
V_PAD = 256
MOE_BLOCK = 64
ITEM_BLOCKS = 8
INLINE_BLOCKS = 5
GATHER_UNROLL = 8
ATTN_KEY_CHUNK = 512
ROW_DMA_PRIORITY = 0
WEIGHT_DMA_PRIORITY = 1
VMEM_LIMIT = 56 * 1024 * 1024
NEG_BIG = -1e30


def _cparams(sem):
    return pltpu.CompilerParams(dimension_semantics=sem, vmem_limit_bytes=VMEM_LIMIT)


def _bdot(a, b):
    return jnp.dot(a, b, preferred_element_type=F32)


def _pack_halves(v):
    n = v.shape[1] // 2
    hi = pltpu.bitcast(v[:, :n].astype(BF16).astype(F32), jnp.uint32)
    lo = pltpu.bitcast(v[:, n:].astype(BF16).astype(F32), jnp.uint32)
    return hi | (lo >> 16)


def _unpack_halves(u):
    hi = pltpu.bitcast(u & jnp.uint32(0xFFFF0000), F32)
    lo = pltpu.bitcast(u << 16, F32)
    return hi, lo


def _ada_kernel(c_ref, w_ref, b_ref, o_ref):
    c = c_ref[...]
    s = (c * jax.nn.sigmoid(c)).astype(BF16)
    o_ref[...] = _bdot(s, w_ref[...].astype(BF16)) + b_ref[...]


def _ada_mod(cond8, w_ada, b_ada):
    d, n = w_ada.shape
    tn = 1024
    return pl.pallas_call(
        _ada_kernel,
        grid=(n // tn,),
        in_specs=[pl.BlockSpec((8, d), lambda i: (0, 0)),
                  pl.BlockSpec((d, tn), lambda i: (0, i)),
                  pl.BlockSpec((1, tn), lambda i: (0, i))],
        out_specs=pl.BlockSpec((8, tn), lambda i: (0, i)),
        out_shape=jax.ShapeDtypeStruct((8, n), F32),
        compiler_params=_cparams(("arbitrary",)),
        name="ada_mod",
    )(cond8, w_ada, b_ada.reshape(1, n))


def _in_proj_kernel(x_ref, sh_ref, sc_ref, g_ref, gq_ref, gkv_ref, wf_ref, wq_ref, wkv_ref, wr_ref, dc_ref,
                    *out_refs, with_q):
    x = x_ref[...]
    ms = jnp.mean(x * x, axis=-1, keepdims=True)
    a = g_ref[...] * (1.0 + sc_ref[...])
    h = (x * lax.rsqrt(ms + EPS) * a + sh_ref[...]).astype(BF16)
    if with_q:
        u_ref, cq_ref, ckv_ref, kr_ref = out_refs
        f = _bdot(h, wf_ref[...]).astype(BF16)
        dc = dc_ref[...]
        for g in range(N_FOURIER_GROUPS):
            lo = g * FOURIER_GROUP_DIM
            ug = _bdot(f[:, lo:lo + FOURIER_GROUP_DIM], dc)
            u_ref[:, lo:lo + FOURIER_GROUP_DIM] = ug[:, :FOURIER_GROUP_DIM]
            u_ref[:, FOURIER_DIM + lo:FOURIER_DIM + lo + FOURIER_GROUP_DIM] = ug[:, FOURIER_GROUP_DIM:]
        pq = _bdot(h, wq_ref[...])
        msq = jnp.mean(pq * pq, axis=-1, keepdims=True)
        cq_ref[...] = (pq * lax.rsqrt(msq + EPS) * gq_ref[...]).astype(BF16)
    else:
        ckv_ref, kr_ref = out_refs
    pkv = _bdot(h, wkv_ref[...])
    mskv = jnp.mean(pkv * pkv, axis=-1, keepdims=True)
    ckv_ref[...] = (pkv * lax.rsqrt(mskv + EPS) * gkv_ref[...]).astype(BF16)
    kr_ref[...] = _bdot(h, wr_ref[...])


def _in_proj(x2, shift, scale, rows_per_mod, g1, gq, gkv, wf, wq, wkv, wr2, dc, *, with_q, tm):
    t, d = x2.shape
    nt = t // tm
    tiles_per_mod = rows_per_mod // tm

    def const(shape):
        return pl.BlockSpec(shape, lambda i: (0,) * len(shape))

    mod_spec = pl.BlockSpec((None, 1, d), lambda i: (i // tiles_per_mod, 0, 0))
    in_specs = [pl.BlockSpec((tm, d), lambda i: (i, 0)), mod_spec, mod_spec,
                const((1, d)), const((1, Q_LORA_RANK)), const((1, KV_LORA_RANK)),
                const(wf.shape), const(wq.shape), const(wkv.shape), const(wr2.shape), const(dc.shape)]

    def rows(n):
        return pl.BlockSpec((tm, n), lambda i: (i, 0))

    out_specs = [rows(KV_LORA_RANK), rows(LANES)]
    out_shape = [jax.ShapeDtypeStruct((t, KV_LORA_RANK), BF16), jax.ShapeDtypeStruct((t, LANES), F32)]
    if with_q:
        out_specs = [rows(2 * FOURIER_DIM), rows(Q_LORA_RANK)] + out_specs
        out_shape = [jax.ShapeDtypeStruct((t, 2 * FOURIER_DIM), F32),
                     jax.ShapeDtypeStruct((t, Q_LORA_RANK), BF16)] + out_shape
    return pl.pallas_call(
        functools.partial(_in_proj_kernel, with_q=with_q),
        grid=(nt,),
        in_specs=in_specs,
        out_specs=out_specs,
        out_shape=out_shape,
        compiler_params=_cparams(("arbitrary",)),
        name="in_proj_x" if with_q else "in_proj_ctx",
    )(x2, shift, scale, g1, gq, gkv, wf, wq, wkv, wr2, dc)


def _swap_halves(y, first_half):
    return jnp.where(first_half, pltpu.roll(y, LANES - 16, 1), pltpu.roll(y, 16, 1))


def _qkv_kernel(*refs, with_q, with_rope):
    it = iter(refs)
    if with_q:
        cq_ref = next(it)
    ckv_ref = next(it)
    kr_ref = next(it)
    if with_rope:
        cos_ref = next(it)
        sin_ref = next(it)
    if with_q:
        wqn_ref = next(it)
        wqr_ref = next(it)
        gqn_ref = next(it)
        gqr_ref = next(it)
    wkn_ref = next(it)
    wv_ref = next(it)
    gkn_ref = next(it)
    gkr_ref = next(it)
    if with_q:
        q_ref = next(it)
    k_ref = next(it)
    v_ref = next(it)

    tm = ckv_ref.shape[0]
    lane = lax.broadcasted_iota(jnp.int32, (tm, LANES), 1)
    low = lane < QK_ROPE_DIM
    first_half = (lane % 32) < 16
    inv_dim = 1.0 / QK_HEAD_DIM

    def rope(y):
        if not with_rope:
            return y
        return y * cos_ref[...] + _swap_halves(y, first_half) * sin_ref[...]

    if with_q:
        cq = cq_ref[...]
        qn = _bdot(cq, wqn_ref[...])
        qr = _bdot(cq, wqr_ref[...])
        qscale = QK_HEAD_DIM ** -0.5
        for p in range(MLA_HEADS // 2):
            blk = qr[:, p * LANES:(p + 1) * LANES]
            sq = blk * blk
            ss_lo = jnp.sum(jnp.where(low, sq, 0.0), axis=-1, keepdims=True)
            ss_hi = jnp.sum(jnp.where(low, 0.0, sq), axis=-1, keepdims=True)
            scales = []
            for hh, ssr in ((2 * p, ss_lo), (2 * p + 1, ss_hi)):
                nh = qn[:, hh * LANES:(hh + 1) * LANES]
                ssq = jnp.sum(nh * nh, axis=-1, keepdims=True) + ssr
                s = lax.rsqrt(ssq * inv_dim + EPS)
                scales.append(s)
                q_ref[hh, :, 0:LANES] = (nh * s * gqn_ref[...] * qscale).astype(BF16)
            s_pair = jnp.where(low, scales[0], scales[1])
            r = rope(blk * s_pair * gqr_ref[...]) * qscale
            q_ref[2 * p, :, LANES:2 * LANES] = jnp.where(low, r, 0.0).astype(BF16)
            q_ref[2 * p + 1, :, LANES:2 * LANES] = jnp.where(low, pltpu.roll(r, QK_ROPE_DIM, 1), 0.0).astype(BF16)

    ckv = ckv_ref[...]
    kn = _bdot(ckv, wkn_ref[...])
    v = _bdot(ckv, wv_ref[...])
    kr = kr_ref[...]
    ss_r = jnp.sum(jnp.where(low, kr * kr, 0.0), axis=-1, keepdims=True)
    base = rope(kr * gkr_ref[...])
    ones_col = jnp.where(lane == 0, 1.0, 0.0).astype(BF16)
    for hh in range(MLA_HEADS):
        nh = kn[:, hh * LANES:(hh + 1) * LANES]
        ssq = jnp.sum(nh * nh, axis=-1, keepdims=True) + ss_r
        s = lax.rsqrt(ssq * inv_dim + EPS)
        k_ref[hh, :, 0:LANES] = (nh * s * gkn_ref[...]).astype(BF16)
        k_ref[hh, :, LANES:2 * LANES] = jnp.where(low, base * s, 0.0).astype(BF16)
        v_ref[hh, :, 0:LANES] = v[:, hh * LANES:(hh + 1) * LANES].astype(BF16)
        v_ref[hh, :, LANES:2 * LANES] = ones_col


def _qkv(cq, ckv, kr2, cos_t, sin_t, wqn, wqr, gqn, gqr2, wkn, wv, gkn, gkr2, *, batch, seq, tm, with_q,
         with_rope):
    t = ckv.shape[0]
    nt = t // tm
    tiles_per_b = seq // tm

    def rows(n):
        return pl.BlockSpec((tm, n), lambda i: (i, 0))

    def const(arr):
        return pl.BlockSpec(arr.shape, lambda i: (0,) * arr.ndim)

    tab_spec = pl.BlockSpec((tm, LANES), lambda i: (i % tiles_per_b, 0))

    def head_out(width):
        return pl.BlockSpec((None, MLA_HEADS, tm, width), lambda i: (i // tiles_per_b, 0, i % tiles_per_b, 0))

    args, in_specs = [], []
    if with_q:
        args.append(cq)
        in_specs.append(rows(Q_LORA_RANK))
    args += [ckv, kr2]
    in_specs += [rows(KV_LORA_RANK), rows(LANES)]
    if with_rope:
        args += [cos_t, sin_t]
        in_specs += [tab_spec, tab_spec]
    if with_q:
        args += [wqn, wqr, gqn, gqr2]
        in_specs += [const(wqn), const(wqr), const(gqn), const(gqr2)]
    args += [wkn, wv, gkn, gkr2]
    in_specs += [const(wkn), const(wv), const(gkn), const(gkr2)]

    out_specs = [head_out(HEAD_PAD), head_out(V_PAD)]
    out_shape = [jax.ShapeDtypeStruct((batch, MLA_HEADS, seq, HEAD_PAD), BF16),
                 jax.ShapeDtypeStruct((batch, MLA_HEADS, seq, V_PAD), BF16)]
    if with_q:
        out_specs = [head_out(HEAD_PAD)] + out_specs
        out_shape = [jax.ShapeDtypeStruct((batch, MLA_HEADS, seq, HEAD_PAD), BF16)] + out_shape
    return pl.pallas_call(
        functools.partial(_qkv_kernel, with_q=with_q, with_rope=with_rope),
        grid=(nt,),
        in_specs=in_specs,
        out_specs=out_specs,
        out_shape=out_shape,
        compiler_params=_cparams(("arbitrary",)),
        name="qkv_x" if with_q else "kv_ctx",
    )(*args)


def _attn_kernel(q_ref, kx_ref, kc_ref, vx_ref, vc_ref, o_ref):
    q = q_ref[...]
    tq = q.shape[0]
    dn = (((1,), (1,)), ((), ()))

    def chunk(k, v, state):
        m, acc = state
        s = lax.dot_general(q, k, dn, preferred_element_type=F32)
        m_new = jnp.maximum(m, jnp.max(s, axis=-1, keepdims=True))
        p = jnp.exp(s - m_new).astype(BF16)
        acc = jnp.exp(m - m_new) * acc + _bdot(p, v)
        return m_new, acc

    state = (jnp.full((tq, 1), NEG_BIG, F32), jnp.zeros((tq, V_PAD), F32))
    for c in range(kx_ref.shape[0] // ATTN_KEY_CHUNK):
        rows = slice(c * ATTN_KEY_CHUNK, (c + 1) * ATTN_KEY_CHUNK)
        state = chunk(kx_ref[rows, :], vx_ref[rows, :], state)
    _, acc = chunk(kc_ref[...], vc_ref[...], state)
    o_ref[...] = acc[:, :V_HEAD_DIM] / acc[:, V_HEAD_DIM:V_HEAD_DIM + 1]


def _attention(q, kx, kc, vx, vc, *, tq):
    b, h, s, _ = q.shape
    lc = kc.shape[2]
    return pl.pallas_call(
        _attn_kernel,
        grid=(b, h, s // tq),
        in_specs=[pl.BlockSpec((None, None, tq, HEAD_PAD), lambda bi, hi, qi: (bi, hi, qi, 0)),
                  pl.BlockSpec((None, None, s, HEAD_PAD), lambda bi, hi, qi: (bi, hi, 0, 0)),
                  pl.BlockSpec((None, None, lc, HEAD_PAD), lambda bi, hi, qi: (bi, hi, 0, 0)),
                  pl.BlockSpec((None, None, s, V_PAD), lambda bi, hi, qi: (bi, hi, 0, 0)),
                  pl.BlockSpec((None, None, lc, V_PAD), lambda bi, hi, qi: (bi, hi, 0, 0))],
        out_specs=pl.BlockSpec((None, tq, V_HEAD_DIM), lambda bi, hi, qi: (bi, qi, hi)),
        out_shape=jax.ShapeDtypeStruct((b, s, h * V_HEAD_DIM), F32),
        compiler_params=_cparams(("arbitrary", "arbitrary", "arbitrary")),
        name="attention",
    )(q, kx, kc, vx, vc)


def _dft_stage1_kernel(u_ref, r_ref, e_ref, o_ref):
    n = 2 * GRID_W * 8
    t = _bdot(r_ref[...].astype(BF16), e_ref[...])
    row = lax.broadcasted_iota(jnp.int32, (n, n), 0)
    col = lax.broadcasted_iota(jnp.int32, (n, n), 1)
    t = jnp.where((row % 8) == (col % 8), t, 0.0).astype(BF16)
    u = u_ref[...].reshape(GRID_W * 8, 2 * FOURIER_DIM)
    rhs = jnp.concatenate([u[:, :FOURIER_DIM], u[:, FOURIER_DIM:]], axis=0).astype(BF16)
    a = _bdot(t, rhs)
    o_ref[...] = a.reshape(2, GRID_W, 8, FOURIER_DIM)


def _dft_stage2_kernel(a_ref, t_ref, o_ref):
    rhs = a_ref[...].reshape(2 * 8 * GRID_W, FOURIER_DIM).astype(BF16)
    y = _bdot(t_ref[...].astype(BF16), rhs)
    o_ref[...] = y.reshape(GRID_W, 8, FOURIER_DIM)


def _seq_dft_tables(n_seq):
    w = GRID_W
    ch = np.arange(8).reshape(8, 1, 1, 1)
    kb = np.arange(w).reshape(1, w, 1, 1)
    j = np.arange(8).reshape(1, 1, 8, 1)
    r = np.arange(w).reshape(1, 1, 1, w)
    ang = (2.0 * np.pi / n_seq) * ((kb * (w * r + 8 * ch + j)) % n_seq)
    c, s = np.cos(ang), np.sin(ang)
    rot = np.stack([np.stack([c, s], axis=3), np.stack([-s, c], axis=3)], axis=1)
    r1 = rot.reshape(8, 2 * w * 8, 2 * w).astype(np.float32)
    expand = (np.arange(2 * w * 8)[None, :] // 8 == np.arange(2 * w)[:, None]).astype(np.float32)
    ka = np.arange(w).reshape(w, 1)
    cp = np.arange(w).reshape(1, w)
    ang2 = (2.0 * np.pi / w) * ((ka * cp) % w)
    norm = 1.0 / math.sqrt(n_seq * FOURIER_GROUP_DIM)
    cs = np.stack([np.cos(ang2), np.sin(ang2)], axis=1) * norm
    eye8 = np.eye(8)
    t2 = (cs[:, None, :, None, :] * eye8[None, :, None, :, None]).reshape(w * 8, 2 * 8 * w).astype(np.float32)
    return jnp.asarray(r1), jnp.asarray(expand).astype(BF16), jnp.asarray(t2)


def _seq_dft(u, batch, n_seq):
    w = GRID_W
    r1, expand, t2 = _seq_dft_tables(n_seq)
    u5 = u.reshape(batch, w, 8, 8, 2 * FOURIER_DIM)
    a = pl.pallas_call(
        _dft_stage1_kernel,
        grid=(batch, 8),
        in_specs=[pl.BlockSpec((None, w, None, 8, 2 * FOURIER_DIM), lambda b, c: (b, 0, c, 0, 0)),
                  pl.BlockSpec((None, 2 * w * 8, 2 * w), lambda b, c: (c, 0, 0)),
                  pl.BlockSpec((2 * w, 2 * w * 8), lambda b, c: (0, 0))],
        out_specs=pl.BlockSpec((None, 2, w, None, 8, FOURIER_DIM), lambda b, c: (b, 0, 0, c, 0, 0)),
        out_shape=jax.ShapeDtypeStruct((batch, 2, w, 8, 8, FOURIER_DIM), F32),
        compiler_params=_cparams(("arbitrary", "arbitrary")),
        name="seq_dft_stage1",
    )(u5, r1, expand)
    y = pl.pallas_call(
        _dft_stage2_kernel,
        grid=(batch, 8),
        in_specs=[pl.BlockSpec((None, 2, 8, 8, 8, FOURIER_DIM), lambda b, k: (b, 0, k, 0, 0, 0)),
                  pl.BlockSpec((w * 8, 2 * 8 * w), lambda b, k: (0, 0))],
        out_specs=pl.BlockSpec((None, w, None, 8, FOURIER_DIM), lambda b, k: (b, 0, k, 0, 0)),
        out_shape=jax.ShapeDtypeStruct((batch, w, 8, 8, FOURIER_DIM), F32),
        compiler_params=_cparams(("arbitrary", "arbitrary")),
        name="seq_dft_stage2",
    )(a, t2)
    return y.reshape(batch * n_seq, FOURIER_DIM)


def _out_router_kernel(x_ref, four_ref, attn_ref, gt1_ref, sh2_ref, sc2_ref, gf_ref, ga_ref, g2_ref,
                       wof_ref, woa_ref, wrh_ref, wrl_ref, br_ref,
                       xnew_ref, hm_ref, ri_ref, rw_ref, cnt_ref, carry_ref):
    i = pl.program_id(0)
    tm = x_ref.shape[0]

    @pl.when(i == 0)
    def _():
        carry_ref[...] = jnp.zeros_like(carry_ref)

    def norm(v, g):
        return (v * lax.rsqrt(jnp.mean(v * v, axis=-1, keepdims=True) + EPS) * g).astype(BF16)

    mix = _bdot(norm(four_ref[...], gf_ref[...]), wof_ref[...]) + _bdot(norm(attn_ref[...], ga_ref[...]), woa_ref[...])
    xn = x_ref[...] + gt1_ref[...] * mix
    xnew_ref[...] = xn
    ms = jnp.mean(xn * xn, axis=-1, keepdims=True)
    hm = xn * lax.rsqrt(ms + EPS) * (g2_ref[...] * (1.0 + sc2_ref[...])) + sh2_ref[...]
    hm_ref[...] = _pack_halves(hm)

    hm_hi = hm.astype(BF16)
    hm_lo = (hm - hm_hi.astype(F32)).astype(BF16)
    logits = _bdot(hm_hi, wrh_ref[...]) + _bdot(hm_lo, wrh_ref[...]) + _bdot(hm_hi, wrl_ref[...]) + br_ref[...]
    lane = lax.broadcasted_iota(jnp.int32, (tm, LANES), 1)
    lanef = lane.astype(F32)
    far = 1e9

    lg = jnp.where(lane < N_GROUPS, logits, NEG_BIG)
    m1 = jnp.max(lg, axis=-1, keepdims=True)
    g_p = 1.0 / jnp.sum(jnp.exp(lg - m1), axis=-1, keepdims=True)
    gidx = jnp.min(jnp.where(lg >= m1, lanef, far), axis=-1, keepdims=True)
    lo = N_GROUPS + EXPERTS_PER_GROUP * gidx
    in_group = jnp.where(lanef >= lo, jnp.where(lanef < lo + EXPERTS_PER_GROUP, 1.0, 0.0), 0.0) > 0.5
    le = jnp.where(in_group, logits, NEG_BIG)
    m2 = jnp.max(le, axis=-1, keepdims=True)
    idx1 = jnp.min(jnp.where(le >= m2, lanef, far), axis=-1, keepdims=True)
    le2 = jnp.where(lanef == idx1, NEG_BIG, le)
    m3 = jnp.max(le2, axis=-1, keepdims=True)
    idx2 = jnp.min(jnp.where(le2 >= m3, lanef, far), axis=-1, keepdims=True)
    t = jnp.exp(m3 - m2)
    p1 = 1.0 / (1.0 + t)
    p2 = t / (1.0 + t)
    e1 = idx1 - N_GROUPS
    e2 = idx2 - N_GROUPS

    oh1 = jnp.where(lanef == e1, 1.0, 0.0)
    oh2 = jnp.where(lanef == e2, 1.0, 0.0)
    ohs = oh1 + oh2
    row = lax.broadcasted_iota(jnp.int32, (tm, tm), 0)
    col = lax.broadcasted_iota(jnp.int32, (tm, tm), 1)
    tri = jnp.where(row > col, 1.0, 0.0).astype(BF16)
    before = _bdot(tri, ohs.astype(BF16)) + carry_ref[...]
    rank1 = jnp.sum(oh1 * before, axis=-1, keepdims=True)
    rank2 = jnp.sum(oh2 * before, axis=-1, keepdims=True)
    carry = carry_ref[...] + jnp.sum(ohs, axis=0, keepdims=True)
    carry_ref[...] = carry
    cnt_ref[...] = jnp.broadcast_to(carry, cnt_ref.shape)

    ri = jnp.where(lane == 0, e1, jnp.where(lane == 1, e2, jnp.where(lane == 2, rank1, jnp.where(lane == 3, rank2, 0.0))))
    ri_ref[...] = ri.astype(jnp.int32)
    rw_ref[...] = jnp.where(lane == 0, g_p * p1, jnp.where(lane == 1, g_p * p2, 0.0))


def _out_router(x2, four, attn, gt1, sh2, sc2, gf, ga, g2, wof, woa, wrh, wrl, br, *, seq, tm):
    t, d = x2.shape
    nt = t // tm
    tiles_per_b = seq // tm

    def rows(n):
        return pl.BlockSpec((tm, n), lambda i: (i, 0))

    def const(arr):
        return pl.BlockSpec(arr.shape, lambda i: (0,) * arr.ndim, pipeline_mode=pl.Buffered(1))

    mod_spec = pl.BlockSpec((None, 1, d), lambda i: (i // tiles_per_b, 0, 0))
    return pl.pallas_call(
        _out_router_kernel,
        grid=(nt,),
        in_specs=[rows(d), rows(FOURIER_DIM), rows(MLA_DIM), mod_spec, mod_spec, mod_spec,
                  const(gf), const(ga), const(g2), const(wof), const(woa), const(wrh), const(wrl), const(br)],
        out_specs=[rows(d), rows(d // 2), rows(LANES), rows(LANES), pl.BlockSpec((8, LANES), lambda i: (0, 0))],
        out_shape=[jax.ShapeDtypeStruct((t, d), F32), jax.ShapeDtypeStruct((t, d // 2), jnp.uint32),
                   jax.ShapeDtypeStruct((t, LANES), jnp.int32), jax.ShapeDtypeStruct((t, LANES), F32),
                   jax.ShapeDtypeStruct((8, LANES), F32)],
        scratch_shapes=[pltpu.VMEM((1, LANES), F32)],
        compiler_params=_cparams(("arbitrary",)),
        name="out_proj_router",
    )(x2, four, attn, gt1, sh2, sc2, gf, ga, g2, wof, woa, wrh, wrl, br)


def _moe_kernel(item_e, item_blk0, item_nblk, row_tok, used_blocks,
                hm_hbm, wg_hbm, wu_hbm, wd_hbm, y_hbm,
                xg, xb, gs, ab, yp, wg_buf, wu_buf, wd_buf, gsem, osem, wsem):
    i = pl.program_id(0)
    j = pl.program_id(1)
    n_items = pl.num_programs(0)
    nj = pl.num_programs(1)
    slot = i % 2
    nblk = item_nblk[i]
    w_hbm = (wg_hbm, wu_hbm, wd_hbm)
    w_buf = (wg_buf, wu_buf, wd_buf)

    def weight_copy(it, ph):
        return pltpu.make_async_copy(w_hbm[ph].at[item_e[it]], w_buf[ph], wsem.at[ph])

    def start_weight(it, ph):
        it_c = jnp.minimum(it, n_items - 1)

        @pl.when(jnp.logical_and(it < n_items, item_nblk[it_c] > 0))
        def _():
            weight_copy(it_c, ph).start(priority=WEIGHT_DMA_PRIORITY)

    def gather_copy(tok, sl, r):
        return pltpu.make_async_copy(hm_hbm.at[pl.ds(tok, 1)], xg.at[sl, pl.ds(r, 1)], gsem.at[sl])

    def gathered_blocks(it):
        return jnp.maximum(item_nblk[it], INLINE_BLOCKS)

    def issue_gather_loop(it, sl, first_block):
        r0 = item_blk0[it] * MOE_BLOCK
        per_block = MOE_BLOCK // GATHER_UNROLL

        def body(r8, carry):
            for k in range(GATHER_UNROLL):
                r = r8 * GATHER_UNROLL + k
                gather_copy(row_tok[r0 + r], sl, r).start(priority=ROW_DMA_PRIORITY)
            return carry

        lax.fori_loop(first_block * per_block, gathered_blocks(it) * per_block, body, 0)

    def issue_gather_inline(it, sl, lo, hi):
        r0 = item_blk0[it] * MOE_BLOCK
        for r in range(lo, hi):
            gather_copy(row_tok[r0 + r], sl, r).start(priority=ROW_DMA_PRIORITY)

    def wait_gather(it, sl):
        for b in range(ITEM_BLOCKS):
            @pl.when(b < gathered_blocks(it))
            def _():
                pltpu.make_async_copy(hm_hbm.at[pl.ds(0, MOE_BLOCK)], xg.at[sl, pl.ds(b * MOE_BLOCK, MOE_BLOCK)],
                                      gsem.at[sl]).wait()

    def out_copy(it, m):
        r0 = pl.multiple_of(item_blk0[it] * MOE_BLOCK, MOE_BLOCK)
        return pltpu.make_async_copy(yp.at[pl.ds(0, m)], y_hbm.at[pl.ds(r0, m)], osem.at[0])

    def wait_out(it):
        for nb in range(1, ITEM_BLOCKS + 1):
            @pl.when(item_nblk[it] == nb)
            def _():
                out_copy(it, nb * MOE_BLOCK).wait()

    nxt = jnp.minimum(i + 1, n_items - 1)
    inline_half = INLINE_BLOCKS * MOE_BLOCK // 2

    @pl.when(j == 0)
    def _():
        @pl.when(i == 0)
        def _():
            start_weight(0, 0)
            start_weight(0, 1)
            issue_gather_loop(0, 0, 0)

        start_weight(i, 2)

        @pl.when(jnp.logical_or(i == 0, item_nblk[jnp.maximum(i - 1, 0)] > 0))
        def _():
            wait_gather(i, slot)

    @pl.when(j == 1)
    def _():
        start_weight(i + 1, 0)

    @pl.when(j == 2)
    def _():
        start_weight(i + 1, 1)

        @pl.when(jnp.logical_and(nblk > 0, item_nblk[nxt] > INLINE_BLOCKS))
        def _():
            issue_gather_loop(nxt, 1 - slot, INLINE_BLOCKS)

        @pl.when(i > 0)
        def _():
            wait_out(i - 1)

    for ph in range(3):
        @pl.when(jnp.logical_and(j == ph, nblk > 0))
        def _():
            weight_copy(i, ph).wait()

    for nb in range(1, ITEM_BLOCKS + 1):
        m = nb * MOE_BLOCK

        @pl.when(jnp.logical_and(nblk == nb, j == 0))
        def _():
            hi, lo = _unpack_halves(xg[slot, 0:m, :])
            half = hi.shape[1]
            xb[0:m, 0:half] = hi.astype(BF16)
            xb[0:m, half:2 * half] = lo.astype(BF16)
            gs[0:m, :] = _bdot(xb[0:m, :], wg_buf[...].astype(BF16))

        @pl.when(jnp.logical_and(nblk == nb, j == 1))
        def _():
            issue_gather_inline(nxt, 1 - slot, 0, inline_half)
            g = gs[0:m, :]
            u = _bdot(xb[0:m, :], wu_buf[...].astype(BF16))
            ab[0:m, :] = (g * jax.nn.sigmoid(g) * u).astype(BF16)

        @pl.when(jnp.logical_and(nblk == nb, j == 2))
        def _():
            issue_gather_inline(nxt, 1 - slot, inline_half, 2 * inline_half)
            yp[0:m, :] = _pack_halves(_bdot(ab[0:m, :], wd_buf[...].astype(BF16)))
            out_copy(i, m).start()

    @pl.when(jnp.logical_and(i == n_items - 1, j == nj - 1))
    def _():
        wait_out(i)
        n_blocks = y_hbm.shape[0] // MOE_BLOCK
        yp[0:MOE_BLOCK, :] = jnp.zeros((MOE_BLOCK, yp.shape[1]), jnp.uint32)

        def tail_copy(blk):
            r0 = pl.multiple_of(blk * MOE_BLOCK, MOE_BLOCK)
            return pltpu.make_async_copy(yp.at[pl.ds(0, MOE_BLOCK)], y_hbm.at[pl.ds(r0, MOE_BLOCK)], osem.at[0])

        def start_body(blk, carry):
            tail_copy(blk).start()
            return carry

        def wait_body(blk, carry):
            tail_copy(blk).wait()
            return carry

        lax.fori_loop(used_blocks[0], n_blocks, start_body, 0)
        lax.fori_loop(used_blocks[0], n_blocks, wait_body, 0)


def _moe(hm, w_gate, w_up, w_down, item_e, item_blk0, item_nblk, row_tok, used_blocks, *, n_rows):
    d, de = w_gate.shape[1], w_gate.shape[2]
    n_items = item_e.shape[0]
    nj = 3
    rows = ITEM_BLOCKS * MOE_BLOCK
    any_spec = pl.BlockSpec(memory_space=pl.ANY)
    grid_spec = pltpu.PrefetchScalarGridSpec(
        num_scalar_prefetch=5,
        grid=(n_items, nj),
        in_specs=[any_spec, any_spec, any_spec, any_spec],
        out_specs=any_spec,
        scratch_shapes=[pltpu.VMEM((2, rows, d // 2), jnp.uint32),
                        pltpu.VMEM((rows, d), BF16),
                        pltpu.VMEM((rows, de), F32),
                        pltpu.VMEM((rows, de), BF16),
                        pltpu.VMEM((rows, d // 2), jnp.uint32),
                        pltpu.VMEM((d, de), F32),
                        pltpu.VMEM((d, de), F32),
                        pltpu.VMEM((de, d), F32),
                        pltpu.SemaphoreType.DMA((2,)),
                        pltpu.SemaphoreType.DMA((1,)),
                        pltpu.SemaphoreType.DMA((3,))],
    )
    return pl.pallas_call(
        _moe_kernel,
        grid_spec=grid_spec,
        out_shape=jax.ShapeDtypeStruct((n_rows, d // 2), jnp.uint32),
        compiler_params=_cparams(("arbitrary", "arbitrary")),
        name="moe_experts",
    )(item_e, item_blk0, item_nblk, row_tok, used_blocks, hm, w_gate, w_up, w_down)


def _combine_kernel(dest, x_ref, gt2_ref, rw_ref, y_hbm, o_ref, ybuf, sem):
    i = pl.program_id(0)
    n = pl.num_programs(0)
    tm = x_ref.shape[0]
    slot = i % 2

    def issue(it, sl):
        base = it * tm

        def body(r4, carry):
            for rr in range(GATHER_UNROLL // 2):
                r = r4 * (GATHER_UNROLL // 2) + rr
                for k in range(2):
                    pltpu.make_async_copy(y_hbm.at[pl.ds(dest[2 * (base + r) + k], 1)], ybuf.at[sl, k, pl.ds(r, 1)],
                                          sem.at[sl]).start(priority=k)
            return carry

        lax.fori_loop(0, tm // (GATHER_UNROLL // 2), body, 0)

    @pl.when(i == 0)
    def _():
        issue(0, 0)

    for k in range(2):
        pltpu.make_async_copy(y_hbm.at[pl.ds(0, tm)], ybuf.at[slot, k], sem.at[slot]).wait()

    @pl.when(i + 1 < n)
    def _():
        issue(i + 1, 1 - slot)

    w = rw_ref[...]
    hi0, lo0 = _unpack_halves(ybuf[slot, 0])
    hi1, lo1 = _unpack_halves(ybuf[slot, 1])
    half = hi0.shape[1]
    o_ref[:, 0:half] = x_ref[:, 0:half] + gt2_ref[:, 0:half] * (w[:, 0:1] * hi0 + w[:, 1:2] * hi1)
    o_ref[:, half:2 * half] = (x_ref[:, half:2 * half]
                               + gt2_ref[:, half:2 * half] * (w[:, 0:1] * lo0 + w[:, 1:2] * lo1))


def _combine(dest, xnew, gt2, rw, y, *, seq, tm):
    t, d = xnew.shape
    tiles_per_b = seq // tm
    grid_spec = pltpu.PrefetchScalarGridSpec(
        num_scalar_prefetch=1,
        grid=(t // tm,),
        in_specs=[pl.BlockSpec((tm, d), lambda i, ds: (i, 0)),
                  pl.BlockSpec((None, 1, d), lambda i, ds: (i // tiles_per_b, 0, 0)),
                  pl.BlockSpec((tm, LANES), lambda i, ds: (i, 0)),
                  pl.BlockSpec(memory_space=pl.ANY)],
        out_specs=pl.BlockSpec((tm, d), lambda i, ds: (i, 0)),
        scratch_shapes=[pltpu.VMEM((2, 2, tm, d // 2), jnp.uint32), pltpu.SemaphoreType.DMA((2,))],
    )
    return pl.pallas_call(
        _combine_kernel,
        grid_spec=grid_spec,
        out_shape=jax.ShapeDtypeStruct((t, d), F32),
        compiler_params=_cparams(("arbitrary",)),
        name="moe_combine",
    )(dest, xnew, gt2, rw, y)


def _rope_tables(n_tokens):
    rows = n_tokens // GRID_W
    row = jnp.repeat(jnp.arange(rows, dtype=jnp.int32), GRID_W).astype(F32)
    col = jnp.tile(jnp.arange(GRID_W, dtype=jnp.int32), rows).astype(F32)
    n_freq = QK_ROPE_DIM // 4
    inv = ROPE_THETA ** (-jnp.arange(n_freq, dtype=F32) / n_freq)
    ar = row[:, None] * inv[None, :]
    ac = col[:, None] * inv[None, :]
    cr, sr, cc, sc = jnp.cos(ar), jnp.sin(ar), jnp.cos(ac), jnp.sin(ac)
    cos64 = jnp.concatenate([cr, cr, cc, cc], axis=-1)
    sin64 = jnp.concatenate([-sr, sr, -sc, sc], axis=-1)
    return jnp.tile(cos64, (1, 2)), jnp.tile(sin64, (1, 2))


def _channel_dft_table():
    c = np.arange(FOURIER_GROUP_DIM).reshape(-1, 1)
    k = np.arange(FOURIER_GROUP_DIM).reshape(1, -1)
    ang = (2.0 * np.pi / FOURIER_GROUP_DIM) * ((c * k) % FOURIER_GROUP_DIM)
    return jnp.asarray(np.concatenate([np.cos(ang), -np.sin(ang)], axis=1).astype(np.float32)).astype(BF16)


def _split_heads(w, widths):
    k = w.shape[0]
    wh = w.reshape(k, MLA_HEADS, sum(widths))
    outs, off = [], 0
    for wd in widths:
        outs.append(wh[:, :, off:off + wd].reshape(k, MLA_HEADS * wd))
        off += wd
    return outs


def kernel(x, c, ctx, c_ctx, w_ada, b_ada, g_norm1, g_norm2, w_in, g_q_a, g_kv_a, w_uq, w_ukv, g_qk_q, g_qk_k,
           g_out_four, g_out_attn, w_out, w_router_group, b_router_group, w_router_expert, b_router_expert,
           w_gate, w_up, w_down):
    b, s, d = x.shape
    lc = ctx.shape[1]
    t = b * s
    layer_params = (w_ada, b_ada, g_norm1, g_norm2, w_in, g_q_a, g_kv_a, w_uq, w_ukv, g_qk_q, g_qk_k, g_out_four,
                    g_out_attn, w_out, w_router_group, b_router_group, w_router_expert, b_router_expert,
                    w_gate, w_up, w_down)
    assert all(p.shape[0] == 1 for p in layer_params), "single-layer block"
    (w_ada, b_ada, g_norm1, g_norm2, w_in, g_q_a, g_kv_a, w_uq, w_ukv, g_qk_q, g_qk_k, g_out_four,
     g_out_attn, w_out, w_router_group, b_router_group, w_router_expert, b_router_expert,
     w_gate, w_up, w_down) = [p.reshape(p.shape[1:]) for p in layer_params]

    cond8 = jnp.concatenate([c, c_ctx[None, :], jnp.zeros((8 - b - 1, d), F32)], axis=0)
    mods = _ada_mod(cond8, w_ada, b_ada)
    sh1, sc1, gt1, sh2, sc2, gt2 = [m[:b].reshape(b, 1, d) for m in jnp.split(mods, N_MOD, axis=-1)]
    csh1, csc1 = [m[b:b + 1].reshape(1, 1, d) for m in jnp.split(mods, N_MOD, axis=-1)[:2]]

    wf = w_in[:, :Q_OFF].astype(BF16)
    wq = w_in[:, Q_OFF:KV_OFF].astype(BF16)
    wkv = w_in[:, KV_OFF:ROPE_OFF].astype(BF16)
    wr = w_in[:, ROPE_OFF:].astype(BF16)
    wr2 = jnp.concatenate([wr, wr], axis=1)
    dc = _channel_dft_table()
    wqn, wqr = [w.astype(BF16) for w in _split_heads(w_uq, (QK_NOPE_DIM, QK_ROPE_DIM))]
    wkn, wv = [w.astype(BF16) for w in _split_heads(w_ukv, (QK_NOPE_DIM, V_HEAD_DIM))]
    gqn = g_qk_q[:QK_NOPE_DIM].reshape(1, -1)
    gqr2 = jnp.tile(g_qk_q[QK_NOPE_DIM:], 2).reshape(1, -1)
    gkn = g_qk_k[:QK_NOPE_DIM].reshape(1, -1)
    gkr2 = jnp.tile(g_qk_k[QK_NOPE_DIM:], 2).reshape(1, -1)
    g1 = g_norm1.reshape(1, d)
    gq = g_q_a.reshape(1, -1)
    gkv = g_kv_a.reshape(1, -1)

    x2 = x.reshape(t, d)
    u, cq, ckv, kr2 = _in_proj(x2, sh1, sc1, s, g1, gq, gkv, wf, wq, wkv, wr2, dc, with_q=True, tm=512)
    ckv_c, kr2_c = _in_proj(ctx.reshape(b * lc, d), csh1, csc1, b * lc, g1, gq, gkv, wf, wq, wkv, wr2, dc,
                            with_q=False, tm=lc)

    cos_t, sin_t = _rope_tables(s)
    q, kx, vx = _qkv(cq, ckv, kr2, cos_t, sin_t, wqn, wqr, gqn, gqr2, wkn, wv, gkn, gkr2,
                     batch=b, seq=s, tm=512, with_q=True, with_rope=True)
    kc, vc = _qkv(None, ckv_c, kr2_c, None, None, None, None, None, None, wkn, wv, gkn, gkr2,
                  batch=b, seq=lc, tm=lc, with_q=False, with_rope=False)

    attn = _attention(q, kx, kc, vx, vc, tq=1024).reshape(t, MLA_DIM)
    four = _seq_dft(u, b, s)

    wof = w_out[:FOURIER_DIM].astype(BF16)
    woa = w_out[FOURIER_DIM:].astype(BF16)
    n_route = N_GROUPS + N_EXPERTS
    wrt = jnp.concatenate([w_router_group, w_router_expert, jnp.zeros((d, LANES - n_route), F32)], axis=1)
    wrt_hi = wrt.astype(BF16)
    wrt_lo = (wrt - wrt_hi.astype(F32)).astype(BF16)
    brt = jnp.concatenate([b_router_group, b_router_expert, jnp.zeros((LANES - n_route,), F32)]).reshape(1, -1)
    xnew, hm, ri, rw, cnt = _out_router(x2, four, attn, gt1, sh2, sc2, g_out_four.reshape(1, -1),
                                        g_out_attn.reshape(1, -1), g_norm2.reshape(1, d), wof, woa, wrt_hi, wrt_lo, brt,
                                        seq=s, tm=512)

    counts = cnt[0, :N_EXPERTS].astype(jnp.int32)
    nblk_e = (counts + MOE_BLOCK - 1) // MOE_BLOCK
    blk_end = jnp.cumsum(nblk_e)
    blk_start = blk_end - nblk_e
    n_slots = t * 2
    n_blocks = -(-(n_slots + N_EXPERTS * (MOE_BLOCK - 1)) // MOE_BLOCK)
    n_rows = n_blocks * MOE_BLOCK
    e12 = ri[:, 0:2]
    seg_start = jnp.sum(jnp.where(e12[:, :, None] == jnp.arange(N_EXPERTS, dtype=jnp.int32), blk_start, 0), axis=-1)
    dest = (seg_start * MOE_BLOCK + ri[:, 2:4]).reshape(-1)
    tok_ids = jnp.repeat(jnp.arange(t, dtype=jnp.int32), 2)
    row_tok = jnp.zeros((n_rows + ITEM_BLOCKS * MOE_BLOCK,), jnp.int32).at[dest].set(tok_ids)
    items_e = (nblk_e + ITEM_BLOCKS - 1) // ITEM_BLOCKS
    item_end = jnp.cumsum(items_e)
    n_items = (n_blocks + (ITEM_BLOCKS - 1) * N_EXPERTS) // ITEM_BLOCKS + 1
    idx = jnp.arange(n_items, dtype=jnp.int32)
    total = item_end[-1]
    idx_c = jnp.minimum(idx, total - 1)
    ie = jnp.minimum(jnp.sum(idx_c[:, None] >= item_end[None, :], axis=1), N_EXPERTS - 1).astype(jnp.int32)
    local = idx_c - (item_end - items_e)[ie]
    item_blk0 = (blk_start[ie] + ITEM_BLOCKS * local).astype(jnp.int32)
    item_nblk = jnp.where(idx < total, jnp.clip(nblk_e[ie] - ITEM_BLOCKS * local, 0, ITEM_BLOCKS), 0).astype(jnp.int32)

    used_blocks = blk_end[-1:].astype(jnp.int32)
    y = _moe(hm, w_gate, w_up, w_down, ie, item_blk0, item_nblk, row_tok, used_blocks, n_rows=n_rows)
    out = _combine(dest.astype(jnp.int32), xnew, gt2, rw, y, seq=s, tm=256)
    return out.reshape(b, s, d)
```

```python
import functools
import math

import numpy as np
import jax
import jax.numpy as jnp
from jax import lax
from jax.experimental import pallas as pl
from jax.experimental.pallas import tpu as pltpu

F32 = jnp.float32
BF16 = jnp.bfloat16

D_MODEL = 2048
GRID_W = 64
EPS = 1e-6
N_MOD = 6
N_FOURIER_GROUPS = 4
FOURIER_GROUP_DIM = 256
FOURIER_DIM = 1024
MLA_HEADS = 8
QK_NOPE_DIM = 128
QK_ROPE_DIM = 64
QK_HEAD_DIM = 192
V_HEAD_DIM = 128
Q_LORA_RANK = 768
KV_LORA_RANK = 512
MLA_DIM = 1024
ROPE_THETA = 10000.0
Q_OFF = FOURIER_DIM
KV_OFF = Q_OFF + Q_LORA_RANK
ROPE_OFF = KV_OFF + KV_LORA_RANK
N_GROUPS = 8
EXPERTS_PER_GROUP = 8
N_EXPERTS = 64
D_EXPERT = 768

LANES = 128
HEAD_PAD = 256
V_PAD = 256
MOE_BLOCK = 64
ITEM_BLOCKS = 8
GATHER_UNROLL = 8
GU_SLOTS = 3
DN_SLOTS = 2
ATTN_KEY_CHUNK = 512
ROW_DMA_PRIORITY = 0
WEIGHT_DMA_PRIORITY = 1
VMEM_LIMIT = 56 * 1024 * 1024
NEG_BIG = -1e30


def _cparams(sem):
    return pltpu.CompilerParams(dimension_semantics=sem, vmem_limit_bytes=VMEM_LIMIT)


def _bdot(a, b):
    return jnp.dot(a, b, preferred_element_type=F32)


def _pack_halves(v):
    n = v.shape[1] // 2
    hi = pltpu.bitcast(v[:, :n].astype(BF16).astype(F32), jnp.uint32)
    lo = pltpu.bitcast(v[:, n:].astype(BF16).astype(F32), jnp.uint32)
    return hi | (lo >> 16)


def _unpack_halves(u):
    hi = pltpu.bitcast(u & jnp.uint32(0xFFFF0000), F32)
    lo = pltpu.bitcast(u << 16, F32)
    return hi, lo


def _ada_kernel(c_ref, w_ref, b_ref, o_ref):
    c = c_ref[...]
    s = (c * jax.nn.sigmoid(c)).astype(BF16)
    o_ref[...] = _bdot(s, w_ref[...].astype(BF16)) + b_ref[...]


def _ada_mod(cond8, w_ada, b_ada):
    d, n = w_ada.shape
    tn = 1024
    return pl.pallas_call(
        _ada_kernel,
        grid=(n // tn,),
        in_specs=[pl.BlockSpec((8, d), lambda i: (0, 0)),
                  pl.BlockSpec((d, tn), lambda i: (0, i)),
                  pl.BlockSpec((1, tn), lambda i: (0, i))],
        out_specs=pl.BlockSpec((8, tn), lambda i: (0, i)),
        out_shape=jax.ShapeDtypeStruct((8, n), F32),
        compiler_params=_cparams(("arbitrary",)),
        name="ada_mod",
    )(cond8, w_ada, b_ada.reshape(1, n))


def _in_proj_kernel(x_ref, sh_ref, sc_ref, g_ref, gq_ref, gkv_ref, w_ref, dc_ref, *out_refs, with_q, mod_row):
    row = pl.ds(mod_row(pl.program_id(0)), 1)
    x = x_ref[...]
    ms = jnp.mean(x * x, axis=-1, keepdims=True)
    a = g_ref[...] * (1.0 + sc_ref[row, :])
    h = (x * lax.rsqrt(ms + EPS) * a + sh_ref[row, :]).astype(BF16)
    if with_q:
        u_ref, cq_ref, ckv_ref, kr_ref = out_refs
        f = _bdot(h, w_ref[:, 0:Q_OFF]).astype(BF16)
        dc = dc_ref[...]
        for g in range(N_FOURIER_GROUPS):
            lo = g * FOURIER_GROUP_DIM
            ug = _bdot(f[:, lo:lo + FOURIER_GROUP_DIM], dc)
            u_ref[:, lo:lo + FOURIER_GROUP_DIM] = ug[:, :FOURIER_GROUP_DIM]
            u_ref[:, FOURIER_DIM + lo:FOURIER_DIM + lo + FOURIER_GROUP_DIM] = ug[:, FOURIER_GROUP_DIM:]
        pq = _bdot(h, w_ref[:, Q_OFF:KV_OFF])
        msq = jnp.mean(pq * pq, axis=-1, keepdims=True)
        cq_ref[...] = (pq * lax.rsqrt(msq + EPS) * gq_ref[...]).astype(BF16)
    else:
        ckv_ref, kr_ref = out_refs
    pkv = _bdot(h, w_ref[:, KV_OFF:ROPE_OFF])
    mskv = jnp.mean(pkv * pkv, axis=-1, keepdims=True)
    ckv_ref[...] = (pkv * lax.rsqrt(mskv + EPS) * gkv_ref[...]).astype(BF16)
    kr_ref[...] = _bdot(h, w_ref[:, ROPE_OFF:ROPE_OFF + LANES])


def _mod_spec(mods, k, d):
    return pl.BlockSpec((mods.shape[0], d), lambda i, *_: (0, k))


def _in_proj(x2, mods, first_row, rows_per_mod, g1, gq, gkv, w_all, dc, *, with_q, tm):
    t, d = x2.shape
    nt = t // tm
    tiles_per_mod = rows_per_mod // tm

    def const(shape):
        return pl.BlockSpec(shape, lambda i: (0,) * len(shape))

    in_specs = [pl.BlockSpec((tm, d), lambda i: (i, 0)), _mod_spec(mods, 0, d), _mod_spec(mods, 1, d),
                const((1, d)), const((1, Q_LORA_RANK)), const((1, KV_LORA_RANK)),
                const(w_all.shape), const(dc.shape)]

    def rows(n):
        return pl.BlockSpec((tm, n), lambda i: (i, 0))

    out_specs = [rows(KV_LORA_RANK), rows(LANES)]
    out_shape = [jax.ShapeDtypeStruct((t, KV_LORA_RANK), BF16), jax.ShapeDtypeStruct((t, LANES), F32)]
    if with_q:
        out_specs = [rows(2 * FOURIER_DIM), rows(Q_LORA_RANK)] + out_specs
        out_shape = [jax.ShapeDtypeStruct((t, 2 * FOURIER_DIM), F32),
                     jax.ShapeDtypeStruct((t, Q_LORA_RANK), BF16)] + out_shape
    return pl.pallas_call(
        functools.partial(_in_proj_kernel, with_q=with_q, mod_row=lambda i: first_row + i // tiles_per_mod),
        grid=(nt,),
        in_specs=in_specs,
        out_specs=out_specs,
        out_shape=out_shape,
        compiler_params=_cparams(("arbitrary",)),
        name="in_proj_x" if with_q else "in_proj_ctx",
    )(x2, mods, mods, g1, gq, gkv, w_all, dc)


def _swap_halves(y, first_half):
    return jnp.where(first_half, pltpu.roll(y, LANES - 16, 1), pltpu.roll(y, 16, 1))


def _qkv_kernel(*refs, with_q, with_rope):
    it = iter(refs)
    if with_q:
        cq_ref = next(it)
    ckv_ref = next(it)
    kr_ref = next(it)
    if with_rope:
        cos_ref = next(it)
        sin_ref = next(it)
    if with_q:
        wqn_ref = next(it)
        wqr_ref = next(it)
        gqn_ref = next(it)
        gqr_ref = next(it)
    wkn_ref = next(it)
    wv_ref = next(it)
    gkn_ref = next(it)
    gkr_ref = next(it)
    if with_q:
        q_ref = next(it)
    k_ref = next(it)
    v_ref = next(it)

    tm = ckv_ref.shape[0]
    lane = lax.broadcasted_iota(jnp.int32, (tm, LANES), 1)
    low = lane < QK_ROPE_DIM
    first_half = (lane % 32) < 16
    inv_dim = 1.0 / QK_HEAD_DIM

    def rope(y):
        if not with_rope:
            return y
        return y * cos_ref[...] + _swap_halves(y, first_half) * sin_ref[...]

    if with_q:
        cq = cq_ref[...]
        qn = _bdot(cq, wqn_ref[...])
        qr = _bdot(cq, wqr_ref[...])
        qscale = QK_HEAD_DIM ** -0.5
        for p in range(MLA_HEADS // 2):
            blk = qr[:, p * LANES:(p + 1) * LANES]
            sq = blk * blk
            ss_lo = jnp.sum(jnp.where(low, sq, 0.0), axis=-1, keepdims=True)
            ss_hi = jnp.sum(jnp.where(low, 0.0, sq), axis=-1, keepdims=True)
            scales = []
            for hh, ssr in ((2 * p, ss_lo), (2 * p + 1, ss_hi)):
                nh = qn[:, hh * LANES:(hh + 1) * LANES]
                ssq = jnp.sum(nh * nh, axis=-1, keepdims=True) + ssr
                s = lax.rsqrt(ssq * inv_dim + EPS)
                scales.append(s)
                q_ref[hh, :, 0:LANES] = (nh * s * gqn_ref[...] * qscale).astype(BF16)
            s_pair = jnp.where(low, scales[0], scales[1])
            r = rope(blk * s_pair * gqr_ref[...]) * qscale
            q_ref[2 * p, :, LANES:2 * LANES] = jnp.where(low, r, 0.0).astype(BF16)
            q_ref[2 * p + 1, :, LANES:2 * LANES] = jnp.where(low, pltpu.roll(r, QK_ROPE_DIM, 1), 0.0).astype(BF16)

    ckv = ckv_ref[...]
    kn = _bdot(ckv, wkn_ref[...])
    v = _bdot(ckv, wv_ref[...])
    kr = kr_ref[...]
    ss_r = jnp.sum(jnp.where(low, kr * kr, 0.0), axis=-1, keepdims=True)
    base = rope(kr * gkr_ref[...])
    ones_col = jnp.where(lane == 0, 1.0, 0.0).astype(BF16)
    for hh in range(MLA_HEADS):
        nh = kn[:, hh * LANES:(hh + 1) * LANES]
        ssq = jnp.sum(nh * nh, axis=-1, keepdims=True) + ss_r
        s = lax.rsqrt(ssq * inv_dim + EPS)
        k_ref[hh, :, 0:LANES] = (nh * s * gkn_ref[...]).astype(BF16)
        k_ref[hh, :, LANES:2 * LANES] = jnp.where(low, base * s, 0.0).astype(BF16)
        v_ref[hh, :, 0:LANES] = v[:, hh * LANES:(hh + 1) * LANES].astype(BF16)
        v_ref[hh, :, LANES:2 * LANES] = ones_col


def _qkv(cq, ckv, kr2, cos_t, sin_t, wqn, wqr, gqn, gqr2, wkn, wv, gkn, gkr2, *, batch, seq, tm, with_q,
         with_rope):
    t = ckv.shape[0]
    nt = t // tm
    tiles_per_b = seq // tm

    def rows(n):
        return pl.BlockSpec((tm, n), lambda i: (i, 0))

    def const(arr):
        return pl.BlockSpec(arr.shape, lambda i: (0,) * arr.ndim)

    tab_spec = pl.BlockSpec((tm, LANES), lambda i: (i % tiles_per_b, 0))

    def head_out(width):
        return pl.BlockSpec((None, MLA_HEADS, tm, width), lambda i: (i // tiles_per_b, 0, i % tiles_per_b, 0))

    args, in_specs = [], []
    if with_q:
        args.append(cq)
        in_specs.append(rows(Q_LORA_RANK))
    args += [ckv, kr2]
    in_specs += [rows(KV_LORA_RANK), rows(LANES)]
    if with_rope:
        args += [cos_t, sin_t]
        in_specs += [tab_spec, tab_spec]
    if with_q:
        args += [wqn, wqr, gqn, gqr2]
        in_specs += [const(wqn), const(wqr), const(gqn), const(gqr2)]
    args += [wkn, wv, gkn, gkr2]
    in_specs += [const(wkn), const(wv), const(gkn), const(gkr2)]

    out_specs = [head_out(HEAD_PAD), head_out(V_PAD)]
    out_shape = [jax.ShapeDtypeStruct((batch, MLA_HEADS, seq, HEAD_PAD), BF16),
                 jax.ShapeDtypeStruct((batch, MLA_HEADS, seq, V_PAD), BF16)]
    if with_q:
        out_specs = [head_out(HEAD_PAD)] + out_specs
        out_shape = [jax.ShapeDtypeStruct((batch, MLA_HEADS, seq, HEAD_PAD), BF16)] + out_shape
    return pl.pallas_call(
        functools.partial(_qkv_kernel, with_q=with_q, with_rope=with_rope),
        grid=(nt,),
        in_specs=in_specs,
        out_specs=out_specs,
        out_shape=out_shape,
        compiler_params=_cparams(("arbitrary",)),
        name="qkv_x" if with_q else "kv_ctx",
    )(*args)


def _attn_kernel(q_ref, kx_ref, kc_ref, vx_ref, vc_ref, o_ref):
    q = q_ref[...]
    tq = q.shape[0]
    dn = (((1,), (1,)), ((), ()))

    def chunk(k, v, state):
        m, acc = state
        s = lax.dot_general(q, k, dn, preferred_element_type=F32)
        m_new = jnp.maximum(m, jnp.max(s, axis=-1, keepdims=True))
        p = jnp.exp(s - m_new).astype(BF16)
        acc = jnp.exp(m - m_new) * acc + _bdot(p, v)
        return m_new, acc

    state = (jnp.full((tq, 1), NEG_BIG, F32), jnp.zeros((tq, V_PAD), F32))
    for c in range(kx_ref.shape[0] // ATTN_KEY_CHUNK):
        rows = slice(c * ATTN_KEY_CHUNK, (c + 1) * ATTN_KEY_CHUNK)
        state = chunk(kx_ref[rows, :], vx_ref[rows, :], state)
    _, acc = chunk(kc_ref[...], vc_ref[...], state)
    o_ref[...] = acc[:, :V_HEAD_DIM] / acc[:, V_HEAD_DIM:V_HEAD_DIM + 1]


def _attention(q, kx, kc, vx, vc, *, tq):
    b, h, s, _ = q.shape
    lc = kc.shape[2]
    return pl.pallas_call(
        _attn_kernel,
        grid=(b, h, s // tq),
        in_specs=[pl.BlockSpec((None, None, tq, HEAD_PAD), lambda bi, hi, qi: (bi, hi, qi, 0)),
                  pl.BlockSpec((None, None, s, HEAD_PAD), lambda bi, hi, qi: (bi, hi, 0, 0)),
                  pl.BlockSpec((None, None, lc, HEAD_PAD), lambda bi, hi, qi: (bi, hi, 0, 0)),
                  pl.BlockSpec((None, None, s, V_PAD), lambda bi, hi, qi: (bi, hi, 0, 0)),
                  pl.BlockSpec((None, None, lc, V_PAD), lambda bi, hi, qi: (bi, hi, 0, 0))],
        out_specs=pl.BlockSpec((None, tq, V_HEAD_DIM), lambda bi, hi, qi: (bi, qi, hi)),
        out_shape=jax.ShapeDtypeStruct((b, s, h * V_HEAD_DIM), F32),
        compiler_params=_cparams(("arbitrary", "arbitrary", "arbitrary")),
        name="attention",
    )(q, kx, kc, vx, vc)


def _dft_stage1_kernel(u_ref, r_ref, e_ref, o_ref):
    n = 2 * GRID_W * 8
    t = _bdot(r_ref[...].astype(BF16), e_ref[...])
    row = lax.broadcasted_iota(jnp.int32, (n, n), 0)
    col = lax.broadcasted_iota(jnp.int32, (n, n), 1)
    t = jnp.where((row % 8) == (col % 8), t, 0.0).astype(BF16)
    u = u_ref[...].reshape(GRID_W * 8, 2 * FOURIER_DIM)
    rhs = jnp.concatenate([u[:, :FOURIER_DIM], u[:, FOURIER_DIM:]], axis=0).astype(BF16)
    a = _bdot(t, rhs)
    o_ref[...] = a.reshape(2, GRID_W, 8, FOURIER_DIM)


def _dft_stage2_kernel(a_ref, t_ref, o_ref):
    rhs = a_ref[...].reshape(2 * 8 * GRID_W, FOURIER_DIM).astype(BF16)
    y = _bdot(t_ref[...].astype(BF16), rhs)
    o_ref[...] = y.reshape(GRID_W, 8, FOURIER_DIM)


def _seq_dft_tables(n_seq):
    w = GRID_W
    ch = np.arange(8).reshape(8, 1, 1, 1)
    kb = np.arange(w).reshape(1, w, 1, 1)
    j = np.arange(8).reshape(1, 1, 8, 1)
    r = np.arange(w).reshape(1, 1, 1, w)
    ang = (2.0 * np.pi / n_seq) * ((kb * (w * r + 8 * ch + j)) % n_seq)
    c, s = np.cos(ang), np.sin(ang)
    rot = np.stack([np.stack([c, s], axis=3), np.stack([-s, c], axis=3)], axis=1)
    r1 = rot.reshape(8, 2 * w * 8, 2 * w).astype(np.float32)
    expand = (np.arange(2 * w * 8)[None, :] // 8 == np.arange(2 * w)[:, None]).astype(np.float32)
    ka = np.arange(w).reshape(w, 1)
    cp = np.arange(w).reshape(1, w)
    ang2 = (2.0 * np.pi / w) * ((ka * cp) % w)
    norm = 1.0 / math.sqrt(n_seq * FOURIER_GROUP_DIM)
    cs = np.stack([np.cos(ang2), np.sin(ang2)], axis=1) * norm
    eye8 = np.eye(8)
    t2 = (cs[:, None, :, None, :] * eye8[None, :, None, :, None]).reshape(w * 8, 2 * 8 * w).astype(np.float32)
    return jnp.asarray(r1), jnp.asarray(expand).astype(BF16), jnp.asarray(t2)


def _seq_dft(u, batch, n_seq):
    w = GRID_W
    r1, expand, t2 = _seq_dft_tables(n_seq)
    u5 = u.reshape(batch, w, 8, 8, 2 * FOURIER_DIM)
    a = pl.pallas_call(
        _dft_stage1_kernel,
        grid=(batch, 8),
        in_specs=[pl.BlockSpec((None, w, None, 8, 2 * FOURIER_DIM), lambda b, c: (b, 0, c, 0, 0)),
                  pl.BlockSpec((None, 2 * w * 8, 2 * w), lambda b, c: (c, 0, 0)),
                  pl.BlockSpec((2 * w, 2 * w * 8), lambda b, c: (0, 0))],
        out_specs=pl.BlockSpec((None, 2, w, None, 8, FOURIER_DIM), lambda b, c: (b, 0, 0, c, 0, 0)),
        out_shape=jax.ShapeDtypeStruct((batch, 2, w, 8, 8, FOURIER_DIM), F32),
        compiler_params=_cparams(("arbitrary", "arbitrary")),
        name="seq_dft_stage1",
    )(u5, r1, expand)
    y = pl.pallas_call(
        _dft_stage2_kernel,
        grid=(batch, 8),
        in_specs=[pl.BlockSpec((None, 2, 8, 8, 8, FOURIER_DIM), lambda b, k: (b, 0, k, 0, 0, 0)),
                  pl.BlockSpec((w * 8, 2 * 8 * w), lambda b, k: (0, 0))],
        out_specs=pl.BlockSpec((None, w, None, 8, FOURIER_DIM), lambda b, k: (b, 0, k, 0, 0)),
        out_shape=jax.ShapeDtypeStruct((batch, w, 8, 8, FOURIER_DIM), F32),
        compiler_params=_cparams(("arbitrary", "arbitrary")),
        name="seq_dft_stage2",
    )(a, t2)
    return y.reshape(batch * n_seq, FOURIER_DIM)


def _out_router_kernel(x_ref, four_ref, attn_ref, gt1_ref, sh2_ref, sc2_ref, gf_ref, ga_ref, g2_ref,
                       wo_ref, wrh_ref, wrl_ref, br_ref,
                       xnew_ref, hm_ref, ri_ref, rw_ref, cnt_ref, carry_ref, *, mod_row):
    i = pl.program_id(0)
    tm = x_ref.shape[0]
    row = pl.ds(mod_row(i), 1)

    @pl.when(i == 0)
    def _():
        carry_ref[...] = jnp.zeros_like(carry_ref)

    def norm(v, g):
        return (v * lax.rsqrt(jnp.mean(v * v, axis=-1, keepdims=True) + EPS) * g).astype(BF16)

    mix = (_bdot(norm(four_ref[...], gf_ref[...]), wo_ref[0:FOURIER_DIM, :])
           + _bdot(norm(attn_ref[...], ga_ref[...]), wo_ref[FOURIER_DIM:FOURIER_DIM + MLA_DIM, :]))
    xn = x_ref[...] + gt1_ref[row, :] * mix
    xnew_ref[...] = xn
    ms = jnp.mean(xn * xn, axis=-1, keepdims=True)
    hm = xn * lax.rsqrt(ms + EPS) * (g2_ref[...] * (1.0 + sc2_ref[row, :])) + sh2_ref[row, :]
    hm_ref[...] = _pack_halves(hm)

    hm_hi = hm.astype(BF16)
    hm_lo = (hm - hm_hi.astype(F32)).astype(BF16)
    logits = _bdot(hm_hi, wrh_ref[...]) + _bdot(hm_lo, wrh_ref[...]) + _bdot(hm_hi, wrl_ref[...]) + br_ref[...]
    lane = lax.broadcasted_iota(jnp.int32, (tm, LANES), 1)
    lanef = lane.astype(F32)
    far = 1e9

    lg = jnp.where(lane < N_GROUPS, logits, NEG_BIG)
    m1 = jnp.max(lg, axis=-1, keepdims=True)
    g_p = 1.0 / jnp.sum(jnp.exp(lg - m1), axis=-1, keepdims=True)
    gidx = jnp.min(jnp.where(lg >= m1, lanef, far), axis=-1, keepdims=True)
    lo = N_GROUPS + EXPERTS_PER_GROUP * gidx
    in_group = jnp.where(lanef >= lo, jnp.where(lanef < lo + EXPERTS_PER_GROUP, 1.0, 0.0), 0.0) > 0.5
    le = jnp.where(in_group, logits, NEG_BIG)
    m2 = jnp.max(le, axis=-1, keepdims=True)
    idx1 = jnp.min(jnp.where(le >= m2, lanef, far), axis=-1, keepdims=True)
    le2 = jnp.where(lanef == idx1, NEG_BIG, le)
    m3 = jnp.max(le2, axis=-1, keepdims=True)
    idx2 = jnp.min(jnp.where(le2 >= m3, lanef, far), axis=-1, keepdims=True)
    t = jnp.exp(m3 - m2)
    p1 = 1.0 / (1.0 + t)
    p2 = t / (1.0 + t)
    e1 = idx1 - N_GROUPS
    e2 = idx2 - N_GROUPS

    oh1 = jnp.where(lanef == e1, 1.0, 0.0)
    oh2 = jnp.where(lanef == e2, 1.0, 0.0)
    ohs = oh1 + oh2
    row = lax.broadcasted_iota(jnp.int32, (tm, tm), 0)
    col = lax.broadcasted_iota(jnp.int32, (tm, tm), 1)
    tri = jnp.where(row > col, 1.0, 0.0).astype(BF16)
    before = _bdot(tri, ohs.astype(BF16)) + carry_ref[...]
    rank1 = jnp.sum(oh1 * before, axis=-1, keepdims=True)
    rank2 = jnp.sum(oh2 * before, axis=-1, keepdims=True)
    carry = carry_ref[...] + jnp.sum(ohs, axis=0, keepdims=True)
    carry_ref[...] = carry
    cnt_ref[...] = jnp.broadcast_to(carry, cnt_ref.shape)

    ri = jnp.where(lane == 0, e1, jnp.where(lane == 1, e2, jnp.where(lane == 2, rank1, jnp.where(lane == 3, rank2, 0.0))))
    ri_ref[...] = ri.astype(jnp.int32)
    rw_ref[...] = jnp.where(lane == 0, g_p * p1, jnp.where(lane == 1, g_p * p2, 0.0))


def _out_router(x2, four, attn, mods, gf, ga, g2, wo, wrh, wrl, br, *, seq, tm):
    t, d = x2.shape
    nt = t // tm
    tiles_per_b = seq // tm

    def rows(n):
        return pl.BlockSpec((tm, n), lambda i: (i, 0))

    def const(arr):
        return pl.BlockSpec(arr.shape, lambda i: (0,) * arr.ndim, pipeline_mode=pl.Buffered(1))

    return pl.pallas_call(
        functools.partial(_out_router_kernel, mod_row=lambda i: i // tiles_per_b),
        grid=(nt,),
        in_specs=[rows(d), rows(FOURIER_DIM), rows(MLA_DIM), _mod_spec(mods, 2, d), _mod_spec(mods, 3, d),
                  _mod_spec(mods, 4, d),
                  const(gf), const(ga), const(g2), const(wo), const(wrh), const(wrl), const(br)],
        out_specs=[rows(d), rows(d // 2), rows(LANES), rows(LANES), pl.BlockSpec((8, LANES), lambda i: (0, 0))],
        out_shape=[jax.ShapeDtypeStruct((t, d), F32), jax.ShapeDtypeStruct((t, d // 2), jnp.uint32),
                   jax.ShapeDtypeStruct((t, LANES), jnp.int32), jax.ShapeDtypeStruct((t, LANES), F32),
                   jax.ShapeDtypeStruct((8, LANES), F32)],
        scratch_shapes=[pltpu.VMEM((1, LANES), F32)],
        compiler_params=_cparams(("arbitrary",)),
        name="out_proj_router",
    )(x2, four, attn, mods, mods, mods, gf, ga, g2, wo, wrh, wrl, br)


def _moe_kernel(item_e, item_blk0, item_nblk, row_tok, used_blocks,
                hm_hbm, wg_hbm, wu_hbm, wd_hbm, y_hbm,
                xg, xb, gs, ab, yp, gu_buf, dn_buf, gsem, osem, gusem, dnsem):
    i = pl.program_id(0)
    j = pl.program_id(1)
    n_items = pl.num_programs(0)
    nj = pl.num_programs(1)
    slot = i % 2
    nblk = item_nblk[i]

    def weight_copy(it, ph):
        if ph == 2:
            ws = it % DN_SLOTS
            return pltpu.make_async_copy(wd_hbm.at[item_e[it]], dn_buf.at[ws], dnsem.at[ws])
        ws = (2 * it + ph) % GU_SLOTS
        return pltpu.make_async_copy((wg_hbm, wu_hbm)[ph].at[item_e[it]], gu_buf.at[ws], gusem.at[ws])

    def start_weight(it, ph):
        it_c = jnp.minimum(it, n_items - 1)

        @pl.when(jnp.logical_and(it < n_items, item_nblk[it_c] > 0))
        def _():
            weight_copy(it_c, ph).start(priority=WEIGHT_DMA_PRIORITY)

    def gather_copy(tok, sl, r):
        return pltpu.make_async_copy(hm_hbm.at[pl.ds(tok, 1)], xg.at[sl, pl.ds(r, 1)], gsem.at[sl])

    def issue_gather(it, sl):
        r0 = item_blk0[it] * MOE_BLOCK

        def body(r8, carry):
            for k in range(GATHER_UNROLL):
                r = r8 * GATHER_UNROLL + k
                gather_copy(row_tok[r0 + r], sl, r).start(priority=ROW_DMA_PRIORITY)
            return carry

        lax.fori_loop(0, item_nblk[it] * (MOE_BLOCK // GATHER_UNROLL), body, 0)

    def wait_gather(it, sl):
        for b in range(ITEM_BLOCKS):
            @pl.when(b < item_nblk[it])
            def _():
                pltpu.make_async_copy(hm_hbm.at[pl.ds(0, MOE_BLOCK)], xg.at[sl, pl.ds(b * MOE_BLOCK, MOE_BLOCK)],
                                      gsem.at[sl]).wait()

    def out_copy(it, m):
        r0 = pl.multiple_of(item_blk0[it] * MOE_BLOCK, MOE_BLOCK)
        return pltpu.make_async_copy(yp.at[pl.ds(0, m)], y_hbm.at[pl.ds(r0, m)], osem.at[0])

    def wait_out(it):
        for nb in range(1, ITEM_BLOCKS + 1):
            @pl.when(item_nblk[it] == nb)
            def _():
                out_copy(it, nb * MOE_BLOCK).wait()

    @pl.when(j == 0)
    def _():
        @pl.when(i == 0)
        def _():
            start_weight(0, 0)
            start_weight(0, 1)
            start_weight(1, 0)
            start_weight(0, 2)
            issue_gather(0, 0)

        start_weight(i + 1, 2)
        wait_gather(i, slot)

        @pl.when(i + 1 < n_items)
        def _():
            issue_gather(i + 1, 1 - slot)

    @pl.when(j == 1)
    def _():
        start_weight(i + 1, 1)

    @pl.when(j == 2)
    def _():
        start_weight(i + 2, 0)

        @pl.when(i > 0)
        def _():
            wait_out(i - 1)

    for ph in range(3):
        @pl.when(jnp.logical_and(j == ph, nblk > 0))
        def _():
            weight_copy(i, ph).wait()

    for nb in range(1, ITEM_BLOCKS + 1):
        m = nb * MOE_BLOCK

        @pl.when(jnp.logical_and(nblk == nb, j == 0))
        def _():
            hi, lo = _unpack_halves(xg[slot, 0:m, :])
            half = hi.shape[1]
            xb[0:m, 0:half] = hi.astype(BF16)
            xb[0:m, half:2 * half] = lo.astype(BF16)
            gs[0:m, :] = _bdot(xb[0:m, :], gu_buf[(2 * i) % GU_SLOTS].astype(BF16))

        @pl.when(jnp.logical_and(nblk == nb, j == 1))
        def _():
            g = gs[0:m, :]
            u = _bdot(xb[0:m, :], gu_buf[(2 * i + 1) % GU_SLOTS].astype(BF16))
            ab[0:m, :] = (g * jax.nn.sigmoid(g) * u).astype(BF16)

        @pl.when(jnp.logical_and(nblk == nb, j == 2))
        def _():
            yp[0:m, :] = _pack_halves(_bdot(ab[0:m, :], dn_buf[i % DN_SLOTS].astype(BF16)))
            out_copy(i, m).start()

    @pl.when(jnp.logical_and(i == n_items - 1, j == nj - 1))
    def _():
        wait_out(i)
        n_blocks = y_hbm.shape[0] // MOE_BLOCK
        yp[0:MOE_BLOCK, :] = jnp.zeros((MOE_BLOCK, yp.shape[1]), jnp.uint32)

        def tail_copy(blk):
            r0 = pl.multiple_of(blk * MOE_BLOCK, MOE_BLOCK)
            return pltpu.make_async_copy(yp.at[pl.ds(0, MOE_BLOCK)], y_hbm.at[pl.ds(r0, MOE_BLOCK)], osem.at[0])

        def start_body(blk, carry):
            tail_copy(blk).start()
            return carry

        def wait_body(blk, carry):
            tail_copy(blk).wait()
            return carry

        lax.fori_loop(used_blocks[0], n_blocks, start_body, 0)
        lax.fori_loop(used_blocks[0], n_blocks, wait_body, 0)


def _moe(hm, w_gate, w_up, w_down, item_e, item_blk0, item_nblk, row_tok, used_blocks, *, n_rows):
    d, de = w_gate.shape[1], w_gate.shape[2]
    n_items = item_e.shape[0]
    nj = 3
    rows = ITEM_BLOCKS * MOE_BLOCK
    any_spec = pl.BlockSpec(memory_space=pl.ANY)
    grid_spec = pltpu.PrefetchScalarGridSpec(
        num_scalar_prefetch=5,
        grid=(n_items, nj),
        in_specs=[any_spec, any_spec, any_spec, any_spec],
        out_specs=any_spec,
        scratch_shapes=[pltpu.VMEM((2, rows, d // 2), jnp.uint32),
                        pltpu.VMEM((rows, d), BF16),
                        pltpu.VMEM((rows, de), F32),
                        pltpu.VMEM((rows, de), BF16),
                        pltpu.VMEM((rows, d // 2), jnp.uint32),
                        pltpu.VMEM((GU_SLOTS, d, de), F32),
                        pltpu.VMEM((DN_SLOTS, de, d), F32),
                        pltpu.SemaphoreType.DMA((2,)),
                        pltpu.SemaphoreType.DMA((1,)),
                        pltpu.SemaphoreType.DMA((GU_SLOTS,)),
                        pltpu.SemaphoreType.DMA((DN_SLOTS,))],
    )
    return pl.pallas_call(
        _moe_kernel,
        grid_spec=grid_spec,
        out_shape=jax.ShapeDtypeStruct((n_rows, d // 2), jnp.uint32),
        compiler_params=_cparams(("arbitrary", "arbitrary")),
        name="moe_experts",
    )(item_e, item_blk0, item_nblk, row_tok, used_blocks, hm, w_gate, w_up, w_down)


def _combine_kernel(dest, x_ref, gt2_ref, rw_ref, y_hbm, o_ref, ybuf, sem, *, mod_row):
    i = pl.program_id(0)
    n = pl.num_programs(0)
    tm = x_ref.shape[0]
    slot = i % 2

    def issue(it, sl):
        base = it * tm

        def body(r4, carry):
            for rr in range(GATHER_UNROLL // 2):
                r = r4 * (GATHER_UNROLL // 2) + rr
                for k in range(2):
                    pltpu.make_async_copy(y_hbm.at[pl.ds(dest[2 * (base + r) + k], 1)], ybuf.at[sl, k, pl.ds(r, 1)],
                                          sem.at[sl]).start(priority=k)
            return carry

        lax.fori_loop(0, tm // (GATHER_UNROLL // 2), body, 0)

    @pl.when(i == 0)
    def _():
        issue(0, 0)

    for k in range(2):
        pltpu.make_async_copy(y_hbm.at[pl.ds(0, tm)], ybuf.at[slot, k], sem.at[slot]).wait()

    @pl.when(i + 1 < n)
    def _():
        issue(i + 1, 1 - slot)

    w = rw_ref[...]
    gate = gt2_ref[pl.ds(mod_row(i), 1), :]
    hi0, lo0 = _unpack_halves(ybuf[slot, 0])
    hi1, lo1 = _unpack_halves(ybuf[slot, 1])
    half = hi0.shape[1]
    o_ref[:, 0:half] = x_ref[:, 0:half] + gate[:, 0:half] * (w[:, 0:1] * hi0 + w[:, 1:2] * hi1)
    o_ref[:, half:2 * half] = (x_ref[:, half:2 * half]
                               + gate[:, half:2 * half] * (w[:, 0:1] * lo0 + w[:, 1:2] * lo1))


def _combine(dest, xnew, mods, rw, y, *, seq, tm):
    t, d = xnew.shape
    tiles_per_b = seq // tm
    grid_spec = pltpu.PrefetchScalarGridSpec(
        num_scalar_prefetch=1,
        grid=(t // tm,),
        in_specs=[pl.BlockSpec((tm, d), lambda i, ds: (i, 0)),
                  _mod_spec(mods, 5, d),
                  pl.BlockSpec((tm, LANES), lambda i, ds: (i, 0)),
                  pl.BlockSpec(memory_space=pl.ANY)],
        out_specs=pl.BlockSpec((tm, d), lambda i, ds: (i, 0)),
        scratch_shapes=[pltpu.VMEM((2, 2, tm, d // 2), jnp.uint32), pltpu.SemaphoreType.DMA((2,))],
    )
    return pl.pallas_call(
        functools.partial(_combine_kernel, mod_row=lambda i: i // tiles_per_b),
        grid_spec=grid_spec,
        out_shape=jax.ShapeDtypeStruct((t, d), F32),
        compiler_params=_cparams(("arbitrary",)),
        name="moe_combine",
    )(dest, xnew, mods, rw, y)


def _rope_tables(n_tokens):
    rows = n_tokens // GRID_W
    row = jnp.repeat(jnp.arange(rows, dtype=jnp.int32), GRID_W).astype(F32)
    col = jnp.tile(jnp.arange(GRID_W, dtype=jnp.int32), rows).astype(F32)
    n_freq = QK_ROPE_DIM // 4
    inv = ROPE_THETA ** (-jnp.arange(n_freq, dtype=F32) / n_freq)
    ar = row[:, None] * inv[None, :]
    ac = col[:, None] * inv[None, :]
    cr, sr, cc, sc = jnp.cos(ar), jnp.sin(ar), jnp.cos(ac), jnp.sin(ac)
    cos64 = jnp.concatenate([cr, cr, cc, cc], axis=-1)
    sin64 = jnp.concatenate([-sr, sr, -sc, sc], axis=-1)
    return jnp.tile(cos64, (1, 2)), jnp.tile(sin64, (1, 2))


def _channel_dft_table():
    c = np.arange(FOURIER_GROUP_DIM).reshape(-1, 1)
    k = np.arange(FOURIER_GROUP_DIM).reshape(1, -1)
    ang = (2.0 * np.pi / FOURIER_GROUP_DIM) * ((c * k) % FOURIER_GROUP_DIM)
    return jnp.asarray(np.concatenate([np.cos(ang), -np.sin(ang)], axis=1).astype(np.float32)).astype(BF16)


def _split_heads(w, widths):
    k = w.shape[0]
    wh = w.reshape(k, MLA_HEADS, sum(widths))
    outs, off = [], 0
    for wd in widths:
        outs.append(wh[:, :, off:off + wd].reshape(k, MLA_HEADS * wd))
        off += wd
    return outs


def kernel(x, c, ctx, c_ctx, w_ada, b_ada, g_norm1, g_norm2, w_in, g_q_a, g_kv_a, w_uq, w_ukv, g_qk_q, g_qk_k,
           g_out_four, g_out_attn, w_out, w_router_group, b_router_group, w_router_expert, b_router_expert,
           w_gate, w_up, w_down):
    b, s, d = x.shape
    lc = ctx.shape[1]
    t = b * s
    layer_params = (w_ada, b_ada, g_norm1, g_norm2, w_in, g_q_a, g_kv_a, w_uq, w_ukv, g_qk_q, g_qk_k, g_out_four,
                    g_out_attn, w_out, w_router_group, b_router_group, w_router_expert, b_router_expert,
                    w_gate, w_up, w_down)
    assert all(p.shape[0] == 1 for p in layer_params), "single-layer block"
    (w_ada, b_ada, g_norm1, g_norm2, w_in, g_q_a, g_kv_a, w_uq, w_ukv, g_qk_q, g_qk_k, g_out_four,
     g_out_attn, w_out, w_router_group, b_router_group, w_router_expert, b_router_expert,
     w_gate, w_up, w_down) = [p.reshape(p.shape[1:]) for p in layer_params]

    cond8 = jnp.concatenate([c, c_ctx[None, :], jnp.zeros((8 - b - 1, d), F32)], axis=0)
    mods = _ada_mod(cond8, w_ada, b_ada)

    w_all = jnp.concatenate([w_in, w_in[:, ROPE_OFF:]], axis=1).astype(BF16)
    dc = _channel_dft_table()
    wqn, wqr = [w.astype(BF16) for w in _split_heads(w_uq, (QK_NOPE_DIM, QK_ROPE_DIM))]
    wkn, wv = [w.astype(BF16) for w in _split_heads(w_ukv, (QK_NOPE_DIM, V_HEAD_DIM))]
    gqn = g_qk_q[:QK_NOPE_DIM].reshape(1, -1)
    gqr2 = jnp.tile(g_qk_q[QK_NOPE_DIM:], 2).reshape(1, -1)
    gkn = g_qk_k[:QK_NOPE_DIM].reshape(1, -1)
    gkr2 = jnp.tile(g_qk_k[QK_NOPE_DIM:], 2).reshape(1, -1)
    g1 = g_norm1.reshape(1, d)
    gq = g_q_a.reshape(1, -1)
    gkv = g_kv_a.reshape(1, -1)

    x2 = x.reshape(t, d)
    u, cq, ckv, kr2 = _in_proj(x2, mods, 0, s, g1, gq, gkv, w_all, dc, with_q=True, tm=512)
    ckv_c, kr2_c = _in_proj(ctx.reshape(b * lc, d), mods, b, b * lc, g1, gq, gkv, w_all, dc, with_q=False, tm=lc)

    cos_t, sin_t = _rope_tables(s)
    q, kx, vx = _qkv(cq, ckv, kr2, cos_t, sin_t, wqn, wqr, gqn, gqr2, wkn, wv, gkn, gkr2,
                     batch=b, seq=s, tm=512, with_q=True, with_rope=True)
    kc, vc = _qkv(None, ckv_c, kr2_c, None, None, None, None, None, None, wkn, wv, gkn, gkr2,
                  batch=b, seq=lc, tm=lc, with_q=False, with_rope=False)

    attn = _attention(q, kx, kc, vx, vc, tq=1024).reshape(t, MLA_DIM)
    four = _seq_dft(u, b, s)

    wo = w_out.astype(BF16)
    n_route = N_GROUPS + N_EXPERTS
    wrt = jnp.concatenate([w_router_group, w_router_expert, jnp.zeros((d, LANES - n_route), F32)], axis=1)
    wrt_hi = wrt.astype(BF16)
    wrt_lo = (wrt - wrt_hi.astype(F32)).astype(BF16)
    brt = jnp.concatenate([b_router_group, b_router_expert, jnp.zeros((LANES - n_route,), F32)]).reshape(1, -1)
    xnew, hm, ri, rw, cnt = _out_router(x2, four, attn, mods, g_out_four.reshape(1, -1), g_out_attn.reshape(1, -1),
                                        g_norm2.reshape(1, d), wo, wrt_hi, wrt_lo, brt, seq=s, tm=512)

    counts = cnt[0, :N_EXPERTS].astype(jnp.int32)
    nblk_e = (counts + MOE_BLOCK - 1) // MOE_BLOCK
    blk_end = jnp.cumsum(nblk_e)
    blk_start = blk_end - nblk_e
    n_slots = t * 2
    n_blocks = -(-(n_slots + N_EXPERTS * (MOE_BLOCK - 1)) // MOE_BLOCK)
    n_rows = n_blocks * MOE_BLOCK
    e12 = ri[:, 0:2]
    seg_start = jnp.sum(jnp.where(e12[:, :, None] == jnp.arange(N_EXPERTS, dtype=jnp.int32), blk_start, 0), axis=-1)
    dest = (seg_start * MOE_BLOCK + ri[:, 2:4]).reshape(-1)
    tok_ids = jnp.repeat(jnp.arange(t, dtype=jnp.int32), 2)
    row_tok = jnp.zeros((n_rows,), jnp.int32).at[dest].set(tok_ids)
    items_e = (nblk_e + ITEM_BLOCKS - 1) // ITEM_BLOCKS
    item_end = jnp.cumsum(items_e)
    n_items = (n_blocks + (ITEM_BLOCKS - 1) * N_EXPERTS) // ITEM_BLOCKS
    idx = jnp.arange(n_items, dtype=jnp.int32)
    total = item_end[-1]
    idx_c = jnp.minimum(idx, total - 1)
    ie = jnp.minimum(jnp.sum(idx_c[:, None] >= item_end[None, :], axis=1), N_EXPERTS - 1).astype(jnp.int32)
    local = idx_c - (item_end - items_e)[ie]
    item_blk0 = (blk_start[ie] + ITEM_BLOCKS * local).astype(jnp.int32)
    item_nblk = jnp.where(idx < total, jnp.clip(nblk_e[ie] - ITEM_BLOCKS * local, 0, ITEM_BLOCKS), 0).astype(jnp.int32)

    used_blocks = blk_end[-1:].astype(jnp.int32)
    y = _moe(hm, w_gate, w_up, w_down, ie, item_blk0, item_nblk, row_tok, used_blocks, n_rows=n_rows)
    out = _combine(dest.astype(jnp.int32), xnew, mods, rw, y, seq=s, tm=256)
    return out.reshape(b, s, d)
```

```python
import functools
import math

import numpy as np
import jax
import jax.numpy as jnp
from jax import lax
from jax.experimental import pallas as pl
from jax.experimental.pallas import tpu as pltpu

F32 = jnp.float32
BF16 = jnp.bfloat16

D_MODEL = 2048
GRID_W = 64
EPS = 1e-6
N_MOD = 6
N_FOURIER_GROUPS = 4
FOURIER_GROUP_DIM = 256
FOURIER_DIM = 1024
MLA_HEADS = 8
QK_NOPE_DIM = 128
QK_ROPE_DIM = 64
QK_HEAD_DIM = 192
V_HEAD_DIM = 128
Q_LORA_RANK = 768
KV_LORA_RANK = 512
MLA_DIM = 1024
ROPE_THETA = 10000.0
Q_OFF = FOURIER_DIM
KV_OFF = Q_OFF + Q_LORA_RANK
ROPE_OFF = KV_OFF + KV_LORA_RANK
N_GROUPS = 8
EXPERTS_PER_GROUP = 8
N_EXPERTS = 64
D_EXPERT = 768

LANES = 128
HEAD_PAD = 256
V_PAD = 256
MOE_BLOCK = 64
ITEM_BLOCKS = 8
GATHER_UNROLL = 8
GU_SLOTS = 3
DN_SLOTS = 2
ATTN_KEY_CHUNK = 512
ROW_DMA_PRIORITY = 0
WEIGHT_DMA_PRIORITY = 1
VMEM_LIMIT = 56 * 1024 * 1024
NEG_BIG = -1e30


def _cparams(sem):
    return pltpu.CompilerParams(dimension_semantics=sem, vmem_limit_bytes=VMEM_LIMIT)


def _bdot(a, b):
    return jnp.dot(a, b, preferred_element_type=F32)


def _pack_halves(v):
    n = v.shape[1] // 2
    hi = pltpu.bitcast(v[:, :n].astype(BF16).astype(F32), jnp.uint32)
    lo = pltpu.bitcast(v[:, n:].astype(BF16).astype(F32), jnp.uint32)
    return hi | (lo >> 16)


def _unpack_halves(u):
    hi = pltpu.bitcast(u & jnp.uint32(0xFFFF0000), F32)
    lo = pltpu.bitcast(u << 16, F32)
    return hi, lo


def _ada_kernel(c_ref, w_ref, b_ref, o_ref):
    c = c_ref[...]
    s = (c * jax.nn.sigmoid(c)).astype(BF16)
    o_ref[...] = _bdot(s, w_ref[...].astype(BF16)) + b_ref[...]


def _ada_mod(cond8, w_ada, b_ada):
    d, n = w_ada.shape
    tn = 1024
    return pl.pallas_call(
        _ada_kernel,
        grid=(n // tn,),
        in_specs=[pl.BlockSpec((8, d), lambda i: (0, 0)),
                  pl.BlockSpec((d, tn), lambda i: (0, i)),
                  pl.BlockSpec((1, tn), lambda i: (0, i))],
        out_specs=pl.BlockSpec((8, tn), lambda i: (0, i)),
        out_shape=jax.ShapeDtypeStruct((8, n), F32),
        compiler_params=_cparams(("arbitrary",)),
        name="ada_mod",
    )(cond8, w_ada, b_ada.reshape(1, n))


def _in_proj_kernel(x_ref, sh_ref, sc_ref, g_ref, gq_ref, gkv_ref, w_ref, dc_ref, *out_refs, with_q, mod_row):
    row = pl.ds(mod_row(pl.program_id(0)), 1)
    x = x_ref[...]
    ms = jnp.mean(x * x, axis=-1, keepdims=True)
    a = g_ref[...] * (1.0 + sc_ref[row, :])
    h = (x * lax.rsqrt(ms + EPS) * a + sh_ref[row, :]).astype(BF16)
    if with_q:
        u_ref, cq_ref, ckv_ref, kr_ref = out_refs
        f = _bdot(h, w_ref[:, 0:Q_OFF]).astype(BF16)
        dc = dc_ref[...]
        for g in range(N_FOURIER_GROUPS):
            lo = g * FOURIER_GROUP_DIM
            ug = _bdot(f[:, lo:lo + FOURIER_GROUP_DIM], dc)
            u_ref[:, lo:lo + FOURIER_GROUP_DIM] = ug[:, :FOURIER_GROUP_DIM]
            u_ref[:, FOURIER_DIM + lo:FOURIER_DIM + lo + FOURIER_GROUP_DIM] = ug[:, FOURIER_GROUP_DIM:]
        pq = _bdot(h, w_ref[:, Q_OFF:KV_OFF])
        msq = jnp.mean(pq * pq, axis=-1, keepdims=True)
        cq_ref[...] = (pq * lax.rsqrt(msq + EPS) * gq_ref[...]).astype(BF16)
    else:
        ckv_ref, kr_ref = out_refs
    pkv = _bdot(h, w_ref[:, KV_OFF:ROPE_OFF])
    mskv = jnp.mean(pkv * pkv, axis=-1, keepdims=True)
    ckv_ref[...] = (pkv * lax.rsqrt(mskv + EPS) * gkv_ref[...]).astype(BF16)
    kr_ref[...] = _bdot(h, w_ref[:, ROPE_OFF:ROPE_OFF + LANES])


def _mod_spec(mods, k, d):
    return pl.BlockSpec((mods.shape[0], d), lambda i, *_: (0, k))


def _in_proj(x2, mods, first_row, rows_per_mod, g1, gq, gkv, w_all, dc, *, with_q, tm):
    t, d = x2.shape
    nt = t // tm
    tiles_per_mod = rows_per_mod // tm

    def const(shape):
        return pl.BlockSpec(shape, lambda i: (0,) * len(shape))

    in_specs = [pl.BlockSpec((tm, d), lambda i: (i, 0)), _mod_spec(mods, 0, d), _mod_spec(mods, 1, d),
                const((1, d)), const((1, Q_LORA_RANK)), const((1, KV_LORA_RANK)),
                const(w_all.shape), const(dc.shape)]

    def rows(n):
        return pl.BlockSpec((tm, n), lambda i: (i, 0))

    out_specs = [rows(KV_LORA_RANK), rows(LANES)]
    out_shape = [jax.ShapeDtypeStruct((t, KV_LORA_RANK), BF16), jax.ShapeDtypeStruct((t, LANES), F32)]
    if with_q:
        out_specs = [rows(2 * FOURIER_DIM), rows(Q_LORA_RANK)] + out_specs
        out_shape = [jax.ShapeDtypeStruct((t, 2 * FOURIER_DIM), F32),
                     jax.ShapeDtypeStruct((t, Q_LORA_RANK), BF16)] + out_shape
    return pl.pallas_call(
        functools.partial(_in_proj_kernel, with_q=with_q, mod_row=lambda i: first_row + i // tiles_per_mod),
        grid=(nt,),
        in_specs=in_specs,
        out_specs=out_specs,
        out_shape=out_shape,
        compiler_params=_cparams(("arbitrary",)),
        name="in_proj_x" if with_q else "in_proj_ctx",
    )(x2, mods, mods, g1, gq, gkv, w_all, dc)


def _swap_halves(y, first_half):
    return jnp.where(first_half, pltpu.roll(y, LANES - 16, 1), pltpu.roll(y, 16, 1))


def _qkv_kernel(*refs, with_q, with_rope):
    it = iter(refs)
    if with_q:
        cq_ref = next(it)
    ckv_ref = next(it)
    kr_ref = next(it)
    if with_rope:
        cos_ref = next(it)
        sin_ref = next(it)
    if with_q:
        wqn_ref = next(it)
        wqr_ref = next(it)
        gqn_ref = next(it)
        gqr_ref = next(it)
    wkn_ref = next(it)
    wv_ref = next(it)
    gkn_ref = next(it)
    gkr_ref = next(it)
    if with_q:
        q_ref = next(it)
    k_ref = next(it)
    v_ref = next(it)

    tm = ckv_ref.shape[0]
    lane = lax.broadcasted_iota(jnp.int32, (tm, LANES), 1)
    low = lane < QK_ROPE_DIM
    first_half = (lane % 32) < 16
    inv_dim = 1.0 / QK_HEAD_DIM

    def rope(y):
        if not with_rope:
            return y
        return y * cos_ref[...] + _swap_halves(y, first_half) * sin_ref[...]

    if with_q:
        cq = cq_ref[...]
        qn = _bdot(cq, wqn_ref[...])
        qr = _bdot(cq, wqr_ref[...])
        qscale = QK_HEAD_DIM ** -0.5
        for p in range(MLA_HEADS // 2):
            blk = qr[:, p * LANES:(p + 1) * LANES]
            sq = blk * blk
            ss_lo = jnp.sum(jnp.where(low, sq, 0.0), axis=-1, keepdims=True)
            ss_hi = jnp.sum(jnp.where(low, 0.0, sq), axis=-1, keepdims=True)
            scales = []
            for hh, ssr in ((2 * p, ss_lo), (2 * p + 1, ss_hi)):
                nh = qn[:, hh * LANES:(hh + 1) * LANES]
                ssq = jnp.sum(nh * nh, axis=-1, keepdims=True) + ssr
                s = lax.rsqrt(ssq * inv_dim + EPS)
                scales.append(s)
                q_ref[hh, :, 0:LANES] = (nh * s * gqn_ref[...] * qscale).astype(BF16)
            s_pair = jnp.where(low, scales[0], scales[1])
            r = rope(blk * s_pair * gqr_ref[...]) * qscale
            q_ref[2 * p, :, LANES:2 * LANES] = jnp.where(low, r, 0.0).astype(BF16)
            q_ref[2 * p + 1, :, LANES:2 * LANES] = jnp.where(low, pltpu.roll(r, QK_ROPE_DIM, 1), 0.0).astype(BF16)

    ckv = ckv_ref[...]
    kn = _bdot(ckv, wkn_ref[...])
    v = _bdot(ckv, wv_ref[...])
    kr = kr_ref[...]
    ss_r = jnp.sum(jnp.where(low, kr * kr, 0.0), axis=-1, keepdims=True)
    base = rope(kr * gkr_ref[...])
    ones_col = jnp.where(lane == 0, 1.0, 0.0).astype(BF16)
    for hh in range(MLA_HEADS):
        nh = kn[:, hh * LANES:(hh + 1) * LANES]
        ssq = jnp.sum(nh * nh, axis=-1, keepdims=True) + ss_r
        s = lax.rsqrt(ssq * inv_dim + EPS)
        k_ref[hh, :, 0:LANES] = (nh * s * gkn_ref[...]).astype(BF16)
        k_ref[hh, :, LANES:2 * LANES] = jnp.where(low, base * s, 0.0).astype(BF16)
        v_ref[hh, :, 0:LANES] = v[:, hh * LANES:(hh + 1) * LANES].astype(BF16)
        v_ref[hh, :, LANES:2 * LANES] = ones_col


def _qkv(cq, ckv, kr2, cos_t, sin_t, wqn, wqr, gqn, gqr2, wkn, wv, gkn, gkr2, *, batch, seq, tm, with_q,
         with_rope):
    t = ckv.shape[0]
    nt = t // tm
    tiles_per_b = seq // tm

    def rows(n):
        return pl.BlockSpec((tm, n), lambda i: (i, 0))

    def const(arr):
        return pl.BlockSpec(arr.shape, lambda i: (0,) * arr.ndim)

    tab_spec = pl.BlockSpec((tm, LANES), lambda i: (i % tiles_per_b, 0))

    def head_out(width):
        return pl.BlockSpec((None, MLA_HEADS, tm, width), lambda i: (i // tiles_per_b, 0, i % tiles_per_b, 0))

    args, in_specs = [], []
    if with_q:
        args.append(cq)
        in_specs.append(rows(Q_LORA_RANK))
    args += [ckv, kr2]
    in_specs += [rows(KV_LORA_RANK), rows(LANES)]
    if with_rope:
        args += [cos_t, sin_t]
        in_specs += [tab_spec, tab_spec]
    if with_q:
        args += [wqn, wqr, gqn, gqr2]
        in_specs += [const(wqn), const(wqr), const(gqn), const(gqr2)]
    args += [wkn, wv, gkn, gkr2]
    in_specs += [const(wkn), const(wv), const(gkn), const(gkr2)]

    out_specs = [head_out(HEAD_PAD), head_out(V_PAD)]
    out_shape = [jax.ShapeDtypeStruct((batch, MLA_HEADS, seq, HEAD_PAD), BF16),
                 jax.ShapeDtypeStruct((batch, MLA_HEADS, seq, V_PAD), BF16)]
    if with_q:
        out_specs = [head_out(HEAD_PAD)] + out_specs
        out_shape = [jax.ShapeDtypeStruct((batch, MLA_HEADS, seq, HEAD_PAD), BF16)] + out_shape
    return pl.pallas_call(
        functools.partial(_qkv_kernel, with_q=with_q, with_rope=with_rope),
        grid=(nt,),
        in_specs=in_specs,
        out_specs=out_specs,
        out_shape=out_shape,
        compiler_params=_cparams(("arbitrary",)),
        name="qkv_x" if with_q else "kv_ctx",
    )(*args)


def _attn_kernel(q_ref, kx_ref, kc_ref, vx_ref, vc_ref, o_ref):
    q = q_ref[...]
    tq = q.shape[0]
    dn = (((1,), (1,)), ((), ()))

    def chunk(k, v, state):
        m, acc = state
        s = lax.dot_general(q, k, dn, preferred_element_type=F32)
        m_new = jnp.maximum(m, jnp.max(s, axis=-1, keepdims=True))
        p = jnp.exp(s - m_new).astype(BF16)
        acc = jnp.exp(m - m_new) * acc + _bdot(p, v)
        return m_new, acc

    state = (jnp.full((tq, 1), NEG_BIG, F32), jnp.zeros((tq, V_PAD), F32))
    for c in range(kx_ref.shape[0] // ATTN_KEY_CHUNK):
        rows = slice(c * ATTN_KEY_CHUNK, (c + 1) * ATTN_KEY_CHUNK)
        state = chunk(kx_ref[rows, :], vx_ref[rows, :], state)
    _, acc = chunk(kc_ref[...], vc_ref[...], state)
    o_ref[...] = acc[:, :V_HEAD_DIM] / acc[:, V_HEAD_DIM:V_HEAD_DIM + 1]


def _attention(q, kx, kc, vx, vc, *, tq):
    b, h, s, _ = q.shape
    lc = kc.shape[2]
    return pl.pallas_call(
        _attn_kernel,
        grid=(b, h, s // tq),
        in_specs=[pl.BlockSpec((None, None, tq, HEAD_PAD), lambda bi, hi, qi: (bi, hi, qi, 0)),
                  pl.BlockSpec((None, None, s, HEAD_PAD), lambda bi, hi, qi: (bi, hi, 0, 0)),
                  pl.BlockSpec((None, None, lc, HEAD_PAD), lambda bi, hi, qi: (bi, hi, 0, 0)),
                  pl.BlockSpec((None, None, s, V_PAD), lambda bi, hi, qi: (bi, hi, 0, 0)),
                  pl.BlockSpec((None, None, lc, V_PAD), lambda bi, hi, qi: (bi, hi, 0, 0))],
        out_specs=pl.BlockSpec((None, tq, V_HEAD_DIM), lambda bi, hi, qi: (bi, qi, hi)),
        out_shape=jax.ShapeDtypeStruct((b, s, h * V_HEAD_DIM), F32),
        compiler_params=_cparams(("arbitrary", "arbitrary", "arbitrary")),
        name="attention",
    )(q, kx, kc, vx, vc)


def _seq_dft_kernel(ure_ref, uim_ref, r_ref, e_ref, t2_ref, o_ref, a_ref):
    s = pl.program_id(2)
    n = 2 * GRID_W * 8
    cols = ure_ref.shape[-1]

    @pl.when(s < 8)
    def _():
        t = _bdot(r_ref[...].astype(BF16), e_ref[...])
        row = lax.broadcasted_iota(jnp.int32, (n, n), 0)
        col = lax.broadcasted_iota(jnp.int32, (n, n), 1)
        t = jnp.where((row % 8) == (col % 8), t, 0.0).astype(BF16)
        rhs = jnp.concatenate([ure_ref[...].reshape(GRID_W * 8, cols), uim_ref[...].reshape(GRID_W * 8, cols)],
                              axis=0).astype(BF16)
        a = _bdot(t, rhs)
        a_ref[:, :, pl.ds(pl.multiple_of(s * 8, 8), 8), :] = a.reshape(2, GRID_W, 8, cols)

    @pl.when(s >= 8)
    def _():
        k0 = pl.multiple_of((s - 8) * 8, 8)
        rhs = a_ref[:, pl.ds(k0, 8), :, :].reshape(2 * 8 * GRID_W, cols).astype(BF16)
        y = _bdot(t2_ref[...].astype(BF16), rhs)
        o_ref[...] = y.reshape(GRID_W, 8, cols)


def _seq_dft_tables(n_seq):
    w = GRID_W
    ch = np.arange(8).reshape(8, 1, 1, 1)
    kb = np.arange(w).reshape(1, w, 1, 1)
    j = np.arange(8).reshape(1, 1, 8, 1)
    r = np.arange(w).reshape(1, 1, 1, w)
    ang = (2.0 * np.pi / n_seq) * ((kb * (w * r + 8 * ch + j)) % n_seq)
    c, s = np.cos(ang), np.sin(ang)
    rot = np.stack([np.stack([c, s], axis=3), np.stack([-s, c], axis=3)], axis=1)
    r1 = rot.reshape(8, 2 * w * 8, 2 * w).astype(np.float32)
    expand = (np.arange(2 * w * 8)[None, :] // 8 == np.arange(2 * w)[:, None]).astype(np.float32)
    ka = np.arange(w).reshape(w, 1)
    cp = np.arange(w).reshape(1, w)
    ang2 = (2.0 * np.pi / w) * ((ka * cp) % w)
    norm = 1.0 / math.sqrt(n_seq * FOURIER_GROUP_DIM)
    cs = np.stack([np.cos(ang2), np.sin(ang2)], axis=1) * norm
    eye8 = np.eye(8)
    t2 = (cs[:, None, :, None, :] * eye8[None, :, None, :, None]).reshape(w * 8, 2 * 8 * w).astype(np.float32)
    return jnp.asarray(r1), jnp.asarray(expand).astype(BF16), jnp.asarray(t2)


def _seq_dft(u, batch, n_seq):
    w = GRID_W
    r1, expand, t2 = _seq_dft_tables(n_seq)
    halves = 2
    cols = FOURIER_DIM // halves
    u5 = u.reshape(batch, w, 8, 8, 2 * FOURIER_DIM)

    def u_spec(part):
        return pl.BlockSpec((None, w, None, 8, cols),
                            lambda b, h, s: (b, 0, jnp.minimum(s, 7), 0, part * halves + h))

    y = pl.pallas_call(
        _seq_dft_kernel,
        grid=(batch, halves, 16),
        in_specs=[u_spec(0), u_spec(1),
                  pl.BlockSpec((None, 2 * w * 8, 2 * w), lambda b, h, s: (jnp.minimum(s, 7), 0, 0)),
                  pl.BlockSpec((2 * w, 2 * w * 8), lambda b, h, s: (0, 0)),
                  pl.BlockSpec((w * 8, 2 * 8 * w), lambda b, h, s: (0, 0))],
        out_specs=pl.BlockSpec((None, w, None, 8, cols), lambda b, h, s: (b, 0, jnp.maximum(s - 8, 0), 0, h)),
        out_shape=jax.ShapeDtypeStruct((batch, w, 8, 8, FOURIER_DIM), F32),
        scratch_shapes=[pltpu.VMEM((2, w, w, cols), F32)],
        compiler_params=_cparams(("arbitrary", "arbitrary", "arbitrary")),
        name="seq_dft",
    )(u5, u5, r1, expand, t2)
    return y.reshape(batch * n_seq, FOURIER_DIM)


def _out_router_kernel(x_ref, four_ref, attn_ref, gt1_ref, sh2_ref, sc2_ref, gf_ref, ga_ref, g2_ref,
                       wo_ref, wrh_ref, wrl_ref, br_ref,
                       xnew_ref, hm_ref, ri_ref, rw_ref, cnt_ref, carry_ref, *, mod_row):
    i = pl.program_id(0)
    tm = x_ref.shape[0]
    row = pl.ds(mod_row(i), 1)

    @pl.when(i == 0)
    def _():
        carry_ref[...] = jnp.zeros_like(carry_ref)

    def norm(v, g):
        return (v * lax.rsqrt(jnp.mean(v * v, axis=-1, keepdims=True) + EPS) * g).astype(BF16)

    mix = (_bdot(norm(four_ref[...], gf_ref[...]), wo_ref[0:FOURIER_DIM, :])
           + _bdot(norm(attn_ref[...], ga_ref[...]), wo_ref[FOURIER_DIM:FOURIER_DIM + MLA_DIM, :]))
    xn = x_ref[...] + gt1_ref[row, :] * mix
    xnew_ref[...] = xn
    ms = jnp.mean(xn * xn, axis=-1, keepdims=True)
    hm = xn * lax.rsqrt(ms + EPS) * (g2_ref[...] * (1.0 + sc2_ref[row, :])) + sh2_ref[row, :]
    hm_ref[...] = _pack_halves(hm)

    hm_hi = hm.astype(BF16)
    hm_lo = (hm - hm_hi.astype(F32)).astype(BF16)
    logits = _bdot(hm_hi, wrh_ref[...]) + _bdot(hm_lo, wrh_ref[...]) + _bdot(hm_hi, wrl_ref[...]) + br_ref[...]
    lane = lax.broadcasted_iota(jnp.int32, (tm, LANES), 1)
    lanef = lane.astype(F32)
    far = 1e9

    lg = jnp.where(lane < N_GROUPS, logits, NEG_BIG)
    m1 = jnp.max(lg, axis=-1, keepdims=True)
    g_p = 1.0 / jnp.sum(jnp.exp(lg - m1), axis=-1, keepdims=True)
    gidx = jnp.min(jnp.where(lg >= m1, lanef, far), axis=-1, keepdims=True)
    lo = N_GROUPS + EXPERTS_PER_GROUP * gidx
    in_group = jnp.where(lanef >= lo, jnp.where(lanef < lo + EXPERTS_PER_GROUP, 1.0, 0.0), 0.0) > 0.5
    le = jnp.where(in_group, logits, NEG_BIG)
    m2 = jnp.max(le, axis=-1, keepdims=True)
    idx1 = jnp.min(jnp.where(le >= m2, lanef, far), axis=-1, keepdims=True)
    le2 = jnp.where(lanef == idx1, NEG_BIG, le)
    m3 = jnp.max(le2, axis=-1, keepdims=True)
    idx2 = jnp.min(jnp.where(le2 >= m3, lanef, far), axis=-1, keepdims=True)
    t = jnp.exp(m3 - m2)
    p1 = 1.0 / (1.0 + t)
    p2 = t / (1.0 + t)
    e1 = idx1 - N_GROUPS
    e2 = idx2 - N_GROUPS

    oh1 = jnp.where(lanef == e1, 1.0, 0.0)
    oh2 = jnp.where(lanef == e2, 1.0, 0.0)
    ohs = oh1 + oh2
    row = lax.broadcasted_iota(jnp.int32, (tm, tm), 0)
    col = lax.broadcasted_iota(jnp.int32, (tm, tm), 1)
    tri = jnp.where(row > col, 1.0, 0.0).astype(BF16)
    before = _bdot(tri, ohs.astype(BF16)) + carry_ref[...]
    rank1 = jnp.sum(oh1 * before, axis=-1, keepdims=True)
    rank2 = jnp.sum(oh2 * before, axis=-1, keepdims=True)
    carry = carry_ref[...] + jnp.sum(ohs, axis=0, keepdims=True)
    carry_ref[...] = carry
    cnt_ref[...] = jnp.broadcast_to(carry, cnt_ref.shape)

    ri = jnp.where(lane == 0, e1, jnp.where(lane == 1, e2, jnp.where(lane == 2, rank1, jnp.where(lane == 3, rank2, 0.0))))
    ri_ref[...] = ri.astype(jnp.int32)
    rw_ref[...] = jnp.where(lane == 0, g_p * p1, jnp.where(lane == 1, g_p * p2, 0.0))


def _out_router(x2, four, attn, mods, gf, ga, g2, wo, wrh, wrl, br, *, seq, tm):
    t, d = x2.shape
    nt = t // tm
    tiles_per_b = seq // tm

    def rows(n):
        return pl.BlockSpec((tm, n), lambda i: (i, 0))

    def const(arr):
        return pl.BlockSpec(arr.shape, lambda i: (0,) * arr.ndim, pipeline_mode=pl.Buffered(1))

    return pl.pallas_call(
        functools.partial(_out_router_kernel, mod_row=lambda i: i // tiles_per_b),
        grid=(nt,),
        in_specs=[rows(d), rows(FOURIER_DIM), rows(MLA_DIM), _mod_spec(mods, 2, d), _mod_spec(mods, 3, d),
                  _mod_spec(mods, 4, d),
                  const(gf), const(ga), const(g2), const(wo), const(wrh), const(wrl), const(br)],
        out_specs=[rows(d), rows(d // 2), rows(LANES), rows(LANES), pl.BlockSpec((8, LANES), lambda i: (0, 0))],
        out_shape=[jax.ShapeDtypeStruct((t, d), F32), jax.ShapeDtypeStruct((t, d // 2), jnp.uint32),
                   jax.ShapeDtypeStruct((t, LANES), jnp.int32), jax.ShapeDtypeStruct((t, LANES), F32),
                   jax.ShapeDtypeStruct((8, LANES), F32)],
        scratch_shapes=[pltpu.VMEM((1, LANES), F32)],
        compiler_params=_cparams(("arbitrary",)),
        name="out_proj_router",
    )(x2, four, attn, mods, mods, mods, gf, ga, g2, wo, wrh, wrl, br)


def _moe_kernel(item_e, item_blk0, item_nblk, row_tok, used_blocks,
                hm_hbm, wg_hbm, wu_hbm, wd_hbm, y_hbm,
                xg, xb, gs, ab, yp, gu_buf, dn_buf, gsem, osem, gusem, dnsem):
    i = pl.program_id(0)
    j = pl.program_id(1)
    n_items = pl.num_programs(0)
    nj = pl.num_programs(1)
    slot = i % 2
    nblk = item_nblk[i]

    def weight_copy(it, ph):
        if ph == 2:
            ws = it % DN_SLOTS
            return pltpu.make_async_copy(wd_hbm.at[item_e[it]], dn_buf.at[ws], dnsem.at[ws])
        ws = (2 * it + ph) % GU_SLOTS
        return pltpu.make_async_copy((wg_hbm, wu_hbm)[ph].at[item_e[it]], gu_buf.at[ws], gusem.at[ws])

    def start_weight(it, ph):
        it_c = jnp.minimum(it, n_items - 1)

        @pl.when(jnp.logical_and(it < n_items, item_nblk[it_c] > 0))
        def _():
            weight_copy(it_c, ph).start(priority=WEIGHT_DMA_PRIORITY)

    def gather_copy(tok, sl, r):
        return pltpu.make_async_copy(hm_hbm.at[pl.ds(tok, 1)], xg.at[sl, pl.ds(r, 1)], gsem.at[sl])

    def issue_gather(it, sl):
        r0 = item_blk0[it] * MOE_BLOCK

        def body(r8, carry):
            for k in range(GATHER_UNROLL):
                r = r8 * GATHER_UNROLL + k
                gather_copy(row_tok[r0 + r], sl, r).start(priority=ROW_DMA_PRIORITY)
            return carry

        lax.fori_loop(0, item_nblk[it] * (MOE_BLOCK // GATHER_UNROLL), body, 0)

    def wait_gather(it, sl):
        for b in range(ITEM_BLOCKS):
            @pl.when(b < item_nblk[it])
            def _():
                pltpu.make_async_copy(hm_hbm.at[pl.ds(0, MOE_BLOCK)], xg.at[sl, pl.ds(b * MOE_BLOCK, MOE_BLOCK)],
                                      gsem.at[sl]).wait()

    def out_copy(it, m):
        r0 = pl.multiple_of(item_blk0[it] * MOE_BLOCK, MOE_BLOCK)
        return pltpu.make_async_copy(yp.at[pl.ds(0, m)], y_hbm.at[pl.ds(r0, m)], osem.at[0])

    def wait_out(it):
        for nb in range(1, ITEM_BLOCKS + 1):
            @pl.when(item_nblk[it] == nb)
            def _():
                out_copy(it, nb * MOE_BLOCK).wait()

    @pl.when(j == 0)
    def _():
        @pl.when(i == 0)
        def _():
            start_weight(0, 0)
            start_weight(0, 1)
            start_weight(1, 0)
            start_weight(0, 2)
            issue_gather(0, 0)

        start_weight(i + 1, 2)
        wait_gather(i, slot)

        @pl.when(i + 1 < n_items)
        def _():
            issue_gather(i + 1, 1 - slot)

    @pl.when(j == 1)
    def _():
        start_weight(i + 1, 1)

    @pl.when(j == 2)
    def _():
        start_weight(i + 2, 0)

        @pl.when(i > 0)
        def _():
            wait_out(i - 1)

    for ph in range(3):
        @pl.when(jnp.logical_and(j == ph, nblk > 0))
        def _():
            weight_copy(i, ph).wait()

    for nb in range(1, ITEM_BLOCKS + 1):
        m = nb * MOE_BLOCK

        @pl.when(jnp.logical_and(nblk == nb, j == 0))
        def _():
            hi, lo = _unpack_halves(xg[slot, 0:m, :])
            half = hi.shape[1]
            xb[0:m, 0:half] = hi.astype(BF16)
            xb[0:m, half:2 * half] = lo.astype(BF16)
            gs[0:m, :] = _bdot(xb[0:m, :], gu_buf[(2 * i) % GU_SLOTS].astype(BF16))

        @pl.when(jnp.logical_and(nblk == nb, j == 1))
        def _():
            g = gs[0:m, :]
            u = _bdot(xb[0:m, :], gu_buf[(2 * i + 1) % GU_SLOTS].astype(BF16))
            ab[0:m, :] = (g * jax.nn.sigmoid(g) * u).astype(BF16)

        @pl.when(jnp.logical_and(nblk == nb, j == 2))
        def _():
            yp[0:m, :] = _pack_halves(_bdot(ab[0:m, :], dn_buf[i % DN_SLOTS].astype(BF16)))
            out_copy(i, m).start()

    @pl.when(jnp.logical_and(i == n_items - 1, j == nj - 1))
    def _():
        wait_out(i)
        n_blocks = y_hbm.shape[0] // MOE_BLOCK
        yp[0:MOE_BLOCK, :] = jnp.zeros((MOE_BLOCK, yp.shape[1]), jnp.uint32)

        def tail_copy(blk):
            r0 = pl.multiple_of(blk * MOE_BLOCK, MOE_BLOCK)
            return pltpu.make_async_copy(yp.at[pl.ds(0, MOE_BLOCK)], y_hbm.at[pl.ds(r0, MOE_BLOCK)], osem.at[0])

        def start_body(blk, carry):
            tail_copy(blk).start()
            return carry

        def wait_body(blk, carry):
            tail_copy(blk).wait()
            return carry

        lax.fori_loop(used_blocks[0], n_blocks, start_body, 0)
        lax.fori_loop(used_blocks[0], n_blocks, wait_body, 0)


def _moe(hm, w_gate, w_up, w_down, item_e, item_blk0, item_nblk, row_tok, used_blocks, *, n_rows):
    d, de = w_gate.shape[1], w_gate.shape[2]
    n_items = item_e.shape[0]
    nj = 3
    rows = ITEM_BLOCKS * MOE_BLOCK
    any_spec = pl.BlockSpec(memory_space=pl.ANY)
    grid_spec = pltpu.PrefetchScalarGridSpec(
        num_scalar_prefetch=5,
        grid=(n_items, nj),
        in_specs=[any_spec, any_spec, any_spec, any_spec],
        out_specs=any_spec,
        scratch_shapes=[pltpu.VMEM((2, rows, d // 2), jnp.uint32),
                        pltpu.VMEM((rows, d), BF16),
                        pltpu.VMEM((rows, de), F32),
                        pltpu.VMEM((rows, de), BF16),
                        pltpu.VMEM((rows, d // 2), jnp.uint32),
                        pltpu.VMEM((GU_SLOTS, d, de), F32),
                        pltpu.VMEM((DN_SLOTS, de, d), F32),
                        pltpu.SemaphoreType.DMA((2,)),
                        pltpu.SemaphoreType.DMA((1,)),
                        pltpu.SemaphoreType.DMA((GU_SLOTS,)),
                        pltpu.SemaphoreType.DMA((DN_SLOTS,))],
    )
    return pl.pallas_call(
        _moe_kernel,
        grid_spec=grid_spec,
        out_shape=jax.ShapeDtypeStruct((n_rows, d // 2), jnp.uint32),
        compiler_params=_cparams(("arbitrary", "arbitrary")),
        name="moe_experts",
    )(item_e, item_blk0, item_nblk, row_tok, used_blocks, hm, w_gate, w_up, w_down)


def _combine_kernel(dest, x_ref, gt2_ref, rw_ref, y_hbm, o_ref, ybuf, sem, *, mod_row):
    i = pl.program_id(0)
    n = pl.num_programs(0)
    tm = x_ref.shape[0]
    slot = i % 2

    def issue(it, sl):
        base = it * tm

        def body(r4, carry):
            for rr in range(GATHER_UNROLL // 2):
                r = r4 * (GATHER_UNROLL // 2) + rr
                for k in range(2):
                    pltpu.make_async_copy(y_hbm.at[pl.ds(dest[2 * (base + r) + k], 1)], ybuf.at[sl, k, pl.ds(r, 1)],
                                          sem.at[sl]).start(priority=k)
            return carry

        lax.fori_loop(0, tm // (GATHER_UNROLL // 2), body, 0)

    @pl.when(i == 0)
    def _():
        issue(0, 0)

    for k in range(2):
        pltpu.make_async_copy(y_hbm.at[pl.ds(0, tm)], ybuf.at[slot, k], sem.at[slot]).wait()

    @pl.when(i + 1 < n)
    def _():
        issue(i + 1, 1 - slot)

    w = rw_ref[...]
    gate = gt2_ref[pl.ds(mod_row(i), 1), :]
    hi0, lo0 = _unpack_halves(ybuf[slot, 0])
    hi1, lo1 = _unpack_halves(ybuf[slot, 1])
    half = hi0.shape[1]
    o_ref[:, 0:half] = x_ref[:, 0:half] + gate[:, 0:half] * (w[:, 0:1] * hi0 + w[:, 1:2] * hi1)
    o_ref[:, half:2 * half] = (x_ref[:, half:2 * half]
                               + gate[:, half:2 * half] * (w[:, 0:1] * lo0 + w[:, 1:2] * lo1))


def _combine(dest, xnew, mods, rw, y, *, seq, tm):
    t, d = xnew.shape
    tiles_per_b = seq // tm
    grid_spec = pltpu.PrefetchScalarGridSpec(
        num_scalar_prefetch=1,
        grid=(t // tm,),
        in_specs=[pl.BlockSpec((tm, d), lambda i, ds: (i, 0)),
                  _mod_spec(mods, 5, d),
                  pl.BlockSpec((tm, LANES), lambda i, ds: (i, 0)),
                  pl.BlockSpec(memory_space=pl.ANY)],
        out_specs=pl.BlockSpec((tm, d), lambda i, ds: (i, 0)),
        scratch_shapes=[pltpu.VMEM((2, 2, tm, d // 2), jnp.uint32), pltpu.SemaphoreType.DMA((2,))],
    )
    return pl.pallas_call(
        functools.partial(_combine_kernel, mod_row=lambda i: i // tiles_per_b),
        grid_spec=grid_spec,
        out_shape=jax.ShapeDtypeStruct((t, d), F32),
        compiler_params=_cparams(("arbitrary",)),
        name="moe_combine",
    )(dest, xnew, mods, rw, y)


def _rope_tables(n_tokens):
    rows = n_tokens // GRID_W
    row = jnp.repeat(jnp.arange(rows, dtype=jnp.int32), GRID_W).astype(F32)
    col = jnp.tile(jnp.arange(GRID_W, dtype=jnp.int32), rows).astype(F32)
    n_freq = QK_ROPE_DIM // 4
    inv = ROPE_THETA ** (-jnp.arange(n_freq, dtype=F32) / n_freq)
    ar = row[:, None] * inv[None, :]
    ac = col[:, None] * inv[None, :]
    cr, sr, cc, sc = jnp.cos(ar), jnp.sin(ar), jnp.cos(ac), jnp.sin(ac)
    cos64 = jnp.concatenate([cr, cr, cc, cc], axis=-1)
    sin64 = jnp.concatenate([-sr, sr, -sc, sc], axis=-1)
    return jnp.tile(cos64, (1, 2)), jnp.tile(sin64, (1, 2))


def _channel_dft_table():
    c = np.arange(FOURIER_GROUP_DIM).reshape(-1, 1)
    k = np.arange(FOURIER_GROUP_DIM).reshape(1, -1)
    ang = (2.0 * np.pi / FOURIER_GROUP_DIM) * ((c * k) % FOURIER_GROUP_DIM)
    return jnp.asarray(np.concatenate([np.cos(ang), -np.sin(ang)], axis=1).astype(np.float32)).astype(BF16)


def _split_heads(w, widths):
    k = w.shape[0]
    wh = w.reshape(k, MLA_HEADS, sum(widths))
    outs, off = [], 0
    for wd in widths:
        outs.append(wh[:, :, off:off + wd].reshape(k, MLA_HEADS * wd))
        off += wd
    return outs


def kernel(x, c, ctx, c_ctx, w_ada, b_ada, g_norm1, g_norm2, w_in, g_q_a, g_kv_a, w_uq, w_ukv, g_qk_q, g_qk_k,
           g_out_four, g_out_attn, w_out, w_router_group, b_router_group, w_router_expert, b_router_expert,
           w_gate, w_up, w_down):
    b, s, d = x.shape
    lc = ctx.shape[1]
    t = b * s
    layer_params = (w_ada, b_ada, g_norm1, g_norm2, w_in, g_q_a, g_kv_a, w_uq, w_ukv, g_qk_q, g_qk_k, g_out_four,
                    g_out_attn, w_out, w_router_group, b_router_group, w_router_expert, b_router_expert,
                    w_gate, w_up, w_down)
    assert all(p.shape[0] == 1 for p in layer_params), "single-layer block"
    (w_ada, b_ada, g_norm1, g_norm2, w_in, g_q_a, g_kv_a, w_uq, w_ukv, g_qk_q, g_qk_k, g_out_four,
     g_out_attn, w_out, w_router_group, b_router_group, w_router_expert, b_router_expert,
     w_gate, w_up, w_down) = [p.reshape(p.shape[1:]) for p in layer_params]

    cond8 = jnp.concatenate([c, c_ctx[None, :], jnp.zeros((8 - b - 1, d), F32)], axis=0)
    mods = _ada_mod(cond8, w_ada, b_ada)

    w_all = jnp.concatenate([w_in, w_in[:, ROPE_OFF:]], axis=1).astype(BF16)
    dc = _channel_dft_table()
    wqn, wqr = [w.astype(BF16) for w in _split_heads(w_uq, (QK_NOPE_DIM, QK_ROPE_DIM))]
    wkn, wv = [w.astype(BF16) for w in _split_heads(w_ukv, (QK_NOPE_DIM, V_HEAD_DIM))]
    gqn = g_qk_q[:QK_NOPE_DIM].reshape(1, -1)
    gqr2 = jnp.tile(g_qk_q[QK_NOPE_DIM:], 2).reshape(1, -1)
    gkn = g_qk_k[:QK_NOPE_DIM].reshape(1, -1)
    gkr2 = jnp.tile(g_qk_k[QK_NOPE_DIM:], 2).reshape(1, -1)
    g1 = g_norm1.reshape(1, d)
    gq = g_q_a.reshape(1, -1)
    gkv = g_kv_a.reshape(1, -1)

    x2 = x.reshape(t, d)
    u, cq, ckv, kr2 = _in_proj(x2, mods, 0, s, g1, gq, gkv, w_all, dc, with_q=True, tm=512)
    ckv_c, kr2_c = _in_proj(ctx.reshape(b * lc, d), mods, b, b * lc, g1, gq, gkv, w_all, dc, with_q=False, tm=lc)

    cos_t, sin_t = _rope_tables(s)
    q, kx, vx = _qkv(cq, ckv, kr2, cos_t, sin_t, wqn, wqr, gqn, gqr2, wkn, wv, gkn, gkr2,
                     batch=b, seq=s, tm=512, with_q=True, with_rope=True)
    kc, vc = _qkv(None, ckv_c, kr2_c, None, None, None, None, None, None, wkn, wv, gkn, gkr2,
                  batch=b, seq=lc, tm=lc, with_q=False, with_rope=False)

    attn = _attention(q, kx, kc, vx, vc, tq=1024).reshape(t, MLA_DIM)
    four = _seq_dft(u, b, s)

    wo = w_out.astype(BF16)
    n_route = N_GROUPS + N_EXPERTS
    wrt = jnp.concatenate([w_router_group, w_router_expert, jnp.zeros((d, LANES - n_route), F32)], axis=1)
    wrt_hi = wrt.astype(BF16)
    wrt_lo = (wrt - wrt_hi.astype(F32)).astype(BF16)
    brt = jnp.concatenate([b_router_group, b_router_expert, jnp.zeros((LANES - n_route,), F32)]).reshape(1, -1)
    xnew, hm, ri, rw, cnt = _out_router(x2, four, attn, mods, g_out_four.reshape(1, -1), g_out_attn.reshape(1, -1),
                                        g_norm2.reshape(1, d), wo, wrt_hi, wrt_lo, brt, seq=s, tm=512)

    counts = cnt[0, :N_EXPERTS].astype(jnp.int32)
    nblk_e = (counts + MOE_BLOCK - 1) // MOE_BLOCK
    blk_end = jnp.cumsum(nblk_e)
    blk_start = blk_end - nblk_e
    n_slots = t * 2
    n_blocks = -(-(n_slots + N_EXPERTS * (MOE_BLOCK - 1)) // MOE_BLOCK)
    n_rows = n_blocks * MOE_BLOCK
    e12 = ri[:, 0:2]
    seg_start = jnp.sum(jnp.where(e12[:, :, None] == jnp.arange(N_EXPERTS, dtype=jnp.int32), blk_start, 0), axis=-1)
    dest = (seg_start * MOE_BLOCK + ri[:, 2:4]).reshape(-1)
    tok_ids = jnp.repeat(jnp.arange(t, dtype=jnp.int32), 2)
    row_tok = jnp.zeros((n_rows,), jnp.int32).at[dest].set(tok_ids)
    items_e = (nblk_e + ITEM_BLOCKS - 1) // ITEM_BLOCKS
    item_end = jnp.cumsum(items_e)
    n_items = (n_blocks + (ITEM_BLOCKS - 1) * N_EXPERTS) // ITEM_BLOCKS
    idx = jnp.arange(n_items, dtype=jnp.int32)
    total = item_end[-1]
    idx_c = jnp.minimum(idx, total - 1)
    ie = jnp.minimum(jnp.sum(idx_c[:, None] >= item_end[None, :], axis=1), N_EXPERTS - 1).astype(jnp.int32)
    local = idx_c - (item_end - items_e)[ie]
    item_blk0 = (blk_start[ie] + ITEM_BLOCKS * local).astype(jnp.int32)
    item_nblk = jnp.where(idx < total, jnp.clip(nblk_e[ie] - ITEM_BLOCKS * local, 0, ITEM_BLOCKS), 0).astype(jnp.int32)

    used_blocks = blk_end[-1:].astype(jnp.int32)
    y = _moe(hm, w_gate, w_up, w_down, ie, item_blk0, item_nblk, row_tok, used_blocks, n_rows=n_rows)
    out = _combine(dest.astype(jnp.int32), xnew, mods, rw, y, seq=s, tm=256)
    return out.reshape(b, s, d)
```

```python
import functools
import math

import numpy as np
import jax
import jax.numpy as jnp
from jax import lax
from jax.experimental import pallas as pl
from jax.experimental.pallas import tpu as pltpu

F32 = jnp.float32
BF16 = jnp.bfloat16

D_MODEL = 2048
GRID_W = 64
EPS = 1e-6
N_MOD = 6
N_FOURIER_GROUPS = 4
FOURIER_GROUP_DIM = 256
FOURIER_DIM = 1024
MLA_HEADS = 8
QK_NOPE_DIM = 128
QK_ROPE_DIM = 64
QK_HEAD_DIM = 192
V_HEAD_DIM = 128
Q_LORA_RANK = 768
KV_LORA_RANK = 512
MLA_DIM = 1024
ROPE_THETA = 10000.0
Q_OFF = FOURIER_DIM
KV_OFF = Q_OFF + Q_LORA_RANK
ROPE_OFF = KV_OFF + KV_LORA_RANK
N_GROUPS = 8
EXPERTS_PER_GROUP = 8
N_EXPERTS = 64
D_EXPERT = 768

ROT_HALF = QK_ROPE_DIM // 4
ROT_BLOCK = 2 * ROT_HALF

ADA_COLS = 1024
IN_PROJ_ROWS = 512
QKV_ROWS = 512
ATTN_Q_ROWS = 1024
OUT_ROUTER_ROWS = 512
COMBINE_ROWS = 256

LANES = 128
SUBLANES = 8
DFT_BLOCKS = GRID_W // SUBLANES
HEAD_PAD = 256
V_PAD = 256
MOE_BLOCK = 32
ITEM_BLOCKS = 16
GATHER_UNROLL = 8
GU_SLOTS = 3
DN_SLOTS = 2
ATTN_KEY_CHUNK = 512
ROW_DMA_PRIORITY = 0
WEIGHT_DMA_PRIORITY = 1
VMEM_LIMIT = 56 * 1024 * 1024
NEG_BIG = -1e30


def _cparams(sem):
    return pltpu.CompilerParams(dimension_semantics=sem, vmem_limit_bytes=VMEM_LIMIT)


def _bdot(a, b):
    return jnp.dot(a, b, preferred_element_type=F32)


def _pack_halves(v):
    n = v.shape[1] // 2
    hi = pltpu.bitcast(v[:, :n].astype(BF16).astype(F32), jnp.uint32)
    lo = pltpu.bitcast(v[:, n:].astype(BF16).astype(F32), jnp.uint32)
    return hi | (lo >> 16)


def _unpack_halves(u):
    hi = pltpu.bitcast(u & jnp.uint32(0xFFFF0000), F32)
    lo = pltpu.bitcast(u << 16, F32)
    return hi, lo


def _ada_kernel(c_ref, w_ref, b_ref, o_ref):
    c = c_ref[...]
    s = (c * jax.nn.sigmoid(c)).astype(BF16)
    o_ref[...] = _bdot(s, w_ref[...].astype(BF16)) + b_ref[...]


def _ada_mod(cond8, w_ada, b_ada):
    d, n = w_ada.shape
    tn = ADA_COLS
    return pl.pallas_call(
        _ada_kernel,
        grid=(n // tn,),
        in_specs=[pl.BlockSpec((8, d), lambda i: (0, 0)),
                  pl.BlockSpec((d, tn), lambda i: (0, i)),
                  pl.BlockSpec((1, tn), lambda i: (0, i))],
        out_specs=pl.BlockSpec((8, tn), lambda i: (0, i)),
        out_shape=jax.ShapeDtypeStruct((8, n), F32),
        compiler_params=_cparams(("arbitrary",)),
        name="ada_mod",
    )(cond8, w_ada, b_ada.reshape(1, n))


def _in_proj_kernel(x_ref, sh_ref, sc_ref, g_ref, gq_ref, gkv_ref, w_ref, dc_ref, *out_refs, with_q, mod_row):
    row = pl.ds(mod_row(pl.program_id(0)), 1)
    x = x_ref[...]
    ms = jnp.mean(x * x, axis=-1, keepdims=True)
    a = g_ref[...] * (1.0 + sc_ref[row, :])
    h = (x * lax.rsqrt(ms + EPS) * a + sh_ref[row, :]).astype(BF16)
    if with_q:
        u_ref, cq_ref, ckv_ref, kr_ref = out_refs
        f = _bdot(h, w_ref[:, 0:Q_OFF]).astype(BF16)
        dc = dc_ref[...]
        for g in range(N_FOURIER_GROUPS):
            lo = g * FOURIER_GROUP_DIM
            ug = _bdot(f[:, lo:lo + FOURIER_GROUP_DIM], dc)
            u_ref[:, lo:lo + FOURIER_GROUP_DIM] = ug[:, :FOURIER_GROUP_DIM]
            u_ref[:, FOURIER_DIM + lo:FOURIER_DIM + lo + FOURIER_GROUP_DIM] = ug[:, FOURIER_GROUP_DIM:]
        pq = _bdot(h, w_ref[:, Q_OFF:KV_OFF])
        msq = jnp.mean(pq * pq, axis=-1, keepdims=True)
        cq_ref[...] = (pq * lax.rsqrt(msq + EPS) * gq_ref[...]).astype(BF16)
    else:
        ckv_ref, kr_ref = out_refs
    pkv = _bdot(h, w_ref[:, KV_OFF:ROPE_OFF])
    mskv = jnp.mean(pkv * pkv, axis=-1, keepdims=True)
    ckv_ref[...] = (pkv * lax.rsqrt(mskv + EPS) * gkv_ref[...]).astype(BF16)
    kr_ref[...] = _bdot(h, w_ref[:, ROPE_OFF:ROPE_OFF + LANES])


def _mod_spec(mods, k, d):
    return pl.BlockSpec((mods.shape[0], d), lambda i, *_: (0, k))


def _in_proj(x2, mods, first_row, rows_per_mod, g1, gq, gkv, w_all, dc, *, with_q, tm):
    t, d = x2.shape
    nt = t // tm
    tiles_per_mod = rows_per_mod // tm

    def const(shape):
        return pl.BlockSpec(shape, lambda i: (0,) * len(shape))

    in_specs = [pl.BlockSpec((tm, d), lambda i: (i, 0)), _mod_spec(mods, 0, d), _mod_spec(mods, 1, d),
                const((1, d)), const((1, Q_LORA_RANK)), const((1, KV_LORA_RANK)),
                const(w_all.shape), const(dc.shape)]

    def rows(n):
        return pl.BlockSpec((tm, n), lambda i: (i, 0))

    out_specs = [rows(KV_LORA_RANK), rows(LANES)]
    out_shape = [jax.ShapeDtypeStruct((t, KV_LORA_RANK), BF16), jax.ShapeDtypeStruct((t, LANES), F32)]
    if with_q:
        out_specs = [rows(2 * FOURIER_DIM), rows(Q_LORA_RANK)] + out_specs
        out_shape = [jax.ShapeDtypeStruct((t, 2 * FOURIER_DIM), F32),
                     jax.ShapeDtypeStruct((t, Q_LORA_RANK), BF16)] + out_shape
    return pl.pallas_call(
        functools.partial(_in_proj_kernel, with_q=with_q, mod_row=lambda i: first_row + i // tiles_per_mod),
        grid=(nt,),
        in_specs=in_specs,
        out_specs=out_specs,
        out_shape=out_shape,
        compiler_params=_cparams(("arbitrary",)),
        name="in_proj_x" if with_q else "in_proj_ctx",
    )(x2, mods, mods, g1, gq, gkv, w_all, dc)


def _swap_halves(y, first_half):
    return jnp.where(first_half, pltpu.roll(y, LANES - ROT_HALF, 1), pltpu.roll(y, ROT_HALF, 1))


def _qkv_kernel(*refs, with_q, with_rope):
    it = iter(refs)
    if with_q:
        cq_ref = next(it)
    ckv_ref = next(it)
    kr_ref = next(it)
    if with_rope:
        cos_ref = next(it)
        sin_ref = next(it)
    if with_q:
        wqn_ref = next(it)
        wqr_ref = next(it)
        gqn_ref = next(it)
        gqr_ref = next(it)
    wkn_ref = next(it)
    wv_ref = next(it)
    gkn_ref = next(it)
    gkr_ref = next(it)
    if with_q:
        q_ref = next(it)
    k_ref = next(it)
    v_ref = next(it)

    tm = ckv_ref.shape[0]
    lane = lax.broadcasted_iota(jnp.int32, (tm, LANES), 1)
    low = lane < QK_ROPE_DIM
    first_half = (lane % ROT_BLOCK) < ROT_HALF
    inv_dim = 1.0 / QK_HEAD_DIM

    def rope(y):
        if not with_rope:
            return y
        return y * cos_ref[...] + _swap_halves(y, first_half) * sin_ref[...]

    if with_q:
        cq = cq_ref[...]
        qn = _bdot(cq, wqn_ref[...])
        qr = _bdot(cq, wqr_ref[...])
        qscale = QK_HEAD_DIM ** -0.5
        for p in range(MLA_HEADS // 2):
            blk = qr[:, p * LANES:(p + 1) * LANES]
            sq = blk * blk
            ss_lo = jnp.sum(jnp.where(low, sq, 0.0), axis=-1, keepdims=True)
            ss_hi = jnp.sum(jnp.where(low, 0.0, sq), axis=-1, keepdims=True)
            scales = []
            for hh, ssr in ((2 * p, ss_lo), (2 * p + 1, ss_hi)):
                nh = qn[:, hh * LANES:(hh + 1) * LANES]
                ssq = jnp.sum(nh * nh, axis=-1, keepdims=True) + ssr
                s = lax.rsqrt(ssq * inv_dim + EPS)
                scales.append(s)
                q_ref[hh, :, 0:LANES] = (nh * s * gqn_ref[...] * qscale).astype(BF16)
            s_pair = jnp.where(low, scales[0], scales[1])
            r = rope(blk * s_pair * gqr_ref[...]) * qscale
            q_ref[2 * p, :, LANES:2 * LANES] = jnp.where(low, r, 0.0).astype(BF16)
            q_ref[2 * p + 1, :, LANES:2 * LANES] = jnp.where(low, pltpu.roll(r, QK_ROPE_DIM, 1), 0.0).astype(BF16)

    ckv = ckv_ref[...]
    kn = _bdot(ckv, wkn_ref[...])
    v = _bdot(ckv, wv_ref[...])
    kr = kr_ref[...]
    ss_r = jnp.sum(jnp.where(low, kr * kr, 0.0), axis=-1, keepdims=True)
    base = rope(kr * gkr_ref[...])
    ones_col = jnp.where(lane == 0, 1.0, 0.0).astype(BF16)
    for hh in range(MLA_HEADS):
        nh = kn[:, hh * LANES:(hh + 1) * LANES]
        ssq = jnp.sum(nh * nh, axis=-1, keepdims=True) + ss_r
        s = lax.rsqrt(ssq * inv_dim + EPS)
        k_ref[hh, :, 0:LANES] = (nh * s * gkn_ref[...]).astype(BF16)
        k_ref[hh, :, LANES:2 * LANES] = jnp.where(low, base * s, 0.0).astype(BF16)
        v_ref[hh, :, 0:LANES] = v[:, hh * LANES:(hh + 1) * LANES].astype(BF16)
        v_ref[hh, :, LANES:2 * LANES] = ones_col


def _qkv(cq, ckv, kr2, cos_t, sin_t, wqn, wqr, gqn, gqr2, wkn, wv, gkn, gkr2, *, batch, seq, tm, with_q,
         with_rope):
    t = ckv.shape[0]
    nt = t // tm
    tiles_per_b = seq // tm

    def rows(n):
        return pl.BlockSpec((tm, n), lambda i: (i, 0))

    def const(arr):
        return pl.BlockSpec(arr.shape, lambda i: (0,) * arr.ndim)

    tab_spec = pl.BlockSpec((tm, LANES), lambda i: (i % tiles_per_b, 0))

    def head_out(width):
        return pl.BlockSpec((None, MLA_HEADS, tm, width), lambda i: (i // tiles_per_b, 0, i % tiles_per_b, 0))

    args, in_specs = [], []
    if with_q:
        args.append(cq)
        in_specs.append(rows(Q_LORA_RANK))
    args += [ckv, kr2]
    in_specs += [rows(KV_LORA_RANK), rows(LANES)]
    if with_rope:
        args += [cos_t, sin_t]
        in_specs += [tab_spec, tab_spec]
    if with_q:
        args += [wqn, wqr, gqn, gqr2]
        in_specs += [const(wqn), const(wqr), const(gqn), const(gqr2)]
    args += [wkn, wv, gkn, gkr2]
    in_specs += [const(wkn), const(wv), const(gkn), const(gkr2)]

    out_specs = [head_out(HEAD_PAD), head_out(V_PAD)]
    out_shape = [jax.ShapeDtypeStruct((batch, MLA_HEADS, seq, HEAD_PAD), BF16),
                 jax.ShapeDtypeStruct((batch, MLA_HEADS, seq, V_PAD), BF16)]
    if with_q:
        out_specs = [head_out(HEAD_PAD)] + out_specs
        out_shape = [jax.ShapeDtypeStruct((batch, MLA_HEADS, seq, HEAD_PAD), BF16)] + out_shape
    return pl.pallas_call(
        functools.partial(_qkv_kernel, with_q=with_q, with_rope=with_rope),
        grid=(nt,),
        in_specs=in_specs,
        out_specs=out_specs,
        out_shape=out_shape,
        compiler_params=_cparams(("arbitrary",)),
        name="qkv_x" if with_q else "kv_ctx",
    )(*args)


def _attn_kernel(q_ref, kx_ref, kc_ref, vx_ref, vc_ref, o_ref):
    q = q_ref[...]
    tq = q.shape[0]
    dn = (((1,), (1,)), ((), ()))

    def chunk(k, v, state):
        m, acc = state
        s = lax.dot_general(q, k, dn, preferred_element_type=F32)
        m_new = jnp.maximum(m, jnp.max(s, axis=-1, keepdims=True))
        p = jnp.exp(s - m_new).astype(BF16)
        acc = jnp.exp(m - m_new) * acc + _bdot(p, v)
        return m_new, acc

    state = (jnp.full((tq, 1), NEG_BIG, F32), jnp.zeros((tq, V_PAD), F32))
    for c in range(kx_ref.shape[0] // ATTN_KEY_CHUNK):
        rows = slice(c * ATTN_KEY_CHUNK, (c + 1) * ATTN_KEY_CHUNK)
        state = chunk(kx_ref[rows, :], vx_ref[rows, :], state)
    _, acc = chunk(kc_ref[...], vc_ref[...], state)
    o_ref[...] = acc[:, :V_HEAD_DIM] / acc[:, V_HEAD_DIM:V_HEAD_DIM + 1]


def _attention(q, kx, kc, vx, vc, *, tq):
    b, h, s, _ = q.shape
    lc = kc.shape[2]
    return pl.pallas_call(
        _attn_kernel,
        grid=(b, h, s // tq),
        in_specs=[pl.BlockSpec((None, None, tq, HEAD_PAD), lambda bi, hi, qi: (bi, hi, qi, 0)),
                  pl.BlockSpec((None, None, s, HEAD_PAD), lambda bi, hi, qi: (bi, hi, 0, 0)),
                  pl.BlockSpec((None, None, lc, HEAD_PAD), lambda bi, hi, qi: (bi, hi, 0, 0)),
                  pl.BlockSpec((None, None, s, V_PAD), lambda bi, hi, qi: (bi, hi, 0, 0)),
                  pl.BlockSpec((None, None, lc, V_PAD), lambda bi, hi, qi: (bi, hi, 0, 0))],
        out_specs=pl.BlockSpec((None, tq, V_HEAD_DIM), lambda bi, hi, qi: (bi, qi, hi)),
        out_shape=jax.ShapeDtypeStruct((b, s, h * V_HEAD_DIM), F32),
        compiler_params=_cparams(("arbitrary", "arbitrary", "arbitrary")),
        name="attention",
    )(q, kx, kc, vx, vc)


def _seq_dft_kernel(ure_ref, uim_ref, r_ref, e_ref, t2_ref, o_ref, a_ref):
    s = pl.program_id(2)
    sub = SUBLANES
    n = 2 * GRID_W * sub
    cols = ure_ref.shape[-1]

    @pl.when(s < DFT_BLOCKS)
    def _():
        t = _bdot(r_ref[...].astype(BF16), e_ref[...])
        row = lax.broadcasted_iota(jnp.int32, (n, n), 0)
        col = lax.broadcasted_iota(jnp.int32, (n, n), 1)
        t = jnp.where((row % sub) == (col % sub), t, 0.0).astype(BF16)
        rhs = jnp.concatenate([ure_ref[...].reshape(GRID_W * sub, cols), uim_ref[...].reshape(GRID_W * sub, cols)],
                              axis=0).astype(BF16)
        a = _bdot(t, rhs)
        a_ref[:, :, pl.ds(pl.multiple_of(s * sub, sub), sub), :] = a.reshape(2, GRID_W, sub, cols)

    @pl.when(s >= DFT_BLOCKS)
    def _():
        k0 = pl.multiple_of((s - DFT_BLOCKS) * sub, sub)
        rhs = a_ref[:, pl.ds(k0, sub), :, :].reshape(2 * sub * GRID_W, cols).astype(BF16)
        y = _bdot(t2_ref[...].astype(BF16), rhs)
        o_ref[...] = y.reshape(GRID_W, sub, cols)


def _seq_dft_tables(n_seq):
    w, sub, nb = GRID_W, SUBLANES, DFT_BLOCKS
    ch = np.arange(nb).reshape(nb, 1, 1, 1)
    kb = np.arange(w).reshape(1, w, 1, 1)
    j = np.arange(sub).reshape(1, 1, sub, 1)
    r = np.arange(w).reshape(1, 1, 1, w)
    ang = (2.0 * np.pi / n_seq) * ((kb * (w * r + sub * ch + j)) % n_seq)
    c, s = np.cos(ang), np.sin(ang)
    rot = np.stack([np.stack([c, s], axis=3), np.stack([-s, c], axis=3)], axis=1)
    r1 = rot.reshape(nb, 2 * w * sub, 2 * w).astype(np.float32)
    expand = (np.arange(2 * w * sub)[None, :] // sub == np.arange(2 * w)[:, None]).astype(np.float32)
    ka = np.arange(w).reshape(w, 1)
    cp = np.arange(w).reshape(1, w)
    ang2 = (2.0 * np.pi / w) * ((ka * cp) % w)
    norm = 1.0 / math.sqrt(n_seq * FOURIER_GROUP_DIM)
    cs = np.stack([np.cos(ang2), np.sin(ang2)], axis=1) * norm
    eye = np.eye(sub)
    t2 = (cs[:, None, :, None, :] * eye[None, :, None, :, None]).reshape(w * sub, 2 * sub * w).astype(np.float32)
    return jnp.asarray(r1), jnp.asarray(expand).astype(BF16), jnp.asarray(t2)


def _seq_dft(u, batch, n_seq):
    assert n_seq == GRID_W * GRID_W
    w, sub, nb = GRID_W, SUBLANES, DFT_BLOCKS
    r1, expand, t2 = _seq_dft_tables(n_seq)
    halves = 2
    cols = FOURIER_DIM // halves
    u5 = u.reshape(batch, w, nb, sub, 2 * FOURIER_DIM)

    def u_spec(part):
        return pl.BlockSpec((None, w, None, sub, cols),
                            lambda b, h, s: (b, 0, jnp.minimum(s, nb - 1), 0, part * halves + h))

    y = pl.pallas_call(
        _seq_dft_kernel,
        grid=(batch, halves, 2 * nb),
        in_specs=[u_spec(0), u_spec(1),
                  pl.BlockSpec((None, 2 * w * sub, 2 * w), lambda b, h, s: (jnp.minimum(s, nb - 1), 0, 0)),
                  pl.BlockSpec((2 * w, 2 * w * sub), lambda b, h, s: (0, 0)),
                  pl.BlockSpec((w * sub, 2 * sub * w), lambda b, h, s: (0, 0))],
        out_specs=pl.BlockSpec((None, w, None, sub, cols), lambda b, h, s: (b, 0, jnp.maximum(s - nb, 0), 0, h)),
        out_shape=jax.ShapeDtypeStruct((batch, w, nb, sub, FOURIER_DIM), F32),
        scratch_shapes=[pltpu.VMEM((2, w, w, cols), F32)],
        compiler_params=_cparams(("arbitrary", "arbitrary", "arbitrary")),
        name="seq_dft",
    )(u5, u5, r1, expand, t2)
    return y.reshape(batch * n_seq, FOURIER_DIM)


def _out_router_kernel(x_ref, four_ref, attn_ref, gt1_ref, sh2_ref, sc2_ref, gf_ref, ga_ref, g2_ref,
                       wo_ref, wrh_ref, wrl_ref, br_ref,
                       xnew_ref, hm_ref, ri_ref, rw_ref, cnt_ref, carry_ref, *, mod_row):
    i = pl.program_id(0)
    tm = x_ref.shape[0]
    row = pl.ds(mod_row(i), 1)

    @pl.when(i == 0)
    def _():
        carry_ref[...] = jnp.zeros_like(carry_ref)

    def norm(v, g):
        return (v * lax.rsqrt(jnp.mean(v * v, axis=-1, keepdims=True) + EPS) * g).astype(BF16)

    mix = (_bdot(norm(four_ref[...], gf_ref[...]), wo_ref[0:FOURIER_DIM, :])
           + _bdot(norm(attn_ref[...], ga_ref[...]), wo_ref[FOURIER_DIM:FOURIER_DIM + MLA_DIM, :]))
    xn = x_ref[...] + gt1_ref[row, :] * mix
    xnew_ref[...] = xn
    ms = jnp.mean(xn * xn, axis=-1, keepdims=True)
    hm = xn * lax.rsqrt(ms + EPS) * (g2_ref[...] * (1.0 + sc2_ref[row, :])) + sh2_ref[row, :]
    hm_ref[...] = _pack_halves(hm)

    hm_hi = hm.astype(BF16)
    hm_lo = (hm - hm_hi.astype(F32)).astype(BF16)
    logits = _bdot(hm_hi, wrh_ref[...]) + _bdot(hm_lo, wrh_ref[...]) + _bdot(hm_hi, wrl_ref[...]) + br_ref[...]
    lane = lax.broadcasted_iota(jnp.int32, (tm, LANES), 1)
    lanef = lane.astype(F32)
    far = 1e9

    lg = jnp.where(lane < N_GROUPS, logits, NEG_BIG)
    m1 = jnp.max(lg, axis=-1, keepdims=True)
    g_p = 1.0 / jnp.sum(jnp.exp(lg - m1), axis=-1, keepdims=True)
    gidx = jnp.min(jnp.where(lg >= m1, lanef, far), axis=-1, keepdims=True)
    lo = N_GROUPS + EXPERTS_PER_GROUP * gidx
    in_group = jnp.where(lanef >= lo, jnp.where(lanef < lo + EXPERTS_PER_GROUP, 1.0, 0.0), 0.0) > 0.5
    le = jnp.where(in_group, logits, NEG_BIG)
    m2 = jnp.max(le, axis=-1, keepdims=True)
    idx1 = jnp.min(jnp.where(le >= m2, lanef, far), axis=-1, keepdims=True)
    le2 = jnp.where(lanef == idx1, NEG_BIG, le)
    m3 = jnp.max(le2, axis=-1, keepdims=True)
    idx2 = jnp.min(jnp.where(le2 >= m3, lanef, far), axis=-1, keepdims=True)
    t = jnp.exp(m3 - m2)
    p1 = 1.0 / (1.0 + t)
    p2 = t / (1.0 + t)
    e1 = idx1 - N_GROUPS
    e2 = idx2 - N_GROUPS

    oh1 = jnp.where(lanef == e1, 1.0, 0.0)
    oh2 = jnp.where(lanef == e2, 1.0, 0.0)
    ohs = oh1 + oh2
    row = lax.broadcasted_iota(jnp.int32, (tm, tm), 0)
    col = lax.broadcasted_iota(jnp.int32, (tm, tm), 1)
    tri = jnp.where(row > col, 1.0, 0.0).astype(BF16)
    before = _bdot(tri, ohs.astype(BF16)) + carry_ref[...]
    rank1 = jnp.sum(oh1 * before, axis=-1, keepdims=True)
    rank2 = jnp.sum(oh2 * before, axis=-1, keepdims=True)
    carry = carry_ref[...] + jnp.sum(ohs, axis=0, keepdims=True)
    carry_ref[...] = carry
    cnt_ref[...] = jnp.broadcast_to(carry, cnt_ref.shape)

    ri = jnp.where(lane == 0, e1, jnp.where(lane == 1, e2, jnp.where(lane == 2, rank1, jnp.where(lane == 3, rank2, 0.0))))
    ri_ref[...] = ri.astype(jnp.int32)
    rw_ref[...] = jnp.where(lane == 0, g_p * p1, jnp.where(lane == 1, g_p * p2, 0.0))


def _out_router(x2, four, attn, mods, gf, ga, g2, wo, wrh, wrl, br, *, seq, tm):
    t, d = x2.shape
    nt = t // tm
    tiles_per_b = seq // tm

    def rows(n):
        return pl.BlockSpec((tm, n), lambda i: (i, 0))

    def const(arr):
        return pl.BlockSpec(arr.shape, lambda i: (0,) * arr.ndim, pipeline_mode=pl.Buffered(1))

    return pl.pallas_call(
        functools.partial(_out_router_kernel, mod_row=lambda i: i // tiles_per_b),
        grid=(nt,),
        in_specs=[rows(d), rows(FOURIER_DIM), rows(MLA_DIM), _mod_spec(mods, 2, d), _mod_spec(mods, 3, d),
                  _mod_spec(mods, 4, d),
                  const(gf), const(ga), const(g2), const(wo), const(wrh), const(wrl), const(br)],
        out_specs=[rows(d), rows(d // 2), rows(LANES), rows(LANES), pl.BlockSpec((8, LANES), lambda i: (0, 0))],
        out_shape=[jax.ShapeDtypeStruct((t, d), F32), jax.ShapeDtypeStruct((t, d // 2), jnp.uint32),
                   jax.ShapeDtypeStruct((t, LANES), jnp.int32), jax.ShapeDtypeStruct((t, LANES), F32),
                   jax.ShapeDtypeStruct((8, LANES), F32)],
        scratch_shapes=[pltpu.VMEM((1, LANES), F32)],
        compiler_params=_cparams(("arbitrary",)),
        name="out_proj_router",
    )(x2, four, attn, mods, mods, mods, gf, ga, g2, wo, wrh, wrl, br)


def _moe_kernel(item_e, item_blk0, item_nblk, row_tok, used_blocks,
                hm_hbm, wg_hbm, wu_hbm, wd_hbm, y_hbm,
                xg, xb, gs, ab, yp, gu_buf, dn_buf, gsem, osem, gusem, dnsem):
    i = pl.program_id(0)
    j = pl.program_id(1)
    n_items = pl.num_programs(0)
    nj = pl.num_programs(1)
    slot = i % 2
    nblk = item_nblk[i]

    def weight_copy(it, ph):
        if ph == 2:
            ws = it % DN_SLOTS
            return pltpu.make_async_copy(wd_hbm.at[item_e[it]], dn_buf.at[ws], dnsem.at[ws])
        ws = (2 * it + ph) % GU_SLOTS
        return pltpu.make_async_copy((wg_hbm, wu_hbm)[ph].at[item_e[it]], gu_buf.at[ws], gusem.at[ws])

    def start_weight(it, ph):
        it_c = jnp.minimum(it, n_items - 1)

        @pl.when(jnp.logical_and(it < n_items, item_nblk[it_c] > 0))
        def _():
            weight_copy(it_c, ph).start(priority=WEIGHT_DMA_PRIORITY)

    def gather_copy(tok, sl, r):
        return pltpu.make_async_copy(hm_hbm.at[pl.ds(tok, 1)], xg.at[sl, pl.ds(r, 1)], gsem.at[sl])

    def issue_gather(it, sl):
        r0 = item_blk0[it] * MOE_BLOCK

        def body(r8, carry):
            for k in range(GATHER_UNROLL):
                r = r8 * GATHER_UNROLL + k
                gather_copy(row_tok[r0 + r], sl, r).start(priority=ROW_DMA_PRIORITY)
            return carry

        lax.fori_loop(0, item_nblk[it] * (MOE_BLOCK // GATHER_UNROLL), body, 0)

    def wait_gather(it, sl):
        for b in range(ITEM_BLOCKS):
            @pl.when(b < item_nblk[it])
            def _():
                pltpu.make_async_copy(hm_hbm.at[pl.ds(0, MOE_BLOCK)], xg.at[sl, pl.ds(b * MOE_BLOCK, MOE_BLOCK)],
                                      gsem.at[sl]).wait()

    def out_copy(it, m):
        r0 = pl.multiple_of(item_blk0[it] * MOE_BLOCK, MOE_BLOCK)
        return pltpu.make_async_copy(yp.at[pl.ds(0, m)], y_hbm.at[pl.ds(r0, m)], osem.at[0])

    def wait_out(it):
        for nb in range(1, ITEM_BLOCKS + 1):
            @pl.when(item_nblk[it] == nb)
            def _():
                out_copy(it, nb * MOE_BLOCK).wait()

    @pl.when(j == 0)
    def _():
        @pl.when(i == 0)
        def _():
            start_weight(0, 0)
            start_weight(0, 1)
            start_weight(1, 0)
            start_weight(0, 2)
            issue_gather(0, 0)

        start_weight(i + 1, 2)
        wait_gather(i, slot)

        @pl.when(i + 1 < n_items)
        def _():
            issue_gather(i + 1, 1 - slot)

    @pl.when(j == 1)
    def _():
        start_weight(i + 1, 1)

    @pl.when(j == 2)
    def _():
        start_weight(i + 2, 0)

        @pl.when(i > 0)
        def _():
            wait_out(i - 1)

    for ph in range(3):
        @pl.when(jnp.logical_and(j == ph, nblk > 0))
        def _():
            weight_copy(i, ph).wait()

    for nb in range(1, ITEM_BLOCKS + 1):
        m = nb * MOE_BLOCK

        @pl.when(jnp.logical_and(nblk == nb, j == 0))
        def _():
            hi, lo = _unpack_halves(xg[slot, 0:m, :])
            half = hi.shape[1]
            xb[0:m, 0:half] = hi.astype(BF16)
            xb[0:m, half:2 * half] = lo.astype(BF16)
            gs[0:m, :] = _bdot(xb[0:m, :], gu_buf[(2 * i) % GU_SLOTS].astype(BF16))

        @pl.when(jnp.logical_and(nblk == nb, j == 1))
        def _():
            g = gs[0:m, :]
            u = _bdot(xb[0:m, :], gu_buf[(2 * i + 1) % GU_SLOTS].astype(BF16))
            ab[0:m, :] = (g * jax.nn.sigmoid(g) * u).astype(BF16)

        @pl.when(jnp.logical_and(nblk == nb, j == 2))
        def _():
            yp[0:m, :] = _pack_halves(_bdot(ab[0:m, :], dn_buf[i % DN_SLOTS].astype(BF16)))
            out_copy(i, m).start()

    @pl.when(jnp.logical_and(i == n_items - 1, j == nj - 1))
    def _():
        wait_out(i)
        n_blocks = y_hbm.shape[0] // MOE_BLOCK
        yp[0:MOE_BLOCK, :] = jnp.zeros((MOE_BLOCK, yp.shape[1]), jnp.uint32)

        def tail_copy(blk):
            r0 = pl.multiple_of(blk * MOE_BLOCK, MOE_BLOCK)
            return pltpu.make_async_copy(yp.at[pl.ds(0, MOE_BLOCK)], y_hbm.at[pl.ds(r0, MOE_BLOCK)], osem.at[0])

        def start_body(blk, carry):
            tail_copy(blk).start()
            return carry

        def wait_body(blk, carry):
            tail_copy(blk).wait()
            return carry

        lax.fori_loop(used_blocks[0], n_blocks, start_body, 0)
        lax.fori_loop(used_blocks[0], n_blocks, wait_body, 0)


def _moe(hm, w_gate, w_up, w_down, item_e, item_blk0, item_nblk, row_tok, used_blocks, *, n_rows):
    d, de = w_gate.shape[1], w_gate.shape[2]
    n_items = item_e.shape[0]
    nj = 3
    rows = ITEM_BLOCKS * MOE_BLOCK
    any_spec = pl.BlockSpec(memory_space=pl.ANY)
    grid_spec = pltpu.PrefetchScalarGridSpec(
        num_scalar_prefetch=5,
        grid=(n_items, nj),
        in_specs=[any_spec, any_spec, any_spec, any_spec],
        out_specs=any_spec,
        scratch_shapes=[pltpu.VMEM((2, rows, d // 2), jnp.uint32),
                        pltpu.VMEM((rows, d), BF16),
                        pltpu.VMEM((rows, de), F32),
                        pltpu.VMEM((rows, de), BF16),
                        pltpu.VMEM((rows, d // 2), jnp.uint32),
                        pltpu.VMEM((GU_SLOTS, d, de), F32),
                        pltpu.VMEM((DN_SLOTS, de, d), F32),
                        pltpu.SemaphoreType.DMA((2,)),
                        pltpu.SemaphoreType.DMA((1,)),
                        pltpu.SemaphoreType.DMA((GU_SLOTS,)),
                        pltpu.SemaphoreType.DMA((DN_SLOTS,))],
    )
    return pl.pallas_call(
        _moe_kernel,
        grid_spec=grid_spec,
        out_shape=jax.ShapeDtypeStruct((n_rows, d // 2), jnp.uint32),
        compiler_params=_cparams(("arbitrary", "arbitrary")),
        name="moe_experts",
    )(item_e, item_blk0, item_nblk, row_tok, used_blocks, hm, w_gate, w_up, w_down)


def _combine_kernel(dest, x_ref, gt2_ref, rw_ref, y_hbm, o_ref, ybuf, sem, *, mod_row):
    i = pl.program_id(0)
    n = pl.num_programs(0)
    tm = x_ref.shape[0]
    slot = i % 2

    def issue(it, sl):
        base = it * tm

        def body(r4, carry):
            for rr in range(GATHER_UNROLL // 2):
                r = r4 * (GATHER_UNROLL // 2) + rr
                for k in range(2):
                    pltpu.make_async_copy(y_hbm.at[pl.ds(dest[2 * (base + r) + k], 1)], ybuf.at[sl, k, pl.ds(r, 1)],
                                          sem.at[sl]).start(priority=k)
            return carry

        lax.fori_loop(0, tm // (GATHER_UNROLL // 2), body, 0)

    @pl.when(i == 0)
    def _():
        issue(0, 0)

    for k in range(2):
        pltpu.make_async_copy(y_hbm.at[pl.ds(0, tm)], ybuf.at[slot, k], sem.at[slot]).wait()

    @pl.when(i + 1 < n)
    def _():
        issue(i + 1, 1 - slot)

    w = rw_ref[...]
    gate = gt2_ref[pl.ds(mod_row(i), 1), :]
    hi0, lo0 = _unpack_halves(ybuf[slot, 0])
    hi1, lo1 = _unpack_halves(ybuf[slot, 1])
    half = hi0.shape[1]
    o_ref[:, 0:half] = x_ref[:, 0:half] + gate[:, 0:half] * (w[:, 0:1] * hi0 + w[:, 1:2] * hi1)
    o_ref[:, half:2 * half] = (x_ref[:, half:2 * half]
                               + gate[:, half:2 * half] * (w[:, 0:1] * lo0 + w[:, 1:2] * lo1))


def _combine(dest, xnew, mods, rw, y, *, seq, tm):
    t, d = xnew.shape
    tiles_per_b = seq // tm
    grid_spec = pltpu.PrefetchScalarGridSpec(
        num_scalar_prefetch=1,
        grid=(t // tm,),
        in_specs=[pl.BlockSpec((tm, d), lambda i, ds: (i, 0)),
                  _mod_spec(mods, 5, d),
                  pl.BlockSpec((tm, LANES), lambda i, ds: (i, 0)),
                  pl.BlockSpec(memory_space=pl.ANY)],
        out_specs=pl.BlockSpec((tm, d), lambda i, ds: (i, 0)),
        scratch_shapes=[pltpu.VMEM((2, 2, tm, d // 2), jnp.uint32), pltpu.SemaphoreType.DMA((2,))],
    )
    return pl.pallas_call(
        functools.partial(_combine_kernel, mod_row=lambda i: i // tiles_per_b),
        grid_spec=grid_spec,
        out_shape=jax.ShapeDtypeStruct((t, d), F32),
        compiler_params=_cparams(("arbitrary",)),
        name="moe_combine",
    )(dest, xnew, mods, rw, y)


def _rope_tables(n_tokens):
    rows = n_tokens // GRID_W
    row = jnp.repeat(jnp.arange(rows, dtype=jnp.int32), GRID_W).astype(F32)
    col = jnp.tile(jnp.arange(GRID_W, dtype=jnp.int32), rows).astype(F32)
    n_freq = QK_ROPE_DIM // 4
    inv = ROPE_THETA ** (-jnp.arange(n_freq, dtype=F32) / n_freq)
    ar = row[:, None] * inv[None, :]
    ac = col[:, None] * inv[None, :]
    cr, sr, cc, sc = jnp.cos(ar), jnp.sin(ar), jnp.cos(ac), jnp.sin(ac)
    cos64 = jnp.concatenate([cr, cr, cc, cc], axis=-1)
    sin64 = jnp.concatenate([-sr, sr, -sc, sc], axis=-1)
    return jnp.tile(cos64, (1, 2)), jnp.tile(sin64, (1, 2))


def _channel_dft_table():
    c = np.arange(FOURIER_GROUP_DIM).reshape(-1, 1)
    k = np.arange(FOURIER_GROUP_DIM).reshape(1, -1)
    ang = (2.0 * np.pi / FOURIER_GROUP_DIM) * ((c * k) % FOURIER_GROUP_DIM)
    return jnp.asarray(np.concatenate([np.cos(ang), -np.sin(ang)], axis=1).astype(np.float32)).astype(BF16)


def _split_heads(w, widths):
    k = w.shape[0]
    wh = w.reshape(k, MLA_HEADS, sum(widths))
    outs, off = [], 0
    for wd in widths:
        outs.append(wh[:, :, off:off + wd].reshape(k, MLA_HEADS * wd))
        off += wd
    return outs


def kernel(x, c, ctx, c_ctx, w_ada, b_ada, g_norm1, g_norm2, w_in, g_q_a, g_kv_a, w_uq, w_ukv, g_qk_q, g_qk_k,
           g_out_four, g_out_attn, w_out, w_router_group, b_router_group, w_router_expert, b_router_expert,
           w_gate, w_up, w_down):
    b, s, d = x.shape
    lc = ctx.shape[1]
    t = b * s
    layer_params = (w_ada, b_ada, g_norm1, g_norm2, w_in, g_q_a, g_kv_a, w_uq, w_ukv, g_qk_q, g_qk_k, g_out_four,
                    g_out_attn, w_out, w_router_group, b_router_group, w_router_expert, b_router_expert,
                    w_gate, w_up, w_down)
    assert all(p.shape[0] == 1 for p in layer_params), "single-layer block"
    (w_ada, b_ada, g_norm1, g_norm2, w_in, g_q_a, g_kv_a, w_uq, w_ukv, g_qk_q, g_qk_k, g_out_four,
     g_out_attn, w_out, w_router_group, b_router_group, w_router_expert, b_router_expert,
     w_gate, w_up, w_down) = [p.reshape(p.shape[1:]) for p in layer_params]

    cond8 = jnp.concatenate([c, c_ctx[None, :], jnp.zeros((8 - b - 1, d), F32)], axis=0)
    mods = _ada_mod(cond8, w_ada, b_ada)

    w_all = jnp.concatenate([w_in, w_in[:, ROPE_OFF:]], axis=1).astype(BF16)
    dc = _channel_dft_table()
    wqn, wqr = [w.astype(BF16) for w in _split_heads(w_uq, (QK_NOPE_DIM, QK_ROPE_DIM))]
    wkn, wv = [w.astype(BF16) for w in _split_heads(w_ukv, (QK_NOPE_DIM, V_HEAD_DIM))]
    gqn = g_qk_q[:QK_NOPE_DIM].reshape(1, -1)
    gqr2 = jnp.tile(g_qk_q[QK_NOPE_DIM:], 2).reshape(1, -1)
    gkn = g_qk_k[:QK_NOPE_DIM].reshape(1, -1)
    gkr2 = jnp.tile(g_qk_k[QK_NOPE_DIM:], 2).reshape(1, -1)
    g1 = g_norm1.reshape(1, d)
    gq = g_q_a.reshape(1, -1)
    gkv = g_kv_a.reshape(1, -1)

    x2 = x.reshape(t, d)
    u, cq, ckv, kr2 = _in_proj(x2, mods, 0, s, g1, gq, gkv, w_all, dc, with_q=True, tm=IN_PROJ_ROWS)
    ckv_c, kr2_c = _in_proj(ctx.reshape(b * lc, d), mods, b, b * lc, g1, gq, gkv, w_all, dc, with_q=False, tm=lc)

    cos_t, sin_t = _rope_tables(s)
    q, kx, vx = _qkv(cq, ckv, kr2, cos_t, sin_t, wqn, wqr, gqn, gqr2, wkn, wv, gkn, gkr2,
                     batch=b, seq=s, tm=QKV_ROWS, with_q=True, with_rope=True)
    kc, vc = _qkv(None, ckv_c, kr2_c, None, None, None, None, None, None, wkn, wv, gkn, gkr2,
                  batch=b, seq=lc, tm=lc, with_q=False, with_rope=False)

    attn = _attention(q, kx, kc, vx, vc, tq=ATTN_Q_ROWS).reshape(t, MLA_DIM)
    four = _seq_dft(u, b, s)

    wo = w_out.astype(BF16)
    n_route = N_GROUPS + N_EXPERTS
    wrt = jnp.concatenate([w_router_group, w_router_expert, jnp.zeros((d, LANES - n_route), F32)], axis=1)
    wrt_hi = wrt.astype(BF16)
    wrt_lo = (wrt - wrt_hi.astype(F32)).astype(BF16)
    brt = jnp.concatenate([b_router_group, b_router_expert, jnp.zeros((LANES - n_route,), F32)]).reshape(1, -1)
    xnew, hm, ri, rw, cnt = _out_router(x2, four, attn, mods, g_out_four.reshape(1, -1), g_out_attn.reshape(1, -1),
                                        g_norm2.reshape(1, d), wo, wrt_hi, wrt_lo, brt, seq=s, tm=OUT_ROUTER_ROWS)

    counts = cnt[0, :N_EXPERTS].astype(jnp.int32)
    nblk_e = (counts + MOE_BLOCK - 1) // MOE_BLOCK
    blk_end = jnp.cumsum(nblk_e)
    blk_start = blk_end - nblk_e
    n_slots = t * 2
    n_blocks = -(-(n_slots + N_EXPERTS * (MOE_BLOCK - 1)) // MOE_BLOCK)
    n_rows = n_blocks * MOE_BLOCK
    e12 = ri[:, 0:2]
    seg_start = jnp.sum(jnp.where(e12[:, :, None] == jnp.arange(N_EXPERTS, dtype=jnp.int32), blk_start, 0), axis=-1)
    dest = (seg_start * MOE_BLOCK + ri[:, 2:4]).reshape(-1)
    tok_ids = jnp.repeat(jnp.arange(t, dtype=jnp.int32), 2)
    row_tok = jnp.zeros((n_rows,), jnp.int32).at[dest].set(tok_ids)
    items_e = (nblk_e + ITEM_BLOCKS - 1) // ITEM_BLOCKS
    item_end = jnp.cumsum(items_e)
    n_items = (n_blocks + (ITEM_BLOCKS - 1) * N_EXPERTS) // ITEM_BLOCKS
    idx = jnp.arange(n_items, dtype=jnp.int32)
    total = item_end[-1]
    idx_c = jnp.minimum(idx, total - 1)
    ie = jnp.minimum(jnp.sum(idx_c[:, None] >= item_end[None, :], axis=1), N_EXPERTS - 1).astype(jnp.int32)
    local = idx_c - (item_end - items_e)[ie]
    item_blk0 = (blk_start[ie] + ITEM_BLOCKS * local).astype(jnp.int32)
    item_nblk = jnp.where(idx < total, jnp.clip(nblk_e[ie] - ITEM_BLOCKS * local, 0, ITEM_BLOCKS), 0).astype(jnp.int32)

    used_blocks = blk_end[-1:].astype(jnp.int32)
    y = _moe(hm, w_gate, w_up, w_down, ie, item_blk0, item_nblk, row_tok, used_blocks, n_rows=n_rows)
    out = _combine(dest.astype(jnp.int32), xnew, mods, rw, y, seq=s, tm=COMBINE_ROWS)
    return out.reshape(b, s, d)
```

```python
import functools
import math

import numpy as np
import jax
import jax.numpy as jnp
from jax import lax
from jax.experimental import pallas as pl
from jax.experimental.pallas import tpu as pltpu

F32 = jnp.float32
BF16 = jnp.bfloat16

D_MODEL = 2048
GRID_W = 64
EPS = 1e-6
N_MOD = 6
N_FOURIER_GROUPS = 4
FOURIER_GROUP_DIM = 256
FOURIER_DIM = 1024
MLA_HEADS = 8
QK_NOPE_DIM = 128
QK_ROPE_DIM = 64
QK_HEAD_DIM = 192
V_HEAD_DIM = 128
Q_LORA_RANK = 768
KV_LORA_RANK = 512
MLA_DIM = 1024
ROPE_THETA = 10000.0
Q_OFF = FOURIER_DIM
KV_OFF = Q_OFF + Q_LORA_RANK
ROPE_OFF = KV_OFF + KV_LORA_RANK
N_GROUPS = 8
EXPERTS_PER_GROUP = 8
N_EXPERTS = 64
D_EXPERT = 768

ROT_HALF = QK_ROPE_DIM // 4
ROT_BLOCK = 2 * ROT_HALF

ADA_COLS = 1024
IN_PROJ_ROWS = 512
QKV_ROWS = 512
ATTN_Q_ROWS = 1024
OUT_ROUTER_ROWS = 512
COMBINE_ROWS = 256

LANES = 128
SUBLANES = 8
DFT_BLOCKS = GRID_W // SUBLANES
HEAD_PAD = 256
V_PAD = 256
MOE_BLOCK = 64
ITEM_BLOCKS = 6
GATHER_UNROLL = 8
GU_SLOTS = 3
DN_SLOTS = 2
ATTN_KEY_CHUNK = 512
ROW_DMA_PRIORITY = 0
WEIGHT_DMA_PRIORITY = 1
VMEM_LIMIT = 56 * 1024 * 1024
NEG_BIG = -1e30


def _cparams(sem):
    return pltpu.CompilerParams(dimension_semantics=sem, vmem_limit_bytes=VMEM_LIMIT)


def _bdot(a, b):
    return jnp.dot(a, b, preferred_element_type=F32)


def _pack_halves(v):
    n = v.shape[1] // 2
    hi = pltpu.bitcast(v[:, :n].astype(BF16).astype(F32), jnp.uint32)
    lo = pltpu.bitcast(v[:, n:].astype(BF16).astype(F32), jnp.uint32)
    return hi | (lo >> 16)


def _unpack_halves(u):
    hi = pltpu.bitcast(u & jnp.uint32(0xFFFF0000), F32)
    lo = pltpu.bitcast(u << 16, F32)
    return hi, lo


def _ada_kernel(c_ref, w_ref, b_ref, o_ref):
    c = c_ref[...]
    s = (c * jax.nn.sigmoid(c)).astype(BF16)
    o_ref[...] = _bdot(s, w_ref[...].astype(BF16)) + b_ref[...]


def _ada_mod(cond8, w_ada, b_ada):
    d, n = w_ada.shape
    tn = ADA_COLS
    return pl.pallas_call(
        _ada_kernel,
        grid=(n // tn,),
        in_specs=[pl.BlockSpec((8, d), lambda i: (0, 0)),
                  pl.BlockSpec((d, tn), lambda i: (0, i)),
                  pl.BlockSpec((1, tn), lambda i: (0, i))],
        out_specs=pl.BlockSpec((8, tn), lambda i: (0, i)),
        out_shape=jax.ShapeDtypeStruct((8, n), F32),
        compiler_params=_cparams(("arbitrary",)),
        name="ada_mod",
    )(cond8, w_ada, b_ada.reshape(1, n))


def _in_proj_kernel(x_ref, sh_ref, sc_ref, g_ref, gq_ref, gkv_ref, w_ref, dc_ref, *out_refs, with_q, mod_row):
    row = pl.ds(mod_row(pl.program_id(0)), 1)
    x = x_ref[...]
    ms = jnp.mean(x * x, axis=-1, keepdims=True)
    a = g_ref[...] * (1.0 + sc_ref[row, :])
    h = (x * lax.rsqrt(ms + EPS) * a + sh_ref[row, :]).astype(BF16)
    if with_q:
        u_ref, cq_ref, ckv_ref, kr_ref = out_refs
        f = _bdot(h, w_ref[:, 0:Q_OFF]).astype(BF16)
        dc = dc_ref[...]
        for g in range(N_FOURIER_GROUPS):
            lo = g * FOURIER_GROUP_DIM
            ug = _bdot(f[:, lo:lo + FOURIER_GROUP_DIM], dc)
            u_ref[:, lo:lo + FOURIER_GROUP_DIM] = ug[:, :FOURIER_GROUP_DIM]
            u_ref[:, FOURIER_DIM + lo:FOURIER_DIM + lo + FOURIER_GROUP_DIM] = ug[:, FOURIER_GROUP_DIM:]
        pq = _bdot(h, w_ref[:, Q_OFF:KV_OFF])
        msq = jnp.mean(pq * pq, axis=-1, keepdims=True)
        cq_ref[...] = (pq * lax.rsqrt(msq + EPS) * gq_ref[...]).astype(BF16)
    else:
        ckv_ref, kr_ref = out_refs
    pkv = _bdot(h, w_ref[:, KV_OFF:ROPE_OFF])
    mskv = jnp.mean(pkv * pkv, axis=-1, keepdims=True)
    ckv_ref[...] = (pkv * lax.rsqrt(mskv + EPS) * gkv_ref[...]).astype(BF16)
    kr_ref[...] = _bdot(h, w_ref[:, ROPE_OFF:ROPE_OFF + LANES])


def _mod_spec(mods, k, d):
    return pl.BlockSpec((mods.shape[0], d), lambda i, *_: (0, k))


def _in_proj(x2, mods, first_row, rows_per_mod, g1, gq, gkv, w_all, dc, *, with_q, tm):
    t, d = x2.shape
    nt = t // tm
    tiles_per_mod = rows_per_mod // tm

    def const(shape):
        return pl.BlockSpec(shape, lambda i: (0,) * len(shape))

    in_specs = [pl.BlockSpec((tm, d), lambda i: (i, 0)), _mod_spec(mods, 0, d), _mod_spec(mods, 1, d),
                const((1, d)), const((1, Q_LORA_RANK)), const((1, KV_LORA_RANK)),
                const(w_all.shape), const(dc.shape)]

    def rows(n):
        return pl.BlockSpec((tm, n), lambda i: (i, 0))

    out_specs = [rows(KV_LORA_RANK), rows(LANES)]
    out_shape = [jax.ShapeDtypeStruct((t, KV_LORA_RANK), BF16), jax.ShapeDtypeStruct((t, LANES), F32)]
    if with_q:
        out_specs = [rows(2 * FOURIER_DIM), rows(Q_LORA_RANK)] + out_specs
        out_shape = [jax.ShapeDtypeStruct((t, 2 * FOURIER_DIM), F32),
                     jax.ShapeDtypeStruct((t, Q_LORA_RANK), BF16)] + out_shape
    return pl.pallas_call(
        functools.partial(_in_proj_kernel, with_q=with_q, mod_row=lambda i: first_row + i // tiles_per_mod),
        grid=(nt,),
        in_specs=in_specs,
        out_specs=out_specs,
        out_shape=out_shape,
        compiler_params=_cparams(("arbitrary",)),
        name="in_proj_x" if with_q else "in_proj_ctx",
    )(x2, mods, mods, g1, gq, gkv, w_all, dc)


def _swap_halves(y, first_half):
    return jnp.where(first_half, pltpu.roll(y, LANES - ROT_HALF, 1), pltpu.roll(y, ROT_HALF, 1))


def _qkv_kernel(*refs, with_q, with_rope):
    it = iter(refs)
    if with_q:
        cq_ref = next(it)
    ckv_ref = next(it)
    kr_ref = next(it)
    if with_rope:
        cos_ref = next(it)
        sin_ref = next(it)
    if with_q:
        wqn_ref = next(it)
        wqr_ref = next(it)
        gqn_ref = next(it)
        gqr_ref = next(it)
    wkn_ref = next(it)
    wv_ref = next(it)
    gkn_ref = next(it)
    gkr_ref = next(it)
    if with_q:
        q_ref = next(it)
    k_ref = next(it)
    v_ref = next(it)

    tm = ckv_ref.shape[0]
    lane = lax.broadcasted_iota(jnp.int32, (tm, LANES), 1)
    low = lane < QK_ROPE_DIM
    first_half = (lane % ROT_BLOCK) < ROT_HALF
    inv_dim = 1.0 / QK_HEAD_DIM

    def rope(y):
        if not with_rope:
            return y
        return y * cos_ref[...] + _swap_halves(y, first_half) * sin_ref[...]

    if with_q:
        cq = cq_ref[...]
        qn = _bdot(cq, wqn_ref[...])
        qr = _bdot(cq, wqr_ref[...])
        qscale = QK_HEAD_DIM ** -0.5
        for p in range(MLA_HEADS // 2):
            blk = qr[:, p * LANES:(p + 1) * LANES]
            sq = blk * blk
            ss_lo = jnp.sum(jnp.where(low, sq, 0.0), axis=-1, keepdims=True)
            ss_hi = jnp.sum(jnp.where(low, 0.0, sq), axis=-1, keepdims=True)
            scales = []
            for hh, ssr in ((2 * p, ss_lo), (2 * p + 1, ss_hi)):
                nh = qn[:, hh * LANES:(hh + 1) * LANES]
                ssq = jnp.sum(nh * nh, axis=-1, keepdims=True) + ssr
                s = lax.rsqrt(ssq * inv_dim + EPS)
                scales.append(s)
                q_ref[hh, :, 0:LANES] = (nh * s * gqn_ref[...] * qscale).astype(BF16)
            s_pair = jnp.where(low, scales[0], scales[1])
            r = rope(blk * s_pair * gqr_ref[...]) * qscale
            q_ref[2 * p, :, LANES:2 * LANES] = jnp.where(low, r, 0.0).astype(BF16)
            q_ref[2 * p + 1, :, LANES:2 * LANES] = jnp.where(low, pltpu.roll(r, QK_ROPE_DIM, 1), 0.0).astype(BF16)

    ckv = ckv_ref[...]
    kn = _bdot(ckv, wkn_ref[...])
    v = _bdot(ckv, wv_ref[...])
    kr = kr_ref[...]
    ss_r = jnp.sum(jnp.where(low, kr * kr, 0.0), axis=-1, keepdims=True)
    base = rope(kr * gkr_ref[...])
    ones_col = jnp.where(lane == 0, 1.0, 0.0).astype(BF16)
    for hh in range(MLA_HEADS):
        nh = kn[:, hh * LANES:(hh + 1) * LANES]
        ssq = jnp.sum(nh * nh, axis=-1, keepdims=True) + ss_r
        s = lax.rsqrt(ssq * inv_dim + EPS)
        k_ref[hh, :, 0:LANES] = (nh * s * gkn_ref[...]).astype(BF16)
        k_ref[hh, :, LANES:2 * LANES] = jnp.where(low, base * s, 0.0).astype(BF16)
        v_ref[hh, :, 0:LANES] = v[:, hh * LANES:(hh + 1) * LANES].astype(BF16)
        v_ref[hh, :, LANES:2 * LANES] = ones_col


def _qkv(cq, ckv, kr2, cos_t, sin_t, wqn, wqr, gqn, gqr2, wkn, wv, gkn, gkr2, *, batch, seq, tm, with_q,
         with_rope):
    t = ckv.shape[0]
    nt = t // tm
    tiles_per_b = seq // tm

    def rows(n):
        return pl.BlockSpec((tm, n), lambda i: (i, 0))

    def const(arr):
        return pl.BlockSpec(arr.shape, lambda i: (0,) * arr.ndim)

    tab_spec = pl.BlockSpec((tm, LANES), lambda i: (i % tiles_per_b, 0))

    def head_out(width):
        return pl.BlockSpec((None, MLA_HEADS, tm, width), lambda i: (i // tiles_per_b, 0, i % tiles_per_b, 0))

    args, in_specs = [], []
    if with_q:
        args.append(cq)
        in_specs.append(rows(Q_LORA_RANK))
    args += [ckv, kr2]
    in_specs += [rows(KV_LORA_RANK), rows(LANES)]
    if with_rope:
        args += [cos_t, sin_t]
        in_specs += [tab_spec, tab_spec]
    if with_q:
        args += [wqn, wqr, gqn, gqr2]
        in_specs += [const(wqn), const(wqr), const(gqn), const(gqr2)]
    args += [wkn, wv, gkn, gkr2]
    in_specs += [const(wkn), const(wv), const(gkn), const(gkr2)]

    out_specs = [head_out(HEAD_PAD), head_out(V_PAD)]
    out_shape = [jax.ShapeDtypeStruct((batch, MLA_HEADS, seq, HEAD_PAD), BF16),
                 jax.ShapeDtypeStruct((batch, MLA_HEADS, seq, V_PAD), BF16)]
    if with_q:
        out_specs = [head_out(HEAD_PAD)] + out_specs
        out_shape = [jax.ShapeDtypeStruct((batch, MLA_HEADS, seq, HEAD_PAD), BF16)] + out_shape
    return pl.pallas_call(
        functools.partial(_qkv_kernel, with_q=with_q, with_rope=with_rope),
        grid=(nt,),
        in_specs=in_specs,
        out_specs=out_specs,
        out_shape=out_shape,
        compiler_params=_cparams(("arbitrary",)),
        name="qkv_x" if with_q else "kv_ctx",
    )(*args)


def _attn_kernel(q_ref, kx_ref, kc_ref, vx_ref, vc_ref, o_ref):
    q = q_ref[...]
    tq = q.shape[0]
    dn = (((1,), (1,)), ((), ()))

    def chunk(k, v, state):
        m, acc = state
        s = lax.dot_general(q, k, dn, preferred_element_type=F32)
        m_new = jnp.maximum(m, jnp.max(s, axis=-1, keepdims=True))
        p = jnp.exp(s - m_new).astype(BF16)
        acc = jnp.exp(m - m_new) * acc + _bdot(p, v)
        return m_new, acc

    state = (jnp.full((tq, 1), NEG_BIG, F32), jnp.zeros((tq, V_PAD), F32))
    for c in range(kx_ref.shape[0] // ATTN_KEY_CHUNK):
        rows = slice(c * ATTN_KEY_CHUNK, (c + 1) * ATTN_KEY_CHUNK)
        state = chunk(kx_ref[rows, :], vx_ref[rows, :], state)
    _, acc = chunk(kc_ref[...], vc_ref[...], state)
    o_ref[...] = acc[:, :V_HEAD_DIM] / acc[:, V_HEAD_DIM:V_HEAD_DIM + 1]


def _attention(q, kx, kc, vx, vc, *, tq):
    b, h, s, _ = q.shape
    lc = kc.shape[2]
    return pl.pallas_call(
        _attn_kernel,
        grid=(b, h, s // tq),
        in_specs=[pl.BlockSpec((None, None, tq, HEAD_PAD), lambda bi, hi, qi: (bi, hi, qi, 0)),
                  pl.BlockSpec((None, None, s, HEAD_PAD), lambda bi, hi, qi: (bi, hi, 0, 0)),
                  pl.BlockSpec((None, None, lc, HEAD_PAD), lambda bi, hi, qi: (bi, hi, 0, 0)),
                  pl.BlockSpec((None, None, s, V_PAD), lambda bi, hi, qi: (bi, hi, 0, 0)),
                  pl.BlockSpec((None, None, lc, V_PAD), lambda bi, hi, qi: (bi, hi, 0, 0))],
        out_specs=pl.BlockSpec((None, tq, V_HEAD_DIM), lambda bi, hi, qi: (bi, qi, hi)),
        out_shape=jax.ShapeDtypeStruct((b, s, h * V_HEAD_DIM), F32),
        compiler_params=_cparams(("arbitrary", "arbitrary", "arbitrary")),
        name="attention",
    )(q, kx, kc, vx, vc)


def _seq_dft_kernel(ure_ref, uim_ref, r_ref, e_ref, t2_ref, o_ref, a_ref):
    s = pl.program_id(2)
    sub = SUBLANES
    n = 2 * GRID_W * sub
    cols = ure_ref.shape[-1]

    @pl.when(s < DFT_BLOCKS)
    def _():
        t = _bdot(r_ref[...].astype(BF16), e_ref[...])
        row = lax.broadcasted_iota(jnp.int32, (n, n), 0)
        col = lax.broadcasted_iota(jnp.int32, (n, n), 1)
        t = jnp.where((row % sub) == (col % sub), t, 0.0).astype(BF16)
        rhs = jnp.concatenate([ure_ref[...].reshape(GRID_W * sub, cols), uim_ref[...].reshape(GRID_W * sub, cols)],
                              axis=0).astype(BF16)
        a = _bdot(t, rhs)
        a_ref[:, :, pl.ds(pl.multiple_of(s * sub, sub), sub), :] = a.reshape(2, GRID_W, sub, cols)

    @pl.when(s >= DFT_BLOCKS)
    def _():
        k0 = pl.multiple_of((s - DFT_BLOCKS) * sub, sub)
        rhs = a_ref[:, pl.ds(k0, sub), :, :].reshape(2 * sub * GRID_W, cols).astype(BF16)
        y = _bdot(t2_ref[...].astype(BF16), rhs)
        o_ref[...] = y.reshape(GRID_W, sub, cols)


def _seq_dft_tables(n_seq):
    w, sub, nb = GRID_W, SUBLANES, DFT_BLOCKS
    ch = np.arange(nb).reshape(nb, 1, 1, 1)
    kb = np.arange(w).reshape(1, w, 1, 1)
    j = np.arange(sub).reshape(1, 1, sub, 1)
    r = np.arange(w).reshape(1, 1, 1, w)
    ang = (2.0 * np.pi / n_seq) * ((kb * (w * r + sub * ch + j)) % n_seq)
    c, s = np.cos(ang), np.sin(ang)
    rot = np.stack([np.stack([c, s], axis=3), np.stack([-s, c], axis=3)], axis=1)
    r1 = rot.reshape(nb, 2 * w * sub, 2 * w).astype(np.float32)
    expand = (np.arange(2 * w * sub)[None, :] // sub == np.arange(2 * w)[:, None]).astype(np.float32)
    ka = np.arange(w).reshape(w, 1)
    cp = np.arange(w).reshape(1, w)
    ang2 = (2.0 * np.pi / w) * ((ka * cp) % w)
    norm = 1.0 / math.sqrt(n_seq * FOURIER_GROUP_DIM)
    cs = np.stack([np.cos(ang2), np.sin(ang2)], axis=1) * norm
    eye = np.eye(sub)
    t2 = (cs[:, None, :, None, :] * eye[None, :, None, :, None]).reshape(w * sub, 2 * sub * w).astype(np.float32)
    return jnp.asarray(r1), jnp.asarray(expand).astype(BF16), jnp.asarray(t2)


def _seq_dft(u, batch, n_seq):
    assert n_seq == GRID_W * GRID_W
    w, sub, nb = GRID_W, SUBLANES, DFT_BLOCKS
    r1, expand, t2 = _seq_dft_tables(n_seq)
    halves = 2
    cols = FOURIER_DIM // halves
    u5 = u.reshape(batch, w, nb, sub, 2 * FOURIER_DIM)

    def u_spec(part):
        return pl.BlockSpec((None, w, None, sub, cols),
                            lambda b, h, s: (b, 0, jnp.minimum(s, nb - 1), 0, part * halves + h))

    y = pl.pallas_call(
        _seq_dft_kernel,
        grid=(batch, halves, 2 * nb),
        in_specs=[u_spec(0), u_spec(1),
                  pl.BlockSpec((None, 2 * w * sub, 2 * w), lambda b, h, s: (jnp.minimum(s, nb - 1), 0, 0)),
                  pl.BlockSpec((2 * w, 2 * w * sub), lambda b, h, s: (0, 0)),
                  pl.BlockSpec((w * sub, 2 * sub * w), lambda b, h, s: (0, 0))],
        out_specs=pl.BlockSpec((None, w, None, sub, cols), lambda b, h, s: (b, 0, jnp.maximum(s - nb, 0), 0, h)),
        out_shape=jax.ShapeDtypeStruct((batch, w, nb, sub, FOURIER_DIM), F32),
        scratch_shapes=[pltpu.VMEM((2, w, w, cols), F32)],
        compiler_params=_cparams(("arbitrary", "arbitrary", "arbitrary")),
        name="seq_dft",
    )(u5, u5, r1, expand, t2)
    return y.reshape(batch * n_seq, FOURIER_DIM)


def _out_router_kernel(x_ref, four_ref, attn_ref, gt1_ref, sh2_ref, sc2_ref, gf_ref, ga_ref, g2_ref,
                       wo_ref, wrh_ref, wrl_ref, br_ref,
                       xnew_ref, hm_ref, ri_ref, rw_ref, cnt_ref, carry_ref, *, mod_row):
    i = pl.program_id(0)
    tm = x_ref.shape[0]
    row = pl.ds(mod_row(i), 1)

    @pl.when(i == 0)
    def _():
        carry_ref[...] = jnp.zeros_like(carry_ref)

    def norm(v, g):
        return (v * lax.rsqrt(jnp.mean(v * v, axis=-1, keepdims=True) + EPS) * g).astype(BF16)

    mix = (_bdot(norm(four_ref[...], gf_ref[...]), wo_ref[0:FOURIER_DIM, :])
           + _bdot(norm(attn_ref[...], ga_ref[...]), wo_ref[FOURIER_DIM:FOURIER_DIM + MLA_DIM, :]))
    xn = x_ref[...] + gt1_ref[row, :] * mix
    xnew_ref[...] = xn
    ms = jnp.mean(xn * xn, axis=-1, keepdims=True)
    hm = xn * lax.rsqrt(ms + EPS) * (g2_ref[...] * (1.0 + sc2_ref[row, :])) + sh2_ref[row, :]
    hm_ref[...] = _pack_halves(hm)

    hm_hi = hm.astype(BF16)
    hm_lo = (hm - hm_hi.astype(F32)).astype(BF16)
    logits = _bdot(hm_hi, wrh_ref[...]) + _bdot(hm_lo, wrh_ref[...]) + _bdot(hm_hi, wrl_ref[...]) + br_ref[...]
    lane = lax.broadcasted_iota(jnp.int32, (tm, LANES), 1)
    lanef = lane.astype(F32)
    far = 1e9

    lg = jnp.where(lane < N_GROUPS, logits, NEG_BIG)
    m1 = jnp.max(lg, axis=-1, keepdims=True)
    g_p = 1.0 / jnp.sum(jnp.exp(lg - m1), axis=-1, keepdims=True)
    gidx = jnp.min(jnp.where(lg >= m1, lanef, far), axis=-1, keepdims=True)
    lo = N_GROUPS + EXPERTS_PER_GROUP * gidx
    in_group = jnp.where(lanef >= lo, jnp.where(lanef < lo + EXPERTS_PER_GROUP, 1.0, 0.0), 0.0) > 0.5
    le = jnp.where(in_group, logits, NEG_BIG)
    m2 = jnp.max(le, axis=-1, keepdims=True)
    idx1 = jnp.min(jnp.where(le >= m2, lanef, far), axis=-1, keepdims=True)
    le2 = jnp.where(lanef == idx1, NEG_BIG, le)
    m3 = jnp.max(le2, axis=-1, keepdims=True)
    idx2 = jnp.min(jnp.where(le2 >= m3, lanef, far), axis=-1, keepdims=True)
    t = jnp.exp(m3 - m2)
    p1 = 1.0 / (1.0 + t)
    p2 = t / (1.0 + t)
    e1 = idx1 - N_GROUPS
    e2 = idx2 - N_GROUPS

    oh1 = jnp.where(lanef == e1, 1.0, 0.0)
    oh2 = jnp.where(lanef == e2, 1.0, 0.0)
    ohs = oh1 + oh2
    row = lax.broadcasted_iota(jnp.int32, (tm, tm), 0)
    col = lax.broadcasted_iota(jnp.int32, (tm, tm), 1)
    tri = jnp.where(row > col, 1.0, 0.0).astype(BF16)
    before = _bdot(tri, ohs.astype(BF16)) + carry_ref[...]
    rank1 = jnp.sum(oh1 * before, axis=-1, keepdims=True)
    rank2 = jnp.sum(oh2 * before, axis=-1, keepdims=True)
    carry = carry_ref[...] + jnp.sum(ohs, axis=0, keepdims=True)
    carry_ref[...] = carry
    cnt_ref[...] = jnp.broadcast_to(carry, cnt_ref.shape)

    ri = jnp.where(lane == 0, e1, jnp.where(lane == 1, e2, jnp.where(lane == 2, rank1, jnp.where(lane == 3, rank2, 0.0))))
    ri_ref[...] = ri.astype(jnp.int32)
    rw_ref[...] = jnp.where(lane == 0, g_p * p1, jnp.where(lane == 1, g_p * p2, 0.0))


def _out_router(x2, four, attn, mods, gf, ga, g2, wo, wrh, wrl, br, *, seq, tm):
    t, d = x2.shape
    nt = t // tm
    tiles_per_b = seq // tm

    def rows(n):
        return pl.BlockSpec((tm, n), lambda i: (i, 0))

    def const(arr):
        return pl.BlockSpec(arr.shape, lambda i: (0,) * arr.ndim, pipeline_mode=pl.Buffered(1))

    return pl.pallas_call(
        functools.partial(_out_router_kernel, mod_row=lambda i: i // tiles_per_b),
        grid=(nt,),
        in_specs=[rows(d), rows(FOURIER_DIM), rows(MLA_DIM), _mod_spec(mods, 2, d), _mod_spec(mods, 3, d),
                  _mod_spec(mods, 4, d),
                  const(gf), const(ga), const(g2), const(wo), const(wrh), const(wrl), const(br)],
        out_specs=[rows(d), rows(d // 2), rows(LANES), rows(LANES), pl.BlockSpec((8, LANES), lambda i: (0, 0))],
        out_shape=[jax.ShapeDtypeStruct((t, d), F32), jax.ShapeDtypeStruct((t, d // 2), jnp.uint32),
                   jax.ShapeDtypeStruct((t, LANES), jnp.int32), jax.ShapeDtypeStruct((t, LANES), F32),
                   jax.ShapeDtypeStruct((8, LANES), F32)],
        scratch_shapes=[pltpu.VMEM((1, LANES), F32)],
        compiler_params=_cparams(("arbitrary",)),
        name="out_proj_router",
    )(x2, four, attn, mods, mods, mods, gf, ga, g2, wo, wrh, wrl, br)


def _moe_kernel(item_e, item_blk0, item_nblk, row_tok, used_blocks,
                hm_hbm, wg_hbm, wu_hbm, wd_hbm, y_hbm,
                xg, xb, gs, ab, yp, gu_buf, dn_buf, gsem, osem, gusem, dnsem):
    i = pl.program_id(0)
    j = pl.program_id(1)
    n_items = pl.num_programs(0)
    nj = pl.num_programs(1)
    slot = i % 2
    nblk = item_nblk[i]

    def weight_copy(it, ph):
        if ph == 2:
            ws = it % DN_SLOTS
            return pltpu.make_async_copy(wd_hbm.at[item_e[it]], dn_buf.at[ws], dnsem.at[ws])
        ws = (2 * it + ph) % GU_SLOTS
        return pltpu.make_async_copy((wg_hbm, wu_hbm)[ph].at[item_e[it]], gu_buf.at[ws], gusem.at[ws])

    def start_weight(it, ph):
        it_c = jnp.minimum(it, n_items - 1)

        @pl.when(jnp.logical_and(it < n_items, item_nblk[it_c] > 0))
        def _():
            weight_copy(it_c, ph).start(priority=WEIGHT_DMA_PRIORITY)

    def gather_copy(tok, sl, r):
        return pltpu.make_async_copy(hm_hbm.at[pl.ds(tok, 1)], xg.at[sl, pl.ds(r, 1)], gsem.at[sl])

    def issue_gather(it, sl):
        r0 = item_blk0[it] * MOE_BLOCK

        def body(r8, carry):
            for k in range(GATHER_UNROLL):
                r = r8 * GATHER_UNROLL + k
                gather_copy(row_tok[r0 + r], sl, r).start(priority=ROW_DMA_PRIORITY)
            return carry

        lax.fori_loop(0, item_nblk[it] * (MOE_BLOCK // GATHER_UNROLL), body, 0)

    def wait_gather(it, sl):
        for b in range(ITEM_BLOCKS):
            @pl.when(b < item_nblk[it])
            def _():
                pltpu.make_async_copy(hm_hbm.at[pl.ds(0, MOE_BLOCK)], xg.at[sl, pl.ds(b * MOE_BLOCK, MOE_BLOCK)],
                                      gsem.at[sl]).wait()

    def out_copy(it, m):
        r0 = pl.multiple_of(item_blk0[it] * MOE_BLOCK, MOE_BLOCK)
        return pltpu.make_async_copy(yp.at[pl.ds(0, m)], y_hbm.at[pl.ds(r0, m)], osem.at[0])

    def wait_out(it):
        for nb in range(1, ITEM_BLOCKS + 1):
            @pl.when(item_nblk[it] == nb)
            def _():
                out_copy(it, nb * MOE_BLOCK).wait()

    @pl.when(j == 0)
    def _():
        @pl.when(i == 0)
        def _():
            start_weight(0, 0)
            start_weight(0, 1)
            start_weight(1, 0)
            start_weight(0, 2)
            issue_gather(0, 0)

        start_weight(i + 1, 2)
        wait_gather(i, slot)

        @pl.when(i + 1 < n_items)
        def _():
            issue_gather(i + 1, 1 - slot)

    @pl.when(j == 1)
    def _():
        start_weight(i + 1, 1)

    @pl.when(j == 2)
    def _():
        start_weight(i + 2, 0)

        @pl.when(i > 0)
        def _():
            wait_out(i - 1)

    for ph in range(3):
        @pl.when(jnp.logical_and(j == ph, nblk > 0))
        def _():
            weight_copy(i, ph).wait()

    for nb in range(1, ITEM_BLOCKS + 1):
        m = nb * MOE_BLOCK

        @pl.when(jnp.logical_and(nblk == nb, j == 0))
        def _():
            hi, lo = _unpack_halves(xg[slot, 0:m, :])
            half = hi.shape[1]
            xb[0:m, 0:half] = hi.astype(BF16)
            xb[0:m, half:2 * half] = lo.astype(BF16)
            gs[0:m, :] = _bdot(xb[0:m, :], gu_buf[(2 * i) % GU_SLOTS].astype(BF16))

        @pl.when(jnp.logical_and(nblk == nb, j == 1))
        def _():
            g = gs[0:m, :]
            u = _bdot(xb[0:m, :], gu_buf[(2 * i + 1) % GU_SLOTS].astype(BF16))
            ab[0:m, :] = (g * jax.nn.sigmoid(g) * u).astype(BF16)

        @pl.when(jnp.logical_and(nblk == nb, j == 2))
        def _():
            yp[0:m, :] = _pack_halves(_bdot(ab[0:m, :], dn_buf[i % DN_SLOTS].astype(BF16)))
            out_copy(i, m).start()

    @pl.when(jnp.logical_and(i == n_items - 1, j == nj - 1))
    def _():
        wait_out(i)
        n_blocks = y_hbm.shape[0] // MOE_BLOCK
        yp[0:MOE_BLOCK, :] = jnp.zeros((MOE_BLOCK, yp.shape[1]), jnp.uint32)

        def tail_copy(blk):
            r0 = pl.multiple_of(blk * MOE_BLOCK, MOE_BLOCK)
            return pltpu.make_async_copy(yp.at[pl.ds(0, MOE_BLOCK)], y_hbm.at[pl.ds(r0, MOE_BLOCK)], osem.at[0])

        def start_body(blk, carry):
            tail_copy(blk).start()
            return carry

        def wait_body(blk, carry):
            tail_copy(blk).wait()
            return carry

        lax.fori_loop(used_blocks[0], n_blocks, start_body, 0)
        lax.fori_loop(used_blocks[0], n_blocks, wait_body, 0)


def _moe(hm, w_gate, w_up, w_down, item_e, item_blk0, item_nblk, row_tok, used_blocks, *, n_rows):
    d, de = w_gate.shape[1], w_gate.shape[2]
    n_items = item_e.shape[0]
    nj = 3
    rows = ITEM_BLOCKS * MOE_BLOCK
    any_spec = pl.BlockSpec(memory_space=pl.ANY)
    grid_spec = pltpu.PrefetchScalarGridSpec(
        num_scalar_prefetch=5,
        grid=(n_items, nj),
        in_specs=[any_spec, any_spec, any_spec, any_spec],
        out_specs=any_spec,
        scratch_shapes=[pltpu.VMEM((2, rows, d // 2), jnp.uint32),
                        pltpu.VMEM((rows, d), BF16),
                        pltpu.VMEM((rows, de), F32),
                        pltpu.VMEM((rows, de), BF16),
                        pltpu.VMEM((rows, d // 2), jnp.uint32),
                        pltpu.VMEM((GU_SLOTS, d, de), F32),
                        pltpu.VMEM((DN_SLOTS, de, d), F32),
                        pltpu.SemaphoreType.DMA((2,)),
                        pltpu.SemaphoreType.DMA((1,)),
                        pltpu.SemaphoreType.DMA((GU_SLOTS,)),
                        pltpu.SemaphoreType.DMA((DN_SLOTS,))],
    )
    return pl.pallas_call(
        _moe_kernel,
        grid_spec=grid_spec,
        out_shape=jax.ShapeDtypeStruct((n_rows, d // 2), jnp.uint32),
        compiler_params=_cparams(("arbitrary", "arbitrary")),
        name="moe_experts",
    )(item_e, item_blk0, item_nblk, row_tok, used_blocks, hm, w_gate, w_up, w_down)


def _combine_kernel(dest, x_ref, gt2_ref, rw_ref, y_hbm, o_ref, ybuf, sem, *, mod_row):
    i = pl.program_id(0)
    n = pl.num_programs(0)
    tm = x_ref.shape[0]
    slot = i % 2

    def issue(it, sl):
        base = it * tm

        def body(r4, carry):
            for rr in range(GATHER_UNROLL // 2):
                r = r4 * (GATHER_UNROLL // 2) + rr
                for k in range(2):
                    pltpu.make_async_copy(y_hbm.at[pl.ds(dest[2 * (base + r) + k], 1)], ybuf.at[sl, k, pl.ds(r, 1)],
                                          sem.at[sl]).start(priority=k)
            return carry

        lax.fori_loop(0, tm // (GATHER_UNROLL // 2), body, 0)

    @pl.when(i == 0)
    def _():
        issue(0, 0)

    for k in range(2):
        pltpu.make_async_copy(y_hbm.at[pl.ds(0, tm)], ybuf.at[slot, k], sem.at[slot]).wait()

    @pl.when(i + 1 < n)
    def _():
        issue(i + 1, 1 - slot)

    w = rw_ref[...]
    gate = gt2_ref[pl.ds(mod_row(i), 1), :]
    hi0, lo0 = _unpack_halves(ybuf[slot, 0])
    hi1, lo1 = _unpack_halves(ybuf[slot, 1])
    half = hi0.shape[1]
    o_ref[:, 0:half] = x_ref[:, 0:half] + gate[:, 0:half] * (w[:, 0:1] * hi0 + w[:, 1:2] * hi1)
    o_ref[:, half:2 * half] = (x_ref[:, half:2 * half]
                               + gate[:, half:2 * half] * (w[:, 0:1] * lo0 + w[:, 1:2] * lo1))


def _combine(dest, xnew, mods, rw, y, *, seq, tm):
    t, d = xnew.shape
    tiles_per_b = seq // tm
    grid_spec = pltpu.PrefetchScalarGridSpec(
        num_scalar_prefetch=1,
        grid=(t // tm,),
        in_specs=[pl.BlockSpec((tm, d), lambda i, ds: (i, 0)),
                  _mod_spec(mods, 5, d),
                  pl.BlockSpec((tm, LANES), lambda i, ds: (i, 0)),
                  pl.BlockSpec(memory_space=pl.ANY)],
        out_specs=pl.BlockSpec((tm, d), lambda i, ds: (i, 0)),
        scratch_shapes=[pltpu.VMEM((2, 2, tm, d // 2), jnp.uint32), pltpu.SemaphoreType.DMA((2,))],
    )
    return pl.pallas_call(
        functools.partial(_combine_kernel, mod_row=lambda i: i // tiles_per_b),
        grid_spec=grid_spec,
        out_shape=jax.ShapeDtypeStruct((t, d), F32),
        compiler_params=_cparams(("arbitrary",)),
        name="moe_combine",
    )(dest, xnew, mods, rw, y)


def _rope_tables(n_tokens):
    rows = n_tokens // GRID_W
    row = jnp.repeat(jnp.arange(rows, dtype=jnp.int32), GRID_W).astype(F32)
    col = jnp.tile(jnp.arange(GRID_W, dtype=jnp.int32), rows).astype(F32)
    n_freq = QK_ROPE_DIM // 4
    inv = ROPE_THETA ** (-jnp.arange(n_freq, dtype=F32) / n_freq)
    ar = row[:, None] * inv[None, :]
    ac = col[:, None] * inv[None, :]
    cr, sr, cc, sc = jnp.cos(ar), jnp.sin(ar), jnp.cos(ac), jnp.sin(ac)
    cos64 = jnp.concatenate([cr, cr, cc, cc], axis=-1)
    sin64 = jnp.concatenate([-sr, sr, -sc, sc], axis=-1)
    return jnp.tile(cos64, (1, 2)), jnp.tile(sin64, (1, 2))


def _channel_dft_table():
    c = np.arange(FOURIER_GROUP_DIM).reshape(-1, 1)
    k = np.arange(FOURIER_GROUP_DIM).reshape(1, -1)
    ang = (2.0 * np.pi / FOURIER_GROUP_DIM) * ((c * k) % FOURIER_GROUP_DIM)
    return jnp.asarray(np.concatenate([np.cos(ang), -np.sin(ang)], axis=1).astype(np.float32)).astype(BF16)


def _split_heads(w, widths):
    k = w.shape[0]
    wh = w.reshape(k, MLA_HEADS, sum(widths))
    outs, off = [], 0
    for wd in widths:
        outs.append(wh[:, :, off:off + wd].reshape(k, MLA_HEADS * wd))
        off += wd
    return outs


def kernel(x, c, ctx, c_ctx, w_ada, b_ada, g_norm1, g_norm2, w_in, g_q_a, g_kv_a, w_uq, w_ukv, g_qk_q, g_qk_k,
           g_out_four, g_out_attn, w_out, w_router_group, b_router_group, w_router_expert, b_router_expert,
           w_gate, w_up, w_down):
    b, s, d = x.shape
    lc = ctx.shape[1]
    t = b * s
    layer_params = (w_ada, b_ada, g_norm1, g_norm2, w_in, g_q_a, g_kv_a, w_uq, w_ukv, g_qk_q, g_qk_k, g_out_four,
                    g_out_attn, w_out, w_router_group, b_router_group, w_router_expert, b_router_expert,
                    w_gate, w_up, w_down)
    assert all(p.shape[0] == 1 for p in layer_params), "single-layer block"
    (w_ada, b_ada, g_norm1, g_norm2, w_in, g_q_a, g_kv_a, w_uq, w_ukv, g_qk_q, g_qk_k, g_out_four,
     g_out_attn, w_out, w_router_group, b_router_group, w_router_expert, b_router_expert,
     w_gate, w_up, w_down) = [p.reshape(p.shape[1:]) for p in layer_params]

    cond8 = jnp.concatenate([c, c_ctx[None, :], jnp.zeros((8 - b - 1, d), F32)], axis=0)
    mods = _ada_mod(cond8, w_ada, b_ada)

    w_all = jnp.concatenate([w_in, w_in[:, ROPE_OFF:]], axis=1).astype(BF16)
    dc = _channel_dft_table()
    wqn, wqr = [w.astype(BF16) for w in _split_heads(w_uq, (QK_NOPE_DIM, QK_ROPE_DIM))]
    wkn, wv = [w.astype(BF16) for w in _split_heads(w_ukv, (QK_NOPE_DIM, V_HEAD_DIM))]
    gqn = g_qk_q[:QK_NOPE_DIM].reshape(1, -1)
    gqr2 = jnp.tile(g_qk_q[QK_NOPE_DIM:], 2).reshape(1, -1)
    gkn = g_qk_k[:QK_NOPE_DIM].reshape(1, -1)
    gkr2 = jnp.tile(g_qk_k[QK_NOPE_DIM:], 2).reshape(1, -1)
    g1 = g_norm1.reshape(1, d)
    gq = g_q_a.reshape(1, -1)
    gkv = g_kv_a.reshape(1, -1)

    x2 = x.reshape(t, d)
    u, cq, ckv, kr2 = _in_proj(x2, mods, 0, s, g1, gq, gkv, w_all, dc, with_q=True, tm=IN_PROJ_ROWS)
    ckv_c, kr2_c = _in_proj(ctx.reshape(b * lc, d), mods, b, b * lc, g1, gq, gkv, w_all, dc, with_q=False, tm=lc)

    cos_t, sin_t = _rope_tables(s)
    q, kx, vx = _qkv(cq, ckv, kr2, cos_t, sin_t, wqn, wqr, gqn, gqr2, wkn, wv, gkn, gkr2,
                     batch=b, seq=s, tm=QKV_ROWS, with_q=True, with_rope=True)
    kc, vc = _qkv(None, ckv_c, kr2_c, None, None, None, None, None, None, wkn, wv, gkn, gkr2,
                  batch=b, seq=lc, tm=lc, with_q=False, with_rope=False)

    attn = _attention(q, kx, kc, vx, vc, tq=ATTN_Q_ROWS).reshape(t, MLA_DIM)
    four = _seq_dft(u, b, s)

    wo = w_out.astype(BF16)
    n_route = N_GROUPS + N_EXPERTS
    wrt = jnp.concatenate([w_router_group, w_router_expert, jnp.zeros((d, LANES - n_route), F32)], axis=1)
    wrt_hi = wrt.astype(BF16)
    wrt_lo = (wrt - wrt_hi.astype(F32)).astype(BF16)
    brt = jnp.concatenate([b_router_group, b_router_expert, jnp.zeros((LANES - n_route,), F32)]).reshape(1, -1)
    xnew, hm, ri, rw, cnt = _out_router(x2, four, attn, mods, g_out_four.reshape(1, -1), g_out_attn.reshape(1, -1),
                                        g_norm2.reshape(1, d), wo, wrt_hi, wrt_lo, brt, seq=s, tm=OUT_ROUTER_ROWS)

    counts = cnt[0, :N_EXPERTS].astype(jnp.int32)
    nblk_e = (counts + MOE_BLOCK - 1) // MOE_BLOCK
    blk_end = jnp.cumsum(nblk_e)
    blk_start = blk_end - nblk_e
    n_slots = t * 2
    n_blocks = -(-(n_slots + N_EXPERTS * (MOE_BLOCK - 1)) // MOE_BLOCK)
    n_rows = n_blocks * MOE_BLOCK
    e12 = ri[:, 0:2]
    seg_start = jnp.sum(jnp.where(e12[:, :, None] == jnp.arange(N_EXPERTS, dtype=jnp.int32), blk_start, 0), axis=-1)
    dest = (seg_start * MOE_BLOCK + ri[:, 2:4]).reshape(-1)
    tok_ids = jnp.repeat(jnp.arange(t, dtype=jnp.int32), 2)
    row_tok = jnp.zeros((n_rows,), jnp.int32).at[dest].set(tok_ids)
    items_e = (nblk_e + ITEM_BLOCKS - 1) // ITEM_BLOCKS
    item_end = jnp.cumsum(items_e)
    n_items = (n_blocks + (ITEM_BLOCKS - 1) * N_EXPERTS) // ITEM_BLOCKS
    idx = jnp.arange(n_items, dtype=jnp.int32)
    total = item_end[-1]
    idx_c = jnp.minimum(idx, total - 1)
    ie = jnp.minimum(jnp.sum(idx_c[:, None] >= item_end[None, :], axis=1), N_EXPERTS - 1).astype(jnp.int32)
    local = idx_c - (item_end - items_e)[ie]
    item_blk0 = (blk_start[ie] + ITEM_BLOCKS * local).astype(jnp.int32)
    item_nblk = jnp.where(idx < total, jnp.clip(nblk_e[ie] - ITEM_BLOCKS * local, 0, ITEM_BLOCKS), 0).astype(jnp.int32)

    used_blocks = blk_end[-1:].astype(jnp.int32)
    y = _moe(hm, w_gate, w_up, w_down, ie, item_blk0, item_nblk, row_tok, used_blocks, n_rows=n_rows)
    out = _combine(dest.astype(jnp.int32), xnew, mods, rw, y, seq=s, tm=COMBINE_ROWS)
    return out.reshape(b, s, d)
```

```python
import functools
import math

import numpy as np
import jax
import jax.numpy as jnp
from jax import lax
from jax.experimental import pallas as pl
from jax.experimental.pallas import tpu as pltpu

F32 = jnp.float32
BF16 = jnp.bfloat16

D_MODEL = 2048
GRID_W = 64
EPS = 1e-6
N_MOD = 6
N_FOURIER_GROUPS = 4
FOURIER_GROUP_DIM = 256
FOURIER_DIM = 1024
MLA_HEADS = 8
QK_NOPE_DIM = 128
QK_ROPE_DIM = 64
QK_HEAD_DIM = 192
V_HEAD_DIM = 128
Q_LORA_RANK = 768
KV_LORA_RANK = 512
MLA_DIM = 1024
ROPE_THETA = 10000.0
Q_OFF = FOURIER_DIM
KV_OFF = Q_OFF + Q_LORA_RANK
ROPE_OFF = KV_OFF + KV_LORA_RANK
N_GROUPS = 8
EXPERTS_PER_GROUP = 8
N_EXPERTS = 64
D_EXPERT = 768

ROT_HALF = QK_ROPE_DIM // 4
ROT_BLOCK = 2 * ROT_HALF

ADA_COLS = 1024
IN_PROJ_ROWS = 512
QKV_ROWS = 512
ATTN_Q_ROWS = 1024
OUT_ROUTER_ROWS = 512
COMBINE_ROWS = 256

LANES = 128
SUBLANES = 8
DFT_BLOCKS = GRID_W // SUBLANES
HEAD_PAD = 256
V_PAD = 256
MOE_BLOCK = 64
ITEM_BLOCKS = 8
GATHER_UNROLL = 8
GU_SLOTS = 3
DN_SLOTS = 2
ATTN_KEY_CHUNK = 512
ROW_DMA_PRIORITY = 0
WEIGHT_DMA_PRIORITY = 1
VMEM_LIMIT = 56 * 1024 * 1024
NEG_BIG = -1e30


def _cparams(sem):
    return pltpu.CompilerParams(dimension_semantics=sem, vmem_limit_bytes=VMEM_LIMIT)


def _bdot(a, b):
    return jnp.dot(a, b, preferred_element_type=F32)


def _pack_halves(v):
    n = v.shape[1] // 2
    hi = pltpu.bitcast(v[:, :n].astype(BF16).astype(F32), jnp.uint32)
    lo = pltpu.bitcast(v[:, n:].astype(BF16).astype(F32), jnp.uint32)
    return hi | (lo >> 16)


def _unpack_halves(u):
    hi = pltpu.bitcast(u & jnp.uint32(0xFFFF0000), F32)
    lo = pltpu.bitcast(u << 16, F32)
    return hi, lo


def _ada_kernel(c_ref, w_ref, b_ref, o_ref):
    c = c_ref[...]
    s = (c * jax.nn.sigmoid(c)).astype(BF16)
    o_ref[...] = _bdot(s, w_ref[...].astype(BF16)) + b_ref[...]


def _ada_mod(cond8, w_ada, b_ada):
    d, n = w_ada.shape
    tn = ADA_COLS
    return pl.pallas_call(
        _ada_kernel,
        grid=(n // tn,),
        in_specs=[pl.BlockSpec((8, d), lambda i: (0, 0)),
                  pl.BlockSpec((d, tn), lambda i: (0, i)),
                  pl.BlockSpec((1, tn), lambda i: (0, i))],
        out_specs=pl.BlockSpec((8, tn), lambda i: (0, i)),
        out_shape=jax.ShapeDtypeStruct((8, n), F32),
        compiler_params=_cparams(("arbitrary",)),
        name="ada_mod",
    )(cond8, w_ada, b_ada.reshape(1, n))


def _in_proj_kernel(x_ref, sh_ref, sc_ref, g_ref, gq_ref, gkv_ref, w_ref, dc_ref, *out_refs, with_q, mod_row):
    row = pl.ds(mod_row(pl.program_id(0)), 1)
    x = x_ref[...]
    ms = jnp.mean(x * x, axis=-1, keepdims=True)
    a = g_ref[...] * (1.0 + sc_ref[row, :])
    h = (x * lax.rsqrt(ms + EPS) * a + sh_ref[row, :]).astype(BF16)
    if with_q:
        u_ref, cq_ref, ckv_ref, kr_ref = out_refs
        f = _bdot(h, w_ref[:, 0:Q_OFF]).astype(BF16)
        dc = dc_ref[...]
        for g in range(N_FOURIER_GROUPS):
            lo = g * FOURIER_GROUP_DIM
            ug = _bdot(f[:, lo:lo + FOURIER_GROUP_DIM], dc)
            u_ref[:, lo:lo + FOURIER_GROUP_DIM] = ug[:, :FOURIER_GROUP_DIM]
            u_ref[:, FOURIER_DIM + lo:FOURIER_DIM + lo + FOURIER_GROUP_DIM] = ug[:, FOURIER_GROUP_DIM:]
        pq = _bdot(h, w_ref[:, Q_OFF:KV_OFF])
        msq = jnp.mean(pq * pq, axis=-1, keepdims=True)
        cq_ref[...] = (pq * lax.rsqrt(msq + EPS) * gq_ref[...]).astype(BF16)
    else:
        ckv_ref, kr_ref = out_refs
    pkv = _bdot(h, w_ref[:, KV_OFF:ROPE_OFF])
    mskv = jnp.mean(pkv * pkv, axis=-1, keepdims=True)
    ckv_ref[...] = (pkv * lax.rsqrt(mskv + EPS) * gkv_ref[...]).astype(BF16)
    kr_ref[...] = _bdot(h, w_ref[:, ROPE_OFF:ROPE_OFF + LANES])


def _mod_spec(mods, k, d):
    return pl.BlockSpec((mods.shape[0], d), lambda i, *_: (0, k))


def _in_proj(x2, mods, first_row, rows_per_mod, g1, gq, gkv, w_all, dc, *, with_q, tm):
    t, d = x2.shape
    nt = t // tm
    tiles_per_mod = rows_per_mod // tm

    def const(shape):
        return pl.BlockSpec(shape, lambda i: (0,) * len(shape))

    in_specs = [pl.BlockSpec((tm, d), lambda i: (i, 0)), _mod_spec(mods, 0, d), _mod_spec(mods, 1, d),
                const((1, d)), const((1, Q_LORA_RANK)), const((1, KV_LORA_RANK)),
                const(w_all.shape), const(dc.shape)]

    def rows(n):
        return pl.BlockSpec((tm, n), lambda i: (i, 0))

    out_specs = [rows(KV_LORA_RANK), rows(LANES)]
    out_shape = [jax.ShapeDtypeStruct((t, KV_LORA_RANK), BF16), jax.ShapeDtypeStruct((t, LANES), F32)]
    if with_q:
        out_specs = [rows(2 * FOURIER_DIM), rows(Q_LORA_RANK)] + out_specs
        out_shape = [jax.ShapeDtypeStruct((t, 2 * FOURIER_DIM), F32),
                     jax.ShapeDtypeStruct((t, Q_LORA_RANK), BF16)] + out_shape
    return pl.pallas_call(
        functools.partial(_in_proj_kernel, with_q=with_q, mod_row=lambda i: first_row + i // tiles_per_mod),
        grid=(nt,),
        in_specs=in_specs,
        out_specs=out_specs,
        out_shape=out_shape,
        compiler_params=_cparams(("arbitrary",)),
        name="in_proj_x" if with_q else "in_proj_ctx",
    )(x2, mods, mods, g1, gq, gkv, w_all, dc)


def _swap_halves(y, first_half):
    return jnp.where(first_half, pltpu.roll(y, LANES - ROT_HALF, 1), pltpu.roll(y, ROT_HALF, 1))


def _qkv_kernel(*refs, with_q, with_rope):
    it = iter(refs)
    if with_q:
        cq_ref = next(it)
    ckv_ref = next(it)
    kr_ref = next(it)
    if with_rope:
        cos_ref = next(it)
        sin_ref = next(it)
    if with_q:
        wqn_ref = next(it)
        wqr_ref = next(it)
        gqn_ref = next(it)
        gqr_ref = next(it)
    wkn_ref = next(it)
    wv_ref = next(it)
    gkn_ref = next(it)
    gkr_ref = next(it)
    if with_q:
        q_ref = next(it)
    k_ref = next(it)
    v_ref = next(it)

    tm = ckv_ref.shape[0]
    lane = lax.broadcasted_iota(jnp.int32, (tm, LANES), 1)
    low = lane < QK_ROPE_DIM
    first_half = (lane % ROT_BLOCK) < ROT_HALF
    inv_dim = 1.0 / QK_HEAD_DIM

    def rope(y):
        if not with_rope:
            return y
        return y * cos_ref[...] + _swap_halves(y, first_half) * sin_ref[...]

    if with_q:
        cq = cq_ref[...]
        qn = _bdot(cq, wqn_ref[...])
        qr = _bdot(cq, wqr_ref[...])
        qscale = QK_HEAD_DIM ** -0.5
        for p in range(MLA_HEADS // 2):
            blk = qr[:, p * LANES:(p + 1) * LANES]
            sq = blk * blk
            ss_lo = jnp.sum(jnp.where(low, sq, 0.0), axis=-1, keepdims=True)
            ss_hi = jnp.sum(jnp.where(low, 0.0, sq), axis=-1, keepdims=True)
            scales = []
            for hh, ssr in ((2 * p, ss_lo), (2 * p + 1, ss_hi)):
                nh = qn[:, hh * LANES:(hh + 1) * LANES]
                ssq = jnp.sum(nh * nh, axis=-1, keepdims=True) + ssr
                s = lax.rsqrt(ssq * inv_dim + EPS)
                scales.append(s)
                q_ref[hh, :, 0:LANES] = (nh * s * gqn_ref[...] * qscale).astype(BF16)
            s_pair = jnp.where(low, scales[0], scales[1])
            r = rope(blk * s_pair * gqr_ref[...]) * qscale
            q_ref[2 * p, :, LANES:2 * LANES] = jnp.where(low, r, 0.0).astype(BF16)
            q_ref[2 * p + 1, :, LANES:2 * LANES] = jnp.where(low, pltpu.roll(r, QK_ROPE_DIM, 1), 0.0).astype(BF16)

    ckv = ckv_ref[...]
    kn = _bdot(ckv, wkn_ref[...])
    v = _bdot(ckv, wv_ref[...])
    kr = kr_ref[...]
    ss_r = jnp.sum(jnp.where(low, kr * kr, 0.0), axis=-1, keepdims=True)
    base = rope(kr * gkr_ref[...])
    ones_col = jnp.where(lane == 0, 1.0, 0.0).astype(BF16)
    for hh in range(MLA_HEADS):
        nh = kn[:, hh * LANES:(hh + 1) * LANES]
        ssq = jnp.sum(nh * nh, axis=-1, keepdims=True) + ss_r
        s = lax.rsqrt(ssq * inv_dim + EPS)
        k_ref[hh, :, 0:LANES] = (nh * s * gkn_ref[...]).astype(BF16)
        k_ref[hh, :, LANES:2 * LANES] = jnp.where(low, base * s, 0.0).astype(BF16)
        v_ref[hh, :, 0:LANES] = v[:, hh * LANES:(hh + 1) * LANES].astype(BF16)
        v_ref[hh, :, LANES:2 * LANES] = ones_col


def _qkv(cq, ckv, kr2, cos_t, sin_t, wqn, wqr, gqn, gqr2, wkn, wv, gkn, gkr2, *, batch, seq, tm, with_q,
         with_rope):
    t = ckv.shape[0]
    nt = t // tm
    tiles_per_b = seq // tm

    def rows(n):
        return pl.BlockSpec((tm, n), lambda i: (i, 0))

    def const(arr):
        return pl.BlockSpec(arr.shape, lambda i: (0,) * arr.ndim)

    tab_spec = pl.BlockSpec((tm, LANES), lambda i: (i % tiles_per_b, 0))

    def head_out(width):
        return pl.BlockSpec((None, MLA_HEADS, tm, width), lambda i: (i // tiles_per_b, 0, i % tiles_per_b, 0))

    args, in_specs = [], []
    if with_q:
        args.append(cq)
        in_specs.append(rows(Q_LORA_RANK))
    args += [ckv, kr2]
    in_specs += [rows(KV_LORA_RANK), rows(LANES)]
    if with_rope:
        args += [cos_t, sin_t]
        in_specs += [tab_spec, tab_spec]
    if with_q:
        args += [wqn, wqr, gqn, gqr2]
        in_specs += [const(wqn), const(wqr), const(gqn), const(gqr2)]
    args += [wkn, wv, gkn, gkr2]
    in_specs += [const(wkn), const(wv), const(gkn), const(gkr2)]

    out_specs = [head_out(HEAD_PAD), head_out(V_PAD)]
    out_shape = [jax.ShapeDtypeStruct((batch, MLA_HEADS, seq, HEAD_PAD), BF16),
                 jax.ShapeDtypeStruct((batch, MLA_HEADS, seq, V_PAD), BF16)]
    if with_q:
        out_specs = [head_out(HEAD_PAD)] + out_specs
        out_shape = [jax.ShapeDtypeStruct((batch, MLA_HEADS, seq, HEAD_PAD), BF16)] + out_shape
    return pl.pallas_call(
        functools.partial(_qkv_kernel, with_q=with_q, with_rope=with_rope),
        grid=(nt,),
        in_specs=in_specs,
        out_specs=out_specs,
        out_shape=out_shape,
        compiler_params=_cparams(("arbitrary",)),
        name="qkv_x" if with_q else "kv_ctx",
    )(*args)


def _attn_kernel(q_ref, kx_ref, kc_ref, vx_ref, vc_ref, o_ref):
    q = q_ref[...]
    tq = q.shape[0]
    dn = (((1,), (1,)), ((), ()))

    def chunk(k, v, state):
        m, acc = state
        s = lax.dot_general(q, k, dn, preferred_element_type=F32)
        m_new = jnp.maximum(m, jnp.max(s, axis=-1, keepdims=True))
        p = jnp.exp(s - m_new).astype(BF16)
        acc = jnp.exp(m - m_new) * acc + _bdot(p, v)
        return m_new, acc

    state = (jnp.full((tq, 1), NEG_BIG, F32), jnp.zeros((tq, V_PAD), F32))
    for c in range(kx_ref.shape[0] // ATTN_KEY_CHUNK):
        rows = slice(c * ATTN_KEY_CHUNK, (c + 1) * ATTN_KEY_CHUNK)
        state = chunk(kx_ref[rows, :], vx_ref[rows, :], state)
    _, acc = chunk(kc_ref[...], vc_ref[...], state)
    o_ref[...] = acc[:, :V_HEAD_DIM] / acc[:, V_HEAD_DIM:V_HEAD_DIM + 1]


def _attention(q, kx, kc, vx, vc, *, tq):
    b, h, s, _ = q.shape
    lc = kc.shape[2]
    return pl.pallas_call(
        _attn_kernel,
        grid=(b, h, s // tq),
        in_specs=[pl.BlockSpec((None, None, tq, HEAD_PAD), lambda bi, hi, qi: (bi, hi, qi, 0)),
                  pl.BlockSpec((None, None, s, HEAD_PAD), lambda bi, hi, qi: (bi, hi, 0, 0)),
                  pl.BlockSpec((None, None, lc, HEAD_PAD), lambda bi, hi, qi: (bi, hi, 0, 0)),
                  pl.BlockSpec((None, None, s, V_PAD), lambda bi, hi, qi: (bi, hi, 0, 0)),
                  pl.BlockSpec((None, None, lc, V_PAD), lambda bi, hi, qi: (bi, hi, 0, 0))],
        out_specs=pl.BlockSpec((None, tq, V_HEAD_DIM), lambda bi, hi, qi: (bi, qi, hi)),
        out_shape=jax.ShapeDtypeStruct((b, s, h * V_HEAD_DIM), F32),
        compiler_params=_cparams(("arbitrary", "arbitrary", "arbitrary")),
        name="attention",
    )(q, kx, kc, vx, vc)


def _seq_dft_kernel(ure_ref, uim_ref, r_ref, e_ref, t2_ref, o_ref, a_ref):
    s = pl.program_id(2)
    sub = SUBLANES
    n = 2 * GRID_W * sub
    cols = ure_ref.shape[-1]

    @pl.when(s < DFT_BLOCKS)
    def _():
        t = _bdot(r_ref[...].astype(BF16), e_ref[...])
        row = lax.broadcasted_iota(jnp.int32, (n, n), 0)
        col = lax.broadcasted_iota(jnp.int32, (n, n), 1)
        t = jnp.where((row % sub) == (col % sub), t, 0.0).astype(BF16)
        rhs = jnp.concatenate([ure_ref[...].reshape(GRID_W * sub, cols), uim_ref[...].reshape(GRID_W * sub, cols)],
                              axis=0).astype(BF16)
        a = _bdot(t, rhs)
        a_ref[:, :, pl.ds(pl.multiple_of(s * sub, sub), sub), :] = a.reshape(2, GRID_W, sub, cols)

    @pl.when(s >= DFT_BLOCKS)
    def _():
        k0 = pl.multiple_of((s - DFT_BLOCKS) * sub, sub)
        rhs = a_ref[:, pl.ds(k0, sub), :, :].reshape(2 * sub * GRID_W, cols).astype(BF16)
        y = _bdot(t2_ref[...].astype(BF16), rhs)
        o_ref[...] = y.reshape(GRID_W, sub, cols)


def _seq_dft_tables(n_seq):
    w, sub, nb = GRID_W, SUBLANES, DFT_BLOCKS
    ch = np.arange(nb).reshape(nb, 1, 1, 1)
    kb = np.arange(w).reshape(1, w, 1, 1)
    j = np.arange(sub).reshape(1, 1, sub, 1)
    r = np.arange(w).reshape(1, 1, 1, w)
    ang = (2.0 * np.pi / n_seq) * ((kb * (w * r + sub * ch + j)) % n_seq)
    c, s = np.cos(ang), np.sin(ang)
    rot = np.stack([np.stack([c, s], axis=3), np.stack([-s, c], axis=3)], axis=1)
    r1 = rot.reshape(nb, 2 * w * sub, 2 * w).astype(np.float32)
    expand = (np.arange(2 * w * sub)[None, :] // sub == np.arange(2 * w)[:, None]).astype(np.float32)
    ka = np.arange(w).reshape(w, 1)
    cp = np.arange(w).reshape(1, w)
    ang2 = (2.0 * np.pi / w) * ((ka * cp) % w)
    norm = 1.0 / math.sqrt(n_seq * FOURIER_GROUP_DIM)
    cs = np.stack([np.cos(ang2), np.sin(ang2)], axis=1) * norm
    eye = np.eye(sub)
    t2 = (cs[:, None, :, None, :] * eye[None, :, None, :, None]).reshape(w * sub, 2 * sub * w).astype(np.float32)
    return jnp.asarray(r1), jnp.asarray(expand).astype(BF16), jnp.asarray(t2)


def _seq_dft(u, batch, n_seq):
    assert n_seq == GRID_W * GRID_W
    w, sub, nb = GRID_W, SUBLANES, DFT_BLOCKS
    r1, expand, t2 = _seq_dft_tables(n_seq)
    halves = 2
    cols = FOURIER_DIM // halves
    u5 = u.reshape(batch, w, nb, sub, 2 * FOURIER_DIM)

    def u_spec(part):
        return pl.BlockSpec((None, w, None, sub, cols),
                            lambda b, h, s: (b, 0, jnp.minimum(s, nb - 1), 0, part * halves + h))

    y = pl.pallas_call(
        _seq_dft_kernel,
        grid=(batch, halves, 2 * nb),
        in_specs=[u_spec(0), u_spec(1),
                  pl.BlockSpec((None, 2 * w * sub, 2 * w), lambda b, h, s: (jnp.minimum(s, nb - 1), 0, 0)),
                  pl.BlockSpec((2 * w, 2 * w * sub), lambda b, h, s: (0, 0)),
                  pl.BlockSpec((w * sub, 2 * sub * w), lambda b, h, s: (0, 0))],
        out_specs=pl.BlockSpec((None, w, None, sub, cols), lambda b, h, s: (b, 0, jnp.maximum(s - nb, 0), 0, h)),
        out_shape=jax.ShapeDtypeStruct((batch, w, nb, sub, FOURIER_DIM), F32),
        scratch_shapes=[pltpu.VMEM((2, w, w, cols), F32)],
        compiler_params=_cparams(("arbitrary", "arbitrary", "arbitrary")),
        name="seq_dft",
    )(u5, u5, r1, expand, t2)
    return y.reshape(batch * n_seq, FOURIER_DIM)


def _out_router_kernel(x_ref, four_ref, attn_ref, gt1_ref, sh2_ref, sc2_ref, gf_ref, ga_ref, g2_ref,
                       wo_ref, wrh_ref, wrl_ref, br_ref,
                       xnew_ref, hm_ref, ri_ref, rw_ref, cnt_ref, carry_ref, *, mod_row):
    i = pl.program_id(0)
    tm = x_ref.shape[0]
    row = pl.ds(mod_row(i), 1)

    @pl.when(i == 0)
    def _():
        carry_ref[...] = jnp.zeros_like(carry_ref)

    def norm(v, g):
        return (v * lax.rsqrt(jnp.mean(v * v, axis=-1, keepdims=True) + EPS) * g).astype(BF16)

    mix = (_bdot(norm(four_ref[...], gf_ref[...]), wo_ref[0:FOURIER_DIM, :])
           + _bdot(norm(attn_ref[...], ga_ref[...]), wo_ref[FOURIER_DIM:FOURIER_DIM + MLA_DIM, :]))
    xn = x_ref[...] + gt1_ref[row, :] * mix
    xnew_ref[...] = xn
    ms = jnp.mean(xn * xn, axis=-1, keepdims=True)
    hm = xn * lax.rsqrt(ms + EPS) * (g2_ref[...] * (1.0 + sc2_ref[row, :])) + sh2_ref[row, :]
    hm_ref[...] = _pack_halves(hm)

    hm_hi = hm.astype(BF16)
    hm_lo = (hm - hm_hi.astype(F32)).astype(BF16)
    logits = _bdot(hm_hi, wrh_ref[...]) + _bdot(hm_lo, wrh_ref[...]) + _bdot(hm_hi, wrl_ref[...]) + br_ref[...]
    lane = lax.broadcasted_iota(jnp.int32, (tm, LANES), 1)
    lanef = lane.astype(F32)
    far = 1e9

    lg = jnp.where(lane < N_GROUPS, logits, NEG_BIG)
    m1 = jnp.max(lg, axis=-1, keepdims=True)
    g_p = 1.0 / jnp.sum(jnp.exp(lg - m1), axis=-1, keepdims=True)
    gidx = jnp.min(jnp.where(lg >= m1, lanef, far), axis=-1, keepdims=True)
    lo = N_GROUPS + EXPERTS_PER_GROUP * gidx
    in_group = jnp.where(lanef >= lo, jnp.where(lanef < lo + EXPERTS_PER_GROUP, 1.0, 0.0), 0.0) > 0.5
    le = jnp.where(in_group, logits, NEG_BIG)
    m2 = jnp.max(le, axis=-1, keepdims=True)
    idx1 = jnp.min(jnp.where(le >= m2, lanef, far), axis=-1, keepdims=True)
    le2 = jnp.where(lanef == idx1, NEG_BIG, le)
    m3 = jnp.max(le2, axis=-1, keepdims=True)
    idx2 = jnp.min(jnp.where(le2 >= m3, lanef, far), axis=-1, keepdims=True)
    t = jnp.exp(m3 - m2)
    p1 = 1.0 / (1.0 + t)
    p2 = t / (1.0 + t)
    e1 = idx1 - N_GROUPS
    e2 = idx2 - N_GROUPS

    oh1 = jnp.where(lanef == e1, 1.0, 0.0)
    oh2 = jnp.where(lanef == e2, 1.0, 0.0)
    ohs = oh1 + oh2
    row = lax.broadcasted_iota(jnp.int32, (tm, tm), 0)
    col = lax.broadcasted_iota(jnp.int32, (tm, tm), 1)
    tri = jnp.where(row > col, 1.0, 0.0).astype(BF16)
    before = _bdot(tri, ohs.astype(BF16)) + carry_ref[...]
    rank1 = jnp.sum(oh1 * before, axis=-1, keepdims=True)
    rank2 = jnp.sum(oh2 * before, axis=-1, keepdims=True)
    carry = carry_ref[...] + jnp.sum(ohs, axis=0, keepdims=True)
    carry_ref[...] = carry
    cnt_ref[...] = jnp.broadcast_to(carry, cnt_ref.shape)

    ri = jnp.where(lane == 0, e1, jnp.where(lane == 1, e2, jnp.where(lane == 2, rank1, jnp.where(lane == 3, rank2, 0.0))))
    ri_ref[...] = ri.astype(jnp.int32)
    rw_ref[...] = jnp.where(lane == 0, g_p * p1, jnp.where(lane == 1, g_p * p2, 0.0))


def _out_router(x2, four, attn, mods, gf, ga, g2, wo, wrh, wrl, br, *, seq, tm):
    t, d = x2.shape
    nt = t // tm
    tiles_per_b = seq // tm

    def rows(n):
        return pl.BlockSpec((tm, n), lambda i: (i, 0))

    def const(arr):
        return pl.BlockSpec(arr.shape, lambda i: (0,) * arr.ndim, pipeline_mode=pl.Buffered(1))

    return pl.pallas_call(
        functools.partial(_out_router_kernel, mod_row=lambda i: i // tiles_per_b),
        grid=(nt,),
        in_specs=[rows(d), rows(FOURIER_DIM), rows(MLA_DIM), _mod_spec(mods, 2, d), _mod_spec(mods, 3, d),
                  _mod_spec(mods, 4, d),
                  const(gf), const(ga), const(g2), const(wo), const(wrh), const(wrl), const(br)],
        out_specs=[rows(d), rows(d // 2), rows(LANES), rows(LANES), pl.BlockSpec((8, LANES), lambda i: (0, 0))],
        out_shape=[jax.ShapeDtypeStruct((t, d), F32), jax.ShapeDtypeStruct((t, d // 2), jnp.uint32),
                   jax.ShapeDtypeStruct((t, LANES), jnp.int32), jax.ShapeDtypeStruct((t, LANES), F32),
                   jax.ShapeDtypeStruct((8, LANES), F32)],
        scratch_shapes=[pltpu.VMEM((1, LANES), F32)],
        compiler_params=_cparams(("arbitrary",)),
        name="out_proj_router",
    )(x2, four, attn, mods, mods, mods, gf, ga, g2, wo, wrh, wrl, br)


def _moe_kernel(item_e, item_blk0, item_nblk, row_tok, used_blocks,
                hm_hbm, wg_hbm, wu_hbm, wd_hbm, y_hbm,
                xg, xb, gs, ab, yp, gu_buf, dn_buf, gsem, osem, gusem, dnsem):
    i = pl.program_id(0)
    j = pl.program_id(1)
    n_items = pl.num_programs(0)
    nj = pl.num_programs(1)
    slot = i % 2
    nblk = item_nblk[i]

    def weight_copy(it, ph):
        if ph == 2:
            ws = it % DN_SLOTS
            return pltpu.make_async_copy(wd_hbm.at[item_e[it]], dn_buf.at[ws], dnsem.at[ws])
        ws = (2 * it + ph) % GU_SLOTS
        return pltpu.make_async_copy((wg_hbm, wu_hbm)[ph].at[item_e[it]], gu_buf.at[ws], gusem.at[ws])

    def start_weight(it, ph):
        it_c = jnp.minimum(it, n_items - 1)

        @pl.when(jnp.logical_and(it < n_items, item_nblk[it_c] > 0))
        def _():
            weight_copy(it_c, ph).start(priority=WEIGHT_DMA_PRIORITY)

    def gather_copy(tok, sl, r):
        return pltpu.make_async_copy(hm_hbm.at[pl.ds(tok, 1)], xg.at[sl, pl.ds(r, 1)], gsem.at[sl])

    def issue_gather(it, sl):
        r0 = item_blk0[it] * MOE_BLOCK

        def body(r8, carry):
            for k in range(GATHER_UNROLL):
                r = r8 * GATHER_UNROLL + k
                gather_copy(row_tok[r0 + r], sl, r).start(priority=ROW_DMA_PRIORITY)
            return carry

        lax.fori_loop(0, item_nblk[it] * (MOE_BLOCK // GATHER_UNROLL), body, 0)

    def wait_gather(it, sl):
        for b in range(ITEM_BLOCKS):
            @pl.when(b < item_nblk[it])
            def _():
                pltpu.make_async_copy(hm_hbm.at[pl.ds(0, MOE_BLOCK)], xg.at[sl, pl.ds(b * MOE_BLOCK, MOE_BLOCK)],
                                      gsem.at[sl]).wait()

    def out_copy(it, m):
        r0 = pl.multiple_of(item_blk0[it] * MOE_BLOCK, MOE_BLOCK)
        return pltpu.make_async_copy(yp.at[pl.ds(0, m)], y_hbm.at[pl.ds(r0, m)], osem.at[0])

    def wait_out(it):
        for nb in range(1, ITEM_BLOCKS + 1):
            @pl.when(item_nblk[it] == nb)
            def _():
                out_copy(it, nb * MOE_BLOCK).wait()

    @pl.when(j == 0)
    def _():
        @pl.when(i == 0)
        def _():
            start_weight(0, 0)
            start_weight(0, 1)
            start_weight(1, 0)
            start_weight(0, 2)
            issue_gather(0, 0)

        start_weight(i + 1, 2)
        wait_gather(i, slot)

        @pl.when(i + 1 < n_items)
        def _():
            issue_gather(i + 1, 1 - slot)

    @pl.when(j == 1)
    def _():
        start_weight(i + 1, 1)

    @pl.when(j == 2)
    def _():
        start_weight(i + 2, 0)

        @pl.when(i > 0)
        def _():
            wait_out(i - 1)

    for ph in range(3):
        @pl.when(jnp.logical_and(j == ph, nblk > 0))
        def _():
            weight_copy(i, ph).wait()

    for nb in range(1, ITEM_BLOCKS + 1):
        m = nb * MOE_BLOCK

        @pl.when(jnp.logical_and(nblk == nb, j == 0))
        def _():
            hi, lo = _unpack_halves(xg[slot, 0:m, :])
            half = hi.shape[1]
            xb[0:m, 0:half] = hi.astype(BF16)
            xb[0:m, half:2 * half] = lo.astype(BF16)
            gs[0:m, :] = _bdot(xb[0:m, :], gu_buf[(2 * i) % GU_SLOTS].astype(BF16))

        @pl.when(jnp.logical_and(nblk == nb, j == 1))
        def _():
            g = gs[0:m, :]
            u = _bdot(xb[0:m, :], gu_buf[(2 * i + 1) % GU_SLOTS].astype(BF16))
            ab[0:m, :] = (g * jax.nn.sigmoid(g) * u).astype(BF16)

        @pl.when(jnp.logical_and(nblk == nb, j == 2))
        def _():
            yp[0:m, :] = _pack_halves(_bdot(ab[0:m, :], dn_buf[i % DN_SLOTS].astype(BF16)))
            out_copy(i, m).start()

    @pl.when(jnp.logical_and(i == n_items - 1, j == nj - 1))
    def _():
        wait_out(i)
        n_blocks = y_hbm.shape[0] // MOE_BLOCK
        yp[0:MOE_BLOCK, :] = jnp.zeros((MOE_BLOCK, yp.shape[1]), jnp.uint32)

        def tail_copy(blk):
            r0 = pl.multiple_of(blk * MOE_BLOCK, MOE_BLOCK)
            return pltpu.make_async_copy(yp.at[pl.ds(0, MOE_BLOCK)], y_hbm.at[pl.ds(r0, MOE_BLOCK)], osem.at[0])

        def start_body(blk, carry):
            tail_copy(blk).start()
            return carry

        def wait_body(blk, carry):
            tail_copy(blk).wait()
            return carry

        lax.fori_loop(used_blocks[0], n_blocks, start_body, 0)
        lax.fori_loop(used_blocks[0], n_blocks, wait_body, 0)


def _moe(hm, w_gate, w_up, w_down, item_e, item_blk0, item_nblk, row_tok, used_blocks, *, n_rows):
    d, de = w_gate.shape[1], w_gate.shape[2]
    n_items = item_e.shape[0]
    nj = 3
    rows = ITEM_BLOCKS * MOE_BLOCK
    any_spec = pl.BlockSpec(memory_space=pl.ANY)
    grid_spec = pltpu.PrefetchScalarGridSpec(
        num_scalar_prefetch=5,
        grid=(n_items, nj),
        in_specs=[any_spec, any_spec, any_spec, any_spec],
        out_specs=any_spec,
        scratch_shapes=[pltpu.VMEM((2, rows, d // 2), jnp.uint32),
                        pltpu.VMEM((rows, d), BF16),
                        pltpu.VMEM((rows, de), F32),
                        pltpu.VMEM((rows, de), BF16),
                        pltpu.VMEM((rows, d // 2), jnp.uint32),
                        pltpu.VMEM((GU_SLOTS, d, de), F32),
                        pltpu.VMEM((DN_SLOTS, de, d), F32),
                        pltpu.SemaphoreType.DMA((2,)),
                        pltpu.SemaphoreType.DMA((1,)),
                        pltpu.SemaphoreType.DMA((GU_SLOTS,)),
                        pltpu.SemaphoreType.DMA((DN_SLOTS,))],
    )
    return pl.pallas_call(
        _moe_kernel,
        grid_spec=grid_spec,
        out_shape=jax.ShapeDtypeStruct((n_rows, d // 2), jnp.uint32),
        compiler_params=_cparams(("arbitrary", "arbitrary")),
        name="moe_experts",
    )(item_e, item_blk0, item_nblk, row_tok, used_blocks, hm, w_gate, w_up, w_down)


def _combine_kernel(dest, x_ref, gt2_ref, rw_ref, y_hbm, o_ref, ybuf, sem, *, mod_row):
    i = pl.program_id(0)
    n = pl.num_programs(0)
    tm = x_ref.shape[0]
    slot = i % 2

    def issue(it, sl):
        base = it * tm

        def body(r4, carry):
            for rr in range(GATHER_UNROLL // 2):
                r = r4 * (GATHER_UNROLL // 2) + rr
                for k in range(2):
                    pltpu.make_async_copy(y_hbm.at[pl.ds(dest[2 * (base + r) + k], 1)], ybuf.at[sl, k, pl.ds(r, 1)],
                                          sem.at[sl]).start(priority=k)
            return carry

        lax.fori_loop(0, tm // (GATHER_UNROLL // 2), body, 0)

    @pl.when(i == 0)
    def _():
        issue(0, 0)

    for k in range(2):
        pltpu.make_async_copy(y_hbm.at[pl.ds(0, tm)], ybuf.at[slot, k], sem.at[slot]).wait()

    @pl.when(i + 1 < n)
    def _():
        issue(i + 1, 1 - slot)

    w = rw_ref[...]
    gate = gt2_ref[pl.ds(mod_row(i), 1), :]
    hi0, lo0 = _unpack_halves(ybuf[slot, 0])
    hi1, lo1 = _unpack_halves(ybuf[slot, 1])
    half = hi0.shape[1]
    o_ref[:, 0:half] = x_ref[:, 0:half] + gate[:, 0:half] * (w[:, 0:1] * hi0 + w[:, 1:2] * hi1)
    o_ref[:, half:2 * half] = (x_ref[:, half:2 * half]
                               + gate[:, half:2 * half] * (w[:, 0:1] * lo0 + w[:, 1:2] * lo1))


def _combine(dest, xnew, mods, rw, y, *, seq, tm):
    t, d = xnew.shape
    tiles_per_b = seq // tm
    grid_spec = pltpu.PrefetchScalarGridSpec(
        num_scalar_prefetch=1,
        grid=(t // tm,),
        in_specs=[pl.BlockSpec((tm, d), lambda i, ds: (i, 0)),
                  _mod_spec(mods, 5, d),
                  pl.BlockSpec((tm, LANES), lambda i, ds: (i, 0)),
                  pl.BlockSpec(memory_space=pl.ANY)],
        out_specs=pl.BlockSpec((tm, d), lambda i, ds: (i, 0)),
        scratch_shapes=[pltpu.VMEM((2, 2, tm, d // 2), jnp.uint32), pltpu.SemaphoreType.DMA((2,))],
    )
    return pl.pallas_call(
        functools.partial(_combine_kernel, mod_row=lambda i: i // tiles_per_b),
        grid_spec=grid_spec,
        out_shape=jax.ShapeDtypeStruct((t, d), F32),
        compiler_params=_cparams(("arbitrary",)),
        name="moe_combine",
    )(dest, xnew, mods, rw, y)


def _rope_tables(n_tokens):
    rows = n_tokens // GRID_W
    row = jnp.repeat(jnp.arange(rows, dtype=jnp.int32), GRID_W).astype(F32)
    col = jnp.tile(jnp.arange(GRID_W, dtype=jnp.int32), rows).astype(F32)
    n_freq = QK_ROPE_DIM // 4
    inv = ROPE_THETA ** (-jnp.arange(n_freq, dtype=F32) / n_freq)
    ar = row[:, None] * inv[None, :]
    ac = col[:, None] * inv[None, :]
    cr, sr, cc, sc = jnp.cos(ar), jnp.sin(ar), jnp.cos(ac), jnp.sin(ac)
    cos64 = jnp.concatenate([cr, cr, cc, cc], axis=-1)
    sin64 = jnp.concatenate([-sr, sr, -sc, sc], axis=-1)
    return jnp.tile(cos64, (1, 2)), jnp.tile(sin64, (1, 2))


def _channel_dft_table():
    c = np.arange(FOURIER_GROUP_DIM).reshape(-1, 1)
    k = np.arange(FOURIER_GROUP_DIM).reshape(1, -1)
    ang = (2.0 * np.pi / FOURIER_GROUP_DIM) * ((c * k) % FOURIER_GROUP_DIM)
    return jnp.asarray(np.concatenate([np.cos(ang), -np.sin(ang)], axis=1).astype(np.float32)).astype(BF16)


def _split_heads(w, widths):
    k = w.shape[0]
    wh = w.reshape(k, MLA_HEADS, sum(widths))
    outs, off = [], 0
    for wd in widths:
        outs.append(wh[:, :, off:off + wd].reshape(k, MLA_HEADS * wd))
        off += wd
    return outs


def kernel(x, c, ctx, c_ctx, w_ada, b_ada, g_norm1, g_norm2, w_in, g_q_a, g_kv_a, w_uq, w_ukv, g_qk_q, g_qk_k,
           g_out_four, g_out_attn, w_out, w_router_group, b_router_group, w_router_expert, b_router_expert,
           w_gate, w_up, w_down):
    b, s, d = x.shape
    lc = ctx.shape[1]
    t = b * s
    layer_params = (w_ada, b_ada, g_norm1, g_norm2, w_in, g_q_a, g_kv_a, w_uq, w_ukv, g_qk_q, g_qk_k, g_out_four,
                    g_out_attn, w_out, w_router_group, b_router_group, w_router_expert, b_router_expert,
                    w_gate, w_up, w_down)
    assert all(p.shape[0] == 1 for p in layer_params), "single-layer block"
    (w_ada, b_ada, g_norm1, g_norm2, w_in, g_q_a, g_kv_a, w_uq, w_ukv, g_qk_q, g_qk_k, g_out_four,
     g_out_attn, w_out, w_router_group, b_router_group, w_router_expert, b_router_expert,
     w_gate, w_up, w_down) = [p.reshape(p.shape[1:]) for p in layer_params]

    cond8 = jnp.concatenate([c, c_ctx[None, :], jnp.zeros((8 - b - 1, d), F32)], axis=0)
    mods = _ada_mod(cond8, w_ada, b_ada)

    w_all = jnp.concatenate([w_in, w_in[:, ROPE_OFF:]], axis=1).astype(BF16)
    dc = _channel_dft_table()
    wqn, wqr = [w.astype(BF16) for w in _split_heads(w_uq, (QK_NOPE_DIM, QK_ROPE_DIM))]
    wkn, wv = [w.astype(BF16) for w in _split_heads(w_ukv, (QK_NOPE_DIM, V_HEAD_DIM))]
    gqn = g_qk_q[:QK_NOPE_DIM].reshape(1, -1)
    gqr2 = jnp.tile(g_qk_q[QK_NOPE_DIM:], 2).reshape(1, -1)
    gkn = g_qk_k[:QK_NOPE_DIM].reshape(1, -1)
    gkr2 = jnp.tile(g_qk_k[QK_NOPE_DIM:], 2).reshape(1, -1)
    g1 = g_norm1.reshape(1, d)
    gq = g_q_a.reshape(1, -1)
    gkv = g_kv_a.reshape(1, -1)

    x2 = x.reshape(t, d)
    u, cq, ckv, kr2 = _in_proj(x2, mods, 0, s, g1, gq, gkv, w_all, dc, with_q=True, tm=IN_PROJ_ROWS)
    ckv_c, kr2_c = _in_proj(ctx.reshape(b * lc, d), mods, b, b * lc, g1, gq, gkv, w_all, dc, with_q=False, tm=lc)

    cos_t, sin_t = _rope_tables(s)
    q, kx, vx = _qkv(cq, ckv, kr2, cos_t, sin_t, wqn, wqr, gqn, gqr2, wkn, wv, gkn, gkr2,
                     batch=b, seq=s, tm=QKV_ROWS, with_q=True, with_rope=True)
    kc, vc = _qkv(None, ckv_c, kr2_c, None, None, None, None, None, None, wkn, wv, gkn, gkr2,
                  batch=b, seq=lc, tm=lc, with_q=False, with_rope=False)

    attn = _attention(q, kx, kc, vx, vc, tq=ATTN_Q_ROWS).reshape(t, MLA_DIM)
    four = _seq_dft(u, b, s)

    wo = w_out.astype(BF16)
    n_route = N_GROUPS + N_EXPERTS
    wrt = jnp.concatenate([w_router_group, w_router_expert, jnp.zeros((d, LANES - n_route), F32)], axis=1)
    wrt_hi = wrt.astype(BF16)
    wrt_lo = (wrt - wrt_hi.astype(F32)).astype(BF16)
    brt = jnp.concatenate([b_router_group, b_router_expert, jnp.zeros((LANES - n_route,), F32)]).reshape(1, -1)
    xnew, hm, ri, rw, cnt = _out_router(x2, four, attn, mods, g_out_four.reshape(1, -1), g_out_attn.reshape(1, -1),
                                        g_norm2.reshape(1, d), wo, wrt_hi, wrt_lo, brt, seq=s, tm=OUT_ROUTER_ROWS)

    counts = cnt[0, :N_EXPERTS].astype(jnp.int32)
    nblk_e = (counts + MOE_BLOCK - 1) // MOE_BLOCK
    blk_end = jnp.cumsum(nblk_e)
    blk_start = blk_end - nblk_e
    n_slots = t * 2
    n_blocks = -(-(n_slots + N_EXPERTS * (MOE_BLOCK - 1)) // MOE_BLOCK)
    n_rows = n_blocks * MOE_BLOCK
    e12 = ri[:, 0:2]
    seg_start = jnp.sum(jnp.where(e12[:, :, None] == jnp.arange(N_EXPERTS, dtype=jnp.int32), blk_start, 0), axis=-1)
    dest = (seg_start * MOE_BLOCK + ri[:, 2:4]).reshape(-1)
    tok_ids = jnp.repeat(jnp.arange(t, dtype=jnp.int32), 2)
    row_tok = jnp.zeros((n_rows,), jnp.int32).at[dest].set(tok_ids)
    items_e = (nblk_e + ITEM_BLOCKS - 1) // ITEM_BLOCKS
    item_end = jnp.cumsum(items_e)
    n_items = (n_blocks + (ITEM_BLOCKS - 1) * N_EXPERTS) // ITEM_BLOCKS
    idx = jnp.arange(n_items, dtype=jnp.int32)
    total = item_end[-1]
    idx_c = jnp.minimum(idx, total - 1)
    ie = jnp.minimum(jnp.sum(idx_c[:, None] >= item_end[None, :], axis=1), N_EXPERTS - 1).astype(jnp.int32)
    local = idx_c - (item_end - items_e)[ie]
    item_blk0 = (blk_start[ie] + ITEM_BLOCKS * local).astype(jnp.int32)
    item_nblk = jnp.where(idx < total, jnp.clip(nblk_e[ie] - ITEM_BLOCKS * local, 0, ITEM_BLOCKS), 0).astype(jnp.int32)

    used_blocks = blk_end[-1:].astype(jnp.int32)
    y = _moe(hm, w_gate, w_up, w_down, ie, item_blk0, item_nblk, row_tok, used_blocks, n_rows=n_rows)
    out = _combine(dest.astype(jnp.int32), xnew, mods, rw, y, seq=s, tm=COMBINE_ROWS)
    return out.reshape(b, s, d)
```

```python
import functools
import math

import numpy as np
import jax
import jax.numpy as jnp
from jax import lax
from jax.experimental import pallas as pl
from jax.experimental.pallas import tpu as pltpu

F32 = jnp.float32
BF16 = jnp.bfloat16

D_MODEL = 2048
GRID_W = 64
EPS = 1e-6
N_MOD = 6
N_FOURIER_GROUPS = 4
FOURIER_GROUP_DIM = 256
FOURIER_DIM = 1024
MLA_HEADS = 8
QK_NOPE_DIM = 128
QK_ROPE_DIM = 64
QK_HEAD_DIM = 192
V_HEAD_DIM = 128
Q_LORA_RANK = 768
KV_LORA_RANK = 512
MLA_DIM = 1024
ROPE_THETA = 10000.0
Q_OFF = FOURIER_DIM
KV_OFF = Q_OFF + Q_LORA_RANK
ROPE_OFF = KV_OFF + KV_LORA_RANK
N_GROUPS = 8
EXPERTS_PER_GROUP = 8
N_EXPERTS = 64
D_EXPERT = 768

ROT_HALF = QK_ROPE_DIM // 4
ROT_BLOCK = 2 * ROT_HALF

ADA_COLS = 1024
IN_PROJ_ROWS = 512
QKV_ROWS = 512
ATTN_Q_ROWS = 1024
OUT_ROUTER_ROWS = 512
COMBINE_ROWS = 256
DISPATCH_ROWS = 512

LANES = 128
SUBLANES = 8
DFT_BLOCKS = GRID_W // SUBLANES
HEAD_PAD = 256
V_PAD = 256
MOE_BLOCK = 64
ITEM_BLOCKS = 8
GATHER_UNROLL = 8
GU_SLOTS = 3
DN_SLOTS = 2
ATTN_KEY_CHUNK = 512
ROW_DMA_PRIORITY = 0
WEIGHT_DMA_PRIORITY = 1
VMEM_LIMIT = 56 * 1024 * 1024
NEG_BIG = -1e30


def _cparams(sem):
    return pltpu.CompilerParams(dimension_semantics=sem, vmem_limit_bytes=VMEM_LIMIT)


def _bdot(a, b):
    return jnp.dot(a, b, preferred_element_type=F32)


def _pack_halves(v):
    n = v.shape[1] // 2
    hi = pltpu.bitcast(v[:, :n].astype(BF16).astype(F32), jnp.uint32)
    lo = pltpu.bitcast(v[:, n:].astype(BF16).astype(F32), jnp.uint32)
    return hi | (lo >> 16)


def _unpack_halves(u):
    hi = pltpu.bitcast(u & jnp.uint32(0xFFFF0000), F32)
    lo = pltpu.bitcast(u << 16, F32)
    return hi, lo


def _ada_kernel(c_ref, w_ref, b_ref, o_ref):
    c = c_ref[...]
    s = (c * jax.nn.sigmoid(c)).astype(BF16)
    o_ref[...] = _bdot(s, w_ref[...].astype(BF16)) + b_ref[...]


def _ada_mod(cond8, w_ada, b_ada):
    d, n = w_ada.shape
    tn = ADA_COLS
    return pl.pallas_call(
        _ada_kernel,
        grid=(n // tn,),
        in_specs=[pl.BlockSpec((8, d), lambda i: (0, 0)),
                  pl.BlockSpec((d, tn), lambda i: (0, i)),
                  pl.BlockSpec((1, tn), lambda i: (0, i))],
        out_specs=pl.BlockSpec((8, tn), lambda i: (0, i)),
        out_shape=jax.ShapeDtypeStruct((8, n), F32),
        compiler_params=_cparams(("arbitrary",)),
        name="ada_mod",
    )(cond8, w_ada, b_ada.reshape(1, n))


def _in_proj_kernel(x_ref, sh_ref, sc_ref, g_ref, gq_ref, gkv_ref, w_ref, dc_ref, *out_refs, with_q, mod_row):
    row = pl.ds(mod_row(pl.program_id(0)), 1)
    x = x_ref[...]
    ms = jnp.mean(x * x, axis=-1, keepdims=True)
    a = g_ref[...] * (1.0 + sc_ref[row, :])
    h = (x * lax.rsqrt(ms + EPS) * a + sh_ref[row, :]).astype(BF16)
    if with_q:
        u_ref, cq_ref, ckv_ref, kr_ref = out_refs
        f = _bdot(h, w_ref[:, 0:Q_OFF]).astype(BF16)
        dc = dc_ref[...]
        for g in range(N_FOURIER_GROUPS):
            lo = g * FOURIER_GROUP_DIM
            ug = _bdot(f[:, lo:lo + FOURIER_GROUP_DIM], dc)
            u_ref[:, lo:lo + FOURIER_GROUP_DIM] = ug[:, :FOURIER_GROUP_DIM]
            u_ref[:, FOURIER_DIM + lo:FOURIER_DIM + lo + FOURIER_GROUP_DIM] = ug[:, FOURIER_GROUP_DIM:]
        pq = _bdot(h, w_ref[:, Q_OFF:KV_OFF])
        msq = jnp.mean(pq * pq, axis=-1, keepdims=True)
        cq_ref[...] = (pq * lax.rsqrt(msq + EPS) * gq_ref[...]).astype(BF16)
    else:
        ckv_ref, kr_ref = out_refs
    pkv = _bdot(h, w_ref[:, KV_OFF:ROPE_OFF])
    mskv = jnp.mean(pkv * pkv, axis=-1, keepdims=True)
    ckv_ref[...] = (pkv * lax.rsqrt(mskv + EPS) * gkv_ref[...]).astype(BF16)
    kr_ref[...] = _bdot(h, w_ref[:, ROPE_OFF:ROPE_OFF + LANES])


def _mod_spec(mods, k, d):
    return pl.BlockSpec((mods.shape[0], d), lambda i, *_: (0, k))


def _in_proj(x2, mods, first_row, rows_per_mod, g1, gq, gkv, w_all, dc, *, with_q, tm):
    t, d = x2.shape
    nt = t // tm
    tiles_per_mod = rows_per_mod // tm

    def const(shape):
        return pl.BlockSpec(shape, lambda i: (0,) * len(shape))

    in_specs = [pl.BlockSpec((tm, d), lambda i: (i, 0)), _mod_spec(mods, 0, d), _mod_spec(mods, 1, d),
                const((1, d)), const((1, Q_LORA_RANK)), const((1, KV_LORA_RANK)),
                const(w_all.shape), const(dc.shape)]

    def rows(n):
        return pl.BlockSpec((tm, n), lambda i: (i, 0))

    out_specs = [rows(KV_LORA_RANK), rows(LANES)]
    out_shape = [jax.ShapeDtypeStruct((t, KV_LORA_RANK), BF16), jax.ShapeDtypeStruct((t, LANES), F32)]
    if with_q:
        out_specs = [rows(2 * FOURIER_DIM), rows(Q_LORA_RANK)] + out_specs
        out_shape = [jax.ShapeDtypeStruct((t, 2 * FOURIER_DIM), F32),
                     jax.ShapeDtypeStruct((t, Q_LORA_RANK), BF16)] + out_shape
    return pl.pallas_call(
        functools.partial(_in_proj_kernel, with_q=with_q, mod_row=lambda i: first_row + i // tiles_per_mod),
        grid=(nt,),
        in_specs=in_specs,
        out_specs=out_specs,
        out_shape=out_shape,
        compiler_params=_cparams(("arbitrary",)),
        name="in_proj_x" if with_q else "in_proj_ctx",
    )(x2, mods, mods, g1, gq, gkv, w_all, dc)


def _swap_halves(y, first_half):
    return jnp.where(first_half, pltpu.roll(y, LANES - ROT_HALF, 1), pltpu.roll(y, ROT_HALF, 1))


def _qkv_kernel(*refs, with_q, with_rope):
    it = iter(refs)
    if with_q:
        cq_ref = next(it)
    ckv_ref = next(it)
    kr_ref = next(it)
    if with_rope:
        cos_ref = next(it)
        sin_ref = next(it)
    if with_q:
        wqn_ref = next(it)
        wqr_ref = next(it)
        gqn_ref = next(it)
        gqr_ref = next(it)
    wkn_ref = next(it)
    wv_ref = next(it)
    gkn_ref = next(it)
    gkr_ref = next(it)
    if with_q:
        q_ref = next(it)
    k_ref = next(it)
    v_ref = next(it)

    tm = ckv_ref.shape[0]
    lane = lax.broadcasted_iota(jnp.int32, (tm, LANES), 1)
    low = lane < QK_ROPE_DIM
    first_half = (lane % ROT_BLOCK) < ROT_HALF
    inv_dim = 1.0 / QK_HEAD_DIM

    def rope(y):
        if not with_rope:
            return y
        return y * cos_ref[...] + _swap_halves(y, first_half) * sin_ref[...]

    if with_q:
        cq = cq_ref[...]
        qn = _bdot(cq, wqn_ref[...])
        qr = _bdot(cq, wqr_ref[...])
        qscale = QK_HEAD_DIM ** -0.5
        for p in range(MLA_HEADS // 2):
            blk = qr[:, p * LANES:(p + 1) * LANES]
            sq = blk * blk
            ss_lo = jnp.sum(jnp.where(low, sq, 0.0), axis=-1, keepdims=True)
            ss_hi = jnp.sum(jnp.where(low, 0.0, sq), axis=-1, keepdims=True)
            scales = []
            for hh, ssr in ((2 * p, ss_lo), (2 * p + 1, ss_hi)):
                nh = qn[:, hh * LANES:(hh + 1) * LANES]
                ssq = jnp.sum(nh * nh, axis=-1, keepdims=True) + ssr
                s = lax.rsqrt(ssq * inv_dim + EPS)
                scales.append(s)
                q_ref[hh, :, 0:LANES] = (nh * s * gqn_ref[...] * qscale).astype(BF16)
            s_pair = jnp.where(low, scales[0], scales[1])
            r = rope(blk * s_pair * gqr_ref[...]) * qscale
            q_ref[2 * p, :, LANES:2 * LANES] = jnp.where(low, r, 0.0).astype(BF16)
            q_ref[2 * p + 1, :, LANES:2 * LANES] = jnp.where(low, pltpu.roll(r, QK_ROPE_DIM, 1), 0.0).astype(BF16)

    ckv = ckv_ref[...]
    kn = _bdot(ckv, wkn_ref[...])
    v = _bdot(ckv, wv_ref[...])
    kr = kr_ref[...]
    ss_r = jnp.sum(jnp.where(low, kr * kr, 0.0), axis=-1, keepdims=True)
    base = rope(kr * gkr_ref[...])
    ones_col = jnp.where(lane == 0, 1.0, 0.0).astype(BF16)
    for hh in range(MLA_HEADS):
        nh = kn[:, hh * LANES:(hh + 1) * LANES]
        ssq = jnp.sum(nh * nh, axis=-1, keepdims=True) + ss_r
        s = lax.rsqrt(ssq * inv_dim + EPS)
        k_ref[hh, :, 0:LANES] = (nh * s * gkn_ref[...]).astype(BF16)
        k_ref[hh, :, LANES:2 * LANES] = jnp.where(low, base * s, 0.0).astype(BF16)
        v_ref[hh, :, 0:LANES] = v[:, hh * LANES:(hh + 1) * LANES].astype(BF16)
        v_ref[hh, :, LANES:2 * LANES] = ones_col


def _qkv(cq, ckv, kr2, cos_t, sin_t, wqn, wqr, gqn, gqr2, wkn, wv, gkn, gkr2, *, batch, seq, tm, with_q,
         with_rope):
    t = ckv.shape[0]
    nt = t // tm
    tiles_per_b = seq // tm

    def rows(n):
        return pl.BlockSpec((tm, n), lambda i: (i, 0))

    def const(arr):
        return pl.BlockSpec(arr.shape, lambda i: (0,) * arr.ndim)

    tab_spec = pl.BlockSpec((tm, LANES), lambda i: (i % tiles_per_b, 0))

    def head_out(width):
        return pl.BlockSpec((None, MLA_HEADS, tm, width), lambda i: (i // tiles_per_b, 0, i % tiles_per_b, 0))

    args, in_specs = [], []
    if with_q:
        args.append(cq)
        in_specs.append(rows(Q_LORA_RANK))
    args += [ckv, kr2]
    in_specs += [rows(KV_LORA_RANK), rows(LANES)]
    if with_rope:
        args += [cos_t, sin_t]
        in_specs += [tab_spec, tab_spec]
    if with_q:
        args += [wqn, wqr, gqn, gqr2]
        in_specs += [const(wqn), const(wqr), const(gqn), const(gqr2)]
    args += [wkn, wv, gkn, gkr2]
    in_specs += [const(wkn), const(wv), const(gkn), const(gkr2)]

    out_specs = [head_out(HEAD_PAD), head_out(V_PAD)]
    out_shape = [jax.ShapeDtypeStruct((batch, MLA_HEADS, seq, HEAD_PAD), BF16),
                 jax.ShapeDtypeStruct((batch, MLA_HEADS, seq, V_PAD), BF16)]
    if with_q:
        out_specs = [head_out(HEAD_PAD)] + out_specs
        out_shape = [jax.ShapeDtypeStruct((batch, MLA_HEADS, seq, HEAD_PAD), BF16)] + out_shape
    return pl.pallas_call(
        functools.partial(_qkv_kernel, with_q=with_q, with_rope=with_rope),
        grid=(nt,),
        in_specs=in_specs,
        out_specs=out_specs,
        out_shape=out_shape,
        compiler_params=_cparams(("arbitrary",)),
        name="qkv_x" if with_q else "kv_ctx",
    )(*args)


def _attn_kernel(q_ref, kx_ref, kc_ref, vx_ref, vc_ref, o_ref):
    q = q_ref[...]
    tq = q.shape[0]
    dn = (((1,), (1,)), ((), ()))

    def chunk(k, v, state):
        m, acc = state
        s = lax.dot_general(q, k, dn, preferred_element_type=F32)
        m_new = jnp.maximum(m, jnp.max(s, axis=-1, keepdims=True))
        p = jnp.exp(s - m_new).astype(BF16)
        acc = jnp.exp(m - m_new) * acc + _bdot(p, v)
        return m_new, acc

    state = (jnp.full((tq, 1), NEG_BIG, F32), jnp.zeros((tq, V_PAD), F32))
    for c in range(kx_ref.shape[0] // ATTN_KEY_CHUNK):
        rows = slice(c * ATTN_KEY_CHUNK, (c + 1) * ATTN_KEY_CHUNK)
        state = chunk(kx_ref[rows, :], vx_ref[rows, :], state)
    _, acc = chunk(kc_ref[...], vc_ref[...], state)
    o_ref[...] = acc[:, :V_HEAD_DIM] / acc[:, V_HEAD_DIM:V_HEAD_DIM + 1]


def _attention(q, kx, kc, vx, vc, *, tq):
    b, h, s, _ = q.shape
    lc = kc.shape[2]
    return pl.pallas_call(
        _attn_kernel,
        grid=(b, h, s // tq),
        in_specs=[pl.BlockSpec((None, None, tq, HEAD_PAD), lambda bi, hi, qi: (bi, hi, qi, 0)),
                  pl.BlockSpec((None, None, s, HEAD_PAD), lambda bi, hi, qi: (bi, hi, 0, 0)),
                  pl.BlockSpec((None, None, lc, HEAD_PAD), lambda bi, hi, qi: (bi, hi, 0, 0)),
                  pl.BlockSpec((None, None, s, V_PAD), lambda bi, hi, qi: (bi, hi, 0, 0)),
                  pl.BlockSpec((None, None, lc, V_PAD), lambda bi, hi, qi: (bi, hi, 0, 0))],
        out_specs=pl.BlockSpec((None, tq, V_HEAD_DIM), lambda bi, hi, qi: (bi, qi, hi)),
        out_shape=jax.ShapeDtypeStruct((b, s, h * V_HEAD_DIM), F32),
        compiler_params=_cparams(("arbitrary", "arbitrary", "arbitrary")),
        name="attention",
    )(q, kx, kc, vx, vc)


def _seq_dft_kernel(ure_ref, uim_ref, r_ref, e_ref, t2_ref, o_ref, a_ref):
    s = pl.program_id(2)
    sub = SUBLANES
    n = 2 * GRID_W * sub
    cols = ure_ref.shape[-1]

    @pl.when(s < DFT_BLOCKS)
    def _():
        t = _bdot(r_ref[...].astype(BF16), e_ref[...])
        row = lax.broadcasted_iota(jnp.int32, (n, n), 0)
        col = lax.broadcasted_iota(jnp.int32, (n, n), 1)
        t = jnp.where((row % sub) == (col % sub), t, 0.0).astype(BF16)
        rhs = jnp.concatenate([ure_ref[...].reshape(GRID_W * sub, cols), uim_ref[...].reshape(GRID_W * sub, cols)],
                              axis=0).astype(BF16)
        a = _bdot(t, rhs)
        a_ref[:, :, pl.ds(pl.multiple_of(s * sub, sub), sub), :] = a.reshape(2, GRID_W, sub, cols)

    @pl.when(s >= DFT_BLOCKS)
    def _():
        k0 = pl.multiple_of((s - DFT_BLOCKS) * sub, sub)
        rhs = a_ref[:, pl.ds(k0, sub), :, :].reshape(2 * sub * GRID_W, cols).astype(BF16)
        y = _bdot(t2_ref[...].astype(BF16), rhs)
        o_ref[...] = y.reshape(GRID_W, sub, cols)


def _seq_dft_tables(n_seq):
    w, sub, nb = GRID_W, SUBLANES, DFT_BLOCKS
    ch = np.arange(nb).reshape(nb, 1, 1, 1)
    kb = np.arange(w).reshape(1, w, 1, 1)
    j = np.arange(sub).reshape(1, 1, sub, 1)
    r = np.arange(w).reshape(1, 1, 1, w)
    ang = (2.0 * np.pi / n_seq) * ((kb * (w * r + sub * ch + j)) % n_seq)
    c, s = np.cos(ang), np.sin(ang)
    rot = np.stack([np.stack([c, s], axis=3), np.stack([-s, c], axis=3)], axis=1)
    r1 = rot.reshape(nb, 2 * w * sub, 2 * w).astype(np.float32)
    expand = (np.arange(2 * w * sub)[None, :] // sub == np.arange(2 * w)[:, None]).astype(np.float32)
    ka = np.arange(w).reshape(w, 1)
    cp = np.arange(w).reshape(1, w)
    ang2 = (2.0 * np.pi / w) * ((ka * cp) % w)
    norm = 1.0 / math.sqrt(n_seq * FOURIER_GROUP_DIM)
    cs = np.stack([np.cos(ang2), np.sin(ang2)], axis=1) * norm
    eye = np.eye(sub)
    t2 = (cs[:, None, :, None, :] * eye[None, :, None, :, None]).reshape(w * sub, 2 * sub * w).astype(np.float32)
    return jnp.asarray(r1), jnp.asarray(expand).astype(BF16), jnp.asarray(t2)


def _seq_dft(u, batch, n_seq):
    assert n_seq == GRID_W * GRID_W
    w, sub, nb = GRID_W, SUBLANES, DFT_BLOCKS
    r1, expand, t2 = _seq_dft_tables(n_seq)
    halves = 2
    cols = FOURIER_DIM // halves
    u5 = u.reshape(batch, w, nb, sub, 2 * FOURIER_DIM)

    def u_spec(part):
        return pl.BlockSpec((None, w, None, sub, cols),
                            lambda b, h, s: (b, 0, jnp.minimum(s, nb - 1), 0, part * halves + h))

    y = pl.pallas_call(
        _seq_dft_kernel,
        grid=(batch, halves, 2 * nb),
        in_specs=[u_spec(0), u_spec(1),
                  pl.BlockSpec((None, 2 * w * sub, 2 * w), lambda b, h, s: (jnp.minimum(s, nb - 1), 0, 0)),
                  pl.BlockSpec((2 * w, 2 * w * sub), lambda b, h, s: (0, 0)),
                  pl.BlockSpec((w * sub, 2 * sub * w), lambda b, h, s: (0, 0))],
        out_specs=pl.BlockSpec((None, w, None, sub, cols), lambda b, h, s: (b, 0, jnp.maximum(s - nb, 0), 0, h)),
        out_shape=jax.ShapeDtypeStruct((batch, w, nb, sub, FOURIER_DIM), F32),
        scratch_shapes=[pltpu.VMEM((2, w, w, cols), F32)],
        compiler_params=_cparams(("arbitrary", "arbitrary", "arbitrary")),
        name="seq_dft",
    )(u5, u5, r1, expand, t2)
    return y.reshape(batch * n_seq, FOURIER_DIM)


def _out_router_kernel(x_ref, four_ref, attn_ref, gt1_ref, sh2_ref, sc2_ref, gf_ref, ga_ref, g2_ref,
                       wo_ref, wrh_ref, wrl_ref, br_ref,
                       xnew_ref, hm_ref, ri_ref, rw_ref, cnt_ref, carry_ref, *, mod_row):
    i = pl.program_id(0)
    tm = x_ref.shape[0]
    row = pl.ds(mod_row(i), 1)

    @pl.when(i == 0)
    def _():
        carry_ref[...] = jnp.zeros_like(carry_ref)

    def norm(v, g):
        return (v * lax.rsqrt(jnp.mean(v * v, axis=-1, keepdims=True) + EPS) * g).astype(BF16)

    mix = (_bdot(norm(four_ref[...], gf_ref[...]), wo_ref[0:FOURIER_DIM, :])
           + _bdot(norm(attn_ref[...], ga_ref[...]), wo_ref[FOURIER_DIM:FOURIER_DIM + MLA_DIM, :]))
    xn = x_ref[...] + gt1_ref[row, :] * mix
    xnew_ref[...] = xn
    ms = jnp.mean(xn * xn, axis=-1, keepdims=True)
    hm = xn * lax.rsqrt(ms + EPS) * (g2_ref[...] * (1.0 + sc2_ref[row, :])) + sh2_ref[row, :]
    hm_ref[...] = _pack_halves(hm)

    hm_hi = hm.astype(BF16)
    hm_lo = (hm - hm_hi.astype(F32)).astype(BF16)
    logits = _bdot(hm_hi, wrh_ref[...]) + _bdot(hm_lo, wrh_ref[...]) + _bdot(hm_hi, wrl_ref[...]) + br_ref[...]
    lane = lax.broadcasted_iota(jnp.int32, (tm, LANES), 1)
    lanef = lane.astype(F32)
    far = 1e9

    lg = jnp.where(lane < N_GROUPS, logits, NEG_BIG)
    m1 = jnp.max(lg, axis=-1, keepdims=True)
    g_p = 1.0 / jnp.sum(jnp.exp(lg - m1), axis=-1, keepdims=True)
    gidx = jnp.min(jnp.where(lg >= m1, lanef, far), axis=-1, keepdims=True)
    lo = N_GROUPS + EXPERTS_PER_GROUP * gidx
    in_group = jnp.where(lanef >= lo, jnp.where(lanef < lo + EXPERTS_PER_GROUP, 1.0, 0.0), 0.0) > 0.5
    le = jnp.where(in_group, logits, NEG_BIG)
    m2 = jnp.max(le, axis=-1, keepdims=True)
    idx1 = jnp.min(jnp.where(le >= m2, lanef, far), axis=-1, keepdims=True)
    le2 = jnp.where(lanef == idx1, NEG_BIG, le)
    m3 = jnp.max(le2, axis=-1, keepdims=True)
    idx2 = jnp.min(jnp.where(le2 >= m3, lanef, far), axis=-1, keepdims=True)
    t = jnp.exp(m3 - m2)
    p1 = 1.0 / (1.0 + t)
    p2 = t / (1.0 + t)
    e1 = idx1 - N_GROUPS
    e2 = idx2 - N_GROUPS

    oh1 = jnp.where(lanef == e1, 1.0, 0.0)
    oh2 = jnp.where(lanef == e2, 1.0, 0.0)
    ohs = oh1 + oh2
    row = lax.broadcasted_iota(jnp.int32, (tm, tm), 0)
    col = lax.broadcasted_iota(jnp.int32, (tm, tm), 1)
    tri = jnp.where(row > col, 1.0, 0.0).astype(BF16)
    before = _bdot(tri, ohs.astype(BF16)) + carry_ref[...]
    rank1 = jnp.sum(oh1 * before, axis=-1, keepdims=True)
    rank2 = jnp.sum(oh2 * before, axis=-1, keepdims=True)
    carry = carry_ref[...] + jnp.sum(ohs, axis=0, keepdims=True)
    carry_ref[...] = carry
    cnt_ref[...] = jnp.broadcast_to(carry, cnt_ref.shape)

    ri = jnp.where(lane == 0, e1, jnp.where(lane == 1, e2, jnp.where(lane == 2, rank1, jnp.where(lane == 3, rank2, 0.0))))
    ri_ref[...] = ri.astype(jnp.int32)
    rw_ref[...] = jnp.where(lane == 0, g_p * p1, jnp.where(lane == 1, g_p * p2, 0.0))


def _out_router(x2, four, attn, mods, gf, ga, g2, wo, wrh, wrl, br, *, seq, tm):
    t, d = x2.shape
    nt = t // tm
    tiles_per_b = seq // tm

    def rows(n):
        return pl.BlockSpec((tm, n), lambda i: (i, 0))

    def const(arr):
        return pl.BlockSpec(arr.shape, lambda i: (0,) * arr.ndim, pipeline_mode=pl.Buffered(1))

    return pl.pallas_call(
        functools.partial(_out_router_kernel, mod_row=lambda i: i // tiles_per_b),
        grid=(nt,),
        in_specs=[rows(d), rows(FOURIER_DIM), rows(MLA_DIM), _mod_spec(mods, 2, d), _mod_spec(mods, 3, d),
                  _mod_spec(mods, 4, d),
                  const(gf), const(ga), const(g2), const(wo), const(wrh), const(wrl), const(br)],
        out_specs=[rows(d), rows(d // 2), rows(LANES), rows(LANES), pl.BlockSpec((8, LANES), lambda i: (0, 0))],
        out_shape=[jax.ShapeDtypeStruct((t, d), F32), jax.ShapeDtypeStruct((t, d // 2), jnp.uint32),
                   jax.ShapeDtypeStruct((t, LANES), jnp.int32), jax.ShapeDtypeStruct((t, LANES), F32),
                   jax.ShapeDtypeStruct((8, LANES), F32)],
        scratch_shapes=[pltpu.VMEM((1, LANES), F32)],
        compiler_params=_cparams(("arbitrary",)),
        name="out_proj_router",
    )(x2, four, attn, mods, mods, mods, gf, ga, g2, wo, wrh, wrl, br)


def _dispatch_kernel(dest, seg_last, seg_blocks, used_blocks, hm_ref, xs_hbm, zbuf, sem, zsem):
    i = pl.program_id(0)
    tm = hm_ref.shape[0]

    @pl.when(i == 0)
    def _():
        zbuf[...] = jnp.zeros(zbuf.shape, zbuf.dtype)
        n_blocks = xs_hbm.shape[0] // MOE_BLOCK

        def zero_copy(blk):
            r0 = pl.multiple_of(blk * MOE_BLOCK, MOE_BLOCK)
            return pltpu.make_async_copy(zbuf, xs_hbm.at[pl.ds(r0, MOE_BLOCK)], zsem.at[0])

        def seg_start(e, carry):
            @pl.when(seg_blocks[e] > 0)
            def _():
                zero_copy(seg_last[e]).start()
            return carry

        def seg_wait(e, carry):
            @pl.when(seg_blocks[e] > 0)
            def _():
                zero_copy(seg_last[e]).wait()
            return carry

        def tail_start(blk, carry):
            zero_copy(blk).start()
            return carry

        def tail_wait(blk, carry):
            zero_copy(blk).wait()
            return carry

        lax.fori_loop(0, N_EXPERTS, seg_start, 0)
        lax.fori_loop(used_blocks[0], n_blocks, tail_start, 0)
        lax.fori_loop(0, N_EXPERTS, seg_wait, 0)
        lax.fori_loop(used_blocks[0], n_blocks, tail_wait, 0)

    def body(r8, carry):
        for u in range(GATHER_UNROLL):
            r = r8 * GATHER_UNROLL + u
            for k in range(2):
                pltpu.make_async_copy(hm_ref.at[pl.ds(r, 1)], xs_hbm.at[pl.ds(dest[2 * (i * tm + r) + k], 1)],
                                      sem.at[0]).start(priority=k)
        return carry

    lax.fori_loop(0, tm // GATHER_UNROLL, body, 0)
    for k in range(2):
        pltpu.make_async_copy(hm_ref, xs_hbm.at[pl.ds(0, tm)], sem.at[0]).wait()


def _dispatch(dest, seg_last, seg_blocks, used_blocks, hm, *, n_rows, tm):
    t, w = hm.shape
    grid_spec = pltpu.PrefetchScalarGridSpec(
        num_scalar_prefetch=4,
        grid=(t // tm,),
        in_specs=[pl.BlockSpec((tm, w), lambda i, *_: (i, 0))],
        out_specs=pl.BlockSpec(memory_space=pl.ANY),
        scratch_shapes=[pltpu.VMEM((MOE_BLOCK, w), hm.dtype), pltpu.SemaphoreType.DMA((1,)),
                        pltpu.SemaphoreType.DMA((1,))],
    )
    return pl.pallas_call(
        _dispatch_kernel,
        grid_spec=grid_spec,
        out_shape=jax.ShapeDtypeStruct((n_rows, w), hm.dtype),
        compiler_params=_cparams(("arbitrary",)),
        name="moe_dispatch",
    )(dest, seg_last, seg_blocks, used_blocks, hm)


def _moe_kernel(item_e, item_blk0, item_nblk, used_blocks,
                xs_hbm, wg_hbm, wu_hbm, wd_hbm, y_hbm,
                xg, xb, gs, ab, yp, gu_buf, dn_buf, gsem, osem, gusem, dnsem):
    i = pl.program_id(0)
    j = pl.program_id(1)
    n_items = pl.num_programs(0)
    nj = pl.num_programs(1)
    slot = i % 2
    nblk = item_nblk[i]

    def weight_copy(it, ph):
        if ph == 2:
            ws = it % DN_SLOTS
            return pltpu.make_async_copy(wd_hbm.at[item_e[it]], dn_buf.at[ws], dnsem.at[ws])
        ws = (2 * it + ph) % GU_SLOTS
        return pltpu.make_async_copy((wg_hbm, wu_hbm)[ph].at[item_e[it]], gu_buf.at[ws], gusem.at[ws])

    def start_weight(it, ph):
        it_c = jnp.minimum(it, n_items - 1)

        @pl.when(jnp.logical_and(it < n_items, item_nblk[it_c] > 0))
        def _():
            weight_copy(it_c, ph).start(priority=WEIGHT_DMA_PRIORITY)

    def x_copy(it, sl, m):
        r0 = pl.multiple_of(item_blk0[it] * MOE_BLOCK, MOE_BLOCK)
        return pltpu.make_async_copy(xs_hbm.at[pl.ds(r0, m)], xg.at[sl, pl.ds(0, m)], gsem.at[sl])

    def issue_gather(it, sl):
        for nb in range(1, ITEM_BLOCKS + 1):
            @pl.when(item_nblk[it] == nb)
            def _():
                x_copy(it, sl, nb * MOE_BLOCK).start(priority=ROW_DMA_PRIORITY)

    def wait_gather(it, sl):
        for nb in range(1, ITEM_BLOCKS + 1):
            @pl.when(item_nblk[it] == nb)
            def _():
                x_copy(it, sl, nb * MOE_BLOCK).wait()

    def out_copy(it, m):
        r0 = pl.multiple_of(item_blk0[it] * MOE_BLOCK, MOE_BLOCK)
        return pltpu.make_async_copy(yp.at[pl.ds(0, m)], y_hbm.at[pl.ds(r0, m)], osem.at[0])

    def wait_out(it):
        for nb in range(1, ITEM_BLOCKS + 1):
            @pl.when(item_nblk[it] == nb)
            def _():
                out_copy(it, nb * MOE_BLOCK).wait()

    @pl.when(j == 0)
    def _():
        @pl.when(i == 0)
        def _():
            start_weight(0, 0)
            start_weight(0, 1)
            start_weight(1, 0)
            start_weight(0, 2)
            issue_gather(0, 0)

        start_weight(i + 1, 2)
        wait_gather(i, slot)

        @pl.when(i + 1 < n_items)
        def _():
            issue_gather(i + 1, 1 - slot)

    @pl.when(j == 1)
    def _():
        start_weight(i + 1, 1)

    @pl.when(j == 2)
    def _():
        start_weight(i + 2, 0)

        @pl.when(i > 0)
        def _():
            wait_out(i - 1)

    for ph in range(3):
        @pl.when(jnp.logical_and(j == ph, nblk > 0))
        def _():
            weight_copy(i, ph).wait()

    for nb in range(1, ITEM_BLOCKS + 1):
        m = nb * MOE_BLOCK

        @pl.when(jnp.logical_and(nblk == nb, j == 0))
        def _():
            hi, lo = _unpack_halves(xg[slot, 0:m, :])
            half = hi.shape[1]
            xb[0:m, 0:half] = hi.astype(BF16)
            xb[0:m, half:2 * half] = lo.astype(BF16)
            gs[0:m, :] = _bdot(xb[0:m, :], gu_buf[(2 * i) % GU_SLOTS].astype(BF16))

        @pl.when(jnp.logical_and(nblk == nb, j == 1))
        def _():
            g = gs[0:m, :]
            u = _bdot(xb[0:m, :], gu_buf[(2 * i + 1) % GU_SLOTS].astype(BF16))
            ab[0:m, :] = (g * jax.nn.sigmoid(g) * u).astype(BF16)

        @pl.when(jnp.logical_and(nblk == nb, j == 2))
        def _():
            yp[0:m, :] = _pack_halves(_bdot(ab[0:m, :], dn_buf[i % DN_SLOTS].astype(BF16)))
            out_copy(i, m).start()

    @pl.when(jnp.logical_and(i == n_items - 1, j == nj - 1))
    def _():
        wait_out(i)
        n_blocks = y_hbm.shape[0] // MOE_BLOCK
        yp[0:MOE_BLOCK, :] = jnp.zeros((MOE_BLOCK, yp.shape[1]), jnp.uint32)

        def tail_copy(blk):
            r0 = pl.multiple_of(blk * MOE_BLOCK, MOE_BLOCK)
            return pltpu.make_async_copy(yp.at[pl.ds(0, MOE_BLOCK)], y_hbm.at[pl.ds(r0, MOE_BLOCK)], osem.at[0])

        def start_body(blk, carry):
            tail_copy(blk).start()
            return carry

        def wait_body(blk, carry):
            tail_copy(blk).wait()
            return carry

        lax.fori_loop(used_blocks[0], n_blocks, start_body, 0)
        lax.fori_loop(used_blocks[0], n_blocks, wait_body, 0)


def _moe(xs, w_gate, w_up, w_down, item_e, item_blk0, item_nblk, used_blocks, *, n_rows):
    d, de = w_gate.shape[1], w_gate.shape[2]
    n_items = item_e.shape[0]
    nj = 3
    rows = ITEM_BLOCKS * MOE_BLOCK
    any_spec = pl.BlockSpec(memory_space=pl.ANY)
    grid_spec = pltpu.PrefetchScalarGridSpec(
        num_scalar_prefetch=4,
        grid=(n_items, nj),
        in_specs=[any_spec, any_spec, any_spec, any_spec],
        out_specs=any_spec,
        scratch_shapes=[pltpu.VMEM((2, rows, d // 2), jnp.uint32),
                        pltpu.VMEM((rows, d), BF16),
                        pltpu.VMEM((rows, de), F32),
                        pltpu.VMEM((rows, de), BF16),
                        pltpu.VMEM((rows, d // 2), jnp.uint32),
                        pltpu.VMEM((GU_SLOTS, d, de), F32),
                        pltpu.VMEM((DN_SLOTS, de, d), F32),
                        pltpu.SemaphoreType.DMA((2,)),
                        pltpu.SemaphoreType.DMA((1,)),
                        pltpu.SemaphoreType.DMA((GU_SLOTS,)),
                        pltpu.SemaphoreType.DMA((DN_SLOTS,))],
    )
    return pl.pallas_call(
        _moe_kernel,
        grid_spec=grid_spec,
        out_shape=jax.ShapeDtypeStruct((n_rows, d // 2), jnp.uint32),
        compiler_params=_cparams(("arbitrary", "arbitrary")),
        name="moe_experts",
    )(item_e, item_blk0, item_nblk, used_blocks, xs, w_gate, w_up, w_down)


def _combine_kernel(dest, x_ref, gt2_ref, rw_ref, y_hbm, o_ref, ybuf, sem, *, mod_row):
    i = pl.program_id(0)
    n = pl.num_programs(0)
    tm = x_ref.shape[0]
    slot = i % 2

    def issue(it, sl):
        base = it * tm

        def body(r4, carry):
            for rr in range(GATHER_UNROLL // 2):
                r = r4 * (GATHER_UNROLL // 2) + rr
                for k in range(2):
                    pltpu.make_async_copy(y_hbm.at[pl.ds(dest[2 * (base + r) + k], 1)], ybuf.at[sl, k, pl.ds(r, 1)],
                                          sem.at[sl]).start(priority=k)
            return carry

        lax.fori_loop(0, tm // (GATHER_UNROLL // 2), body, 0)

    @pl.when(i == 0)
    def _():
        issue(0, 0)

    for k in range(2):
        pltpu.make_async_copy(y_hbm.at[pl.ds(0, tm)], ybuf.at[slot, k], sem.at[slot]).wait()

    @pl.when(i + 1 < n)
    def _():
        issue(i + 1, 1 - slot)

    w = rw_ref[...]
    gate = gt2_ref[pl.ds(mod_row(i), 1), :]
    hi0, lo0 = _unpack_halves(ybuf[slot, 0])
    hi1, lo1 = _unpack_halves(ybuf[slot, 1])
    half = hi0.shape[1]
    o_ref[:, 0:half] = x_ref[:, 0:half] + gate[:, 0:half] * (w[:, 0:1] * hi0 + w[:, 1:2] * hi1)
    o_ref[:, half:2 * half] = (x_ref[:, half:2 * half]
                               + gate[:, half:2 * half] * (w[:, 0:1] * lo0 + w[:, 1:2] * lo1))


def _combine(dest, xnew, mods, rw, y, *, seq, tm):
    t, d = xnew.shape
    tiles_per_b = seq // tm
    grid_spec = pltpu.PrefetchScalarGridSpec(
        num_scalar_prefetch=1,
        grid=(t // tm,),
        in_specs=[pl.BlockSpec((tm, d), lambda i, ds: (i, 0)),
                  _mod_spec(mods, 5, d),
                  pl.BlockSpec((tm, LANES), lambda i, ds: (i, 0)),
                  pl.BlockSpec(memory_space=pl.ANY)],
        out_specs=pl.BlockSpec((tm, d), lambda i, ds: (i, 0)),
        scratch_shapes=[pltpu.VMEM((2, 2, tm, d // 2), jnp.uint32), pltpu.SemaphoreType.DMA((2,))],
    )
    return pl.pallas_call(
        functools.partial(_combine_kernel, mod_row=lambda i: i // tiles_per_b),
        grid_spec=grid_spec,
        out_shape=jax.ShapeDtypeStruct((t, d), F32),
        compiler_params=_cparams(("arbitrary",)),
        name="moe_combine",
    )(dest, xnew, mods, rw, y)


def _rope_tables(n_tokens):
    rows = n_tokens // GRID_W
    row = jnp.repeat(jnp.arange(rows, dtype=jnp.int32), GRID_W).astype(F32)
    col = jnp.tile(jnp.arange(GRID_W, dtype=jnp.int32), rows).astype(F32)
    n_freq = QK_ROPE_DIM // 4
    inv = ROPE_THETA ** (-jnp.arange(n_freq, dtype=F32) / n_freq)
    ar = row[:, None] * inv[None, :]
    ac = col[:, None] * inv[None, :]
    cr, sr, cc, sc = jnp.cos(ar), jnp.sin(ar), jnp.cos(ac), jnp.sin(ac)
    cos64 = jnp.concatenate([cr, cr, cc, cc], axis=-1)
    sin64 = jnp.concatenate([-sr, sr, -sc, sc], axis=-1)
    return jnp.tile(cos64, (1, 2)), jnp.tile(sin64, (1, 2))


def _channel_dft_table():
    c = np.arange(FOURIER_GROUP_DIM).reshape(-1, 1)
    k = np.arange(FOURIER_GROUP_DIM).reshape(1, -1)
    ang = (2.0 * np.pi / FOURIER_GROUP_DIM) * ((c * k) % FOURIER_GROUP_DIM)
    return jnp.asarray(np.concatenate([np.cos(ang), -np.sin(ang)], axis=1).astype(np.float32)).astype(BF16)


def _split_heads(w, widths):
    k = w.shape[0]
    wh = w.reshape(k, MLA_HEADS, sum(widths))
    outs, off = [], 0
    for wd in widths:
        outs.append(wh[:, :, off:off + wd].reshape(k, MLA_HEADS * wd))
        off += wd
    return outs


def kernel(x, c, ctx, c_ctx, w_ada, b_ada, g_norm1, g_norm2, w_in, g_q_a, g_kv_a, w_uq, w_ukv, g_qk_q, g_qk_k,
           g_out_four, g_out_attn, w_out, w_router_group, b_router_group, w_router_expert, b_router_expert,
           w_gate, w_up, w_down):
    b, s, d = x.shape
    lc = ctx.shape[1]
    t = b * s
    layer_params = (w_ada, b_ada, g_norm1, g_norm2, w_in, g_q_a, g_kv_a, w_uq, w_ukv, g_qk_q, g_qk_k, g_out_four,
                    g_out_attn, w_out, w_router_group, b_router_group, w_router_expert, b_router_expert,
                    w_gate, w_up, w_down)
    assert all(p.shape[0] == 1 for p in layer_params), "single-layer block"
    (w_ada, b_ada, g_norm1, g_norm2, w_in, g_q_a, g_kv_a, w_uq, w_ukv, g_qk_q, g_qk_k, g_out_four,
     g_out_attn, w_out, w_router_group, b_router_group, w_router_expert, b_router_expert,
     w_gate, w_up, w_down) = [p.reshape(p.shape[1:]) for p in layer_params]

    cond8 = jnp.concatenate([c, c_ctx[None, :], jnp.zeros((8 - b - 1, d), F32)], axis=0)
    mods = _ada_mod(cond8, w_ada, b_ada)

    w_all = jnp.concatenate([w_in, w_in[:, ROPE_OFF:]], axis=1).astype(BF16)
    dc = _channel_dft_table()
    wqn, wqr = [w.astype(BF16) for w in _split_heads(w_uq, (QK_NOPE_DIM, QK_ROPE_DIM))]
    wkn, wv = [w.astype(BF16) for w in _split_heads(w_ukv, (QK_NOPE_DIM, V_HEAD_DIM))]
    gqn = g_qk_q[:QK_NOPE_DIM].reshape(1, -1)
    gqr2 = jnp.tile(g_qk_q[QK_NOPE_DIM:], 2).reshape(1, -1)
    gkn = g_qk_k[:QK_NOPE_DIM].reshape(1, -1)
    gkr2 = jnp.tile(g_qk_k[QK_NOPE_DIM:], 2).reshape(1, -1)
    g1 = g_norm1.reshape(1, d)
    gq = g_q_a.reshape(1, -1)
    gkv = g_kv_a.reshape(1, -1)

    x2 = x.reshape(t, d)
    u, cq, ckv, kr2 = _in_proj(x2, mods, 0, s, g1, gq, gkv, w_all, dc, with_q=True, tm=IN_PROJ_ROWS)
    ckv_c, kr2_c = _in_proj(ctx.reshape(b * lc, d), mods, b, b * lc, g1, gq, gkv, w_all, dc, with_q=False, tm=lc)

    cos_t, sin_t = _rope_tables(s)
    q, kx, vx = _qkv(cq, ckv, kr2, cos_t, sin_t, wqn, wqr, gqn, gqr2, wkn, wv, gkn, gkr2,
                     batch=b, seq=s, tm=QKV_ROWS, with_q=True, with_rope=True)
    kc, vc = _qkv(None, ckv_c, kr2_c, None, None, None, None, None, None, wkn, wv, gkn, gkr2,
                  batch=b, seq=lc, tm=lc, with_q=False, with_rope=False)

    attn = _attention(q, kx, kc, vx, vc, tq=ATTN_Q_ROWS).reshape(t, MLA_DIM)
    four = _seq_dft(u, b, s)

    wo = w_out.astype(BF16)
    n_route = N_GROUPS + N_EXPERTS
    wrt = jnp.concatenate([w_router_group, w_router_expert, jnp.zeros((d, LANES - n_route), F32)], axis=1)
    wrt_hi = wrt.astype(BF16)
    wrt_lo = (wrt - wrt_hi.astype(F32)).astype(BF16)
    brt = jnp.concatenate([b_router_group, b_router_expert, jnp.zeros((LANES - n_route,), F32)]).reshape(1, -1)
    xnew, hm, ri, rw, cnt = _out_router(x2, four, attn, mods, g_out_four.reshape(1, -1), g_out_attn.reshape(1, -1),
                                        g_norm2.reshape(1, d), wo, wrt_hi, wrt_lo, brt, seq=s, tm=OUT_ROUTER_ROWS)

    counts = cnt[0, :N_EXPERTS].astype(jnp.int32)
    nblk_e = (counts + MOE_BLOCK - 1) // MOE_BLOCK
    blk_end = jnp.cumsum(nblk_e)
    blk_start = blk_end - nblk_e
    n_slots = t * 2
    n_blocks = -(-(n_slots + N_EXPERTS * (MOE_BLOCK - 1)) // MOE_BLOCK)
    n_rows = n_blocks * MOE_BLOCK
    e12 = ri[:, 0:2]
    seg_start = jnp.sum(jnp.where(e12[:, :, None] == jnp.arange(N_EXPERTS, dtype=jnp.int32), blk_start, 0), axis=-1)
    dest = (seg_start * MOE_BLOCK + ri[:, 2:4]).reshape(-1).astype(jnp.int32)
    items_e = (nblk_e + ITEM_BLOCKS - 1) // ITEM_BLOCKS
    item_end = jnp.cumsum(items_e)
    n_items = (n_blocks + (ITEM_BLOCKS - 1) * N_EXPERTS) // ITEM_BLOCKS
    idx = jnp.arange(n_items, dtype=jnp.int32)
    total = item_end[-1]
    idx_c = jnp.minimum(idx, total - 1)
    ie = jnp.minimum(jnp.sum(idx_c[:, None] >= item_end[None, :], axis=1), N_EXPERTS - 1).astype(jnp.int32)
    local = idx_c - (item_end - items_e)[ie]
    item_blk0 = (blk_start[ie] + ITEM_BLOCKS * local).astype(jnp.int32)
    item_nblk = jnp.where(idx < total, jnp.clip(nblk_e[ie] - ITEM_BLOCKS * local, 0, ITEM_BLOCKS), 0).astype(jnp.int32)

    used_blocks = blk_end[-1:].astype(jnp.int32)
    seg_last = jnp.maximum(blk_end - 1, 0).astype(jnp.int32)
    xs = _dispatch(dest, seg_last, nblk_e.astype(jnp.int32), used_blocks, hm, n_rows=n_rows, tm=DISPATCH_ROWS)
    y = _moe(xs, w_gate, w_up, w_down, ie, item_blk0, item_nblk, used_blocks, n_rows=n_rows)
    out = _combine(dest, xnew, mods, rw, y, seq=s, tm=COMBINE_ROWS)
    return out.reshape(b, s, d)
```

```python
import functools
import math

import numpy as np
import jax
import jax.numpy as jnp
from jax import lax
from jax.experimental import pallas as pl
from jax.experimental.pallas import tpu as pltpu

F32 = jnp.float32
BF16 = jnp.bfloat16

D_MODEL = 2048
GRID_W = 64
EPS = 1e-6
N_MOD = 6
N_FOURIER_GROUPS = 4
FOURIER_GROUP_DIM = 256
FOURIER_DIM = 1024
MLA_HEADS = 8
QK_NOPE_DIM = 128
QK_ROPE_DIM = 64
QK_HEAD_DIM = 192
V_HEAD_DIM = 128
Q_LORA_RANK = 768
KV_LORA_RANK = 512
MLA_DIM = 1024
ROPE_THETA = 10000.0
Q_OFF = FOURIER_DIM
KV_OFF = Q_OFF + Q_LORA_RANK
ROPE_OFF = KV_OFF + KV_LORA_RANK
N_GROUPS = 8
EXPERTS_PER_GROUP = 8
N_EXPERTS = 64
D_EXPERT = 768

ROT_HALF = QK_ROPE_DIM // 4
ROT_BLOCK = 2 * ROT_HALF

ADA_COLS = 1024
IN_PROJ_ROWS = 512
QKV_ROWS = 512
ATTN_Q_ROWS = 1024
OUT_ROUTER_ROWS = 512
COMBINE_ROWS = 256
DISPATCH_ROWS = 1024

LANES = 128
SUBLANES = 8
DFT_BLOCKS = GRID_W // SUBLANES
HEAD_PAD = 256
V_PAD = 256
MOE_BLOCK = 64
ITEM_BLOCKS = 8
GATHER_UNROLL = 8
GU_SLOTS = 3
DN_SLOTS = 2
ATTN_KEY_CHUNK = 512
ATTN_BOUND_MARGIN = 1.02
ATTN_MIN_ROW_SUM = 1e-30
LOG2_E = math.log2(math.e)
ROW_DMA_PRIORITY = 0
WEIGHT_DMA_PRIORITY = 1
VMEM_LIMIT = 56 * 1024 * 1024
NEG_BIG = -1e30


def _cparams(sem):
    return pltpu.CompilerParams(dimension_semantics=sem, vmem_limit_bytes=VMEM_LIMIT)


def _bdot(a, b):
    return jnp.dot(a, b, preferred_element_type=F32)


def _pack_halves(v):
    n = v.shape[1] // 2
    hi = pltpu.bitcast(v[:, :n].astype(BF16).astype(F32), jnp.uint32)
    lo = pltpu.bitcast(v[:, n:].astype(BF16).astype(F32), jnp.uint32)
    return hi | (lo >> 16)


def _unpack_halves(u):
    hi = pltpu.bitcast(u & jnp.uint32(0xFFFF0000), F32)
    lo = pltpu.bitcast(u << 16, F32)
    return hi, lo


def _ada_kernel(c_ref, w_ref, b_ref, o_ref):
    c = c_ref[...]
    s = (c * jax.nn.sigmoid(c)).astype(BF16)
    o_ref[...] = _bdot(s, w_ref[...].astype(BF16)) + b_ref[...]


def _ada_mod(cond8, w_ada, b_ada):
    d, n = w_ada.shape
    tn = ADA_COLS
    return pl.pallas_call(
        _ada_kernel,
        grid=(n // tn,),
        in_specs=[pl.BlockSpec((8, d), lambda i: (0, 0)),
                  pl.BlockSpec((d, tn), lambda i: (0, i)),
                  pl.BlockSpec((1, tn), lambda i: (0, i))],
        out_specs=pl.BlockSpec((8, tn), lambda i: (0, i)),
        out_shape=jax.ShapeDtypeStruct((8, n), F32),
        compiler_params=_cparams(("arbitrary",)),
        name="ada_mod",
    )(cond8, w_ada, b_ada.reshape(1, n))


def _in_proj_kernel(x_ref, sh_ref, sc_ref, g_ref, gq_ref, gkv_ref, w_ref, dc_ref, *out_refs, with_q, mod_row):
    row = pl.ds(mod_row(pl.program_id(0)), 1)
    x = x_ref[...]
    ms = jnp.mean(x * x, axis=-1, keepdims=True)
    a = g_ref[...] * (1.0 + sc_ref[row, :])
    h = (x * lax.rsqrt(ms + EPS) * a + sh_ref[row, :]).astype(BF16)
    if with_q:
        u_ref, cq_ref, ckv_ref, kr_ref = out_refs
        f = _bdot(h, w_ref[:, 0:Q_OFF]).astype(BF16)
        dc = dc_ref[...]
        for g in range(N_FOURIER_GROUPS):
            lo = g * FOURIER_GROUP_DIM
            ug = _bdot(f[:, lo:lo + FOURIER_GROUP_DIM], dc)
            u_ref[:, lo:lo + FOURIER_GROUP_DIM] = ug[:, :FOURIER_GROUP_DIM]
            u_ref[:, FOURIER_DIM + lo:FOURIER_DIM + lo + FOURIER_GROUP_DIM] = ug[:, FOURIER_GROUP_DIM:]
        pq = _bdot(h, w_ref[:, Q_OFF:KV_OFF])
        msq = jnp.mean(pq * pq, axis=-1, keepdims=True)
        cq_ref[...] = (pq * lax.rsqrt(msq + EPS) * gq_ref[...]).astype(BF16)
    else:
        ckv_ref, kr_ref = out_refs
    pkv = _bdot(h, w_ref[:, KV_OFF:ROPE_OFF])
    mskv = jnp.mean(pkv * pkv, axis=-1, keepdims=True)
    ckv_ref[...] = (pkv * lax.rsqrt(mskv + EPS) * gkv_ref[...]).astype(BF16)
    kr_ref[...] = _bdot(h, w_ref[:, ROPE_OFF:ROPE_OFF + LANES])


def _mod_spec(mods, k, d):
    return pl.BlockSpec((mods.shape[0], d), lambda i, *_: (0, k))


def _in_proj(x2, mods, first_row, rows_per_mod, g1, gq, gkv, w_all, dc, *, with_q, tm):
    t, d = x2.shape
    nt = t // tm
    tiles_per_mod = rows_per_mod // tm

    def const(shape):
        return pl.BlockSpec(shape, lambda i: (0,) * len(shape))

    in_specs = [pl.BlockSpec((tm, d), lambda i: (i, 0)), _mod_spec(mods, 0, d), _mod_spec(mods, 1, d),
                const((1, d)), const((1, Q_LORA_RANK)), const((1, KV_LORA_RANK)),
                const(w_all.shape), const(dc.shape)]

    def rows(n):
        return pl.BlockSpec((tm, n), lambda i: (i, 0))

    out_specs = [rows(KV_LORA_RANK), rows(LANES)]
    out_shape = [jax.ShapeDtypeStruct((t, KV_LORA_RANK), BF16), jax.ShapeDtypeStruct((t, LANES), F32)]
    if with_q:
        out_specs = [rows(2 * FOURIER_DIM), rows(Q_LORA_RANK)] + out_specs
        out_shape = [jax.ShapeDtypeStruct((t, 2 * FOURIER_DIM), F32),
                     jax.ShapeDtypeStruct((t, Q_LORA_RANK), BF16)] + out_shape
    return pl.pallas_call(
        functools.partial(_in_proj_kernel, with_q=with_q, mod_row=lambda i: first_row + i // tiles_per_mod),
        grid=(nt,),
        in_specs=in_specs,
        out_specs=out_specs,
        out_shape=out_shape,
        compiler_params=_cparams(("arbitrary",)),
        name="in_proj_x" if with_q else "in_proj_ctx",
    )(x2, mods, mods, g1, gq, gkv, w_all, dc)


def _swap_halves(y, first_half):
    return jnp.where(first_half, pltpu.roll(y, LANES - ROT_HALF, 1), pltpu.roll(y, ROT_HALF, 1))


def _qkv_kernel(*refs, with_q, with_rope):
    it = iter(refs)
    if with_q:
        cq_ref = next(it)
    ckv_ref = next(it)
    kr_ref = next(it)
    if with_rope:
        cos_ref = next(it)
        sin_ref = next(it)
    if with_q:
        wqn_ref = next(it)
        wqr_ref = next(it)
        gqn_ref = next(it)
        gqr_ref = next(it)
    wkn_ref = next(it)
    wv_ref = next(it)
    gkn_ref = next(it)
    gkr_ref = next(it)
    if with_q:
        q_ref = next(it)
    k_ref = next(it)
    v_ref = next(it)

    tm = ckv_ref.shape[0]
    lane = lax.broadcasted_iota(jnp.int32, (tm, LANES), 1)
    low = lane < QK_ROPE_DIM
    first_half = (lane % ROT_BLOCK) < ROT_HALF
    inv_dim = 1.0 / QK_HEAD_DIM

    def rope(y):
        if not with_rope:
            return y
        return y * cos_ref[...] + _swap_halves(y, first_half) * sin_ref[...]

    if with_q:
        cq = cq_ref[...]
        qn = _bdot(cq, wqn_ref[...])
        qr = _bdot(cq, wqr_ref[...])
        qscale = QK_HEAD_DIM ** -0.5 * LOG2_E
        for p in range(MLA_HEADS // 2):
            blk = qr[:, p * LANES:(p + 1) * LANES]
            sq = blk * blk
            ss_lo = jnp.sum(jnp.where(low, sq, 0.0), axis=-1, keepdims=True)
            ss_hi = jnp.sum(jnp.where(low, 0.0, sq), axis=-1, keepdims=True)
            scales = []
            for hh, ssr in ((2 * p, ss_lo), (2 * p + 1, ss_hi)):
                nh = qn[:, hh * LANES:(hh + 1) * LANES]
                ssq = jnp.sum(nh * nh, axis=-1, keepdims=True) + ssr
                s = lax.rsqrt(ssq * inv_dim + EPS)
                scales.append(s)
                q_ref[hh, :, 0:LANES] = (nh * s * gqn_ref[...] * qscale).astype(BF16)
            s_pair = jnp.where(low, scales[0], scales[1])
            r = rope(blk * s_pair * gqr_ref[...]) * qscale
            q_ref[2 * p, :, LANES:2 * LANES] = jnp.where(low, r, 0.0).astype(BF16)
            q_ref[2 * p + 1, :, LANES:2 * LANES] = jnp.where(low, pltpu.roll(r, QK_ROPE_DIM, 1), 0.0).astype(BF16)

    ckv = ckv_ref[...]
    kn = _bdot(ckv, wkn_ref[...])
    v = _bdot(ckv, wv_ref[...])
    kr = kr_ref[...]
    ss_r = jnp.sum(jnp.where(low, kr * kr, 0.0), axis=-1, keepdims=True)
    base = rope(kr * gkr_ref[...])
    ones_col = jnp.where(lane == 0, 1.0, 0.0).astype(BF16)
    for hh in range(MLA_HEADS):
        nh = kn[:, hh * LANES:(hh + 1) * LANES]
        ssq = jnp.sum(nh * nh, axis=-1, keepdims=True) + ss_r
        s = lax.rsqrt(ssq * inv_dim + EPS)
        k_ref[hh, :, 0:LANES] = (nh * s * gkn_ref[...]).astype(BF16)
        k_ref[hh, :, LANES:2 * LANES] = jnp.where(low, base * s, 0.0).astype(BF16)
        v_ref[hh, :, 0:LANES] = v[:, hh * LANES:(hh + 1) * LANES].astype(BF16)
        v_ref[hh, :, LANES:2 * LANES] = ones_col


def _qkv(cq, ckv, kr2, cos_t, sin_t, wqn, wqr, gqn, gqr2, wkn, wv, gkn, gkr2, *, batch, seq, tm, with_q,
         with_rope):
    t = ckv.shape[0]
    nt = t // tm
    tiles_per_b = seq // tm

    def rows(n):
        return pl.BlockSpec((tm, n), lambda i: (i, 0))

    def const(arr):
        return pl.BlockSpec(arr.shape, lambda i: (0,) * arr.ndim)

    tab_spec = pl.BlockSpec((tm, LANES), lambda i: (i % tiles_per_b, 0))

    def head_out(width):
        return pl.BlockSpec((None, MLA_HEADS, tm, width), lambda i: (i // tiles_per_b, 0, i % tiles_per_b, 0))

    args, in_specs = [], []
    if with_q:
        args.append(cq)
        in_specs.append(rows(Q_LORA_RANK))
    args += [ckv, kr2]
    in_specs += [rows(KV_LORA_RANK), rows(LANES)]
    if with_rope:
        args += [cos_t, sin_t]
        in_specs += [tab_spec, tab_spec]
    if with_q:
        args += [wqn, wqr, gqn, gqr2]
        in_specs += [const(wqn), const(wqr), const(gqn), const(gqr2)]
    args += [wkn, wv, gkn, gkr2]
    in_specs += [const(wkn), const(wv), const(gkn), const(gkr2)]

    out_specs = [head_out(HEAD_PAD), head_out(V_PAD)]
    out_shape = [jax.ShapeDtypeStruct((batch, MLA_HEADS, seq, HEAD_PAD), BF16),
                 jax.ShapeDtypeStruct((batch, MLA_HEADS, seq, V_PAD), BF16)]
    if with_q:
        out_specs = [head_out(HEAD_PAD)] + out_specs
        out_shape = [jax.ShapeDtypeStruct((batch, MLA_HEADS, seq, HEAD_PAD), BF16)] + out_shape
    return pl.pallas_call(
        functools.partial(_qkv_kernel, with_q=with_q, with_rope=with_rope),
        grid=(nt,),
        in_specs=in_specs,
        out_specs=out_specs,
        out_shape=out_shape,
        compiler_params=_cparams(("arbitrary",)),
        name="qkv_x" if with_q else "kv_ctx",
    )(*args)


def _attn_kernel(bound_ref, q_ref, kx_ref, kc_ref, vx_ref, vc_ref, o_ref):
    q = q_ref[...]
    tq = q.shape[0]
    dn = (((1,), (1,)), ((), ()))
    chunks = [(kx_ref, vx_ref, c * ATTN_KEY_CHUNK, ATTN_KEY_CHUNK) for c in range(kx_ref.shape[0] // ATTN_KEY_CHUNK)]
    chunks.append((kc_ref, vc_ref, 0, kc_ref.shape[0]))

    def scores(k_ref, lo, n):
        return lax.dot_general(q, k_ref[lo:lo + n, :], dn, preferred_element_type=F32)

    def finish(acc):
        o_ref[...] = acc[:, :V_HEAD_DIM] / acc[:, V_HEAD_DIM:V_HEAD_DIM + 1]

    bound = bound_ref[0:1, 0:1]
    acc = jnp.zeros((tq, V_PAD), F32)
    for k_ref, v_ref, lo, n in chunks:
        p = jnp.exp2(scores(k_ref, lo, n) - bound).astype(BF16)
        acc = acc + _bdot(p, v_ref[lo:lo + n, :])
    finish(acc)

    row_sum_ok = jnp.min(acc[:, V_HEAD_DIM:V_HEAD_DIM + 1]) >= ATTN_MIN_ROW_SUM

    @pl.when(jnp.logical_not(row_sum_ok))
    def _():
        m = jnp.full((tq, 1), NEG_BIG, F32)
        acc2 = jnp.zeros((tq, V_PAD), F32)
        for k_ref, v_ref, lo, n in chunks:
            s = scores(k_ref, lo, n)
            m_new = jnp.maximum(m, jnp.max(s, axis=-1, keepdims=True))
            p = jnp.exp2(s - m_new).astype(BF16)
            acc2 = jnp.exp2(m - m_new) * acc2 + _bdot(p, v_ref[lo:lo + n, :])
            m = m_new
        finish(acc2)


def _attention(score_bound, q, kx, kc, vx, vc, *, tq):
    b, h, s, _ = q.shape
    lc = kc.shape[2]
    return pl.pallas_call(
        _attn_kernel,
        grid=(b, h, s // tq),
        in_specs=[pl.BlockSpec((1, LANES), lambda bi, hi, qi: (0, 0)),
                  pl.BlockSpec((None, None, tq, HEAD_PAD), lambda bi, hi, qi: (bi, hi, qi, 0)),
                  pl.BlockSpec((None, None, s, HEAD_PAD), lambda bi, hi, qi: (bi, hi, 0, 0)),
                  pl.BlockSpec((None, None, lc, HEAD_PAD), lambda bi, hi, qi: (bi, hi, 0, 0)),
                  pl.BlockSpec((None, None, s, V_PAD), lambda bi, hi, qi: (bi, hi, 0, 0)),
                  pl.BlockSpec((None, None, lc, V_PAD), lambda bi, hi, qi: (bi, hi, 0, 0))],
        out_specs=pl.BlockSpec((None, tq, V_HEAD_DIM), lambda bi, hi, qi: (bi, qi, hi)),
        out_shape=jax.ShapeDtypeStruct((b, s, h * V_HEAD_DIM), F32),
        compiler_params=_cparams(("arbitrary", "arbitrary", "arbitrary")),
        name="attention",
    )(score_bound, q, kx, kc, vx, vc)


def _seq_dft_kernel(ure_ref, uim_ref, r_ref, e_ref, t2_ref, o_ref, a_ref):
    s = pl.program_id(2)
    sub = SUBLANES
    n = 2 * GRID_W * sub
    cols = ure_ref.shape[-1]

    @pl.when(s < DFT_BLOCKS)
    def _():
        t = _bdot(r_ref[...].astype(BF16), e_ref[...])
        row = lax.broadcasted_iota(jnp.int32, (n, n), 0)
        col = lax.broadcasted_iota(jnp.int32, (n, n), 1)
        t = jnp.where((row % sub) == (col % sub), t, 0.0).astype(BF16)
        rhs = jnp.concatenate([ure_ref[...].reshape(GRID_W * sub, cols), uim_ref[...].reshape(GRID_W * sub, cols)],
                              axis=0).astype(BF16)
        a = _bdot(t, rhs)
        a_ref[:, :, pl.ds(pl.multiple_of(s * sub, sub), sub), :] = a.reshape(2, GRID_W, sub, cols)

    @pl.when(s >= DFT_BLOCKS)
    def _():
        k0 = pl.multiple_of((s - DFT_BLOCKS) * sub, sub)
        rhs = a_ref[:, pl.ds(k0, sub), :, :].reshape(2 * sub * GRID_W, cols).astype(BF16)
        y = _bdot(t2_ref[...].astype(BF16), rhs)
        o_ref[...] = y.reshape(GRID_W, sub, cols)


def _seq_dft_tables(n_seq):
    w, sub, nb = GRID_W, SUBLANES, DFT_BLOCKS
    ch = np.arange(nb).reshape(nb, 1, 1, 1)
    kb = np.arange(w).reshape(1, w, 1, 1)
    j = np.arange(sub).reshape(1, 1, sub, 1)
    r = np.arange(w).reshape(1, 1, 1, w)
    ang = (2.0 * np.pi / n_seq) * ((kb * (w * r + sub * ch + j)) % n_seq)
    c, s = np.cos(ang), np.sin(ang)
    rot = np.stack([np.stack([c, s], axis=3), np.stack([-s, c], axis=3)], axis=1)
    r1 = rot.reshape(nb, 2 * w * sub, 2 * w).astype(np.float32)
    expand = (np.arange(2 * w * sub)[None, :] // sub == np.arange(2 * w)[:, None]).astype(np.float32)
    ka = np.arange(w).reshape(w, 1)
    cp = np.arange(w).reshape(1, w)
    ang2 = (2.0 * np.pi / w) * ((ka * cp) % w)
    norm = 1.0 / math.sqrt(n_seq * FOURIER_GROUP_DIM)
    cs = np.stack([np.cos(ang2), np.sin(ang2)], axis=1) * norm
    eye = np.eye(sub)
    t2 = (cs[:, None, :, None, :] * eye[None, :, None, :, None]).reshape(w * sub, 2 * sub * w).astype(np.float32)
    return jnp.asarray(r1), jnp.asarray(expand).astype(BF16), jnp.asarray(t2)


def _seq_dft(u, batch, n_seq):
    assert n_seq == GRID_W * GRID_W
    w, sub, nb = GRID_W, SUBLANES, DFT_BLOCKS
    r1, expand, t2 = _seq_dft_tables(n_seq)
    halves = 2
    cols = FOURIER_DIM // halves
    u5 = u.reshape(batch, w, nb, sub, 2 * FOURIER_DIM)

    def u_spec(part):
        return pl.BlockSpec((None, w, None, sub, cols),
                            lambda b, h, s: (b, 0, jnp.minimum(s, nb - 1), 0, part * halves + h))

    y = pl.pallas_call(
        _seq_dft_kernel,
        grid=(batch, halves, 2 * nb),
        in_specs=[u_spec(0), u_spec(1),
                  pl.BlockSpec((None, 2 * w * sub, 2 * w), lambda b, h, s: (jnp.minimum(s, nb - 1), 0, 0)),
                  pl.BlockSpec((2 * w, 2 * w * sub), lambda b, h, s: (0, 0)),
                  pl.BlockSpec((w * sub, 2 * sub * w), lambda b, h, s: (0, 0))],
        out_specs=pl.BlockSpec((None, w, None, sub, cols), lambda b, h, s: (b, 0, jnp.maximum(s - nb, 0), 0, h)),
        out_shape=jax.ShapeDtypeStruct((batch, w, nb, sub, FOURIER_DIM), F32),
        scratch_shapes=[pltpu.VMEM((2, w, w, cols), F32)],
        compiler_params=_cparams(("arbitrary", "arbitrary", "arbitrary")),
        name="seq_dft",
    )(u5, u5, r1, expand, t2)
    return y.reshape(batch * n_seq, FOURIER_DIM)


def _out_router_kernel(x_ref, four_ref, attn_ref, gt1_ref, sh2_ref, sc2_ref, gf_ref, ga_ref, g2_ref,
                       wo_ref, wrh_ref, wrl_ref, br_ref,
                       xnew_ref, hm_ref, ri_ref, rw_ref, cnt_ref, carry_ref, *, mod_row):
    i = pl.program_id(0)
    tm = x_ref.shape[0]
    row = pl.ds(mod_row(i), 1)

    @pl.when(i == 0)
    def _():
        carry_ref[...] = jnp.zeros_like(carry_ref)

    def norm(v, g):
        return (v * lax.rsqrt(jnp.mean(v * v, axis=-1, keepdims=True) + EPS) * g).astype(BF16)

    mix = (_bdot(norm(four_ref[...], gf_ref[...]), wo_ref[0:FOURIER_DIM, :])
           + _bdot(norm(attn_ref[...], ga_ref[...]), wo_ref[FOURIER_DIM:FOURIER_DIM + MLA_DIM, :]))
    xn = x_ref[...] + gt1_ref[row, :] * mix
    xnew_ref[...] = xn
    ms = jnp.mean(xn * xn, axis=-1, keepdims=True)
    hm = xn * lax.rsqrt(ms + EPS) * (g2_ref[...] * (1.0 + sc2_ref[row, :])) + sh2_ref[row, :]
    hm_ref[...] = _pack_halves(hm)

    hm_hi = hm.astype(BF16)
    hm_lo = (hm - hm_hi.astype(F32)).astype(BF16)
    logits = _bdot(hm_hi, wrh_ref[...]) + _bdot(hm_lo, wrh_ref[...]) + _bdot(hm_hi, wrl_ref[...]) + br_ref[...]
    lane = lax.broadcasted_iota(jnp.int32, (tm, LANES), 1)
    lanef = lane.astype(F32)
    far = 1e9

    lg = jnp.where(lane < N_GROUPS, logits, NEG_BIG)
    m1 = jnp.max(lg, axis=-1, keepdims=True)
    g_p = 1.0 / jnp.sum(jnp.exp(lg - m1), axis=-1, keepdims=True)
    gidx = jnp.min(jnp.where(lg >= m1, lanef, far), axis=-1, keepdims=True)
    lo = N_GROUPS + EXPERTS_PER_GROUP * gidx
    in_group = jnp.where(lanef >= lo, jnp.where(lanef < lo + EXPERTS_PER_GROUP, 1.0, 0.0), 0.0) > 0.5
    le = jnp.where(in_group, logits, NEG_BIG)
    m2 = jnp.max(le, axis=-1, keepdims=True)
    idx1 = jnp.min(jnp.where(le >= m2, lanef, far), axis=-1, keepdims=True)
    le2 = jnp.where(lanef == idx1, NEG_BIG, le)
    m3 = jnp.max(le2, axis=-1, keepdims=True)
    idx2 = jnp.min(jnp.where(le2 >= m3, lanef, far), axis=-1, keepdims=True)
    t = jnp.exp(m3 - m2)
    p1 = 1.0 / (1.0 + t)
    p2 = t / (1.0 + t)
    e1 = idx1 - N_GROUPS
    e2 = idx2 - N_GROUPS

    oh1 = jnp.where(lanef == e1, 1.0, 0.0)
    oh2 = jnp.where(lanef == e2, 1.0, 0.0)
    ohs = oh1 + oh2
    row = lax.broadcasted_iota(jnp.int32, (tm, tm), 0)
    col = lax.broadcasted_iota(jnp.int32, (tm, tm), 1)
    tri = jnp.where(row > col, 1.0, 0.0).astype(BF16)
    before = _bdot(tri, ohs.astype(BF16)) + carry_ref[...]
    rank1 = jnp.sum(oh1 * before, axis=-1, keepdims=True)
    rank2 = jnp.sum(oh2 * before, axis=-1, keepdims=True)
    carry = carry_ref[...] + jnp.sum(ohs, axis=0, keepdims=True)
    carry_ref[...] = carry
    cnt_ref[...] = jnp.broadcast_to(carry, cnt_ref.shape)

    ri = jnp.where(lane == 0, e1, jnp.where(lane == 1, e2, jnp.where(lane == 2, rank1, jnp.where(lane == 3, rank2, 0.0))))
    ri_ref[...] = ri.astype(jnp.int32)
    rw_ref[...] = jnp.where(lane == 0, g_p * p1, jnp.where(lane == 1, g_p * p2, 0.0))


def _out_router(x2, four, attn, mods, gf, ga, g2, wo, wrh, wrl, br, *, seq, tm):
    t, d = x2.shape
    nt = t // tm
    tiles_per_b = seq // tm

    def rows(n):
        return pl.BlockSpec((tm, n), lambda i: (i, 0))

    def const(arr):
        return pl.BlockSpec(arr.shape, lambda i: (0,) * arr.ndim, pipeline_mode=pl.Buffered(1))

    return pl.pallas_call(
        functools.partial(_out_router_kernel, mod_row=lambda i: i // tiles_per_b),
        grid=(nt,),
        in_specs=[rows(d), rows(FOURIER_DIM), rows(MLA_DIM), _mod_spec(mods, 2, d), _mod_spec(mods, 3, d),
                  _mod_spec(mods, 4, d),
                  const(gf), const(ga), const(g2), const(wo), const(wrh), const(wrl), const(br)],
        out_specs=[rows(d), rows(d // 2), rows(LANES), rows(LANES), pl.BlockSpec((8, LANES), lambda i: (0, 0))],
        out_shape=[jax.ShapeDtypeStruct((t, d), F32), jax.ShapeDtypeStruct((t, d // 2), jnp.uint32),
                   jax.ShapeDtypeStruct((t, LANES), jnp.int32), jax.ShapeDtypeStruct((t, LANES), F32),
                   jax.ShapeDtypeStruct((8, LANES), F32)],
        scratch_shapes=[pltpu.VMEM((1, LANES), F32)],
        compiler_params=_cparams(("arbitrary",)),
        name="out_proj_router",
    )(x2, four, attn, mods, mods, mods, gf, ga, g2, wo, wrh, wrl, br)


def _dispatch_kernel(dest, seg_last, seg_blocks, used_blocks, hm_ref, xs_hbm, zbuf, sem, zsem):
    i = pl.program_id(0)
    tm = hm_ref.shape[0]

    @pl.when(i == 0)
    def _():
        zbuf[...] = jnp.zeros(zbuf.shape, zbuf.dtype)
        n_blocks = xs_hbm.shape[0] // MOE_BLOCK

        def zero_copy(blk):
            r0 = pl.multiple_of(blk * MOE_BLOCK, MOE_BLOCK)
            return pltpu.make_async_copy(zbuf, xs_hbm.at[pl.ds(r0, MOE_BLOCK)], zsem.at[0])

        def seg_start(e, carry):
            @pl.when(seg_blocks[e] > 0)
            def _():
                zero_copy(seg_last[e]).start()
            return carry

        def seg_wait(e, carry):
            @pl.when(seg_blocks[e] > 0)
            def _():
                zero_copy(seg_last[e]).wait()
            return carry

        def tail_start(blk, carry):
            zero_copy(blk).start()
            return carry

        def tail_wait(blk, carry):
            zero_copy(blk).wait()
            return carry

        lax.fori_loop(0, N_EXPERTS, seg_start, 0)
        lax.fori_loop(used_blocks[0], n_blocks, tail_start, 0)
        lax.fori_loop(0, N_EXPERTS, seg_wait, 0)
        lax.fori_loop(used_blocks[0], n_blocks, tail_wait, 0)

    def body(r8, carry):
        for u in range(GATHER_UNROLL):
            r = r8 * GATHER_UNROLL + u
            for k in range(2):
                pltpu.make_async_copy(hm_ref.at[pl.ds(r, 1)], xs_hbm.at[pl.ds(dest[2 * (i * tm + r) + k], 1)],
                                      sem.at[0]).start(priority=k)
        return carry

    lax.fori_loop(0, tm // GATHER_UNROLL, body, 0)
    for k in range(2):
        pltpu.make_async_copy(hm_ref, xs_hbm.at[pl.ds(0, tm)], sem.at[0]).wait()


def _dispatch(dest, seg_last, seg_blocks, used_blocks, hm, *, n_rows, tm):
    t, w = hm.shape
    grid_spec = pltpu.PrefetchScalarGridSpec(
        num_scalar_prefetch=4,
        grid=(t // tm,),
        in_specs=[pl.BlockSpec((tm, w), lambda i, *_: (i, 0))],
        out_specs=pl.BlockSpec(memory_space=pl.ANY),
        scratch_shapes=[pltpu.VMEM((MOE_BLOCK, w), hm.dtype), pltpu.SemaphoreType.DMA((1,)),
                        pltpu.SemaphoreType.DMA((1,))],
    )
    return pl.pallas_call(
        _dispatch_kernel,
        grid_spec=grid_spec,
        out_shape=jax.ShapeDtypeStruct((n_rows, w), hm.dtype),
        compiler_params=_cparams(("arbitrary",)),
        name="moe_dispatch",
    )(dest, seg_last, seg_blocks, used_blocks, hm)


def _moe_kernel(item_e, item_blk0, item_nblk, used_blocks,
                xs_hbm, wg_hbm, wu_hbm, wd_hbm, y_hbm,
                xg, xb, gs, ab, yp, gu_buf, dn_buf, gsem, osem, gusem, dnsem):
    i = pl.program_id(0)
    j = pl.program_id(1)
    n_items = pl.num_programs(0)
    nj = pl.num_programs(1)
    slot = i % 2
    nblk = item_nblk[i]

    def weight_copy(it, ph):
        if ph == 2:
            ws = it % DN_SLOTS
            return pltpu.make_async_copy(wd_hbm.at[item_e[it]], dn_buf.at[ws], dnsem.at[ws])
        ws = (2 * it + ph) % GU_SLOTS
        return pltpu.make_async_copy((wg_hbm, wu_hbm)[ph].at[item_e[it]], gu_buf.at[ws], gusem.at[ws])

    def start_weight(it, ph):
        it_c = jnp.minimum(it, n_items - 1)

        @pl.when(jnp.logical_and(it < n_items, item_nblk[it_c] > 0))
        def _():
            weight_copy(it_c, ph).start(priority=WEIGHT_DMA_PRIORITY)

    def x_copy(it, sl, m):
        r0 = pl.multiple_of(item_blk0[it] * MOE_BLOCK, MOE_BLOCK)
        return pltpu.make_async_copy(xs_hbm.at[pl.ds(r0, m)], xg.at[sl, pl.ds(0, m)], gsem.at[sl])

    def issue_gather(it, sl):
        for nb in range(1, ITEM_BLOCKS + 1):
            @pl.when(item_nblk[it] == nb)
            def _():
                x_copy(it, sl, nb * MOE_BLOCK).start(priority=ROW_DMA_PRIORITY)

    def wait_gather(it, sl):
        for nb in range(1, ITEM_BLOCKS + 1):
            @pl.when(item_nblk[it] == nb)
            def _():
                x_copy(it, sl, nb * MOE_BLOCK).wait()

    def out_copy(it, m):
        r0 = pl.multiple_of(item_blk0[it] * MOE_BLOCK, MOE_BLOCK)
        return pltpu.make_async_copy(yp.at[pl.ds(0, m)], y_hbm.at[pl.ds(r0, m)], osem.at[0])

    def wait_out(it):
        for nb in range(1, ITEM_BLOCKS + 1):
            @pl.when(item_nblk[it] == nb)
            def _():
                out_copy(it, nb * MOE_BLOCK).wait()

    @pl.when(j == 0)
    def _():
        @pl.when(i == 0)
        def _():
            start_weight(0, 0)
            start_weight(0, 1)
            start_weight(1, 0)
            start_weight(0, 2)
            issue_gather(0, 0)

        start_weight(i + 1, 2)
        wait_gather(i, slot)

        @pl.when(i + 1 < n_items)
        def _():
            issue_gather(i + 1, 1 - slot)

    @pl.when(j == 1)
    def _():
        start_weight(i + 1, 1)

    @pl.when(j == 2)
    def _():
        start_weight(i + 2, 0)

        @pl.when(i > 0)
        def _():
            wait_out(i - 1)

    for ph in range(3):
        @pl.when(jnp.logical_and(j == ph, nblk > 0))
        def _():
            weight_copy(i, ph).wait()

    for nb in range(1, ITEM_BLOCKS + 1):
        m = nb * MOE_BLOCK

        @pl.when(jnp.logical_and(nblk == nb, j == 0))
        def _():
            hi, lo = _unpack_halves(xg[slot, 0:m, :])
            half = hi.shape[1]
            xb[0:m, 0:half] = hi.astype(BF16)
            xb[0:m, half:2 * half] = lo.astype(BF16)
            gs[0:m, :] = _bdot(xb[0:m, :], gu_buf[(2 * i) % GU_SLOTS].astype(BF16))

        @pl.when(jnp.logical_and(nblk == nb, j == 1))
        def _():
            g = gs[0:m, :]
            u = _bdot(xb[0:m, :], gu_buf[(2 * i + 1) % GU_SLOTS].astype(BF16))
            ab[0:m, :] = (g * jax.nn.sigmoid(g) * u).astype(BF16)

        @pl.when(jnp.logical_and(nblk == nb, j == 2))
        def _():
            yp[0:m, :] = _pack_halves(_bdot(ab[0:m, :], dn_buf[i % DN_SLOTS].astype(BF16)))
            out_copy(i, m).start()

    @pl.when(jnp.logical_and(i == n_items - 1, j == nj - 1))
    def _():
        wait_out(i)
        n_blocks = y_hbm.shape[0] // MOE_BLOCK
        yp[0:MOE_BLOCK, :] = jnp.zeros((MOE_BLOCK, yp.shape[1]), jnp.uint32)

        def tail_copy(blk):
            r0 = pl.multiple_of(blk * MOE_BLOCK, MOE_BLOCK)
            return pltpu.make_async_copy(yp.at[pl.ds(0, MOE_BLOCK)], y_hbm.at[pl.ds(r0, MOE_BLOCK)], osem.at[0])

        def start_body(blk, carry):
            tail_copy(blk).start()
            return carry

        def wait_body(blk, carry):
            tail_copy(blk).wait()
            return carry

        lax.fori_loop(used_blocks[0], n_blocks, start_body, 0)
        lax.fori_loop(used_blocks[0], n_blocks, wait_body, 0)


def _moe(xs, w_gate, w_up, w_down, item_e, item_blk0, item_nblk, used_blocks, *, n_rows):
    d, de = w_gate.shape[1], w_gate.shape[2]
    n_items = item_e.shape[0]
    nj = 3
    rows = ITEM_BLOCKS * MOE_BLOCK
    any_spec = pl.BlockSpec(memory_space=pl.ANY)
    grid_spec = pltpu.PrefetchScalarGridSpec(
        num_scalar_prefetch=4,
        grid=(n_items, nj),
        in_specs=[any_spec, any_spec, any_spec, any_spec],
        out_specs=any_spec,
        scratch_shapes=[pltpu.VMEM((2, rows, d // 2), jnp.uint32),
                        pltpu.VMEM((rows, d), BF16),
                        pltpu.VMEM((rows, de), F32),
                        pltpu.VMEM((rows, de), BF16),
                        pltpu.VMEM((rows, d // 2), jnp.uint32),
                        pltpu.VMEM((GU_SLOTS, d, de), F32),
                        pltpu.VMEM((DN_SLOTS, de, d), F32),
                        pltpu.SemaphoreType.DMA((2,)),
                        pltpu.SemaphoreType.DMA((1,)),
                        pltpu.SemaphoreType.DMA((GU_SLOTS,)),
                        pltpu.SemaphoreType.DMA((DN_SLOTS,))],
    )
    return pl.pallas_call(
        _moe_kernel,
        grid_spec=grid_spec,
        out_shape=jax.ShapeDtypeStruct((n_rows, d // 2), jnp.uint32),
        compiler_params=_cparams(("arbitrary", "arbitrary")),
        name="moe_experts",
    )(item_e, item_blk0, item_nblk, used_blocks, xs, w_gate, w_up, w_down)


def _combine_kernel(dest, x_ref, gt2_ref, rw_ref, y_hbm, o_ref, ybuf, sem, *, mod_row):
    i = pl.program_id(0)
    n = pl.num_programs(0)
    tm = x_ref.shape[0]
    slot = i % 2

    def issue(it, sl):
        base = it * tm

        def body(r4, carry):
            for rr in range(GATHER_UNROLL // 2):
                r = r4 * (GATHER_UNROLL // 2) + rr
                for k in range(2):
                    pltpu.make_async_copy(y_hbm.at[pl.ds(dest[2 * (base + r) + k], 1)], ybuf.at[sl, k, pl.ds(r, 1)],
                                          sem.at[sl]).start(priority=k)
            return carry

        lax.fori_loop(0, tm // (GATHER_UNROLL // 2), body, 0)

    @pl.when(i == 0)
    def _():
        issue(0, 0)

    for k in range(2):
        pltpu.make_async_copy(y_hbm.at[pl.ds(0, tm)], ybuf.at[slot, k], sem.at[slot]).wait()

    @pl.when(i + 1 < n)
    def _():
        issue(i + 1, 1 - slot)

    w = rw_ref[...]
    gate = gt2_ref[pl.ds(mod_row(i), 1), :]
    hi0, lo0 = _unpack_halves(ybuf[slot, 0])
    hi1, lo1 = _unpack_halves(ybuf[slot, 1])
    half = hi0.shape[1]
    o_ref[:, 0:half] = x_ref[:, 0:half] + gate[:, 0:half] * (w[:, 0:1] * hi0 + w[:, 1:2] * hi1)
    o_ref[:, half:2 * half] = (x_ref[:, half:2 * half]
                               + gate[:, half:2 * half] * (w[:, 0:1] * lo0 + w[:, 1:2] * lo1))


def _combine(dest, xnew, mods, rw, y, *, seq, tm):
    t, d = xnew.shape
    tiles_per_b = seq // tm
    grid_spec = pltpu.PrefetchScalarGridSpec(
        num_scalar_prefetch=1,
        grid=(t // tm,),
        in_specs=[pl.BlockSpec((tm, d), lambda i, ds: (i, 0)),
                  _mod_spec(mods, 5, d),
                  pl.BlockSpec((tm, LANES), lambda i, ds: (i, 0)),
                  pl.BlockSpec(memory_space=pl.ANY)],
        out_specs=pl.BlockSpec((tm, d), lambda i, ds: (i, 0)),
        scratch_shapes=[pltpu.VMEM((2, 2, tm, d // 2), jnp.uint32), pltpu.SemaphoreType.DMA((2,))],
    )
    return pl.pallas_call(
        functools.partial(_combine_kernel, mod_row=lambda i: i // tiles_per_b),
        grid_spec=grid_spec,
        out_shape=jax.ShapeDtypeStruct((t, d), F32),
        compiler_params=_cparams(("arbitrary",)),
        name="moe_combine",
    )(dest, xnew, mods, rw, y)


def _rope_tables(n_tokens):
    rows = n_tokens // GRID_W
    row = jnp.repeat(jnp.arange(rows, dtype=jnp.int32), GRID_W).astype(F32)
    col = jnp.tile(jnp.arange(GRID_W, dtype=jnp.int32), rows).astype(F32)
    n_freq = QK_ROPE_DIM // 4
    inv = ROPE_THETA ** (-jnp.arange(n_freq, dtype=F32) / n_freq)
    ar = row[:, None] * inv[None, :]
    ac = col[:, None] * inv[None, :]
    cr, sr, cc, sc = jnp.cos(ar), jnp.sin(ar), jnp.cos(ac), jnp.sin(ac)
    cos64 = jnp.concatenate([cr, cr, cc, cc], axis=-1)
    sin64 = jnp.concatenate([-sr, sr, -sc, sc], axis=-1)
    return jnp.tile(cos64, (1, 2)), jnp.tile(sin64, (1, 2))


def _channel_dft_table():
    c = np.arange(FOURIER_GROUP_DIM).reshape(-1, 1)
    k = np.arange(FOURIER_GROUP_DIM).reshape(1, -1)
    ang = (2.0 * np.pi / FOURIER_GROUP_DIM) * ((c * k) % FOURIER_GROUP_DIM)
    return jnp.asarray(np.concatenate([np.cos(ang), -np.sin(ang)], axis=1).astype(np.float32)).astype(BF16)


def _split_heads(w, widths):
    k = w.shape[0]
    wh = w.reshape(k, MLA_HEADS, sum(widths))
    outs, off = [], 0
    for wd in widths:
        outs.append(wh[:, :, off:off + wd].reshape(k, MLA_HEADS * wd))
        off += wd
    return outs


def kernel(x, c, ctx, c_ctx, w_ada, b_ada, g_norm1, g_norm2, w_in, g_q_a, g_kv_a, w_uq, w_ukv, g_qk_q, g_qk_k,
           g_out_four, g_out_attn, w_out, w_router_group, b_router_group, w_router_expert, b_router_expert,
           w_gate, w_up, w_down):
    b, s, d = x.shape
    lc = ctx.shape[1]
    t = b * s
    layer_params = (w_ada, b_ada, g_norm1, g_norm2, w_in, g_q_a, g_kv_a, w_uq, w_ukv, g_qk_q, g_qk_k, g_out_four,
                    g_out_attn, w_out, w_router_group, b_router_group, w_router_expert, b_router_expert,
                    w_gate, w_up, w_down)
    assert all(p.shape[0] == 1 for p in layer_params), "single-layer block"
    (w_ada, b_ada, g_norm1, g_norm2, w_in, g_q_a, g_kv_a, w_uq, w_ukv, g_qk_q, g_qk_k, g_out_four,
     g_out_attn, w_out, w_router_group, b_router_group, w_router_expert, b_router_expert,
     w_gate, w_up, w_down) = [p.reshape(p.shape[1:]) for p in layer_params]

    cond8 = jnp.concatenate([c, c_ctx[None, :], jnp.zeros((8 - b - 1, d), F32)], axis=0)
    mods = _ada_mod(cond8, w_ada, b_ada)

    w_all = jnp.concatenate([w_in, w_in[:, ROPE_OFF:]], axis=1).astype(BF16)
    dc = _channel_dft_table()
    wqn, wqr = [w.astype(BF16) for w in _split_heads(w_uq, (QK_NOPE_DIM, QK_ROPE_DIM))]
    wkn, wv = [w.astype(BF16) for w in _split_heads(w_ukv, (QK_NOPE_DIM, V_HEAD_DIM))]
    gqn = g_qk_q[:QK_NOPE_DIM].reshape(1, -1)
    gqr2 = jnp.tile(g_qk_q[QK_NOPE_DIM:], 2).reshape(1, -1)
    gkn = g_qk_k[:QK_NOPE_DIM].reshape(1, -1)
    gkr2 = jnp.tile(g_qk_k[QK_NOPE_DIM:], 2).reshape(1, -1)
    g1 = g_norm1.reshape(1, d)
    gq = g_q_a.reshape(1, -1)
    gkv = g_kv_a.reshape(1, -1)

    x2 = x.reshape(t, d)
    u, cq, ckv, kr2 = _in_proj(x2, mods, 0, s, g1, gq, gkv, w_all, dc, with_q=True, tm=IN_PROJ_ROWS)
    ckv_c, kr2_c = _in_proj(ctx.reshape(b * lc, d), mods, b, b * lc, g1, gq, gkv, w_all, dc, with_q=False, tm=lc)

    cos_t, sin_t = _rope_tables(s)
    q, kx, vx = _qkv(cq, ckv, kr2, cos_t, sin_t, wqn, wqr, gqn, gqr2, wkn, wv, gkn, gkr2,
                     batch=b, seq=s, tm=QKV_ROWS, with_q=True, with_rope=True)
    kc, vc = _qkv(None, ckv_c, kr2_c, None, None, None, None, None, None, wkn, wv, gkn, gkr2,
                  batch=b, seq=lc, tm=lc, with_q=False, with_rope=False)

    score_bound = (QK_HEAD_DIM ** 0.5 * LOG2_E * ATTN_BOUND_MARGIN) * jnp.max(jnp.abs(g_qk_q)) * jnp.max(jnp.abs(g_qk_k))
    score_bound = jnp.full((1, LANES), score_bound, F32)
    attn = _attention(score_bound, q, kx, kc, vx, vc, tq=ATTN_Q_ROWS).reshape(t, MLA_DIM)
    four = _seq_dft(u, b, s)

    wo = w_out.astype(BF16)
    n_route = N_GROUPS + N_EXPERTS
    wrt = jnp.concatenate([w_router_group, w_router_expert, jnp.zeros((d, LANES - n_route), F32)], axis=1)
    wrt_hi = wrt.astype(BF16)
    wrt_lo = (wrt - wrt_hi.astype(F32)).astype(BF16)
    brt = jnp.concatenate([b_router_group, b_router_expert, jnp.zeros((LANES - n_route,), F32)]).reshape(1, -1)
    xnew, hm, ri, rw, cnt = _out_router(x2, four, attn, mods, g_out_four.reshape(1, -1), g_out_attn.reshape(1, -1),
                                        g_norm2.reshape(1, d), wo, wrt_hi, wrt_lo, brt, seq=s, tm=OUT_ROUTER_ROWS)

    counts = cnt[0, :N_EXPERTS].astype(jnp.int32)
    nblk_e = (counts + MOE_BLOCK - 1) // MOE_BLOCK
    blk_end = jnp.cumsum(nblk_e)
    blk_start = blk_end - nblk_e
    n_slots = t * 2
    n_blocks = -(-(n_slots + N_EXPERTS * (MOE_BLOCK - 1)) // MOE_BLOCK)
    n_rows = n_blocks * MOE_BLOCK
    e12 = ri[:, 0:2]
    seg_start = jnp.sum(jnp.where(e12[:, :, None] == jnp.arange(N_EXPERTS, dtype=jnp.int32), blk_start, 0), axis=-1)
    dest = (seg_start * MOE_BLOCK + ri[:, 2:4]).reshape(-1).astype(jnp.int32)
    items_e = (nblk_e + ITEM_BLOCKS - 1) // ITEM_BLOCKS
    item_end = jnp.cumsum(items_e)
    n_items = (n_blocks + (ITEM_BLOCKS - 1) * N_EXPERTS) // ITEM_BLOCKS
    idx = jnp.arange(n_items, dtype=jnp.int32)
    total = item_end[-1]
    idx_c = jnp.minimum(idx, total - 1)
    ie = jnp.minimum(jnp.sum(idx_c[:, None] >= item_end[None, :], axis=1), N_EXPERTS - 1).astype(jnp.int32)
    local = idx_c - (item_end - items_e)[ie]
    item_blk0 = (blk_start[ie] + ITEM_BLOCKS * local).astype(jnp.int32)
    item_nblk = jnp.where(idx < total, jnp.clip(nblk_e[ie] - ITEM_BLOCKS * local, 0, ITEM_BLOCKS), 0).astype(jnp.int32)

    used_blocks = blk_end[-1:].astype(jnp.int32)
    seg_last = jnp.maximum(blk_end - 1, 0).astype(jnp.int32)
    xs = _dispatch(dest, seg_last, nblk_e.astype(jnp.int32), used_blocks, hm, n_rows=n_rows, tm=DISPATCH_ROWS)
    y = _moe(xs, w_gate, w_up, w_down, ie, item_blk0, item_nblk, used_blocks, n_rows=n_rows)
    out = _combine(dest, xnew, mods, rw, y, seq=s, tm=COMBINE_ROWS)
    return out.reshape(b, s, d)
```

```python
import functools
import math

import numpy as np
import jax
import jax.numpy as jnp
from jax import lax
from jax.experimental import pallas as pl
from jax.experimental.pallas import tpu as pltpu

F32 = jnp.float32
BF16 = jnp.bfloat16

D_MODEL = 2048
GRID_W = 64
EPS = 1e-6
N_MOD = 6
N_FOURIER_GROUPS = 4
FOURIER_GROUP_DIM = 256
FOURIER_DIM = 1024
MLA_HEADS = 8
QK_NOPE_DIM = 128
QK_ROPE_DIM = 64
QK_HEAD_DIM = 192
V_HEAD_DIM = 128
Q_LORA_RANK = 768
KV_LORA_RANK = 512
MLA_DIM = 1024
ROPE_THETA = 10000.0
Q_OFF = FOURIER_DIM
KV_OFF = Q_OFF + Q_LORA_RANK
ROPE_OFF = KV_OFF + KV_LORA_RANK
N_GROUPS = 8
EXPERTS_PER_GROUP = 8
N_EXPERTS = 64
D_EXPERT = 768

ROT_HALF = QK_ROPE_DIM // 4
ROT_BLOCK = 2 * ROT_HALF

ADA_COLS = 1024
IN_PROJ_ROWS = 512
QKV_ROWS = 512
ATTN_Q_ROWS = 2048
OUT_ROUTER_ROWS = 512
COMBINE_ROWS = 512
DISPATCH_ROWS = 1024

LANES = 128
SUBLANES = 8
DFT_BLOCKS = GRID_W // SUBLANES
HEAD_PAD = 256
V_PAD = 256
MOE_BLOCK = 64
ITEM_BLOCKS = 8
GATHER_UNROLL = 8
GU_SLOTS = 3
DN_SLOTS = 2
ATTN_KEY_CHUNK = 512
ATTN_BOUND_MARGIN = 1.02
ATTN_MIN_ROW_SUM = 1e-30
LOG2_E = math.log2(math.e)
ROW_DMA_PRIORITY = 0
WEIGHT_DMA_PRIORITY = 1
VMEM_LIMIT = 56 * 1024 * 1024
NEG_BIG = -1e30


def _cparams(sem):
    return pltpu.CompilerParams(dimension_semantics=sem, vmem_limit_bytes=VMEM_LIMIT)


def _bdot(a, b):
    return jnp.dot(a, b, preferred_element_type=F32)


def _pack_halves(v):
    n = v.shape[1] // 2
    hi = pltpu.bitcast(v[:, :n].astype(BF16).astype(F32), jnp.uint32)
    lo = pltpu.bitcast(v[:, n:].astype(BF16).astype(F32), jnp.uint32)
    return hi | (lo >> 16)


def _unpack_halves(u):
    hi = pltpu.bitcast(u & jnp.uint32(0xFFFF0000), F32)
    lo = pltpu.bitcast(u << 16, F32)
    return hi, lo


def _ada_kernel(c_ref, w_ref, b_ref, o_ref):
    c = c_ref[...]
    s = (c * jax.nn.sigmoid(c)).astype(BF16)
    o_ref[...] = _bdot(s, w_ref[...].astype(BF16)) + b_ref[...]


def _ada_mod(cond8, w_ada, b_ada):
    d, n = w_ada.shape
    tn = ADA_COLS
    return pl.pallas_call(
        _ada_kernel,
        grid=(n // tn,),
        in_specs=[pl.BlockSpec((8, d), lambda i: (0, 0)),
                  pl.BlockSpec((d, tn), lambda i: (0, i)),
                  pl.BlockSpec((1, tn), lambda i: (0, i))],
        out_specs=pl.BlockSpec((8, tn), lambda i: (0, i)),
        out_shape=jax.ShapeDtypeStruct((8, n), F32),
        compiler_params=_cparams(("arbitrary",)),
        name="ada_mod",
    )(cond8, w_ada, b_ada.reshape(1, n))


def _in_proj_kernel(x_ref, sh_ref, sc_ref, g_ref, gq_ref, gkv_ref, w_ref, dc_ref, *out_refs, with_q, mod_row):
    row = pl.ds(mod_row(pl.program_id(0)), 1)
    x = x_ref[...]
    ms = jnp.mean(x * x, axis=-1, keepdims=True)
    a = g_ref[...] * (1.0 + sc_ref[row, :])
    h = (x * lax.rsqrt(ms + EPS) * a + sh_ref[row, :]).astype(BF16)
    if with_q:
        u_ref, cq_ref, ckv_ref, kr_ref = out_refs
        f = _bdot(h, w_ref[:, 0:Q_OFF]).astype(BF16)
        dc = dc_ref[...]
        for g in range(N_FOURIER_GROUPS):
            lo = g * FOURIER_GROUP_DIM
            ug = _bdot(f[:, lo:lo + FOURIER_GROUP_DIM], dc)
            u_ref[:, lo:lo + FOURIER_GROUP_DIM] = ug[:, :FOURIER_GROUP_DIM]
            u_ref[:, FOURIER_DIM + lo:FOURIER_DIM + lo + FOURIER_GROUP_DIM] = ug[:, FOURIER_GROUP_DIM:]
        pq = _bdot(h, w_ref[:, Q_OFF:KV_OFF])
        msq = jnp.mean(pq * pq, axis=-1, keepdims=True)
        cq_ref[...] = (pq * lax.rsqrt(msq + EPS) * gq_ref[...]).astype(BF16)
    else:
        ckv_ref, kr_ref = out_refs
    pkv = _bdot(h, w_ref[:, KV_OFF:ROPE_OFF])
    mskv = jnp.mean(pkv * pkv, axis=-1, keepdims=True)
    ckv_ref[...] = (pkv * lax.rsqrt(mskv + EPS) * gkv_ref[...]).astype(BF16)
    kr_ref[...] = _bdot(h, w_ref[:, ROPE_OFF:ROPE_OFF + LANES])


def _mod_spec(mods, k, d):
    return pl.BlockSpec((mods.shape[0], d), lambda i, *_: (0, k))


def _in_proj(x2, mods, first_row, rows_per_mod, g1, gq, gkv, w_all, dc, *, with_q, tm):
    t, d = x2.shape
    nt = t // tm
    tiles_per_mod = rows_per_mod // tm

    def const(shape):
        return pl.BlockSpec(shape, lambda i: (0,) * len(shape))

    in_specs = [pl.BlockSpec((tm, d), lambda i: (i, 0)), _mod_spec(mods, 0, d), _mod_spec(mods, 1, d),
                const((1, d)), const((1, Q_LORA_RANK)), const((1, KV_LORA_RANK)),
                const(w_all.shape), const(dc.shape)]

    def rows(n):
        return pl.BlockSpec((tm, n), lambda i: (i, 0))

    out_specs = [rows(KV_LORA_RANK), rows(LANES)]
    out_shape = [jax.ShapeDtypeStruct((t, KV_LORA_RANK), BF16), jax.ShapeDtypeStruct((t, LANES), F32)]
    if with_q:
        out_specs = [rows(2 * FOURIER_DIM), rows(Q_LORA_RANK)] + out_specs
        out_shape = [jax.ShapeDtypeStruct((t, 2 * FOURIER_DIM), F32),
                     jax.ShapeDtypeStruct((t, Q_LORA_RANK), BF16)] + out_shape
    return pl.pallas_call(
        functools.partial(_in_proj_kernel, with_q=with_q, mod_row=lambda i: first_row + i // tiles_per_mod),
        grid=(nt,),
        in_specs=in_specs,
        out_specs=out_specs,
        out_shape=out_shape,
        compiler_params=_cparams(("arbitrary",)),
        name="in_proj_x" if with_q else "in_proj_ctx",
    )(x2, mods, mods, g1, gq, gkv, w_all, dc)


def _swap_halves(y, first_half):
    return jnp.where(first_half, pltpu.roll(y, LANES - ROT_HALF, 1), pltpu.roll(y, ROT_HALF, 1))


def _qkv_kernel(*refs, with_q, with_rope):
    it = iter(refs)
    if with_q:
        cq_ref = next(it)
    ckv_ref = next(it)
    kr_ref = next(it)
    if with_rope:
        cos_ref = next(it)
        sin_ref = next(it)
    if with_q:
        wqn_ref = next(it)
        wqr_ref = next(it)
        gqn_ref = next(it)
        gqr_ref = next(it)
    wkn_ref = next(it)
    wv_ref = next(it)
    gkn_ref = next(it)
    gkr_ref = next(it)
    if with_q:
        q_ref = next(it)
    k_ref = next(it)
    v_ref = next(it)

    tm = ckv_ref.shape[0]
    lane = lax.broadcasted_iota(jnp.int32, (tm, LANES), 1)
    low = lane < QK_ROPE_DIM
    first_half = (lane % ROT_BLOCK) < ROT_HALF
    inv_dim = 1.0 / QK_HEAD_DIM

    def rope(y):
        if not with_rope:
            return y
        return y * cos_ref[...] + _swap_halves(y, first_half) * sin_ref[...]

    if with_q:
        cq = cq_ref[...]
        qn = _bdot(cq, wqn_ref[...])
        qr = _bdot(cq, wqr_ref[...])
        qscale = QK_HEAD_DIM ** -0.5 * LOG2_E
        for p in range(MLA_HEADS // 2):
            blk = qr[:, p * LANES:(p + 1) * LANES]
            sq = blk * blk
            ss_lo = jnp.sum(jnp.where(low, sq, 0.0), axis=-1, keepdims=True)
            ss_hi = jnp.sum(jnp.where(low, 0.0, sq), axis=-1, keepdims=True)
            scales = []
            for hh, ssr in ((2 * p, ss_lo), (2 * p + 1, ss_hi)):
                nh = qn[:, hh * LANES:(hh + 1) * LANES]
                ssq = jnp.sum(nh * nh, axis=-1, keepdims=True) + ssr
                s = lax.rsqrt(ssq * inv_dim + EPS)
                scales.append(s)
                q_ref[hh, :, 0:LANES] = (nh * s * gqn_ref[...] * qscale).astype(BF16)
            s_pair = jnp.where(low, scales[0], scales[1])
            r = rope(blk * s_pair * gqr_ref[...]) * qscale
            q_ref[2 * p, :, LANES:2 * LANES] = jnp.where(low, r, 0.0).astype(BF16)
            q_ref[2 * p + 1, :, LANES:2 * LANES] = jnp.where(low, pltpu.roll(r, QK_ROPE_DIM, 1), 0.0).astype(BF16)

    ckv = ckv_ref[...]
    kn = _bdot(ckv, wkn_ref[...])
    v = _bdot(ckv, wv_ref[...])
    kr = kr_ref[...]
    ss_r = jnp.sum(jnp.where(low, kr * kr, 0.0), axis=-1, keepdims=True)
    base = rope(kr * gkr_ref[...])
    ones_col = jnp.where(lane == 0, 1.0, 0.0).astype(BF16)
    for hh in range(MLA_HEADS):
        nh = kn[:, hh * LANES:(hh + 1) * LANES]
        ssq = jnp.sum(nh * nh, axis=-1, keepdims=True) + ss_r
        s = lax.rsqrt(ssq * inv_dim + EPS)
        k_ref[hh, :, 0:LANES] = (nh * s * gkn_ref[...]).astype(BF16)
        k_ref[hh, :, LANES:2 * LANES] = jnp.where(low, base * s, 0.0).astype(BF16)
        v_ref[hh, :, 0:LANES] = v[:, hh * LANES:(hh + 1) * LANES].astype(BF16)
        v_ref[hh, :, LANES:2 * LANES] = ones_col


def _qkv(cq, ckv, kr2, cos_t, sin_t, wqn, wqr, gqn, gqr2, wkn, wv, gkn, gkr2, *, batch, seq, tm, with_q,
         with_rope):
    t = ckv.shape[0]
    nt = t // tm
    tiles_per_b = seq // tm

    def rows(n):
        return pl.BlockSpec((tm, n), lambda i: (i, 0))

    def const(arr):
        return pl.BlockSpec(arr.shape, lambda i: (0,) * arr.ndim)

    tab_spec = pl.BlockSpec((tm, LANES), lambda i: (i % tiles_per_b, 0))

    def head_out(width):
        return pl.BlockSpec((None, MLA_HEADS, tm, width), lambda i: (i // tiles_per_b, 0, i % tiles_per_b, 0))

    args, in_specs = [], []
    if with_q:
        args.append(cq)
        in_specs.append(rows(Q_LORA_RANK))
    args += [ckv, kr2]
    in_specs += [rows(KV_LORA_RANK), rows(LANES)]
    if with_rope:
        args += [cos_t, sin_t]
        in_specs += [tab_spec, tab_spec]
    if with_q:
        args += [wqn, wqr, gqn, gqr2]
        in_specs += [const(wqn), const(wqr), const(gqn), const(gqr2)]
    args += [wkn, wv, gkn, gkr2]
    in_specs += [const(wkn), const(wv), const(gkn), const(gkr2)]

    out_specs = [head_out(HEAD_PAD), head_out(V_PAD)]
    out_shape = [jax.ShapeDtypeStruct((batch, MLA_HEADS, seq, HEAD_PAD), BF16),
                 jax.ShapeDtypeStruct((batch, MLA_HEADS, seq, V_PAD), BF16)]
    if with_q:
        out_specs = [head_out(HEAD_PAD)] + out_specs
        out_shape = [jax.ShapeDtypeStruct((batch, MLA_HEADS, seq, HEAD_PAD), BF16)] + out_shape
    return pl.pallas_call(
        functools.partial(_qkv_kernel, with_q=with_q, with_rope=with_rope),
        grid=(nt,),
        in_specs=in_specs,
        out_specs=out_specs,
        out_shape=out_shape,
        compiler_params=_cparams(("arbitrary",)),
        name="qkv_x" if with_q else "kv_ctx",
    )(*args)


def _attn_kernel(bound_ref, q_ref, kx_ref, kc_ref, vx_ref, vc_ref, o_ref):
    q = q_ref[...]
    tq = q.shape[0]
    dn = (((1,), (1,)), ((), ()))
    chunks = [(kx_ref, vx_ref, c * ATTN_KEY_CHUNK, ATTN_KEY_CHUNK) for c in range(kx_ref.shape[0] // ATTN_KEY_CHUNK)]
    chunks.append((kc_ref, vc_ref, 0, kc_ref.shape[0]))

    def scores(k_ref, lo, n):
        return lax.dot_general(q, k_ref[lo:lo + n, :], dn, preferred_element_type=F32)

    def finish(acc):
        o_ref[...] = acc[:, :V_HEAD_DIM] / acc[:, V_HEAD_DIM:V_HEAD_DIM + 1]

    bound = bound_ref[0:1, 0:1]
    acc = jnp.zeros((tq, V_PAD), F32)
    for k_ref, v_ref, lo, n in chunks:
        p = jnp.exp2(scores(k_ref, lo, n) - bound).astype(BF16)
        acc = acc + _bdot(p, v_ref[lo:lo + n, :])
    finish(acc)

    row_sum_ok = jnp.min(acc[:, V_HEAD_DIM:V_HEAD_DIM + 1]) >= ATTN_MIN_ROW_SUM

    @pl.when(jnp.logical_not(row_sum_ok))
    def _():
        m = jnp.full((tq, 1), NEG_BIG, F32)
        acc2 = jnp.zeros((tq, V_PAD), F32)
        for k_ref, v_ref, lo, n in chunks:
            s = scores(k_ref, lo, n)
            m_new = jnp.maximum(m, jnp.max(s, axis=-1, keepdims=True))
            p = jnp.exp2(s - m_new).astype(BF16)
            acc2 = jnp.exp2(m - m_new) * acc2 + _bdot(p, v_ref[lo:lo + n, :])
            m = m_new
        finish(acc2)


def _attention(score_bound, q, kx, kc, vx, vc, *, tq):
    b, h, s, _ = q.shape
    lc = kc.shape[2]
    return pl.pallas_call(
        _attn_kernel,
        grid=(b, h, s // tq),
        in_specs=[pl.BlockSpec((1, LANES), lambda bi, hi, qi: (0, 0)),
                  pl.BlockSpec((None, None, tq, HEAD_PAD), lambda bi, hi, qi: (bi, hi, qi, 0)),
                  pl.BlockSpec((None, None, s, HEAD_PAD), lambda bi, hi, qi: (bi, hi, 0, 0)),
                  pl.BlockSpec((None, None, lc, HEAD_PAD), lambda bi, hi, qi: (bi, hi, 0, 0)),
                  pl.BlockSpec((None, None, s, V_PAD), lambda bi, hi, qi: (bi, hi, 0, 0)),
                  pl.BlockSpec((None, None, lc, V_PAD), lambda bi, hi, qi: (bi, hi, 0, 0))],
        out_specs=pl.BlockSpec((None, tq, V_HEAD_DIM), lambda bi, hi, qi: (bi, qi, hi)),
        out_shape=jax.ShapeDtypeStruct((b, s, h * V_HEAD_DIM), F32),
        compiler_params=_cparams(("arbitrary", "arbitrary", "arbitrary")),
        name="attention",
    )(score_bound, q, kx, kc, vx, vc)


def _seq_dft_kernel(ure_ref, uim_ref, r_ref, e_ref, t2_ref, o_ref, a_ref):
    s = pl.program_id(2)
    sub = SUBLANES
    n = 2 * GRID_W * sub
    cols = ure_ref.shape[-1]

    @pl.when(s < DFT_BLOCKS)
    def _():
        t = _bdot(r_ref[...].astype(BF16), e_ref[...])
        row = lax.broadcasted_iota(jnp.int32, (n, n), 0)
        col = lax.broadcasted_iota(jnp.int32, (n, n), 1)
        t = jnp.where((row % sub) == (col % sub), t, 0.0).astype(BF16)
        rhs = jnp.concatenate([ure_ref[...].reshape(GRID_W * sub, cols), uim_ref[...].reshape(GRID_W * sub, cols)],
                              axis=0).astype(BF16)
        a = _bdot(t, rhs)
        a_ref[:, :, pl.ds(pl.multiple_of(s * sub, sub), sub), :] = a.reshape(2, GRID_W, sub, cols)

    @pl.when(s >= DFT_BLOCKS)
    def _():
        k0 = pl.multiple_of((s - DFT_BLOCKS) * sub, sub)
        rhs = a_ref[:, pl.ds(k0, sub), :, :].reshape(2 * sub * GRID_W, cols).astype(BF16)
        y = _bdot(t2_ref[...].astype(BF16), rhs)
        o_ref[...] = y.reshape(GRID_W, sub, cols)


def _seq_dft_tables(n_seq):
    w, sub, nb = GRID_W, SUBLANES, DFT_BLOCKS
    ch = np.arange(nb).reshape(nb, 1, 1, 1)
    kb = np.arange(w).reshape(1, w, 1, 1)
    j = np.arange(sub).reshape(1, 1, sub, 1)
    r = np.arange(w).reshape(1, 1, 1, w)
    ang = (2.0 * np.pi / n_seq) * ((kb * (w * r + sub * ch + j)) % n_seq)
    c, s = np.cos(ang), np.sin(ang)
    rot = np.stack([np.stack([c, s], axis=3), np.stack([-s, c], axis=3)], axis=1)
    r1 = rot.reshape(nb, 2 * w * sub, 2 * w).astype(np.float32)
    expand = (np.arange(2 * w * sub)[None, :] // sub == np.arange(2 * w)[:, None]).astype(np.float32)
    ka = np.arange(w).reshape(w, 1)
    cp = np.arange(w).reshape(1, w)
    ang2 = (2.0 * np.pi / w) * ((ka * cp) % w)
    norm = 1.0 / math.sqrt(n_seq * FOURIER_GROUP_DIM)
    cs = np.stack([np.cos(ang2), np.sin(ang2)], axis=1) * norm
    eye = np.eye(sub)
    t2 = (cs[:, None, :, None, :] * eye[None, :, None, :, None]).reshape(w * sub, 2 * sub * w).astype(np.float32)
    return jnp.asarray(r1), jnp.asarray(expand).astype(BF16), jnp.asarray(t2)


def _seq_dft(u, batch, n_seq):
    assert n_seq == GRID_W * GRID_W
    w, sub, nb = GRID_W, SUBLANES, DFT_BLOCKS
    r1, expand, t2 = _seq_dft_tables(n_seq)
    halves = 2
    cols = FOURIER_DIM // halves
    u5 = u.reshape(batch, w, nb, sub, 2 * FOURIER_DIM)

    def u_spec(part):
        return pl.BlockSpec((None, w, None, sub, cols),
                            lambda b, h, s: (b, 0, jnp.minimum(s, nb - 1), 0, part * halves + h))

    y = pl.pallas_call(
        _seq_dft_kernel,
        grid=(batch, halves, 2 * nb),
        in_specs=[u_spec(0), u_spec(1),
                  pl.BlockSpec((None, 2 * w * sub, 2 * w), lambda b, h, s: (jnp.minimum(s, nb - 1), 0, 0)),
                  pl.BlockSpec((2 * w, 2 * w * sub), lambda b, h, s: (0, 0)),
                  pl.BlockSpec((w * sub, 2 * sub * w), lambda b, h, s: (0, 0))],
        out_specs=pl.BlockSpec((None, w, None, sub, cols), lambda b, h, s: (b, 0, jnp.maximum(s - nb, 0), 0, h)),
        out_shape=jax.ShapeDtypeStruct((batch, w, nb, sub, FOURIER_DIM), F32),
        scratch_shapes=[pltpu.VMEM((2, w, w, cols), F32)],
        compiler_params=_cparams(("arbitrary", "arbitrary", "arbitrary")),
        name="seq_dft",
    )(u5, u5, r1, expand, t2)
    return y.reshape(batch * n_seq, FOURIER_DIM)


def _out_router_kernel(x_ref, four_ref, attn_ref, gt1_ref, sh2_ref, sc2_ref, gf_ref, ga_ref, g2_ref,
                       wo_ref, wrh_ref, wrl_ref, br_ref,
                       xnew_ref, hm_ref, ri_ref, rw_ref, cnt_ref, carry_ref, *, mod_row):
    i = pl.program_id(0)
    tm = x_ref.shape[0]
    row = pl.ds(mod_row(i), 1)

    @pl.when(i == 0)
    def _():
        carry_ref[...] = jnp.zeros_like(carry_ref)

    def norm(v, g):
        return (v * lax.rsqrt(jnp.mean(v * v, axis=-1, keepdims=True) + EPS) * g).astype(BF16)

    mix = (_bdot(norm(four_ref[...], gf_ref[...]), wo_ref[0:FOURIER_DIM, :])
           + _bdot(norm(attn_ref[...], ga_ref[...]), wo_ref[FOURIER_DIM:FOURIER_DIM + MLA_DIM, :]))
    xn = x_ref[...] + gt1_ref[row, :] * mix
    xnew_ref[...] = xn
    ms = jnp.mean(xn * xn, axis=-1, keepdims=True)
    hm = xn * lax.rsqrt(ms + EPS) * (g2_ref[...] * (1.0 + sc2_ref[row, :])) + sh2_ref[row, :]
    hm_ref[...] = _pack_halves(hm)

    hm_hi = hm.astype(BF16)
    hm_lo = (hm - hm_hi.astype(F32)).astype(BF16)
    logits = _bdot(hm_hi, wrh_ref[...]) + _bdot(hm_lo, wrh_ref[...]) + _bdot(hm_hi, wrl_ref[...]) + br_ref[...]
    lane = lax.broadcasted_iota(jnp.int32, (tm, LANES), 1)
    lanef = lane.astype(F32)
    far = 1e9

    lg = jnp.where(lane < N_GROUPS, logits, NEG_BIG)
    m1 = jnp.max(lg, axis=-1, keepdims=True)
    g_p = 1.0 / jnp.sum(jnp.exp(lg - m1), axis=-1, keepdims=True)
    gidx = jnp.min(jnp.where(lg >= m1, lanef, far), axis=-1, keepdims=True)
    lo = N_GROUPS + EXPERTS_PER_GROUP * gidx
    in_group = jnp.where(lanef >= lo, jnp.where(lanef < lo + EXPERTS_PER_GROUP, 1.0, 0.0), 0.0) > 0.5
    le = jnp.where(in_group, logits, NEG_BIG)
    m2 = jnp.max(le, axis=-1, keepdims=True)
    idx1 = jnp.min(jnp.where(le >= m2, lanef, far), axis=-1, keepdims=True)
    le2 = jnp.where(lanef == idx1, NEG_BIG, le)
    m3 = jnp.max(le2, axis=-1, keepdims=True)
    idx2 = jnp.min(jnp.where(le2 >= m3, lanef, far), axis=-1, keepdims=True)
    t = jnp.exp(m3 - m2)
    p1 = 1.0 / (1.0 + t)
    p2 = t / (1.0 + t)
    e1 = idx1 - N_GROUPS
    e2 = idx2 - N_GROUPS

    oh1 = jnp.where(lanef == e1, 1.0, 0.0)
    oh2 = jnp.where(lanef == e2, 1.0, 0.0)
    ohs = oh1 + oh2
    row = lax.broadcasted_iota(jnp.int32, (tm, tm), 0)
    col = lax.broadcasted_iota(jnp.int32, (tm, tm), 1)
    tri = jnp.where(row > col, 1.0, 0.0).astype(BF16)
    before = _bdot(tri, ohs.astype(BF16)) + carry_ref[...]
    rank1 = jnp.sum(oh1 * before, axis=-1, keepdims=True)
    rank2 = jnp.sum(oh2 * before, axis=-1, keepdims=True)
    carry = carry_ref[...] + jnp.sum(ohs, axis=0, keepdims=True)
    carry_ref[...] = carry
    cnt_ref[...] = jnp.broadcast_to(carry, cnt_ref.shape)

    ri = jnp.where(lane == 0, e1, jnp.where(lane == 1, e2, jnp.where(lane == 2, rank1, jnp.where(lane == 3, rank2, 0.0))))
    ri_ref[...] = ri.astype(jnp.int32)
    rw_ref[...] = jnp.where(lane == 0, g_p * p1, jnp.where(lane == 1, g_p * p2, 0.0))


def _out_router(x2, four, attn, mods, gf, ga, g2, wo, wrh, wrl, br, *, seq, tm):
    t, d = x2.shape
    nt = t // tm
    tiles_per_b = seq // tm

    def rows(n):
        return pl.BlockSpec((tm, n), lambda i: (i, 0))

    def const(arr):
        return pl.BlockSpec(arr.shape, lambda i: (0,) * arr.ndim, pipeline_mode=pl.Buffered(1))

    return pl.pallas_call(
        functools.partial(_out_router_kernel, mod_row=lambda i: i // tiles_per_b),
        grid=(nt,),
        in_specs=[rows(d), rows(FOURIER_DIM), rows(MLA_DIM), _mod_spec(mods, 2, d), _mod_spec(mods, 3, d),
                  _mod_spec(mods, 4, d),
                  const(gf), const(ga), const(g2), const(wo), const(wrh), const(wrl), const(br)],
        out_specs=[rows(d), rows(d // 2), rows(LANES), rows(LANES), pl.BlockSpec((8, LANES), lambda i: (0, 0))],
        out_shape=[jax.ShapeDtypeStruct((t, d), F32), jax.ShapeDtypeStruct((t, d // 2), jnp.uint32),
                   jax.ShapeDtypeStruct((t, LANES), jnp.int32), jax.ShapeDtypeStruct((t, LANES), F32),
                   jax.ShapeDtypeStruct((8, LANES), F32)],
        scratch_shapes=[pltpu.VMEM((1, LANES), F32)],
        compiler_params=_cparams(("arbitrary",)),
        name="out_proj_router",
    )(x2, four, attn, mods, mods, mods, gf, ga, g2, wo, wrh, wrl, br)


def _dispatch_kernel(dest, seg_last, seg_blocks, used_blocks, hm_ref, xs_hbm, zbuf, sem, zsem):
    i = pl.program_id(0)
    tm = hm_ref.shape[0]

    @pl.when(i == 0)
    def _():
        zbuf[...] = jnp.zeros(zbuf.shape, zbuf.dtype)
        n_blocks = xs_hbm.shape[0] // MOE_BLOCK

        def zero_copy(blk):
            r0 = pl.multiple_of(blk * MOE_BLOCK, MOE_BLOCK)
            return pltpu.make_async_copy(zbuf, xs_hbm.at[pl.ds(r0, MOE_BLOCK)], zsem.at[0])

        def seg_start(e, carry):
            @pl.when(seg_blocks[e] > 0)
            def _():
                zero_copy(seg_last[e]).start()
            return carry

        def seg_wait(e, carry):
            @pl.when(seg_blocks[e] > 0)
            def _():
                zero_copy(seg_last[e]).wait()
            return carry

        def tail_start(blk, carry):
            zero_copy(blk).start()
            return carry

        def tail_wait(blk, carry):
            zero_copy(blk).wait()
            return carry

        lax.fori_loop(0, N_EXPERTS, seg_start, 0)
        lax.fori_loop(used_blocks[0], n_blocks, tail_start, 0)
        lax.fori_loop(0, N_EXPERTS, seg_wait, 0)
        lax.fori_loop(used_blocks[0], n_blocks, tail_wait, 0)

    def body(r8, carry):
        for u in range(GATHER_UNROLL):
            r = r8 * GATHER_UNROLL + u
            for k in range(2):
                pltpu.make_async_copy(hm_ref.at[pl.ds(r, 1)], xs_hbm.at[pl.ds(dest[2 * (i * tm + r) + k], 1)],
                                      sem.at[0]).start(priority=k)
        return carry

    lax.fori_loop(0, tm // GATHER_UNROLL, body, 0)
    for k in range(2):
        pltpu.make_async_copy(hm_ref, xs_hbm.at[pl.ds(0, tm)], sem.at[0]).wait()


def _dispatch(dest, seg_last, seg_blocks, used_blocks, hm, *, n_rows, tm):
    t, w = hm.shape
    grid_spec = pltpu.PrefetchScalarGridSpec(
        num_scalar_prefetch=4,
        grid=(t // tm,),
        in_specs=[pl.BlockSpec((tm, w), lambda i, *_: (i, 0))],
        out_specs=pl.BlockSpec(memory_space=pl.ANY),
        scratch_shapes=[pltpu.VMEM((MOE_BLOCK, w), hm.dtype), pltpu.SemaphoreType.DMA((1,)),
                        pltpu.SemaphoreType.DMA((1,))],
    )
    return pl.pallas_call(
        _dispatch_kernel,
        grid_spec=grid_spec,
        out_shape=jax.ShapeDtypeStruct((n_rows, w), hm.dtype),
        compiler_params=_cparams(("arbitrary",)),
        name="moe_dispatch",
    )(dest, seg_last, seg_blocks, used_blocks, hm)


def _moe_kernel(item_e, item_blk0, item_nblk, used_blocks,
                xs_hbm, wg_hbm, wu_hbm, wd_hbm, y_hbm,
                xg, xb, gs, ab, yp, gu_buf, dn_buf, gsem, osem, gusem, dnsem):
    i = pl.program_id(0)
    j = pl.program_id(1)
    n_items = pl.num_programs(0)
    nj = pl.num_programs(1)
    slot = i % 2
    nblk = item_nblk[i]

    def weight_copy(it, ph):
        if ph == 2:
            ws = it % DN_SLOTS
            return pltpu.make_async_copy(wd_hbm.at[item_e[it]], dn_buf.at[ws], dnsem.at[ws])
        ws = (2 * it + ph) % GU_SLOTS
        return pltpu.make_async_copy((wg_hbm, wu_hbm)[ph].at[item_e[it]], gu_buf.at[ws], gusem.at[ws])

    def start_weight(it, ph):
        it_c = jnp.minimum(it, n_items - 1)

        @pl.when(jnp.logical_and(it < n_items, item_nblk[it_c] > 0))
        def _():
            weight_copy(it_c, ph).start(priority=WEIGHT_DMA_PRIORITY)

    def x_copy(it, sl, m):
        r0 = pl.multiple_of(item_blk0[it] * MOE_BLOCK, MOE_BLOCK)
        return pltpu.make_async_copy(xs_hbm.at[pl.ds(r0, m)], xg.at[sl, pl.ds(0, m)], gsem.at[sl])

    def issue_gather(it, sl):
        for nb in range(1, ITEM_BLOCKS + 1):
            @pl.when(item_nblk[it] == nb)
            def _():
                x_copy(it, sl, nb * MOE_BLOCK).start(priority=ROW_DMA_PRIORITY)

    def wait_gather(it, sl):
        for nb in range(1, ITEM_BLOCKS + 1):
            @pl.when(item_nblk[it] == nb)
            def _():
                x_copy(it, sl, nb * MOE_BLOCK).wait()

    def out_copy(it, m):
        r0 = pl.multiple_of(item_blk0[it] * MOE_BLOCK, MOE_BLOCK)
        return pltpu.make_async_copy(yp.at[pl.ds(0, m)], y_hbm.at[pl.ds(r0, m)], osem.at[0])

    def wait_out(it):
        for nb in range(1, ITEM_BLOCKS + 1):
            @pl.when(item_nblk[it] == nb)
            def _():
                out_copy(it, nb * MOE_BLOCK).wait()

    @pl.when(j == 0)
    def _():
        @pl.when(i == 0)
        def _():
            start_weight(0, 0)
            start_weight(0, 1)
            start_weight(1, 0)
            start_weight(0, 2)
            issue_gather(0, 0)

        start_weight(i + 1, 2)
        wait_gather(i, slot)

        @pl.when(i + 1 < n_items)
        def _():
            issue_gather(i + 1, 1 - slot)

    @pl.when(j == 1)
    def _():
        start_weight(i + 1, 1)

    @pl.when(j == 2)
    def _():
        start_weight(i + 2, 0)

        @pl.when(i > 0)
        def _():
            wait_out(i - 1)

    for ph in range(3):
        @pl.when(jnp.logical_and(j == ph, nblk > 0))
        def _():
            weight_copy(i, ph).wait()

    for nb in range(1, ITEM_BLOCKS + 1):
        m = nb * MOE_BLOCK

        @pl.when(jnp.logical_and(nblk == nb, j == 0))
        def _():
            hi, lo = _unpack_halves(xg[slot, 0:m, :])
            half = hi.shape[1]
            xb[0:m, 0:half] = hi.astype(BF16)
            xb[0:m, half:2 * half] = lo.astype(BF16)
            gs[0:m, :] = _bdot(xb[0:m, :], gu_buf[(2 * i) % GU_SLOTS].astype(BF16))

        @pl.when(jnp.logical_and(nblk == nb, j == 1))
        def _():
            g = gs[0:m, :]
            u = _bdot(xb[0:m, :], gu_buf[(2 * i + 1) % GU_SLOTS].astype(BF16))
            ab[0:m, :] = (g * jax.nn.sigmoid(g) * u).astype(BF16)

        @pl.when(jnp.logical_and(nblk == nb, j == 2))
        def _():
            yp[0:m, :] = _pack_halves(_bdot(ab[0:m, :], dn_buf[i % DN_SLOTS].astype(BF16)))
            out_copy(i, m).start()

    @pl.when(jnp.logical_and(i == n_items - 1, j == nj - 1))
    def _():
        wait_out(i)
        n_blocks = y_hbm.shape[0] // MOE_BLOCK
        yp[0:MOE_BLOCK, :] = jnp.zeros((MOE_BLOCK, yp.shape[1]), jnp.uint32)

        def tail_copy(blk):
            r0 = pl.multiple_of(blk * MOE_BLOCK, MOE_BLOCK)
            return pltpu.make_async_copy(yp.at[pl.ds(0, MOE_BLOCK)], y_hbm.at[pl.ds(r0, MOE_BLOCK)], osem.at[0])

        def start_body(blk, carry):
            tail_copy(blk).start()
            return carry

        def wait_body(blk, carry):
            tail_copy(blk).wait()
            return carry

        lax.fori_loop(used_blocks[0], n_blocks, start_body, 0)
        lax.fori_loop(used_blocks[0], n_blocks, wait_body, 0)


def _moe(xs, w_gate, w_up, w_down, item_e, item_blk0, item_nblk, used_blocks, *, n_rows):
    d, de = w_gate.shape[1], w_gate.shape[2]
    n_items = item_e.shape[0]
    nj = 3
    rows = ITEM_BLOCKS * MOE_BLOCK
    any_spec = pl.BlockSpec(memory_space=pl.ANY)
    grid_spec = pltpu.PrefetchScalarGridSpec(
        num_scalar_prefetch=4,
        grid=(n_items, nj),
        in_specs=[any_spec, any_spec, any_spec, any_spec],
        out_specs=any_spec,
        scratch_shapes=[pltpu.VMEM((2, rows, d // 2), jnp.uint32),
                        pltpu.VMEM((rows, d), BF16),
                        pltpu.VMEM((rows, de), F32),
                        pltpu.VMEM((rows, de), BF16),
                        pltpu.VMEM((rows, d // 2), jnp.uint32),
                        pltpu.VMEM((GU_SLOTS, d, de), F32),
                        pltpu.VMEM((DN_SLOTS, de, d), F32),
                        pltpu.SemaphoreType.DMA((2,)),
                        pltpu.SemaphoreType.DMA((1,)),
                        pltpu.SemaphoreType.DMA((GU_SLOTS,)),
                        pltpu.SemaphoreType.DMA((DN_SLOTS,))],
    )
    return pl.pallas_call(
        _moe_kernel,
        grid_spec=grid_spec,
        out_shape=jax.ShapeDtypeStruct((n_rows, d // 2), jnp.uint32),
        compiler_params=_cparams(("arbitrary", "arbitrary")),
        name="moe_experts",
    )(item_e, item_blk0, item_nblk, used_blocks, xs, w_gate, w_up, w_down)


def _combine_kernel(dest, x_ref, gt2_ref, rw_ref, y_hbm, o_ref, ybuf, sem, *, mod_row):
    i = pl.program_id(0)
    n = pl.num_programs(0)
    tm = x_ref.shape[0]
    slot = i % 2

    def issue(it, sl):
        base = it * tm

        def body(r4, carry):
            for rr in range(GATHER_UNROLL // 2):
                r = r4 * (GATHER_UNROLL // 2) + rr
                for k in range(2):
                    pltpu.make_async_copy(y_hbm.at[pl.ds(dest[2 * (base + r) + k], 1)], ybuf.at[sl, k, pl.ds(r, 1)],
                                          sem.at[sl]).start(priority=k)
            return carry

        lax.fori_loop(0, tm // (GATHER_UNROLL // 2), body, 0)

    @pl.when(i == 0)
    def _():
        issue(0, 0)

    for k in range(2):
        pltpu.make_async_copy(y_hbm.at[pl.ds(0, tm)], ybuf.at[slot, k], sem.at[slot]).wait()

    @pl.when(i + 1 < n)
    def _():
        issue(i + 1, 1 - slot)

    w = rw_ref[...]
    gate = gt2_ref[pl.ds(mod_row(i), 1), :]
    hi0, lo0 = _unpack_halves(ybuf[slot, 0])
    hi1, lo1 = _unpack_halves(ybuf[slot, 1])
    half = hi0.shape[1]
    o_ref[:, 0:half] = x_ref[:, 0:half] + gate[:, 0:half] * (w[:, 0:1] * hi0 + w[:, 1:2] * hi1)
    o_ref[:, half:2 * half] = (x_ref[:, half:2 * half]
                               + gate[:, half:2 * half] * (w[:, 0:1] * lo0 + w[:, 1:2] * lo1))


def _combine(dest, xnew, mods, rw, y, *, seq, tm):
    t, d = xnew.shape
    tiles_per_b = seq // tm
    grid_spec = pltpu.PrefetchScalarGridSpec(
        num_scalar_prefetch=1,
        grid=(t // tm,),
        in_specs=[pl.BlockSpec((tm, d), lambda i, ds: (i, 0)),
                  _mod_spec(mods, 5, d),
                  pl.BlockSpec((tm, LANES), lambda i, ds: (i, 0)),
                  pl.BlockSpec(memory_space=pl.ANY)],
        out_specs=pl.BlockSpec((tm, d), lambda i, ds: (i, 0)),
        scratch_shapes=[pltpu.VMEM((2, 2, tm, d // 2), jnp.uint32), pltpu.SemaphoreType.DMA((2,))],
    )
    return pl.pallas_call(
        functools.partial(_combine_kernel, mod_row=lambda i: i // tiles_per_b),
        grid_spec=grid_spec,
        out_shape=jax.ShapeDtypeStruct((t, d), F32),
        compiler_params=_cparams(("arbitrary",)),
        name="moe_combine",
    )(dest, xnew, mods, rw, y)


def _rope_tables(n_tokens):
    rows = n_tokens // GRID_W
    row = jnp.repeat(jnp.arange(rows, dtype=jnp.int32), GRID_W).astype(F32)
    col = jnp.tile(jnp.arange(GRID_W, dtype=jnp.int32), rows).astype(F32)
    n_freq = QK_ROPE_DIM // 4
    inv = ROPE_THETA ** (-jnp.arange(n_freq, dtype=F32) / n_freq)
    ar = row[:, None] * inv[None, :]
    ac = col[:, None] * inv[None, :]
    cr, sr, cc, sc = jnp.cos(ar), jnp.sin(ar), jnp.cos(ac), jnp.sin(ac)
    cos64 = jnp.concatenate([cr, cr, cc, cc], axis=-1)
    sin64 = jnp.concatenate([-sr, sr, -sc, sc], axis=-1)
    return jnp.tile(cos64, (1, 2)), jnp.tile(sin64, (1, 2))


def _channel_dft_table():
    c = np.arange(FOURIER_GROUP_DIM).reshape(-1, 1)
    k = np.arange(FOURIER_GROUP_DIM).reshape(1, -1)
    ang = (2.0 * np.pi / FOURIER_GROUP_DIM) * ((c * k) % FOURIER_GROUP_DIM)
    return jnp.asarray(np.concatenate([np.cos(ang), -np.sin(ang)], axis=1).astype(np.float32)).astype(BF16)


def _split_heads(w, widths):
    k = w.shape[0]
    wh = w.reshape(k, MLA_HEADS, sum(widths))
    outs, off = [], 0
    for wd in widths:
        outs.append(wh[:, :, off:off + wd].reshape(k, MLA_HEADS * wd))
        off += wd
    return outs


def kernel(x, c, ctx, c_ctx, w_ada, b_ada, g_norm1, g_norm2, w_in, g_q_a, g_kv_a, w_uq, w_ukv, g_qk_q, g_qk_k,
           g_out_four, g_out_attn, w_out, w_router_group, b_router_group, w_router_expert, b_router_expert,
           w_gate, w_up, w_down):
    b, s, d = x.shape
    lc = ctx.shape[1]
    t = b * s
    layer_params = (w_ada, b_ada, g_norm1, g_norm2, w_in, g_q_a, g_kv_a, w_uq, w_ukv, g_qk_q, g_qk_k, g_out_four,
                    g_out_attn, w_out, w_router_group, b_router_group, w_router_expert, b_router_expert,
                    w_gate, w_up, w_down)
    assert all(p.shape[0] == 1 for p in layer_params), "single-layer block"
    (w_ada, b_ada, g_norm1, g_norm2, w_in, g_q_a, g_kv_a, w_uq, w_ukv, g_qk_q, g_qk_k, g_out_four,
     g_out_attn, w_out, w_router_group, b_router_group, w_router_expert, b_router_expert,
     w_gate, w_up, w_down) = [p.reshape(p.shape[1:]) for p in layer_params]

    cond8 = jnp.concatenate([c, c_ctx[None, :], jnp.zeros((8 - b - 1, d), F32)], axis=0)
    mods = _ada_mod(cond8, w_ada, b_ada)

    w_all = jnp.concatenate([w_in, w_in[:, ROPE_OFF:]], axis=1).astype(BF16)
    dc = _channel_dft_table()
    wqn, wqr = [w.astype(BF16) for w in _split_heads(w_uq, (QK_NOPE_DIM, QK_ROPE_DIM))]
    wkn, wv = [w.astype(BF16) for w in _split_heads(w_ukv, (QK_NOPE_DIM, V_HEAD_DIM))]
    gqn = g_qk_q[:QK_NOPE_DIM].reshape(1, -1)
    gqr2 = jnp.tile(g_qk_q[QK_NOPE_DIM:], 2).reshape(1, -1)
    gkn = g_qk_k[:QK_NOPE_DIM].reshape(1, -1)
    gkr2 = jnp.tile(g_qk_k[QK_NOPE_DIM:], 2).reshape(1, -1)
    g1 = g_norm1.reshape(1, d)
    gq = g_q_a.reshape(1, -1)
    gkv = g_kv_a.reshape(1, -1)

    x2 = x.reshape(t, d)
    u, cq, ckv, kr2 = _in_proj(x2, mods, 0, s, g1, gq, gkv, w_all, dc, with_q=True, tm=IN_PROJ_ROWS)
    ckv_c, kr2_c = _in_proj(ctx.reshape(b * lc, d), mods, b, b * lc, g1, gq, gkv, w_all, dc, with_q=False, tm=lc)

    cos_t, sin_t = _rope_tables(s)
    q, kx, vx = _qkv(cq, ckv, kr2, cos_t, sin_t, wqn, wqr, gqn, gqr2, wkn, wv, gkn, gkr2,
                     batch=b, seq=s, tm=QKV_ROWS, with_q=True, with_rope=True)
    kc, vc = _qkv(None, ckv_c, kr2_c, None, None, None, None, None, None, wkn, wv, gkn, gkr2,
                  batch=b, seq=lc, tm=lc, with_q=False, with_rope=False)

    score_bound = (QK_HEAD_DIM ** 0.5 * LOG2_E * ATTN_BOUND_MARGIN) * jnp.max(jnp.abs(g_qk_q)) * jnp.max(jnp.abs(g_qk_k))
    score_bound = jnp.full((1, LANES), score_bound, F32)
    attn = _attention(score_bound, q, kx, kc, vx, vc, tq=ATTN_Q_ROWS).reshape(t, MLA_DIM)
    four = _seq_dft(u, b, s)

    wo = w_out.astype(BF16)
    n_route = N_GROUPS + N_EXPERTS
    wrt = jnp.concatenate([w_router_group, w_router_expert, jnp.zeros((d, LANES - n_route), F32)], axis=1)
    wrt_hi = wrt.astype(BF16)
    wrt_lo = (wrt - wrt_hi.astype(F32)).astype(BF16)
    brt = jnp.concatenate([b_router_group, b_router_expert, jnp.zeros((LANES - n_route,), F32)]).reshape(1, -1)
    xnew, hm, ri, rw, cnt = _out_router(x2, four, attn, mods, g_out_four.reshape(1, -1), g_out_attn.reshape(1, -1),
                                        g_norm2.reshape(1, d), wo, wrt_hi, wrt_lo, brt, seq=s, tm=OUT_ROUTER_ROWS)

    counts = cnt[0, :N_EXPERTS].astype(jnp.int32)
    nblk_e = (counts + MOE_BLOCK - 1) // MOE_BLOCK
    blk_end = jnp.cumsum(nblk_e)
    blk_start = blk_end - nblk_e
    n_slots = t * 2
    n_blocks = -(-(n_slots + N_EXPERTS * (MOE_BLOCK - 1)) // MOE_BLOCK)
    n_rows = n_blocks * MOE_BLOCK
    e12 = ri[:, 0:2]
    seg_start = jnp.sum(jnp.where(e12[:, :, None] == jnp.arange(N_EXPERTS, dtype=jnp.int32), blk_start, 0), axis=-1)
    dest = (seg_start * MOE_BLOCK + ri[:, 2:4]).reshape(-1).astype(jnp.int32)
    items_e = (nblk_e + ITEM_BLOCKS - 1) // ITEM_BLOCKS
    item_end = jnp.cumsum(items_e)
    n_items = (n_blocks + (ITEM_BLOCKS - 1) * N_EXPERTS) // ITEM_BLOCKS
    idx = jnp.arange(n_items, dtype=jnp.int32)
    total = item_end[-1]
    idx_c = jnp.minimum(idx, total - 1)
    ie = jnp.minimum(jnp.sum(idx_c[:, None] >= item_end[None, :], axis=1), N_EXPERTS - 1).astype(jnp.int32)
    local = idx_c - (item_end - items_e)[ie]
    item_blk0 = (blk_start[ie] + ITEM_BLOCKS * local).astype(jnp.int32)
    item_nblk = jnp.where(idx < total, jnp.clip(nblk_e[ie] - ITEM_BLOCKS * local, 0, ITEM_BLOCKS), 0).astype(jnp.int32)

    used_blocks = blk_end[-1:].astype(jnp.int32)
    seg_last = jnp.maximum(blk_end - 1, 0).astype(jnp.int32)
    xs = _dispatch(dest, seg_last, nblk_e.astype(jnp.int32), used_blocks, hm, n_rows=n_rows, tm=DISPATCH_ROWS)
    y = _moe(xs, w_gate, w_up, w_down, ie, item_blk0, item_nblk, used_blocks, n_rows=n_rows)
    out = _combine(dest, xnew, mods, rw, y, seq=s, tm=COMBINE_ROWS)
    return out.reshape(b, s, d)
```

```python
import functools
import math

import numpy as np
import jax
import jax.numpy as jnp
from jax import lax
from jax.experimental import pallas as pl
from jax.experimental.pallas import tpu as pltpu

F32 = jnp.float32
BF16 = jnp.bfloat16

D_MODEL = 2048
GRID_W = 64
EPS = 1e-6
N_MOD = 6
N_FOURIER_GROUPS = 4
FOURIER_GROUP_DIM = 256
FOURIER_DIM = 1024
MLA_HEADS = 8
QK_NOPE_DIM = 128
QK_ROPE_DIM = 64
QK_HEAD_DIM = 192
V_HEAD_DIM = 128
Q_LORA_RANK = 768
KV_LORA_RANK = 512
MLA_DIM = 1024
ROPE_THETA = 10000.0
Q_OFF = FOURIER_DIM
KV_OFF = Q_OFF + Q_LORA_RANK
ROPE_OFF = KV_OFF + KV_LORA_RANK
N_GROUPS = 8
EXPERTS_PER_GROUP = 8
N_EXPERTS = 64
D_EXPERT = 768

ROT_HALF = QK_ROPE_DIM // 4
ROT_BLOCK = 2 * ROT_HALF

ADA_COLS = 1024
IN_PROJ_ROWS = 512
QKV_ROWS = 512
ATTN_Q_ROWS = 2048
OUT_ROUTER_ROWS = 512
COMBINE_ROWS = 512
DISPATCH_ROWS = 1024

LANES = 128
SUBLANES = 8
DFT_BLOCKS = GRID_W // SUBLANES
HEAD_PAD = 256
V_PAD = 256
MOE_BLOCK = 64
ITEM_BLOCKS = 8
GATHER_UNROLL = 8
GU_SLOTS = 3
DN_SLOTS = 2
ATTN_KEY_CHUNK = 256
ATTN_BOUND_MARGIN = 1.02
ATTN_MIN_ROW_SUM = 1e-30
LOG2_E = math.log2(math.e)
ROW_DMA_PRIORITY = 0
WEIGHT_DMA_PRIORITY = 1
VMEM_LIMIT = 56 * 1024 * 1024
NEG_BIG = -1e30


def _cparams(sem):
    return pltpu.CompilerParams(dimension_semantics=sem, vmem_limit_bytes=VMEM_LIMIT)


def _bdot(a, b):
    return jnp.dot(a, b, preferred_element_type=F32)


def _pack_halves(v):
    n = v.shape[1] // 2
    hi = pltpu.bitcast(v[:, :n].astype(BF16).astype(F32), jnp.uint32)
    lo = pltpu.bitcast(v[:, n:].astype(BF16).astype(F32), jnp.uint32)
    return hi | (lo >> 16)


def _unpack_halves(u):
    hi = pltpu.bitcast(u & jnp.uint32(0xFFFF0000), F32)
    lo = pltpu.bitcast(u << 16, F32)
    return hi, lo


def _ada_kernel(c_ref, w_ref, b_ref, o_ref):
    c = c_ref[...]
    s = (c * jax.nn.sigmoid(c)).astype(BF16)
    o_ref[...] = _bdot(s, w_ref[...].astype(BF16)) + b_ref[...]


def _ada_mod(cond8, w_ada, b_ada):
    d, n = w_ada.shape
    tn = ADA_COLS
    return pl.pallas_call(
        _ada_kernel,
        grid=(n // tn,),
        in_specs=[pl.BlockSpec((8, d), lambda i: (0, 0)),
                  pl.BlockSpec((d, tn), lambda i: (0, i)),
                  pl.BlockSpec((1, tn), lambda i: (0, i))],
        out_specs=pl.BlockSpec((8, tn), lambda i: (0, i)),
        out_shape=jax.ShapeDtypeStruct((8, n), F32),
        compiler_params=_cparams(("arbitrary",)),
        name="ada_mod",
    )(cond8, w_ada, b_ada.reshape(1, n))


def _in_proj_kernel(x_ref, sh_ref, sc_ref, g_ref, gq_ref, gkv_ref, w_ref, dc_ref, *out_refs, with_q, mod_row):
    row = pl.ds(mod_row(pl.program_id(0)), 1)
    x = x_ref[...]
    ms = jnp.mean(x * x, axis=-1, keepdims=True)
    a = g_ref[...] * (1.0 + sc_ref[row, :])
    h = (x * lax.rsqrt(ms + EPS) * a + sh_ref[row, :]).astype(BF16)
    if with_q:
        u_ref, cq_ref, ckv_ref, kr_ref = out_refs
        f = _bdot(h, w_ref[:, 0:Q_OFF]).astype(BF16)
        dc = dc_ref[...]
        for g in range(N_FOURIER_GROUPS):
            lo = g * FOURIER_GROUP_DIM
            ug = _bdot(f[:, lo:lo + FOURIER_GROUP_DIM], dc)
            u_ref[:, lo:lo + FOURIER_GROUP_DIM] = ug[:, :FOURIER_GROUP_DIM]
            u_ref[:, FOURIER_DIM + lo:FOURIER_DIM + lo + FOURIER_GROUP_DIM] = ug[:, FOURIER_GROUP_DIM:]
        pq = _bdot(h, w_ref[:, Q_OFF:KV_OFF])
        msq = jnp.mean(pq * pq, axis=-1, keepdims=True)
        cq_ref[...] = (pq * lax.rsqrt(msq + EPS) * gq_ref[...]).astype(BF16)
    else:
        ckv_ref, kr_ref = out_refs
    pkv = _bdot(h, w_ref[:, KV_OFF:ROPE_OFF])
    mskv = jnp.mean(pkv * pkv, axis=-1, keepdims=True)
    ckv_ref[...] = (pkv * lax.rsqrt(mskv + EPS) * gkv_ref[...]).astype(BF16)
    kr_ref[...] = _bdot(h, w_ref[:, ROPE_OFF:ROPE_OFF + LANES])


def _mod_spec(mods, k, d):
    return pl.BlockSpec((mods.shape[0], d), lambda i, *_: (0, k))


def _in_proj(x2, mods, first_row, rows_per_mod, g1, gq, gkv, w_all, dc, *, with_q, tm):
    t, d = x2.shape
    nt = t // tm
    tiles_per_mod = rows_per_mod // tm

    def const(shape):
        return pl.BlockSpec(shape, lambda i: (0,) * len(shape))

    in_specs = [pl.BlockSpec((tm, d), lambda i: (i, 0)), _mod_spec(mods, 0, d), _mod_spec(mods, 1, d),
                const((1, d)), const((1, Q_LORA_RANK)), const((1, KV_LORA_RANK)),
                const(w_all.shape), const(dc.shape)]

    def rows(n):
        return pl.BlockSpec((tm, n), lambda i: (i, 0))

    out_specs = [rows(KV_LORA_RANK), rows(LANES)]
    out_shape = [jax.ShapeDtypeStruct((t, KV_LORA_RANK), BF16), jax.ShapeDtypeStruct((t, LANES), F32)]
    if with_q:
        out_specs = [rows(2 * FOURIER_DIM), rows(Q_LORA_RANK)] + out_specs
        out_shape = [jax.ShapeDtypeStruct((t, 2 * FOURIER_DIM), F32),
                     jax.ShapeDtypeStruct((t, Q_LORA_RANK), BF16)] + out_shape
    return pl.pallas_call(
        functools.partial(_in_proj_kernel, with_q=with_q, mod_row=lambda i: first_row + i // tiles_per_mod),
        grid=(nt,),
        in_specs=in_specs,
        out_specs=out_specs,
        out_shape=out_shape,
        compiler_params=_cparams(("arbitrary",)),
        name="in_proj_x" if with_q else "in_proj_ctx",
    )(x2, mods, mods, g1, gq, gkv, w_all, dc)


def _swap_halves(y, first_half):
    return jnp.where(first_half, pltpu.roll(y, LANES - ROT_HALF, 1), pltpu.roll(y, ROT_HALF, 1))


def _qkv_kernel(*refs, with_q, with_rope):
    it = iter(refs)
    if with_q:
        cq_ref = next(it)
    ckv_ref = next(it)
    kr_ref = next(it)
    if with_rope:
        cos_ref = next(it)
        sin_ref = next(it)
    if with_q:
        wqn_ref = next(it)
        wqr_ref = next(it)
        gqn_ref = next(it)
        gqr_ref = next(it)
    wkn_ref = next(it)
    wv_ref = next(it)
    gkn_ref = next(it)
    gkr_ref = next(it)
    if with_q:
        q_ref = next(it)
    k_ref = next(it)
    v_ref = next(it)

    tm = ckv_ref.shape[0]
    lane = lax.broadcasted_iota(jnp.int32, (tm, LANES), 1)
    low = lane < QK_ROPE_DIM
    first_half = (lane % ROT_BLOCK) < ROT_HALF
    inv_dim = 1.0 / QK_HEAD_DIM

    def rope(y):
        if not with_rope:
            return y
        return y * cos_ref[...] + _swap_halves(y, first_half) * sin_ref[...]

    if with_q:
        cq = cq_ref[...]
        qn = _bdot(cq, wqn_ref[...])
        qr = _bdot(cq, wqr_ref[...])
        qscale = QK_HEAD_DIM ** -0.5 * LOG2_E
        for p in range(MLA_HEADS // 2):
            blk = qr[:, p * LANES:(p + 1) * LANES]
            sq = blk * blk
            ss_lo = jnp.sum(jnp.where(low, sq, 0.0), axis=-1, keepdims=True)
            ss_hi = jnp.sum(jnp.where(low, 0.0, sq), axis=-1, keepdims=True)
            scales = []
            for hh, ssr in ((2 * p, ss_lo), (2 * p + 1, ss_hi)):
                nh = qn[:, hh * LANES:(hh + 1) * LANES]
                ssq = jnp.sum(nh * nh, axis=-1, keepdims=True) + ssr
                s = lax.rsqrt(ssq * inv_dim + EPS)
                scales.append(s)
                q_ref[hh, :, 0:LANES] = (nh * s * gqn_ref[...] * qscale).astype(BF16)
            s_pair = jnp.where(low, scales[0], scales[1])
            r = rope(blk * s_pair * gqr_ref[...]) * qscale
            q_ref[2 * p, :, LANES:2 * LANES] = jnp.where(low, r, 0.0).astype(BF16)
            q_ref[2 * p + 1, :, LANES:2 * LANES] = jnp.where(low, pltpu.roll(r, QK_ROPE_DIM, 1), 0.0).astype(BF16)

    ckv = ckv_ref[...]
    kn = _bdot(ckv, wkn_ref[...])
    v = _bdot(ckv, wv_ref[...])
    kr = kr_ref[...]
    ss_r = jnp.sum(jnp.where(low, kr * kr, 0.0), axis=-1, keepdims=True)
    base = rope(kr * gkr_ref[...])
    ones_col = jnp.where(lane == 0, 1.0, 0.0).astype(BF16)
    for hh in range(MLA_HEADS):
        nh = kn[:, hh * LANES:(hh + 1) * LANES]
        ssq = jnp.sum(nh * nh, axis=-1, keepdims=True) + ss_r
        s = lax.rsqrt(ssq * inv_dim + EPS)
        k_ref[hh, :, 0:LANES] = (nh * s * gkn_ref[...]).astype(BF16)
        k_ref[hh, :, LANES:2 * LANES] = jnp.where(low, base * s, 0.0).astype(BF16)
        v_ref[hh, :, 0:LANES] = v[:, hh * LANES:(hh + 1) * LANES].astype(BF16)
        v_ref[hh, :, LANES:2 * LANES] = ones_col


def _qkv(cq, ckv, kr2, cos_t, sin_t, wqn, wqr, gqn, gqr2, wkn, wv, gkn, gkr2, *, batch, seq, tm, with_q,
         with_rope):
    t = ckv.shape[0]
    nt = t // tm
    tiles_per_b = seq // tm

    def rows(n):
        return pl.BlockSpec((tm, n), lambda i: (i, 0))

    def const(arr):
        return pl.BlockSpec(arr.shape, lambda i: (0,) * arr.ndim)

    tab_spec = pl.BlockSpec((tm, LANES), lambda i: (i % tiles_per_b, 0))

    def head_out(width):
        return pl.BlockSpec((None, MLA_HEADS, tm, width), lambda i: (i // tiles_per_b, 0, i % tiles_per_b, 0))

    args, in_specs = [], []
    if with_q:
        args.append(cq)
        in_specs.append(rows(Q_LORA_RANK))
    args += [ckv, kr2]
    in_specs += [rows(KV_LORA_RANK), rows(LANES)]
    if with_rope:
        args += [cos_t, sin_t]
        in_specs += [tab_spec, tab_spec]
    if with_q:
        args += [wqn, wqr, gqn, gqr2]
        in_specs += [const(wqn), const(wqr), const(gqn), const(gqr2)]
    args += [wkn, wv, gkn, gkr2]
    in_specs += [const(wkn), const(wv), const(gkn), const(gkr2)]

    out_specs = [head_out(HEAD_PAD), head_out(V_PAD)]
    out_shape = [jax.ShapeDtypeStruct((batch, MLA_HEADS, seq, HEAD_PAD), BF16),
                 jax.ShapeDtypeStruct((batch, MLA_HEADS, seq, V_PAD), BF16)]
    if with_q:
        out_specs = [head_out(HEAD_PAD)] + out_specs
        out_shape = [jax.ShapeDtypeStruct((batch, MLA_HEADS, seq, HEAD_PAD), BF16)] + out_shape
    return pl.pallas_call(
        functools.partial(_qkv_kernel, with_q=with_q, with_rope=with_rope),
        grid=(nt,),
        in_specs=in_specs,
        out_specs=out_specs,
        out_shape=out_shape,
        compiler_params=_cparams(("arbitrary",)),
        name="qkv_x" if with_q else "kv_ctx",
    )(*args)


def _attn_kernel(bound_ref, q_ref, kx_ref, kc_ref, vx_ref, vc_ref, o_ref):
    q = q_ref[...]
    tq = q.shape[0]
    dn = (((1,), (1,)), ((), ()))
    chunks = [(kx_ref, vx_ref, c * ATTN_KEY_CHUNK, ATTN_KEY_CHUNK) for c in range(kx_ref.shape[0] // ATTN_KEY_CHUNK)]
    chunks.append((kc_ref, vc_ref, 0, kc_ref.shape[0]))

    def scores(k_ref, lo, n):
        return lax.dot_general(q, k_ref[lo:lo + n, :], dn, preferred_element_type=F32)

    def finish(acc):
        o_ref[...] = acc[:, :V_HEAD_DIM] / acc[:, V_HEAD_DIM:V_HEAD_DIM + 1]

    bound = bound_ref[0:1, 0:1]
    acc = jnp.zeros((tq, V_PAD), F32)
    for k_ref, v_ref, lo, n in chunks:
        p = jnp.exp2(scores(k_ref, lo, n) - bound).astype(BF16)
        acc = acc + _bdot(p, v_ref[lo:lo + n, :])
    finish(acc)

    row_sum_ok = jnp.min(acc[:, V_HEAD_DIM:V_HEAD_DIM + 1]) >= ATTN_MIN_ROW_SUM

    @pl.when(jnp.logical_not(row_sum_ok))
    def _():
        m = jnp.full((tq, 1), NEG_BIG, F32)
        acc2 = jnp.zeros((tq, V_PAD), F32)
        for k_ref, v_ref, lo, n in chunks:
            s = scores(k_ref, lo, n)
            m_new = jnp.maximum(m, jnp.max(s, axis=-1, keepdims=True))
            p = jnp.exp2(s - m_new).astype(BF16)
            acc2 = jnp.exp2(m - m_new) * acc2 + _bdot(p, v_ref[lo:lo + n, :])
            m = m_new
        finish(acc2)


def _attention(score_bound, q, kx, kc, vx, vc, *, tq):
    b, h, s, _ = q.shape
    lc = kc.shape[2]
    return pl.pallas_call(
        _attn_kernel,
        grid=(b, h, s // tq),
        in_specs=[pl.BlockSpec((1, LANES), lambda bi, hi, qi: (0, 0)),
                  pl.BlockSpec((None, None, tq, HEAD_PAD), lambda bi, hi, qi: (bi, hi, qi, 0)),
                  pl.BlockSpec((None, None, s, HEAD_PAD), lambda bi, hi, qi: (bi, hi, 0, 0)),
                  pl.BlockSpec((None, None, lc, HEAD_PAD), lambda bi, hi, qi: (bi, hi, 0, 0)),
                  pl.BlockSpec((None, None, s, V_PAD), lambda bi, hi, qi: (bi, hi, 0, 0)),
                  pl.BlockSpec((None, None, lc, V_PAD), lambda bi, hi, qi: (bi, hi, 0, 0))],
        out_specs=pl.BlockSpec((None, tq, V_HEAD_DIM), lambda bi, hi, qi: (bi, qi, hi)),
        out_shape=jax.ShapeDtypeStruct((b, s, h * V_HEAD_DIM), F32),
        compiler_params=_cparams(("arbitrary", "arbitrary", "arbitrary")),
        name="attention",
    )(score_bound, q, kx, kc, vx, vc)


def _seq_dft_kernel(ure_ref, uim_ref, r_ref, e_ref, t2_ref, o_ref, a_ref):
    s = pl.program_id(2)
    sub = SUBLANES
    n = 2 * GRID_W * sub
    cols = ure_ref.shape[-1]

    @pl.when(s < DFT_BLOCKS)
    def _():
        t = _bdot(r_ref[...].astype(BF16), e_ref[...])
        row = lax.broadcasted_iota(jnp.int32, (n, n), 0)
        col = lax.broadcasted_iota(jnp.int32, (n, n), 1)
        t = jnp.where((row % sub) == (col % sub), t, 0.0).astype(BF16)
        rhs = jnp.concatenate([ure_ref[...].reshape(GRID_W * sub, cols), uim_ref[...].reshape(GRID_W * sub, cols)],
                              axis=0).astype(BF16)
        a = _bdot(t, rhs)
        a_ref[:, :, pl.ds(pl.multiple_of(s * sub, sub), sub), :] = a.reshape(2, GRID_W, sub, cols)

    @pl.when(s >= DFT_BLOCKS)
    def _():
        k0 = pl.multiple_of((s - DFT_BLOCKS) * sub, sub)
        rhs = a_ref[:, pl.ds(k0, sub), :, :].reshape(2 * sub * GRID_W, cols).astype(BF16)
        y = _bdot(t2_ref[...].astype(BF16), rhs)
        o_ref[...] = y.reshape(GRID_W, sub, cols)


def _seq_dft_tables(n_seq):
    w, sub, nb = GRID_W, SUBLANES, DFT_BLOCKS
    ch = np.arange(nb).reshape(nb, 1, 1, 1)
    kb = np.arange(w).reshape(1, w, 1, 1)
    j = np.arange(sub).reshape(1, 1, sub, 1)
    r = np.arange(w).reshape(1, 1, 1, w)
    ang = (2.0 * np.pi / n_seq) * ((kb * (w * r + sub * ch + j)) % n_seq)
    c, s = np.cos(ang), np.sin(ang)
    rot = np.stack([np.stack([c, s], axis=3), np.stack([-s, c], axis=3)], axis=1)
    r1 = rot.reshape(nb, 2 * w * sub, 2 * w).astype(np.float32)
    expand = (np.arange(2 * w * sub)[None, :] // sub == np.arange(2 * w)[:, None]).astype(np.float32)
    ka = np.arange(w).reshape(w, 1)
    cp = np.arange(w).reshape(1, w)
    ang2 = (2.0 * np.pi / w) * ((ka * cp) % w)
    norm = 1.0 / math.sqrt(n_seq * FOURIER_GROUP_DIM)
    cs = np.stack([np.cos(ang2), np.sin(ang2)], axis=1) * norm
    eye = np.eye(sub)
    t2 = (cs[:, None, :, None, :] * eye[None, :, None, :, None]).reshape(w * sub, 2 * sub * w).astype(np.float32)
    return jnp.asarray(r1), jnp.asarray(expand).astype(BF16), jnp.asarray(t2)


def _seq_dft(u, batch, n_seq):
    assert n_seq == GRID_W * GRID_W
    w, sub, nb = GRID_W, SUBLANES, DFT_BLOCKS
    r1, expand, t2 = _seq_dft_tables(n_seq)
    halves = 2
    cols = FOURIER_DIM // halves
    u5 = u.reshape(batch, w, nb, sub, 2 * FOURIER_DIM)

    def u_spec(part):
        return pl.BlockSpec((None, w, None, sub, cols),
                            lambda b, h, s: (b, 0, jnp.minimum(s, nb - 1), 0, part * halves + h))

    y = pl.pallas_call(
        _seq_dft_kernel,
        grid=(batch, halves, 2 * nb),
        in_specs=[u_spec(0), u_spec(1),
                  pl.BlockSpec((None, 2 * w * sub, 2 * w), lambda b, h, s: (jnp.minimum(s, nb - 1), 0, 0)),
                  pl.BlockSpec((2 * w, 2 * w * sub), lambda b, h, s: (0, 0)),
                  pl.BlockSpec((w * sub, 2 * sub * w), lambda b, h, s: (0, 0))],
        out_specs=pl.BlockSpec((None, w, None, sub, cols), lambda b, h, s: (b, 0, jnp.maximum(s - nb, 0), 0, h)),
        out_shape=jax.ShapeDtypeStruct((batch, w, nb, sub, FOURIER_DIM), F32),
        scratch_shapes=[pltpu.VMEM((2, w, w, cols), F32)],
        compiler_params=_cparams(("arbitrary", "arbitrary", "arbitrary")),
        name="seq_dft",
    )(u5, u5, r1, expand, t2)
    return y.reshape(batch * n_seq, FOURIER_DIM)


def _out_router_kernel(x_ref, four_ref, attn_ref, gt1_ref, sh2_ref, sc2_ref, gf_ref, ga_ref, g2_ref,
                       wo_ref, wrh_ref, wrl_ref, br_ref,
                       xnew_ref, hm_ref, ri_ref, rw_ref, cnt_ref, carry_ref, *, mod_row):
    i = pl.program_id(0)
    tm = x_ref.shape[0]
    row = pl.ds(mod_row(i), 1)

    @pl.when(i == 0)
    def _():
        carry_ref[...] = jnp.zeros_like(carry_ref)

    def norm(v, g):
        return (v * lax.rsqrt(jnp.mean(v * v, axis=-1, keepdims=True) + EPS) * g).astype(BF16)

    mix = (_bdot(norm(four_ref[...], gf_ref[...]), wo_ref[0:FOURIER_DIM, :])
           + _bdot(norm(attn_ref[...], ga_ref[...]), wo_ref[FOURIER_DIM:FOURIER_DIM + MLA_DIM, :]))
    xn = x_ref[...] + gt1_ref[row, :] * mix
    xnew_ref[...] = xn
    ms = jnp.mean(xn * xn, axis=-1, keepdims=True)
    hm = xn * lax.rsqrt(ms + EPS) * (g2_ref[...] * (1.0 + sc2_ref[row, :])) + sh2_ref[row, :]
    hm_ref[...] = _pack_halves(hm)

    hm_hi = hm.astype(BF16)
    hm_lo = (hm - hm_hi.astype(F32)).astype(BF16)
    logits = _bdot(hm_hi, wrh_ref[...]) + _bdot(hm_lo, wrh_ref[...]) + _bdot(hm_hi, wrl_ref[...]) + br_ref[...]
    lane = lax.broadcasted_iota(jnp.int32, (tm, LANES), 1)
    lanef = lane.astype(F32)
    far = 1e9

    lg = jnp.where(lane < N_GROUPS, logits, NEG_BIG)
    m1 = jnp.max(lg, axis=-1, keepdims=True)
    g_p = 1.0 / jnp.sum(jnp.exp(lg - m1), axis=-1, keepdims=True)
    gidx = jnp.min(jnp.where(lg >= m1, lanef, far), axis=-1, keepdims=True)
    lo = N_GROUPS + EXPERTS_PER_GROUP * gidx
    in_group = jnp.where(lanef >= lo, jnp.where(lanef < lo + EXPERTS_PER_GROUP, 1.0, 0.0), 0.0) > 0.5
    le = jnp.where(in_group, logits, NEG_BIG)
    m2 = jnp.max(le, axis=-1, keepdims=True)
    idx1 = jnp.min(jnp.where(le >= m2, lanef, far), axis=-1, keepdims=True)
    le2 = jnp.where(lanef == idx1, NEG_BIG, le)
    m3 = jnp.max(le2, axis=-1, keepdims=True)
    idx2 = jnp.min(jnp.where(le2 >= m3, lanef, far), axis=-1, keepdims=True)
    t = jnp.exp(m3 - m2)
    p1 = 1.0 / (1.0 + t)
    p2 = t / (1.0 + t)
    e1 = idx1 - N_GROUPS
    e2 = idx2 - N_GROUPS

    oh1 = jnp.where(lanef == e1, 1.0, 0.0)
    oh2 = jnp.where(lanef == e2, 1.0, 0.0)
    ohs = oh1 + oh2
    row = lax.broadcasted_iota(jnp.int32, (tm, tm), 0)
    col = lax.broadcasted_iota(jnp.int32, (tm, tm), 1)
    tri = jnp.where(row > col, 1.0, 0.0).astype(BF16)
    before = _bdot(tri, ohs.astype(BF16)) + carry_ref[...]
    rank1 = jnp.sum(oh1 * before, axis=-1, keepdims=True)
    rank2 = jnp.sum(oh2 * before, axis=-1, keepdims=True)
    carry = carry_ref[...] + jnp.sum(ohs, axis=0, keepdims=True)
    carry_ref[...] = carry
    cnt_ref[...] = jnp.broadcast_to(carry, cnt_ref.shape)

    ri = jnp.where(lane == 0, e1, jnp.where(lane == 1, e2, jnp.where(lane == 2, rank1, jnp.where(lane == 3, rank2, 0.0))))
    ri_ref[...] = ri.astype(jnp.int32)
    rw_ref[...] = jnp.where(lane == 0, g_p * p1, jnp.where(lane == 1, g_p * p2, 0.0))


def _out_router(x2, four, attn, mods, gf, ga, g2, wo, wrh, wrl, br, *, seq, tm):
    t, d = x2.shape
    nt = t // tm
    tiles_per_b = seq // tm

    def rows(n):
        return pl.BlockSpec((tm, n), lambda i: (i, 0))

    def const(arr):
        return pl.BlockSpec(arr.shape, lambda i: (0,) * arr.ndim, pipeline_mode=pl.Buffered(1))

    return pl.pallas_call(
        functools.partial(_out_router_kernel, mod_row=lambda i: i // tiles_per_b),
        grid=(nt,),
        in_specs=[rows(d), rows(FOURIER_DIM), rows(MLA_DIM), _mod_spec(mods, 2, d), _mod_spec(mods, 3, d),
                  _mod_spec(mods, 4, d),
                  const(gf), const(ga), const(g2), const(wo), const(wrh), const(wrl), const(br)],
        out_specs=[rows(d), rows(d // 2), rows(LANES), rows(LANES), pl.BlockSpec((8, LANES), lambda i: (0, 0))],
        out_shape=[jax.ShapeDtypeStruct((t, d), F32), jax.ShapeDtypeStruct((t, d // 2), jnp.uint32),
                   jax.ShapeDtypeStruct((t, LANES), jnp.int32), jax.ShapeDtypeStruct((t, LANES), F32),
                   jax.ShapeDtypeStruct((8, LANES), F32)],
        scratch_shapes=[pltpu.VMEM((1, LANES), F32)],
        compiler_params=_cparams(("arbitrary",)),
        name="out_proj_router",
    )(x2, four, attn, mods, mods, mods, gf, ga, g2, wo, wrh, wrl, br)


def _dispatch_kernel(dest, seg_last, seg_blocks, used_blocks, hm_ref, xs_hbm, zbuf, sem, zsem):
    i = pl.program_id(0)
    tm = hm_ref.shape[0]

    @pl.when(i == 0)
    def _():
        zbuf[...] = jnp.zeros(zbuf.shape, zbuf.dtype)
        n_blocks = xs_hbm.shape[0] // MOE_BLOCK

        def zero_copy(blk):
            r0 = pl.multiple_of(blk * MOE_BLOCK, MOE_BLOCK)
            return pltpu.make_async_copy(zbuf, xs_hbm.at[pl.ds(r0, MOE_BLOCK)], zsem.at[0])

        def seg_start(e, carry):
            @pl.when(seg_blocks[e] > 0)
            def _():
                zero_copy(seg_last[e]).start()
            return carry

        def seg_wait(e, carry):
            @pl.when(seg_blocks[e] > 0)
            def _():
                zero_copy(seg_last[e]).wait()
            return carry

        def tail_start(blk, carry):
            zero_copy(blk).start()
            return carry

        def tail_wait(blk, carry):
            zero_copy(blk).wait()
            return carry

        lax.fori_loop(0, N_EXPERTS, seg_start, 0)
        lax.fori_loop(used_blocks[0], n_blocks, tail_start, 0)
        lax.fori_loop(0, N_EXPERTS, seg_wait, 0)
        lax.fori_loop(used_blocks[0], n_blocks, tail_wait, 0)

    def body(r8, carry):
        for u in range(GATHER_UNROLL):
            r = r8 * GATHER_UNROLL + u
            for k in range(2):
                pltpu.make_async_copy(hm_ref.at[pl.ds(r, 1)], xs_hbm.at[pl.ds(dest[2 * (i * tm + r) + k], 1)],
                                      sem.at[0]).start(priority=k)
        return carry

    lax.fori_loop(0, tm // GATHER_UNROLL, body, 0)
    for k in range(2):
        pltpu.make_async_copy(hm_ref, xs_hbm.at[pl.ds(0, tm)], sem.at[0]).wait()


def _dispatch(dest, seg_last, seg_blocks, used_blocks, hm, *, n_rows, tm):
    t, w = hm.shape
    grid_spec = pltpu.PrefetchScalarGridSpec(
        num_scalar_prefetch=4,
        grid=(t // tm,),
        in_specs=[pl.BlockSpec((tm, w), lambda i, *_: (i, 0))],
        out_specs=pl.BlockSpec(memory_space=pl.ANY),
        scratch_shapes=[pltpu.VMEM((MOE_BLOCK, w), hm.dtype), pltpu.SemaphoreType.DMA((1,)),
                        pltpu.SemaphoreType.DMA((1,))],
    )
    return pl.pallas_call(
        _dispatch_kernel,
        grid_spec=grid_spec,
        out_shape=jax.ShapeDtypeStruct((n_rows, w), hm.dtype),
        compiler_params=_cparams(("arbitrary",)),
        name="moe_dispatch",
    )(dest, seg_last, seg_blocks, used_blocks, hm)


def _moe_kernel(item_e, item_blk0, item_nblk, used_blocks,
                xs_hbm, wg_hbm, wu_hbm, wd_hbm, y_hbm,
                xg, xb, gs, ab, yp, gu_buf, dn_buf, gsem, osem, gusem, dnsem):
    i = pl.program_id(0)
    j = pl.program_id(1)
    n_items = pl.num_programs(0)
    nj = pl.num_programs(1)
    slot = i % 2
    nblk = item_nblk[i]

    def weight_copy(it, ph):
        if ph == 2:
            ws = it % DN_SLOTS
            return pltpu.make_async_copy(wd_hbm.at[item_e[it]], dn_buf.at[ws], dnsem.at[ws])
        ws = (2 * it + ph) % GU_SLOTS
        return pltpu.make_async_copy((wg_hbm, wu_hbm)[ph].at[item_e[it]], gu_buf.at[ws], gusem.at[ws])

    def start_weight(it, ph):
        it_c = jnp.minimum(it, n_items - 1)

        @pl.when(jnp.logical_and(it < n_items, item_nblk[it_c] > 0))
        def _():
            weight_copy(it_c, ph).start(priority=WEIGHT_DMA_PRIORITY)

    def x_copy(it, sl, m):
        r0 = pl.multiple_of(item_blk0[it] * MOE_BLOCK, MOE_BLOCK)
        return pltpu.make_async_copy(xs_hbm.at[pl.ds(r0, m)], xg.at[sl, pl.ds(0, m)], gsem.at[sl])

    def issue_gather(it, sl):
        for nb in range(1, ITEM_BLOCKS + 1):
            @pl.when(item_nblk[it] == nb)
            def _():
                x_copy(it, sl, nb * MOE_BLOCK).start(priority=ROW_DMA_PRIORITY)

    def wait_gather(it, sl):
        for nb in range(1, ITEM_BLOCKS + 1):
            @pl.when(item_nblk[it] == nb)
            def _():
                x_copy(it, sl, nb * MOE_BLOCK).wait()

    def out_copy(it, m):
        r0 = pl.multiple_of(item_blk0[it] * MOE_BLOCK, MOE_BLOCK)
        return pltpu.make_async_copy(yp.at[pl.ds(0, m)], y_hbm.at[pl.ds(r0, m)], osem.at[0])

    def wait_out(it):
        for nb in range(1, ITEM_BLOCKS + 1):
            @pl.when(item_nblk[it] == nb)
            def _():
                out_copy(it, nb * MOE_BLOCK).wait()

    @pl.when(j == 0)
    def _():
        @pl.when(i == 0)
        def _():
            start_weight(0, 0)
            start_weight(0, 1)
            start_weight(1, 0)
            start_weight(0, 2)
            issue_gather(0, 0)

        start_weight(i + 1, 2)
        wait_gather(i, slot)

        @pl.when(i + 1 < n_items)
        def _():
            issue_gather(i + 1, 1 - slot)

    @pl.when(j == 1)
    def _():
        start_weight(i + 1, 1)

    @pl.when(j == 2)
    def _():
        start_weight(i + 2, 0)

        @pl.when(i > 0)
        def _():
            wait_out(i - 1)

    for ph in range(3):
        @pl.when(jnp.logical_and(j == ph, nblk > 0))
        def _():
            weight_copy(i, ph).wait()

    for nb in range(1, ITEM_BLOCKS + 1):
        m = nb * MOE_BLOCK

        @pl.when(jnp.logical_and(nblk == nb, j == 0))
        def _():
            hi, lo = _unpack_halves(xg[slot, 0:m, :])
            half = hi.shape[1]
            xb[0:m, 0:half] = hi.astype(BF16)
            xb[0:m, half:2 * half] = lo.astype(BF16)
            gs[0:m, :] = _bdot(xb[0:m, :], gu_buf[(2 * i) % GU_SLOTS].astype(BF16))

        @pl.when(jnp.logical_and(nblk == nb, j == 1))
        def _():
            g = gs[0:m, :]
            u = _bdot(xb[0:m, :], gu_buf[(2 * i + 1) % GU_SLOTS].astype(BF16))
            ab[0:m, :] = (g * jax.nn.sigmoid(g) * u).astype(BF16)

        @pl.when(jnp.logical_and(nblk == nb, j == 2))
        def _():
            yp[0:m, :] = _pack_halves(_bdot(ab[0:m, :], dn_buf[i % DN_SLOTS].astype(BF16)))
            out_copy(i, m).start()

    @pl.when(jnp.logical_and(i == n_items - 1, j == nj - 1))
    def _():
        wait_out(i)
        n_blocks = y_hbm.shape[0] // MOE_BLOCK
        yp[0:MOE_BLOCK, :] = jnp.zeros((MOE_BLOCK, yp.shape[1]), jnp.uint32)

        def tail_copy(blk):
            r0 = pl.multiple_of(blk * MOE_BLOCK, MOE_BLOCK)
            return pltpu.make_async_copy(yp.at[pl.ds(0, MOE_BLOCK)], y_hbm.at[pl.ds(r0, MOE_BLOCK)], osem.at[0])

        def start_body(blk, carry):
            tail_copy(blk).start()
            return carry

        def wait_body(blk, carry):
            tail_copy(blk).wait()
            return carry

        lax.fori_loop(used_blocks[0], n_blocks, start_body, 0)
        lax.fori_loop(used_blocks[0], n_blocks, wait_body, 0)


def _moe(xs, w_gate, w_up, w_down, item_e, item_blk0, item_nblk, used_blocks, *, n_rows):
    d, de = w_gate.shape[1], w_gate.shape[2]
    n_items = item_e.shape[0]
    nj = 3
    rows = ITEM_BLOCKS * MOE_BLOCK
    any_spec = pl.BlockSpec(memory_space=pl.ANY)
    grid_spec = pltpu.PrefetchScalarGridSpec(
        num_scalar_prefetch=4,
        grid=(n_items, nj),
        in_specs=[any_spec, any_spec, any_spec, any_spec],
        out_specs=any_spec,
        scratch_shapes=[pltpu.VMEM((2, rows, d // 2), jnp.uint32),
                        pltpu.VMEM((rows, d), BF16),
                        pltpu.VMEM((rows, de), F32),
                        pltpu.VMEM((rows, de), BF16),
                        pltpu.VMEM((rows, d // 2), jnp.uint32),
                        pltpu.VMEM((GU_SLOTS, d, de), F32),
                        pltpu.VMEM((DN_SLOTS, de, d), F32),
                        pltpu.SemaphoreType.DMA((2,)),
                        pltpu.SemaphoreType.DMA((1,)),
                        pltpu.SemaphoreType.DMA((GU_SLOTS,)),
                        pltpu.SemaphoreType.DMA((DN_SLOTS,))],
    )
    return pl.pallas_call(
        _moe_kernel,
        grid_spec=grid_spec,
        out_shape=jax.ShapeDtypeStruct((n_rows, d // 2), jnp.uint32),
        compiler_params=_cparams(("arbitrary", "arbitrary")),
        name="moe_experts",
    )(item_e, item_blk0, item_nblk, used_blocks, xs, w_gate, w_up, w_down)


def _combine_kernel(dest, x_ref, gt2_ref, rw_ref, y_hbm, o_ref, ybuf, sem, *, mod_row):
    i = pl.program_id(0)
    n = pl.num_programs(0)
    tm = x_ref.shape[0]
    slot = i % 2

    def issue(it, sl):
        base = it * tm

        def body(r4, carry):
            for rr in range(GATHER_UNROLL // 2):
                r = r4 * (GATHER_UNROLL // 2) + rr
                for k in range(2):
                    pltpu.make_async_copy(y_hbm.at[pl.ds(dest[2 * (base + r) + k], 1)], ybuf.at[sl, k, pl.ds(r, 1)],
                                          sem.at[sl]).start(priority=k)
            return carry

        lax.fori_loop(0, tm // (GATHER_UNROLL // 2), body, 0)

    @pl.when(i == 0)
    def _():
        issue(0, 0)

    for k in range(2):
        pltpu.make_async_copy(y_hbm.at[pl.ds(0, tm)], ybuf.at[slot, k], sem.at[slot]).wait()

    @pl.when(i + 1 < n)
    def _():
        issue(i + 1, 1 - slot)

    w = rw_ref[...]
    gate = gt2_ref[pl.ds(mod_row(i), 1), :]
    hi0, lo0 = _unpack_halves(ybuf[slot, 0])
    hi1, lo1 = _unpack_halves(ybuf[slot, 1])
    half = hi0.shape[1]
    o_ref[:, 0:half] = x_ref[:, 0:half] + gate[:, 0:half] * (w[:, 0:1] * hi0 + w[:, 1:2] * hi1)
    o_ref[:, half:2 * half] = (x_ref[:, half:2 * half]
                               + gate[:, half:2 * half] * (w[:, 0:1] * lo0 + w[:, 1:2] * lo1))


def _combine(dest, xnew, mods, rw, y, *, seq, tm):
    t, d = xnew.shape
    tiles_per_b = seq // tm
    grid_spec = pltpu.PrefetchScalarGridSpec(
        num_scalar_prefetch=1,
        grid=(t // tm,),
        in_specs=[pl.BlockSpec((tm, d), lambda i, ds: (i, 0)),
                  _mod_spec(mods, 5, d),
                  pl.BlockSpec((tm, LANES), lambda i, ds: (i, 0)),
                  pl.BlockSpec(memory_space=pl.ANY)],
        out_specs=pl.BlockSpec((tm, d), lambda i, ds: (i, 0)),
        scratch_shapes=[pltpu.VMEM((2, 2, tm, d // 2), jnp.uint32), pltpu.SemaphoreType.DMA((2,))],
    )
    return pl.pallas_call(
        functools.partial(_combine_kernel, mod_row=lambda i: i // tiles_per_b),
        grid_spec=grid_spec,
        out_shape=jax.ShapeDtypeStruct((t, d), F32),
        compiler_params=_cparams(("arbitrary",)),
        name="moe_combine",
    )(dest, xnew, mods, rw, y)


def _rope_tables(n_tokens):
    rows = n_tokens // GRID_W
    row = jnp.repeat(jnp.arange(rows, dtype=jnp.int32), GRID_W).astype(F32)
    col = jnp.tile(jnp.arange(GRID_W, dtype=jnp.int32), rows).astype(F32)
    n_freq = QK_ROPE_DIM // 4
    inv = ROPE_THETA ** (-jnp.arange(n_freq, dtype=F32) / n_freq)
    ar = row[:, None] * inv[None, :]
    ac = col[:, None] * inv[None, :]
    cr, sr, cc, sc = jnp.cos(ar), jnp.sin(ar), jnp.cos(ac), jnp.sin(ac)
    cos64 = jnp.concatenate([cr, cr, cc, cc], axis=-1)
    sin64 = jnp.concatenate([-sr, sr, -sc, sc], axis=-1)
    return jnp.tile(cos64, (1, 2)), jnp.tile(sin64, (1, 2))


def _channel_dft_table():
    c = np.arange(FOURIER_GROUP_DIM).reshape(-1, 1)
    k = np.arange(FOURIER_GROUP_DIM).reshape(1, -1)
    ang = (2.0 * np.pi / FOURIER_GROUP_DIM) * ((c * k) % FOURIER_GROUP_DIM)
    return jnp.asarray(np.concatenate([np.cos(ang), -np.sin(ang)], axis=1).astype(np.float32)).astype(BF16)


def _split_heads(w, widths):
    k = w.shape[0]
    wh = w.reshape(k, MLA_HEADS, sum(widths))
    outs, off = [], 0
    for wd in widths:
        outs.append(wh[:, :, off:off + wd].reshape(k, MLA_HEADS * wd))
        off += wd
    return outs


def kernel(x, c, ctx, c_ctx, w_ada, b_ada, g_norm1, g_norm2, w_in, g_q_a, g_kv_a, w_uq, w_ukv, g_qk_q, g_qk_k,
           g_out_four, g_out_attn, w_out, w_router_group, b_router_group, w_router_expert, b_router_expert,
           w_gate, w_up, w_down):
    b, s, d = x.shape
    lc = ctx.shape[1]
    t = b * s
    layer_params = (w_ada, b_ada, g_norm1, g_norm2, w_in, g_q_a, g_kv_a, w_uq, w_ukv, g_qk_q, g_qk_k, g_out_four,
                    g_out_attn, w_out, w_router_group, b_router_group, w_router_expert, b_router_expert,
                    w_gate, w_up, w_down)
    assert all(p.shape[0] == 1 for p in layer_params), "single-layer block"
    (w_ada, b_ada, g_norm1, g_norm2, w_in, g_q_a, g_kv_a, w_uq, w_ukv, g_qk_q, g_qk_k, g_out_four,
     g_out_attn, w_out, w_router_group, b_router_group, w_router_expert, b_router_expert,
     w_gate, w_up, w_down) = [p.reshape(p.shape[1:]) for p in layer_params]

    cond8 = jnp.concatenate([c, c_ctx[None, :], jnp.zeros((8 - b - 1, d), F32)], axis=0)
    mods = _ada_mod(cond8, w_ada, b_ada)

    w_all = jnp.concatenate([w_in, w_in[:, ROPE_OFF:]], axis=1).astype(BF16)
    dc = _channel_dft_table()
    wqn, wqr = [w.astype(BF16) for w in _split_heads(w_uq, (QK_NOPE_DIM, QK_ROPE_DIM))]
    wkn, wv = [w.astype(BF16) for w in _split_heads(w_ukv, (QK_NOPE_DIM, V_HEAD_DIM))]
    gqn = g_qk_q[:QK_NOPE_DIM].reshape(1, -1)
    gqr2 = jnp.tile(g_qk_q[QK_NOPE_DIM:], 2).reshape(1, -1)
    gkn = g_qk_k[:QK_NOPE_DIM].reshape(1, -1)
    gkr2 = jnp.tile(g_qk_k[QK_NOPE_DIM:], 2).reshape(1, -1)
    g1 = g_norm1.reshape(1, d)
    gq = g_q_a.reshape(1, -1)
    gkv = g_kv_a.reshape(1, -1)

    x2 = x.reshape(t, d)
    u, cq, ckv, kr2 = _in_proj(x2, mods, 0, s, g1, gq, gkv, w_all, dc, with_q=True, tm=IN_PROJ_ROWS)
    ckv_c, kr2_c = _in_proj(ctx.reshape(b * lc, d), mods, b, b * lc, g1, gq, gkv, w_all, dc, with_q=False, tm=lc)

    cos_t, sin_t = _rope_tables(s)
    q, kx, vx = _qkv(cq, ckv, kr2, cos_t, sin_t, wqn, wqr, gqn, gqr2, wkn, wv, gkn, gkr2,
                     batch=b, seq=s, tm=QKV_ROWS, with_q=True, with_rope=True)
    kc, vc = _qkv(None, ckv_c, kr2_c, None, None, None, None, None, None, wkn, wv, gkn, gkr2,
                  batch=b, seq=lc, tm=lc, with_q=False, with_rope=False)

    score_bound = (QK_HEAD_DIM ** 0.5 * LOG2_E * ATTN_BOUND_MARGIN) * jnp.max(jnp.abs(g_qk_q)) * jnp.max(jnp.abs(g_qk_k))
    score_bound = jnp.full((1, LANES), score_bound, F32)
    attn = _attention(score_bound, q, kx, kc, vx, vc, tq=ATTN_Q_ROWS).reshape(t, MLA_DIM)
    four = _seq_dft(u, b, s)

    wo = w_out.astype(BF16)
    n_route = N_GROUPS + N_EXPERTS
    wrt = jnp.concatenate([w_router_group, w_router_expert, jnp.zeros((d, LANES - n_route), F32)], axis=1)
    wrt_hi = wrt.astype(BF16)
    wrt_lo = (wrt - wrt_hi.astype(F32)).astype(BF16)
    brt = jnp.concatenate([b_router_group, b_router_expert, jnp.zeros((LANES - n_route,), F32)]).reshape(1, -1)
    xnew, hm, ri, rw, cnt = _out_router(x2, four, attn, mods, g_out_four.reshape(1, -1), g_out_attn.reshape(1, -1),
                                        g_norm2.reshape(1, d), wo, wrt_hi, wrt_lo, brt, seq=s, tm=OUT_ROUTER_ROWS)

    counts = cnt[0, :N_EXPERTS].astype(jnp.int32)
    nblk_e = (counts + MOE_BLOCK - 1) // MOE_BLOCK
    blk_end = jnp.cumsum(nblk_e)
    blk_start = blk_end - nblk_e
    n_slots = t * 2
    n_blocks = -(-(n_slots + N_EXPERTS * (MOE_BLOCK - 1)) // MOE_BLOCK)
    n_rows = n_blocks * MOE_BLOCK
    e12 = ri[:, 0:2]
    seg_start = jnp.sum(jnp.where(e12[:, :, None] == jnp.arange(N_EXPERTS, dtype=jnp.int32), blk_start, 0), axis=-1)
    dest = (seg_start * MOE_BLOCK + ri[:, 2:4]).reshape(-1).astype(jnp.int32)
    items_e = (nblk_e + ITEM_BLOCKS - 1) // ITEM_BLOCKS
    item_end = jnp.cumsum(items_e)
    n_items = (n_blocks + (ITEM_BLOCKS - 1) * N_EXPERTS) // ITEM_BLOCKS
    idx = jnp.arange(n_items, dtype=jnp.int32)
    total = item_end[-1]
    idx_c = jnp.minimum(idx, total - 1)
    ie = jnp.minimum(jnp.sum(idx_c[:, None] >= item_end[None, :], axis=1), N_EXPERTS - 1).astype(jnp.int32)
    local = idx_c - (item_end - items_e)[ie]
    item_blk0 = (blk_start[ie] + ITEM_BLOCKS * local).astype(jnp.int32)
    item_nblk = jnp.where(idx < total, jnp.clip(nblk_e[ie] - ITEM_BLOCKS * local, 0, ITEM_BLOCKS), 0).astype(jnp.int32)

    used_blocks = blk_end[-1:].astype(jnp.int32)
    seg_last = jnp.maximum(blk_end - 1, 0).astype(jnp.int32)
    xs = _dispatch(dest, seg_last, nblk_e.astype(jnp.int32), used_blocks, hm, n_rows=n_rows, tm=DISPATCH_ROWS)
    y = _moe(xs, w_gate, w_up, w_down, ie, item_blk0, item_nblk, used_blocks, n_rows=n_rows)
    out = _combine(dest, xnew, mods, rw, y, seq=s, tm=COMBINE_ROWS)
    return out.reshape(b, s, d)
```

```python
import functools
import math

import numpy as np
import jax
import jax.numpy as jnp
from jax import lax
from jax.experimental import pallas as pl
from jax.experimental.pallas import tpu as pltpu

F32 = jnp.float32
BF16 = jnp.bfloat16

D_MODEL = 2048
GRID_W = 64
EPS = 1e-6
N_MOD = 6
N_FOURIER_GROUPS = 4
FOURIER_GROUP_DIM = 256
FOURIER_DIM = 1024
MLA_HEADS = 8
QK_NOPE_DIM = 128
QK_ROPE_DIM = 64
QK_HEAD_DIM = 192
V_HEAD_DIM = 128
Q_LORA_RANK = 768
KV_LORA_RANK = 512
MLA_DIM = 1024
ROPE_THETA = 10000.0
Q_OFF = FOURIER_DIM
KV_OFF = Q_OFF + Q_LORA_RANK
ROPE_OFF = KV_OFF + KV_LORA_RANK
N_GROUPS = 8
EXPERTS_PER_GROUP = 8
N_EXPERTS = 64
D_EXPERT = 768

ROT_HALF = QK_ROPE_DIM // 4
ROT_BLOCK = 2 * ROT_HALF

ADA_COLS = 1024
IN_PROJ_ROWS = 512
QKV_ROWS = 512
ATTN_Q_ROWS = 2048
OUT_ROUTER_ROWS = 512
COMBINE_ROWS = 512
DISPATCH_ROWS = 1024

LANES = 128
SUBLANES = 8
DFT_BLOCKS = GRID_W // SUBLANES
HEAD_PAD = 256
V_PAD = 256
MOE_BLOCK = 64
ITEM_BLOCKS = 8
GATHER_UNROLL = 8
GU_SLOTS = 4
DN_SLOTS = 3
ATTN_KEY_CHUNK = 256
ATTN_BOUND_MARGIN = 1.02
ATTN_MIN_ROW_SUM = 1e-30
LOG2_E = math.log2(math.e)
ROW_DMA_PRIORITY = 0
WEIGHT_DMA_PRIORITY = 1
VMEM_LIMIT = 56 * 1024 * 1024
NEG_BIG = -1e30


def _cparams(sem):
    return pltpu.CompilerParams(dimension_semantics=sem, vmem_limit_bytes=VMEM_LIMIT)


def _bdot(a, b):
    return jnp.dot(a, b, preferred_element_type=F32)


def _pack_halves(v):
    n = v.shape[1] // 2
    hi = pltpu.bitcast(v[:, :n].astype(BF16).astype(F32), jnp.uint32)
    lo = pltpu.bitcast(v[:, n:].astype(BF16).astype(F32), jnp.uint32)
    return hi | (lo >> 16)


def _unpack_halves(u):
    hi = pltpu.bitcast(u & jnp.uint32(0xFFFF0000), F32)
    lo = pltpu.bitcast(u << 16, F32)
    return hi, lo


def _ada_kernel(c_ref, w_ref, b_ref, o_ref):
    c = c_ref[...]
    s = (c * jax.nn.sigmoid(c)).astype(BF16)
    o_ref[...] = _bdot(s, w_ref[...].astype(BF16)) + b_ref[...]


def _ada_mod(cond8, w_ada, b_ada):
    d, n = w_ada.shape
    tn = ADA_COLS
    return pl.pallas_call(
        _ada_kernel,
        grid=(n // tn,),
        in_specs=[pl.BlockSpec((8, d), lambda i: (0, 0)),
                  pl.BlockSpec((d, tn), lambda i: (0, i)),
                  pl.BlockSpec((1, tn), lambda i: (0, i))],
        out_specs=pl.BlockSpec((8, tn), lambda i: (0, i)),
        out_shape=jax.ShapeDtypeStruct((8, n), F32),
        compiler_params=_cparams(("arbitrary",)),
        name="ada_mod",
    )(cond8, w_ada, b_ada.reshape(1, n))


def _in_proj_kernel(x_ref, sh_ref, sc_ref, g_ref, gq_ref, gkv_ref, w_ref, dc_ref, *out_refs, with_q, mod_row):
    row = pl.ds(mod_row(pl.program_id(0)), 1)
    x = x_ref[...]
    ms = jnp.mean(x * x, axis=-1, keepdims=True)
    a = g_ref[...] * (1.0 + sc_ref[row, :])
    h = (x * lax.rsqrt(ms + EPS) * a + sh_ref[row, :]).astype(BF16)
    if with_q:
        u_ref, cq_ref, ckv_ref, kr_ref = out_refs
        f = _bdot(h, w_ref[:, 0:Q_OFF]).astype(BF16)
        dc = dc_ref[...]
        for g in range(N_FOURIER_GROUPS):
            lo = g * FOURIER_GROUP_DIM
            ug = _bdot(f[:, lo:lo + FOURIER_GROUP_DIM], dc)
            u_ref[:, lo:lo + FOURIER_GROUP_DIM] = ug[:, :FOURIER_GROUP_DIM]
            u_ref[:, FOURIER_DIM + lo:FOURIER_DIM + lo + FOURIER_GROUP_DIM] = ug[:, FOURIER_GROUP_DIM:]
        pq = _bdot(h, w_ref[:, Q_OFF:KV_OFF])
        msq = jnp.mean(pq * pq, axis=-1, keepdims=True)
        cq_ref[...] = (pq * lax.rsqrt(msq + EPS) * gq_ref[...]).astype(BF16)
    else:
        ckv_ref, kr_ref = out_refs
    pkv = _bdot(h, w_ref[:, KV_OFF:ROPE_OFF])
    mskv = jnp.mean(pkv * pkv, axis=-1, keepdims=True)
    ckv_ref[...] = (pkv * lax.rsqrt(mskv + EPS) * gkv_ref[...]).astype(BF16)
    kr_ref[...] = _bdot(h, w_ref[:, ROPE_OFF:ROPE_OFF + LANES])


def _mod_spec(mods, k, d):
    return pl.BlockSpec((mods.shape[0], d), lambda i, *_: (0, k))


def _in_proj(x2, mods, first_row, rows_per_mod, g1, gq, gkv, w_all, dc, *, with_q, tm):
    t, d = x2.shape
    nt = t // tm
    tiles_per_mod = rows_per_mod // tm

    def const(shape):
        return pl.BlockSpec(shape, lambda i: (0,) * len(shape))

    in_specs = [pl.BlockSpec((tm, d), lambda i: (i, 0)), _mod_spec(mods, 0, d), _mod_spec(mods, 1, d),
                const((1, d)), const((1, Q_LORA_RANK)), const((1, KV_LORA_RANK)),
                const(w_all.shape), const(dc.shape)]

    def rows(n):
        return pl.BlockSpec((tm, n), lambda i: (i, 0))

    out_specs = [rows(KV_LORA_RANK), rows(LANES)]
    out_shape = [jax.ShapeDtypeStruct((t, KV_LORA_RANK), BF16), jax.ShapeDtypeStruct((t, LANES), F32)]
    if with_q:
        out_specs = [rows(2 * FOURIER_DIM), rows(Q_LORA_RANK)] + out_specs
        out_shape = [jax.ShapeDtypeStruct((t, 2 * FOURIER_DIM), F32),
                     jax.ShapeDtypeStruct((t, Q_LORA_RANK), BF16)] + out_shape
    return pl.pallas_call(
        functools.partial(_in_proj_kernel, with_q=with_q, mod_row=lambda i: first_row + i // tiles_per_mod),
        grid=(nt,),
        in_specs=in_specs,
        out_specs=out_specs,
        out_shape=out_shape,
        compiler_params=_cparams(("arbitrary",)),
        name="in_proj_x" if with_q else "in_proj_ctx",
    )(x2, mods, mods, g1, gq, gkv, w_all, dc)


def _swap_halves(y, first_half):
    return jnp.where(first_half, pltpu.roll(y, LANES - ROT_HALF, 1), pltpu.roll(y, ROT_HALF, 1))


def _qkv_kernel(*refs, with_q, with_rope):
    it = iter(refs)
    if with_q:
        cq_ref = next(it)
    ckv_ref = next(it)
    kr_ref = next(it)
    if with_rope:
        cos_ref = next(it)
        sin_ref = next(it)
    if with_q:
        wqn_ref = next(it)
        wqr_ref = next(it)
        gqn_ref = next(it)
        gqr_ref = next(it)
    wkn_ref = next(it)
    wv_ref = next(it)
    gkn_ref = next(it)
    gkr_ref = next(it)
    if with_q:
        q_ref = next(it)
    k_ref = next(it)
    v_ref = next(it)

    tm = ckv_ref.shape[0]
    lane = lax.broadcasted_iota(jnp.int32, (tm, LANES), 1)
    low = lane < QK_ROPE_DIM
    first_half = (lane % ROT_BLOCK) < ROT_HALF
    inv_dim = 1.0 / QK_HEAD_DIM

    def rope(y):
        if not with_rope:
            return y
        return y * cos_ref[...] + _swap_halves(y, first_half) * sin_ref[...]

    if with_q:
        cq = cq_ref[...]
        qn = _bdot(cq, wqn_ref[...])
        qr = _bdot(cq, wqr_ref[...])
        qscale = QK_HEAD_DIM ** -0.5 * LOG2_E
        for p in range(MLA_HEADS // 2):
            blk = qr[:, p * LANES:(p + 1) * LANES]
            sq = blk * blk
            ss_lo = jnp.sum(jnp.where(low, sq, 0.0), axis=-1, keepdims=True)
            ss_hi = jnp.sum(jnp.where(low, 0.0, sq), axis=-1, keepdims=True)
            scales = []
            for hh, ssr in ((2 * p, ss_lo), (2 * p + 1, ss_hi)):
                nh = qn[:, hh * LANES:(hh + 1) * LANES]
                ssq = jnp.sum(nh * nh, axis=-1, keepdims=True) + ssr
                s = lax.rsqrt(ssq * inv_dim + EPS)
                scales.append(s)
                q_ref[hh, :, 0:LANES] = (nh * s * gqn_ref[...] * qscale).astype(BF16)
            s_pair = jnp.where(low, scales[0], scales[1])
            r = rope(blk * s_pair * gqr_ref[...]) * qscale
            q_ref[2 * p, :, LANES:2 * LANES] = jnp.where(low, r, 0.0).astype(BF16)
            q_ref[2 * p + 1, :, LANES:2 * LANES] = jnp.where(low, pltpu.roll(r, QK_ROPE_DIM, 1), 0.0).astype(BF16)

    ckv = ckv_ref[...]
    kn = _bdot(ckv, wkn_ref[...])
    v = _bdot(ckv, wv_ref[...])
    kr = kr_ref[...]
    ss_r = jnp.sum(jnp.where(low, kr * kr, 0.0), axis=-1, keepdims=True)
    base = rope(kr * gkr_ref[...])
    ones_col = jnp.where(lane == 0, 1.0, 0.0).astype(BF16)
    for hh in range(MLA_HEADS):
        nh = kn[:, hh * LANES:(hh + 1) * LANES]
        ssq = jnp.sum(nh * nh, axis=-1, keepdims=True) + ss_r
        s = lax.rsqrt(ssq * inv_dim + EPS)
        k_ref[hh, :, 0:LANES] = (nh * s * gkn_ref[...]).astype(BF16)
        k_ref[hh, :, LANES:2 * LANES] = jnp.where(low, base * s, 0.0).astype(BF16)
        v_ref[hh, :, 0:LANES] = v[:, hh * LANES:(hh + 1) * LANES].astype(BF16)
        v_ref[hh, :, LANES:2 * LANES] = ones_col


def _qkv(cq, ckv, kr2, cos_t, sin_t, wqn, wqr, gqn, gqr2, wkn, wv, gkn, gkr2, *, batch, seq, tm, with_q,
         with_rope):
    t = ckv.shape[0]
    nt = t // tm
    tiles_per_b = seq // tm

    def rows(n):
        return pl.BlockSpec((tm, n), lambda i: (i, 0))

    def const(arr):
        return pl.BlockSpec(arr.shape, lambda i: (0,) * arr.ndim)

    tab_spec = pl.BlockSpec((tm, LANES), lambda i: (i % tiles_per_b, 0))

    def head_out(width):
        return pl.BlockSpec((None, MLA_HEADS, tm, width), lambda i: (i // tiles_per_b, 0, i % tiles_per_b, 0))

    args, in_specs = [], []
    if with_q:
        args.append(cq)
        in_specs.append(rows(Q_LORA_RANK))
    args += [ckv, kr2]
    in_specs += [rows(KV_LORA_RANK), rows(LANES)]
    if with_rope:
        args += [cos_t, sin_t]
        in_specs += [tab_spec, tab_spec]
    if with_q:
        args += [wqn, wqr, gqn, gqr2]
        in_specs += [const(wqn), const(wqr), const(gqn), const(gqr2)]
    args += [wkn, wv, gkn, gkr2]
    in_specs += [const(wkn), const(wv), const(gkn), const(gkr2)]

    out_specs = [head_out(HEAD_PAD), head_out(V_PAD)]
    out_shape = [jax.ShapeDtypeStruct((batch, MLA_HEADS, seq, HEAD_PAD), BF16),
                 jax.ShapeDtypeStruct((batch, MLA_HEADS, seq, V_PAD), BF16)]
    if with_q:
        out_specs = [head_out(HEAD_PAD)] + out_specs
        out_shape = [jax.ShapeDtypeStruct((batch, MLA_HEADS, seq, HEAD_PAD), BF16)] + out_shape
    return pl.pallas_call(
        functools.partial(_qkv_kernel, with_q=with_q, with_rope=with_rope),
        grid=(nt,),
        in_specs=in_specs,
        out_specs=out_specs,
        out_shape=out_shape,
        compiler_params=_cparams(("arbitrary",)),
        name="qkv_x" if with_q else "kv_ctx",
    )(*args)


def _attn_kernel(bound_ref, q_ref, kx_ref, kc_ref, vx_ref, vc_ref, o_ref):
    q = q_ref[...]
    tq = q.shape[0]
    dn = (((1,), (1,)), ((), ()))
    chunks = [(kx_ref, vx_ref, c * ATTN_KEY_CHUNK, ATTN_KEY_CHUNK) for c in range(kx_ref.shape[0] // ATTN_KEY_CHUNK)]
    chunks.append((kc_ref, vc_ref, 0, kc_ref.shape[0]))

    def scores(k_ref, lo, n):
        return lax.dot_general(q, k_ref[lo:lo + n, :], dn, preferred_element_type=F32)

    def finish(acc):
        o_ref[...] = acc[:, :V_HEAD_DIM] / acc[:, V_HEAD_DIM:V_HEAD_DIM + 1]

    bound = bound_ref[0:1, 0:1]
    acc = jnp.zeros((tq, V_PAD), F32)
    for k_ref, v_ref, lo, n in chunks:
        p = jnp.exp2(scores(k_ref, lo, n) - bound).astype(BF16)
        acc = acc + _bdot(p, v_ref[lo:lo + n, :])
    finish(acc)

    row_sum_ok = jnp.min(acc[:, V_HEAD_DIM:V_HEAD_DIM + 1]) >= ATTN_MIN_ROW_SUM

    @pl.when(jnp.logical_not(row_sum_ok))
    def _():
        m = jnp.full((tq, 1), NEG_BIG, F32)
        acc2 = jnp.zeros((tq, V_PAD), F32)
        for k_ref, v_ref, lo, n in chunks:
            s = scores(k_ref, lo, n)
            m_new = jnp.maximum(m, jnp.max(s, axis=-1, keepdims=True))
            p = jnp.exp2(s - m_new).astype(BF16)
            acc2 = jnp.exp2(m - m_new) * acc2 + _bdot(p, v_ref[lo:lo + n, :])
            m = m_new
        finish(acc2)


def _attention(score_bound, q, kx, kc, vx, vc, *, tq):
    b, h, s, _ = q.shape
    lc = kc.shape[2]
    return pl.pallas_call(
        _attn_kernel,
        grid=(b, h, s // tq),
        in_specs=[pl.BlockSpec((1, LANES), lambda bi, hi, qi: (0, 0)),
                  pl.BlockSpec((None, None, tq, HEAD_PAD), lambda bi, hi, qi: (bi, hi, qi, 0)),
                  pl.BlockSpec((None, None, s, HEAD_PAD), lambda bi, hi, qi: (bi, hi, 0, 0)),
                  pl.BlockSpec((None, None, lc, HEAD_PAD), lambda bi, hi, qi: (bi, hi, 0, 0)),
                  pl.BlockSpec((None, None, s, V_PAD), lambda bi, hi, qi: (bi, hi, 0, 0)),
                  pl.BlockSpec((None, None, lc, V_PAD), lambda bi, hi, qi: (bi, hi, 0, 0))],
        out_specs=pl.BlockSpec((None, tq, V_HEAD_DIM), lambda bi, hi, qi: (bi, qi, hi)),
        out_shape=jax.ShapeDtypeStruct((b, s, h * V_HEAD_DIM), F32),
        compiler_params=_cparams(("arbitrary", "arbitrary", "arbitrary")),
        name="attention",
    )(score_bound, q, kx, kc, vx, vc)


def _seq_dft_kernel(ure_ref, uim_ref, r_ref, e_ref, t2_ref, o_ref, a_ref):
    s = pl.program_id(2)
    sub = SUBLANES
    n = 2 * GRID_W * sub
    cols = ure_ref.shape[-1]

    @pl.when(s < DFT_BLOCKS)
    def _():
        t = _bdot(r_ref[...].astype(BF16), e_ref[...])
        row = lax.broadcasted_iota(jnp.int32, (n, n), 0)
        col = lax.broadcasted_iota(jnp.int32, (n, n), 1)
        t = jnp.where((row % sub) == (col % sub), t, 0.0).astype(BF16)
        rhs = jnp.concatenate([ure_ref[...].reshape(GRID_W * sub, cols), uim_ref[...].reshape(GRID_W * sub, cols)],
                              axis=0).astype(BF16)
        a = _bdot(t, rhs)
        a_ref[:, :, pl.ds(pl.multiple_of(s * sub, sub), sub), :] = a.reshape(2, GRID_W, sub, cols)

    @pl.when(s >= DFT_BLOCKS)
    def _():
        k0 = pl.multiple_of((s - DFT_BLOCKS) * sub, sub)
        rhs = a_ref[:, pl.ds(k0, sub), :, :].reshape(2 * sub * GRID_W, cols).astype(BF16)
        y = _bdot(t2_ref[...].astype(BF16), rhs)
        o_ref[...] = y.reshape(GRID_W, sub, cols)


def _seq_dft_tables(n_seq):
    w, sub, nb = GRID_W, SUBLANES, DFT_BLOCKS
    ch = np.arange(nb).reshape(nb, 1, 1, 1)
    kb = np.arange(w).reshape(1, w, 1, 1)
    j = np.arange(sub).reshape(1, 1, sub, 1)
    r = np.arange(w).reshape(1, 1, 1, w)
    ang = (2.0 * np.pi / n_seq) * ((kb * (w * r + sub * ch + j)) % n_seq)
    c, s = np.cos(ang), np.sin(ang)
    rot = np.stack([np.stack([c, s], axis=3), np.stack([-s, c], axis=3)], axis=1)
    r1 = rot.reshape(nb, 2 * w * sub, 2 * w).astype(np.float32)
    expand = (np.arange(2 * w * sub)[None, :] // sub == np.arange(2 * w)[:, None]).astype(np.float32)
    ka = np.arange(w).reshape(w, 1)
    cp = np.arange(w).reshape(1, w)
    ang2 = (2.0 * np.pi / w) * ((ka * cp) % w)
    norm = 1.0 / math.sqrt(n_seq * FOURIER_GROUP_DIM)
    cs = np.stack([np.cos(ang2), np.sin(ang2)], axis=1) * norm
    eye = np.eye(sub)
    t2 = (cs[:, None, :, None, :] * eye[None, :, None, :, None]).reshape(w * sub, 2 * sub * w).astype(np.float32)
    return jnp.asarray(r1), jnp.asarray(expand).astype(BF16), jnp.asarray(t2)


def _seq_dft(u, batch, n_seq):
    assert n_seq == GRID_W * GRID_W
    w, sub, nb = GRID_W, SUBLANES, DFT_BLOCKS
    r1, expand, t2 = _seq_dft_tables(n_seq)
    halves = 2
    cols = FOURIER_DIM // halves
    u5 = u.reshape(batch, w, nb, sub, 2 * FOURIER_DIM)

    def u_spec(part):
        return pl.BlockSpec((None, w, None, sub, cols),
                            lambda b, h, s: (b, 0, jnp.minimum(s, nb - 1), 0, part * halves + h))

    y = pl.pallas_call(
        _seq_dft_kernel,
        grid=(batch, halves, 2 * nb),
        in_specs=[u_spec(0), u_spec(1),
                  pl.BlockSpec((None, 2 * w * sub, 2 * w), lambda b, h, s: (jnp.minimum(s, nb - 1), 0, 0)),
                  pl.BlockSpec((2 * w, 2 * w * sub), lambda b, h, s: (0, 0)),
                  pl.BlockSpec((w * sub, 2 * sub * w), lambda b, h, s: (0, 0))],
        out_specs=pl.BlockSpec((None, w, None, sub, cols), lambda b, h, s: (b, 0, jnp.maximum(s - nb, 0), 0, h)),
        out_shape=jax.ShapeDtypeStruct((batch, w, nb, sub, FOURIER_DIM), F32),
        scratch_shapes=[pltpu.VMEM((2, w, w, cols), F32)],
        compiler_params=_cparams(("arbitrary", "arbitrary", "arbitrary")),
        name="seq_dft",
    )(u5, u5, r1, expand, t2)
    return y.reshape(batch * n_seq, FOURIER_DIM)


def _out_router_kernel(x_ref, four_ref, attn_ref, gt1_ref, sh2_ref, sc2_ref, gf_ref, ga_ref, g2_ref,
                       wo_ref, wrh_ref, wrl_ref, br_ref,
                       xnew_ref, hm_ref, ri_ref, rw_ref, cnt_ref, carry_ref, *, mod_row):
    i = pl.program_id(0)
    tm = x_ref.shape[0]
    row = pl.ds(mod_row(i), 1)

    @pl.when(i == 0)
    def _():
        carry_ref[...] = jnp.zeros_like(carry_ref)

    def norm(v, g):
        return (v * lax.rsqrt(jnp.mean(v * v, axis=-1, keepdims=True) + EPS) * g).astype(BF16)

    mix = (_bdot(norm(four_ref[...], gf_ref[...]), wo_ref[0:FOURIER_DIM, :])
           + _bdot(norm(attn_ref[...], ga_ref[...]), wo_ref[FOURIER_DIM:FOURIER_DIM + MLA_DIM, :]))
    xn = x_ref[...] + gt1_ref[row, :] * mix
    xnew_ref[...] = xn
    ms = jnp.mean(xn * xn, axis=-1, keepdims=True)
    hm = xn * lax.rsqrt(ms + EPS) * (g2_ref[...] * (1.0 + sc2_ref[row, :])) + sh2_ref[row, :]
    hm_ref[...] = _pack_halves(hm)

    hm_hi = hm.astype(BF16)
    hm_lo = (hm - hm_hi.astype(F32)).astype(BF16)
    logits = _bdot(hm_hi, wrh_ref[...]) + _bdot(hm_lo, wrh_ref[...]) + _bdot(hm_hi, wrl_ref[...]) + br_ref[...]
    lane = lax.broadcasted_iota(jnp.int32, (tm, LANES), 1)
    lanef = lane.astype(F32)
    far = 1e9

    lg = jnp.where(lane < N_GROUPS, logits, NEG_BIG)
    m1 = jnp.max(lg, axis=-1, keepdims=True)
    g_p = 1.0 / jnp.sum(jnp.exp(lg - m1), axis=-1, keepdims=True)
    gidx = jnp.min(jnp.where(lg >= m1, lanef, far), axis=-1, keepdims=True)
    lo = N_GROUPS + EXPERTS_PER_GROUP * gidx
    in_group = jnp.where(lanef >= lo, jnp.where(lanef < lo + EXPERTS_PER_GROUP, 1.0, 0.0), 0.0) > 0.5
    le = jnp.where(in_group, logits, NEG_BIG)
    m2 = jnp.max(le, axis=-1, keepdims=True)
    idx1 = jnp.min(jnp.where(le >= m2, lanef, far), axis=-1, keepdims=True)
    le2 = jnp.where(lanef == idx1, NEG_BIG, le)
    m3 = jnp.max(le2, axis=-1, keepdims=True)
    idx2 = jnp.min(jnp.where(le2 >= m3, lanef, far), axis=-1, keepdims=True)
    t = jnp.exp(m3 - m2)
    p1 = 1.0 / (1.0 + t)
    p2 = t / (1.0 + t)
    e1 = idx1 - N_GROUPS
    e2 = idx2 - N_GROUPS

    oh1 = jnp.where(lanef == e1, 1.0, 0.0)
    oh2 = jnp.where(lanef == e2, 1.0, 0.0)
    ohs = oh1 + oh2
    row = lax.broadcasted_iota(jnp.int32, (tm, tm), 0)
    col = lax.broadcasted_iota(jnp.int32, (tm, tm), 1)
    tri = jnp.where(row > col, 1.0, 0.0).astype(BF16)
    before = _bdot(tri, ohs.astype(BF16)) + carry_ref[...]
    rank1 = jnp.sum(oh1 * before, axis=-1, keepdims=True)
    rank2 = jnp.sum(oh2 * before, axis=-1, keepdims=True)
    carry = carry_ref[...] + jnp.sum(ohs, axis=0, keepdims=True)
    carry_ref[...] = carry
    cnt_ref[...] = jnp.broadcast_to(carry, cnt_ref.shape)

    ri = jnp.where(lane == 0, e1, jnp.where(lane == 1, e2, jnp.where(lane == 2, rank1, jnp.where(lane == 3, rank2, 0.0))))
    ri_ref[...] = ri.astype(jnp.int32)
    rw_ref[...] = jnp.where(lane == 0, g_p * p1, jnp.where(lane == 1, g_p * p2, 0.0))


def _out_router(x2, four, attn, mods, gf, ga, g2, wo, wrh, wrl, br, *, seq, tm):
    t, d = x2.shape
    nt = t // tm
    tiles_per_b = seq // tm

    def rows(n):
        return pl.BlockSpec((tm, n), lambda i: (i, 0))

    def const(arr):
        return pl.BlockSpec(arr.shape, lambda i: (0,) * arr.ndim, pipeline_mode=pl.Buffered(1))

    return pl.pallas_call(
        functools.partial(_out_router_kernel, mod_row=lambda i: i // tiles_per_b),
        grid=(nt,),
        in_specs=[rows(d), rows(FOURIER_DIM), rows(MLA_DIM), _mod_spec(mods, 2, d), _mod_spec(mods, 3, d),
                  _mod_spec(mods, 4, d),
                  const(gf), const(ga), const(g2), const(wo), const(wrh), const(wrl), const(br)],
        out_specs=[rows(d), rows(d // 2), rows(LANES), rows(LANES), pl.BlockSpec((8, LANES), lambda i: (0, 0))],
        out_shape=[jax.ShapeDtypeStruct((t, d), F32), jax.ShapeDtypeStruct((t, d // 2), jnp.uint32),
                   jax.ShapeDtypeStruct((t, LANES), jnp.int32), jax.ShapeDtypeStruct((t, LANES), F32),
                   jax.ShapeDtypeStruct((8, LANES), F32)],
        scratch_shapes=[pltpu.VMEM((1, LANES), F32)],
        compiler_params=_cparams(("arbitrary",)),
        name="out_proj_router",
    )(x2, four, attn, mods, mods, mods, gf, ga, g2, wo, wrh, wrl, br)


def _dispatch_kernel(dest, seg_last, seg_blocks, used_blocks, hm_ref, xs_hbm, zbuf, sem, zsem):
    i = pl.program_id(0)
    tm = hm_ref.shape[0]

    @pl.when(i == 0)
    def _():
        zbuf[...] = jnp.zeros(zbuf.shape, zbuf.dtype)
        n_blocks = xs_hbm.shape[0] // MOE_BLOCK

        def zero_copy(blk):
            r0 = pl.multiple_of(blk * MOE_BLOCK, MOE_BLOCK)
            return pltpu.make_async_copy(zbuf, xs_hbm.at[pl.ds(r0, MOE_BLOCK)], zsem.at[0])

        def seg_start(e, carry):
            @pl.when(seg_blocks[e] > 0)
            def _():
                zero_copy(seg_last[e]).start()
            return carry

        def seg_wait(e, carry):
            @pl.when(seg_blocks[e] > 0)
            def _():
                zero_copy(seg_last[e]).wait()
            return carry

        def tail_start(blk, carry):
            zero_copy(blk).start()
            return carry

        def tail_wait(blk, carry):
            zero_copy(blk).wait()
            return carry

        lax.fori_loop(0, N_EXPERTS, seg_start, 0)
        lax.fori_loop(used_blocks[0], n_blocks, tail_start, 0)
        lax.fori_loop(0, N_EXPERTS, seg_wait, 0)
        lax.fori_loop(used_blocks[0], n_blocks, tail_wait, 0)

    def body(r8, carry):
        for u in range(GATHER_UNROLL):
            r = r8 * GATHER_UNROLL + u
            for k in range(2):
                pltpu.make_async_copy(hm_ref.at[pl.ds(r, 1)], xs_hbm.at[pl.ds(dest[2 * (i * tm + r) + k], 1)],
                                      sem.at[0]).start(priority=k)
        return carry

    lax.fori_loop(0, tm // GATHER_UNROLL, body, 0)
    for k in range(2):
        pltpu.make_async_copy(hm_ref, xs_hbm.at[pl.ds(0, tm)], sem.at[0]).wait()


def _dispatch(dest, seg_last, seg_blocks, used_blocks, hm, *, n_rows, tm):
    t, w = hm.shape
    grid_spec = pltpu.PrefetchScalarGridSpec(
        num_scalar_prefetch=4,
        grid=(t // tm,),
        in_specs=[pl.BlockSpec((tm, w), lambda i, *_: (i, 0))],
        out_specs=pl.BlockSpec(memory_space=pl.ANY),
        scratch_shapes=[pltpu.VMEM((MOE_BLOCK, w), hm.dtype), pltpu.SemaphoreType.DMA((1,)),
                        pltpu.SemaphoreType.DMA((1,))],
    )
    return pl.pallas_call(
        _dispatch_kernel,
        grid_spec=grid_spec,
        out_shape=jax.ShapeDtypeStruct((n_rows, w), hm.dtype),
        compiler_params=_cparams(("arbitrary",)),
        name="moe_dispatch",
    )(dest, seg_last, seg_blocks, used_blocks, hm)


def _moe_kernel(item_e, item_blk0, item_nblk, used_blocks,
                xs_hbm, wg_hbm, wu_hbm, wd_hbm, y_hbm,
                xg, xb, gs, ab, yp, gu_buf, dn_buf, gsem, osem, gusem, dnsem):
    i = pl.program_id(0)
    j = pl.program_id(1)
    n_items = pl.num_programs(0)
    nj = pl.num_programs(1)
    slot = i % 2
    nblk = item_nblk[i]

    def weight_copy(it, ph):
        if ph == 2:
            ws = it % DN_SLOTS
            return pltpu.make_async_copy(wd_hbm.at[item_e[it]], dn_buf.at[ws], dnsem.at[ws])
        ws = (2 * it + ph) % GU_SLOTS
        return pltpu.make_async_copy((wg_hbm, wu_hbm)[ph].at[item_e[it]], gu_buf.at[ws], gusem.at[ws])

    def start_weight(it, ph):
        it_c = jnp.minimum(it, n_items - 1)

        @pl.when(jnp.logical_and(it < n_items, item_nblk[it_c] > 0))
        def _():
            weight_copy(it_c, ph).start(priority=WEIGHT_DMA_PRIORITY)

    def x_copy(it, sl, m):
        r0 = pl.multiple_of(item_blk0[it] * MOE_BLOCK, MOE_BLOCK)
        return pltpu.make_async_copy(xs_hbm.at[pl.ds(r0, m)], xg.at[sl, pl.ds(0, m)], gsem.at[sl])

    def issue_gather(it, sl):
        for nb in range(1, ITEM_BLOCKS + 1):
            @pl.when(item_nblk[it] == nb)
            def _():
                x_copy(it, sl, nb * MOE_BLOCK).start(priority=ROW_DMA_PRIORITY)

    def wait_gather(it, sl):
        for nb in range(1, ITEM_BLOCKS + 1):
            @pl.when(item_nblk[it] == nb)
            def _():
                x_copy(it, sl, nb * MOE_BLOCK).wait()

    def out_copy(it, m):
        r0 = pl.multiple_of(item_blk0[it] * MOE_BLOCK, MOE_BLOCK)
        return pltpu.make_async_copy(yp.at[pl.ds(0, m)], y_hbm.at[pl.ds(r0, m)], osem.at[0])

    def wait_out(it):
        for nb in range(1, ITEM_BLOCKS + 1):
            @pl.when(item_nblk[it] == nb)
            def _():
                out_copy(it, nb * MOE_BLOCK).wait()

    @pl.when(j == 0)
    def _():
        @pl.when(i == 0)
        def _():
            for t in range(GU_SLOTS):
                start_weight(t // 2, t % 2)
            for k in range(DN_SLOTS - 1):
                start_weight(k, 2)
            issue_gather(0, 0)

        start_weight(i + DN_SLOTS - 1, 2)
        wait_gather(i, slot)

        @pl.when(i + 1 < n_items)
        def _():
            issue_gather(i + 1, 1 - slot)

    @pl.when(j == 1)
    def _():
        start_weight(i + GU_SLOTS // 2, 0)

    @pl.when(j == 2)
    def _():
        start_weight(i + GU_SLOTS // 2, 1)

        @pl.when(i > 0)
        def _():
            wait_out(i - 1)

    for ph in range(3):
        @pl.when(jnp.logical_and(j == ph, nblk > 0))
        def _():
            weight_copy(i, ph).wait()

    for nb in range(1, ITEM_BLOCKS + 1):
        m = nb * MOE_BLOCK

        @pl.when(jnp.logical_and(nblk == nb, j == 0))
        def _():
            hi, lo = _unpack_halves(xg[slot, 0:m, :])
            half = hi.shape[1]
            xb[0:m, 0:half] = hi.astype(BF16)
            xb[0:m, half:2 * half] = lo.astype(BF16)
            gs[0:m, :] = _bdot(xb[0:m, :], gu_buf[(2 * i) % GU_SLOTS].astype(BF16))

        @pl.when(jnp.logical_and(nblk == nb, j == 1))
        def _():
            g = gs[0:m, :]
            u = _bdot(xb[0:m, :], gu_buf[(2 * i + 1) % GU_SLOTS].astype(BF16))
            ab[0:m, :] = (g * jax.nn.sigmoid(g) * u).astype(BF16)

        @pl.when(jnp.logical_and(nblk == nb, j == 2))
        def _():
            yp[0:m, :] = _pack_halves(_bdot(ab[0:m, :], dn_buf[i % DN_SLOTS].astype(BF16)))
            out_copy(i, m).start()

    @pl.when(jnp.logical_and(i == n_items - 1, j == nj - 1))
    def _():
        wait_out(i)
        n_blocks = y_hbm.shape[0] // MOE_BLOCK
        yp[0:MOE_BLOCK, :] = jnp.zeros((MOE_BLOCK, yp.shape[1]), jnp.uint32)

        def tail_copy(blk):
            r0 = pl.multiple_of(blk * MOE_BLOCK, MOE_BLOCK)
            return pltpu.make_async_copy(yp.at[pl.ds(0, MOE_BLOCK)], y_hbm.at[pl.ds(r0, MOE_BLOCK)], osem.at[0])

        def start_body(blk, carry):
            tail_copy(blk).start()
            return carry

        def wait_body(blk, carry):
            tail_copy(blk).wait()
            return carry

        lax.fori_loop(used_blocks[0], n_blocks, start_body, 0)
        lax.fori_loop(used_blocks[0], n_blocks, wait_body, 0)


def _moe(xs, w_gate, w_up, w_down, item_e, item_blk0, item_nblk, used_blocks, *, n_rows):
    d, de = w_gate.shape[1], w_gate.shape[2]
    n_items = item_e.shape[0]
    nj = 3
    rows = ITEM_BLOCKS * MOE_BLOCK
    any_spec = pl.BlockSpec(memory_space=pl.ANY)
    grid_spec = pltpu.PrefetchScalarGridSpec(
        num_scalar_prefetch=4,
        grid=(n_items, nj),
        in_specs=[any_spec, any_spec, any_spec, any_spec],
        out_specs=any_spec,
        scratch_shapes=[pltpu.VMEM((2, rows, d // 2), jnp.uint32),
                        pltpu.VMEM((rows, d), BF16),
                        pltpu.VMEM((rows, de), F32),
                        pltpu.VMEM((rows, de), BF16),
                        pltpu.VMEM((rows, d // 2), jnp.uint32),
                        pltpu.VMEM((GU_SLOTS, d, de), F32),
                        pltpu.VMEM((DN_SLOTS, de, d), F32),
                        pltpu.SemaphoreType.DMA((2,)),
                        pltpu.SemaphoreType.DMA((1,)),
                        pltpu.SemaphoreType.DMA((GU_SLOTS,)),
                        pltpu.SemaphoreType.DMA((DN_SLOTS,))],
    )
    return pl.pallas_call(
        _moe_kernel,
        grid_spec=grid_spec,
        out_shape=jax.ShapeDtypeStruct((n_rows, d // 2), jnp.uint32),
        compiler_params=_cparams(("arbitrary", "arbitrary")),
        name="moe_experts",
    )(item_e, item_blk0, item_nblk, used_blocks, xs, w_gate, w_up, w_down)


def _combine_kernel(dest, x_ref, gt2_ref, rw_ref, y_hbm, o_ref, ybuf, sem, *, mod_row):
    i = pl.program_id(0)
    n = pl.num_programs(0)
    tm = x_ref.shape[0]
    slot = i % 2

    def issue(it, sl):
        base = it * tm

        def body(r4, carry):
            for rr in range(GATHER_UNROLL // 2):
                r = r4 * (GATHER_UNROLL // 2) + rr
                for k in range(2):
                    pltpu.make_async_copy(y_hbm.at[pl.ds(dest[2 * (base + r) + k], 1)], ybuf.at[sl, k, pl.ds(r, 1)],
                                          sem.at[sl]).start(priority=k)
            return carry

        lax.fori_loop(0, tm // (GATHER_UNROLL // 2), body, 0)

    @pl.when(i == 0)
    def _():
        issue(0, 0)

    for k in range(2):
        pltpu.make_async_copy(y_hbm.at[pl.ds(0, tm)], ybuf.at[slot, k], sem.at[slot]).wait()

    @pl.when(i + 1 < n)
    def _():
        issue(i + 1, 1 - slot)

    w = rw_ref[...]
    gate = gt2_ref[pl.ds(mod_row(i), 1), :]
    hi0, lo0 = _unpack_halves(ybuf[slot, 0])
    hi1, lo1 = _unpack_halves(ybuf[slot, 1])
    half = hi0.shape[1]
    o_ref[:, 0:half] = x_ref[:, 0:half] + gate[:, 0:half] * (w[:, 0:1] * hi0 + w[:, 1:2] * hi1)
    o_ref[:, half:2 * half] = (x_ref[:, half:2 * half]
                               + gate[:, half:2 * half] * (w[:, 0:1] * lo0 + w[:, 1:2] * lo1))


def _combine(dest, xnew, mods, rw, y, *, seq, tm):
    t, d = xnew.shape
    tiles_per_b = seq // tm
    grid_spec = pltpu.PrefetchScalarGridSpec(
        num_scalar_prefetch=1,
        grid=(t // tm,),
        in_specs=[pl.BlockSpec((tm, d), lambda i, ds: (i, 0)),
                  _mod_spec(mods, 5, d),
                  pl.BlockSpec((tm, LANES), lambda i, ds: (i, 0)),
                  pl.BlockSpec(memory_space=pl.ANY)],
        out_specs=pl.BlockSpec((tm, d), lambda i, ds: (i, 0)),
        scratch_shapes=[pltpu.VMEM((2, 2, tm, d // 2), jnp.uint32), pltpu.SemaphoreType.DMA((2,))],
    )
    return pl.pallas_call(
        functools.partial(_combine_kernel, mod_row=lambda i: i // tiles_per_b),
        grid_spec=grid_spec,
        out_shape=jax.ShapeDtypeStruct((t, d), F32),
        compiler_params=_cparams(("arbitrary",)),
        name="moe_combine",
    )(dest, xnew, mods, rw, y)


def _rope_tables(n_tokens):
    rows = n_tokens // GRID_W
    row = jnp.repeat(jnp.arange(rows, dtype=jnp.int32), GRID_W).astype(F32)
    col = jnp.tile(jnp.arange(GRID_W, dtype=jnp.int32), rows).astype(F32)
    n_freq = QK_ROPE_DIM // 4
    inv = ROPE_THETA ** (-jnp.arange(n_freq, dtype=F32) / n_freq)
    ar = row[:, None] * inv[None, :]
    ac = col[:, None] * inv[None, :]
    cr, sr, cc, sc = jnp.cos(ar), jnp.sin(ar), jnp.cos(ac), jnp.sin(ac)
    cos64 = jnp.concatenate([cr, cr, cc, cc], axis=-1)
    sin64 = jnp.concatenate([-sr, sr, -sc, sc], axis=-1)
    return jnp.tile(cos64, (1, 2)), jnp.tile(sin64, (1, 2))


def _channel_dft_table():
    c = np.arange(FOURIER_GROUP_DIM).reshape(-1, 1)
    k = np.arange(FOURIER_GROUP_DIM).reshape(1, -1)
    ang = (2.0 * np.pi / FOURIER_GROUP_DIM) * ((c * k) % FOURIER_GROUP_DIM)
    return jnp.asarray(np.concatenate([np.cos(ang), -np.sin(ang)], axis=1).astype(np.float32)).astype(BF16)


def _split_heads(w, widths):
    k = w.shape[0]
    wh = w.reshape(k, MLA_HEADS, sum(widths))
    outs, off = [], 0
    for wd in widths:
        outs.append(wh[:, :, off:off + wd].reshape(k, MLA_HEADS * wd))
        off += wd
    return outs


def kernel(x, c, ctx, c_ctx, w_ada, b_ada, g_norm1, g_norm2, w_in, g_q_a, g_kv_a, w_uq, w_ukv, g_qk_q, g_qk_k,
           g_out_four, g_out_attn, w_out, w_router_group, b_router_group, w_router_expert, b_router_expert,
           w_gate, w_up, w_down):
    b, s, d = x.shape
    lc = ctx.shape[1]
    t = b * s
    layer_params = (w_ada, b_ada, g_norm1, g_norm2, w_in, g_q_a, g_kv_a, w_uq, w_ukv, g_qk_q, g_qk_k, g_out_four,
                    g_out_attn, w_out, w_router_group, b_router_group, w_router_expert, b_router_expert,
                    w_gate, w_up, w_down)
    assert all(p.shape[0] == 1 for p in layer_params), "single-layer block"
    (w_ada, b_ada, g_norm1, g_norm2, w_in, g_q_a, g_kv_a, w_uq, w_ukv, g_qk_q, g_qk_k, g_out_four,
     g_out_attn, w_out, w_router_group, b_router_group, w_router_expert, b_router_expert,
     w_gate, w_up, w_down) = [p.reshape(p.shape[1:]) for p in layer_params]

    cond8 = jnp.concatenate([c, c_ctx[None, :], jnp.zeros((8 - b - 1, d), F32)], axis=0)
    mods = _ada_mod(cond8, w_ada, b_ada)

    w_all = jnp.concatenate([w_in, w_in[:, ROPE_OFF:]], axis=1).astype(BF16)
    dc = _channel_dft_table()
    wqn, wqr = [w.astype(BF16) for w in _split_heads(w_uq, (QK_NOPE_DIM, QK_ROPE_DIM))]
    wkn, wv = [w.astype(BF16) for w in _split_heads(w_ukv, (QK_NOPE_DIM, V_HEAD_DIM))]
    gqn = g_qk_q[:QK_NOPE_DIM].reshape(1, -1)
    gqr2 = jnp.tile(g_qk_q[QK_NOPE_DIM:], 2).reshape(1, -1)
    gkn = g_qk_k[:QK_NOPE_DIM].reshape(1, -1)
    gkr2 = jnp.tile(g_qk_k[QK_NOPE_DIM:], 2).reshape(1, -1)
    g1 = g_norm1.reshape(1, d)
    gq = g_q_a.reshape(1, -1)
    gkv = g_kv_a.reshape(1, -1)

    x2 = x.reshape(t, d)
    u, cq, ckv, kr2 = _in_proj(x2, mods, 0, s, g1, gq, gkv, w_all, dc, with_q=True, tm=IN_PROJ_ROWS)
    ckv_c, kr2_c = _in_proj(ctx.reshape(b * lc, d), mods, b, b * lc, g1, gq, gkv, w_all, dc, with_q=False, tm=lc)

    cos_t, sin_t = _rope_tables(s)
    q, kx, vx = _qkv(cq, ckv, kr2, cos_t, sin_t, wqn, wqr, gqn, gqr2, wkn, wv, gkn, gkr2,
                     batch=b, seq=s, tm=QKV_ROWS, with_q=True, with_rope=True)
    kc, vc = _qkv(None, ckv_c, kr2_c, None, None, None, None, None, None, wkn, wv, gkn, gkr2,
                  batch=b, seq=lc, tm=lc, with_q=False, with_rope=False)

    score_bound = (QK_HEAD_DIM ** 0.5 * LOG2_E * ATTN_BOUND_MARGIN) * jnp.max(jnp.abs(g_qk_q)) * jnp.max(jnp.abs(g_qk_k))
    score_bound = jnp.full((1, LANES), score_bound, F32)
    attn = _attention(score_bound, q, kx, kc, vx, vc, tq=ATTN_Q_ROWS).reshape(t, MLA_DIM)
    four = _seq_dft(u, b, s)

    wo = w_out.astype(BF16)
    n_route = N_GROUPS + N_EXPERTS
    wrt = jnp.concatenate([w_router_group, w_router_expert, jnp.zeros((d, LANES - n_route), F32)], axis=1)
    wrt_hi = wrt.astype(BF16)
    wrt_lo = (wrt - wrt_hi.astype(F32)).astype(BF16)
    brt = jnp.concatenate([b_router_group, b_router_expert, jnp.zeros((LANES - n_route,), F32)]).reshape(1, -1)
    xnew, hm, ri, rw, cnt = _out_router(x2, four, attn, mods, g_out_four.reshape(1, -1), g_out_attn.reshape(1, -1),
                                        g_norm2.reshape(1, d), wo, wrt_hi, wrt_lo, brt, seq=s, tm=OUT_ROUTER_ROWS)

    counts = cnt[0, :N_EXPERTS].astype(jnp.int32)
    nblk_e = (counts + MOE_BLOCK - 1) // MOE_BLOCK
    blk_end = jnp.cumsum(nblk_e)
    blk_start = blk_end - nblk_e
    n_slots = t * 2
    n_blocks = -(-(n_slots + N_EXPERTS * (MOE_BLOCK - 1)) // MOE_BLOCK)
    n_rows = n_blocks * MOE_BLOCK
    e12 = ri[:, 0:2]
    seg_start = jnp.sum(jnp.where(e12[:, :, None] == jnp.arange(N_EXPERTS, dtype=jnp.int32), blk_start, 0), axis=-1)
    dest = (seg_start * MOE_BLOCK + ri[:, 2:4]).reshape(-1).astype(jnp.int32)
    items_e = (nblk_e + ITEM_BLOCKS - 1) // ITEM_BLOCKS
    item_end = jnp.cumsum(items_e)
    n_items = (n_blocks + (ITEM_BLOCKS - 1) * N_EXPERTS) // ITEM_BLOCKS
    idx = jnp.arange(n_items, dtype=jnp.int32)
    total = item_end[-1]
    idx_c = jnp.minimum(idx, total - 1)
    ie = jnp.minimum(jnp.sum(idx_c[:, None] >= item_end[None, :], axis=1), N_EXPERTS - 1).astype(jnp.int32)
    local = idx_c - (item_end - items_e)[ie]
    item_blk0 = (blk_start[ie] + ITEM_BLOCKS * local).astype(jnp.int32)
    item_nblk = jnp.where(idx < total, jnp.clip(nblk_e[ie] - ITEM_BLOCKS * local, 0, ITEM_BLOCKS), 0).astype(jnp.int32)

    used_blocks = blk_end[-1:].astype(jnp.int32)
    seg_last = jnp.maximum(blk_end - 1, 0).astype(jnp.int32)
    xs = _dispatch(dest, seg_last, nblk_e.astype(jnp.int32), used_blocks, hm, n_rows=n_rows, tm=DISPATCH_ROWS)
    y = _moe(xs, w_gate, w_up, w_down, ie, item_blk0, item_nblk, used_blocks, n_rows=n_rows)
    out = _combine(dest, xnew, mods, rw, y, seq=s, tm=COMBINE_ROWS)
    return out.reshape(b, s, d)
```

```python
import functools
import math

import numpy as np
import jax
import jax.numpy as jnp
from jax import lax
from jax.experimental import pallas as pl
from jax.experimental.pallas import tpu as pltpu

F32 = jnp.float32
BF16 = jnp.bfloat16

D_MODEL = 2048
GRID_W = 64
EPS = 1e-6
N_MOD = 6
N_FOURIER_GROUPS = 4
FOURIER_GROUP_DIM = 256
FOURIER_DIM = 1024
MLA_HEADS = 8
QK_NOPE_DIM = 128
QK_ROPE_DIM = 64
QK_HEAD_DIM = 192
V_HEAD_DIM = 128
Q_LORA_RANK = 768
KV_LORA_RANK = 512
MLA_DIM = 1024
ROPE_THETA = 10000.0
Q_OFF = FOURIER_DIM
KV_OFF = Q_OFF + Q_LORA_RANK
ROPE_OFF = KV_OFF + KV_LORA_RANK
N_GROUPS = 8
EXPERTS_PER_GROUP = 8
N_EXPERTS = 64
D_EXPERT = 768

ROT_HALF = QK_ROPE_DIM // 4
ROT_BLOCK = 2 * ROT_HALF

ADA_COLS = 1024
IN_PROJ_ROWS = 512
QKV_ROWS = 512
ATTN_Q_ROWS = 2048
OUT_ROUTER_ROWS = 512
COMBINE_ROWS = 512
DISPATCH_ROWS = 1024

LANES = 128
SUBLANES = 8
DFT_BLOCKS = GRID_W // SUBLANES
HEAD_PAD = 256
V_PAD = 256
MOE_BLOCK = 64
ITEM_BLOCKS = 8
GATHER_UNROLL = 16
GU_SLOTS = 3
DN_SLOTS = 2
ATTN_KEY_CHUNK = 256
ATTN_BOUND_MARGIN = 1.02
ATTN_MIN_ROW_SUM = 1e-30
LOG2_E = math.log2(math.e)
ROW_DMA_PRIORITY = 0
WEIGHT_DMA_PRIORITY = 1
VMEM_LIMIT = 56 * 1024 * 1024
NEG_BIG = -1e30


def _cparams(sem):
    return pltpu.CompilerParams(dimension_semantics=sem, vmem_limit_bytes=VMEM_LIMIT)


def _bdot(a, b):
    return jnp.dot(a, b, preferred_element_type=F32)


def _pack_halves(v):
    n = v.shape[1] // 2
    hi = pltpu.bitcast(v[:, :n].astype(BF16).astype(F32), jnp.uint32)
    lo = pltpu.bitcast(v[:, n:].astype(BF16).astype(F32), jnp.uint32)
    return hi | (lo >> 16)


def _unpack_halves(u):
    hi = pltpu.bitcast(u & jnp.uint32(0xFFFF0000), F32)
    lo = pltpu.bitcast(u << 16, F32)
    return hi, lo


def _ada_kernel(c_ref, w_ref, b_ref, o_ref):
    c = c_ref[...]
    s = (c * jax.nn.sigmoid(c)).astype(BF16)
    o_ref[...] = _bdot(s, w_ref[...].astype(BF16)) + b_ref[...]


def _ada_mod(cond8, w_ada, b_ada):
    d, n = w_ada.shape
    tn = ADA_COLS
    return pl.pallas_call(
        _ada_kernel,
        grid=(n // tn,),
        in_specs=[pl.BlockSpec((8, d), lambda i: (0, 0)),
                  pl.BlockSpec((d, tn), lambda i: (0, i)),
                  pl.BlockSpec((1, tn), lambda i: (0, i))],
        out_specs=pl.BlockSpec((8, tn), lambda i: (0, i)),
        out_shape=jax.ShapeDtypeStruct((8, n), F32),
        compiler_params=_cparams(("arbitrary",)),
        name="ada_mod",
    )(cond8, w_ada, b_ada.reshape(1, n))


def _in_proj_kernel(x_ref, sh_ref, sc_ref, g_ref, gq_ref, gkv_ref, w_ref, dc_ref, *out_refs, with_q, mod_row):
    row = pl.ds(mod_row(pl.program_id(0)), 1)
    x = x_ref[...]
    ms = jnp.mean(x * x, axis=-1, keepdims=True)
    a = g_ref[...] * (1.0 + sc_ref[row, :])
    h = (x * lax.rsqrt(ms + EPS) * a + sh_ref[row, :]).astype(BF16)
    if with_q:
        u_ref, cq_ref, ckv_ref, kr_ref = out_refs
        f = _bdot(h, w_ref[:, 0:Q_OFF]).astype(BF16)
        dc = dc_ref[...]
        for g in range(N_FOURIER_GROUPS):
            lo = g * FOURIER_GROUP_DIM
            ug = _bdot(f[:, lo:lo + FOURIER_GROUP_DIM], dc)
            u_ref[:, lo:lo + FOURIER_GROUP_DIM] = ug[:, :FOURIER_GROUP_DIM]
            u_ref[:, FOURIER_DIM + lo:FOURIER_DIM + lo + FOURIER_GROUP_DIM] = ug[:, FOURIER_GROUP_DIM:]
        pq = _bdot(h, w_ref[:, Q_OFF:KV_OFF])
        msq = jnp.mean(pq * pq, axis=-1, keepdims=True)
        cq_ref[...] = (pq * lax.rsqrt(msq + EPS) * gq_ref[...]).astype(BF16)
    else:
        ckv_ref, kr_ref = out_refs
    pkv = _bdot(h, w_ref[:, KV_OFF:ROPE_OFF])
    mskv = jnp.mean(pkv * pkv, axis=-1, keepdims=True)
    ckv_ref[...] = (pkv * lax.rsqrt(mskv + EPS) * gkv_ref[...]).astype(BF16)
    kr_ref[...] = _bdot(h, w_ref[:, ROPE_OFF:ROPE_OFF + LANES])


def _mod_spec(mods, k, d):
    return pl.BlockSpec((mods.shape[0], d), lambda i, *_: (0, k))


def _in_proj(x2, mods, first_row, rows_per_mod, g1, gq, gkv, w_all, dc, *, with_q, tm):
    t, d = x2.shape
    nt = t // tm
    tiles_per_mod = rows_per_mod // tm

    def const(shape):
        return pl.BlockSpec(shape, lambda i: (0,) * len(shape))

    in_specs = [pl.BlockSpec((tm, d), lambda i: (i, 0)), _mod_spec(mods, 0, d), _mod_spec(mods, 1, d),
                const((1, d)), const((1, Q_LORA_RANK)), const((1, KV_LORA_RANK)),
                const(w_all.shape), const(dc.shape)]

    def rows(n):
        return pl.BlockSpec((tm, n), lambda i: (i, 0))

    out_specs = [rows(KV_LORA_RANK), rows(LANES)]
    out_shape = [jax.ShapeDtypeStruct((t, KV_LORA_RANK), BF16), jax.ShapeDtypeStruct((t, LANES), F32)]
    if with_q:
        out_specs = [rows(2 * FOURIER_DIM), rows(Q_LORA_RANK)] + out_specs
        out_shape = [jax.ShapeDtypeStruct((t, 2 * FOURIER_DIM), F32),
                     jax.ShapeDtypeStruct((t, Q_LORA_RANK), BF16)] + out_shape
    return pl.pallas_call(
        functools.partial(_in_proj_kernel, with_q=with_q, mod_row=lambda i: first_row + i // tiles_per_mod),
        grid=(nt,),
        in_specs=in_specs,
        out_specs=out_specs,
        out_shape=out_shape,
        compiler_params=_cparams(("arbitrary",)),
        name="in_proj_x" if with_q else "in_proj_ctx",
    )(x2, mods, mods, g1, gq, gkv, w_all, dc)


def _swap_halves(y, first_half):
    return jnp.where(first_half, pltpu.roll(y, LANES - ROT_HALF, 1), pltpu.roll(y, ROT_HALF, 1))


def _qkv_kernel(*refs, with_q, with_rope):
    it = iter(refs)
    if with_q:
        cq_ref = next(it)
    ckv_ref = next(it)
    kr_ref = next(it)
    if with_rope:
        cos_ref = next(it)
        sin_ref = next(it)
    if with_q:
        wqn_ref = next(it)
        wqr_ref = next(it)
        gqn_ref = next(it)
        gqr_ref = next(it)
    wkn_ref = next(it)
    wv_ref = next(it)
    gkn_ref = next(it)
    gkr_ref = next(it)
    if with_q:
        q_ref = next(it)
    k_ref = next(it)
    v_ref = next(it)

    tm = ckv_ref.shape[0]
    lane = lax.broadcasted_iota(jnp.int32, (tm, LANES), 1)
    low = lane < QK_ROPE_DIM
    first_half = (lane % ROT_BLOCK) < ROT_HALF
    inv_dim = 1.0 / QK_HEAD_DIM

    def rope(y):
        if not with_rope:
            return y
        return y * cos_ref[...] + _swap_halves(y, first_half) * sin_ref[...]

    if with_q:
        cq = cq_ref[...]
        qn = _bdot(cq, wqn_ref[...])
        qr = _bdot(cq, wqr_ref[...])
        qscale = QK_HEAD_DIM ** -0.5 * LOG2_E
        for p in range(MLA_HEADS // 2):
            blk = qr[:, p * LANES:(p + 1) * LANES]
            sq = blk * blk
            ss_lo = jnp.sum(jnp.where(low, sq, 0.0), axis=-1, keepdims=True)
            ss_hi = jnp.sum(jnp.where(low, 0.0, sq), axis=-1, keepdims=True)
            scales = []
            for hh, ssr in ((2 * p, ss_lo), (2 * p + 1, ss_hi)):
                nh = qn[:, hh * LANES:(hh + 1) * LANES]
                ssq = jnp.sum(nh * nh, axis=-1, keepdims=True) + ssr
                s = lax.rsqrt(ssq * inv_dim + EPS)
                scales.append(s)
                q_ref[hh, :, 0:LANES] = (nh * s * gqn_ref[...] * qscale).astype(BF16)
            s_pair = jnp.where(low, scales[0], scales[1])
            r = rope(blk * s_pair * gqr_ref[...]) * qscale
            q_ref[2 * p, :, LANES:2 * LANES] = jnp.where(low, r, 0.0).astype(BF16)
            q_ref[2 * p + 1, :, LANES:2 * LANES] = jnp.where(low, pltpu.roll(r, QK_ROPE_DIM, 1), 0.0).astype(BF16)

    ckv = ckv_ref[...]
    kn = _bdot(ckv, wkn_ref[...])
    v = _bdot(ckv, wv_ref[...])
    kr = kr_ref[...]
    ss_r = jnp.sum(jnp.where(low, kr * kr, 0.0), axis=-1, keepdims=True)
    base = rope(kr * gkr_ref[...])
    ones_col = jnp.where(lane == 0, 1.0, 0.0).astype(BF16)
    for hh in range(MLA_HEADS):
        nh = kn[:, hh * LANES:(hh + 1) * LANES]
        ssq = jnp.sum(nh * nh, axis=-1, keepdims=True) + ss_r
        s = lax.rsqrt(ssq * inv_dim + EPS)
        k_ref[hh, :, 0:LANES] = (nh * s * gkn_ref[...]).astype(BF16)
        k_ref[hh, :, LANES:2 * LANES] = jnp.where(low, base * s, 0.0).astype(BF16)
        v_ref[hh, :, 0:LANES] = v[:, hh * LANES:(hh + 1) * LANES].astype(BF16)
        v_ref[hh, :, LANES:2 * LANES] = ones_col


def _qkv(cq, ckv, kr2, cos_t, sin_t, wqn, wqr, gqn, gqr2, wkn, wv, gkn, gkr2, *, batch, seq, tm, with_q,
         with_rope):
    t = ckv.shape[0]
    nt = t // tm
    tiles_per_b = seq // tm

    def rows(n):
        return pl.BlockSpec((tm, n), lambda i: (i, 0))

    def const(arr):
        return pl.BlockSpec(arr.shape, lambda i: (0,) * arr.ndim)

    tab_spec = pl.BlockSpec((tm, LANES), lambda i: (i % tiles_per_b, 0))

    def head_out(width):
        return pl.BlockSpec((None, MLA_HEADS, tm, width), lambda i: (i // tiles_per_b, 0, i % tiles_per_b, 0))

    args, in_specs = [], []
    if with_q:
        args.append(cq)
        in_specs.append(rows(Q_LORA_RANK))
    args += [ckv, kr2]
    in_specs += [rows(KV_LORA_RANK), rows(LANES)]
    if with_rope:
        args += [cos_t, sin_t]
        in_specs += [tab_spec, tab_spec]
    if with_q:
        args += [wqn, wqr, gqn, gqr2]
        in_specs += [const(wqn), const(wqr), const(gqn), const(gqr2)]
    args += [wkn, wv, gkn, gkr2]
    in_specs += [const(wkn), const(wv), const(gkn), const(gkr2)]

    out_specs = [head_out(HEAD_PAD), head_out(V_PAD)]
    out_shape = [jax.ShapeDtypeStruct((batch, MLA_HEADS, seq, HEAD_PAD), BF16),
                 jax.ShapeDtypeStruct((batch, MLA_HEADS, seq, V_PAD), BF16)]
    if with_q:
        out_specs = [head_out(HEAD_PAD)] + out_specs
        out_shape = [jax.ShapeDtypeStruct((batch, MLA_HEADS, seq, HEAD_PAD), BF16)] + out_shape
    return pl.pallas_call(
        functools.partial(_qkv_kernel, with_q=with_q, with_rope=with_rope),
        grid=(nt,),
        in_specs=in_specs,
        out_specs=out_specs,
        out_shape=out_shape,
        compiler_params=_cparams(("arbitrary",)),
        name="qkv_x" if with_q else "kv_ctx",
    )(*args)


def _attn_kernel(bound_ref, q_ref, kx_ref, kc_ref, vx_ref, vc_ref, o_ref):
    q = q_ref[...]
    tq = q.shape[0]
    dn = (((1,), (1,)), ((), ()))
    chunks = [(kx_ref, vx_ref, c * ATTN_KEY_CHUNK, ATTN_KEY_CHUNK) for c in range(kx_ref.shape[0] // ATTN_KEY_CHUNK)]
    chunks.append((kc_ref, vc_ref, 0, kc_ref.shape[0]))

    def scores(k_ref, lo, n):
        return lax.dot_general(q, k_ref[lo:lo + n, :], dn, preferred_element_type=F32)

    def finish(acc):
        o_ref[...] = acc[:, :V_HEAD_DIM] / acc[:, V_HEAD_DIM:V_HEAD_DIM + 1]

    bound = bound_ref[0:1, 0:1]
    acc = jnp.zeros((tq, V_PAD), F32)
    for k_ref, v_ref, lo, n in chunks:
        p = jnp.exp2(scores(k_ref, lo, n) - bound).astype(BF16)
        acc = acc + _bdot(p, v_ref[lo:lo + n, :])
    finish(acc)

    row_sum_ok = jnp.min(acc[:, V_HEAD_DIM:V_HEAD_DIM + 1]) >= ATTN_MIN_ROW_SUM

    @pl.when(jnp.logical_not(row_sum_ok))
    def _():
        m = jnp.full((tq, 1), NEG_BIG, F32)
        acc2 = jnp.zeros((tq, V_PAD), F32)
        for k_ref, v_ref, lo, n in chunks:
            s = scores(k_ref, lo, n)
            m_new = jnp.maximum(m, jnp.max(s, axis=-1, keepdims=True))
            p = jnp.exp2(s - m_new).astype(BF16)
            acc2 = jnp.exp2(m - m_new) * acc2 + _bdot(p, v_ref[lo:lo + n, :])
            m = m_new
        finish(acc2)


def _attention(score_bound, q, kx, kc, vx, vc, *, tq):
    b, h, s, _ = q.shape
    lc = kc.shape[2]
    return pl.pallas_call(
        _attn_kernel,
        grid=(b, h, s // tq),
        in_specs=[pl.BlockSpec((1, LANES), lambda bi, hi, qi: (0, 0)),
                  pl.BlockSpec((None, None, tq, HEAD_PAD), lambda bi, hi, qi: (bi, hi, qi, 0)),
                  pl.BlockSpec((None, None, s, HEAD_PAD), lambda bi, hi, qi: (bi, hi, 0, 0)),
                  pl.BlockSpec((None, None, lc, HEAD_PAD), lambda bi, hi, qi: (bi, hi, 0, 0)),
                  pl.BlockSpec((None, None, s, V_PAD), lambda bi, hi, qi: (bi, hi, 0, 0)),
                  pl.BlockSpec((None, None, lc, V_PAD), lambda bi, hi, qi: (bi, hi, 0, 0))],
        out_specs=pl.BlockSpec((None, tq, V_HEAD_DIM), lambda bi, hi, qi: (bi, qi, hi)),
        out_shape=jax.ShapeDtypeStruct((b, s, h * V_HEAD_DIM), F32),
        compiler_params=_cparams(("arbitrary", "arbitrary", "arbitrary")),
        name="attention",
    )(score_bound, q, kx, kc, vx, vc)


def _seq_dft_kernel(ure_ref, uim_ref, r_ref, e_ref, t2_ref, o_ref, a_ref):
    s = pl.program_id(2)
    sub = SUBLANES
    n = 2 * GRID_W * sub
    cols = ure_ref.shape[-1]

    @pl.when(s < DFT_BLOCKS)
    def _():
        t = _bdot(r_ref[...].astype(BF16), e_ref[...])
        row = lax.broadcasted_iota(jnp.int32, (n, n), 0)
        col = lax.broadcasted_iota(jnp.int32, (n, n), 1)
        t = jnp.where((row % sub) == (col % sub), t, 0.0).astype(BF16)
        rhs = jnp.concatenate([ure_ref[...].reshape(GRID_W * sub, cols), uim_ref[...].reshape(GRID_W * sub, cols)],
                              axis=0).astype(BF16)
        a = _bdot(t, rhs)
        a_ref[:, :, pl.ds(pl.multiple_of(s * sub, sub), sub), :] = a.reshape(2, GRID_W, sub, cols)

    @pl.when(s >= DFT_BLOCKS)
    def _():
        k0 = pl.multiple_of((s - DFT_BLOCKS) * sub, sub)
        rhs = a_ref[:, pl.ds(k0, sub), :, :].reshape(2 * sub * GRID_W, cols).astype(BF16)
        y = _bdot(t2_ref[...].astype(BF16), rhs)
        o_ref[...] = y.reshape(GRID_W, sub, cols)


def _seq_dft_tables(n_seq):
    w, sub, nb = GRID_W, SUBLANES, DFT_BLOCKS
    ch = np.arange(nb).reshape(nb, 1, 1, 1)
    kb = np.arange(w).reshape(1, w, 1, 1)
    j = np.arange(sub).reshape(1, 1, sub, 1)
    r = np.arange(w).reshape(1, 1, 1, w)
    ang = (2.0 * np.pi / n_seq) * ((kb * (w * r + sub * ch + j)) % n_seq)
    c, s = np.cos(ang), np.sin(ang)
    rot = np.stack([np.stack([c, s], axis=3), np.stack([-s, c], axis=3)], axis=1)
    r1 = rot.reshape(nb, 2 * w * sub, 2 * w).astype(np.float32)
    expand = (np.arange(2 * w * sub)[None, :] // sub == np.arange(2 * w)[:, None]).astype(np.float32)
    ka = np.arange(w).reshape(w, 1)
    cp = np.arange(w).reshape(1, w)
    ang2 = (2.0 * np.pi / w) * ((ka * cp) % w)
    norm = 1.0 / math.sqrt(n_seq * FOURIER_GROUP_DIM)
    cs = np.stack([np.cos(ang2), np.sin(ang2)], axis=1) * norm
    eye = np.eye(sub)
    t2 = (cs[:, None, :, None, :] * eye[None, :, None, :, None]).reshape(w * sub, 2 * sub * w).astype(np.float32)
    return jnp.asarray(r1), jnp.asarray(expand).astype(BF16), jnp.asarray(t2)


def _seq_dft(u, batch, n_seq):
    assert n_seq == GRID_W * GRID_W
    w, sub, nb = GRID_W, SUBLANES, DFT_BLOCKS
    r1, expand, t2 = _seq_dft_tables(n_seq)
    halves = 2
    cols = FOURIER_DIM // halves
    u5 = u.reshape(batch, w, nb, sub, 2 * FOURIER_DIM)

    def u_spec(part):
        return pl.BlockSpec((None, w, None, sub, cols),
                            lambda b, h, s: (b, 0, jnp.minimum(s, nb - 1), 0, part * halves + h))

    y = pl.pallas_call(
        _seq_dft_kernel,
        grid=(batch, halves, 2 * nb),
        in_specs=[u_spec(0), u_spec(1),
                  pl.BlockSpec((None, 2 * w * sub, 2 * w), lambda b, h, s: (jnp.minimum(s, nb - 1), 0, 0)),
                  pl.BlockSpec((2 * w, 2 * w * sub), lambda b, h, s: (0, 0)),
                  pl.BlockSpec((w * sub, 2 * sub * w), lambda b, h, s: (0, 0))],
        out_specs=pl.BlockSpec((None, w, None, sub, cols), lambda b, h, s: (b, 0, jnp.maximum(s - nb, 0), 0, h)),
        out_shape=jax.ShapeDtypeStruct((batch, w, nb, sub, FOURIER_DIM), F32),
        scratch_shapes=[pltpu.VMEM((2, w, w, cols), F32)],
        compiler_params=_cparams(("arbitrary", "arbitrary", "arbitrary")),
        name="seq_dft",
    )(u5, u5, r1, expand, t2)
    return y.reshape(batch * n_seq, FOURIER_DIM)


def _out_router_kernel(x_ref, four_ref, attn_ref, gt1_ref, sh2_ref, sc2_ref, gf_ref, ga_ref, g2_ref,
                       wo_ref, wrh_ref, wrl_ref, br_ref,
                       xnew_ref, hm_ref, ri_ref, rw_ref, cnt_ref, carry_ref, *, mod_row):
    i = pl.program_id(0)
    tm = x_ref.shape[0]
    row = pl.ds(mod_row(i), 1)

    @pl.when(i == 0)
    def _():
        carry_ref[...] = jnp.zeros_like(carry_ref)

    def norm(v, g):
        return (v * lax.rsqrt(jnp.mean(v * v, axis=-1, keepdims=True) + EPS) * g).astype(BF16)

    mix = (_bdot(norm(four_ref[...], gf_ref[...]), wo_ref[0:FOURIER_DIM, :])
           + _bdot(norm(attn_ref[...], ga_ref[...]), wo_ref[FOURIER_DIM:FOURIER_DIM + MLA_DIM, :]))
    xn = x_ref[...] + gt1_ref[row, :] * mix
    xnew_ref[...] = xn
    ms = jnp.mean(xn * xn, axis=-1, keepdims=True)
    hm = xn * lax.rsqrt(ms + EPS) * (g2_ref[...] * (1.0 + sc2_ref[row, :])) + sh2_ref[row, :]
    hm_ref[...] = _pack_halves(hm)

    hm_hi = hm.astype(BF16)
    hm_lo = (hm - hm_hi.astype(F32)).astype(BF16)
    logits = _bdot(hm_hi, wrh_ref[...]) + _bdot(hm_lo, wrh_ref[...]) + _bdot(hm_hi, wrl_ref[...]) + br_ref[...]
    lane = lax.broadcasted_iota(jnp.int32, (tm, LANES), 1)
    lanef = lane.astype(F32)
    far = 1e9

    lg = jnp.where(lane < N_GROUPS, logits, NEG_BIG)
    m1 = jnp.max(lg, axis=-1, keepdims=True)
    g_p = 1.0 / jnp.sum(jnp.exp(lg - m1), axis=-1, keepdims=True)
    gidx = jnp.min(jnp.where(lg >= m1, lanef, far), axis=-1, keepdims=True)
    lo = N_GROUPS + EXPERTS_PER_GROUP * gidx
    in_group = jnp.where(lanef >= lo, jnp.where(lanef < lo + EXPERTS_PER_GROUP, 1.0, 0.0), 0.0) > 0.5
    le = jnp.where(in_group, logits, NEG_BIG)
    m2 = jnp.max(le, axis=-1, keepdims=True)
    idx1 = jnp.min(jnp.where(le >= m2, lanef, far), axis=-1, keepdims=True)
    le2 = jnp.where(lanef == idx1, NEG_BIG, le)
    m3 = jnp.max(le2, axis=-1, keepdims=True)
    idx2 = jnp.min(jnp.where(le2 >= m3, lanef, far), axis=-1, keepdims=True)
    t = jnp.exp(m3 - m2)
    p1 = 1.0 / (1.0 + t)
    p2 = t / (1.0 + t)
    e1 = idx1 - N_GROUPS
    e2 = idx2 - N_GROUPS

    oh1 = jnp.where(lanef == e1, 1.0, 0.0)
    oh2 = jnp.where(lanef == e2, 1.0, 0.0)
    ohs = oh1 + oh2
    row = lax.broadcasted_iota(jnp.int32, (tm, tm), 0)
    col = lax.broadcasted_iota(jnp.int32, (tm, tm), 1)
    tri = jnp.where(row > col, 1.0, 0.0).astype(BF16)
    before = _bdot(tri, ohs.astype(BF16)) + carry_ref[...]
    rank1 = jnp.sum(oh1 * before, axis=-1, keepdims=True)
    rank2 = jnp.sum(oh2 * before, axis=-1, keepdims=True)
    carry = carry_ref[...] + jnp.sum(ohs, axis=0, keepdims=True)
    carry_ref[...] = carry
    cnt_ref[...] = jnp.broadcast_to(carry, cnt_ref.shape)

    ri = jnp.where(lane == 0, e1, jnp.where(lane == 1, e2, jnp.where(lane == 2, rank1, jnp.where(lane == 3, rank2, 0.0))))
    ri_ref[...] = ri.astype(jnp.int32)
    rw_ref[...] = jnp.where(lane == 0, g_p * p1, jnp.where(lane == 1, g_p * p2, 0.0))


def _out_router(x2, four, attn, mods, gf, ga, g2, wo, wrh, wrl, br, *, seq, tm):
    t, d = x2.shape
    nt = t // tm
    tiles_per_b = seq // tm

    def rows(n):
        return pl.BlockSpec((tm, n), lambda i: (i, 0))

    def const(arr):
        return pl.BlockSpec(arr.shape, lambda i: (0,) * arr.ndim, pipeline_mode=pl.Buffered(1))

    return pl.pallas_call(
        functools.partial(_out_router_kernel, mod_row=lambda i: i // tiles_per_b),
        grid=(nt,),
        in_specs=[rows(d), rows(FOURIER_DIM), rows(MLA_DIM), _mod_spec(mods, 2, d), _mod_spec(mods, 3, d),
                  _mod_spec(mods, 4, d),
                  const(gf), const(ga), const(g2), const(wo), const(wrh), const(wrl), const(br)],
        out_specs=[rows(d), rows(d // 2), rows(LANES), rows(LANES), pl.BlockSpec((8, LANES), lambda i: (0, 0))],
        out_shape=[jax.ShapeDtypeStruct((t, d), F32), jax.ShapeDtypeStruct((t, d // 2), jnp.uint32),
                   jax.ShapeDtypeStruct((t, LANES), jnp.int32), jax.ShapeDtypeStruct((t, LANES), F32),
                   jax.ShapeDtypeStruct((8, LANES), F32)],
        scratch_shapes=[pltpu.VMEM((1, LANES), F32)],
        compiler_params=_cparams(("arbitrary",)),
        name="out_proj_router",
    )(x2, four, attn, mods, mods, mods, gf, ga, g2, wo, wrh, wrl, br)


def _dispatch_kernel(dest, seg_last, seg_blocks, used_blocks, hm_ref, xs_hbm, zbuf, sem, zsem):
    i = pl.program_id(0)
    tm = hm_ref.shape[0]

    @pl.when(i == 0)
    def _():
        zbuf[...] = jnp.zeros(zbuf.shape, zbuf.dtype)
        n_blocks = xs_hbm.shape[0] // MOE_BLOCK

        def zero_copy(blk):
            r0 = pl.multiple_of(blk * MOE_BLOCK, MOE_BLOCK)
            return pltpu.make_async_copy(zbuf, xs_hbm.at[pl.ds(r0, MOE_BLOCK)], zsem.at[0])

        def seg_start(e, carry):
            @pl.when(seg_blocks[e] > 0)
            def _():
                zero_copy(seg_last[e]).start()
            return carry

        def seg_wait(e, carry):
            @pl.when(seg_blocks[e] > 0)
            def _():
                zero_copy(seg_last[e]).wait()
            return carry

        def tail_start(blk, carry):
            zero_copy(blk).start()
            return carry

        def tail_wait(blk, carry):
            zero_copy(blk).wait()
            return carry

        lax.fori_loop(0, N_EXPERTS, seg_start, 0)
        lax.fori_loop(used_blocks[0], n_blocks, tail_start, 0)
        lax.fori_loop(0, N_EXPERTS, seg_wait, 0)
        lax.fori_loop(used_blocks[0], n_blocks, tail_wait, 0)

    def body(r8, carry):
        for u in range(GATHER_UNROLL):
            r = r8 * GATHER_UNROLL + u
            for k in range(2):
                pltpu.make_async_copy(hm_ref.at[pl.ds(r, 1)], xs_hbm.at[pl.ds(dest[2 * (i * tm + r) + k], 1)],
                                      sem.at[0]).start(priority=k)
        return carry

    lax.fori_loop(0, tm // GATHER_UNROLL, body, 0)
    for k in range(2):
        pltpu.make_async_copy(hm_ref, xs_hbm.at[pl.ds(0, tm)], sem.at[0]).wait()


def _dispatch(dest, seg_last, seg_blocks, used_blocks, hm, *, n_rows, tm):
    t, w = hm.shape
    grid_spec = pltpu.PrefetchScalarGridSpec(
        num_scalar_prefetch=4,
        grid=(t // tm,),
        in_specs=[pl.BlockSpec((tm, w), lambda i, *_: (i, 0))],
        out_specs=pl.BlockSpec(memory_space=pl.ANY),
        scratch_shapes=[pltpu.VMEM((MOE_BLOCK, w), hm.dtype), pltpu.SemaphoreType.DMA((1,)),
                        pltpu.SemaphoreType.DMA((1,))],
    )
    return pl.pallas_call(
        _dispatch_kernel,
        grid_spec=grid_spec,
        out_shape=jax.ShapeDtypeStruct((n_rows, w), hm.dtype),
        compiler_params=_cparams(("arbitrary",)),
        name="moe_dispatch",
    )(dest, seg_last, seg_blocks, used_blocks, hm)


def _moe_kernel(item_e, item_blk0, item_nblk, used_blocks,
                xs_hbm, wg_hbm, wu_hbm, wd_hbm, y_hbm,
                xg, xb, gs, ab, yp, gu_buf, dn_buf, gsem, osem, gusem, dnsem):
    i = pl.program_id(0)
    j = pl.program_id(1)
    n_items = pl.num_programs(0)
    nj = pl.num_programs(1)
    slot = i % 2
    nblk = item_nblk[i]

    def weight_copy(it, ph):
        if ph == 2:
            ws = it % DN_SLOTS
            return pltpu.make_async_copy(wd_hbm.at[item_e[it]], dn_buf.at[ws], dnsem.at[ws])
        ws = (2 * it + ph) % GU_SLOTS
        return pltpu.make_async_copy((wg_hbm, wu_hbm)[ph].at[item_e[it]], gu_buf.at[ws], gusem.at[ws])

    def start_weight(it, ph):
        it_c = jnp.minimum(it, n_items - 1)

        @pl.when(jnp.logical_and(it < n_items, item_nblk[it_c] > 0))
        def _():
            weight_copy(it_c, ph).start(priority=WEIGHT_DMA_PRIORITY)

    def x_copy(it, sl, m):
        r0 = pl.multiple_of(item_blk0[it] * MOE_BLOCK, MOE_BLOCK)
        return pltpu.make_async_copy(xs_hbm.at[pl.ds(r0, m)], xg.at[sl, pl.ds(0, m)], gsem.at[sl])

    def issue_gather(it, sl):
        for nb in range(1, ITEM_BLOCKS + 1):
            @pl.when(item_nblk[it] == nb)
            def _():
                x_copy(it, sl, nb * MOE_BLOCK).start(priority=ROW_DMA_PRIORITY)

    def wait_gather(it, sl):
        for nb in range(1, ITEM_BLOCKS + 1):
            @pl.when(item_nblk[it] == nb)
            def _():
                x_copy(it, sl, nb * MOE_BLOCK).wait()

    def out_copy(it, m):
        r0 = pl.multiple_of(item_blk0[it] * MOE_BLOCK, MOE_BLOCK)
        return pltpu.make_async_copy(yp.at[pl.ds(0, m)], y_hbm.at[pl.ds(r0, m)], osem.at[0])

    def wait_out(it):
        for nb in range(1, ITEM_BLOCKS + 1):
            @pl.when(item_nblk[it] == nb)
            def _():
                out_copy(it, nb * MOE_BLOCK).wait()

    @pl.when(j == 0)
    def _():
        @pl.when(i == 0)
        def _():
            start_weight(0, 0)
            start_weight(0, 1)
            start_weight(1, 0)
            start_weight(0, 2)
            issue_gather(0, 0)

        start_weight(i + 1, 2)
        wait_gather(i, slot)

        @pl.when(i + 1 < n_items)
        def _():
            issue_gather(i + 1, 1 - slot)

    @pl.when(j == 1)
    def _():
        start_weight(i + 1, 1)

    @pl.when(j == 2)
    def _():
        start_weight(i + 2, 0)

        @pl.when(i > 0)
        def _():
            wait_out(i - 1)

    for ph in range(3):
        @pl.when(jnp.logical_and(j == ph, nblk > 0))
        def _():
            weight_copy(i, ph).wait()

    for nb in range(1, ITEM_BLOCKS + 1):
        m = nb * MOE_BLOCK

        @pl.when(jnp.logical_and(nblk == nb, j == 0))
        def _():
            hi, lo = _unpack_halves(xg[slot, 0:m, :])
            half = hi.shape[1]
            xb[0:m, 0:half] = hi.astype(BF16)
            xb[0:m, half:2 * half] = lo.astype(BF16)
            gs[0:m, :] = _bdot(xb[0:m, :], gu_buf[(2 * i) % GU_SLOTS].astype(BF16))

        @pl.when(jnp.logical_and(nblk == nb, j == 1))
        def _():
            g = gs[0:m, :]
            u = _bdot(xb[0:m, :], gu_buf[(2 * i + 1) % GU_SLOTS].astype(BF16))
            ab[0:m, :] = (g * jax.nn.sigmoid(g) * u).astype(BF16)

        @pl.when(jnp.logical_and(nblk == nb, j == 2))
        def _():
            yp[0:m, :] = _pack_halves(_bdot(ab[0:m, :], dn_buf[i % DN_SLOTS].astype(BF16)))
            out_copy(i, m).start()

    @pl.when(jnp.logical_and(i == n_items - 1, j == nj - 1))
    def _():
        wait_out(i)
        n_blocks = y_hbm.shape[0] // MOE_BLOCK
        yp[0:MOE_BLOCK, :] = jnp.zeros((MOE_BLOCK, yp.shape[1]), jnp.uint32)

        def tail_copy(blk):
            r0 = pl.multiple_of(blk * MOE_BLOCK, MOE_BLOCK)
            return pltpu.make_async_copy(yp.at[pl.ds(0, MOE_BLOCK)], y_hbm.at[pl.ds(r0, MOE_BLOCK)], osem.at[0])

        def start_body(blk, carry):
            tail_copy(blk).start()
            return carry

        def wait_body(blk, carry):
            tail_copy(blk).wait()
            return carry

        lax.fori_loop(used_blocks[0], n_blocks, start_body, 0)
        lax.fori_loop(used_blocks[0], n_blocks, wait_body, 0)


def _moe(xs, w_gate, w_up, w_down, item_e, item_blk0, item_nblk, used_blocks, *, n_rows):
    d, de = w_gate.shape[1], w_gate.shape[2]
    n_items = item_e.shape[0]
    nj = 3
    rows = ITEM_BLOCKS * MOE_BLOCK
    any_spec = pl.BlockSpec(memory_space=pl.ANY)
    grid_spec = pltpu.PrefetchScalarGridSpec(
        num_scalar_prefetch=4,
        grid=(n_items, nj),
        in_specs=[any_spec, any_spec, any_spec, any_spec],
        out_specs=any_spec,
        scratch_shapes=[pltpu.VMEM((2, rows, d // 2), jnp.uint32),
                        pltpu.VMEM((rows, d), BF16),
                        pltpu.VMEM((rows, de), F32),
                        pltpu.VMEM((rows, de), BF16),
                        pltpu.VMEM((rows, d // 2), jnp.uint32),
                        pltpu.VMEM((GU_SLOTS, d, de), F32),
                        pltpu.VMEM((DN_SLOTS, de, d), F32),
                        pltpu.SemaphoreType.DMA((2,)),
                        pltpu.SemaphoreType.DMA((1,)),
                        pltpu.SemaphoreType.DMA((GU_SLOTS,)),
                        pltpu.SemaphoreType.DMA((DN_SLOTS,))],
    )
    return pl.pallas_call(
        _moe_kernel,
        grid_spec=grid_spec,
        out_shape=jax.ShapeDtypeStruct((n_rows, d // 2), jnp.uint32),
        compiler_params=_cparams(("arbitrary", "arbitrary")),
        name="moe_experts",
    )(item_e, item_blk0, item_nblk, used_blocks, xs, w_gate, w_up, w_down)


def _combine_kernel(dest, x_ref, gt2_ref, rw_ref, y_hbm, o_ref, ybuf, sem, *, mod_row):
    i = pl.program_id(0)
    n = pl.num_programs(0)
    tm = x_ref.shape[0]
    slot = i % 2

    def issue(it, sl):
        base = it * tm

        def body(r4, carry):
            for rr in range(GATHER_UNROLL // 2):
                r = r4 * (GATHER_UNROLL // 2) + rr
                for k in range(2):
                    pltpu.make_async_copy(y_hbm.at[pl.ds(dest[2 * (base + r) + k], 1)], ybuf.at[sl, k, pl.ds(r, 1)],
                                          sem.at[sl]).start(priority=k)
            return carry

        lax.fori_loop(0, tm // (GATHER_UNROLL // 2), body, 0)

    @pl.when(i == 0)
    def _():
        issue(0, 0)

    for k in range(2):
        pltpu.make_async_copy(y_hbm.at[pl.ds(0, tm)], ybuf.at[slot, k], sem.at[slot]).wait()

    @pl.when(i + 1 < n)
    def _():
        issue(i + 1, 1 - slot)

    w = rw_ref[...]
    gate = gt2_ref[pl.ds(mod_row(i), 1), :]
    hi0, lo0 = _unpack_halves(ybuf[slot, 0])
    hi1, lo1 = _unpack_halves(ybuf[slot, 1])
    half = hi0.shape[1]
    o_ref[:, 0:half] = x_ref[:, 0:half] + gate[:, 0:half] * (w[:, 0:1] * hi0 + w[:, 1:2] * hi1)
    o_ref[:, half:2 * half] = (x_ref[:, half:2 * half]
                               + gate[:, half:2 * half] * (w[:, 0:1] * lo0 + w[:, 1:2] * lo1))


def _combine(dest, xnew, mods, rw, y, *, seq, tm):
    t, d = xnew.shape
    tiles_per_b = seq // tm
    grid_spec = pltpu.PrefetchScalarGridSpec(
        num_scalar_prefetch=1,
        grid=(t // tm,),
        in_specs=[pl.BlockSpec((tm, d), lambda i, ds: (i, 0)),
                  _mod_spec(mods, 5, d),
                  pl.BlockSpec((tm, LANES), lambda i, ds: (i, 0)),
                  pl.BlockSpec(memory_space=pl.ANY)],
        out_specs=pl.BlockSpec((tm, d), lambda i, ds: (i, 0)),
        scratch_shapes=[pltpu.VMEM((2, 2, tm, d // 2), jnp.uint32), pltpu.SemaphoreType.DMA((2,))],
    )
    return pl.pallas_call(
        functools.partial(_combine_kernel, mod_row=lambda i: i // tiles_per_b),
        grid_spec=grid_spec,
        out_shape=jax.ShapeDtypeStruct((t, d), F32),
        compiler_params=_cparams(("arbitrary",)),
        name="moe_combine",
    )(dest, xnew, mods, rw, y)


def _rope_tables(n_tokens):
    rows = n_tokens // GRID_W
    row = jnp.repeat(jnp.arange(rows, dtype=jnp.int32), GRID_W).astype(F32)
    col = jnp.tile(jnp.arange(GRID_W, dtype=jnp.int32), rows).astype(F32)
    n_freq = QK_ROPE_DIM // 4
    inv = ROPE_THETA ** (-jnp.arange(n_freq, dtype=F32) / n_freq)
    ar = row[:, None] * inv[None, :]
    ac = col[:, None] * inv[None, :]
    cr, sr, cc, sc = jnp.cos(ar), jnp.sin(ar), jnp.cos(ac), jnp.sin(ac)
    cos64 = jnp.concatenate([cr, cr, cc, cc], axis=-1)
    sin64 = jnp.concatenate([-sr, sr, -sc, sc], axis=-1)
    return jnp.tile(cos64, (1, 2)), jnp.tile(sin64, (1, 2))


def _channel_dft_table():
    c = np.arange(FOURIER_GROUP_DIM).reshape(-1, 1)
    k = np.arange(FOURIER_GROUP_DIM).reshape(1, -1)
    ang = (2.0 * np.pi / FOURIER_GROUP_DIM) * ((c * k) % FOURIER_GROUP_DIM)
    return jnp.asarray(np.concatenate([np.cos(ang), -np.sin(ang)], axis=1).astype(np.float32)).astype(BF16)


def _split_heads(w, widths):
    k = w.shape[0]
    wh = w.reshape(k, MLA_HEADS, sum(widths))
    outs, off = [], 0
    for wd in widths:
        outs.append(wh[:, :, off:off + wd].reshape(k, MLA_HEADS * wd))
        off += wd
    return outs


def kernel(x, c, ctx, c_ctx, w_ada, b_ada, g_norm1, g_norm2, w_in, g_q_a, g_kv_a, w_uq, w_ukv, g_qk_q, g_qk_k,
           g_out_four, g_out_attn, w_out, w_router_group, b_router_group, w_router_expert, b_router_expert,
           w_gate, w_up, w_down):
    b, s, d = x.shape
    lc = ctx.shape[1]
    t = b * s
    layer_params = (w_ada, b_ada, g_norm1, g_norm2, w_in, g_q_a, g_kv_a, w_uq, w_ukv, g_qk_q, g_qk_k, g_out_four,
                    g_out_attn, w_out, w_router_group, b_router_group, w_router_expert, b_router_expert,
                    w_gate, w_up, w_down)
    assert all(p.shape[0] == 1 for p in layer_params), "single-layer block"
    (w_ada, b_ada, g_norm1, g_norm2, w_in, g_q_a, g_kv_a, w_uq, w_ukv, g_qk_q, g_qk_k, g_out_four,
     g_out_attn, w_out, w_router_group, b_router_group, w_router_expert, b_router_expert,
     w_gate, w_up, w_down) = [p.reshape(p.shape[1:]) for p in layer_params]

    cond8 = jnp.concatenate([c, c_ctx[None, :], jnp.zeros((8 - b - 1, d), F32)], axis=0)
    mods = _ada_mod(cond8, w_ada, b_ada)

    w_all = jnp.concatenate([w_in, w_in[:, ROPE_OFF:]], axis=1).astype(BF16)
    dc = _channel_dft_table()
    wqn, wqr = [w.astype(BF16) for w in _split_heads(w_uq, (QK_NOPE_DIM, QK_ROPE_DIM))]
    wkn, wv = [w.astype(BF16) for w in _split_heads(w_ukv, (QK_NOPE_DIM, V_HEAD_DIM))]
    gqn = g_qk_q[:QK_NOPE_DIM].reshape(1, -1)
    gqr2 = jnp.tile(g_qk_q[QK_NOPE_DIM:], 2).reshape(1, -1)
    gkn = g_qk_k[:QK_NOPE_DIM].reshape(1, -1)
    gkr2 = jnp.tile(g_qk_k[QK_NOPE_DIM:], 2).reshape(1, -1)
    g1 = g_norm1.reshape(1, d)
    gq = g_q_a.reshape(1, -1)
    gkv = g_kv_a.reshape(1, -1)

    x2 = x.reshape(t, d)
    u, cq, ckv, kr2 = _in_proj(x2, mods, 0, s, g1, gq, gkv, w_all, dc, with_q=True, tm=IN_PROJ_ROWS)
    ckv_c, kr2_c = _in_proj(ctx.reshape(b * lc, d), mods, b, b * lc, g1, gq, gkv, w_all, dc, with_q=False, tm=lc)

    cos_t, sin_t = _rope_tables(s)
    q, kx, vx = _qkv(cq, ckv, kr2, cos_t, sin_t, wqn, wqr, gqn, gqr2, wkn, wv, gkn, gkr2,
                     batch=b, seq=s, tm=QKV_ROWS, with_q=True, with_rope=True)
    kc, vc = _qkv(None, ckv_c, kr2_c, None, None, None, None, None, None, wkn, wv, gkn, gkr2,
                  batch=b, seq=lc, tm=lc, with_q=False, with_rope=False)

    score_bound = (QK_HEAD_DIM ** 0.5 * LOG2_E * ATTN_BOUND_MARGIN) * jnp.max(jnp.abs(g_qk_q)) * jnp.max(jnp.abs(g_qk_k))
    score_bound = jnp.full((1, LANES), score_bound, F32)
    attn = _attention(score_bound, q, kx, kc, vx, vc, tq=ATTN_Q_ROWS).reshape(t, MLA_DIM)
    four = _seq_dft(u, b, s)

    wo = w_out.astype(BF16)
    n_route = N_GROUPS + N_EXPERTS
    wrt = jnp.concatenate([w_router_group, w_router_expert, jnp.zeros((d, LANES - n_route), F32)], axis=1)
    wrt_hi = wrt.astype(BF16)
    wrt_lo = (wrt - wrt_hi.astype(F32)).astype(BF16)
    brt = jnp.concatenate([b_router_group, b_router_expert, jnp.zeros((LANES - n_route,), F32)]).reshape(1, -1)
    xnew, hm, ri, rw, cnt = _out_router(x2, four, attn, mods, g_out_four.reshape(1, -1), g_out_attn.reshape(1, -1),
                                        g_norm2.reshape(1, d), wo, wrt_hi, wrt_lo, brt, seq=s, tm=OUT_ROUTER_ROWS)

    counts = cnt[0, :N_EXPERTS].astype(jnp.int32)
    nblk_e = (counts + MOE_BLOCK - 1) // MOE_BLOCK
    blk_end = jnp.cumsum(nblk_e)
    blk_start = blk_end - nblk_e
    n_slots = t * 2
    n_blocks = -(-(n_slots + N_EXPERTS * (MOE_BLOCK - 1)) // MOE_BLOCK)
    n_rows = n_blocks * MOE_BLOCK
    e12 = ri[:, 0:2]
    seg_start = jnp.sum(jnp.where(e12[:, :, None] == jnp.arange(N_EXPERTS, dtype=jnp.int32), blk_start, 0), axis=-1)
    dest = (seg_start * MOE_BLOCK + ri[:, 2:4]).reshape(-1).astype(jnp.int32)
    items_e = (nblk_e + ITEM_BLOCKS - 1) // ITEM_BLOCKS
    item_end = jnp.cumsum(items_e)
    n_items = (n_blocks + (ITEM_BLOCKS - 1) * N_EXPERTS) // ITEM_BLOCKS
    idx = jnp.arange(n_items, dtype=jnp.int32)
    total = item_end[-1]
    idx_c = jnp.minimum(idx, total - 1)
    ie = jnp.minimum(jnp.sum(idx_c[:, None] >= item_end[None, :], axis=1), N_EXPERTS - 1).astype(jnp.int32)
    local = idx_c - (item_end - items_e)[ie]
    item_blk0 = (blk_start[ie] + ITEM_BLOCKS * local).astype(jnp.int32)
    item_nblk = jnp.where(idx < total, jnp.clip(nblk_e[ie] - ITEM_BLOCKS * local, 0, ITEM_BLOCKS), 0).astype(jnp.int32)

    used_blocks = blk_end[-1:].astype(jnp.int32)
    seg_last = jnp.maximum(blk_end - 1, 0).astype(jnp.int32)
    xs = _dispatch(dest, seg_last, nblk_e.astype(jnp.int32), used_blocks, hm, n_rows=n_rows, tm=DISPATCH_ROWS)
    y = _moe(xs, w_gate, w_up, w_down, ie, item_blk0, item_nblk, used_blocks, n_rows=n_rows)
    out = _combine(dest, xnew, mods, rw, y, seq=s, tm=COMBINE_ROWS)
    return out.reshape(b, s, d)
```

```python
import functools
import math

import numpy as np
import jax
import jax.numpy as jnp
from jax import lax
from jax.experimental import pallas as pl
from jax.experimental.pallas import tpu as pltpu

F32 = jnp.float32
BF16 = jnp.bfloat16

D_MODEL = 2048
GRID_W = 64
EPS = 1e-6
N_MOD = 6
N_FOURIER_GROUPS = 4
FOURIER_GROUP_DIM = 256
FOURIER_DIM = 1024
MLA_HEADS = 8
QK_NOPE_DIM = 128
QK_ROPE_DIM = 64
QK_HEAD_DIM = 192
V_HEAD_DIM = 128
Q_LORA_RANK = 768
KV_LORA_RANK = 512
MLA_DIM = 1024
ROPE_THETA = 10000.0
Q_OFF = FOURIER_DIM
KV_OFF = Q_OFF + Q_LORA_RANK
ROPE_OFF = KV_OFF + KV_LORA_RANK
N_GROUPS = 8
EXPERTS_PER_GROUP = 8
N_EXPERTS = 64
D_EXPERT = 768

ROT_HALF = QK_ROPE_DIM // 4
ROT_BLOCK = 2 * ROT_HALF

ADA_COLS = 1024
IN_PROJ_ROWS = 512
QKV_ROWS = 512
ATTN_Q_ROWS = 2048
OUT_ROUTER_ROWS = 512
COMBINE_ROWS = 512
DISPATCH_ROWS = 1024

LANES = 128
SUBLANES = 8
DFT_BLOCKS = GRID_W // SUBLANES
HEAD_PAD = 256
V_PAD = 256
MOE_BLOCK = 64
ITEM_BLOCKS = 8
GATHER_UNROLL = 16
GU_SLOTS = 3
DN_SLOTS = 2
ATTN_KEY_CHUNK = 256
ATTN_BOUND_MARGIN = 1.02
ATTN_MIN_ROW_SUM = 1e-30
LOG2_E = math.log2(math.e)
ROW_DMA_PRIORITY = 0
WEIGHT_DMA_PRIORITY = 1
VMEM_LIMIT = 56 * 1024 * 1024
NEG_BIG = -1e30


def _cparams(sem):
    return pltpu.CompilerParams(dimension_semantics=sem, vmem_limit_bytes=VMEM_LIMIT)


def _bdot(a, b):
    return jnp.dot(a, b, preferred_element_type=F32)


def _pack_halves(v):
    n = v.shape[1] // 2
    hi = pltpu.bitcast(v[:, :n].astype(BF16).astype(F32), jnp.uint32)
    lo = pltpu.bitcast(v[:, n:].astype(BF16).astype(F32), jnp.uint32)
    return hi | (lo >> 16)


def _unpack_halves(u):
    hi = pltpu.bitcast(u & jnp.uint32(0xFFFF0000), F32)
    lo = pltpu.bitcast(u << 16, F32)
    return hi, lo


def _ada_kernel(c_ref, w_ref, b_ref, o_ref):
    c = c_ref[...]
    s = (c * jax.nn.sigmoid(c)).astype(BF16)
    o_ref[...] = _bdot(s, w_ref[...].astype(BF16)) + b_ref[...]


def _ada_mod(cond8, w_ada, b_ada):
    d, n = w_ada.shape
    tn = ADA_COLS
    return pl.pallas_call(
        _ada_kernel,
        grid=(n // tn,),
        in_specs=[pl.BlockSpec((8, d), lambda i: (0, 0)),
                  pl.BlockSpec((d, tn), lambda i: (0, i)),
                  pl.BlockSpec((1, tn), lambda i: (0, i))],
        out_specs=pl.BlockSpec((8, tn), lambda i: (0, i)),
        out_shape=jax.ShapeDtypeStruct((8, n), F32),
        compiler_params=_cparams(("arbitrary",)),
        name="ada_mod",
    )(cond8, w_ada, b_ada.reshape(1, n))


def _in_proj_kernel(x_ref, sh_ref, sc_ref, g_ref, gq_ref, gkv_ref, w_ref, wr_ref, dc_ref, *out_refs, with_q, mod_row):
    row = pl.ds(mod_row(pl.program_id(0)), 1)
    x = x_ref[...]
    ms = jnp.mean(x * x, axis=-1, keepdims=True)
    a = g_ref[...] * (1.0 + sc_ref[row, :])
    h = (x * lax.rsqrt(ms + EPS) * a + sh_ref[row, :]).astype(BF16)
    if with_q:
        u_ref, cq_ref, ckv_ref, kr_ref = out_refs
        f = _bdot(h, w_ref[:, 0:Q_OFF]).astype(BF16)
        dc = dc_ref[...]
        for g in range(N_FOURIER_GROUPS):
            lo = g * FOURIER_GROUP_DIM
            ug = _bdot(f[:, lo:lo + FOURIER_GROUP_DIM], dc)
            u_ref[:, lo:lo + FOURIER_GROUP_DIM] = ug[:, :FOURIER_GROUP_DIM]
            u_ref[:, FOURIER_DIM + lo:FOURIER_DIM + lo + FOURIER_GROUP_DIM] = ug[:, FOURIER_GROUP_DIM:]
        pq = _bdot(h, w_ref[:, Q_OFF:KV_OFF])
        msq = jnp.mean(pq * pq, axis=-1, keepdims=True)
        cq_ref[...] = (pq * lax.rsqrt(msq + EPS) * gq_ref[...]).astype(BF16)
    else:
        ckv_ref, kr_ref = out_refs
    pkv = _bdot(h, w_ref[:, KV_OFF:ROPE_OFF])
    mskv = jnp.mean(pkv * pkv, axis=-1, keepdims=True)
    ckv_ref[...] = (pkv * lax.rsqrt(mskv + EPS) * gkv_ref[...]).astype(BF16)
    kr_ref[...] = _bdot(h, wr_ref[...])


def _mod_spec(mods, k, d):
    return pl.BlockSpec((mods.shape[0], d), lambda i, *_: (0, k))


def _in_proj(x2, mods, first_row, rows_per_mod, g1, gq, gkv, w_all, wr2, dc, *, with_q, tm):
    t, d = x2.shape
    nt = t // tm
    tiles_per_mod = rows_per_mod // tm

    def const(shape):
        return pl.BlockSpec(shape, lambda i: (0,) * len(shape))

    in_specs = [pl.BlockSpec((tm, d), lambda i: (i, 0)), _mod_spec(mods, 0, d), _mod_spec(mods, 1, d),
                const((1, d)), const((1, Q_LORA_RANK)), const((1, KV_LORA_RANK)),
                const(w_all.shape), const(wr2.shape), const(dc.shape)]

    def rows(n):
        return pl.BlockSpec((tm, n), lambda i: (i, 0))

    out_specs = [rows(KV_LORA_RANK), rows(LANES)]
    out_shape = [jax.ShapeDtypeStruct((t, KV_LORA_RANK), BF16), jax.ShapeDtypeStruct((t, LANES), F32)]
    if with_q:
        out_specs = [rows(2 * FOURIER_DIM), rows(Q_LORA_RANK)] + out_specs
        out_shape = [jax.ShapeDtypeStruct((t, 2 * FOURIER_DIM), F32),
                     jax.ShapeDtypeStruct((t, Q_LORA_RANK), BF16)] + out_shape
    return pl.pallas_call(
        functools.partial(_in_proj_kernel, with_q=with_q, mod_row=lambda i: first_row + i // tiles_per_mod),
        grid=(nt,),
        in_specs=in_specs,
        out_specs=out_specs,
        out_shape=out_shape,
        compiler_params=_cparams(("arbitrary",)),
        name="in_proj_x" if with_q else "in_proj_ctx",
    )(x2, mods, mods, g1, gq, gkv, w_all, wr2, dc)


def _swap_halves(y, first_half):
    return jnp.where(first_half, pltpu.roll(y, LANES - ROT_HALF, 1), pltpu.roll(y, ROT_HALF, 1))


def _qkv_kernel(*refs, with_q, with_rope):
    it = iter(refs)
    if with_q:
        cq_ref = next(it)
    ckv_ref = next(it)
    kr_ref = next(it)
    if with_rope:
        cos_ref = next(it)
        sin_ref = next(it)
    if with_q:
        wqn_ref = next(it)
        wqr_ref = next(it)
        gqn_ref = next(it)
        gqr_ref = next(it)
    wkn_ref = next(it)
    wv_ref = next(it)
    gkn_ref = next(it)
    gkr_ref = next(it)
    if with_q:
        q_ref = next(it)
    k_ref = next(it)
    v_ref = next(it)

    tm = ckv_ref.shape[0]
    lane = lax.broadcasted_iota(jnp.int32, (tm, LANES), 1)
    low = lane < QK_ROPE_DIM
    first_half = (lane % ROT_BLOCK) < ROT_HALF
    inv_dim = 1.0 / QK_HEAD_DIM

    def rope(y):
        if not with_rope:
            return y
        return y * cos_ref[...] + _swap_halves(y, first_half) * sin_ref[...]

    if with_q:
        cq = cq_ref[...]
        qn = _bdot(cq, wqn_ref[...])
        qr = _bdot(cq, wqr_ref[...])
        qscale = QK_HEAD_DIM ** -0.5 * LOG2_E
        for p in range(MLA_HEADS // 2):
            blk = qr[:, p * LANES:(p + 1) * LANES]
            sq = blk * blk
            ss_lo = jnp.sum(jnp.where(low, sq, 0.0), axis=-1, keepdims=True)
            ss_hi = jnp.sum(jnp.where(low, 0.0, sq), axis=-1, keepdims=True)
            scales = []
            for hh, ssr in ((2 * p, ss_lo), (2 * p + 1, ss_hi)):
                nh = qn[:, hh * LANES:(hh + 1) * LANES]
                ssq = jnp.sum(nh * nh, axis=-1, keepdims=True) + ssr
                s = lax.rsqrt(ssq * inv_dim + EPS)
                scales.append(s)
                q_ref[hh, :, 0:LANES] = (nh * s * gqn_ref[...] * qscale).astype(BF16)
            s_pair = jnp.where(low, scales[0], scales[1])
            r = rope(blk * s_pair * gqr_ref[...]) * qscale
            q_ref[2 * p, :, LANES:2 * LANES] = jnp.where(low, r, 0.0).astype(BF16)
            q_ref[2 * p + 1, :, LANES:2 * LANES] = jnp.where(low, pltpu.roll(r, QK_ROPE_DIM, 1), 0.0).astype(BF16)

    ckv = ckv_ref[...]
    kn = _bdot(ckv, wkn_ref[...])
    v = _bdot(ckv, wv_ref[...])
    kr = kr_ref[...]
    ss_r = jnp.sum(jnp.where(low, kr * kr, 0.0), axis=-1, keepdims=True)
    base = rope(kr * gkr_ref[...])
    ones_col = jnp.where(lane == 0, 1.0, 0.0).astype(BF16)
    for hh in range(MLA_HEADS):
        nh = kn[:, hh * LANES:(hh + 1) * LANES]
        ssq = jnp.sum(nh * nh, axis=-1, keepdims=True) + ss_r
        s = lax.rsqrt(ssq * inv_dim + EPS)
        k_ref[hh, :, 0:LANES] = (nh * s * gkn_ref[...]).astype(BF16)
        k_ref[hh, :, LANES:2 * LANES] = jnp.where(low, base * s, 0.0).astype(BF16)
        v_ref[hh, :, 0:LANES] = v[:, hh * LANES:(hh + 1) * LANES].astype(BF16)
        v_ref[hh, :, LANES:2 * LANES] = ones_col


def _qkv(cq, ckv, kr2, cos_t, sin_t, wqn, wqr, gqn, gqr2, wkn, wv, gkn, gkr2, *, batch, seq, tm, with_q,
         with_rope):
    t = ckv.shape[0]
    nt = t // tm
    tiles_per_b = seq // tm

    def rows(n):
        return pl.BlockSpec((tm, n), lambda i: (i, 0))

    def const(arr):
        return pl.BlockSpec(arr.shape, lambda i: (0,) * arr.ndim)

    tab_spec = pl.BlockSpec((tm, LANES), lambda i: (i % tiles_per_b, 0))

    def head_out(width):
        return pl.BlockSpec((None, MLA_HEADS, tm, width), lambda i: (i // tiles_per_b, 0, i % tiles_per_b, 0))

    args, in_specs = [], []
    if with_q:
        args.append(cq)
        in_specs.append(rows(Q_LORA_RANK))
    args += [ckv, kr2]
    in_specs += [rows(KV_LORA_RANK), rows(LANES)]
    if with_rope:
        args += [cos_t, sin_t]
        in_specs += [tab_spec, tab_spec]
    if with_q:
        args += [wqn, wqr, gqn, gqr2]
        in_specs += [const(wqn), const(wqr), const(gqn), const(gqr2)]
    args += [wkn, wv, gkn, gkr2]
    in_specs += [const(wkn), const(wv), const(gkn), const(gkr2)]

    out_specs = [head_out(HEAD_PAD), head_out(V_PAD)]
    out_shape = [jax.ShapeDtypeStruct((batch, MLA_HEADS, seq, HEAD_PAD), BF16),
                 jax.ShapeDtypeStruct((batch, MLA_HEADS, seq, V_PAD), BF16)]
    if with_q:
        out_specs = [head_out(HEAD_PAD)] + out_specs
        out_shape = [jax.ShapeDtypeStruct((batch, MLA_HEADS, seq, HEAD_PAD), BF16)] + out_shape
    return pl.pallas_call(
        functools.partial(_qkv_kernel, with_q=with_q, with_rope=with_rope),
        grid=(nt,),
        in_specs=in_specs,
        out_specs=out_specs,
        out_shape=out_shape,
        compiler_params=_cparams(("arbitrary",)),
        name="qkv_x" if with_q else "kv_ctx",
    )(*args)


def _attn_kernel(bound_ref, q_ref, kx_ref, kc_ref, vx_ref, vc_ref, o_ref):
    q = q_ref[...]
    tq = q.shape[0]
    dn = (((1,), (1,)), ((), ()))
    chunks = [(kx_ref, vx_ref, c * ATTN_KEY_CHUNK, ATTN_KEY_CHUNK) for c in range(kx_ref.shape[0] // ATTN_KEY_CHUNK)]
    chunks.append((kc_ref, vc_ref, 0, kc_ref.shape[0]))

    def scores(k_ref, lo, n):
        return lax.dot_general(q, k_ref[lo:lo + n, :], dn, preferred_element_type=F32)

    def finish(acc):
        o_ref[...] = acc[:, :V_HEAD_DIM] / acc[:, V_HEAD_DIM:V_HEAD_DIM + 1]

    bound = bound_ref[0:1, 0:1]
    acc = jnp.zeros((tq, V_PAD), F32)
    for k_ref, v_ref, lo, n in chunks:
        p = jnp.exp2(scores(k_ref, lo, n) - bound).astype(BF16)
        acc = acc + _bdot(p, v_ref[lo:lo + n, :])
    finish(acc)

    row_sum_ok = jnp.min(acc[:, V_HEAD_DIM:V_HEAD_DIM + 1]) >= ATTN_MIN_ROW_SUM

    @pl.when(jnp.logical_not(row_sum_ok))
    def _():
        m = jnp.full((tq, 1), NEG_BIG, F32)
        acc2 = jnp.zeros((tq, V_PAD), F32)
        for k_ref, v_ref, lo, n in chunks:
            s = scores(k_ref, lo, n)
            m_new = jnp.maximum(m, jnp.max(s, axis=-1, keepdims=True))
            p = jnp.exp2(s - m_new).astype(BF16)
            acc2 = jnp.exp2(m - m_new) * acc2 + _bdot(p, v_ref[lo:lo + n, :])
            m = m_new
        finish(acc2)


def _attention(score_bound, q, kx, kc, vx, vc, *, tq):
    b, h, s, _ = q.shape
    lc = kc.shape[2]
    return pl.pallas_call(
        _attn_kernel,
        grid=(b, h, s // tq),
        in_specs=[pl.BlockSpec((1, LANES), lambda bi, hi, qi: (0, 0)),
                  pl.BlockSpec((None, None, tq, HEAD_PAD), lambda bi, hi, qi: (bi, hi, qi, 0)),
                  pl.BlockSpec((None, None, s, HEAD_PAD), lambda bi, hi, qi: (bi, hi, 0, 0)),
                  pl.BlockSpec((None, None, lc, HEAD_PAD), lambda bi, hi, qi: (bi, hi, 0, 0)),
                  pl.BlockSpec((None, None, s, V_PAD), lambda bi, hi, qi: (bi, hi, 0, 0)),
                  pl.BlockSpec((None, None, lc, V_PAD), lambda bi, hi, qi: (bi, hi, 0, 0))],
        out_specs=pl.BlockSpec((None, tq, V_HEAD_DIM), lambda bi, hi, qi: (bi, qi, hi)),
        out_shape=jax.ShapeDtypeStruct((b, s, h * V_HEAD_DIM), F32),
        compiler_params=_cparams(("arbitrary", "arbitrary", "arbitrary")),
        name="attention",
    )(score_bound, q, kx, kc, vx, vc)


def _seq_dft_kernel(ure_ref, uim_ref, r_ref, e_ref, t2_ref, o_ref, a_ref):
    s = pl.program_id(2)
    sub = SUBLANES
    n = 2 * GRID_W * sub
    cols = ure_ref.shape[-1]

    @pl.when(s < DFT_BLOCKS)
    def _():
        t = _bdot(r_ref[...].astype(BF16), e_ref[...])
        row = lax.broadcasted_iota(jnp.int32, (n, n), 0)
        col = lax.broadcasted_iota(jnp.int32, (n, n), 1)
        t = jnp.where((row % sub) == (col % sub), t, 0.0).astype(BF16)
        rhs = jnp.concatenate([ure_ref[...].reshape(GRID_W * sub, cols), uim_ref[...].reshape(GRID_W * sub, cols)],
                              axis=0).astype(BF16)
        a = _bdot(t, rhs)
        a_ref[:, :, pl.ds(pl.multiple_of(s * sub, sub), sub), :] = a.reshape(2, GRID_W, sub, cols)

    @pl.when(s >= DFT_BLOCKS)
    def _():
        k0 = pl.multiple_of((s - DFT_BLOCKS) * sub, sub)
        rhs = a_ref[:, pl.ds(k0, sub), :, :].reshape(2 * sub * GRID_W, cols).astype(BF16)
        y = _bdot(t2_ref[...].astype(BF16), rhs)
        o_ref[...] = y.reshape(GRID_W, sub, cols)


def _seq_dft_tables(n_seq):
    w, sub, nb = GRID_W, SUBLANES, DFT_BLOCKS
    ch = np.arange(nb).reshape(nb, 1, 1, 1)
    kb = np.arange(w).reshape(1, w, 1, 1)
    j = np.arange(sub).reshape(1, 1, sub, 1)
    r = np.arange(w).reshape(1, 1, 1, w)
    ang = (2.0 * np.pi / n_seq) * ((kb * (w * r + sub * ch + j)) % n_seq)
    c, s = np.cos(ang), np.sin(ang)
    rot = np.stack([np.stack([c, s], axis=3), np.stack([-s, c], axis=3)], axis=1)
    r1 = rot.reshape(nb, 2 * w * sub, 2 * w).astype(np.float32)
    expand = (np.arange(2 * w * sub)[None, :] // sub == np.arange(2 * w)[:, None]).astype(np.float32)
    ka = np.arange(w).reshape(w, 1)
    cp = np.arange(w).reshape(1, w)
    ang2 = (2.0 * np.pi / w) * ((ka * cp) % w)
    norm = 1.0 / math.sqrt(n_seq * FOURIER_GROUP_DIM)
    cs = np.stack([np.cos(ang2), np.sin(ang2)], axis=1) * norm
    eye = np.eye(sub)
    t2 = (cs[:, None, :, None, :] * eye[None, :, None, :, None]).reshape(w * sub, 2 * sub * w).astype(np.float32)
    return jnp.asarray(r1), jnp.asarray(expand).astype(BF16), jnp.asarray(t2)


def _seq_dft(u, batch, n_seq):
    assert n_seq == GRID_W * GRID_W
    w, sub, nb = GRID_W, SUBLANES, DFT_BLOCKS
    r1, expand, t2 = _seq_dft_tables(n_seq)
    halves = 2
    cols = FOURIER_DIM // halves
    u5 = u.reshape(batch, w, nb, sub, 2 * FOURIER_DIM)

    def u_spec(part):
        return pl.BlockSpec((None, w, None, sub, cols),
                            lambda b, h, s: (b, 0, jnp.minimum(s, nb - 1), 0, part * halves + h))

    y = pl.pallas_call(
        _seq_dft_kernel,
        grid=(batch, halves, 2 * nb),
        in_specs=[u_spec(0), u_spec(1),
                  pl.BlockSpec((None, 2 * w * sub, 2 * w), lambda b, h, s: (jnp.minimum(s, nb - 1), 0, 0)),
                  pl.BlockSpec((2 * w, 2 * w * sub), lambda b, h, s: (0, 0)),
                  pl.BlockSpec((w * sub, 2 * sub * w), lambda b, h, s: (0, 0))],
        out_specs=pl.BlockSpec((None, w, None, sub, cols), lambda b, h, s: (b, 0, jnp.maximum(s - nb, 0), 0, h)),
        out_shape=jax.ShapeDtypeStruct((batch, w, nb, sub, FOURIER_DIM), F32),
        scratch_shapes=[pltpu.VMEM((2, w, w, cols), F32)],
        compiler_params=_cparams(("arbitrary", "arbitrary", "arbitrary")),
        name="seq_dft",
    )(u5, u5, r1, expand, t2)
    return y.reshape(batch * n_seq, FOURIER_DIM)


def _out_router_kernel(x_ref, four_ref, attn_ref, gt1_ref, sh2_ref, sc2_ref, gf_ref, ga_ref, g2_ref,
                       wo_ref, wrh_ref, wrl_ref, br_ref,
                       xnew_ref, hm_ref, ri_ref, rw_ref, cnt_ref, carry_ref, *, mod_row):
    i = pl.program_id(0)
    tm = x_ref.shape[0]
    row = pl.ds(mod_row(i), 1)

    @pl.when(i == 0)
    def _():
        carry_ref[...] = jnp.zeros_like(carry_ref)

    def norm(v, g):
        return (v * lax.rsqrt(jnp.mean(v * v, axis=-1, keepdims=True) + EPS) * g).astype(BF16)

    mix = (_bdot(norm(four_ref[...], gf_ref[...]), wo_ref[0:FOURIER_DIM, :])
           + _bdot(norm(attn_ref[...], ga_ref[...]), wo_ref[FOURIER_DIM:FOURIER_DIM + MLA_DIM, :]))
    xn = x_ref[...] + gt1_ref[row, :] * mix
    xnew_ref[...] = xn
    ms = jnp.mean(xn * xn, axis=-1, keepdims=True)
    hm = xn * lax.rsqrt(ms + EPS) * (g2_ref[...] * (1.0 + sc2_ref[row, :])) + sh2_ref[row, :]
    hm_ref[...] = _pack_halves(hm)

    hm_hi = hm.astype(BF16)
    hm_lo = (hm - hm_hi.astype(F32)).astype(BF16)
    logits = _bdot(hm_hi, wrh_ref[...]) + _bdot(hm_lo, wrh_ref[...]) + _bdot(hm_hi, wrl_ref[...]) + br_ref[...]
    lane = lax.broadcasted_iota(jnp.int32, (tm, LANES), 1)
    lanef = lane.astype(F32)
    far = 1e9

    lg = jnp.where(lane < N_GROUPS, logits, NEG_BIG)
    m1 = jnp.max(lg, axis=-1, keepdims=True)
    g_p = 1.0 / jnp.sum(jnp.exp(lg - m1), axis=-1, keepdims=True)
    gidx = jnp.min(jnp.where(lg >= m1, lanef, far), axis=-1, keepdims=True)
    lo = N_GROUPS + EXPERTS_PER_GROUP * gidx
    in_group = jnp.where(lanef >= lo, jnp.where(lanef < lo + EXPERTS_PER_GROUP, 1.0, 0.0), 0.0) > 0.5
    le = jnp.where(in_group, logits, NEG_BIG)
    m2 = jnp.max(le, axis=-1, keepdims=True)
    idx1 = jnp.min(jnp.where(le >= m2, lanef, far), axis=-1, keepdims=True)
    le2 = jnp.where(lanef == idx1, NEG_BIG, le)
    m3 = jnp.max(le2, axis=-1, keepdims=True)
    idx2 = jnp.min(jnp.where(le2 >= m3, lanef, far), axis=-1, keepdims=True)
    t = jnp.exp(m3 - m2)
    p1 = 1.0 / (1.0 + t)
    p2 = t / (1.0 + t)
    e1 = idx1 - N_GROUPS
    e2 = idx2 - N_GROUPS

    oh1 = jnp.where(lanef == e1, 1.0, 0.0)
    oh2 = jnp.where(lanef == e2, 1.0, 0.0)
    ohs = oh1 + oh2
    row = lax.broadcasted_iota(jnp.int32, (tm, tm), 0)
    col = lax.broadcasted_iota(jnp.int32, (tm, tm), 1)
    tri = jnp.where(row > col, 1.0, 0.0).astype(BF16)
    before = _bdot(tri, ohs.astype(BF16)) + carry_ref[...]
    rank1 = jnp.sum(oh1 * before, axis=-1, keepdims=True)
    rank2 = jnp.sum(oh2 * before, axis=-1, keepdims=True)
    carry = carry_ref[...] + jnp.sum(ohs, axis=0, keepdims=True)
    carry_ref[...] = carry
    cnt_ref[...] = jnp.broadcast_to(carry, cnt_ref.shape)

    ri = jnp.where(lane == 0, e1, jnp.where(lane == 1, e2, jnp.where(lane == 2, rank1, jnp.where(lane == 3, rank2, 0.0))))
    ri_ref[...] = ri.astype(jnp.int32)
    rw_ref[...] = jnp.where(lane == 0, g_p * p1, jnp.where(lane == 1, g_p * p2, 0.0))


def _out_router(x2, four, attn, mods, gf, ga, g2, wo, wrh, wrl, br, *, seq, tm):
    t, d = x2.shape
    nt = t // tm
    tiles_per_b = seq // tm

    def rows(n):
        return pl.BlockSpec((tm, n), lambda i: (i, 0))

    def const(arr):
        return pl.BlockSpec(arr.shape, lambda i: (0,) * arr.ndim, pipeline_mode=pl.Buffered(1))

    return pl.pallas_call(
        functools.partial(_out_router_kernel, mod_row=lambda i: i // tiles_per_b),
        grid=(nt,),
        in_specs=[rows(d), rows(FOURIER_DIM), rows(MLA_DIM), _mod_spec(mods, 2, d), _mod_spec(mods, 3, d),
                  _mod_spec(mods, 4, d),
                  const(gf), const(ga), const(g2), const(wo), const(wrh), const(wrl), const(br)],
        out_specs=[rows(d), rows(d // 2), rows(LANES), rows(LANES), pl.BlockSpec((8, LANES), lambda i: (0, 0))],
        out_shape=[jax.ShapeDtypeStruct((t, d), F32), jax.ShapeDtypeStruct((t, d // 2), jnp.uint32),
                   jax.ShapeDtypeStruct((t, LANES), jnp.int32), jax.ShapeDtypeStruct((t, LANES), F32),
                   jax.ShapeDtypeStruct((8, LANES), F32)],
        scratch_shapes=[pltpu.VMEM((1, LANES), F32)],
        compiler_params=_cparams(("arbitrary",)),
        name="out_proj_router",
    )(x2, four, attn, mods, mods, mods, gf, ga, g2, wo, wrh, wrl, br)


def _dispatch_kernel(dest, seg_last, seg_blocks, used_blocks, hm_ref, xs_hbm, zbuf, sem, zsem):
    i = pl.program_id(0)
    tm = hm_ref.shape[0]

    @pl.when(i == 0)
    def _():
        zbuf[...] = jnp.zeros(zbuf.shape, zbuf.dtype)
        n_blocks = xs_hbm.shape[0] // MOE_BLOCK

        def zero_copy(blk):
            r0 = pl.multiple_of(blk * MOE_BLOCK, MOE_BLOCK)
            return pltpu.make_async_copy(zbuf, xs_hbm.at[pl.ds(r0, MOE_BLOCK)], zsem.at[0])

        def seg_start(e, carry):
            @pl.when(seg_blocks[e] > 0)
            def _():
                zero_copy(seg_last[e]).start()
            return carry

        def seg_wait(e, carry):
            @pl.when(seg_blocks[e] > 0)
            def _():
                zero_copy(seg_last[e]).wait()
            return carry

        def tail_start(blk, carry):
            zero_copy(blk).start()
            return carry

        def tail_wait(blk, carry):
            zero_copy(blk).wait()
            return carry

        lax.fori_loop(0, N_EXPERTS, seg_start, 0)
        lax.fori_loop(used_blocks[0], n_blocks, tail_start, 0)
        lax.fori_loop(0, N_EXPERTS, seg_wait, 0)
        lax.fori_loop(used_blocks[0], n_blocks, tail_wait, 0)

    def body(r8, carry):
        for u in range(GATHER_UNROLL):
            r = r8 * GATHER_UNROLL + u
            for k in range(2):
                pltpu.make_async_copy(hm_ref.at[pl.ds(r, 1)], xs_hbm.at[pl.ds(dest[2 * (i * tm + r) + k], 1)],
                                      sem.at[0]).start(priority=k)
        return carry

    lax.fori_loop(0, tm // GATHER_UNROLL, body, 0)
    for k in range(2):
        pltpu.make_async_copy(hm_ref, xs_hbm.at[pl.ds(0, tm)], sem.at[0]).wait()


def _dispatch(dest, seg_last, seg_blocks, used_blocks, hm, *, n_rows, tm):
    t, w = hm.shape
    grid_spec = pltpu.PrefetchScalarGridSpec(
        num_scalar_prefetch=4,
        grid=(t // tm,),
        in_specs=[pl.BlockSpec((tm, w), lambda i, *_: (i, 0))],
        out_specs=pl.BlockSpec(memory_space=pl.ANY),
        scratch_shapes=[pltpu.VMEM((MOE_BLOCK, w), hm.dtype), pltpu.SemaphoreType.DMA((1,)),
                        pltpu.SemaphoreType.DMA((1,))],
    )
    return pl.pallas_call(
        _dispatch_kernel,
        grid_spec=grid_spec,
        out_shape=jax.ShapeDtypeStruct((n_rows, w), hm.dtype),
        compiler_params=_cparams(("arbitrary",)),
        name="moe_dispatch",
    )(dest, seg_last, seg_blocks, used_blocks, hm)


def _moe_kernel(item_e, item_blk0, item_nblk, used_blocks,
                xs_hbm, wg_hbm, wu_hbm, wd_hbm, y_hbm,
                xg, xb, gs, ab, yp, gu_buf, dn_buf, gsem, osem, gusem, dnsem):
    i = pl.program_id(0)
    j = pl.program_id(1)
    n_items = pl.num_programs(0)
    nj = pl.num_programs(1)
    slot = i % 2
    nblk = item_nblk[i]

    def weight_copy(it, ph):
        if ph == 2:
            ws = it % DN_SLOTS
            return pltpu.make_async_copy(wd_hbm.at[item_e[it]], dn_buf.at[ws], dnsem.at[ws])
        ws = (2 * it + ph) % GU_SLOTS
        return pltpu.make_async_copy((wg_hbm, wu_hbm)[ph].at[item_e[it]], gu_buf.at[ws], gusem.at[ws])

    def start_weight(it, ph):
        it_c = jnp.minimum(it, n_items - 1)

        @pl.when(jnp.logical_and(it < n_items, item_nblk[it_c] > 0))
        def _():
            weight_copy(it_c, ph).start(priority=WEIGHT_DMA_PRIORITY)

    def x_copy(it, sl, m):
        r0 = pl.multiple_of(item_blk0[it] * MOE_BLOCK, MOE_BLOCK)
        return pltpu.make_async_copy(xs_hbm.at[pl.ds(r0, m)], xg.at[sl, pl.ds(0, m)], gsem.at[sl])

    def issue_gather(it, sl):
        for nb in range(1, ITEM_BLOCKS + 1):
            @pl.when(item_nblk[it] == nb)
            def _():
                x_copy(it, sl, nb * MOE_BLOCK).start(priority=ROW_DMA_PRIORITY)

    def wait_gather(it, sl):
        for nb in range(1, ITEM_BLOCKS + 1):
            @pl.when(item_nblk[it] == nb)
            def _():
                x_copy(it, sl, nb * MOE_BLOCK).wait()

    def out_copy(it, m):
        r0 = pl.multiple_of(item_blk0[it] * MOE_BLOCK, MOE_BLOCK)
        return pltpu.make_async_copy(yp.at[pl.ds(0, m)], y_hbm.at[pl.ds(r0, m)], osem.at[0])

    def wait_out(it):
        for nb in range(1, ITEM_BLOCKS + 1):
            @pl.when(item_nblk[it] == nb)
            def _():
                out_copy(it, nb * MOE_BLOCK).wait()

    @pl.when(j == 0)
    def _():
        @pl.when(i == 0)
        def _():
            start_weight(0, 0)
            start_weight(0, 1)
            start_weight(1, 0)
            start_weight(0, 2)
            issue_gather(0, 0)

        start_weight(i + 1, 2)
        wait_gather(i, slot)

        @pl.when(i + 1 < n_items)
        def _():
            issue_gather(i + 1, 1 - slot)

    @pl.when(j == 1)
    def _():
        start_weight(i + 1, 1)

    @pl.when(j == 2)
    def _():
        start_weight(i + 2, 0)

        @pl.when(i > 0)
        def _():
            wait_out(i - 1)

    for ph in range(3):
        @pl.when(jnp.logical_and(j == ph, nblk > 0))
        def _():
            weight_copy(i, ph).wait()

    for nb in range(1, ITEM_BLOCKS + 1):
        m = nb * MOE_BLOCK

        @pl.when(jnp.logical_and(nblk == nb, j == 0))
        def _():
            hi, lo = _unpack_halves(xg[slot, 0:m, :])
            half = hi.shape[1]
            xb[0:m, 0:half] = hi.astype(BF16)
            xb[0:m, half:2 * half] = lo.astype(BF16)
            gs[0:m, :] = _bdot(xb[0:m, :], gu_buf[(2 * i) % GU_SLOTS].astype(BF16))

        @pl.when(jnp.logical_and(nblk == nb, j == 1))
        def _():
            g = gs[0:m, :]
            u = _bdot(xb[0:m, :], gu_buf[(2 * i + 1) % GU_SLOTS].astype(BF16))
            ab[0:m, :] = (g * jax.nn.sigmoid(g) * u).astype(BF16)

        @pl.when(jnp.logical_and(nblk == nb, j == 2))
        def _():
            yp[0:m, :] = _pack_halves(_bdot(ab[0:m, :], dn_buf[i % DN_SLOTS].astype(BF16)))
            out_copy(i, m).start()

    @pl.when(jnp.logical_and(i == n_items - 1, j == nj - 1))
    def _():
        wait_out(i)
        n_blocks = y_hbm.shape[0] // MOE_BLOCK
        yp[0:MOE_BLOCK, :] = jnp.zeros((MOE_BLOCK, yp.shape[1]), jnp.uint32)

        def tail_copy(blk):
            r0 = pl.multiple_of(blk * MOE_BLOCK, MOE_BLOCK)
            return pltpu.make_async_copy(yp.at[pl.ds(0, MOE_BLOCK)], y_hbm.at[pl.ds(r0, MOE_BLOCK)], osem.at[0])

        def start_body(blk, carry):
            tail_copy(blk).start()
            return carry

        def wait_body(blk, carry):
            tail_copy(blk).wait()
            return carry

        lax.fori_loop(used_blocks[0], n_blocks, start_body, 0)
        lax.fori_loop(used_blocks[0], n_blocks, wait_body, 0)


def _moe(xs, w_gate, w_up, w_down, item_e, item_blk0, item_nblk, used_blocks, *, n_rows):
    d, de = w_gate.shape[1], w_gate.shape[2]
    n_items = item_e.shape[0]
    nj = 3
    rows = ITEM_BLOCKS * MOE_BLOCK
    any_spec = pl.BlockSpec(memory_space=pl.ANY)
    grid_spec = pltpu.PrefetchScalarGridSpec(
        num_scalar_prefetch=4,
        grid=(n_items, nj),
        in_specs=[any_spec, any_spec, any_spec, any_spec],
        out_specs=any_spec,
        scratch_shapes=[pltpu.VMEM((2, rows, d // 2), jnp.uint32),
                        pltpu.VMEM((rows, d), BF16),
                        pltpu.VMEM((rows, de), F32),
                        pltpu.VMEM((rows, de), BF16),
                        pltpu.VMEM((rows, d // 2), jnp.uint32),
                        pltpu.VMEM((GU_SLOTS, d, de), F32),
                        pltpu.VMEM((DN_SLOTS, de, d), F32),
                        pltpu.SemaphoreType.DMA((2,)),
                        pltpu.SemaphoreType.DMA((1,)),
                        pltpu.SemaphoreType.DMA((GU_SLOTS,)),
                        pltpu.SemaphoreType.DMA((DN_SLOTS,))],
    )
    return pl.pallas_call(
        _moe_kernel,
        grid_spec=grid_spec,
        out_shape=jax.ShapeDtypeStruct((n_rows, d // 2), jnp.uint32),
        compiler_params=_cparams(("arbitrary", "arbitrary")),
        name="moe_experts",
    )(item_e, item_blk0, item_nblk, used_blocks, xs, w_gate, w_up, w_down)


def _combine_kernel(dest, x_ref, gt2_ref, rw_ref, y_hbm, o_ref, ybuf, sem, *, mod_row):
    i = pl.program_id(0)
    n = pl.num_programs(0)
    tm = x_ref.shape[0]
    slot = i % 2

    def issue(it, sl):
        base = it * tm

        def body(r4, carry):
            for rr in range(GATHER_UNROLL // 2):
                r = r4 * (GATHER_UNROLL // 2) + rr
                for k in range(2):
                    pltpu.make_async_copy(y_hbm.at[pl.ds(dest[2 * (base + r) + k], 1)], ybuf.at[sl, k, pl.ds(r, 1)],
                                          sem.at[sl]).start(priority=k)
            return carry

        lax.fori_loop(0, tm // (GATHER_UNROLL // 2), body, 0)

    @pl.when(i == 0)
    def _():
        issue(0, 0)

    for k in range(2):
        pltpu.make_async_copy(y_hbm.at[pl.ds(0, tm)], ybuf.at[slot, k], sem.at[slot]).wait()

    @pl.when(i + 1 < n)
    def _():
        issue(i + 1, 1 - slot)

    w = rw_ref[...]
    gate = gt2_ref[pl.ds(mod_row(i), 1), :]
    hi0, lo0 = _unpack_halves(ybuf[slot, 0])
    hi1, lo1 = _unpack_halves(ybuf[slot, 1])
    half = hi0.shape[1]
    o_ref[:, 0:half] = x_ref[:, 0:half] + gate[:, 0:half] * (w[:, 0:1] * hi0 + w[:, 1:2] * hi1)
    o_ref[:, half:2 * half] = (x_ref[:, half:2 * half]
                               + gate[:, half:2 * half] * (w[:, 0:1] * lo0 + w[:, 1:2] * lo1))


def _combine(dest, xnew, mods, rw, y, *, seq, tm):
    t, d = xnew.shape
    tiles_per_b = seq // tm
    grid_spec = pltpu.PrefetchScalarGridSpec(
        num_scalar_prefetch=1,
        grid=(t // tm,),
        in_specs=[pl.BlockSpec((tm, d), lambda i, ds: (i, 0)),
                  _mod_spec(mods, 5, d),
                  pl.BlockSpec((tm, LANES), lambda i, ds: (i, 0)),
                  pl.BlockSpec(memory_space=pl.ANY)],
        out_specs=pl.BlockSpec((tm, d), lambda i, ds: (i, 0)),
        scratch_shapes=[pltpu.VMEM((2, 2, tm, d // 2), jnp.uint32), pltpu.SemaphoreType.DMA((2,))],
    )
    return pl.pallas_call(
        functools.partial(_combine_kernel, mod_row=lambda i: i // tiles_per_b),
        grid_spec=grid_spec,
        out_shape=jax.ShapeDtypeStruct((t, d), F32),
        compiler_params=_cparams(("arbitrary",)),
        name="moe_combine",
    )(dest, xnew, mods, rw, y)


def _rope_tables(n_tokens):
    rows = n_tokens // GRID_W
    row = jnp.repeat(jnp.arange(rows, dtype=jnp.int32), GRID_W).astype(F32)
    col = jnp.tile(jnp.arange(GRID_W, dtype=jnp.int32), rows).astype(F32)
    n_freq = QK_ROPE_DIM // 4
    inv = ROPE_THETA ** (-jnp.arange(n_freq, dtype=F32) / n_freq)
    ar = row[:, None] * inv[None, :]
    ac = col[:, None] * inv[None, :]
    cr, sr, cc, sc = jnp.cos(ar), jnp.sin(ar), jnp.cos(ac), jnp.sin(ac)
    cos64 = jnp.concatenate([cr, cr, cc, cc], axis=-1)
    sin64 = jnp.concatenate([-sr, sr, -sc, sc], axis=-1)
    return jnp.tile(cos64, (1, 2)), jnp.tile(sin64, (1, 2))


def _channel_dft_table():
    c = np.arange(FOURIER_GROUP_DIM).reshape(-1, 1)
    k = np.arange(FOURIER_GROUP_DIM).reshape(1, -1)
    ang = (2.0 * np.pi / FOURIER_GROUP_DIM) * ((c * k) % FOURIER_GROUP_DIM)
    return jnp.asarray(np.concatenate([np.cos(ang), -np.sin(ang)], axis=1).astype(np.float32)).astype(BF16)


def _split_heads(w, widths):
    k = w.shape[0]
    wh = w.reshape(k, MLA_HEADS, sum(widths))
    outs, off = [], 0
    for wd in widths:
        outs.append(wh[:, :, off:off + wd].reshape(k, MLA_HEADS * wd))
        off += wd
    return outs


def kernel(x, c, ctx, c_ctx, w_ada, b_ada, g_norm1, g_norm2, w_in, g_q_a, g_kv_a, w_uq, w_ukv, g_qk_q, g_qk_k,
           g_out_four, g_out_attn, w_out, w_router_group, b_router_group, w_router_expert, b_router_expert,
           w_gate, w_up, w_down):
    b, s, d = x.shape
    lc = ctx.shape[1]
    t = b * s
    layer_params = (w_ada, b_ada, g_norm1, g_norm2, w_in, g_q_a, g_kv_a, w_uq, w_ukv, g_qk_q, g_qk_k, g_out_four,
                    g_out_attn, w_out, w_router_group, b_router_group, w_router_expert, b_router_expert,
                    w_gate, w_up, w_down)
    assert all(p.shape[0] == 1 for p in layer_params), "single-layer block"
    (w_ada, b_ada, g_norm1, g_norm2, w_in, g_q_a, g_kv_a, w_uq, w_ukv, g_qk_q, g_qk_k, g_out_four,
     g_out_attn, w_out, w_router_group, b_router_group, w_router_expert, b_router_expert,
     w_gate, w_up, w_down) = [p.reshape(p.shape[1:]) for p in layer_params]

    cond8 = jnp.concatenate([c, c_ctx[None, :], jnp.zeros((8 - b - 1, d), F32)], axis=0)
    mods = _ada_mod(cond8, w_ada, b_ada)

    w_all = w_in.astype(BF16)
    wr2 = jnp.concatenate([w_in[:, ROPE_OFF:], w_in[:, ROPE_OFF:]], axis=1).astype(BF16)
    dc = _channel_dft_table()
    wqn, wqr = [w.astype(BF16) for w in _split_heads(w_uq, (QK_NOPE_DIM, QK_ROPE_DIM))]
    wkn, wv = [w.astype(BF16) for w in _split_heads(w_ukv, (QK_NOPE_DIM, V_HEAD_DIM))]
    gqn = g_qk_q[:QK_NOPE_DIM].reshape(1, -1)
    gqr2 = jnp.tile(g_qk_q[QK_NOPE_DIM:], 2).reshape(1, -1)
    gkn = g_qk_k[:QK_NOPE_DIM].reshape(1, -1)
    gkr2 = jnp.tile(g_qk_k[QK_NOPE_DIM:], 2).reshape(1, -1)
    g1 = g_norm1.reshape(1, d)
    gq = g_q_a.reshape(1, -1)
    gkv = g_kv_a.reshape(1, -1)

    x2 = x.reshape(t, d)
    u, cq, ckv, kr2 = _in_proj(x2, mods, 0, s, g1, gq, gkv, w_all, wr2, dc, with_q=True, tm=IN_PROJ_ROWS)
    ckv_c, kr2_c = _in_proj(ctx.reshape(b * lc, d), mods, b, b * lc, g1, gq, gkv, w_all, wr2, dc, with_q=False,
                            tm=lc)

    cos_t, sin_t = _rope_tables(s)
    q, kx, vx = _qkv(cq, ckv, kr2, cos_t, sin_t, wqn, wqr, gqn, gqr2, wkn, wv, gkn, gkr2,
                     batch=b, seq=s, tm=QKV_ROWS, with_q=True, with_rope=True)
    kc, vc = _qkv(None, ckv_c, kr2_c, None, None, None, None, None, None, wkn, wv, gkn, gkr2,
                  batch=b, seq=lc, tm=lc, with_q=False, with_rope=False)

    score_bound = (QK_HEAD_DIM ** 0.5 * LOG2_E * ATTN_BOUND_MARGIN) * jnp.max(jnp.abs(g_qk_q)) * jnp.max(jnp.abs(g_qk_k))
    score_bound = jnp.full((1, LANES), score_bound, F32)
    attn = _attention(score_bound, q, kx, kc, vx, vc, tq=ATTN_Q_ROWS).reshape(t, MLA_DIM)
    four = _seq_dft(u, b, s)

    wo = w_out.astype(BF16)
    n_route = N_GROUPS + N_EXPERTS
    wrt = jnp.concatenate([w_router_group, w_router_expert, jnp.zeros((d, LANES - n_route), F32)], axis=1)
    wrt_hi = wrt.astype(BF16)
    wrt_lo = (wrt - wrt_hi.astype(F32)).astype(BF16)
    brt = jnp.concatenate([b_router_group, b_router_expert, jnp.zeros((LANES - n_route,), F32)]).reshape(1, -1)
    xnew, hm, ri, rw, cnt = _out_router(x2, four, attn, mods, g_out_four.reshape(1, -1), g_out_attn.reshape(1, -1),
                                        g_norm2.reshape(1, d), wo, wrt_hi, wrt_lo, brt, seq=s, tm=OUT_ROUTER_ROWS)

    counts = cnt[0, :N_EXPERTS].astype(jnp.int32)
    nblk_e = (counts + MOE_BLOCK - 1) // MOE_BLOCK
    blk_end = jnp.cumsum(nblk_e)
    blk_start = blk_end - nblk_e
    n_slots = t * 2
    n_blocks = -(-(n_slots + N_EXPERTS * (MOE_BLOCK - 1)) // MOE_BLOCK)
    n_rows = n_blocks * MOE_BLOCK
    e12 = ri[:, 0:2]
    seg_start = jnp.sum(jnp.where(e12[:, :, None] == jnp.arange(N_EXPERTS, dtype=jnp.int32), blk_start, 0), axis=-1)
    dest = (seg_start * MOE_BLOCK + ri[:, 2:4]).reshape(-1).astype(jnp.int32)
    items_e = (nblk_e + ITEM_BLOCKS - 1) // ITEM_BLOCKS
    item_end = jnp.cumsum(items_e)
    n_items = (n_blocks + (ITEM_BLOCKS - 1) * N_EXPERTS) // ITEM_BLOCKS
    idx = jnp.arange(n_items, dtype=jnp.int32)
    total = item_end[-1]
    idx_c = jnp.minimum(idx, total - 1)
    ie = jnp.minimum(jnp.sum(idx_c[:, None] >= item_end[None, :], axis=1), N_EXPERTS - 1).astype(jnp.int32)
    local = idx_c - (item_end - items_e)[ie]
    item_blk0 = (blk_start[ie] + ITEM_BLOCKS * local).astype(jnp.int32)
    item_nblk = jnp.where(idx < total, jnp.clip(nblk_e[ie] - ITEM_BLOCKS * local, 0, ITEM_BLOCKS), 0).astype(jnp.int32)

    used_blocks = blk_end[-1:].astype(jnp.int32)
    seg_last = jnp.maximum(blk_end - 1, 0).astype(jnp.int32)
    xs = _dispatch(dest, seg_last, nblk_e.astype(jnp.int32), used_blocks, hm, n_rows=n_rows, tm=DISPATCH_ROWS)
    y = _moe(xs, w_gate, w_up, w_down, ie, item_blk0, item_nblk, used_blocks, n_rows=n_rows)
    out = _combine(dest, xnew, mods, rw, y, seq=s, tm=COMBINE_ROWS)
    return out.reshape(b, s, d)
```
